```python
import math
import jax, jax.numpy as jnp
from jax import lax
import numpy as np

D_MODEL = 1024
BATCH = 8
SEQ = 8192
DEPTH = 4

CHUNK = 64
N_A_LAYERS = DEPTH // 2
N_B_LAYERS = DEPTH - N_A_LAYERS

A_HEADS = 8
A_HEAD_K = 128
A_HEAD_V = 128
A_QK_WIDTH = A_HEADS * A_HEAD_K
A_V_WIDTH = A_HEADS * A_HEAD_V
A_CONV = 4
A_CONV_WIDTH = 2 * A_QK_WIDTH + A_V_WIDTH
A_IN_WIDTH = 2 * A_QK_WIDTH + 2 * A_V_WIDTH + 2 * A_HEADS

B_HEADS = 16
B_HEAD_DIM = 64
B_WIDTH = B_HEADS * B_HEAD_DIM
LEFT_CHUNKS = 8
BAND = (LEFT_CHUNKS + 1) * CHUNK
REL_CLIP = 256

FFN_DIM = 2816
FFN_CONV = 3

EPS = 1e-6
NEG_INF = -1e30

kernel_name = "yoco_gdn_chunkattn_convffn"


def rmsnorm(x, g):
    xf = x.astype(jnp.float32)
    y = xf * lax.rsqrt(jnp.mean(xf * xf, axis=-1, keepdims=True) + EPS)
    return (y * g.astype(jnp.float32)).astype(x.dtype)


def causal_dwconv(x, w):
    width = w.shape[0]
    return lax.conv_general_dilated(
        x, w[:, None, :].astype(x.dtype), window_strides=(1,), padding=[(width - 1, 0)],
        dimension_numbers=("NWC", "WIO", "NWC"), feature_group_count=x.shape[-1])


def _l2norm(t):
    return t * lax.rsqrt(jnp.sum(t * t, axis=-1, keepdims=True) + EPS)


def chunk_gated_delta_rule(q, k, v, beta, g):
    bsz, seq, nh, dk = q.shape
    dv = v.shape[-1]
    nc = seq // CHUNK

    def chunks(t):
        return t.reshape(bsz, nc, CHUNK, nh, -1).transpose(0, 3, 1, 2, 4)

    q, k, v = chunks(q), chunks(k), chunks(v)
    beta = chunks(beta[..., None])[..., 0]
    gcum = jnp.cumsum(chunks(g[..., None])[..., 0], axis=-1)

    causal = jnp.tril(jnp.ones((CHUNK, CHUNK), dtype=bool))
    strict = jnp.tril(jnp.ones((CHUNK, CHUNK), dtype=bool), k=-1)
    diff = gcum[..., :, None] - gcum[..., None, :]
    decay = jnp.where(causal, jnp.exp(jnp.where(causal, diff, 0.0)), 0.0)

    k_beta = k * beta[..., None]
    m = jnp.where(strict, jnp.einsum("bhnid,bhnjd->bhnij", k_beta, k) * decay, 0.0)
    eye = jnp.eye(CHUNK, dtype=m.dtype)
    rhs = jnp.concatenate([v * beta[..., None], k_beta * jnp.exp(gcum)[..., None]], axis=-1)
    uw = lax.linalg.triangular_solve(m + eye, rhs, left_side=True, lower=True, unit_diagonal=True)
    u, w = uw[..., :dv], uw[..., dv:]

    attn_qk = jnp.einsum("bhnid,bhnjd->bhnij", q, k) * decay
    q_dec = q * jnp.exp(gcum)[..., None]
    k_end = k * jnp.exp(gcum[..., -1:] - gcum)[..., None]
    chunk_decay = jnp.exp(gcum[..., -1])

    xs = tuple(jnp.moveaxis(t, 2, 0) for t in (q_dec, k_end, u, w, attn_qk, chunk_decay))

    def step(state, inp):
        qd, ke, u_c, w_c, a_c, dec = inp
        v_new = u_c - jnp.einsum("bhcd,bhdv->bhcv", w_c, state)
        o_c = jnp.einsum("bhcd,bhdv->bhcv", qd, state) + jnp.einsum("bhcj,bhjv->bhcv", a_c, v_new)
        state = state * dec[..., None, None] + jnp.einsum("bhcd,bhcv->bhdv", ke, v_new)
        return state, o_c

    state0 = jnp.zeros((bsz, nh, dk, dv), jnp.float32)
    _, o = lax.scan(step, state0, xs)
    return o.transpose(1, 0, 3, 2, 4).reshape(bsz, seq, nh, dv)


def gated_deltanet(xn, w_in, conv_w, A_log, dt_bias, out_norm_w, w_out):
    bsz, seq, _ = xn.shape
    proj = xn @ w_in
    qkv = jax.nn.silu(causal_dwconv(proj[..., :A_CONV_WIDTH], conv_w))
    z = proj[..., A_CONV_WIDTH:A_CONV_WIDTH + A_V_WIDTH]
    b_raw = proj[..., A_CONV_WIDTH + A_V_WIDTH:A_CONV_WIDTH + A_V_WIDTH + A_HEADS]
    a_raw = proj[..., A_CONV_WIDTH + A_V_WIDTH + A_HEADS:]
    f32 = jnp.float32
    q = _l2norm(qkv[..., :A_QK_WIDTH].reshape(bsz, seq, A_HEADS, A_HEAD_K).astype(f32)) * (A_HEAD_K ** -0.5)
    k = _l2norm(qkv[..., A_QK_WIDTH:2 * A_QK_WIDTH].reshape(bsz, seq, A_HEADS, A_HEAD_K).astype(f32))
    v = qkv[..., 2 * A_QK_WIDTH:].reshape(bsz, seq, A_HEADS, A_HEAD_V).astype(f32)
    beta = jax.nn.sigmoid(b_raw.astype(f32))
    g = -jnp.exp(A_log.astype(f32)) * jax.nn.softplus(a_raw.astype(f32) + dt_bias.astype(f32))
    o = chunk_gated_delta_rule(q, k, v, beta, g).astype(xn.dtype)
    o = rmsnorm(o, out_norm_w) * jax.nn.silu(z.reshape(bsz, seq, A_HEADS, A_HEAD_V))
    return o.reshape(bsz, seq, A_V_WIDTH) @ w_out


def chunk_attention(xn, w_q, rel_bias, w_out, k_pad, v_pad):
    bsz, seq, _ = xn.shape
    nc = seq // CHUNK
    q = (xn @ w_q).reshape(bsz, nc, CHUNK, B_HEADS, B_HEAD_DIM).transpose(1, 0, 2, 3, 4)
    rel = jnp.arange(CHUNK)[:, None] + LEFT_CHUNKS * CHUNK - jnp.arange(BAND)[None, :]
    bias = rel_bias[:, jnp.clip(rel, -REL_CLIP, REL_CLIP) + REL_CLIP].astype(jnp.float32)
    scale = B_HEAD_DIM ** -0.5

    def one_chunk(args):
        n, q_n = args
        k_band = lax.dynamic_slice_in_dim(k_pad, n * CHUNK, BAND, axis=1)
        v_band = lax.dynamic_slice_in_dim(v_pad, n * CHUNK, BAND, axis=1)
        s = jnp.einsum("bqhd,bkhd->bhqk", q_n, k_band).astype(jnp.float32) * scale + bias
        valid = jnp.arange(BAND) >= (LEFT_CHUNKS - n) * CHUNK
        s = jnp.where(valid, s, NEG_INF)
        p = jax.nn.softmax(s, axis=-1).astype(v_band.dtype)
        return jnp.einsum("bhqk,bkhd->bqhd", p, v_band)

    o = lax.map(one_chunk, (jnp.arange(nc, dtype=jnp.int32), q))
    o = o.transpose(1, 0, 2, 3, 4).reshape(bsz, seq, B_WIDTH)
    return o @ w_out


def conv_ffn(xn, w_up, conv_w, conv_b, w_down):
    h = causal_dwconv(xn @ w_up, conv_w) + conv_b
    gate, val = h[..., :FFN_DIM], h[..., FFN_DIM:]
    return (jax.nn.silu(gate) * val) @ w_down


def _fwd_setup_inputs(seed: int = 0) -> dict:
    key = jax.random.key(seed)
    ks = jax.random.split(key, 22)

    def nrm(k, shape, scale):
        return jax.random.normal(k, shape, jnp.float32) * scale

    dt = jnp.exp(jax.random.uniform(ks[5], (N_A_LAYERS, A_HEADS), jnp.float32,
                                    minval=math.log(1e-3), maxval=math.log(1e-1)))
    return {
        "x": nrm(ks[0], (BATCH, SEQ, D_MODEL), 1.0),
        "a_norm": 1.0 + nrm(ks[1], (N_A_LAYERS, D_MODEL), 0.02),
        "a_w_in": nrm(ks[2], (N_A_LAYERS, D_MODEL, A_IN_WIDTH), D_MODEL ** -0.5),
        "a_conv": nrm(ks[3], (N_A_LAYERS, A_CONV, A_CONV_WIDTH), A_CONV ** -0.5),
        "a_A_log": jnp.log(jax.random.uniform(ks[4], (N_A_LAYERS, A_HEADS), jnp.float32, minval=1.0, maxval=16.0)),
        "a_dt_bias": dt + jnp.log(-jnp.expm1(-dt)),
        "a_out_norm": 1.0 + nrm(ks[6], (N_A_LAYERS, A_HEAD_V), 0.02),
        "a_w_out": nrm(ks[7], (N_A_LAYERS, A_V_WIDTH, D_MODEL), A_V_WIDTH ** -0.5),
        "kv_norm": 1.0 + nrm(ks[8], (D_MODEL,), 0.02),
        "w_kv": nrm(ks[9], (D_MODEL, 2 * B_WIDTH), D_MODEL ** -0.5),
        "b_norm": 1.0 + nrm(ks[10], (N_B_LAYERS, D_MODEL), 0.02),
        "b_w_q": nrm(ks[11], (N_B_LAYERS, D_MODEL, B_WIDTH), D_MODEL ** -0.5),
        "b_rel_bias": nrm(ks[12], (N_B_LAYERS, B_HEADS, 2 * REL_CLIP + 1), 0.1),
        "b_w_out": nrm(ks[13], (N_B_LAYERS, B_WIDTH, D_MODEL), B_WIDTH ** -0.5),
        "f_norm": 1.0 + nrm(ks[14], (DEPTH, D_MODEL), 0.02),
        "f_w_up": nrm(ks[15], (DEPTH, D_MODEL, 2 * FFN_DIM), D_MODEL ** -0.5),
        "f_conv": nrm(ks[16], (DEPTH, FFN_CONV, 2 * FFN_DIM), FFN_CONV ** -0.5),
        "f_conv_b": nrm(ks[17], (DEPTH, 2 * FFN_DIM), 0.01),
        "f_w_down": nrm(ks[18], (DEPTH, FFN_DIM, D_MODEL), FFN_DIM ** -0.5),
        "final_norm": 1.0 + nrm(ks[19], (D_MODEL,), 0.02),
    }


def _fwd_reference(x, a_norm, a_w_in, a_conv, a_A_log, a_dt_bias, a_out_norm, a_w_out,
              kv_norm, w_kv, b_norm, b_w_q, b_rel_bias, b_w_out,
              f_norm, f_w_up, f_conv, f_conv_b, f_w_down, final_norm):
    bsz, seq, _ = x.shape
    h = x
    k_pad = None
    v_pad = None
    for layer in range(DEPTH):
        if layer < N_A_LAYERS:
            i = layer
            h = h + gated_deltanet(rmsnorm(h, a_norm[i]), a_w_in[i], a_conv[i], a_A_log[i],
                                   a_dt_bias[i], a_out_norm[i], a_w_out[i])
        else:
            if layer == N_A_LAYERS:
                kv = rmsnorm(h, kv_norm) @ w_kv
                pad = ((0, 0), (LEFT_CHUNKS * CHUNK, 0), (0, 0), (0, 0))
                k_pad = jnp.pad(kv[..., :B_WIDTH].reshape(bsz, seq, B_HEADS, B_HEAD_DIM), pad)
                v_pad = jnp.pad(kv[..., B_WIDTH:].reshape(bsz, seq, B_HEADS, B_HEAD_DIM), pad)
            j = layer - N_A_LAYERS
            h = h + chunk_attention(rmsnorm(h, b_norm[j]), b_w_q[j], b_rel_bias[j], b_w_out[j], k_pad, v_pad)
        h = h + conv_ffn(rmsnorm(h, f_norm[layer]), f_w_up[layer], f_conv[layer], f_conv_b[layer], f_w_down[layer])
    return rmsnorm(h, final_norm)


import jax as _jax
import jax.numpy as _jnp

TWIN_FORMAT = 'train_step'
FWD_PARAMS = ['x', 'a_norm', 'a_w_in', 'a_conv', 'a_A_log', 'a_dt_bias', 'a_out_norm', 'a_w_out', 'kv_norm', 'w_kv', 'b_norm', 'b_w_q', 'b_rel_bias', 'b_w_out', 'f_norm', 'f_w_up', 'f_conv', 'f_conv_b', 'f_w_down', 'final_norm']
TWIN_WEIGHTS = ['a_norm', 'a_w_in', 'a_conv', 'a_A_log', 'a_dt_bias', 'a_out_norm', 'a_w_out', 'kv_norm', 'w_kv', 'b_norm', 'b_w_q', 'b_rel_bias', 'b_w_out', 'f_norm', 'f_w_up', 'f_conv', 'f_conv_b', 'f_w_down', 'final_norm']
TWIN_DIFF_INPUT = 'x'
TWIN_INPUTS = ['x', 'a_norm', 'a_w_in', 'a_conv', 'a_A_log', 'a_dt_bias', 'a_out_norm', 'a_w_out', 'kv_norm', 'w_kv', 'b_norm', 'b_w_q', 'b_rel_bias', 'b_w_out', 'f_norm', 'f_w_up', 'f_conv', 'f_conv_b', 'f_w_down', 'final_norm', 'loss_target', 'm_a_norm', 'm_a_w_in', 'm_a_conv', 'm_a_A_log', 'm_a_dt_bias', 'm_a_out_norm', 'm_a_w_out', 'm_kv_norm', 'm_w_kv', 'm_b_norm', 'm_b_w_q', 'm_b_rel_bias', 'm_b_w_out', 'm_f_norm', 'm_f_w_up', 'm_f_conv', 'm_f_conv_b', 'm_f_w_down', 'm_final_norm', 'v_a_norm', 'v_a_w_in', 'v_a_conv', 'v_a_A_log', 'v_a_dt_bias', 'v_a_out_norm', 'v_a_w_out', 'v_kv_norm', 'v_w_kv', 'v_b_norm', 'v_b_w_q', 'v_b_rel_bias', 'v_b_w_out', 'v_f_norm', 'v_f_w_up', 'v_f_conv', 'v_f_conv_b', 'v_f_w_down', 'v_final_norm']
TWIN_OUTPUTS = ['loss', 'grad_x', 'grad_a_norm', 'grad_a_w_in', 'grad_a_conv', 'grad_a_A_log', 'grad_a_dt_bias', 'grad_a_out_norm', 'grad_a_w_out', 'grad_kv_norm', 'grad_w_kv', 'grad_b_norm', 'grad_b_w_q', 'grad_b_rel_bias', 'grad_b_w_out', 'grad_f_norm', 'grad_f_w_up', 'grad_f_conv', 'grad_f_conv_b', 'grad_f_w_down', 'grad_final_norm', 'delta_a_norm', 'delta_a_w_in', 'delta_a_conv', 'delta_a_A_log', 'delta_a_dt_bias', 'delta_a_out_norm', 'delta_a_w_out', 'delta_kv_norm', 'delta_w_kv', 'delta_b_norm', 'delta_b_w_q', 'delta_b_rel_bias', 'delta_b_w_out', 'delta_f_norm', 'delta_f_w_up', 'delta_f_conv', 'delta_f_conv_b', 'delta_f_w_down', 'delta_final_norm', 'new_m_a_norm', 'new_m_a_w_in', 'new_m_a_conv', 'new_m_a_A_log', 'new_m_a_dt_bias', 'new_m_a_out_norm', 'new_m_a_w_out', 'new_m_kv_norm', 'new_m_w_kv', 'new_m_b_norm', 'new_m_b_w_q', 'new_m_b_rel_bias', 'new_m_b_w_out', 'new_m_f_norm', 'new_m_f_w_up', 'new_m_f_conv', 'new_m_f_conv_b', 'new_m_f_w_down', 'new_m_final_norm', 'new_v_a_norm', 'new_v_a_w_in', 'new_v_a_conv', 'new_v_a_A_log', 'new_v_a_dt_bias', 'new_v_a_out_norm', 'new_v_a_w_out', 'new_v_kv_norm', 'new_v_w_kv', 'new_v_b_norm', 'new_v_b_w_q', 'new_v_b_rel_bias', 'new_v_b_w_out', 'new_v_f_norm', 'new_v_f_w_up', 'new_v_f_conv', 'new_v_f_conv_b', 'new_v_f_w_down', 'new_v_final_norm']
TWIN_LEAF_KINDS = {'loss': 'loss', 'grad_x': 'grad_x', 'grad_a_norm': 'grad_w', 'grad_a_w_in': 'grad_w', 'grad_a_conv': 'grad_w', 'grad_a_A_log': 'grad_w', 'grad_a_dt_bias': 'grad_w', 'grad_a_out_norm': 'grad_w', 'grad_a_w_out': 'grad_w', 'grad_kv_norm': 'grad_w', 'grad_w_kv': 'grad_w', 'grad_b_norm': 'grad_w', 'grad_b_w_q': 'grad_w', 'grad_b_rel_bias': 'grad_w', 'grad_b_w_out': 'grad_w', 'grad_f_norm': 'grad_w', 'grad_f_w_up': 'grad_w', 'grad_f_conv': 'grad_w', 'grad_f_conv_b': 'grad_w', 'grad_f_w_down': 'grad_w', 'grad_final_norm': 'grad_w', 'delta_a_norm': 'delta_w', 'delta_a_w_in': 'delta_w', 'delta_a_conv': 'delta_w', 'delta_a_A_log': 'delta_w', 'delta_a_dt_bias': 'delta_w', 'delta_a_out_norm': 'delta_w', 'delta_a_w_out': 'delta_w', 'delta_kv_norm': 'delta_w', 'delta_w_kv': 'delta_w', 'delta_b_norm': 'delta_w', 'delta_b_w_q': 'delta_w', 'delta_b_rel_bias': 'delta_w', 'delta_b_w_out': 'delta_w', 'delta_f_norm': 'delta_w', 'delta_f_w_up': 'delta_w', 'delta_f_conv': 'delta_w', 'delta_f_conv_b': 'delta_w', 'delta_f_w_down': 'delta_w', 'delta_final_norm': 'delta_w', 'new_m_a_norm': 'new_m', 'new_m_a_w_in': 'new_m', 'new_m_a_conv': 'new_m', 'new_m_a_A_log': 'new_m', 'new_m_a_dt_bias': 'new_m', 'new_m_a_out_norm': 'new_m', 'new_m_a_w_out': 'new_m', 'new_m_kv_norm': 'new_m', 'new_m_w_kv': 'new_m', 'new_m_b_norm': 'new_m', 'new_m_b_w_q': 'new_m', 'new_m_b_rel_bias': 'new_m', 'new_m_b_w_out': 'new_m', 'new_m_f_norm': 'new_m', 'new_m_f_w_up': 'new_m', 'new_m_f_conv': 'new_m', 'new_m_f_conv_b': 'new_m', 'new_m_f_w_down': 'new_m', 'new_m_final_norm': 'new_m', 'new_v_a_norm': 'new_v', 'new_v_a_w_in': 'new_v', 'new_v_a_conv': 'new_v', 'new_v_a_A_log': 'new_v', 'new_v_a_dt_bias': 'new_v', 'new_v_a_out_norm': 'new_v', 'new_v_a_w_out': 'new_v', 'new_v_kv_norm': 'new_v', 'new_v_w_kv': 'new_v', 'new_v_b_norm': 'new_v', 'new_v_b_w_q': 'new_v', 'new_v_b_rel_bias': 'new_v', 'new_v_b_w_out': 'new_v', 'new_v_f_norm': 'new_v', 'new_v_f_w_up': 'new_v', 'new_v_f_conv': 'new_v', 'new_v_f_conv_b': 'new_v', 'new_v_f_w_down': 'new_v', 'new_v_final_norm': 'new_v'}


def _forward(args):
    return _fwd_reference(*[args[k] for k in FWD_PARAMS])


def _output_shape():
    def fwd():
        inp = _fwd_setup_inputs(0)
        return _fwd_reference(*[inp[k] for k in FWD_PARAMS])
    out = _jax.eval_shape(fwd)
    return out.shape, out.dtype

N_MICROBATCH = 1
ADAM_LR = 0.001
ADAM_B1 = 0.9
ADAM_B2 = 0.999
ADAM_EPS = 1e-08
ADAM_WD = 0.01
ADAM_STEP = 10
PER_EXAMPLE_BATCH_AXIS = {'x': 0, 'loss_target': 0}
SHARED_INPUTS = []
_WEIGHT_DTYPES = {'a_norm': _jnp.float32, 'a_w_in': _jnp.float32, 'a_conv': _jnp.float32, 'a_A_log': _jnp.float32, 'a_dt_bias': _jnp.float32, 'a_out_norm': _jnp.float32, 'a_w_out': _jnp.float32, 'kv_norm': _jnp.float32, 'w_kv': _jnp.float32, 'b_norm': _jnp.float32, 'b_w_q': _jnp.float32, 'b_rel_bias': _jnp.float32, 'b_w_out': _jnp.float32, 'f_norm': _jnp.float32, 'f_w_up': _jnp.float32, 'f_conv': _jnp.float32, 'f_conv_b': _jnp.float32, 'f_w_down': _jnp.float32, 'final_norm': _jnp.float32}
MOMENT_SCALE = {'a_norm': 2.828885e-01, 'a_w_in': 1.397760e-01, 'a_conv': 1.279276e-01, 'a_A_log': 8.365639e-01, 'a_dt_bias': 7.884344e-01, 'a_out_norm': 4.889831e-01, 'a_w_out': 1.672425e-01, 'kv_norm': 5.655929e-02, 'w_kv': 3.977320e-02, 'b_norm': 2.378925e-02, 'b_w_q': 2.402158e-02, 'b_rel_bias': 8.113066e-03, 'b_w_out': 3.133135e-02, 'f_norm': 1.762333e-01, 'f_w_up': 7.232529e-02, 'f_conv': 7.262539e-02, 'f_conv_b': 7.299730e-02, 'f_w_down': 1.182996e-01, 'final_norm': 6.400963e+01}


def _to_microbatches(a, axis):
    t = _jnp.moveaxis(a, axis, 0)
    t = t.reshape((N_MICROBATCH, t.shape[0] // N_MICROBATCH) + t.shape[1:])
    return _jnp.moveaxis(t, 1, axis + 1)


def setup_inputs(seed: int = 0) -> dict:
    inp = _fwd_setup_inputs(seed)
    key = _jax.random.fold_in(_jax.random.key(seed), 7919)
    shape, _ = _output_shape()
    out = dict(inp)
    out["loss_target"] = _jax.random.normal(_jax.random.fold_in(key, 0), shape, _jnp.float32)
    for i, name in enumerate(TWIN_WEIGHTS):
        w = inp[name].astype(_jnp.float32)
        if MOMENT_SCALE is None:
            s = _jnp.sqrt(_jnp.mean(_jnp.square(w)) + 1e-30)
        else:
            s = MOMENT_SCALE[name]
        km, kv = _jax.random.split(_jax.random.fold_in(key, i + 1))
        out[name] = w
        out["m_" + name] = s * _jax.random.normal(km, w.shape, _jnp.float32)
        out["v_" + name] = (s * s) * _jax.random.uniform(kv, w.shape, _jnp.float32, 0.5, 1.5)
    if N_MICROBATCH > 1:
        for name, axis in PER_EXAMPLE_BATCH_AXIS.items():
            out[name] = _to_microbatches(out[name], axis)
    return {'x': out['x'], 'a_norm': out['a_norm'], 'a_w_in': out['a_w_in'], 'a_conv': out['a_conv'], 'a_A_log': out['a_A_log'], 'a_dt_bias': out['a_dt_bias'], 'a_out_norm': out['a_out_norm'], 'a_w_out': out['a_w_out'], 'kv_norm': out['kv_norm'], 'w_kv': out['w_kv'], 'b_norm': out['b_norm'], 'b_w_q': out['b_w_q'], 'b_rel_bias': out['b_rel_bias'], 'b_w_out': out['b_w_out'], 'f_norm': out['f_norm'], 'f_w_up': out['f_w_up'], 'f_conv': out['f_conv'], 'f_conv_b': out['f_conv_b'], 'f_w_down': out['f_w_down'], 'final_norm': out['final_norm'], 'loss_target': out['loss_target'], 'm_a_norm': out['m_a_norm'], 'm_a_w_in': out['m_a_w_in'], 'm_a_conv': out['m_a_conv'], 'm_a_A_log': out['m_a_A_log'], 'm_a_dt_bias': out['m_a_dt_bias'], 'm_a_out_norm': out['m_a_out_norm'], 'm_a_w_out': out['m_a_w_out'], 'm_kv_norm': out['m_kv_norm'], 'm_w_kv': out['m_w_kv'], 'm_b_norm': out['m_b_norm'], 'm_b_w_q': out['m_b_w_q'], 'm_b_rel_bias': out['m_b_rel_bias'], 'm_b_w_out': out['m_b_w_out'], 'm_f_norm': out['m_f_norm'], 'm_f_w_up': out['m_f_w_up'], 'm_f_conv': out['m_f_conv'], 'm_f_conv_b': out['m_f_conv_b'], 'm_f_w_down': out['m_f_w_down'], 'm_final_norm': out['m_final_norm'], 'v_a_norm': out['v_a_norm'], 'v_a_w_in': out['v_a_w_in'], 'v_a_conv': out['v_a_conv'], 'v_a_A_log': out['v_a_A_log'], 'v_a_dt_bias': out['v_a_dt_bias'], 'v_a_out_norm': out['v_a_out_norm'], 'v_a_w_out': out['v_a_w_out'], 'v_kv_norm': out['v_kv_norm'], 'v_w_kv': out['v_w_kv'], 'v_b_norm': out['v_b_norm'], 'v_b_w_q': out['v_b_w_q'], 'v_b_rel_bias': out['v_b_rel_bias'], 'v_b_w_out': out['v_b_w_out'], 'v_f_norm': out['v_f_norm'], 'v_f_w_up': out['v_f_w_up'], 'v_f_conv': out['v_f_conv'], 'v_f_conv_b': out['v_f_conv_b'], 'v_f_w_down': out['v_f_w_down'], 'v_final_norm': out['v_final_norm']}


def _loss(weights, diff, rest, loss_target):
    with _jax.named_scope("forward"):
        args = {**rest, TWIN_DIFF_INPUT: diff, **{k: w.astype(_WEIGHT_DTYPES[k]) for k, w in weights.items()}}
        y = _forward(args)
    with _jax.named_scope("loss_head"):
        err = _jnp.square(y.astype(_jnp.float32) - loss_target)
        return 0.5 * _jnp.sum(_jnp.mean(err, axis=-1)) if err.ndim else 0.5 * err


def _adamw(w, g, m, v):
    m = ADAM_B1 * m + (1.0 - ADAM_B1) * g
    v = ADAM_B2 * v + (1.0 - ADAM_B2) * _jnp.square(g)
    m_hat = m / (1.0 - ADAM_B1 ** ADAM_STEP)
    v_hat = v / (1.0 - ADAM_B2 ** ADAM_STEP)
    delta = -ADAM_LR * (m_hat / (_jnp.sqrt(v_hat) + ADAM_EPS) + ADAM_WD * w)
    return delta, m, v


def reference(x, a_norm, a_w_in, a_conv, a_A_log, a_dt_bias, a_out_norm, a_w_out, kv_norm, w_kv, b_norm, b_w_q, b_rel_bias, b_w_out, f_norm, f_w_up, f_conv, f_conv_b, f_w_down, final_norm, loss_target, m_a_norm, m_a_w_in, m_a_conv, m_a_A_log, m_a_dt_bias, m_a_out_norm, m_a_w_out, m_kv_norm, m_w_kv, m_b_norm, m_b_w_q, m_b_rel_bias, m_b_w_out, m_f_norm, m_f_w_up, m_f_conv, m_f_conv_b, m_f_w_down, m_final_norm, v_a_norm, v_a_w_in, v_a_conv, v_a_A_log, v_a_dt_bias, v_a_out_norm, v_a_w_out, v_kv_norm, v_w_kv, v_b_norm, v_b_w_q, v_b_rel_bias, v_b_w_out, v_f_norm, v_f_w_up, v_f_conv, v_f_conv_b, v_f_w_down, v_final_norm):
    given = dict(x=x, a_norm=a_norm, a_w_in=a_w_in, a_conv=a_conv, a_A_log=a_A_log, a_dt_bias=a_dt_bias, a_out_norm=a_out_norm, a_w_out=a_w_out, kv_norm=kv_norm, w_kv=w_kv, b_norm=b_norm, b_w_q=b_w_q, b_rel_bias=b_rel_bias, b_w_out=b_w_out, f_norm=f_norm, f_w_up=f_w_up, f_conv=f_conv, f_conv_b=f_conv_b, f_w_down=f_w_down, final_norm=final_norm, loss_target=loss_target, m_a_norm=m_a_norm, m_a_w_in=m_a_w_in, m_a_conv=m_a_conv, m_a_A_log=m_a_A_log, m_a_dt_bias=m_a_dt_bias, m_a_out_norm=m_a_out_norm, m_a_w_out=m_a_w_out, m_kv_norm=m_kv_norm, m_w_kv=m_w_kv, m_b_norm=m_b_norm, m_b_w_q=m_b_w_q, m_b_rel_bias=m_b_rel_bias, m_b_w_out=m_b_w_out, m_f_norm=m_f_norm, m_f_w_up=m_f_w_up, m_f_conv=m_f_conv, m_f_conv_b=m_f_conv_b, m_f_w_down=m_f_w_down, m_final_norm=m_final_norm, v_a_norm=v_a_norm, v_a_w_in=v_a_w_in, v_a_conv=v_a_conv, v_a_A_log=v_a_A_log, v_a_dt_bias=v_a_dt_bias, v_a_out_norm=v_a_out_norm, v_a_w_out=v_a_w_out, v_kv_norm=v_kv_norm, v_w_kv=v_w_kv, v_b_norm=v_b_norm, v_b_w_q=v_b_w_q, v_b_rel_bias=v_b_rel_bias, v_b_w_out=v_b_w_out, v_f_norm=v_f_norm, v_f_w_up=v_f_w_up, v_f_conv=v_f_conv, v_f_conv_b=v_f_conv_b, v_f_w_down=v_f_w_down, v_final_norm=v_final_norm)
    weights = {n: given[n] for n in TWIN_WEIGHTS}
    shared = {n: given[n] for n in SHARED_INPUTS}
    per_example = {n: given[n] for n in ['x']}
    grad_fn = _jax.value_and_grad(_loss, argnums=(0, 1))

    def one_microbatch(ex, loss_target):
        ex = dict(ex)
        diff = ex.pop(TWIN_DIFF_INPUT)
        return grad_fn(weights, diff, {**shared, **ex}, loss_target)

    if N_MICROBATCH == 1:
        loss, (grad_w, grad_x) = one_microbatch(per_example, given["loss_target"])
    else:
        def body(carry, xs):
            loss_sum, grad_sum = carry
            l_k, (gw_k, gx_k) = one_microbatch(xs[0], xs[1])
            with _jax.named_scope("update"):
                return (loss_sum + l_k, _jax.tree.map(_jnp.add, grad_sum, gw_k)), gx_k

        init = (_jnp.zeros((), _jnp.float32), _jax.tree.map(_jnp.zeros_like, weights))
        (loss, grad_w), grad_x = _jax.lax.scan(body, init, (per_example, given["loss_target"]))
    with _jax.named_scope("update"):
        delta_w, new_m, new_v = {}, {}, {}
        for n in TWIN_WEIGHTS:
            delta_w[n], new_m[n], new_v[n] = _adamw(weights[n], grad_w[n], given["m_" + n], given["v_" + n])
    return (loss, grad_x, *[grad_w[n] for n in TWIN_WEIGHTS], *[delta_w[n] for n in TWIN_WEIGHTS],
            *[new_m[n] for n in TWIN_WEIGHTS], *[new_v[n] for n in TWIN_WEIGHTS])
```

```python
import functools
import math

import jax
import jax.numpy as jnp
from jax import lax
from jax.experimental import pallas as pl
from jax.experimental.pallas import tpu as pltpu

F32 = jnp.float32
BF16 = jnp.bfloat16
HIGHEST = lax.Precision.HIGHEST
MESH = pl.DeviceIdType.MESH

D_MODEL = 1024
CHUNK = 64
A_HEADS = 8
A_HEAD = 128
A_QK = A_HEADS * A_HEAD
A_CONV_WIDTH = 3 * A_QK
B_HEADS = 16
B_HEAD = 64
LEFT = 8 * CHUNK
QBLK = 4 * CHUNK
KBLK = LEFT + QBLK
REL_CLIP = 256
REL_PAD = 640
FFN_DIM = 2816
EPS = 1e-6
NEG_INF = -1e30
LANE = 128
SUB = 8
VMEM_LIMIT = 56 * 1024 * 1024

ADAM_LR = 0.001
ADAM_B1 = 0.9
ADAM_B2 = 0.999
ADAM_EPS = 1e-08
ADAM_WD = 0.01
ADAM_STEP = 10


def _params(sem=None):
    return pltpu.CompilerParams(dimension_semantics=sem, vmem_limit_bytes=VMEM_LIMIT)


def _tile(n, cap):
    if n <= cap:
        return n
    best = None
    for t in range(LANE, cap + 1, LANE):
        if n % t == 0:
            best = t
    assert best is not None, n
    return best


def _sigmoid(x):
    return 1.0 / (1.0 + jnp.exp(-x))


def _softplus(x):
    return jnp.maximum(x, 0.0) + jnp.log(1.0 + jnp.exp(-jnp.abs(x)))


def _dot(a, b, dims, prec=None):
    return lax.dot_general(a, b, (dims, ((), ())), preferred_element_type=F32, precision=prec)


NN = ((1,), (0,))
NT = ((1,), (1,))
TN = ((0,), (0,))


def _bdot(a, b, dims):
    return _dot(a.astype(BF16), b.astype(BF16), dims)


def _mm(a, b, mode="nn", out_dtype=F32, res=None, name="mm"):
    if mode == "nn":
        (m, k), (k2, n) = a.shape, b.shape
    elif mode == "nt":
        (m, k), (n, k2) = a.shape, b.shape
    else:
        (k, m), (k2, n) = a.shape, b.shape
    assert k == k2, (a.shape, b.shape, mode)
    tm, tn, tk = _tile(m, 1408), _tile(n, 1408), _tile(k, 1408)
    if m == 8192:
        tm = 1024
    if k == 8192:
        tk = 1024
    nk = k // tk
    dims = {"nn": NN, "nt": NT, "tn": TN}[mode]
    a_spec = {"nn": pl.BlockSpec((tm, tk), lambda i, j, kk: (i, kk)),
              "nt": pl.BlockSpec((tm, tk), lambda i, j, kk: (i, kk)),
              "tn": pl.BlockSpec((tk, tm), lambda i, j, kk: (kk, i))}[mode]
    b_spec = {"nn": pl.BlockSpec((tk, tn), lambda i, j, kk: (kk, j)),
              "nt": pl.BlockSpec((tn, tk), lambda i, j, kk: (j, kk)),
              "tn": pl.BlockSpec((tk, tn), lambda i, j, kk: (kk, j))}[mode]
    o_spec = pl.BlockSpec((tm, tn), lambda i, j, kk: (i, j))
    has_res = res is not None

    def body(a_ref, b_ref, *rest):
        if has_res:
            res_ref, o_ref, acc = rest
        else:
            o_ref, acc = rest
        kk = pl.program_id(2)

        @pl.when(kk == 0)
        def _():
            acc[...] = jnp.zeros_like(acc)

        acc[...] += _bdot(a_ref[...], b_ref[...], dims)

        @pl.when(kk == nk - 1)
        def _():
            r = acc[...]
            if has_res:
                r = r + res_ref[...]
            o_ref[...] = r.astype(out_dtype)

    args = [a, b] + ([res] if has_res else [])
    in_specs = [a_spec, b_spec] + ([o_spec] if has_res else [])
    return pl.pallas_call(
        body, name=name, grid=(m // tm, n // tn, nk),
        in_specs=in_specs, out_specs=o_spec,
        out_shape=jax.ShapeDtypeStruct((m, n), out_dtype),
        scratch_shapes=[pltpu.VMEM((tm, tn), F32)],
        compiler_params=_params(("parallel", "parallel", "arbitrary")),
    )(*args)


def _rmsnorm_fwd(x, g):
    s, d = x.shape
    tr = _tile(s, 1024)

    def body(x_ref, g_ref, o_ref):
        xv = x_ref[...]
        r = lax.rsqrt(jnp.mean(xv * xv, axis=-1, keepdims=True) + EPS)
        o_ref[...] = (xv * r * g_ref[...]).astype(BF16)

    return pl.pallas_call(
        body, name="rmsnorm_fwd", grid=(s // tr,),
        in_specs=[pl.BlockSpec((tr, d), lambda i: (i, 0)), pl.BlockSpec((1, d), lambda i: (0, 0))],
        out_specs=pl.BlockSpec((tr, d), lambda i: (i, 0)),
        out_shape=jax.ShapeDtypeStruct((s, d), BF16),
        compiler_params=_params(("parallel",)),
    )(x, g)


def _rmsnorm_bwd(x, g, dxn, dres):
    s, d = x.shape
    tr = _tile(s, 1024)

    def body(x_ref, g_ref, dxn_ref, dres_ref, dx_ref, dg_ref):
        @pl.when(pl.program_id(0) == 0)
        def _():
            dg_ref[...] = jnp.zeros_like(dg_ref)

        xv = x_ref[...]
        r = lax.rsqrt(jnp.mean(xv * xv, axis=-1, keepdims=True) + EPS)
        dy = dxn_ref[...]
        t = dy * g_ref[...]
        c = jnp.mean(t * xv, axis=-1, keepdims=True)
        dx_ref[...] = dres_ref[...] + r * t - xv * (r * r * r) * c
        dg_ref[...] += jnp.sum(dy * xv * r, axis=0, keepdims=True)

    row = pl.BlockSpec((tr, d), lambda i: (i, 0))
    vec = pl.BlockSpec((1, d), lambda i: (0, 0))
    return pl.pallas_call(
        body, name="rmsnorm_bwd", grid=(s // tr,),
        in_specs=[row, vec, row, row], out_specs=[row, vec],
        out_shape=[jax.ShapeDtypeStruct((s, d), F32), jax.ShapeDtypeStruct((1, d), F32)],
        compiler_params=_params(("arbitrary",)),
    )(x, g, dxn, dres)


def _final_loss(h, g, tgt):
    s, d = h.shape
    tr = _tile(s, 1024)

    def body(x_ref, g_ref, t_ref, loss_ref, dx_ref, dg_ref):
        @pl.when(pl.program_id(0) == 0)
        def _():
            dg_ref[...] = jnp.zeros_like(dg_ref)
            loss_ref[...] = jnp.zeros_like(loss_ref)

        xv = x_ref[...]
        r = lax.rsqrt(jnp.mean(xv * xv, axis=-1, keepdims=True) + EPS)
        xh = xv * r
        err = xh * g_ref[...] - t_ref[...]
        per_row = jnp.mean(err * err, axis=-1, keepdims=True)
        loss_ref[...] += 0.5 * jnp.sum(per_row, axis=0, keepdims=True)
        dy = err * (1.0 / d)
        t = dy * g_ref[...]
        c = jnp.mean(t * xv, axis=-1, keepdims=True)
        dx_ref[...] = r * t - xv * (r * r * r) * c
        dg_ref[...] += jnp.sum(dy * xh, axis=0, keepdims=True)

    row = pl.BlockSpec((tr, d), lambda i: (i, 0))
    vec = pl.BlockSpec((1, d), lambda i: (0, 0))
    return pl.pallas_call(
        body, name="final_loss", grid=(s // tr,),
        in_specs=[row, vec, row],
        out_specs=[pl.BlockSpec((1, LANE), lambda i: (0, 0)), row, vec],
        out_shape=[jax.ShapeDtypeStruct((1, LANE), F32), jax.ShapeDtypeStruct((s, d), F32),
                   jax.ShapeDtypeStruct((1, d), F32)],
        compiler_params=_params(("arbitrary",)),
    )(h, g, tgt)


def _shift_rows(cur, edge, k, reverse):
    if k == 0:
        return cur
    tr, tc = cur.shape
    row = lax.broadcasted_iota(jnp.int32, (SUB, tc), 0)
    if not reverse:
        r = pltpu.roll(cur, k, 0)
        e = pltpu.roll(edge, k, 0)
        first = jnp.where(row < k, e, r[:SUB])
        return jnp.concatenate([first, r[SUB:]], axis=0)
    r = pltpu.roll(cur, tr - k, 0)
    e = pltpu.roll(edge, SUB - k, 0)
    last = jnp.where(row >= SUB - k, e, r[tr - SUB:])
    return jnp.concatenate([r[:tr - SUB], last], axis=0)


def _conv_taps(cur, edge, w, reverse):
    width = w.shape[0]
    acc = None
    for k in range(width):
        term = _shift_rows(cur, edge, width - 1 - k, reverse) * w[k:k + 1, :]
        acc = term if acc is None else acc + term
    return acc


def _conv_tiles(s, c):
    return _tile(s, 256), _tile(c, 1408)


def _conv_silu_fwd(pre, w):
    s, c = pre.shape
    width = w.shape[0]
    tr, tc = _conv_tiles(s, c)

    def body(x_ref, w_ref, o_ref, tail):
        @pl.when(pl.program_id(1) == 0)
        def _():
            tail[...] = jnp.zeros_like(tail)

        cur = x_ref[...]
        y = _conv_taps(cur, tail[...], w_ref[...], False)
        o_ref[...] = y * _sigmoid(y)
        tail[...] = cur[tr - SUB:]

    blk = pl.BlockSpec((tr, tc), lambda j, i: (i, j))
    return pl.pallas_call(
        body, name="conv_silu_fwd", grid=(c // tc, s // tr),
        in_specs=[blk, pl.BlockSpec((width, tc), lambda j, i: (0, j))], out_specs=blk,
        out_shape=jax.ShapeDtypeStruct((s, c), F32),
        scratch_shapes=[pltpu.VMEM((SUB, tc), F32)],
        compiler_params=_params(("parallel", "arbitrary")),
    )(pre, w)


def _conv_silu_bwd(pre, w, dact):
    s, c = pre.shape
    width = w.shape[0]
    tr, tc = _conv_tiles(s, c)

    def body(x_ref, w_ref, d_ref, dy_ref, dw_ref, tail):
        @pl.when(pl.program_id(1) == 0)
        def _():
            tail[...] = jnp.zeros_like(tail)
            dw_ref[...] = jnp.zeros_like(dw_ref)

        cur = x_ref[...]
        edge = tail[...]
        y = _conv_taps(cur, edge, w_ref[...], False)
        sg = _sigmoid(y)
        dy = d_ref[...] * sg * (1.0 + y * (1.0 - sg))
        dy_ref[...] = dy
        for k in range(width):
            xs = _shift_rows(cur, edge, width - 1 - k, False)
            dw_ref[k:k + 1, :] += jnp.sum(dy * xs, axis=0, keepdims=True)
        tail[...] = cur[tr - SUB:]

    blk = pl.BlockSpec((tr, tc), lambda j, i: (i, j))
    wblk = pl.BlockSpec((width, tc), lambda j, i: (0, j))
    return pl.pallas_call(
        body, name="conv_silu_bwd", grid=(c // tc, s // tr),
        in_specs=[blk, wblk, blk], out_specs=[blk, wblk],
        out_shape=[jax.ShapeDtypeStruct((s, c), F32), jax.ShapeDtypeStruct((width, c), F32)],
        scratch_shapes=[pltpu.VMEM((SUB, tc), F32)],
        compiler_params=_params(("parallel", "arbitrary")),
    )(pre, w, dact)


def _conv_transpose(dy, w):
    s, c = dy.shape
    width = w.shape[0]
    tr, tc = _conv_tiles(s, c)
    nr = s // tr

    def body(d_ref, w_ref, o_ref, head):
        @pl.when(pl.program_id(1) == 0)
        def _():
            head[...] = jnp.zeros_like(head)

        cur = d_ref[...]
        o_ref[...] = _conv_taps(cur, head[...], w_ref[...], True)
        head[...] = cur[:SUB]

    blk = pl.BlockSpec((tr, tc), lambda j, i: (nr - 1 - i, j))
    return pl.pallas_call(
        body, name="conv_transpose", grid=(c // tc, nr),
        in_specs=[blk, pl.BlockSpec((width, tc), lambda j, i: (0, j))], out_specs=blk,
        out_shape=jax.ShapeDtypeStruct((s, c), F32),
        scratch_shapes=[pltpu.VMEM((SUB, tc), F32)],
        compiler_params=_params(("parallel", "arbitrary")),
    )(dy, w)


def _ffn_act_fwd(pg, pv, wg, wv, bg, bv):
    s, c = pg.shape
    width = wg.shape[0]
    tr, tc = _conv_tiles(s, c)

    def body(g_ref, v_ref, wg_ref, wv_ref, bg_ref, bv_ref, o_ref, tg, tv):
        @pl.when(pl.program_id(1) == 0)
        def _():
            tg[...] = jnp.zeros_like(tg)
            tv[...] = jnp.zeros_like(tv)

        cg, cv = g_ref[...], v_ref[...]
        yg = _conv_taps(cg, tg[...], wg_ref[...], False) + bg_ref[...]
        yv = _conv_taps(cv, tv[...], wv_ref[...], False) + bv_ref[...]
        o_ref[...] = (yg * _sigmoid(yg) * yv).astype(BF16)
        tg[...] = cg[tr - SUB:]
        tv[...] = cv[tr - SUB:]

    blk = pl.BlockSpec((tr, tc), lambda j, i: (i, j))
    wblk = pl.BlockSpec((width, tc), lambda j, i: (0, j))
    bblk = pl.BlockSpec((1, tc), lambda j, i: (0, j))
    return pl.pallas_call(
        body, name="ffn_act_fwd", grid=(c // tc, s // tr),
        in_specs=[blk, blk, wblk, wblk, bblk, bblk], out_specs=blk,
        out_shape=jax.ShapeDtypeStruct((s, c), BF16),
        scratch_shapes=[pltpu.VMEM((SUB, tc), F32), pltpu.VMEM((SUB, tc), F32)],
        compiler_params=_params(("parallel", "arbitrary")),
    )(pg, pv, wg, wv, bg, bv)


def _ffn_act_bwd(pg, pv, wg, wv, bg, bv, dact):
    s, c = pg.shape
    width = wg.shape[0]
    tr, tc = _conv_tiles(s, c)

    def body(g_ref, v_ref, wg_ref, wv_ref, bg_ref, bv_ref, d_ref,
             dyg_ref, dyv_ref, dwg_ref, dwv_ref, dbg_ref, dbv_ref, tg, tv):
        @pl.when(pl.program_id(1) == 0)
        def _():
            for r in (tg, tv, dwg_ref, dwv_ref, dbg_ref, dbv_ref):
                r[...] = jnp.zeros_like(r)

        cg, cv = g_ref[...], v_ref[...]
        eg, ev = tg[...], tv[...]
        yg = _conv_taps(cg, eg, wg_ref[...], False) + bg_ref[...]
        yv = _conv_taps(cv, ev, wv_ref[...], False) + bv_ref[...]
        sg = _sigmoid(yg)
        da = d_ref[...]
        dyv = da * yg * sg
        dyg = da * yv * sg * (1.0 + yg * (1.0 - sg))
        dyg_ref[...] = dyg
        dyv_ref[...] = dyv
        dbg_ref[...] += jnp.sum(dyg, axis=0, keepdims=True)
        dbv_ref[...] += jnp.sum(dyv, axis=0, keepdims=True)
        for k in range(width):
            dwg_ref[k:k + 1, :] += jnp.sum(dyg * _shift_rows(cg, eg, width - 1 - k, False), axis=0, keepdims=True)
            dwv_ref[k:k + 1, :] += jnp.sum(dyv * _shift_rows(cv, ev, width - 1 - k, False), axis=0, keepdims=True)
        tg[...] = cg[tr - SUB:]
        tv[...] = cv[tr - SUB:]

    blk = pl.BlockSpec((tr, tc), lambda j, i: (i, j))
    wblk = pl.BlockSpec((width, tc), lambda j, i: (0, j))
    bblk = pl.BlockSpec((1, tc), lambda j, i: (0, j))
    big = jax.ShapeDtypeStruct((s, c), F32)
    wsh = jax.ShapeDtypeStruct((width, c), F32)
    bsh = jax.ShapeDtypeStruct((1, c), F32)
    return pl.pallas_call(
        body, name="ffn_act_bwd", grid=(c // tc, s // tr),
        in_specs=[blk, blk, wblk, wblk, bblk, bblk, blk],
        out_specs=[blk, blk, wblk, wblk, bblk, bblk],
        out_shape=[big, big, wsh, wsh, bsh, bsh],
        scratch_shapes=[pltpu.VMEM((SUB, tc), F32), pltpu.VMEM((SUB, tc), F32)],
        compiler_params=_params(("parallel", "arbitrary")),
    )(pg, pv, wg, wv, bg, bv, dact)


def _tri_masks():
    row = lax.broadcasted_iota(jnp.int32, (CHUNK, CHUNK), 0)
    col = lax.broadcasted_iota(jnp.int32, (CHUNK, CHUNK), 1)
    return row, col


def _tri_inv(m, row, col):
    hp = functools.partial(_dot, dims=NN, prec=HIGHEST)
    eye = (row == col).astype(F32)
    same_blk = (row >> 4) == (col >> 4)
    md = jnp.where(same_blk, m, 0.0)
    off = m - md
    p2 = hp(md, md)
    x = hp(eye - md, eye + p2)
    p4 = hp(p2, p2)
    x = hp(x, eye + p4)
    p8 = hp(p4, p4)
    x = hp(x, eye + p8)
    p = hp(x, off)
    pp = hp(p, p)
    y = hp(eye - p, eye + pp)
    return hp(y, x)


def _gdn_gates(ba, alog, dtb, row, col):
    sig = _sigmoid(ba)
    neg_a = -jnp.exp(alog)
    g = neg_a * _softplus(ba + dtb)
    lower = (row >= col).astype(F32)
    gcum = _dot(lower, g, NN, HIGHEST)
    return sig, neg_a, g, gcum


def _gdn_head_fwd(q_raw, k_raw, v, beta, gc, gr, s0, row, col):
    causal = row >= col
    strict = row > col
    rq = lax.rsqrt(jnp.sum(q_raw * q_raw, axis=-1, keepdims=True) + EPS)
    rk = lax.rsqrt(jnp.sum(k_raw * k_raw, axis=-1, keepdims=True) + EPS)
    q = q_raw * (rq * (A_HEAD ** -0.5))
    k = k_raw * rk
    decay = jnp.where(causal, jnp.exp(jnp.where(causal, gc - gr, 0.0)), 0.0)
    eg = jnp.exp(gc)
    gl = gc[CHUNK - 1:CHUNK, :]
    ekl = jnp.exp(gl - gc)
    dec = jnp.exp(gl)
    kb = k * beta
    kk = _bdot(kb, k, NT)
    t = _tri_inv(jnp.where(strict, kk * decay, 0.0), row, col)
    vb = v * beta
    kbg = kb * eg
    u = _dot(t, vb, NN, HIGHEST)
    w = _dot(t, kbg, NN, HIGHEST)
    qk = _bdot(q, k, NT)
    a = jnp.where(causal, qk * decay, 0.0)
    qd = q * eg
    ke = k * ekl
    vnew = u - _bdot(w, s0, NN)
    return dict(rq=rq, rk=rk, q=q, k=k, decay=decay, eg=eg, ekl=ekl, dec=dec, kb=kb, kk=kk, t=t, vb=vb,
                kbg=kbg, u=u, w=w, qk=qk, a=a, qd=qd, ke=ke, vnew=vnew, causal=causal, strict=strict)


def _gdn_fwd(qkv, ba, z, alog, dtb, wn):
    s = qkv.shape[0]
    nc = s // CHUNK

    def body(qkv_ref, ba_ref, z_ref, alog_ref, dtb_ref, wn_ref, y_ref, o_ref, st_ref, state):
        @pl.when(pl.program_id(0) == 0)
        def _():
            state[...] = jnp.zeros_like(state)

        row, col = _tri_masks()
        sig, _, _, gcum = _gdn_gates(ba_ref[...], alog_ref[...], dtb_ref[...], row, col)
        gt = gcum.T
        for h in range(A_HEADS):
            ln = slice(h * A_HEAD, (h + 1) * A_HEAD)
            s0 = state[h]
            f = _gdn_head_fwd(qkv_ref[:, ln], qkv_ref[:, A_QK + h * A_HEAD:A_QK + (h + 1) * A_HEAD],
                              qkv_ref[:, 2 * A_QK + h * A_HEAD:2 * A_QK + (h + 1) * A_HEAD],
                              sig[:, h:h + 1], gcum[:, 8 + h:9 + h], gt[8 + h:9 + h, :], s0, row, col)
            o = _bdot(f["qd"], s0, NN) + _bdot(f["a"], f["vnew"], NN)
            st_ref[0, h] = s0
            state[h] = s0 * f["dec"] + _bdot(f["ke"], f["vnew"], TN)
            o_ref[:, ln] = o
            r = lax.rsqrt(jnp.mean(o * o, axis=-1, keepdims=True) + EPS)
            zz = z_ref[:, ln]
            y_ref[:, ln] = (o * r * wn_ref[...] * zz * _sigmoid(zz)).astype(BF16)

    vec = pl.BlockSpec((1, LANE), lambda n: (0, 0))
    wide = pl.BlockSpec((CHUNK, A_QK), lambda n: (n, 0))
    return pl.pallas_call(
        body, name="gdn_fwd", grid=(nc,),
        in_specs=[pl.BlockSpec((CHUNK, A_CONV_WIDTH), lambda n: (n, 0)),
                  pl.BlockSpec((CHUNK, LANE), lambda n: (n, 0)), wide, vec, vec, vec],
        out_specs=[wide, wide, pl.BlockSpec((1, A_HEADS, A_HEAD, A_HEAD), lambda n: (n, 0, 0, 0))],
        out_shape=[jax.ShapeDtypeStruct((s, A_QK), BF16), jax.ShapeDtypeStruct((s, A_QK), F32),
                   jax.ShapeDtypeStruct((nc, A_HEADS, A_HEAD, A_HEAD), F32)],
        scratch_shapes=[pltpu.VMEM((A_HEADS, A_HEAD, A_HEAD), F32)],
        compiler_params=_params(("arbitrary",)),
    )(qkv, ba, z, alog, dtb, wn)


def _gdn_bwd(qkv, ba, z, o_raw, dy, states, alog, dtb, wn):
    s = qkv.shape[0]
    nc = s // CHUNK

    def body(qkv_ref, ba_ref, z_ref, o_ref, dy_ref, st_ref, alog_ref, dtb_ref, wn_ref,
             dqkv_ref, dba_ref, dz_ref, dalog_ref, ddtb_ref, dwn_ref, dstate):
        @pl.when(pl.program_id(0) == 0)
        def _():
            for r in (dstate, dalog_ref, ddtb_ref, dwn_ref):
                r[...] = jnp.zeros_like(r)

        row, col = _tri_masks()
        bat = ba_ref[...]
        sig, neg_a, g, gcum = _gdn_gates(bat, alog_ref[...], dtb_ref[...], row, col)
        gt = gcum.T
        lane = lax.broadcasted_iota(jnp.int32, (CHUNK, LANE), 1)
        ones = jnp.ones((CHUNK, LANE), F32)
        last_row = lax.broadcasted_iota(jnp.int32, (CHUNK, 1), 0) == CHUNK - 1
        wnv = wn_ref[...]
        dgc_tile = jnp.zeros((CHUNK, LANE), F32)
        dbeta_tile = jnp.zeros((CHUNK, LANE), F32)
        dwn_acc = jnp.zeros((1, LANE), F32)
        for h in range(A_HEADS):
            ln = slice(h * A_HEAD, (h + 1) * A_HEAD)
            lk = slice(A_QK + h * A_HEAD, A_QK + (h + 1) * A_HEAD)
            lv = slice(2 * A_QK + h * A_HEAD, 2 * A_QK + (h + 1) * A_HEAD)
            q_raw, k_raw, v = qkv_ref[:, ln], qkv_ref[:, lk], qkv_ref[:, lv]
            beta = sig[:, h:h + 1]
            s0 = st_ref[0, h]
            ds1 = dstate[h]
            f = _gdn_head_fwd(q_raw, k_raw, v, beta, gcum[:, 8 + h:9 + h], gt[8 + h:9 + h, :], s0, row, col)
            o = o_ref[:, ln]
            zz = z_ref[:, ln]
            dyv = dy_ref[:, ln]
            r = lax.rsqrt(jnp.mean(o * o, axis=-1, keepdims=True) + EPS)
            sz = _sigmoid(zz)
            silu = zz * sz
            dz_ref[:, ln] = dyv * o * r * wnv * sz * (1.0 + zz * (1.0 - sz))
            dwn_acc = dwn_acc + jnp.sum(dyv * silu * o * r, axis=0, keepdims=True)
            tt = dyv * silu * wnv
            do = r * tt - o * (r * r * r) * jnp.mean(tt * o, axis=-1, keepdims=True)
            dvnew = _bdot(f["a"], do, TN) + _bdot(f["ke"], ds1, NN)
            dqd = _bdot(do, s0, NT)
            da = jnp.where(f["causal"], _bdot(do, f["vnew"], NT), 0.0)
            dke = _bdot(f["vnew"], ds1, NT)
            ddec = jnp.sum(jnp.sum(s0 * ds1, axis=1, keepdims=True), axis=0, keepdims=True)
            dstate[h] = _bdot(f["qd"], do, TN) + f["dec"] * ds1 - _bdot(f["w"], dvnew, TN)
            dw = -_bdot(dvnew, s0, NT)
            t = f["t"]
            dvb = _bdot(t, dvnew, TN)
            dkbg = _bdot(t, dw, TN)
            dt = _bdot(dvnew, f["vb"], NT) + _bdot(dw, f["kbg"], NT)
            dm = -_dot(_dot(t, dt, TN, HIGHEST), t, NT, HIGHEST)
            dm = jnp.where(f["strict"], dm, 0.0)
            dkk = dm * f["decay"]
            dqk = da * f["decay"]
            ddecay = (dm * f["kk"] + da * f["qk"]) * f["decay"]
            k, q, kb = f["k"], f["q"], f["kb"]
            dkb = _bdot(dkk, k, NN) + dkbg * f["eg"]
            dk = _bdot(dkk, kb, TN) + _bdot(dqk, q, TN) + dke * f["ekl"]
            dq = _bdot(dqk, k, NN) + dqd * f["eg"]
            dk = dk + dkb * beta
            col_sum = _dot(ddecay, ones, TN, HIGHEST)[:, :1]
            dke_ke = jnp.sum(dke * f["ke"], axis=-1, keepdims=True)
            dgc = (jnp.sum(ddecay, axis=-1, keepdims=True) - col_sum
                   + jnp.sum(dqd * f["qd"], axis=-1, keepdims=True) - dke_ke
                   + jnp.sum(dkbg * f["kbg"], axis=-1, keepdims=True))
            dgl = jnp.sum(dke_ke, axis=0, keepdims=True) + ddec * f["dec"]
            dgc = dgc + jnp.where(last_row, dgl, 0.0)
            dbeta = jnp.sum(dkb * k, axis=-1, keepdims=True) + jnp.sum(dvb * v, axis=-1, keepdims=True)
            dgc_tile = dgc_tile + jnp.where(lane == 8 + h, dgc, 0.0)
            dbeta_tile = dbeta_tile + jnp.where(lane == h, dbeta, 0.0)
            dqn = dq * (A_HEAD ** -0.5)
            rq, rk = f["rq"], f["rk"]
            dqkv_ref[:, ln] = rq * dqn - q_raw * (rq * rq * rq) * jnp.sum(dqn * q_raw, axis=-1, keepdims=True)
            dqkv_ref[:, lk] = rk * dk - k_raw * (rk * rk * rk) * jnp.sum(dk * k_raw, axis=-1, keepdims=True)
            dqkv_ref[:, lv] = dvb * beta
        upper = (row <= col).astype(F32)
        dg = _dot(upper, dgc_tile, NN, HIGHEST)
        da_raw = dg * neg_a * _sigmoid(bat + dtb_ref[...])
        dba_ref[...] = jnp.where(lane < 8, dbeta_tile * sig * (1.0 - sig), jnp.where(lane < 16, da_raw, 0.0))
        dwn_ref[...] += dwn_acc
        ddtb_ref[...] += jnp.sum(da_raw, axis=0, keepdims=True)
        dalog_ref[...] += jnp.sum(dg * g, axis=0, keepdims=True)

    rev = lambda n: (nc - 1 - n, 0)
    vec = pl.BlockSpec((1, LANE), lambda n: (0, 0))
    wide = pl.BlockSpec((CHUNK, A_QK), rev)
    qkv_blk = pl.BlockSpec((CHUNK, A_CONV_WIDTH), rev)
    ba_blk = pl.BlockSpec((CHUNK, LANE), rev)
    vsh = jax.ShapeDtypeStruct((1, LANE), F32)
    return pl.pallas_call(
        body, name="gdn_bwd", grid=(nc,),
        in_specs=[qkv_blk, ba_blk, wide, wide, wide,
                  pl.BlockSpec((1, A_HEADS, A_HEAD, A_HEAD), lambda n: (nc - 1 - n, 0, 0, 0)), vec, vec, vec],
        out_specs=[qkv_blk, ba_blk, wide, vec, vec, vec],
        out_shape=[jax.ShapeDtypeStruct((s, A_CONV_WIDTH), F32), jax.ShapeDtypeStruct((s, LANE), F32),
                   jax.ShapeDtypeStruct((s, A_QK), F32), vsh, vsh, vsh],
        scratch_shapes=[pltpu.VMEM((A_HEADS, A_HEAD, A_HEAD), F32)],
        compiler_params=_params(("arbitrary",)),
    )(qkv, ba, z, o_raw, dy, states, alog, dtb, wn)


def _rel_onehot(r):
    t = lax.broadcasted_iota(jnp.int32, (REL_PAD, KBLK), 0)
    j = lax.broadcasted_iota(jnp.int32, (REL_PAD, KBLK), 1)
    idx = jnp.clip(r + LEFT - j, -REL_CLIP, REL_CLIP) + REL_CLIP
    return (t == idx).astype(F32)


def _relbias_expand(table):
    def body(t_ref, o_ref):
        r = pl.program_id(0)
        val = _dot(t_ref[...], _rel_onehot(r), NN, HIGHEST)
        j = lax.broadcasted_iota(jnp.int32, (B_HEADS, KBLK), 1)
        lo = (r // CHUNK) * CHUNK
        o_ref[0] = jnp.where((j >= lo) & (j < lo + LEFT + CHUNK), val, NEG_INF)

    return pl.pallas_call(
        body, name="relbias_expand", grid=(QBLK,),
        in_specs=[pl.BlockSpec((B_HEADS, REL_PAD), lambda r: (0, 0))],
        out_specs=pl.BlockSpec((1, B_HEADS, KBLK), lambda r: (r, 0, 0)),
        out_shape=jax.ShapeDtypeStruct((QBLK, B_HEADS, KBLK), F32),
        compiler_params=_params(("parallel",)),
    )(table)


def _relbias_reduce(ds):
    def body(d_ref, o_ref):
        r = pl.program_id(0)

        @pl.when(r == 0)
        def _():
            o_ref[...] = jnp.zeros_like(o_ref)

        o_ref[...] += _dot(d_ref[0], _rel_onehot(r), NT, HIGHEST)

    return pl.pallas_call(
        body, name="relbias_reduce", grid=(QBLK,),
        in_specs=[pl.BlockSpec((1, B_HEADS, KBLK), lambda r: (r, 0, 0))],
        out_specs=pl.BlockSpec((B_HEADS, REL_PAD), lambda r: (0, 0)),
        out_shape=jax.ShapeDtypeStruct((B_HEADS, REL_PAD), F32),
        compiler_params=_params(("arbitrary",)),
    )(ds)


def _attn_probs(q_ref, kb, b_ref, hh, q0):
    hl = slice(hh * B_HEAD, (hh + 1) * B_HEAD)
    qh = q_ref[:, hl]
    kh = kb[:, hl]
    jpos = lax.broadcasted_iota(jnp.int32, (QBLK, KBLK), 1)
    sc = _bdot(qh, kh, NT) * (B_HEAD ** -0.5) + b_ref[hh]
    sc = jnp.where(jpos + q0 >= LEFT, sc, NEG_INF)
    mx = jnp.max(sc, axis=-1, keepdims=True)
    p = jnp.exp(sc - mx)
    return p / jnp.sum(p, axis=-1, keepdims=True), qh, kh


def _attn_fwd(q, kpad, vpad, bias):
    s = q.shape[0]

    def body(q_ref, k_ref, v_ref, b_ref, o_ref):
        q0 = pl.multiple_of(pl.program_id(1) * QBLK, QBLK)
        kb = k_ref[pl.ds(q0, KBLK), :]
        vb = v_ref[pl.ds(q0, KBLK), :]
        outs = []
        for hh in range(2):
            p, _, _ = _attn_probs(q_ref, kb, b_ref, hh, q0)
            outs.append(_bdot(p, vb[:, hh * B_HEAD:(hh + 1) * B_HEAD], NN))
        o_ref[...] = jnp.concatenate(outs, axis=1).astype(BF16)

    qblk = pl.BlockSpec((QBLK, LANE), lambda g, m: (m, g))
    kblk = pl.BlockSpec((LEFT + s, LANE), lambda g, m: (0, g))
    return pl.pallas_call(
        body, name="attn_fwd", grid=(B_HEADS // 2, s // QBLK),
        in_specs=[qblk, kblk, kblk, pl.BlockSpec((2, QBLK, KBLK), lambda g, m: (g, 0, 0))],
        out_specs=qblk,
        out_shape=jax.ShapeDtypeStruct((s, D_MODEL), BF16),
        compiler_params=_params(("parallel", "arbitrary")),
    )(q, kpad, vpad, bias)


def _attn_bwd(q, kpad, vpad, bias, do):
    s = q.shape[0]

    def body(q_ref, k_ref, v_ref, b_ref, do_ref, dq_ref, dk_ref, dv_ref, db_ref):
        @pl.when(pl.program_id(1) == 0)
        def _():
            for r in (dk_ref, dv_ref, db_ref):
                r[...] = jnp.zeros_like(r)

        q0 = pl.multiple_of(pl.program_id(1) * QBLK, QBLK)
        kb = k_ref[pl.ds(q0, KBLK), :]
        vb = v_ref[pl.ds(q0, KBLK), :]
        dqs, dks, dvs = [], [], []
        for hh in range(2):
            hl = slice(hh * B_HEAD, (hh + 1) * B_HEAD)
            p, qh, kh = _attn_probs(q_ref, kb, b_ref, hh, q0)
            doh = do_ref[:, hl]
            dp = _bdot(doh, vb[:, hl], NT)
            dsc = p * (dp - jnp.sum(p * dp, axis=-1, keepdims=True))
            db_ref[hh] += dsc
            dqs.append(_bdot(dsc, kh, NN) * (B_HEAD ** -0.5))
            dks.append(_bdot(dsc, qh, TN) * (B_HEAD ** -0.5))
            dvs.append(_bdot(p, doh, TN))
        dq_ref[...] = jnp.concatenate(dqs, axis=1)
        dk_ref[pl.ds(q0, KBLK), :] += jnp.concatenate(dks, axis=1)
        dv_ref[pl.ds(q0, KBLK), :] += jnp.concatenate(dvs, axis=1)

    qblk = pl.BlockSpec((QBLK, LANE), lambda g, m: (m, g))
    kblk = pl.BlockSpec((LEFT + s, LANE), lambda g, m: (0, g))
    bblk = pl.BlockSpec((2, QBLK, KBLK), lambda g, m: (g, 0, 0))
    return pl.pallas_call(
        body, name="attn_bwd", grid=(B_HEADS // 2, s // QBLK),
        in_specs=[qblk, kblk, kblk, bblk, qblk],
        out_specs=[qblk, kblk, kblk, bblk],
        out_shape=[jax.ShapeDtypeStruct((s, D_MODEL), F32), jax.ShapeDtypeStruct((LEFT + s, D_MODEL), F32),
                   jax.ShapeDtypeStruct((LEFT + s, D_MODEL), F32),
                   jax.ShapeDtypeStruct((B_HEADS, QBLK, KBLK), F32)],
        compiler_params=_params(("parallel", "arbitrary")),
    )(q, kpad, vpad, bias, do)


def _adamw(w, g, m, v):
    r, c = w.shape
    tr = r
    for cand in (512, 256, 128, 64, 32, 16, 8):
        if r % cand == 0 and cand * c * 4 <= 2 * 1024 * 1024:
            tr = cand
            break
    c1 = 1.0 / (1.0 - ADAM_B1 ** ADAM_STEP)
    c2 = 1.0 / (1.0 - ADAM_B2 ** ADAM_STEP)

    def body(w_ref, g_ref, m_ref, v_ref, d_ref, mo_ref, vo_ref):
        gv = g_ref[...]
        mn = ADAM_B1 * m_ref[...] + (1.0 - ADAM_B1) * gv
        vn = ADAM_B2 * v_ref[...] + (1.0 - ADAM_B2) * (gv * gv)
        mo_ref[...] = mn
        vo_ref[...] = vn
        d_ref[...] = -ADAM_LR * ((mn * c1) / (jnp.sqrt(vn * c2) + ADAM_EPS) + ADAM_WD * w_ref[...])

    blk = pl.BlockSpec((tr, c), lambda i: (i, 0))
    sh = jax.ShapeDtypeStruct((r, c), F32)
    return pl.pallas_call(
        body, name="adamw", grid=(r // tr,),
        in_specs=[blk] * 4, out_specs=[blk] * 3, out_shape=[sh] * 3,
        compiler_params=_params(("parallel",)),
    )(w, g, m, v)


def _row(v, width=None):
    v = v.reshape(1, -1)
    if width is not None and v.shape[1] < width:
        v = jnp.pad(v, ((0, 0), (0, width - v.shape[1])))
    return v


def _gate_row(v):
    return jnp.pad(v.reshape(1, A_HEADS), ((0, 0), (A_HEADS, LANE - 2 * A_HEADS)))


def _a_layer_fwd(h, p):
    xn = _rmsnorm_fwd(h, _row(p["norm"]))
    w_in = p["w_in"]
    w_qkv, w_z = w_in[:, :A_CONV_WIDTH], w_in[:, A_CONV_WIDTH:A_CONV_WIDTH + A_QK]
    w_ba = jnp.pad(w_in[:, A_CONV_WIDTH + A_QK:], ((0, 0), (0, LANE - 2 * A_HEADS)))
    pre = _mm(xn, w_qkv, name="a_qkv")
    z = _mm(xn, w_z, name="a_z")
    ba = _mm(xn, w_ba, name="a_ba")
    act = _conv_silu_fwd(pre, p["conv"])
    alog, dtb, wn = _gate_row(p["A_log"]), _gate_row(p["dt_bias"]), _row(p["out_norm"])
    y, o_raw, states = _gdn_fwd(act, ba, z, alog, dtb, wn)
    h2 = _mm(y, p["w_out"], res=h, name="a_out")
    saved = dict(h=h, xn=xn, pre=pre, z=z, ba=ba, act=act, o_raw=o_raw, y=y, states=states,
                 w_qkv=w_qkv, w_z=w_z, w_ba=w_ba, alog=alog, dtb=dtb, wn=wn)
    return h2, saved


def _a_layer_bwd(dh2, p, sv):
    g = {}
    g["w_out"] = _mm(sv["y"], dh2, "tn", name="a_out_dw")
    dy = _mm(dh2, p["w_out"], "nt", name="a_out_dx")
    dact, dba, dz, dalog, ddtb, dwn = _gdn_bwd(sv["act"], sv["ba"], sv["z"], sv["o_raw"], dy, sv["states"],
                                               sv["alog"], sv["dtb"], sv["wn"])
    dyc, g["conv"] = _conv_silu_bwd(sv["pre"], p["conv"], dact)
    dpre = _conv_transpose(dyc, p["conv"])
    xn = sv["xn"]
    g["w_in"] = jnp.concatenate([_mm(xn, dpre, "tn", name="a_qkv_dw"), _mm(xn, dz, "tn", name="a_z_dw"),
                                 _mm(xn, dba, "tn", name="a_ba_dw")[:, :2 * A_HEADS]], axis=1)
    dxn = _mm(dpre, sv["w_qkv"], "nt", name="a_qkv_dx")
    dxn = _mm(dz, sv["w_z"], "nt", res=dxn, name="a_z_dx")
    dxn = _mm(dba, sv["w_ba"], "nt", res=dxn, name="a_ba_dx")
    dh, dnorm = _rmsnorm_bwd(sv["h"], _row(p["norm"]), dxn, dh2)
    g["norm"] = dnorm[0]
    g["A_log"] = dalog[0, A_HEADS:2 * A_HEADS]
    g["dt_bias"] = ddtb[0, A_HEADS:2 * A_HEADS]
    g["out_norm"] = dwn[0]
    return dh, g


def _ffn_fwd(h, p):
    xn = _rmsnorm_fwd(h, _row(p["norm"]))
    w_g, w_v = p["w_up"][:, :FFN_DIM], p["w_up"][:, FFN_DIM:]
    cg, cv = p["conv"][:, :FFN_DIM], p["conv"][:, FFN_DIM:]
    bg, bv = _row(p["conv_b"][:FFN_DIM]), _row(p["conv_b"][FFN_DIM:])
    pg = _mm(xn, w_g, name="f_gate")
    pv = _mm(xn, w_v, name="f_val")
    act = _ffn_act_fwd(pg, pv, cg, cv, bg, bv)
    h2 = _mm(act, p["w_down"], res=h, name="f_down")
    return h2, dict(h=h, xn=xn, pg=pg, pv=pv, act=act, w_g=w_g, w_v=w_v, cg=cg, cv=cv, bg=bg, bv=bv)


def _ffn_bwd(dh2, p, sv):
    g = {}
    g["w_down"] = _mm(sv["act"], dh2, "tn", name="f_down_dw")
    dact = _mm(dh2, p["w_down"], "nt", name="f_down_dx")
    dyg, dyv, dcg, dcv, dbg, dbv = _ffn_act_bwd(sv["pg"], sv["pv"], sv["cg"], sv["cv"], sv["bg"], sv["bv"], dact)
    dpg = _conv_transpose(dyg, sv["cg"])
    dpv = _conv_transpose(dyv, sv["cv"])
    xn = sv["xn"]
    g["w_up"] = jnp.concatenate([_mm(xn, dpg, "tn", name="f_gate_dw"), _mm(xn, dpv, "tn", name="f_val_dw")], axis=1)
    g["conv"] = jnp.concatenate([dcg, dcv], axis=1)
    g["conv_b"] = jnp.concatenate([dbg[0], dbv[0]])
    dxn = _mm(dpg, sv["w_g"], "nt", name="f_gate_dx")
    dxn = _mm(dpv, sv["w_v"], "nt", res=dxn, name="f_val_dx")
    dh, dnorm = _rmsnorm_bwd(sv["h"], _row(p["norm"]), dxn, dh2)
    g["norm"] = dnorm[0]
    return dh, g


def _b_layer_fwd(h, p, kpad, vpad):
    xn = _rmsnorm_fwd(h, _row(p["norm"]))
    q = _mm(xn, p["w_q"], name="b_q")
    table = jnp.pad(p["rel_bias"], ((0, 0), (0, REL_PAD - p["rel_bias"].shape[1])))
    bias = jnp.transpose(_relbias_expand(table), (1, 0, 2))
    o = _attn_fwd(q, kpad, vpad, bias)
    h2 = _mm(o, p["w_out"], res=h, name="b_out")
    return h2, dict(h=h, xn=xn, q=q, o=o, bias=bias)


def _b_layer_bwd(dh2, p, sv, kpad, vpad):
    g = {}
    g["w_out"] = _mm(sv["o"], dh2, "tn", name="b_out_dw")
    do = _mm(dh2, p["w_out"], "nt", name="b_out_dx")
    dq, dkp, dvp, dsc = _attn_bwd(sv["q"], kpad, vpad, sv["bias"], do)
    g["rel_bias"] = _relbias_reduce(jnp.transpose(dsc, (1, 0, 2)))[:, :2 * REL_CLIP + 1]
    g["w_q"] = _mm(sv["xn"], dq, "tn", name="b_q_dw")
    dxn = _mm(dq, p["w_q"], "nt", name="b_q_dx")
    dh, dnorm = _rmsnorm_bwd(sv["h"], _row(p["norm"]), dxn, dh2)
    g["norm"] = dnorm[0]
    return dh, g, dkp, dvp


def _local_step(x, tgt, w):
    n_a = w["a_norm"].shape[0]
    n_b = w["b_norm"].shape[0]
    depth = n_a + n_b

    def a_params(i):
        return dict(norm=w["a_norm"][i], w_in=w["a_w_in"][i], conv=w["a_conv"][i], A_log=w["a_A_log"][i],
                    dt_bias=w["a_dt_bias"][i], out_norm=w["a_out_norm"][i], w_out=w["a_w_out"][i])

    def b_params(j):
        return dict(norm=w["b_norm"][j], w_q=w["b_w_q"][j], rel_bias=w["b_rel_bias"][j], w_out=w["b_w_out"][j])

    def f_params(l):
        return dict(norm=w["f_norm"][l], w_up=w["f_w_up"][l], conv=w["f_conv"][l], conv_b=w["f_conv_b"][l],
                    w_down=w["f_w_down"][l])

    h = x
    saved = []
    kv_saved = None
    kpad = vpad = None
    for layer in range(depth):
        if layer < n_a:
            h, sm = _a_layer_fwd(h, a_params(layer))
        else:
            if layer == n_a:
                xn_kv = _rmsnorm_fwd(h, _row(w["kv_norm"]))
                kv = _mm(xn_kv, w["w_kv"], out_dtype=BF16, name="kv")
                kpad = jnp.pad(kv[:, :D_MODEL], ((LEFT, 0), (0, 0)))
                vpad = jnp.pad(kv[:, D_MODEL:], ((LEFT, 0), (0, 0)))
                kv_saved = dict(h=h, xn=xn_kv)
            h, sm = _b_layer_fwd(h, b_params(layer - n_a), kpad, vpad)
        h, sf = _ffn_fwd(h, f_params(layer))
        saved.append((sm, sf))

    loss, dh, dfinal = _final_loss(h, _row(w["final_norm"]), tgt)

    ga = [None] * n_a
    gb = [None] * n_b
    gf = [None] * depth
    dk_tot = dv_tot = None
    g_kv = g_kvn = None
    for layer in reversed(range(depth)):
        sm, sf = saved[layer]
        dh, gf[layer] = _ffn_bwd(dh, f_params(layer), sf)
        if layer >= n_a:
            dh, gb[layer - n_a], dkp, dvp = _b_layer_bwd(dh, b_params(layer - n_a), sm, kpad, vpad)
            dk_tot = dkp if dk_tot is None else dk_tot + dkp
            dv_tot = dvp if dv_tot is None else dv_tot + dvp
            if layer == n_a:
                dkv = jnp.concatenate([dk_tot[LEFT:], dv_tot[LEFT:]], axis=1)
                g_kv = _mm(kv_saved["xn"], dkv, "tn", name="kv_dw")
                dxn = _mm(dkv, w["w_kv"], "nt", name="kv_dx")
                dh, g_kvn = _rmsnorm_bwd(kv_saved["h"], _row(w["kv_norm"]), dxn, dh)
        else:
            dh, ga[layer] = _a_layer_bwd(dh, a_params(layer), sm)

    def stack(gs, key):
        return jnp.stack([g[key] for g in gs])

    grads = dict(
        a_norm=stack(ga, "norm"), a_w_in=stack(ga, "w_in"), a_conv=stack(ga, "conv"), a_A_log=stack(ga, "A_log"),
        a_dt_bias=stack(ga, "dt_bias"), a_out_norm=stack(ga, "out_norm"), a_w_out=stack(ga, "w_out"),
        kv_norm=g_kvn[0], w_kv=g_kv,
        b_norm=stack(gb, "norm"), b_w_q=stack(gb, "w_q"), b_rel_bias=stack(gb, "rel_bias"), b_w_out=stack(gb, "w_out"),
        f_norm=stack(gf, "norm"), f_w_up=stack(gf, "w_up"), f_conv=stack(gf, "conv"), f_conv_b=stack(gf, "conv_b"),
        f_w_down=stack(gf, "w_down"), final_norm=dfinal[0])
    return loss, dh, grads


HBM_SPEC = pl.BlockSpec(memory_space=pl.ANY)
VMEM_SPEC = pl.BlockSpec(memory_space=pltpu.VMEM)
PACK_COLS = 1024
PACK_ROW_QUANTUM = 1280


def _place():
    x, y, c = lax.axis_index("x"), lax.axis_index("y"), lax.axis_index("c")
    chips = [(1 - x, y), (x, 1 - y), (1 - x, 1 - y)]
    return x, y, c, chips


def _remote(src, dst, send_sem, recv_sem, to):
    return pltpu.make_async_remote_copy(src_ref=src, dst_ref=dst, send_sem=send_sem, recv_sem=recv_sem,
                                        device_id=to, device_id_type=MESH)


def _allgather_weights(shard):
    r, cols = shard.shape
    rh = r // 2

    def body(x_ref, out_ref, send_sems, recv_sems, local_sem):
        x, y, c, chips = _place()
        sibling = (x, y, 1 - c)

        def half(px, py, hc):
            return out_ref.at[2 * px + py, pl.ds(hc * rh, rh), :]

        mine = pltpu.make_async_copy(x_ref, out_ref.at[2 * x + y], local_sem)
        mine.start()
        first = [_remote(x_ref.at[pl.ds(c * rh, rh), :], half(x, y, c), send_sems.at[j], recv_sems.at[j], (*chip, c))
                 for j, chip in enumerate(chips)]
        for cp in first:
            cp.start()
        passed = [_remote(half(*chip, c), half(*chip, c), send_sems.at[3 + j], recv_sems.at[3 + j], sibling)
                  for j, chip in enumerate(chips)]
        for j, chip in enumerate(chips):
            _remote(half(*chip, c), half(*chip, c), send_sems.at[j], recv_sems.at[j], (*chip, c)).wait_recv()
            passed[j].start()
        for j, chip in enumerate(chips):
            _remote(half(*chip, 1 - c), half(*chip, 1 - c), send_sems.at[3 + j], recv_sems.at[3 + j],
                    sibling).wait_recv()
        for cp in first + passed:
            cp.wait_send()
        mine.wait()

    return pl.pallas_call(
        body, name="allgather_weights",
        out_shape=jax.ShapeDtypeStruct((4, r, cols), shard.dtype),
        in_specs=[HBM_SPEC], out_specs=HBM_SPEC,
        scratch_shapes=[pltpu.SemaphoreType.DMA((6,)), pltpu.SemaphoreType.DMA((6,)), pltpu.SemaphoreType.DMA],
    )(shard)


def _pair_exchange(g):
    n, r, cols = g.shape
    rh = r // 2

    def body(g_ref, out_ref, send_sem, recv_sem):
        x, y, c, _ = _place()
        cp = _remote(g_ref.at[:, pl.ds((1 - c) * rh, rh), :], out_ref, send_sem, recv_sem, (x, y, 1 - c))
        cp.start()
        cp.wait()

    return pl.pallas_call(
        body, name="rs_pair_exchange",
        out_shape=jax.ShapeDtypeStruct((n, rh, cols), g.dtype),
        in_specs=[HBM_SPEC], out_specs=HBM_SPEC,
        scratch_shapes=[pltpu.SemaphoreType.DMA, pltpu.SemaphoreType.DMA],
    )(g)


def _pair_add(g, other, c_idx):
    n, r, cols = g.shape
    rh = r // 2
    tr = PACK_ROW_QUANTUM // 2
    nb = rh // tr

    def body(c_ref, a_ref, b_ref, o_ref):
        o_ref[...] = a_ref[...] + b_ref[...]

    return pl.pallas_call(
        body, name="rs_pair_add",
        grid_spec=pltpu.PrefetchScalarGridSpec(
            num_scalar_prefetch=1, grid=(n, nb),
            in_specs=[pl.BlockSpec((1, tr, cols), lambda s, i, c_ref: (s, c_ref[0] * nb + i, 0)),
                      pl.BlockSpec((1, tr, cols), lambda s, i, c_ref: (s, i, 0))],
            out_specs=pl.BlockSpec((1, tr, cols), lambda s, i, c_ref: (s, i, 0))),
        out_shape=jax.ShapeDtypeStruct((n, rh, cols), g.dtype),
        compiler_params=_params(("parallel", "parallel")),
    )(c_idx, g, other)


def _chip_exchange(p):
    n, rh, cols = p.shape

    def body(p_ref, out_ref, send_sems, recv_sems):
        x, y, c, chips = _place()
        cps = [_remote(p_ref.at[2 * chip[0] + chip[1]], out_ref.at[j], send_sems.at[j], recv_sems.at[j], (*chip, c))
               for j, chip in enumerate(chips)]
        for cp in cps:
            cp.start()
        for cp in cps:
            cp.wait()

    return pl.pallas_call(
        body, name="rs_chip_exchange",
        out_shape=jax.ShapeDtypeStruct((3, rh, cols), p.dtype),
        in_specs=[HBM_SPEC], out_specs=HBM_SPEC,
        scratch_shapes=[pltpu.SemaphoreType.DMA((3,)), pltpu.SemaphoreType.DMA((3,))],
    )(p)


def _chip_add(p, recv, chip_idx):
    n, rh, cols = p.shape
    tr = PACK_ROW_QUANTUM // 2

    def body(s_ref, own_ref, r_ref, o_ref):
        o_ref[...] = ((own_ref[0] + r_ref[0]) + r_ref[1]) + r_ref[2]

    return pl.pallas_call(
        body, name="rs_chip_add",
        grid_spec=pltpu.PrefetchScalarGridSpec(
            num_scalar_prefetch=1, grid=(rh // tr,),
            in_specs=[pl.BlockSpec((1, tr, cols), lambda i, s_ref: (s_ref[0], i, 0)),
                      pl.BlockSpec((3, tr, cols), lambda i, s_ref: (0, i, 0))],
            out_specs=pl.BlockSpec((tr, cols), lambda i, s_ref: (i, 0))),
        out_shape=jax.ShapeDtypeStruct((rh, cols), p.dtype),
        compiler_params=_params(("parallel",)),
    )(chip_idx, p, recv)


def _pair_gather(f):
    rh, cols = f.shape

    def body(f_ref, out_ref, send_sem, recv_sem, local_sem):
        x, y, c, _ = _place()
        mine = pltpu.make_async_copy(f_ref, out_ref.at[c], local_sem)
        mine.start()
        cp = _remote(f_ref, out_ref.at[c], send_sem, recv_sem, (x, y, 1 - c))
        cp.start()
        cp.wait()
        mine.wait()

    return pl.pallas_call(
        body, name="rs_pair_gather",
        out_shape=jax.ShapeDtypeStruct((2, rh, cols), f.dtype),
        in_specs=[HBM_SPEC], out_specs=HBM_SPEC,
        scratch_shapes=[pltpu.SemaphoreType.DMA, pltpu.SemaphoreType.DMA, pltpu.SemaphoreType.DMA],
    )(f)


def _allreduce_small(v):
    r, cols = v.shape

    def body(x_ref, out_ref, slots, send_sems, recv_sems):
        x, y, c, _ = _place()
        bits = [(bx, by, bc) for bx in (0, 1) for by in (0, 1) for bc in (0, 1)]

        def flip(b):
            return (1 - x if b[0] else x, 1 - y if b[1] else y, 1 - c if b[2] else c)

        slots[0] = x_ref[...]
        cps = [_remote(x_ref, slots.at[k], send_sems.at[k - 1], recv_sems.at[k - 1], flip(bits[k]))
               for k in range(1, 8)]
        for cp in cps:
            cp.start()
        for cp in cps:
            cp.wait()
        acc = None
        for b in bits:
            fx, fy, fc = flip(b)
            term = slots[4 * fx + 2 * fy + fc]
            acc = term if acc is None else acc + term
        out_ref[...] = acc

    return pl.pallas_call(
        body, name="allreduce_small",
        out_shape=jax.ShapeDtypeStruct((r, cols), v.dtype),
        in_specs=[VMEM_SPEC], out_specs=VMEM_SPEC,
        scratch_shapes=[pltpu.VMEM((8, r, cols), v.dtype), pltpu.SemaphoreType.DMA((7,)),
                        pltpu.SemaphoreType.DMA((7,))],
        compiler_params=pltpu.CompilerParams(vmem_limit_bytes=VMEM_LIMIT),
    )(v)


BIG = (("a_w_in", 2), ("a_w_out", 1), ("w_kv", 1), ("b_w_q", 1), ("b_w_out", 1), ("f_w_up", 2), ("f_w_down", 1))
SMALL = (("a_norm", 1), ("a_conv", 2), ("a_A_log", None), ("a_dt_bias", None), ("a_out_norm", None),
         ("kv_norm", None), ("b_norm", None), ("b_rel_bias", None), ("f_norm", None), ("f_conv", 2),
         ("f_conv_b", None), ("final_norm", None))
WEIGHT_ORDER = ("a_norm", "a_w_in", "a_conv", "a_A_log", "a_dt_bias", "a_out_norm", "a_w_out", "kv_norm", "w_kv",
                "b_norm", "b_w_q", "b_rel_bias", "b_w_out", "f_norm", "f_w_up", "f_conv", "f_conv_b", "f_w_down",
                "final_norm")


def _pad_rows(flat, cols, quantum):
    n = flat.shape[-1]
    rows = -(-n // (cols * quantum)) * quantum
    pad = [(0, 0)] * (flat.ndim - 1) + [(0, rows * cols - n)]
    return jnp.pad(flat, pad).reshape(flat.shape[:-1] + (rows, cols))


def _pack_shards(shards, dtype):
    return _pad_rows(jnp.concatenate([shards[n].astype(dtype).reshape(-1) for n, _ in BIG]), PACK_COLS,
                     PACK_ROW_QUANTUM)


def _unpack_gathered(gathered, shard_shapes):
    flat = gathered.reshape(4, -1)
    out, off = {}, 0
    for name, axis in BIG:
        shp = shard_shapes[name]
        n = math.prod(shp)
        seg = flat[:, off:off + n].reshape((4,) + shp)
        out[name] = jnp.concatenate([seg[s] for s in range(4)], axis=axis)
        off += n
    return out


def _pack_grads(grads):
    parts = []
    for name, axis in BIG:
        g = grads[name]
        parts.append(jnp.stack([piece.reshape(-1) for piece in jnp.split(g, 4, axis=axis)]))
    return _pad_rows(jnp.concatenate(parts, axis=1), PACK_COLS, PACK_ROW_QUANTUM)


def _unpack_reduced(reduced, shard_shapes):
    flat = reduced.reshape(-1)
    out, off = {}, 0
    for name, _ in BIG:
        shp = shard_shapes[name]
        n = math.prod(shp)
        out[name] = flat[off:off + n].reshape(shp)
        off += n
    return out


def _pack_small(values, names):
    return _pad_rows(jnp.concatenate([values[n].reshape(-1) for n in names]), LANE, SUB)


def _unpack_small(packed, shapes, names):
    flat = packed.reshape(-1)
    out, off = {}, 0
    for n in names:
        size = math.prod(shapes[n])
        out[n] = flat[off:off + size].reshape(shapes[n])
        off += size
    return out


def _adamw_nd(w, g, m, v):
    shp = w.shape
    two = (math.prod(shp[:-1]), shp[-1])
    d, mn, vn = _adamw(w.reshape(two), g.reshape(two), m.reshape(two), v.reshape(two))
    return d.reshape(shp), mn.reshape(shp), vn.reshape(shp)


def kernel(x, a_norm, a_w_in, a_conv, a_A_log, a_dt_bias, a_out_norm, a_w_out, kv_norm, w_kv, b_norm, b_w_q, b_rel_bias, b_w_out, f_norm, f_w_up, f_conv, f_conv_b, f_w_down, final_norm, loss_target, m_a_norm, m_a_w_in, m_a_conv, m_a_A_log, m_a_dt_bias, m_a_out_norm, m_a_w_out, m_kv_norm, m_w_kv, m_b_norm, m_b_w_q, m_b_rel_bias, m_b_w_out, m_f_norm, m_f_w_up, m_f_conv, m_f_conv_b, m_f_w_down, m_final_norm, v_a_norm, v_a_w_in, v_a_conv, v_a_A_log, v_a_dt_bias, v_a_out_norm, v_a_w_out, v_kv_norm, v_w_kv, v_b_norm, v_b_w_q, v_b_rel_bias, v_b_w_out, v_f_norm, v_f_w_up, v_f_conv, v_f_conv_b, v_f_w_down, v_final_norm):
    w = dict(a_norm=a_norm, a_w_in=a_w_in, a_conv=a_conv, a_A_log=a_A_log, a_dt_bias=a_dt_bias,
             a_out_norm=a_out_norm, a_w_out=a_w_out, kv_norm=kv_norm, w_kv=w_kv, b_norm=b_norm, b_w_q=b_w_q,
             b_rel_bias=b_rel_bias, b_w_out=b_w_out, f_norm=f_norm, f_w_up=f_w_up, f_conv=f_conv,
             f_conv_b=f_conv_b, f_w_down=f_w_down, final_norm=final_norm)
    m = dict(a_norm=m_a_norm, a_w_in=m_a_w_in, a_conv=m_a_conv, a_A_log=m_a_A_log, a_dt_bias=m_a_dt_bias,
             a_out_norm=m_a_out_norm, a_w_out=m_a_w_out, kv_norm=m_kv_norm, w_kv=m_w_kv, b_norm=m_b_norm,
             b_w_q=m_b_w_q, b_rel_bias=m_b_rel_bias, b_w_out=m_b_w_out, f_norm=m_f_norm, f_w_up=m_f_w_up,
             f_conv=m_f_conv, f_conv_b=m_f_conv_b, f_w_down=m_f_w_down, final_norm=m_final_norm)
    v = dict(a_norm=v_a_norm, a_w_in=v_a_w_in, a_conv=v_a_conv, a_A_log=v_a_A_log, a_dt_bias=v_a_dt_bias,
             a_out_norm=v_a_out_norm, a_w_out=v_a_w_out, kv_norm=v_kv_norm, w_kv=v_w_kv, b_norm=v_b_norm,
             b_w_q=v_b_w_q, b_rel_bias=v_b_rel_bias, b_w_out=v_b_w_out, f_norm=v_f_norm, f_w_up=v_f_w_up,
             f_conv=v_f_conv, f_conv_b=v_f_conv_b, f_w_down=v_f_w_down, final_norm=v_final_norm)
    xi, yi, ci = lax.axis_index("x"), lax.axis_index("y"), lax.axis_index("c")
    chip = 2 * xi + yi
    shard_shapes = {n: w[n].shape for n in WEIGHT_ORDER}

    gathered = _allgather_weights(_pack_shards(w, BF16))
    full = _unpack_gathered(gathered, shard_shapes)
    sharded_small = [n for n, axis in SMALL if axis is not None]
    placed = {}
    for n, axis in SMALL:
        if axis is not None:
            wide = list(w[n].shape)
            wide[axis] *= 4
            mine_once = w[n] * (1 - ci).astype(F32)
            placed[n] = lax.dynamic_update_slice_in_dim(jnp.zeros(wide, F32), mine_once, chip * w[n].shape[axis], axis)
    placed_shapes = {n: placed[n].shape for n in sharded_small}
    full.update(_unpack_small(_allreduce_small(_pack_small(placed, sharded_small)), placed_shapes, sharded_small))
    for n, axis in SMALL:
        if axis is None:
            full[n] = w[n]

    loss_part, grad_x, grads = _local_step(x[0], loss_target[0], full)

    packed = _pack_grads(grads)
    c_idx = jnp.reshape(ci, (1,)).astype(jnp.int32)
    chip_idx = jnp.reshape(chip, (1,)).astype(jnp.int32)
    pair = _pair_add(packed, _pair_exchange(packed), c_idx)
    mine = _chip_add(pair, _chip_exchange(pair), chip_idx)
    both = _pair_gather(mine)
    red = _unpack_reduced(both.reshape(-1, PACK_COLS), shard_shapes)

    small_names = [n for n, _ in SMALL]
    small_vals = {n: grads[n] for n in small_names}
    small_vals["loss"] = loss_part[0, :1]
    names = ["loss"] + small_names
    shapes = {n: small_vals[n].shape for n in names}
    summed = _unpack_small(_allreduce_small(_pack_small(small_vals, names)), shapes, names)
    loss = summed["loss"][0]
    for n, axis in SMALL:
        g = summed[n]
        if axis is not None:
            g = lax.dynamic_slice_in_dim(g, chip * w[n].shape[axis], w[n].shape[axis], axis)
        red[n] = g

    delta, new_m, new_v = {}, {}, {}
    for n, _ in BIG:
        delta[n], new_m[n], new_v[n] = _adamw_nd(w[n], red[n], m[n], v[n])
    local_shapes = {n: w[n].shape for n in small_names}
    packs = [_pack_small(t, small_names) for t in (w, red, m, v)]
    outs = _adamw(*packs)
    ds, ms, vs = (_unpack_small(o, local_shapes, small_names) for o in outs)
    delta.update(ds)
    new_m.update(ms)
    new_v.update(vs)

    return (loss, grad_x[None], *[red[n] for n in WEIGHT_ORDER], *[delta[n] for n in WEIGHT_ORDER],
            *[new_m[n] for n in WEIGHT_ORDER], *[new_v[n] for n in WEIGHT_ORDER])
```

```python
import functools
import math

import jax
import jax.numpy as jnp
from jax import lax
from jax.experimental import pallas as pl
from jax.experimental.pallas import tpu as pltpu

F32 = jnp.float32
BF16 = jnp.bfloat16
HIGHEST = lax.Precision.HIGHEST
MESH = pl.DeviceIdType.MESH

D_MODEL = 1024
CHUNK = 64
A_HEADS = 8
A_HEAD = 128
A_QK = A_HEADS * A_HEAD
A_CONV_WIDTH = 3 * A_QK
B_HEADS = 16
B_HEAD = 64
LEFT = 8 * CHUNK
QBLK = 4 * CHUNK
KBLK = LEFT + QBLK
REL_CLIP = 256
REL_PAD = 640
FFN_DIM = 2816
EPS = 1e-6
NEG_INF = -1e30
LANE = 128
SUB = 8
VMEM_LIMIT = 56 * 1024 * 1024

ADAM_LR = 0.001
ADAM_B1 = 0.9
ADAM_B2 = 0.999
ADAM_EPS = 1e-08
ADAM_WD = 0.01
ADAM_STEP = 10


VMEM_FULL = pl.BlockSpec(memory_space=pltpu.VMEM)


def _params(sem=None):
    return pltpu.CompilerParams(dimension_semantics=sem, vmem_limit_bytes=VMEM_LIMIT)


def _tile(n, cap):
    if n <= cap:
        return n
    best = None
    for t in range(LANE, cap + 1, LANE):
        if n % t == 0:
            best = t
    assert best is not None, n
    return best


def _sigmoid(x):
    return 1.0 / (1.0 + jnp.exp(-x))


def _softplus(x):
    return jnp.maximum(x, 0.0) + jnp.log(1.0 + jnp.exp(-jnp.abs(x)))


def _dot(a, b, dims, prec=None):
    return lax.dot_general(a, b, (dims, ((), ())), preferred_element_type=F32, precision=prec)


NN = ((1,), (0,))
NT = ((1,), (1,))
TN = ((0,), (0,))


def _bdot(a, b, dims):
    return _dot(a.astype(BF16), b.astype(BF16), dims)


def _mm(a, b, mode="nn", out_dtype=F32, res=None, name="mm"):
    if mode == "nn":
        (m, k), (k2, n) = a.shape, b.shape
    elif mode == "nt":
        (m, k), (n, k2) = a.shape, b.shape
    else:
        (k, m), (k2, n) = a.shape, b.shape
    assert k == k2, (a.shape, b.shape, mode)
    tm, tn, tk = _tile(m, 1408), _tile(n, 1408), _tile(k, 1408)
    if m == 8192:
        tm = 1024
    if k == 8192:
        tk = 1024
    nk = k // tk
    dims = {"nn": NN, "nt": NT, "tn": TN}[mode]
    a_spec = {"nn": pl.BlockSpec((tm, tk), lambda i, j, kk: (i, kk)),
              "nt": pl.BlockSpec((tm, tk), lambda i, j, kk: (i, kk)),
              "tn": pl.BlockSpec((tk, tm), lambda i, j, kk: (kk, i))}[mode]
    b_spec = {"nn": pl.BlockSpec((tk, tn), lambda i, j, kk: (kk, j)),
              "nt": pl.BlockSpec((tn, tk), lambda i, j, kk: (j, kk)),
              "tn": pl.BlockSpec((tk, tn), lambda i, j, kk: (kk, j))}[mode]
    o_spec = pl.BlockSpec((tm, tn), lambda i, j, kk: (i, j))
    has_res = res is not None

    def body(a_ref, b_ref, *rest):
        if has_res:
            res_ref, o_ref, acc = rest
        else:
            o_ref, acc = rest
        kk = pl.program_id(2)

        @pl.when(kk == 0)
        def _():
            acc[...] = jnp.zeros_like(acc)

        acc[...] += _bdot(a_ref[...], b_ref[...], dims)

        @pl.when(kk == nk - 1)
        def _():
            r = acc[...]
            if has_res:
                r = r + res_ref[...]
            o_ref[...] = r.astype(out_dtype)

    args = [a, b] + ([res] if has_res else [])
    in_specs = [a_spec, b_spec] + ([o_spec] if has_res else [])
    return pl.pallas_call(
        body, name=name, grid=(m // tm, n // tn, nk),
        in_specs=in_specs, out_specs=o_spec,
        out_shape=jax.ShapeDtypeStruct((m, n), out_dtype),
        scratch_shapes=[pltpu.VMEM((tm, tn), F32)],
        compiler_params=_params(("parallel", "parallel", "arbitrary")),
    )(*args)


def _rmsnorm_fwd(x, g):
    s, d = x.shape
    tr = _tile(s, 1024)

    def body(x_ref, g_ref, o_ref):
        xv = x_ref[...]
        r = lax.rsqrt(jnp.mean(xv * xv, axis=-1, keepdims=True) + EPS)
        o_ref[...] = (xv * r * g_ref[...]).astype(BF16)

    return pl.pallas_call(
        body, name="rmsnorm_fwd", grid=(s // tr,),
        in_specs=[pl.BlockSpec((tr, d), lambda i: (i, 0)), pl.BlockSpec((1, d), lambda i: (0, 0))],
        out_specs=pl.BlockSpec((tr, d), lambda i: (i, 0)),
        out_shape=jax.ShapeDtypeStruct((s, d), BF16),
        compiler_params=_params(("parallel",)),
    )(x, g)


def _rmsnorm_bwd(x, g, dxn, dres):
    s, d = x.shape
    tr = _tile(s, 1024)

    def body(x_ref, g_ref, dxn_ref, dres_ref, dx_ref, dg_ref):
        @pl.when(pl.program_id(0) == 0)
        def _():
            dg_ref[...] = jnp.zeros_like(dg_ref)

        xv = x_ref[...]
        r = lax.rsqrt(jnp.mean(xv * xv, axis=-1, keepdims=True) + EPS)
        dy = dxn_ref[...]
        t = dy * g_ref[...]
        c = jnp.mean(t * xv, axis=-1, keepdims=True)
        dx_ref[...] = dres_ref[...] + r * t - xv * (r * r * r) * c
        dg_ref[...] += jnp.sum(dy * xv * r, axis=0, keepdims=True)

    row = pl.BlockSpec((tr, d), lambda i: (i, 0))
    vec = pl.BlockSpec((1, d), lambda i: (0, 0))
    return pl.pallas_call(
        body, name="rmsnorm_bwd", grid=(s // tr,),
        in_specs=[row, vec, row, row], out_specs=[row, vec],
        out_shape=[jax.ShapeDtypeStruct((s, d), F32), jax.ShapeDtypeStruct((1, d), F32)],
        compiler_params=_params(("arbitrary",)),
    )(x, g, dxn, dres)


def _final_loss(h, g, tgt):
    s, d = h.shape
    tr = _tile(s, 1024)

    def body(x_ref, g_ref, t_ref, loss_ref, dx_ref, dg_ref):
        @pl.when(pl.program_id(0) == 0)
        def _():
            dg_ref[...] = jnp.zeros_like(dg_ref)
            loss_ref[...] = jnp.zeros_like(loss_ref)

        xv = x_ref[...]
        r = lax.rsqrt(jnp.mean(xv * xv, axis=-1, keepdims=True) + EPS)
        xh = xv * r
        err = xh * g_ref[...] - t_ref[...]
        per_row = jnp.mean(err * err, axis=-1, keepdims=True)
        loss_ref[...] += 0.5 * jnp.sum(per_row, axis=0, keepdims=True)
        dy = err * (1.0 / d)
        t = dy * g_ref[...]
        c = jnp.mean(t * xv, axis=-1, keepdims=True)
        dx_ref[...] = r * t - xv * (r * r * r) * c
        dg_ref[...] += jnp.sum(dy * xh, axis=0, keepdims=True)

    row = pl.BlockSpec((tr, d), lambda i: (i, 0))
    vec = pl.BlockSpec((1, d), lambda i: (0, 0))
    return pl.pallas_call(
        body, name="final_loss", grid=(s // tr,),
        in_specs=[row, vec, row],
        out_specs=[pl.BlockSpec((1, LANE), lambda i: (0, 0)), row, vec],
        out_shape=[jax.ShapeDtypeStruct((1, LANE), F32), jax.ShapeDtypeStruct((s, d), F32),
                   jax.ShapeDtypeStruct((1, d), F32)],
        compiler_params=_params(("arbitrary",)),
    )(h, g, tgt)


def _shift_rows(cur, edge, k, reverse):
    if k == 0:
        return cur
    tr, tc = cur.shape
    row = lax.broadcasted_iota(jnp.int32, (SUB, tc), 0)
    if not reverse:
        r = pltpu.roll(cur, k, 0)
        e = pltpu.roll(edge, k, 0)
        first = jnp.where(row < k, e, r[:SUB])
        return jnp.concatenate([first, r[SUB:]], axis=0)
    r = pltpu.roll(cur, tr - k, 0)
    e = pltpu.roll(edge, SUB - k, 0)
    last = jnp.where(row >= SUB - k, e, r[tr - SUB:])
    return jnp.concatenate([r[:tr - SUB], last], axis=0)


def _conv_taps(cur, edge, w, reverse):
    width = w.shape[0]
    acc = None
    for k in range(width):
        term = _shift_rows(cur, edge, width - 1 - k, reverse) * w[k:k + 1, :]
        acc = term if acc is None else acc + term
    return acc


def _conv_tiles(s, c):
    return _tile(s, 256), _tile(c, 1408)


def _conv_silu_fwd(pre, w):
    s, c = pre.shape
    width = w.shape[0]
    tr, tc = _conv_tiles(s, c)

    def body(x_ref, w_ref, o_ref, tail):
        @pl.when(pl.program_id(1) == 0)
        def _():
            tail[...] = jnp.zeros_like(tail)

        cur = x_ref[...]
        y = _conv_taps(cur, tail[...], w_ref[...], False)
        o_ref[...] = y * _sigmoid(y)
        tail[...] = cur[tr - SUB:]

    blk = pl.BlockSpec((tr, tc), lambda j, i: (i, j))
    return pl.pallas_call(
        body, name="conv_silu_fwd", grid=(c // tc, s // tr),
        in_specs=[blk, pl.BlockSpec((width, tc), lambda j, i: (0, j))], out_specs=blk,
        out_shape=jax.ShapeDtypeStruct((s, c), F32),
        scratch_shapes=[pltpu.VMEM((SUB, tc), F32)],
        compiler_params=_params(("parallel", "arbitrary")),
    )(pre, w)


def _conv_silu_bwd(pre, w, dact):
    s, c = pre.shape
    width = w.shape[0]
    tr, tc = _conv_tiles(s, c)

    def body(x_ref, w_ref, d_ref, dy_ref, dw_ref, tail):
        @pl.when(pl.program_id(1) == 0)
        def _():
            tail[...] = jnp.zeros_like(tail)
            dw_ref[...] = jnp.zeros_like(dw_ref)

        cur = x_ref[...]
        edge = tail[...]
        y = _conv_taps(cur, edge, w_ref[...], False)
        sg = _sigmoid(y)
        dy = d_ref[...] * sg * (1.0 + y * (1.0 - sg))
        dy_ref[...] = dy
        for k in range(width):
            xs = _shift_rows(cur, edge, width - 1 - k, False)
            dw_ref[k:k + 1, :] += jnp.sum(dy * xs, axis=0, keepdims=True)
        tail[...] = cur[tr - SUB:]

    blk = pl.BlockSpec((tr, tc), lambda j, i: (i, j))
    wblk = pl.BlockSpec((width, tc), lambda j, i: (0, j))
    return pl.pallas_call(
        body, name="conv_silu_bwd", grid=(c // tc, s // tr),
        in_specs=[blk, wblk, blk], out_specs=[blk, wblk],
        out_shape=[jax.ShapeDtypeStruct((s, c), F32), jax.ShapeDtypeStruct((width, c), F32)],
        scratch_shapes=[pltpu.VMEM((SUB, tc), F32)],
        compiler_params=_params(("parallel", "arbitrary")),
    )(pre, w, dact)


def _conv_transpose(dy, w):
    s, c = dy.shape
    width = w.shape[0]
    tr, tc = _conv_tiles(s, c)
    nr = s // tr

    def body(d_ref, w_ref, o_ref, head):
        @pl.when(pl.program_id(1) == 0)
        def _():
            head[...] = jnp.zeros_like(head)

        cur = d_ref[...]
        o_ref[...] = _conv_taps(cur, head[...], w_ref[...], True)
        head[...] = cur[:SUB]

    blk = pl.BlockSpec((tr, tc), lambda j, i: (nr - 1 - i, j))
    return pl.pallas_call(
        body, name="conv_transpose", grid=(c // tc, nr),
        in_specs=[blk, pl.BlockSpec((width, tc), lambda j, i: (0, j))], out_specs=blk,
        out_shape=jax.ShapeDtypeStruct((s, c), F32),
        scratch_shapes=[pltpu.VMEM((SUB, tc), F32)],
        compiler_params=_params(("parallel", "arbitrary")),
    )(dy, w)


def _ffn_act_fwd(pg, pv, wg, wv, bg, bv):
    s, c = pg.shape
    width = wg.shape[0]
    tr, tc = _conv_tiles(s, c)

    def body(g_ref, v_ref, wg_ref, wv_ref, bg_ref, bv_ref, o_ref, tg, tv):
        @pl.when(pl.program_id(1) == 0)
        def _():
            tg[...] = jnp.zeros_like(tg)
            tv[...] = jnp.zeros_like(tv)

        cg, cv = g_ref[...], v_ref[...]
        yg = _conv_taps(cg, tg[...], wg_ref[...], False) + bg_ref[...]
        yv = _conv_taps(cv, tv[...], wv_ref[...], False) + bv_ref[...]
        o_ref[...] = (yg * _sigmoid(yg) * yv).astype(BF16)
        tg[...] = cg[tr - SUB:]
        tv[...] = cv[tr - SUB:]

    blk = pl.BlockSpec((tr, tc), lambda j, i: (i, j))
    wblk = pl.BlockSpec((width, tc), lambda j, i: (0, j))
    bblk = pl.BlockSpec((1, tc), lambda j, i: (0, j))
    return pl.pallas_call(
        body, name="ffn_act_fwd", grid=(c // tc, s // tr),
        in_specs=[blk, blk, wblk, wblk, bblk, bblk], out_specs=blk,
        out_shape=jax.ShapeDtypeStruct((s, c), BF16),
        scratch_shapes=[pltpu.VMEM((SUB, tc), F32), pltpu.VMEM((SUB, tc), F32)],
        compiler_params=_params(("parallel", "arbitrary")),
    )(pg, pv, wg, wv, bg, bv)


def _ffn_act_bwd(pg, pv, wg, wv, bg, bv, dact):
    s, c = pg.shape
    width = wg.shape[0]
    tr, tc = _conv_tiles(s, c)

    def body(g_ref, v_ref, wg_ref, wv_ref, bg_ref, bv_ref, d_ref,
             dyg_ref, dyv_ref, dwg_ref, dwv_ref, dbg_ref, dbv_ref, tg, tv):
        @pl.when(pl.program_id(1) == 0)
        def _():
            for r in (tg, tv, dwg_ref, dwv_ref, dbg_ref, dbv_ref):
                r[...] = jnp.zeros_like(r)

        cg, cv = g_ref[...], v_ref[...]
        eg, ev = tg[...], tv[...]
        yg = _conv_taps(cg, eg, wg_ref[...], False) + bg_ref[...]
        yv = _conv_taps(cv, ev, wv_ref[...], False) + bv_ref[...]
        sg = _sigmoid(yg)
        da = d_ref[...]
        dyv = da * yg * sg
        dyg = da * yv * sg * (1.0 + yg * (1.0 - sg))
        dyg_ref[...] = dyg
        dyv_ref[...] = dyv
        dbg_ref[...] += jnp.sum(dyg, axis=0, keepdims=True)
        dbv_ref[...] += jnp.sum(dyv, axis=0, keepdims=True)
        for k in range(width):
            dwg_ref[k:k + 1, :] += jnp.sum(dyg * _shift_rows(cg, eg, width - 1 - k, False), axis=0, keepdims=True)
            dwv_ref[k:k + 1, :] += jnp.sum(dyv * _shift_rows(cv, ev, width - 1 - k, False), axis=0, keepdims=True)
        tg[...] = cg[tr - SUB:]
        tv[...] = cv[tr - SUB:]

    blk = pl.BlockSpec((tr, tc), lambda j, i: (i, j))
    wblk = pl.BlockSpec((width, tc), lambda j, i: (0, j))
    bblk = pl.BlockSpec((1, tc), lambda j, i: (0, j))
    big = jax.ShapeDtypeStruct((s, c), F32)
    wsh = jax.ShapeDtypeStruct((width, c), F32)
    bsh = jax.ShapeDtypeStruct((1, c), F32)
    return pl.pallas_call(
        body, name="ffn_act_bwd", grid=(c // tc, s // tr),
        in_specs=[blk, blk, wblk, wblk, bblk, bblk, blk],
        out_specs=[blk, blk, wblk, wblk, bblk, bblk],
        out_shape=[big, big, wsh, wsh, bsh, bsh],
        scratch_shapes=[pltpu.VMEM((SUB, tc), F32), pltpu.VMEM((SUB, tc), F32)],
        compiler_params=_params(("parallel", "arbitrary")),
    )(pg, pv, wg, wv, bg, bv, dact)


def _tri_masks():
    row = lax.broadcasted_iota(jnp.int32, (CHUNK, CHUNK), 0)
    col = lax.broadcasted_iota(jnp.int32, (CHUNK, CHUNK), 1)
    return row, col


def _tri_inv(m, row, col):
    eye = (row == col).astype(F32)
    same_blk = (row >> 4) == (col >> 4)
    md = jnp.where(same_blk, m, 0.0)
    off = m - md
    x = eye - md
    p = _bdot(md, md, NN)
    for _ in range(2):
        r = _bdot(jnp.concatenate([x, p], axis=0), p, NN)
        x = x + r[:CHUNK]
        p = r[CHUNK:]
    x = x + _bdot(x, p, NN)
    p = _bdot(x, off, NN)
    y = eye - p
    y = y + _bdot(y, _bdot(p, p, NN), NN)
    return _bdot(y, x, NN)


def _gdn_gates(ba, alog, dtb, row, col):
    sig = _sigmoid(ba)
    neg_a = -jnp.exp(alog)
    g = neg_a * _softplus(ba + dtb)
    lower = (row >= col).astype(F32)
    gcum = _dot(lower, g, NN, HIGHEST)
    return sig, neg_a, g, gcum


def _gdn_head_common(q_raw, k_raw, v, beta, gc, gr, row, col):
    causal = row >= col
    strict = row > col
    rq = lax.rsqrt(jnp.sum(q_raw * q_raw, axis=-1, keepdims=True) + EPS)
    rk = lax.rsqrt(jnp.sum(k_raw * k_raw, axis=-1, keepdims=True) + EPS)
    q = q_raw * (rq * (A_HEAD ** -0.5))
    k = k_raw * rk
    decay = jnp.where(causal, jnp.exp(jnp.where(causal, gc - gr, 0.0)), 0.0)
    eg = jnp.exp(gc)
    gl = gc[CHUNK - 1:CHUNK, :]
    ekl = jnp.exp(gl - gc)
    dec = jnp.exp(gl)
    kb = k * beta
    kbq = jnp.concatenate([kb, q], axis=0)
    both = _bdot(kbq, k, NT)
    kk, qk = both[:CHUNK], both[CHUNK:]
    a = jnp.where(causal, qk * decay, 0.0)
    return dict(rq=rq, rk=rk, q=q, k=k, decay=decay, eg=eg, ekl=ekl, dec=dec, kb=kb, kbq=kbq, kk=kk, qk=qk, a=a,
                vb=v * beta, kbg=kb * eg, qd=q * eg, ke=k * ekl, causal=causal, strict=strict)


def _gdn_fwd(qkv, ba, z, alog, dtb, wn):
    s = qkv.shape[0]
    nc = s // CHUNK

    def body(qkv_ref, ba_ref, z_ref, alog_ref, dtb_ref, wn_ref, y_ref, o_ref, st_ref, t_ref, w_ref, vn_ref, state):
        @pl.when(pl.program_id(0) == 0)
        def _():
            state[...] = jnp.zeros_like(state)

        row, col = _tri_masks()
        sig, _, _, gcum = _gdn_gates(ba_ref[...], alog_ref[...], dtb_ref[...], row, col)
        gt = gcum.T
        for h in range(A_HEADS):
            ln = slice(h * A_HEAD, (h + 1) * A_HEAD)
            s0 = state[h]
            f = _gdn_head_common(qkv_ref[:, ln], qkv_ref[:, A_QK + h * A_HEAD:A_QK + (h + 1) * A_HEAD],
                                 qkv_ref[:, 2 * A_QK + h * A_HEAD:2 * A_QK + (h + 1) * A_HEAD],
                                 sig[:, h:h + 1], gcum[:, 8 + h:9 + h], gt[8 + h:9 + h, :], row, col)
            t = _tri_inv(jnp.where(f["strict"], f["kk"] * f["decay"], 0.0), row, col).astype(BF16)
            uw = _bdot(t, jnp.concatenate([f["vb"], f["kbg"]], axis=1), NN)
            u, w = uw[:, :A_HEAD], uw[:, A_HEAD:].astype(BF16)
            ws = _bdot(jnp.concatenate([w, f["qd"].astype(BF16)], axis=0), s0, NN)
            vnew = (u - ws[:CHUNK]).astype(BF16)
            o = ws[CHUNK:] + _bdot(f["a"], vnew, NN)
            st_ref[0, h] = s0
            t_ref[0, h] = t
            w_ref[:, ln] = w
            vn_ref[:, ln] = vnew
            state[h] = s0 * f["dec"] + _bdot(f["ke"], vnew, TN)
            o_ref[:, ln] = o
            r = lax.rsqrt(jnp.mean(o * o, axis=-1, keepdims=True) + EPS)
            zz = z_ref[:, ln]
            y_ref[:, ln] = (o * r * wn_ref[...] * zz * _sigmoid(zz)).astype(BF16)

    vec = pl.BlockSpec((1, LANE), lambda n: (0, 0))
    wide = pl.BlockSpec((CHUNK, A_QK), lambda n: (n, 0))
    return pl.pallas_call(
        body, name="gdn_fwd", grid=(nc,),
        in_specs=[pl.BlockSpec((CHUNK, A_CONV_WIDTH), lambda n: (n, 0)),
                  pl.BlockSpec((CHUNK, LANE), lambda n: (n, 0)), wide, vec, vec, vec],
        out_specs=[wide, wide, pl.BlockSpec((1, A_HEADS, A_HEAD, A_HEAD), lambda n: (n, 0, 0, 0)),
                   pl.BlockSpec((1, A_HEADS, CHUNK, CHUNK), lambda n: (n, 0, 0, 0)), wide, wide],
        out_shape=[jax.ShapeDtypeStruct((s, A_QK), BF16), jax.ShapeDtypeStruct((s, A_QK), F32),
                   jax.ShapeDtypeStruct((nc, A_HEADS, A_HEAD, A_HEAD), F32),
                   jax.ShapeDtypeStruct((nc, A_HEADS, CHUNK, CHUNK), BF16),
                   jax.ShapeDtypeStruct((s, A_QK), BF16), jax.ShapeDtypeStruct((s, A_QK), BF16)],
        scratch_shapes=[pltpu.VMEM((A_HEADS, A_HEAD, A_HEAD), F32)],
        compiler_params=_params(("arbitrary",)),
    )(qkv, ba, z, alog, dtb, wn)


def _gdn_bwd(qkv, ba, z, o_raw, dy, states, t_all, w_all, vn_all, alog, dtb, wn):
    s = qkv.shape[0]
    nc = s // CHUNK

    def body(qkv_ref, ba_ref, z_ref, o_ref, dy_ref, st_ref, t_ref, w_ref, vn_ref, alog_ref, dtb_ref, wn_ref,
             dqkv_ref, dba_ref, dz_ref, dalog_ref, ddtb_ref, dwn_ref, dstate):
        @pl.when(pl.program_id(0) == 0)
        def _():
            for r in (dstate, dalog_ref, ddtb_ref, dwn_ref):
                r[...] = jnp.zeros_like(r)

        row, col = _tri_masks()
        bat = ba_ref[...]
        sig, neg_a, g, gcum = _gdn_gates(bat, alog_ref[...], dtb_ref[...], row, col)
        gt = gcum.T
        lane = lax.broadcasted_iota(jnp.int32, (CHUNK, LANE), 1)
        ones = jnp.ones((CHUNK, LANE), F32)
        last_row = lax.broadcasted_iota(jnp.int32, (CHUNK, 1), 0) == CHUNK - 1
        wnv = wn_ref[...]
        dgc_tile = jnp.zeros((CHUNK, LANE), F32)
        dbeta_tile = jnp.zeros((CHUNK, LANE), F32)
        dwn_acc = jnp.zeros((1, LANE), F32)
        ddecays = []
        for h in range(A_HEADS):
            ln = slice(h * A_HEAD, (h + 1) * A_HEAD)
            lk = slice(A_QK + h * A_HEAD, A_QK + (h + 1) * A_HEAD)
            lv = slice(2 * A_QK + h * A_HEAD, 2 * A_QK + (h + 1) * A_HEAD)
            q_raw, k_raw, v = qkv_ref[:, ln], qkv_ref[:, lk], qkv_ref[:, lv]
            beta = sig[:, h:h + 1]
            s0 = st_ref[0, h]
            ds1 = dstate[h]
            f = _gdn_head_common(q_raw, k_raw, v, beta, gcum[:, 8 + h:9 + h], gt[8 + h:9 + h, :], row, col)
            t, w, vnew = t_ref[0, h], w_ref[:, ln], vn_ref[:, ln]
            o = o_ref[:, ln]
            zz = z_ref[:, ln]
            dyv = dy_ref[:, ln]
            r = lax.rsqrt(jnp.mean(o * o, axis=-1, keepdims=True) + EPS)
            sz = _sigmoid(zz)
            silu = zz * sz
            dz_ref[:, ln] = dyv * o * r * wnv * sz * (1.0 + zz * (1.0 - sz))
            dwn_acc = dwn_acc + jnp.sum(dyv * silu * o * r, axis=0, keepdims=True)
            tt = dyv * silu * wnv
            do = r * tt - o * (r * r * r) * jnp.mean(tt * o, axis=-1, keepdims=True)
            do_b = do.astype(BF16)
            dvnew = _bdot(f["a"], do_b, TN) + _bdot(f["ke"], ds1, NN)
            do_dv = jnp.concatenate([do_b, dvnew.astype(BF16)], axis=0)
            both = _bdot(do_dv, s0, NT)
            dqd, dw = both[:CHUNK], -both[CHUNK:]
            da = jnp.where(f["causal"], _bdot(do_b, vnew, NT), 0.0)
            dke = _bdot(vnew, ds1, NT)
            ddec = jnp.sum(jnp.sum(s0 * ds1, axis=1, keepdims=True), axis=0, keepdims=True)
            qd_w = jnp.concatenate([f["qd"].astype(BF16), -w], axis=0)
            dstate[h] = _bdot(qd_w, do_dv, TN) + f["dec"] * ds1
            dd = jnp.concatenate([dvnew, dw], axis=1).astype(BF16)
            tdd = _bdot(t, dd, TN)
            dvb, dkbg = tdd[:, :A_HEAD], tdd[:, A_HEAD:]
            dt = _bdot(dd, jnp.concatenate([f["vb"], f["kbg"]], axis=1), NT)
            dm = -_bdot(_bdot(t, dt, TN), t, NT)
            dm = jnp.where(f["strict"], dm, 0.0)
            dkk = dm * f["decay"]
            dqk = da * f["decay"]
            ddecay = (dm * f["kk"] + da * f["qk"]) * f["decay"]
            ddecays.append(ddecay)
            k, q, kb = f["k"], f["q"], f["kb"]
            dboth = jnp.concatenate([dkk, dqk], axis=0).astype(BF16)
            r2 = _bdot(dboth, k, NN)
            dkb = r2[:CHUNK] + dkbg * f["eg"]
            dq = r2[CHUNK:] + dqd * f["eg"]
            dk = _bdot(dboth, f["kbq"], TN) + dke * f["ekl"] + dkb * beta
            dke_ke = jnp.sum(dke * f["ke"], axis=-1, keepdims=True)
            dgc = (jnp.sum(ddecay, axis=-1, keepdims=True)
                   + jnp.sum(dqd * f["qd"], axis=-1, keepdims=True) - dke_ke
                   + jnp.sum(dkbg * f["kbg"], axis=-1, keepdims=True))
            dgl = jnp.sum(dke_ke, axis=0, keepdims=True) + ddec * f["dec"]
            dgc = dgc + jnp.where(last_row, dgl, 0.0)
            dbeta = jnp.sum(dkb * k, axis=-1, keepdims=True) + jnp.sum(dvb * v, axis=-1, keepdims=True)
            dgc_tile = dgc_tile + jnp.where(lane == 8 + h, dgc, 0.0)
            dbeta_tile = dbeta_tile + jnp.where(lane == h, dbeta, 0.0)
            dqn = dq * (A_HEAD ** -0.5)
            rq, rk = f["rq"], f["rk"]
            dqkv_ref[:, ln] = rq * dqn - q_raw * (rq * rq * rq) * jnp.sum(dqn * q_raw, axis=-1, keepdims=True)
            dqkv_ref[:, lk] = rk * dk - k_raw * (rk * rk * rk) * jnp.sum(dk * k_raw, axis=-1, keepdims=True)
            dqkv_ref[:, lv] = dvb * beta
        col_sums = _dot(jnp.concatenate(ddecays, axis=1), ones, TN, HIGHEST)
        for h in range(A_HEADS):
            dgc_tile = dgc_tile - jnp.where(lane == 8 + h, col_sums[h * CHUNK:(h + 1) * CHUNK, :1], 0.0)
        upper = (row <= col).astype(F32)
        dg = _dot(upper, dgc_tile, NN, HIGHEST)
        da_raw = dg * neg_a * _sigmoid(bat + dtb_ref[...])
        dba_ref[...] = jnp.where(lane < 8, dbeta_tile * sig * (1.0 - sig), jnp.where(lane < 16, da_raw, 0.0))
        dwn_ref[...] += dwn_acc
        ddtb_ref[...] += jnp.sum(da_raw, axis=0, keepdims=True)
        dalog_ref[...] += jnp.sum(dg * g, axis=0, keepdims=True)

    rev = lambda n: (nc - 1 - n, 0)
    vec = pl.BlockSpec((1, LANE), lambda n: (0, 0))
    wide = pl.BlockSpec((CHUNK, A_QK), rev)
    qkv_blk = pl.BlockSpec((CHUNK, A_CONV_WIDTH), rev)
    ba_blk = pl.BlockSpec((CHUNK, LANE), rev)
    vsh = jax.ShapeDtypeStruct((1, LANE), F32)
    return pl.pallas_call(
        body, name="gdn_bwd", grid=(nc,),
        in_specs=[qkv_blk, ba_blk, wide, wide, wide,
                  pl.BlockSpec((1, A_HEADS, A_HEAD, A_HEAD), lambda n: (nc - 1 - n, 0, 0, 0)),
                  pl.BlockSpec((1, A_HEADS, CHUNK, CHUNK), lambda n: (nc - 1 - n, 0, 0, 0)), wide, wide,
                  vec, vec, vec],
        out_specs=[qkv_blk, ba_blk, wide, vec, vec, vec],
        out_shape=[jax.ShapeDtypeStruct((s, A_CONV_WIDTH), F32), jax.ShapeDtypeStruct((s, LANE), F32),
                   jax.ShapeDtypeStruct((s, A_QK), F32), vsh, vsh, vsh],
        scratch_shapes=[pltpu.VMEM((A_HEADS, A_HEAD, A_HEAD), F32)],
        compiler_params=_params(("arbitrary",)),
    )(qkv, ba, z, o_raw, dy, states, t_all, w_all, vn_all, alog, dtb, wn)


REL_RING = 1024
QBLK_BITS = 8


def _rel_ring_onehot():
    m = lax.broadcasted_iota(jnp.int32, (REL_RING, REL_PAD), 0)
    t = lax.broadcasted_iota(jnp.int32, (REL_RING, REL_PAD), 1)
    u = jnp.where(m < KBLK, m, m - REL_RING)
    idx = jnp.clip(LEFT - u, -REL_CLIP, REL_CLIP) + REL_CLIP
    return (t == idx).astype(F32)


def _relbias_ring(table, transpose):
    n_in, n_out = (REL_RING, REL_PAD) if transpose else (REL_PAD, REL_RING)

    def body(t_ref, o_ref):
        o_ref[...] = _dot(t_ref[...], _rel_ring_onehot(), NN if transpose else NT, HIGHEST)

    return pl.pallas_call(
        body, name="relbias_ring_bwd" if transpose else "relbias_ring",
        out_shape=jax.ShapeDtypeStruct((B_HEADS, n_out), F32),
        in_specs=[VMEM_FULL], out_specs=VMEM_FULL,
        compiler_params=_params(),
    )(table)


def _row_bit(shape, bit):
    return ((lax.broadcasted_iota(jnp.int32, shape, 0) >> bit) & 1) == 1


def _relbias_expand(ring):
    def body(r_ref, o_ref):
        b = jnp.broadcast_to(r_ref[0], (QBLK, REL_RING))
        for bit in range(QBLK_BITS):
            b = jnp.where(_row_bit(b.shape, bit), pltpu.roll(b, 1 << bit, 1), b)
        j = lax.broadcasted_iota(jnp.int32, (QBLK, KBLK), 1)
        r = lax.broadcasted_iota(jnp.int32, (QBLK, KBLK), 0)
        lo = (r >> 6) << 6
        o_ref[0] = jnp.where((j >= lo) & (j < lo + LEFT + CHUNK), b[:, :KBLK], NEG_INF)

    return pl.pallas_call(
        body, name="relbias_expand", grid=(B_HEADS,),
        in_specs=[pl.BlockSpec((1, 1, REL_RING), lambda h: (h, 0, 0))],
        out_specs=pl.BlockSpec((1, QBLK, KBLK), lambda h: (h, 0, 0)),
        out_shape=jax.ShapeDtypeStruct((B_HEADS, QBLK, KBLK), F32),
        compiler_params=_params(("parallel",)),
    )(ring)


def _relbias_reduce(ds):
    def body(d_ref, o_ref):
        d = jnp.concatenate([d_ref[0], jnp.zeros((QBLK, REL_RING - KBLK), F32)], axis=1)
        for bit in range(QBLK_BITS):
            d = jnp.where(_row_bit(d.shape, bit), pltpu.roll(d, REL_RING - (1 << bit), 1), d)
        o_ref[0] = jnp.sum(d, axis=0, keepdims=True)

    return pl.pallas_call(
        body, name="relbias_reduce", grid=(B_HEADS,),
        in_specs=[pl.BlockSpec((1, QBLK, KBLK), lambda h: (h, 0, 0))],
        out_specs=pl.BlockSpec((1, 1, REL_RING), lambda h: (h, 0, 0)),
        out_shape=jax.ShapeDtypeStruct((B_HEADS, 1, REL_RING), F32),
        compiler_params=_params(("parallel",)),
    )(ds)


def _attn_probs(q_ref, kb, b_ref, hh, q0):
    hl = slice(hh * B_HEAD, (hh + 1) * B_HEAD)
    qh = q_ref[:, hl]
    kh = kb[:, hl]
    jpos = lax.broadcasted_iota(jnp.int32, (QBLK, KBLK), 1)
    sc = _bdot(qh, kh, NT) * (B_HEAD ** -0.5) + b_ref[hh]
    sc = jnp.where(jpos + q0 >= LEFT, sc, NEG_INF)
    mx = jnp.max(sc, axis=-1, keepdims=True)
    p = jnp.exp(sc - mx)
    return p / jnp.sum(p, axis=-1, keepdims=True), qh, kh


def _attn_fwd(q, kpad, vpad, bias):
    s = q.shape[0]

    def body(q_ref, k_ref, v_ref, b_ref, o_ref):
        q0 = pl.multiple_of(pl.program_id(1) * QBLK, QBLK)
        kb = k_ref[pl.ds(q0, KBLK), :]
        vb = v_ref[pl.ds(q0, KBLK), :]
        outs = []
        for hh in range(2):
            p, _, _ = _attn_probs(q_ref, kb, b_ref, hh, q0)
            outs.append(_bdot(p, vb[:, hh * B_HEAD:(hh + 1) * B_HEAD], NN))
        o_ref[...] = jnp.concatenate(outs, axis=1).astype(BF16)

    qblk = pl.BlockSpec((QBLK, LANE), lambda g, m: (m, g))
    kblk = pl.BlockSpec((LEFT + s, LANE), lambda g, m: (0, g))
    return pl.pallas_call(
        body, name="attn_fwd", grid=(B_HEADS // 2, s // QBLK),
        in_specs=[qblk, kblk, kblk, pl.BlockSpec((2, QBLK, KBLK), lambda g, m: (g, 0, 0))],
        out_specs=qblk,
        out_shape=jax.ShapeDtypeStruct((s, D_MODEL), BF16),
        compiler_params=_params(("parallel", "arbitrary")),
    )(q, kpad, vpad, bias)


def _attn_bwd(q, kpad, vpad, bias, do):
    s = q.shape[0]

    def body(q_ref, k_ref, v_ref, b_ref, do_ref, dq_ref, dk_ref, dv_ref, db_ref):
        @pl.when(pl.program_id(1) == 0)
        def _():
            for r in (dk_ref, dv_ref, db_ref):
                r[...] = jnp.zeros_like(r)

        q0 = pl.multiple_of(pl.program_id(1) * QBLK, QBLK)
        kb = k_ref[pl.ds(q0, KBLK), :]
        vb = v_ref[pl.ds(q0, KBLK), :]
        dqs, dks, dvs = [], [], []
        for hh in range(2):
            hl = slice(hh * B_HEAD, (hh + 1) * B_HEAD)
            p, qh, kh = _attn_probs(q_ref, kb, b_ref, hh, q0)
            doh = do_ref[:, hl]
            dp = _bdot(doh, vb[:, hl], NT)
            dsc = p * (dp - jnp.sum(p * dp, axis=-1, keepdims=True))
            db_ref[hh] += dsc
            dqs.append(_bdot(dsc, kh, NN) * (B_HEAD ** -0.5))
            dks.append(_bdot(dsc, qh, TN) * (B_HEAD ** -0.5))
            dvs.append(_bdot(p, doh, TN))
        dq_ref[...] = jnp.concatenate(dqs, axis=1)
        dk_ref[pl.ds(q0, KBLK), :] += jnp.concatenate(dks, axis=1)
        dv_ref[pl.ds(q0, KBLK), :] += jnp.concatenate(dvs, axis=1)

    qblk = pl.BlockSpec((QBLK, LANE), lambda g, m: (m, g))
    kblk = pl.BlockSpec((LEFT + s, LANE), lambda g, m: (0, g))
    bblk = pl.BlockSpec((2, QBLK, KBLK), lambda g, m: (g, 0, 0))
    return pl.pallas_call(
        body, name="attn_bwd", grid=(B_HEADS // 2, s // QBLK),
        in_specs=[qblk, kblk, kblk, bblk, qblk],
        out_specs=[qblk, kblk, kblk, bblk],
        out_shape=[jax.ShapeDtypeStruct((s, D_MODEL), F32), jax.ShapeDtypeStruct((LEFT + s, D_MODEL), F32),
                   jax.ShapeDtypeStruct((LEFT + s, D_MODEL), F32),
                   jax.ShapeDtypeStruct((B_HEADS, QBLK, KBLK), F32)],
        compiler_params=_params(("parallel", "arbitrary")),
    )(q, kpad, vpad, bias, do)


def _adamw(w, g, m, v):
    r, c = w.shape
    tr = r
    for cand in (512, 256, 128, 64, 32, 16, 8):
        if r % cand == 0 and cand * c * 4 <= 2 * 1024 * 1024:
            tr = cand
            break
    c1 = 1.0 / (1.0 - ADAM_B1 ** ADAM_STEP)
    c2 = 1.0 / (1.0 - ADAM_B2 ** ADAM_STEP)

    def body(w_ref, g_ref, m_ref, v_ref, d_ref, mo_ref, vo_ref):
        gv = g_ref[...]
        mn = ADAM_B1 * m_ref[...] + (1.0 - ADAM_B1) * gv
        vn = ADAM_B2 * v_ref[...] + (1.0 - ADAM_B2) * (gv * gv)
        mo_ref[...] = mn
        vo_ref[...] = vn
        d_ref[...] = -ADAM_LR * ((mn * c1) / (jnp.sqrt(vn * c2) + ADAM_EPS) + ADAM_WD * w_ref[...])

    blk = pl.BlockSpec((tr, c), lambda i: (i, 0))
    sh = jax.ShapeDtypeStruct((r, c), F32)
    return pl.pallas_call(
        body, name="adamw", grid=(r // tr,),
        in_specs=[blk] * 4, out_specs=[blk] * 3, out_shape=[sh] * 3,
        compiler_params=_params(("parallel",)),
    )(w, g, m, v)


def _row(v, width=None):
    v = v.reshape(1, -1)
    if width is not None and v.shape[1] < width:
        v = jnp.pad(v, ((0, 0), (0, width - v.shape[1])))
    return v


def _gate_row(v):
    return jnp.pad(v.reshape(1, A_HEADS), ((0, 0), (A_HEADS, LANE - 2 * A_HEADS)))


def _a_layer_fwd(h, p):
    xn = _rmsnorm_fwd(h, _row(p["norm"]))
    w_in = p["w_in"]
    w_qkv, w_z = w_in[:, :A_CONV_WIDTH], w_in[:, A_CONV_WIDTH:A_CONV_WIDTH + A_QK]
    w_ba = jnp.pad(w_in[:, A_CONV_WIDTH + A_QK:], ((0, 0), (0, LANE - 2 * A_HEADS)))
    pre = _mm(xn, w_qkv, name="a_qkv")
    z = _mm(xn, w_z, name="a_z")
    ba = _mm(xn, w_ba, name="a_ba")
    act = _conv_silu_fwd(pre, p["conv"])
    alog, dtb, wn = _gate_row(p["A_log"]), _gate_row(p["dt_bias"]), _row(p["out_norm"])
    y, o_raw, states, t_all, w_all, vn_all = _gdn_fwd(act, ba, z, alog, dtb, wn)
    h2 = _mm(y, p["w_out"], res=h, name="a_out")
    saved = dict(h=h, xn=xn, pre=pre, z=z, ba=ba, act=act, o_raw=o_raw, y=y, states=states,
                 t_all=t_all, w_all=w_all, vn_all=vn_all,
                 w_qkv=w_qkv, w_z=w_z, w_ba=w_ba, alog=alog, dtb=dtb, wn=wn)
    return h2, saved


def _a_layer_bwd(dh2, p, sv):
    g = {}
    g["w_out"] = _mm(sv["y"], dh2, "tn", name="a_out_dw")
    dy = _mm(dh2, p["w_out"], "nt", name="a_out_dx")
    dact, dba, dz, dalog, ddtb, dwn = _gdn_bwd(sv["act"], sv["ba"], sv["z"], sv["o_raw"], dy, sv["states"],
                                               sv["t_all"], sv["w_all"], sv["vn_all"],
                                               sv["alog"], sv["dtb"], sv["wn"])
    dyc, g["conv"] = _conv_silu_bwd(sv["pre"], p["conv"], dact)
    dpre = _conv_transpose(dyc, p["conv"])
    xn = sv["xn"]
    g["w_in"] = jnp.concatenate([_mm(xn, dpre, "tn", name="a_qkv_dw"), _mm(xn, dz, "tn", name="a_z_dw"),
                                 _mm(xn, dba, "tn", name="a_ba_dw")[:, :2 * A_HEADS]], axis=1)
    dxn = _mm(dpre, sv["w_qkv"], "nt", name="a_qkv_dx")
    dxn = _mm(dz, sv["w_z"], "nt", res=dxn, name="a_z_dx")
    dxn = _mm(dba, sv["w_ba"], "nt", res=dxn, name="a_ba_dx")
    dh, dnorm = _rmsnorm_bwd(sv["h"], _row(p["norm"]), dxn, dh2)
    g["norm"] = dnorm[0]
    g["A_log"] = dalog[0, A_HEADS:2 * A_HEADS]
    g["dt_bias"] = ddtb[0, A_HEADS:2 * A_HEADS]
    g["out_norm"] = dwn[0]
    return dh, g


def _ffn_fwd(h, p):
    xn = _rmsnorm_fwd(h, _row(p["norm"]))
    w_g, w_v = p["w_up"][:, :FFN_DIM], p["w_up"][:, FFN_DIM:]
    cg, cv = p["conv"][:, :FFN_DIM], p["conv"][:, FFN_DIM:]
    bg, bv = _row(p["conv_b"][:FFN_DIM]), _row(p["conv_b"][FFN_DIM:])
    pg = _mm(xn, w_g, name="f_gate")
    pv = _mm(xn, w_v, name="f_val")
    act = _ffn_act_fwd(pg, pv, cg, cv, bg, bv)
    h2 = _mm(act, p["w_down"], res=h, name="f_down")
    return h2, dict(h=h, xn=xn, pg=pg, pv=pv, act=act, w_g=w_g, w_v=w_v, cg=cg, cv=cv, bg=bg, bv=bv)


def _ffn_bwd(dh2, p, sv):
    g = {}
    g["w_down"] = _mm(sv["act"], dh2, "tn", name="f_down_dw")
    dact = _mm(dh2, p["w_down"], "nt", name="f_down_dx")
    dyg, dyv, dcg, dcv, dbg, dbv = _ffn_act_bwd(sv["pg"], sv["pv"], sv["cg"], sv["cv"], sv["bg"], sv["bv"], dact)
    dpg = _conv_transpose(dyg, sv["cg"])
    dpv = _conv_transpose(dyv, sv["cv"])
    xn = sv["xn"]
    g["w_up"] = jnp.concatenate([_mm(xn, dpg, "tn", name="f_gate_dw"), _mm(xn, dpv, "tn", name="f_val_dw")], axis=1)
    g["conv"] = jnp.concatenate([dcg, dcv], axis=1)
    g["conv_b"] = jnp.concatenate([dbg[0], dbv[0]])
    dxn = _mm(dpg, sv["w_g"], "nt", name="f_gate_dx")
    dxn = _mm(dpv, sv["w_v"], "nt", res=dxn, name="f_val_dx")
    dh, dnorm = _rmsnorm_bwd(sv["h"], _row(p["norm"]), dxn, dh2)
    g["norm"] = dnorm[0]
    return dh, g


def _b_layer_fwd(h, p, kpad, vpad):
    xn = _rmsnorm_fwd(h, _row(p["norm"]))
    q = _mm(xn, p["w_q"], name="b_q")
    table = jnp.pad(p["rel_bias"], ((0, 0), (0, REL_PAD - p["rel_bias"].shape[1])))
    bias = _relbias_expand(_relbias_ring(table, False).reshape(B_HEADS, 1, REL_RING))
    o = _attn_fwd(q, kpad, vpad, bias)
    h2 = _mm(o, p["w_out"], res=h, name="b_out")
    return h2, dict(h=h, xn=xn, q=q, o=o, bias=bias)


def _b_layer_bwd(dh2, p, sv, kpad, vpad):
    g = {}
    g["w_out"] = _mm(sv["o"], dh2, "tn", name="b_out_dw")
    do = _mm(dh2, p["w_out"], "nt", name="b_out_dx")
    dq, dkp, dvp, dsc = _attn_bwd(sv["q"], kpad, vpad, sv["bias"], do)
    dring = _relbias_reduce(dsc).reshape(B_HEADS, REL_RING)
    g["rel_bias"] = _relbias_ring(dring, True)[:, :2 * REL_CLIP + 1]
    g["w_q"] = _mm(sv["xn"], dq, "tn", name="b_q_dw")
    dxn = _mm(dq, p["w_q"], "nt", name="b_q_dx")
    dh, dnorm = _rmsnorm_bwd(sv["h"], _row(p["norm"]), dxn, dh2)
    g["norm"] = dnorm[0]
    return dh, g, dkp, dvp


def _local_step(x, tgt, w):
    n_a = w["a_norm"].shape[0]
    n_b = w["b_norm"].shape[0]
    depth = n_a + n_b

    def a_params(i):
        return dict(norm=w["a_norm"][i], w_in=w["a_w_in"][i], conv=w["a_conv"][i], A_log=w["a_A_log"][i],
                    dt_bias=w["a_dt_bias"][i], out_norm=w["a_out_norm"][i], w_out=w["a_w_out"][i])

    def b_params(j):
        return dict(norm=w["b_norm"][j], w_q=w["b_w_q"][j], rel_bias=w["b_rel_bias"][j], w_out=w["b_w_out"][j])

    def f_params(l):
        return dict(norm=w["f_norm"][l], w_up=w["f_w_up"][l], conv=w["f_conv"][l], conv_b=w["f_conv_b"][l],
                    w_down=w["f_w_down"][l])

    h = x
    saved = []
    kv_saved = None
    kpad = vpad = None
    for layer in range(depth):
        if layer < n_a:
            h, sm = _a_layer_fwd(h, a_params(layer))
        else:
            if layer == n_a:
                xn_kv = _rmsnorm_fwd(h, _row(w["kv_norm"]))
                kv = _mm(xn_kv, w["w_kv"], out_dtype=BF16, name="kv")
                kpad = jnp.pad(kv[:, :D_MODEL], ((LEFT, 0), (0, 0)))
                vpad = jnp.pad(kv[:, D_MODEL:], ((LEFT, 0), (0, 0)))
                kv_saved = dict(h=h, xn=xn_kv)
            h, sm = _b_layer_fwd(h, b_params(layer - n_a), kpad, vpad)
        h, sf = _ffn_fwd(h, f_params(layer))
        saved.append((sm, sf))

    loss, dh, dfinal = _final_loss(h, _row(w["final_norm"]), tgt)

    ga = [None] * n_a
    gb = [None] * n_b
    gf = [None] * depth
    dk_tot = dv_tot = None
    g_kv = g_kvn = None
    for layer in reversed(range(depth)):
        sm, sf = saved[layer]
        dh, gf[layer] = _ffn_bwd(dh, f_params(layer), sf)
        if layer >= n_a:
            dh, gb[layer - n_a], dkp, dvp = _b_layer_bwd(dh, b_params(layer - n_a), sm, kpad, vpad)
            dk_tot = dkp if dk_tot is None else dk_tot + dkp
            dv_tot = dvp if dv_tot is None else dv_tot + dvp
            if layer == n_a:
                dkv = jnp.concatenate([dk_tot[LEFT:], dv_tot[LEFT:]], axis=1)
                g_kv = _mm(kv_saved["xn"], dkv, "tn", name="kv_dw")
                dxn = _mm(dkv, w["w_kv"], "nt", name="kv_dx")
                dh, g_kvn = _rmsnorm_bwd(kv_saved["h"], _row(w["kv_norm"]), dxn, dh)
        else:
            dh, ga[layer] = _a_layer_bwd(dh, a_params(layer), sm)

    def stack(gs, key):
        return jnp.stack([g[key] for g in gs])

    grads = dict(
        a_norm=stack(ga, "norm"), a_w_in=stack(ga, "w_in"), a_conv=stack(ga, "conv"), a_A_log=stack(ga, "A_log"),
        a_dt_bias=stack(ga, "dt_bias"), a_out_norm=stack(ga, "out_norm"), a_w_out=stack(ga, "w_out"),
        kv_norm=g_kvn[0], w_kv=g_kv,
        b_norm=stack(gb, "norm"), b_w_q=stack(gb, "w_q"), b_rel_bias=stack(gb, "rel_bias"), b_w_out=stack(gb, "w_out"),
        f_norm=stack(gf, "norm"), f_w_up=stack(gf, "w_up"), f_conv=stack(gf, "conv"), f_conv_b=stack(gf, "conv_b"),
        f_w_down=stack(gf, "w_down"), final_norm=dfinal[0])
    return loss, dh, grads


HBM_SPEC = pl.BlockSpec(memory_space=pl.ANY)
VMEM_SPEC = pl.BlockSpec(memory_space=pltpu.VMEM)
PACK_COLS = 1024
PACK_ROW_QUANTUM = 1280


def _place():
    x, y, c = lax.axis_index("x"), lax.axis_index("y"), lax.axis_index("c")
    chips = [(1 - x, y), (x, 1 - y), (1 - x, 1 - y)]
    return x, y, c, chips


def _remote(src, dst, send_sem, recv_sem, to):
    return pltpu.make_async_remote_copy(src_ref=src, dst_ref=dst, send_sem=send_sem, recv_sem=recv_sem,
                                        device_id=to, device_id_type=MESH)


def _allgather_weights(shard):
    r, cols = shard.shape
    rh = r // 2

    def body(x_ref, out_ref, send_sems, recv_sems):
        x, y, c, chips = _place()
        sibling = (x, y, 1 - c)

        def half(px, py, hc):
            return out_ref.at[2 * px + py, pl.ds(hc * rh, rh), :]

        first =[_remote(x_ref.at[pl.ds(c * rh, rh), :], half(x, y, c), send_sems.at[j], recv_sems.at[j], (*chip, c))
                 for j, chip in enumerate(chips)]
        for cp in first:
            cp.start()
        passed = [_remote(half(*chip, c), half(*chip, c), send_sems.at[3 + j], recv_sems.at[3 + j], sibling)
                  for j, chip in enumerate(chips)]
        for j, chip in enumerate(chips):
            _remote(half(*chip, c), half(*chip, c), send_sems.at[j], recv_sems.at[j], (*chip, c)).wait_recv()
            passed[j].start()
        for j, chip in enumerate(chips):
            _remote(half(*chip, 1 - c), half(*chip, 1 - c), send_sems.at[3 + j], recv_sems.at[3 + j],
                    sibling).wait_recv()
        for cp in first + passed:
            cp.wait_send()

    return pl.pallas_call(
        body, name="allgather_weights",
        out_shape=jax.ShapeDtypeStruct((4, r, cols), shard.dtype),
        in_specs=[HBM_SPEC], out_specs=HBM_SPEC,
        scratch_shapes=[pltpu.SemaphoreType.DMA((6,)), pltpu.SemaphoreType.DMA((6,))],
    )(shard)


def _pair_exchange(g):
    n, r, cols = g.shape
    rh = r // 2

    def body(g_ref, out_ref, send_sem, recv_sem):
        x, y, c, _ = _place()
        cp = _remote(g_ref.at[:, pl.ds((1 - c) * rh, rh), :], out_ref, send_sem, recv_sem, (x, y, 1 - c))
        cp.start()
        cp.wait()

    return pl.pallas_call(
        body, name="rs_pair_exchange",
        out_shape=jax.ShapeDtypeStruct((n, rh, cols), g.dtype),
        in_specs=[HBM_SPEC], out_specs=HBM_SPEC,
        scratch_shapes=[pltpu.SemaphoreType.DMA, pltpu.SemaphoreType.DMA],
    )(g)


def _pair_add(g, other, c_idx):
    n, r, cols = g.shape
    rh = r // 2
    tr = PACK_ROW_QUANTUM // 2
    nb = rh // tr

    def body(c_ref, a_ref, b_ref, o_ref, ob_ref):
        sm = a_ref[...] + b_ref[...]
        o_ref[...] = sm
        ob_ref[...] = sm.astype(BF16)

    out_blk = pl.BlockSpec((1, tr, cols), lambda s, i, c_ref: (s, i, 0))
    return pl.pallas_call(
        body, name="rs_pair_add",
        grid_spec=pltpu.PrefetchScalarGridSpec(
            num_scalar_prefetch=1, grid=(n, nb),
            in_specs=[pl.BlockSpec((1, tr, cols), lambda s, i, c_ref: (s, c_ref[0] * nb + i, 0)), out_blk],
            out_specs=[out_blk, out_blk]),
        out_shape=[jax.ShapeDtypeStruct((n, rh, cols), F32), jax.ShapeDtypeStruct((n, rh, cols), BF16)],
        compiler_params=_params(("parallel", "parallel")),
    )(c_idx, g, other)


def _chip_exchange(p):
    n, rh, cols = p.shape

    def body(p_ref, out_ref, send_sems, recv_sems):
        x, y, c, chips = _place()
        cps = [_remote(p_ref.at[2 * chip[0] + chip[1]], out_ref.at[j], send_sems.at[j], recv_sems.at[j], (*chip, c))
               for j, chip in enumerate(chips)]
        for cp in cps:
            cp.start()
        for cp in cps:
            cp.wait()

    return pl.pallas_call(
        body, name="rs_chip_exchange",
        out_shape=jax.ShapeDtypeStruct((3, rh, cols), p.dtype),
        in_specs=[HBM_SPEC], out_specs=HBM_SPEC,
        scratch_shapes=[pltpu.SemaphoreType.DMA((3,)), pltpu.SemaphoreType.DMA((3,))],
    )(p)


def _chip_add(p, recv, chip_idx):
    n, rh, cols = p.shape
    tr = PACK_ROW_QUANTUM // 2

    def body(s_ref, own_ref, r_ref, o_ref):
        o_ref[...] = ((own_ref[0] + r_ref[0].astype(F32)) + r_ref[1].astype(F32)) + r_ref[2].astype(F32)

    return pl.pallas_call(
        body, name="rs_chip_add",
        grid_spec=pltpu.PrefetchScalarGridSpec(
            num_scalar_prefetch=1, grid=(rh // tr,),
            in_specs=[pl.BlockSpec((1, tr, cols), lambda i, s_ref: (s_ref[0], i, 0)),
                      pl.BlockSpec((3, tr, cols), lambda i, s_ref: (0, i, 0))],
            out_specs=pl.BlockSpec((tr, cols), lambda i, s_ref: (i, 0))),
        out_shape=jax.ShapeDtypeStruct((rh, cols), p.dtype),
        compiler_params=_params(("parallel",)),
    )(chip_idx, p, recv)


def _pair_gather(f):
    rh, cols = f.shape

    def body(f_ref, out_ref, send_sem, recv_sem):
        x, y, c, _ = _place()
        cp = _remote(f_ref, out_ref, send_sem, recv_sem, (x, y, 1 - c))
        cp.start()
        cp.wait()

    return pl.pallas_call(
        body, name="rs_pair_gather",
        out_shape=jax.ShapeDtypeStruct((rh, cols), f.dtype),
        in_specs=[HBM_SPEC], out_specs=HBM_SPEC,
        scratch_shapes=[pltpu.SemaphoreType.DMA, pltpu.SemaphoreType.DMA],
    )(f)


def _allreduce_small(v):
    r, cols = v.shape

    def body(x_ref, out_ref, slots, send_sems, recv_sems):
        x, y, c, _ = _place()
        bits = [(bx, by, bc) for bx in (0, 1) for by in (0, 1) for bc in (0, 1)]

        def flip(b):
            return (1 - x if b[0] else x, 1 - y if b[1] else y, 1 - c if b[2] else c)

        slots[0] = x_ref[...]
        cps = [_remote(x_ref, slots.at[k], send_sems.at[k - 1], recv_sems.at[k - 1], flip(bits[k]))
               for k in range(1, 8)]
        for cp in cps:
            cp.start()
        for cp in cps:
            cp.wait()
        acc = None
        for b in bits:
            fx, fy, fc = flip(b)
            term = slots[4 * fx + 2 * fy + fc]
            acc = term if acc is None else acc + term
        out_ref[...] = acc

    return pl.pallas_call(
        body, name="allreduce_small",
        out_shape=jax.ShapeDtypeStruct((r, cols), v.dtype),
        in_specs=[VMEM_SPEC], out_specs=VMEM_SPEC,
        scratch_shapes=[pltpu.VMEM((8, r, cols), v.dtype), pltpu.SemaphoreType.DMA((7,)),
                        pltpu.SemaphoreType.DMA((7,))],
        compiler_params=pltpu.CompilerParams(vmem_limit_bytes=VMEM_LIMIT),
    )(v)


BIG = (("a_w_in", 2), ("a_w_out", 1), ("w_kv", 1), ("b_w_q", 1), ("b_w_out", 1), ("f_w_up", 2), ("f_w_down", 1))
SMALL = (("a_norm", 1), ("a_conv", 2), ("a_A_log", None), ("a_dt_bias", None), ("a_out_norm", None),
         ("kv_norm", None), ("b_norm", None), ("b_rel_bias", None), ("f_norm", None), ("f_conv", 2),
         ("f_conv_b", None), ("final_norm", None))
WEIGHT_ORDER = ("a_norm", "a_w_in", "a_conv", "a_A_log", "a_dt_bias", "a_out_norm", "a_w_out", "kv_norm", "w_kv",
                "b_norm", "b_w_q", "b_rel_bias", "b_w_out", "f_norm", "f_w_up", "f_conv", "f_conv_b", "f_w_down",
                "final_norm")


def _pad_rows(flat, cols, quantum):
    n = flat.shape[-1]
    rows = -(-n // (cols * quantum)) * quantum
    pad = [(0, 0)] * (flat.ndim - 1) + [(0, rows * cols - n)]
    return jnp.pad(flat, pad).reshape(flat.shape[:-1] + (rows, cols))


def _pack_shards(shards, dtype):
    return _pad_rows(jnp.concatenate([shards[n].astype(dtype).reshape(-1) for n, _ in BIG]), PACK_COLS,
                     PACK_ROW_QUANTUM)


def _unpack_gathered(gathered, shard_shapes):
    flat = gathered.reshape(4, -1)
    out, off = {}, 0
    for name, axis in BIG:
        shp = shard_shapes[name]
        n = math.prod(shp)
        seg = flat[:, off:off + n].reshape((4,) + shp)
        out[name] = jnp.concatenate([seg[s] for s in range(4)], axis=axis)
        off += n
    return out


def _pack_grads(grads):
    parts = []
    for name, axis in BIG:
        g = grads[name]
        parts.append(jnp.stack([piece.reshape(-1) for piece in jnp.split(g, 4, axis=axis)]))
    return _pad_rows(jnp.concatenate(parts, axis=1), PACK_COLS, PACK_ROW_QUANTUM)


def _unpack_reduced(reduced, shard_shapes):
    flat = reduced.reshape(-1)
    out, off = {}, 0
    for name, _ in BIG:
        shp = shard_shapes[name]
        n = math.prod(shp)
        out[name] = flat[off:off + n].reshape(shp)
        off += n
    return out


def _pack_small(values, names):
    return _pad_rows(jnp.concatenate([values[n].reshape(-1) for n in names]), LANE, SUB)


def _unpack_small(packed, shapes, names):
    flat = packed.reshape(-1)
    out, off = {}, 0
    for n in names:
        size = math.prod(shapes[n])
        out[n] = flat[off:off + size].reshape(shapes[n])
        off += size
    return out


def _adamw_nd(w, g, m, v):
    shp = w.shape
    two = (math.prod(shp[:-1]), shp[-1])
    d, mn, vn = _adamw(w.reshape(two), g.reshape(two), m.reshape(two), v.reshape(two))
    return d.reshape(shp), mn.reshape(shp), vn.reshape(shp)


def kernel(x, a_norm, a_w_in, a_conv, a_A_log, a_dt_bias, a_out_norm, a_w_out, kv_norm, w_kv, b_norm, b_w_q, b_rel_bias, b_w_out, f_norm, f_w_up, f_conv, f_conv_b, f_w_down, final_norm, loss_target, m_a_norm, m_a_w_in, m_a_conv, m_a_A_log, m_a_dt_bias, m_a_out_norm, m_a_w_out, m_kv_norm, m_w_kv, m_b_norm, m_b_w_q, m_b_rel_bias, m_b_w_out, m_f_norm, m_f_w_up, m_f_conv, m_f_conv_b, m_f_w_down, m_final_norm, v_a_norm, v_a_w_in, v_a_conv, v_a_A_log, v_a_dt_bias, v_a_out_norm, v_a_w_out, v_kv_norm, v_w_kv, v_b_norm, v_b_w_q, v_b_rel_bias, v_b_w_out, v_f_norm, v_f_w_up, v_f_conv, v_f_conv_b, v_f_w_down, v_final_norm):
    w = dict(a_norm=a_norm, a_w_in=a_w_in, a_conv=a_conv, a_A_log=a_A_log, a_dt_bias=a_dt_bias,
             a_out_norm=a_out_norm, a_w_out=a_w_out, kv_norm=kv_norm, w_kv=w_kv, b_norm=b_norm, b_w_q=b_w_q,
             b_rel_bias=b_rel_bias, b_w_out=b_w_out, f_norm=f_norm, f_w_up=f_w_up, f_conv=f_conv,
             f_conv_b=f_conv_b, f_w_down=f_w_down, final_norm=final_norm)
    m = dict(a_norm=m_a_norm, a_w_in=m_a_w_in, a_conv=m_a_conv, a_A_log=m_a_A_log, a_dt_bias=m_a_dt_bias,
             a_out_norm=m_a_out_norm, a_w_out=m_a_w_out, kv_norm=m_kv_norm, w_kv=m_w_kv, b_norm=m_b_norm,
             b_w_q=m_b_w_q, b_rel_bias=m_b_rel_bias, b_w_out=m_b_w_out, f_norm=m_f_norm, f_w_up=m_f_w_up,
             f_conv=m_f_conv, f_conv_b=m_f_conv_b, f_w_down=m_f_w_down, final_norm=m_final_norm)
    v = dict(a_norm=v_a_norm, a_w_in=v_a_w_in, a_conv=v_a_conv, a_A_log=v_a_A_log, a_dt_bias=v_a_dt_bias,
             a_out_norm=v_a_out_norm, a_w_out=v_a_w_out, kv_norm=v_kv_norm, w_kv=v_w_kv, b_norm=v_b_norm,
             b_w_q=v_b_w_q, b_rel_bias=v_b_rel_bias, b_w_out=v_b_w_out, f_norm=v_f_norm, f_w_up=v_f_w_up,
             f_conv=v_f_conv, f_conv_b=v_f_conv_b, f_w_down=v_f_w_down, final_norm=v_final_norm)
    xi, yi, ci = lax.axis_index("x"), lax.axis_index("y"), lax.axis_index("c")
    chip = 2 * xi + yi
    shard_shapes = {n: w[n].shape for n in WEIGHT_ORDER}

    my_shard = _pack_shards(w, BF16)
    gathered = lax.dynamic_update_slice(_allgather_weights(my_shard), my_shard[None], (chip, 0, 0))
    full = _unpack_gathered(gathered, shard_shapes)
    sharded_small = [n for n, axis in SMALL if axis is not None]
    placed = {}
    for n, axis in SMALL:
        if axis is not None:
            wide = list(w[n].shape)
            wide[axis] *= 4
            mine_once = w[n] * (1 - ci).astype(F32)
            placed[n] = lax.dynamic_update_slice_in_dim(jnp.zeros(wide, F32), mine_once, chip * w[n].shape[axis], axis)
    placed_shapes = {n: placed[n].shape for n in sharded_small}
    full.update(_unpack_small(_allreduce_small(_pack_small(placed, sharded_small)), placed_shapes, sharded_small))
    for n, axis in SMALL:
        if axis is None:
            full[n] = w[n]

    loss_part, grad_x, grads = _local_step(x[0], loss_target[0], full)

    packed = _pack_grads(grads)
    c_idx = jnp.reshape(ci, (1,)).astype(jnp.int32)
    chip_idx = jnp.reshape(chip, (1,)).astype(jnp.int32)
    pair, pair_bf16 = _pair_add(packed, _pair_exchange(packed), c_idx)
    mine = _chip_add(pair, _chip_exchange(pair_bf16), chip_idx)
    theirs = _pair_gather(mine)
    both = jnp.concatenate([jnp.where(ci == 0, mine, theirs), jnp.where(ci == 0, theirs, mine)], axis=0)
    red = _unpack_reduced(both, shard_shapes)

    small_names = [n for n, _ in SMALL]
    small_vals = {n: grads[n] for n in small_names}
    small_vals["loss"] = loss_part[0, :1]
    names = ["loss"] + small_names
    shapes = {n: small_vals[n].shape for n in names}
    summed = _unpack_small(_allreduce_small(_pack_small(small_vals, names)), shapes, names)
    loss = summed["loss"][0]
    for n, axis in SMALL:
        g = summed[n]
        if axis is not None:
            g = lax.dynamic_slice_in_dim(g, chip * w[n].shape[axis], w[n].shape[axis], axis)
        red[n] = g

    delta, new_m, new_v = {}, {}, {}
    for n, _ in BIG:
        delta[n], new_m[n], new_v[n] = _adamw_nd(w[n], red[n], m[n], v[n])
    local_shapes = {n: w[n].shape for n in small_names}
    packs = [_pack_small(t, small_names) for t in (w, red, m, v)]
    outs = _adamw(*packs)
    ds, ms, vs = (_unpack_small(o, local_shapes, small_names) for o in outs)
    delta.update(ds)
    new_m.update(ms)
    new_v.update(vs)

    return (loss, grad_x[None], *[red[n] for n in WEIGHT_ORDER], *[delta[n] for n in WEIGHT_ORDER],
            *[new_m[n] for n in WEIGHT_ORDER], *[new_v[n] for n in WEIGHT_ORDER])
```

```python
import functools
import math

import jax
import jax.numpy as jnp
from jax import lax
from jax.experimental import pallas as pl
from jax.experimental.pallas import tpu as pltpu

F32 = jnp.float32
BF16 = jnp.bfloat16
HIGHEST = lax.Precision.HIGHEST
MESH = pl.DeviceIdType.MESH

D_MODEL = 1024
CHUNK = 64
A_HEADS = 8
A_HEAD = 128
A_QK = A_HEADS * A_HEAD
A_CONV_WIDTH = 3 * A_QK
B_HEADS = 16
B_HEAD = 64
LEFT = 8 * CHUNK
QBLK = 4 * CHUNK
KBLK = LEFT + QBLK
REL_CLIP = 256
REL_PAD = 640
FFN_DIM = 2816
EPS = 1e-6
NEG_INF = -1e30
LANE = 128
SUB = 8
VMEM_LIMIT = 56 * 1024 * 1024

ADAM_LR = 0.001
ADAM_B1 = 0.9
ADAM_B2 = 0.999
ADAM_EPS = 1e-08
ADAM_WD = 0.01
ADAM_STEP = 10


VMEM_FULL = pl.BlockSpec(memory_space=pltpu.VMEM)


def _params(sem=None):
    return pltpu.CompilerParams(dimension_semantics=sem, vmem_limit_bytes=VMEM_LIMIT)


def _tile(n, cap):
    if n <= cap:
        return n
    best = None
    for t in range(LANE, cap + 1, LANE):
        if n % t == 0:
            best = t
    assert best is not None, n
    return best


def _sigmoid(x):
    return 1.0 / (1.0 + jnp.exp(-x))


def _softplus(x):
    return jnp.maximum(x, 0.0) + jnp.log(1.0 + jnp.exp(-jnp.abs(x)))


def _dot(a, b, dims, prec=None):
    return lax.dot_general(a, b, (dims, ((), ())), preferred_element_type=F32, precision=prec)


NN = ((1,), (0,))
NT = ((1,), (1,))
TN = ((0,), (0,))


def _bdot(a, b, dims):
    return _dot(a.astype(BF16), b.astype(BF16), dims)


def _mm(a, b, mode="nn", out_dtype=F32, res=None, name="mm"):
    if mode == "nn":
        (m, k), (k2, n) = a.shape, b.shape
    elif mode == "nt":
        (m, k), (n, k2) = a.shape, b.shape
    else:
        (k, m), (k2, n) = a.shape, b.shape
    assert k == k2, (a.shape, b.shape, mode)
    tm, tn, tk = _tile(m, 1408), _tile(n, 1408), _tile(k, 1408)
    if m == 8192:
        tm = 1024
    if k == 8192:
        tk = 1024
    nk = k // tk
    dims = {"nn": NN, "nt": NT, "tn": TN}[mode]
    a_spec = {"nn": pl.BlockSpec((tm, tk), lambda i, j, kk: (i, kk)),
              "nt": pl.BlockSpec((tm, tk), lambda i, j, kk: (i, kk)),
              "tn": pl.BlockSpec((tk, tm), lambda i, j, kk: (kk, i))}[mode]
    b_spec = {"nn": pl.BlockSpec((tk, tn), lambda i, j, kk: (kk, j)),
              "nt": pl.BlockSpec((tn, tk), lambda i, j, kk: (j, kk)),
              "tn": pl.BlockSpec((tk, tn), lambda i, j, kk: (kk, j))}[mode]
    o_spec = pl.BlockSpec((tm, tn), lambda i, j, kk: (i, j))
    has_res = res is not None

    def body(a_ref, b_ref, *rest):
        if has_res:
            res_ref, o_ref, acc = rest
        else:
            o_ref, acc = rest
        kk = pl.program_id(2)

        @pl.when(kk == 0)
        def _():
            acc[...] = jnp.zeros_like(acc)

        acc[...] += _bdot(a_ref[...], b_ref[...], dims)

        @pl.when(kk == nk - 1)
        def _():
            r = acc[...]
            if has_res:
                r = r + res_ref[...]
            o_ref[...] = r.astype(out_dtype)

    args = [a, b] + ([res] if has_res else [])
    in_specs = [a_spec, b_spec] + ([o_spec] if has_res else [])
    return pl.pallas_call(
        body, name=name, grid=(m // tm, n // tn, nk),
        in_specs=in_specs, out_specs=o_spec,
        out_shape=jax.ShapeDtypeStruct((m, n), out_dtype),
        scratch_shapes=[pltpu.VMEM((tm, tn), F32)],
        compiler_params=_params(("parallel", "parallel", "arbitrary")),
    )(*args)


def _rmsnorm_fwd(x, g):
    s, d = x.shape
    tr = _tile(s, 1024)

    def body(x_ref, g_ref, o_ref):
        xv = x_ref[...]
        r = lax.rsqrt(jnp.mean(xv * xv, axis=-1, keepdims=True) + EPS)
        o_ref[...] = (xv * r * g_ref[...]).astype(BF16)

    return pl.pallas_call(
        body, name="rmsnorm_fwd", grid=(s // tr,),
        in_specs=[pl.BlockSpec((tr, d), lambda i: (i, 0)), pl.BlockSpec((1, d), lambda i: (0, 0))],
        out_specs=pl.BlockSpec((tr, d), lambda i: (i, 0)),
        out_shape=jax.ShapeDtypeStruct((s, d), BF16),
        compiler_params=_params(("parallel",)),
    )(x, g)


def _rmsnorm_bwd(x, g, dxn, dres):
    s, d = x.shape
    tr = _tile(s, 1024)

    def body(x_ref, g_ref, dxn_ref, dres_ref, dx_ref, dg_ref):
        @pl.when(pl.program_id(0) == 0)
        def _():
            dg_ref[...] = jnp.zeros_like(dg_ref)

        xv = x_ref[...]
        r = lax.rsqrt(jnp.mean(xv * xv, axis=-1, keepdims=True) + EPS)
        dy = dxn_ref[...]
        t = dy * g_ref[...]
        c = jnp.mean(t * xv, axis=-1, keepdims=True)
        dx_ref[...] = dres_ref[...] + r * t - xv * (r * r * r) * c
        dg_ref[...] += jnp.sum(dy * xv * r, axis=0, keepdims=True)

    row = pl.BlockSpec((tr, d), lambda i: (i, 0))
    vec = pl.BlockSpec((1, d), lambda i: (0, 0))
    return pl.pallas_call(
        body, name="rmsnorm_bwd", grid=(s // tr,),
        in_specs=[row, vec, row, row], out_specs=[row, vec],
        out_shape=[jax.ShapeDtypeStruct((s, d), F32), jax.ShapeDtypeStruct((1, d), F32)],
        compiler_params=_params(("arbitrary",)),
    )(x, g, dxn, dres)


def _final_loss(h, g, tgt):
    s, d = h.shape
    tr = _tile(s, 1024)

    def body(x_ref, g_ref, t_ref, loss_ref, dx_ref, dg_ref):
        @pl.when(pl.program_id(0) == 0)
        def _():
            dg_ref[...] = jnp.zeros_like(dg_ref)
            loss_ref[...] = jnp.zeros_like(loss_ref)

        xv = x_ref[...]
        r = lax.rsqrt(jnp.mean(xv * xv, axis=-1, keepdims=True) + EPS)
        xh = xv * r
        err = xh * g_ref[...] - t_ref[...]
        per_row = jnp.mean(err * err, axis=-1, keepdims=True)
        loss_ref[...] += 0.5 * jnp.sum(per_row, axis=0, keepdims=True)
        dy = err * (1.0 / d)
        t = dy * g_ref[...]
        c = jnp.mean(t * xv, axis=-1, keepdims=True)
        dx_ref[...] = r * t - xv * (r * r * r) * c
        dg_ref[...] += jnp.sum(dy * xh, axis=0, keepdims=True)

    row = pl.BlockSpec((tr, d), lambda i: (i, 0))
    vec = pl.BlockSpec((1, d), lambda i: (0, 0))
    return pl.pallas_call(
        body, name="final_loss", grid=(s // tr,),
        in_specs=[row, vec, row],
        out_specs=[pl.BlockSpec((1, LANE), lambda i: (0, 0)), row, vec],
        out_shape=[jax.ShapeDtypeStruct((1, LANE), F32), jax.ShapeDtypeStruct((s, d), F32),
                   jax.ShapeDtypeStruct((1, d), F32)],
        compiler_params=_params(("arbitrary",)),
    )(h, g, tgt)


def _shift_rows(cur, edge, k, reverse):
    if k == 0:
        return cur
    tr, tc = cur.shape
    row = lax.broadcasted_iota(jnp.int32, (SUB, tc), 0)
    if not reverse:
        r = pltpu.roll(cur, k, 0)
        e = pltpu.roll(edge, k, 0)
        first = jnp.where(row < k, e, r[:SUB])
        return jnp.concatenate([first, r[SUB:]], axis=0)
    r = pltpu.roll(cur, tr - k, 0)
    e = pltpu.roll(edge, SUB - k, 0)
    last = jnp.where(row >= SUB - k, e, r[tr - SUB:])
    return jnp.concatenate([r[:tr - SUB], last], axis=0)


def _conv_taps(cur, edge, w, reverse):
    width = w.shape[0]
    acc = None
    for k in range(width):
        term = _shift_rows(cur, edge, width - 1 - k, reverse) * w[k:k + 1, :]
        acc = term if acc is None else acc + term
    return acc


def _conv_tiles(s, c):
    return _tile(s, 256), _tile(c, 1408)


def _conv_silu_fwd(pre, w):
    s, c = pre.shape
    width = w.shape[0]
    tr, tc = _conv_tiles(s, c)

    def body(x_ref, w_ref, o_ref, tail):
        @pl.when(pl.program_id(1) == 0)
        def _():
            tail[...] = jnp.zeros_like(tail)

        cur = x_ref[...]
        y = _conv_taps(cur, tail[...], w_ref[...], False)
        o_ref[...] = y * _sigmoid(y)
        tail[...] = cur[tr - SUB:]

    blk = pl.BlockSpec((tr, tc), lambda j, i: (i, j))
    return pl.pallas_call(
        body, name="conv_silu_fwd", grid=(c // tc, s // tr),
        in_specs=[blk, pl.BlockSpec((width, tc), lambda j, i: (0, j))], out_specs=blk,
        out_shape=jax.ShapeDtypeStruct((s, c), F32),
        scratch_shapes=[pltpu.VMEM((SUB, tc), F32)],
        compiler_params=_params(("parallel", "arbitrary")),
    )(pre, w)


def _conv_silu_bwd(pre, w, dact):
    s, c = pre.shape
    width = w.shape[0]
    tr, tc = _conv_tiles(s, c)

    def body(x_ref, w_ref, d_ref, dy_ref, dw_ref, tail):
        @pl.when(pl.program_id(1) == 0)
        def _():
            tail[...] = jnp.zeros_like(tail)
            dw_ref[...] = jnp.zeros_like(dw_ref)

        cur = x_ref[...]
        edge = tail[...]
        y = _conv_taps(cur, edge, w_ref[...], False)
        sg = _sigmoid(y)
        dy = d_ref[...] * sg * (1.0 + y * (1.0 - sg))
        dy_ref[...] = dy
        for k in range(width):
            xs = _shift_rows(cur, edge, width - 1 - k, False)
            dw_ref[k:k + 1, :] += jnp.sum(dy * xs, axis=0, keepdims=True)
        tail[...] = cur[tr - SUB:]

    blk = pl.BlockSpec((tr, tc), lambda j, i: (i, j))
    wblk = pl.BlockSpec((width, tc), lambda j, i: (0, j))
    return pl.pallas_call(
        body, name="conv_silu_bwd", grid=(c // tc, s // tr),
        in_specs=[blk, wblk, blk], out_specs=[blk, wblk],
        out_shape=[jax.ShapeDtypeStruct((s, c), F32), jax.ShapeDtypeStruct((width, c), F32)],
        scratch_shapes=[pltpu.VMEM((SUB, tc), F32)],
        compiler_params=_params(("parallel", "arbitrary")),
    )(pre, w, dact)


def _conv_transpose(dy, w):
    s, c = dy.shape
    width = w.shape[0]
    tr, tc = _conv_tiles(s, c)
    nr = s // tr

    def body(d_ref, w_ref, o_ref, head):
        @pl.when(pl.program_id(1) == 0)
        def _():
            head[...] = jnp.zeros_like(head)

        cur = d_ref[...]
        o_ref[...] = _conv_taps(cur, head[...], w_ref[...], True)
        head[...] = cur[:SUB]

    blk = pl.BlockSpec((tr, tc), lambda j, i: (nr - 1 - i, j))
    return pl.pallas_call(
        body, name="conv_transpose", grid=(c // tc, nr),
        in_specs=[blk, pl.BlockSpec((width, tc), lambda j, i: (0, j))], out_specs=blk,
        out_shape=jax.ShapeDtypeStruct((s, c), F32),
        scratch_shapes=[pltpu.VMEM((SUB, tc), F32)],
        compiler_params=_params(("parallel", "arbitrary")),
    )(dy, w)


def _ffn_act_fwd(pg, pv, wg, wv, bg, bv):
    s, c = pg.shape
    width = wg.shape[0]
    tr, tc = _conv_tiles(s, c)

    def body(g_ref, v_ref, wg_ref, wv_ref, bg_ref, bv_ref, o_ref, tg, tv):
        @pl.when(pl.program_id(1) == 0)
        def _():
            tg[...] = jnp.zeros_like(tg)
            tv[...] = jnp.zeros_like(tv)

        cg, cv = g_ref[...], v_ref[...]
        yg = _conv_taps(cg, tg[...], wg_ref[...], False) + bg_ref[...]
        yv = _conv_taps(cv, tv[...], wv_ref[...], False) + bv_ref[...]
        o_ref[...] = (yg * _sigmoid(yg) * yv).astype(BF16)
        tg[...] = cg[tr - SUB:]
        tv[...] = cv[tr - SUB:]

    blk = pl.BlockSpec((tr, tc), lambda j, i: (i, j))
    wblk = pl.BlockSpec((width, tc), lambda j, i: (0, j))
    bblk = pl.BlockSpec((1, tc), lambda j, i: (0, j))
    return pl.pallas_call(
        body, name="ffn_act_fwd", grid=(c // tc, s // tr),
        in_specs=[blk, blk, wblk, wblk, bblk, bblk], out_specs=blk,
        out_shape=jax.ShapeDtypeStruct((s, c), BF16),
        scratch_shapes=[pltpu.VMEM((SUB, tc), F32), pltpu.VMEM((SUB, tc), F32)],
        compiler_params=_params(("parallel", "arbitrary")),
    )(pg, pv, wg, wv, bg, bv)


def _ffn_act_bwd(pg, pv, wg, wv, bg, bv, dact):
    s, c = pg.shape
    width = wg.shape[0]
    tr, tc = _conv_tiles(s, c)

    def body(g_ref, v_ref, wg_ref, wv_ref, bg_ref, bv_ref, d_ref,
             dyg_ref, dyv_ref, dwg_ref, dwv_ref, dbg_ref, dbv_ref, tg, tv):
        @pl.when(pl.program_id(1) == 0)
        def _():
            for r in (tg, tv, dwg_ref, dwv_ref, dbg_ref, dbv_ref):
                r[...] = jnp.zeros_like(r)

        cg, cv = g_ref[...], v_ref[...]
        eg, ev = tg[...], tv[...]
        yg = _conv_taps(cg, eg, wg_ref[...], False) + bg_ref[...]
        yv = _conv_taps(cv, ev, wv_ref[...], False) + bv_ref[...]
        sg = _sigmoid(yg)
        da = d_ref[...]
        dyv = da * yg * sg
        dyg = da * yv * sg * (1.0 + yg * (1.0 - sg))
        dyg_ref[...] = dyg
        dyv_ref[...] = dyv
        dbg_ref[...] += jnp.sum(dyg, axis=0, keepdims=True)
        dbv_ref[...] += jnp.sum(dyv, axis=0, keepdims=True)
        for k in range(width):
            dwg_ref[k:k + 1, :] += jnp.sum(dyg * _shift_rows(cg, eg, width - 1 - k, False), axis=0, keepdims=True)
            dwv_ref[k:k + 1, :] += jnp.sum(dyv * _shift_rows(cv, ev, width - 1 - k, False), axis=0, keepdims=True)
        tg[...] = cg[tr - SUB:]
        tv[...] = cv[tr - SUB:]

    blk = pl.BlockSpec((tr, tc), lambda j, i: (i, j))
    wblk = pl.BlockSpec((width, tc), lambda j, i: (0, j))
    bblk = pl.BlockSpec((1, tc), lambda j, i: (0, j))
    big = jax.ShapeDtypeStruct((s, c), F32)
    wsh = jax.ShapeDtypeStruct((width, c), F32)
    bsh = jax.ShapeDtypeStruct((1, c), F32)
    return pl.pallas_call(
        body, name="ffn_act_bwd", grid=(c // tc, s // tr),
        in_specs=[blk, blk, wblk, wblk, bblk, bblk, blk],
        out_specs=[blk, blk, wblk, wblk, bblk, bblk],
        out_shape=[big, big, wsh, wsh, bsh, bsh],
        scratch_shapes=[pltpu.VMEM((SUB, tc), F32), pltpu.VMEM((SUB, tc), F32)],
        compiler_params=_params(("parallel", "arbitrary")),
    )(pg, pv, wg, wv, bg, bv, dact)


def _tri_masks():
    row = lax.broadcasted_iota(jnp.int32, (CHUNK, CHUNK), 0)
    col = lax.broadcasted_iota(jnp.int32, (CHUNK, CHUNK), 1)
    return row, col


def _tri_inv(ms, row, col):
    eye = (row == col).astype(F32)
    same_blk = (row >> 4) == (col >> 4)
    mds = [jnp.where(same_blk, m, 0.0) for m in ms]
    offs = [m - md for m, md in zip(ms, mds)]
    xs = [eye - md for md in mds]
    ps = [_bdot(md, md, NN) for md in mds]
    for _ in range(2):
        rs = [_bdot(jnp.concatenate([x, p], axis=0), p, NN) for x, p in zip(xs, ps)]
        xs = [x + r[:CHUNK] for x, r in zip(xs, rs)]
        ps = [r[CHUNK:] for r in rs]
    xs = [x + _bdot(x, p, NN) for x, p in zip(xs, ps)]
    ps = [_bdot(x, off, NN) for x, off in zip(xs, offs)]
    pps = [_bdot(p, p, NN) for p in ps]
    ys = [eye - p for p in ps]
    ys = [y + _bdot(y, pp, NN) for y, pp in zip(ys, pps)]
    return [_bdot(y, x, NN) for y, x in zip(ys, xs)]


def _gdn_gates(ba, alog, dtb, row, col):
    sig = _sigmoid(ba)
    neg_a = -jnp.exp(alog)
    g = neg_a * _softplus(ba + dtb)
    lower = (row >= col).astype(F32)
    gcum = _dot(lower, g, NN, HIGHEST)
    return sig, neg_a, g, gcum


def _gdn_head_common(q_raw, k_raw, v, beta, gc, gr, row, col):
    causal = row >= col
    strict = row > col
    rq = lax.rsqrt(jnp.sum(q_raw * q_raw, axis=-1, keepdims=True) + EPS)
    rk = lax.rsqrt(jnp.sum(k_raw * k_raw, axis=-1, keepdims=True) + EPS)
    q = q_raw * (rq * (A_HEAD ** -0.5))
    k = k_raw * rk
    decay = jnp.where(causal, jnp.exp(jnp.where(causal, gc - gr, 0.0)), 0.0)
    eg = jnp.exp(gc)
    gl = gc[CHUNK - 1:CHUNK, :]
    ekl = jnp.exp(gl - gc)
    dec = jnp.exp(gl)
    kb = k * beta
    kbq = jnp.concatenate([kb, q], axis=0)
    both = _bdot(kbq, k, NT)
    kk, qk = both[:CHUNK], both[CHUNK:]
    a = jnp.where(causal, qk * decay, 0.0)
    return dict(rq=rq, rk=rk, q=q, k=k, decay=decay, eg=eg, ekl=ekl, dec=dec, kb=kb, kbq=kbq, kk=kk, qk=qk, a=a,
                vb=v * beta, kbg=kb * eg, qd=q * eg, ke=k * ekl, causal=causal, strict=strict)


def _gdn_fwd(qkv, ba, z, alog, dtb, wn):
    s = qkv.shape[0]
    nc = s // CHUNK

    def body(qkv_ref, ba_ref, z_ref, alog_ref, dtb_ref, wn_ref, y_ref, o_ref, st_ref, t_ref, w_ref, vn_ref, state):
        @pl.when(pl.program_id(0) == 0)
        def _():
            state[...] = jnp.zeros_like(state)

        row, col = _tri_masks()
        sig, _, _, gcum = _gdn_gates(ba_ref[...], alog_ref[...], dtb_ref[...], row, col)
        gt = gcum.T
        heads = range(A_HEADS)
        lanes = [slice(h * A_HEAD, (h + 1) * A_HEAD) for h in heads]
        fs = [_gdn_head_common(qkv_ref[:, lanes[h]], qkv_ref[:, A_QK + h * A_HEAD:A_QK + (h + 1) * A_HEAD],
                               qkv_ref[:, 2 * A_QK + h * A_HEAD:2 * A_QK + (h + 1) * A_HEAD],
                               sig[:, h:h + 1], gcum[:, 8 + h:9 + h], gt[8 + h:9 + h, :], row, col) for h in heads]
        ts = [t.astype(BF16) for t in
              _tri_inv([jnp.where(f["strict"], f["kk"] * f["decay"], 0.0) for f in fs], row, col)]
        uws = [_bdot(t, jnp.concatenate([f["vb"], f["kbg"]], axis=1), NN) for t, f in zip(ts, fs)]
        s0s = [state[h] for h in heads]
        ws_ = [uw[:, A_HEAD:].astype(BF16) for uw in uws]
        wss = [_bdot(jnp.concatenate([w, f["qd"].astype(BF16)], axis=0), s0, NN) for w, f, s0 in zip(ws_, fs, s0s)]
        vnews = [(uw[:, :A_HEAD] - wsq[:CHUNK]).astype(BF16) for uw, wsq in zip(uws, wss)]
        os_ = [wsq[CHUNK:] + _bdot(f["a"], vn, NN) for wsq, f, vn in zip(wss, fs, vnews)]
        s1s = [s0 * f["dec"] + _bdot(f["ke"], vn, TN) for s0, f, vn in zip(s0s, fs, vnews)]
        for h in heads:
            ln = lanes[h]
            st_ref[0, h] = s0s[h]
            t_ref[0, h] = ts[h]
            w_ref[:, ln] = ws_[h]
            vn_ref[:, ln] = vnews[h]
            state[h] = s1s[h]
            o = os_[h]
            o_ref[:, ln] = o
            r = lax.rsqrt(jnp.mean(o * o, axis=-1, keepdims=True) + EPS)
            zz = z_ref[:, ln]
            y_ref[:, ln] = (o * r * wn_ref[...] * zz * _sigmoid(zz)).astype(BF16)

    vec = pl.BlockSpec((1, LANE), lambda n: (0, 0))
    wide = pl.BlockSpec((CHUNK, A_QK), lambda n: (n, 0))
    return pl.pallas_call(
        body, name="gdn_fwd", grid=(nc,),
        in_specs=[pl.BlockSpec((CHUNK, A_CONV_WIDTH), lambda n: (n, 0)),
                  pl.BlockSpec((CHUNK, LANE), lambda n: (n, 0)), wide, vec, vec, vec],
        out_specs=[wide, wide, pl.BlockSpec((1, A_HEADS, A_HEAD, A_HEAD), lambda n: (n, 0, 0, 0)),
                   pl.BlockSpec((1, A_HEADS, CHUNK, CHUNK), lambda n: (n, 0, 0, 0)), wide, wide],
        out_shape=[jax.ShapeDtypeStruct((s, A_QK), BF16), jax.ShapeDtypeStruct((s, A_QK), F32),
                   jax.ShapeDtypeStruct((nc, A_HEADS, A_HEAD, A_HEAD), F32),
                   jax.ShapeDtypeStruct((nc, A_HEADS, CHUNK, CHUNK), BF16),
                   jax.ShapeDtypeStruct((s, A_QK), BF16), jax.ShapeDtypeStruct((s, A_QK), BF16)],
        scratch_shapes=[pltpu.VMEM((A_HEADS, A_HEAD, A_HEAD), F32)],
        compiler_params=_params(("arbitrary",)),
    )(qkv, ba, z, alog, dtb, wn)


def _gdn_bwd(qkv, ba, z, o_raw, dy, states, t_all, w_all, vn_all, alog, dtb, wn):
    s = qkv.shape[0]
    nc = s // CHUNK

    def body(qkv_ref, ba_ref, z_ref, o_ref, dy_ref, st_ref, t_ref, w_ref, vn_ref, alog_ref, dtb_ref, wn_ref,
             dqkv_ref, dba_ref, dz_ref, dalog_ref, ddtb_ref, dwn_ref, dstate):
        @pl.when(pl.program_id(0) == 0)
        def _():
            for r in (dstate, dalog_ref, ddtb_ref, dwn_ref):
                r[...] = jnp.zeros_like(r)

        row, col = _tri_masks()
        bat = ba_ref[...]
        sig, neg_a, g, gcum = _gdn_gates(bat, alog_ref[...], dtb_ref[...], row, col)
        gt = gcum.T
        lane = lax.broadcasted_iota(jnp.int32, (CHUNK, LANE), 1)
        ones = jnp.ones((CHUNK, LANE), F32)
        last_row = lax.broadcasted_iota(jnp.int32, (CHUNK, 1), 0) == CHUNK - 1
        wnv = wn_ref[...]
        dgc_tile = jnp.zeros((CHUNK, LANE), F32)
        dbeta_tile = jnp.zeros((CHUNK, LANE), F32)
        dwn_acc = jnp.zeros((1, LANE), F32)
        hs = []
        for h in range(A_HEADS):
            ln = slice(h * A_HEAD, (h + 1) * A_HEAD)
            lk = slice(A_QK + h * A_HEAD, A_QK + (h + 1) * A_HEAD)
            lv = slice(2 * A_QK + h * A_HEAD, 2 * A_QK + (h + 1) * A_HEAD)
            q_raw, k_raw, v = qkv_ref[:, ln], qkv_ref[:, lk], qkv_ref[:, lv]
            f = _gdn_head_common(q_raw, k_raw, v, sig[:, h:h + 1], gcum[:, 8 + h:9 + h], gt[8 + h:9 + h, :], row, col)
            f.update(h=h, ln=ln, lk=lk, lv=lv, q_raw=q_raw, k_raw=k_raw, v=v, beta=sig[:, h:h + 1],
                     s0=st_ref[0, h], ds1=dstate[h], t=t_ref[0, h], w=w_ref[:, ln], vnew=vn_ref[:, ln])
            o = o_ref[:, ln]
            zz = z_ref[:, ln]
            dyv = dy_ref[:, ln]
            r = lax.rsqrt(jnp.mean(o * o, axis=-1, keepdims=True) + EPS)
            sz = _sigmoid(zz)
            silu = zz * sz
            dz_ref[:, ln] = dyv * o * r * wnv * sz * (1.0 + zz * (1.0 - sz))
            dwn_acc = dwn_acc + jnp.sum(dyv * silu * o * r, axis=0, keepdims=True)
            tt = dyv * silu * wnv
            do = r * tt - o * (r * r * r) * jnp.mean(tt * o, axis=-1, keepdims=True)
            f["do_b"] = do.astype(BF16)
            hs.append(f)
        for f in hs:
            f["dvnew"] = _bdot(f["a"], f["do_b"], TN) + _bdot(f["ke"], f["ds1"], NN)
            f["da"] = jnp.where(f["causal"], _bdot(f["do_b"], f["vnew"], NT), 0.0)
            f["dke"] = _bdot(f["vnew"], f["ds1"], NT)
            f["ddec"] = jnp.sum(jnp.sum(f["s0"] * f["ds1"], axis=1, keepdims=True), axis=0, keepdims=True)
        for f in hs:
            do_dv = jnp.concatenate([f["do_b"], f["dvnew"].astype(BF16)], axis=0)
            both = _bdot(do_dv, f["s0"], NT)
            f["dqd"], f["dw"] = both[:CHUNK], -both[CHUNK:]
            qd_w = jnp.concatenate([f["qd"].astype(BF16), -f["w"]], axis=0)
            dstate[f["h"]] = _bdot(qd_w, do_dv, TN) + f["dec"] * f["ds1"]
        for f in hs:
            dd = jnp.concatenate([f["dvnew"], f["dw"]], axis=1).astype(BF16)
            tdd = _bdot(f["t"], dd, TN)
            f["dvb"], f["dkbg"] = tdd[:, :A_HEAD], tdd[:, A_HEAD:]
            f["dt"] = _bdot(dd, jnp.concatenate([f["vb"], f["kbg"]], axis=1), NT)
        for f in hs:
            f["tdt"] = _bdot(f["t"], f["dt"], TN)
        for f in hs:
            dm = jnp.where(f["strict"], -_bdot(f["tdt"], f["t"], NT), 0.0)
            f["ddecay"] = (dm * f["kk"] + f["da"] * f["qk"]) * f["decay"]
            f["dboth"] = jnp.concatenate([dm * f["decay"], f["da"] * f["decay"]], axis=0).astype(BF16)
        for f in hs:
            f["r2"] = _bdot(f["dboth"], f["k"], NN)
            f["dk0"] = _bdot(f["dboth"], f["kbq"], TN)
        for f in hs:
            h, k, beta = f["h"], f["k"], f["beta"]
            dkb = f["r2"][:CHUNK] + f["dkbg"] * f["eg"]
            dq = f["r2"][CHUNK:] + f["dqd"] * f["eg"]
            dk = f["dk0"] + f["dke"] * f["ekl"] + dkb * beta
            dke_ke = jnp.sum(f["dke"] * f["ke"], axis=-1, keepdims=True)
            dgc = (jnp.sum(f["ddecay"], axis=-1, keepdims=True)
                   + jnp.sum(f["dqd"] * f["qd"], axis=-1, keepdims=True) - dke_ke
                   + jnp.sum(f["dkbg"] * f["kbg"], axis=-1, keepdims=True))
            dgl = jnp.sum(dke_ke, axis=0, keepdims=True) + f["ddec"] * f["dec"]
            dgc = dgc + jnp.where(last_row, dgl, 0.0)
            dbeta = jnp.sum(dkb * k, axis=-1, keepdims=True) + jnp.sum(f["dvb"] * f["v"], axis=-1, keepdims=True)
            dgc_tile = dgc_tile + jnp.where(lane == 8 + h, dgc, 0.0)
            dbeta_tile = dbeta_tile + jnp.where(lane == h, dbeta, 0.0)
            dqn = dq * (A_HEAD ** -0.5)
            rq, rk, q_raw, k_raw = f["rq"], f["rk"], f["q_raw"], f["k_raw"]
            dqkv_ref[:, f["ln"]] = rq * dqn - q_raw * (rq * rq * rq) * jnp.sum(dqn * q_raw, axis=-1, keepdims=True)
            dqkv_ref[:, f["lk"]] = rk * dk - k_raw * (rk * rk * rk) * jnp.sum(dk * k_raw, axis=-1, keepdims=True)
            dqkv_ref[:, f["lv"]] = f["dvb"] * beta
        ddecays = [f["ddecay"] for f in hs]
        col_sums = _dot(jnp.concatenate(ddecays, axis=1), ones, TN, HIGHEST)
        for h in range(A_HEADS):
            dgc_tile = dgc_tile - jnp.where(lane == 8 + h, col_sums[h * CHUNK:(h + 1) * CHUNK, :1], 0.0)
        upper = (row <= col).astype(F32)
        dg = _dot(upper, dgc_tile, NN, HIGHEST)
        da_raw = dg * neg_a * _sigmoid(bat + dtb_ref[...])
        dba_ref[...] = jnp.where(lane < 8, dbeta_tile * sig * (1.0 - sig), jnp.where(lane < 16, da_raw, 0.0))
        dwn_ref[...] += dwn_acc
        ddtb_ref[...] += jnp.sum(da_raw, axis=0, keepdims=True)
        dalog_ref[...] += jnp.sum(dg * g, axis=0, keepdims=True)

    rev = lambda n: (nc - 1 - n, 0)
    vec = pl.BlockSpec((1, LANE), lambda n: (0, 0))
    wide = pl.BlockSpec((CHUNK, A_QK), rev)
    qkv_blk = pl.BlockSpec((CHUNK, A_CONV_WIDTH), rev)
    ba_blk = pl.BlockSpec((CHUNK, LANE), rev)
    vsh = jax.ShapeDtypeStruct((1, LANE), F32)
    return pl.pallas_call(
        body, name="gdn_bwd", grid=(nc,),
        in_specs=[qkv_blk, ba_blk, wide, wide, wide,
                  pl.BlockSpec((1, A_HEADS, A_HEAD, A_HEAD), lambda n: (nc - 1 - n, 0, 0, 0)),
                  pl.BlockSpec((1, A_HEADS, CHUNK, CHUNK), lambda n: (nc - 1 - n, 0, 0, 0)), wide, wide,
                  vec, vec, vec],
        out_specs=[qkv_blk, ba_blk, wide, vec, vec, vec],
        out_shape=[jax.ShapeDtypeStruct((s, A_CONV_WIDTH), F32), jax.ShapeDtypeStruct((s, LANE), F32),
                   jax.ShapeDtypeStruct((s, A_QK), F32), vsh, vsh, vsh],
        scratch_shapes=[pltpu.VMEM((A_HEADS, A_HEAD, A_HEAD), F32)],
        compiler_params=_params(("arbitrary",)),
    )(qkv, ba, z, o_raw, dy, states, t_all, w_all, vn_all, alog, dtb, wn)


REL_RING = 1024
QBLK_BITS = 8


def _rel_ring_onehot():
    m = lax.broadcasted_iota(jnp.int32, (REL_RING, REL_PAD), 0)
    t = lax.broadcasted_iota(jnp.int32, (REL_RING, REL_PAD), 1)
    u = jnp.where(m < KBLK, m, m - REL_RING)
    idx = jnp.clip(LEFT - u, -REL_CLIP, REL_CLIP) + REL_CLIP
    return (t == idx).astype(F32)


def _relbias_ring(table, transpose):
    n_in, n_out = (REL_RING, REL_PAD) if transpose else (REL_PAD, REL_RING)

    def body(t_ref, o_ref):
        o_ref[...] = _dot(t_ref[...], _rel_ring_onehot(), NN if transpose else NT, HIGHEST)

    return pl.pallas_call(
        body, name="relbias_ring_bwd" if transpose else "relbias_ring",
        out_shape=jax.ShapeDtypeStruct((B_HEADS, n_out), F32),
        in_specs=[VMEM_FULL], out_specs=VMEM_FULL,
        compiler_params=_params(),
    )(table)


def _row_bit(shape, bit):
    return ((lax.broadcasted_iota(jnp.int32, shape, 0) >> bit) & 1) == 1


def _relbias_expand(ring):
    def body(r_ref, o_ref):
        b = jnp.broadcast_to(r_ref[0], (QBLK, REL_RING))
        for bit in range(QBLK_BITS):
            b = jnp.where(_row_bit(b.shape, bit), pltpu.roll(b, 1 << bit, 1), b)
        j = lax.broadcasted_iota(jnp.int32, (QBLK, KBLK), 1)
        r = lax.broadcasted_iota(jnp.int32, (QBLK, KBLK), 0)
        lo = (r >> 6) << 6
        o_ref[0] = jnp.where((j >= lo) & (j < lo + LEFT + CHUNK), b[:, :KBLK], NEG_INF)

    return pl.pallas_call(
        body, name="relbias_expand", grid=(B_HEADS,),
        in_specs=[pl.BlockSpec((1, 1, REL_RING), lambda h: (h, 0, 0))],
        out_specs=pl.BlockSpec((1, QBLK, KBLK), lambda h: (h, 0, 0)),
        out_shape=jax.ShapeDtypeStruct((B_HEADS, QBLK, KBLK), F32),
        compiler_params=_params(("parallel",)),
    )(ring)


def _relbias_reduce(ds):
    def body(d_ref, o_ref):
        d = jnp.concatenate([d_ref[0], jnp.zeros((QBLK, REL_RING - KBLK), F32)], axis=1)
        for bit in range(QBLK_BITS):
            d = jnp.where(_row_bit(d.shape, bit), pltpu.roll(d, REL_RING - (1 << bit), 1), d)
        o_ref[0] = jnp.sum(d, axis=0, keepdims=True)

    return pl.pallas_call(
        body, name="relbias_reduce", grid=(B_HEADS,),
        in_specs=[pl.BlockSpec((1, QBLK, KBLK), lambda h: (h, 0, 0))],
        out_specs=pl.BlockSpec((1, 1, REL_RING), lambda h: (h, 0, 0)),
        out_shape=jax.ShapeDtypeStruct((B_HEADS, 1, REL_RING), F32),
        compiler_params=_params(("parallel",)),
    )(ds)


def _attn_probs(q_ref, kb, b_ref, hh, q0):
    hl = slice(hh * B_HEAD, (hh + 1) * B_HEAD)
    qh = q_ref[:, hl]
    kh = kb[:, hl]
    jpos = lax.broadcasted_iota(jnp.int32, (QBLK, KBLK), 1)
    sc = _bdot(qh, kh, NT) * (B_HEAD ** -0.5) + b_ref[hh]
    sc = jnp.where(jpos + q0 >= LEFT, sc, NEG_INF)
    mx = jnp.max(sc, axis=-1, keepdims=True)
    p = jnp.exp(sc - mx)
    return p / jnp.sum(p, axis=-1, keepdims=True), qh, kh


def _attn_fwd(q, kpad, vpad, bias):
    s = q.shape[0]

    def body(q_ref, k_ref, v_ref, b_ref, o_ref):
        q0 = pl.multiple_of(pl.program_id(1) * QBLK, QBLK)
        kb = k_ref[pl.ds(q0, KBLK), :]
        vb = v_ref[pl.ds(q0, KBLK), :]
        outs = []
        for hh in range(2):
            p, _, _ = _attn_probs(q_ref, kb, b_ref, hh, q0)
            outs.append(_bdot(p, vb[:, hh * B_HEAD:(hh + 1) * B_HEAD], NN))
        o_ref[...] = jnp.concatenate(outs, axis=1).astype(BF16)

    qblk = pl.BlockSpec((QBLK, LANE), lambda g, m: (m, g))
    kblk = pl.BlockSpec((LEFT + s, LANE), lambda g, m: (0, g))
    return pl.pallas_call(
        body, name="attn_fwd", grid=(B_HEADS // 2, s // QBLK),
        in_specs=[qblk, kblk, kblk, pl.BlockSpec((2, QBLK, KBLK), lambda g, m: (g, 0, 0))],
        out_specs=qblk,
        out_shape=jax.ShapeDtypeStruct((s, D_MODEL), BF16),
        compiler_params=_params(("parallel", "arbitrary")),
    )(q, kpad, vpad, bias)


def _attn_bwd(q, kpad, vpad, bias, do):
    s = q.shape[0]

    def body(q_ref, k_ref, v_ref, b_ref, do_ref, dq_ref, dk_ref, dv_ref, db_ref):
        @pl.when(pl.program_id(1) == 0)
        def _():
            for r in (dk_ref, dv_ref, db_ref):
                r[...] = jnp.zeros_like(r)

        q0 = pl.multiple_of(pl.program_id(1) * QBLK, QBLK)
        kb = k_ref[pl.ds(q0, KBLK), :]
        vb = v_ref[pl.ds(q0, KBLK), :]
        dqs, dks, dvs = [], [], []
        for hh in range(2):
            hl = slice(hh * B_HEAD, (hh + 1) * B_HEAD)
            p, qh, kh = _attn_probs(q_ref, kb, b_ref, hh, q0)
            doh = do_ref[:, hl]
            dp = _bdot(doh, vb[:, hl], NT)
            dsc = p * (dp - jnp.sum(p * dp, axis=-1, keepdims=True))
            db_ref[hh] += dsc
            dqs.append(_bdot(dsc, kh, NN) * (B_HEAD ** -0.5))
            dks.append(_bdot(dsc, qh, TN) * (B_HEAD ** -0.5))
            dvs.append(_bdot(p, doh, TN))
        dq_ref[...] = jnp.concatenate(dqs, axis=1)
        dk_ref[pl.ds(q0, KBLK), :] += jnp.concatenate(dks, axis=1)
        dv_ref[pl.ds(q0, KBLK), :] += jnp.concatenate(dvs, axis=1)

    qblk = pl.BlockSpec((QBLK, LANE), lambda g, m: (m, g))
    kblk = pl.BlockSpec((LEFT + s, LANE), lambda g, m: (0, g))
    bblk = pl.BlockSpec((2, QBLK, KBLK), lambda g, m: (g, 0, 0))
    return pl.pallas_call(
        body, name="attn_bwd", grid=(B_HEADS // 2, s // QBLK),
        in_specs=[qblk, kblk, kblk, bblk, qblk],
        out_specs=[qblk, kblk, kblk, bblk],
        out_shape=[jax.ShapeDtypeStruct((s, D_MODEL), F32), jax.ShapeDtypeStruct((LEFT + s, D_MODEL), F32),
                   jax.ShapeDtypeStruct((LEFT + s, D_MODEL), F32),
                   jax.ShapeDtypeStruct((B_HEADS, QBLK, KBLK), F32)],
        compiler_params=_params(("parallel", "arbitrary")),
    )(q, kpad, vpad, bias, do)


def _adamw(w, g, m, v):
    r, c = w.shape
    tr = r
    for cand in (512, 256, 128, 64, 32, 16, 8):
        if r % cand == 0 and cand * c * 4 <= 2 * 1024 * 1024:
            tr = cand
            break
    c1 = 1.0 / (1.0 - ADAM_B1 ** ADAM_STEP)
    c2 = 1.0 / (1.0 - ADAM_B2 ** ADAM_STEP)

    def body(w_ref, g_ref, m_ref, v_ref, d_ref, mo_ref, vo_ref):
        gv = g_ref[...]
        mn = ADAM_B1 * m_ref[...] + (1.0 - ADAM_B1) * gv
        vn = ADAM_B2 * v_ref[...] + (1.0 - ADAM_B2) * (gv * gv)
        mo_ref[...] = mn
        vo_ref[...] = vn
        d_ref[...] = -ADAM_LR * ((mn * c1) / (jnp.sqrt(vn * c2) + ADAM_EPS) + ADAM_WD * w_ref[...])

    blk = pl.BlockSpec((tr, c), lambda i: (i, 0))
    sh = jax.ShapeDtypeStruct((r, c), F32)
    return pl.pallas_call(
        body, name="adamw", grid=(r // tr,),
        in_specs=[blk] * 4, out_specs=[blk] * 3, out_shape=[sh] * 3,
        compiler_params=_params(("parallel",)),
    )(w, g, m, v)


def _row(v, width=None):
    v = v.reshape(1, -1)
    if width is not None and v.shape[1] < width:
        v = jnp.pad(v, ((0, 0), (0, width - v.shape[1])))
    return v


def _gate_row(v):
    return jnp.pad(v.reshape(1, A_HEADS), ((0, 0), (A_HEADS, LANE - 2 * A_HEADS)))


def _a_layer_fwd(h, p):
    xn = _rmsnorm_fwd(h, _row(p["norm"]))
    w_in = p["w_in"]
    w_qkv, w_z = w_in[:, :A_CONV_WIDTH], w_in[:, A_CONV_WIDTH:A_CONV_WIDTH + A_QK]
    w_ba = jnp.pad(w_in[:, A_CONV_WIDTH + A_QK:], ((0, 0), (0, LANE - 2 * A_HEADS)))
    pre = _mm(xn, w_qkv, name="a_qkv")
    z = _mm(xn, w_z, name="a_z")
    ba = _mm(xn, w_ba, name="a_ba")
    act = _conv_silu_fwd(pre, p["conv"])
    alog, dtb, wn = _gate_row(p["A_log"]), _gate_row(p["dt_bias"]), _row(p["out_norm"])
    y, o_raw, states, t_all, w_all, vn_all = _gdn_fwd(act, ba, z, alog, dtb, wn)
    h2 = _mm(y, p["w_out"], res=h, name="a_out")
    saved = dict(h=h, xn=xn, pre=pre, z=z, ba=ba, act=act, o_raw=o_raw, y=y, states=states,
                 t_all=t_all, w_all=w_all, vn_all=vn_all,
                 w_qkv=w_qkv, w_z=w_z, w_ba=w_ba, alog=alog, dtb=dtb, wn=wn)
    return h2, saved


def _a_layer_bwd(dh2, p, sv):
    g = {}
    g["w_out"] = _mm(sv["y"], dh2, "tn", name="a_out_dw")
    dy = _mm(dh2, p["w_out"], "nt", name="a_out_dx")
    dact, dba, dz, dalog, ddtb, dwn = _gdn_bwd(sv["act"], sv["ba"], sv["z"], sv["o_raw"], dy, sv["states"],
                                               sv["t_all"], sv["w_all"], sv["vn_all"],
                                               sv["alog"], sv["dtb"], sv["wn"])
    dyc, g["conv"] = _conv_silu_bwd(sv["pre"], p["conv"], dact)
    dpre = _conv_transpose(dyc, p["conv"])
    xn = sv["xn"]
    g["w_in"] = jnp.concatenate([_mm(xn, dpre, "tn", name="a_qkv_dw"), _mm(xn, dz, "tn", name="a_z_dw"),
                                 _mm(xn, dba, "tn", name="a_ba_dw")[:, :2 * A_HEADS]], axis=1)
    dxn = _mm(dpre, sv["w_qkv"], "nt", name="a_qkv_dx")
    dxn = _mm(dz, sv["w_z"], "nt", res=dxn, name="a_z_dx")
    dxn = _mm(dba, sv["w_ba"], "nt", res=dxn, name="a_ba_dx")
    dh, dnorm = _rmsnorm_bwd(sv["h"], _row(p["norm"]), dxn, dh2)
    g["norm"] = dnorm[0]
    g["A_log"] = dalog[0, A_HEADS:2 * A_HEADS]
    g["dt_bias"] = ddtb[0, A_HEADS:2 * A_HEADS]
    g["out_norm"] = dwn[0]
    return dh, g


def _ffn_fwd(h, p):
    xn = _rmsnorm_fwd(h, _row(p["norm"]))
    w_g, w_v = p["w_up"][:, :FFN_DIM], p["w_up"][:, FFN_DIM:]
    cg, cv = p["conv"][:, :FFN_DIM], p["conv"][:, FFN_DIM:]
    bg, bv = _row(p["conv_b"][:FFN_DIM]), _row(p["conv_b"][FFN_DIM:])
    pg = _mm(xn, w_g, name="f_gate")
    pv = _mm(xn, w_v, name="f_val")
    act = _ffn_act_fwd(pg, pv, cg, cv, bg, bv)
    h2 = _mm(act, p["w_down"], res=h, name="f_down")
    return h2, dict(h=h, xn=xn, pg=pg, pv=pv, act=act, w_g=w_g, w_v=w_v, cg=cg, cv=cv, bg=bg, bv=bv)


def _ffn_bwd(dh2, p, sv):
    g = {}
    g["w_down"] = _mm(sv["act"], dh2, "tn", name="f_down_dw")
    dact = _mm(dh2, p["w_down"], "nt", name="f_down_dx")
    dyg, dyv, dcg, dcv, dbg, dbv = _ffn_act_bwd(sv["pg"], sv["pv"], sv["cg"], sv["cv"], sv["bg"], sv["bv"], dact)
    dpg = _conv_transpose(dyg, sv["cg"])
    dpv = _conv_transpose(dyv, sv["cv"])
    xn = sv["xn"]
    g["w_up"] = jnp.concatenate([_mm(xn, dpg, "tn", name="f_gate_dw"), _mm(xn, dpv, "tn", name="f_val_dw")], axis=1)
    g["conv"] = jnp.concatenate([dcg, dcv], axis=1)
    g["conv_b"] = jnp.concatenate([dbg[0], dbv[0]])
    dxn = _mm(dpg, sv["w_g"], "nt", name="f_gate_dx")
    dxn = _mm(dpv, sv["w_v"], "nt", res=dxn, name="f_val_dx")
    dh, dnorm = _rmsnorm_bwd(sv["h"], _row(p["norm"]), dxn, dh2)
    g["norm"] = dnorm[0]
    return dh, g


def _b_layer_fwd(h, p, kpad, vpad):
    xn = _rmsnorm_fwd(h, _row(p["norm"]))
    q = _mm(xn, p["w_q"], name="b_q")
    table = jnp.pad(p["rel_bias"], ((0, 0), (0, REL_PAD - p["rel_bias"].shape[1])))
    bias = _relbias_expand(_relbias_ring(table, False).reshape(B_HEADS, 1, REL_RING))
    o = _attn_fwd(q, kpad, vpad, bias)
    h2 = _mm(o, p["w_out"], res=h, name="b_out")
    return h2, dict(h=h, xn=xn, q=q, o=o, bias=bias)


def _b_layer_bwd(dh2, p, sv, kpad, vpad):
    g = {}
    g["w_out"] = _mm(sv["o"], dh2, "tn", name="b_out_dw")
    do = _mm(dh2, p["w_out"], "nt", name="b_out_dx")
    dq, dkp, dvp, dsc = _attn_bwd(sv["q"], kpad, vpad, sv["bias"], do)
    dring = _relbias_reduce(dsc).reshape(B_HEADS, REL_RING)
    g["rel_bias"] = _relbias_ring(dring, True)[:, :2 * REL_CLIP + 1]
    g["w_q"] = _mm(sv["xn"], dq, "tn", name="b_q_dw")
    dxn = _mm(dq, p["w_q"], "nt", name="b_q_dx")
    dh, dnorm = _rmsnorm_bwd(sv["h"], _row(p["norm"]), dxn, dh2)
    g["norm"] = dnorm[0]
    return dh, g, dkp, dvp


def _local_step(x, tgt, w):
    n_a = w["a_norm"].shape[0]
    n_b = w["b_norm"].shape[0]
    depth = n_a + n_b

    def a_params(i):
        return dict(norm=w["a_norm"][i], w_in=w["a_w_in"][i], conv=w["a_conv"][i], A_log=w["a_A_log"][i],
                    dt_bias=w["a_dt_bias"][i], out_norm=w["a_out_norm"][i], w_out=w["a_w_out"][i])

    def b_params(j):
        return dict(norm=w["b_norm"][j], w_q=w["b_w_q"][j], rel_bias=w["b_rel_bias"][j], w_out=w["b_w_out"][j])

    def f_params(l):
        return dict(norm=w["f_norm"][l], w_up=w["f_w_up"][l], conv=w["f_conv"][l], conv_b=w["f_conv_b"][l],
                    w_down=w["f_w_down"][l])

    h = x
    saved = []
    kv_saved = None
    kpad = vpad = None
    for layer in range(depth):
        if layer < n_a:
            h, sm = _a_layer_fwd(h, a_params(layer))
        else:
            if layer == n_a:
                xn_kv = _rmsnorm_fwd(h, _row(w["kv_norm"]))
                kv = _mm(xn_kv, w["w_kv"], out_dtype=BF16, name="kv")
                kpad = jnp.pad(kv[:, :D_MODEL], ((LEFT, 0), (0, 0)))
                vpad = jnp.pad(kv[:, D_MODEL:], ((LEFT, 0), (0, 0)))
                kv_saved = dict(h=h, xn=xn_kv)
            h, sm = _b_layer_fwd(h, b_params(layer - n_a), kpad, vpad)
        h, sf = _ffn_fwd(h, f_params(layer))
        saved.append((sm, sf))

    loss, dh, dfinal = _final_loss(h, _row(w["final_norm"]), tgt)

    ga = [None] * n_a
    gb = [None] * n_b
    gf = [None] * depth
    dk_tot = dv_tot = None
    g_kv = g_kvn = None
    for layer in reversed(range(depth)):
        sm, sf = saved[layer]
        dh, gf[layer] = _ffn_bwd(dh, f_params(layer), sf)
        if layer >= n_a:
            dh, gb[layer - n_a], dkp, dvp = _b_layer_bwd(dh, b_params(layer - n_a), sm, kpad, vpad)
            dk_tot = dkp if dk_tot is None else dk_tot + dkp
            dv_tot = dvp if dv_tot is None else dv_tot + dvp
            if layer == n_a:
                dkv = jnp.concatenate([dk_tot[LEFT:], dv_tot[LEFT:]], axis=1)
                g_kv = _mm(kv_saved["xn"], dkv, "tn", name="kv_dw")
                dxn = _mm(dkv, w["w_kv"], "nt", name="kv_dx")
                dh, g_kvn = _rmsnorm_bwd(kv_saved["h"], _row(w["kv_norm"]), dxn, dh)
        else:
            dh, ga[layer] = _a_layer_bwd(dh, a_params(layer), sm)

    def stack(gs, key):
        return jnp.stack([g[key] for g in gs])

    grads = dict(
        a_norm=stack(ga, "norm"), a_w_in=stack(ga, "w_in"), a_conv=stack(ga, "conv"), a_A_log=stack(ga, "A_log"),
        a_dt_bias=stack(ga, "dt_bias"), a_out_norm=stack(ga, "out_norm"), a_w_out=stack(ga, "w_out"),
        kv_norm=g_kvn[0], w_kv=g_kv,
        b_norm=stack(gb, "norm"), b_w_q=stack(gb, "w_q"), b_rel_bias=stack(gb, "rel_bias"), b_w_out=stack(gb, "w_out"),
        f_norm=stack(gf, "norm"), f_w_up=stack(gf, "w_up"), f_conv=stack(gf, "conv"), f_conv_b=stack(gf, "conv_b"),
        f_w_down=stack(gf, "w_down"), final_norm=dfinal[0])
    return loss, dh, grads


HBM_SPEC = pl.BlockSpec(memory_space=pl.ANY)
VMEM_SPEC = pl.BlockSpec(memory_space=pltpu.VMEM)
PACK_COLS = 1024
PACK_ROW_QUANTUM = 1280


def _place():
    x, y, c = lax.axis_index("x"), lax.axis_index("y"), lax.axis_index("c")
    chips = [(1 - x, y), (x, 1 - y), (1 - x, 1 - y)]
    return x, y, c, chips


def _remote(src, dst, send_sem, recv_sem, to):
    return pltpu.make_async_remote_copy(src_ref=src, dst_ref=dst, send_sem=send_sem, recv_sem=recv_sem,
                                        device_id=to, device_id_type=MESH)


def _allgather_weights(shard):
    r, cols = shard.shape
    rh = r // 2

    def body(x_ref, out_ref, send_sems, recv_sems):
        x, y, c, chips = _place()
        sibling = (x, y, 1 - c)

        def half(px, py, hc):
            return out_ref.at[2 * px + py, pl.ds(hc * rh, rh), :]

        first =[_remote(x_ref.at[pl.ds(c * rh, rh), :], half(x, y, c), send_sems.at[j], recv_sems.at[j], (*chip, c))
                 for j, chip in enumerate(chips)]
        for cp in first:
            cp.start()
        passed = [_remote(half(*chip, c), half(*chip, c), send_sems.at[3 + j], recv_sems.at[3 + j], sibling)
                  for j, chip in enumerate(chips)]
        for j, chip in enumerate(chips):
            _remote(half(*chip, c), half(*chip, c), send_sems.at[j], recv_sems.at[j], (*chip, c)).wait_recv()
            passed[j].start()
        for j, chip in enumerate(chips):
            _remote(half(*chip, 1 - c), half(*chip, 1 - c), send_sems.at[3 + j], recv_sems.at[3 + j],
                    sibling).wait_recv()
        for cp in first + passed:
            cp.wait_send()

    return pl.pallas_call(
        body, name="allgather_weights",
        out_shape=jax.ShapeDtypeStruct((4, r, cols), shard.dtype),
        in_specs=[HBM_SPEC], out_specs=HBM_SPEC,
        scratch_shapes=[pltpu.SemaphoreType.DMA((6,)), pltpu.SemaphoreType.DMA((6,))],
    )(shard)


def _pair_exchange(g):
    n, r, cols = g.shape
    rh = r // 2

    def body(g_ref, out_ref, send_sem, recv_sem):
        x, y, c, _ = _place()
        cp = _remote(g_ref.at[:, pl.ds((1 - c) * rh, rh), :], out_ref, send_sem, recv_sem, (x, y, 1 - c))
        cp.start()
        cp.wait()

    return pl.pallas_call(
        body, name="rs_pair_exchange",
        out_shape=jax.ShapeDtypeStruct((n, rh, cols), g.dtype),
        in_specs=[HBM_SPEC], out_specs=HBM_SPEC,
        scratch_shapes=[pltpu.SemaphoreType.DMA, pltpu.SemaphoreType.DMA],
    )(g)


def _pair_add(g, other, c_idx):
    n, r, cols = g.shape
    rh = r // 2
    tr = PACK_ROW_QUANTUM // 2
    nb = rh // tr

    def body(c_ref, a_ref, b_ref, o_ref, ob_ref):
        sm = a_ref[...] + b_ref[...]
        o_ref[...] = sm
        ob_ref[...] = sm.astype(BF16)

    out_blk = pl.BlockSpec((1, tr, cols), lambda s, i, c_ref: (s, i, 0))
    return pl.pallas_call(
        body, name="rs_pair_add",
        grid_spec=pltpu.PrefetchScalarGridSpec(
            num_scalar_prefetch=1, grid=(n, nb),
            in_specs=[pl.BlockSpec((1, tr, cols), lambda s, i, c_ref: (s, c_ref[0] * nb + i, 0)), out_blk],
            out_specs=[out_blk, out_blk]),
        out_shape=[jax.ShapeDtypeStruct((n, rh, cols), F32), jax.ShapeDtypeStruct((n, rh, cols), BF16)],
        compiler_params=_params(("parallel", "parallel")),
    )(c_idx, g, other)


def _chip_exchange(p):
    n, rh, cols = p.shape

    def body(p_ref, out_ref, send_sems, recv_sems):
        x, y, c, chips = _place()
        cps = [_remote(p_ref.at[2 * chip[0] + chip[1]], out_ref.at[j], send_sems.at[j], recv_sems.at[j], (*chip, c))
               for j, chip in enumerate(chips)]
        for cp in cps:
            cp.start()
        for cp in cps:
            cp.wait()

    return pl.pallas_call(
        body, name="rs_chip_exchange",
        out_shape=jax.ShapeDtypeStruct((3, rh, cols), p.dtype),
        in_specs=[HBM_SPEC], out_specs=HBM_SPEC,
        scratch_shapes=[pltpu.SemaphoreType.DMA((3,)), pltpu.SemaphoreType.DMA((3,))],
    )(p)


def _chip_add(p, recv, chip_idx):
    n, rh, cols = p.shape
    tr = PACK_ROW_QUANTUM // 2

    def body(s_ref, own_ref, r_ref, o_ref):
        o_ref[...] = ((own_ref[0] + r_ref[0].astype(F32)) + r_ref[1].astype(F32)) + r_ref[2].astype(F32)

    return pl.pallas_call(
        body, name="rs_chip_add",
        grid_spec=pltpu.PrefetchScalarGridSpec(
            num_scalar_prefetch=1, grid=(rh // tr,),
            in_specs=[pl.BlockSpec((1, tr, cols), lambda i, s_ref: (s_ref[0], i, 0)),
                      pl.BlockSpec((3, tr, cols), lambda i, s_ref: (0, i, 0))],
            out_specs=pl.BlockSpec((tr, cols), lambda i, s_ref: (i, 0))),
        out_shape=jax.ShapeDtypeStruct((rh, cols), p.dtype),
        compiler_params=_params(("parallel",)),
    )(chip_idx, p, recv)


def _pair_gather(f):
    rh, cols = f.shape

    def body(f_ref, out_ref, send_sem, recv_sem):
        x, y, c, _ = _place()
        cp = _remote(f_ref, out_ref, send_sem, recv_sem, (x, y, 1 - c))
        cp.start()
        cp.wait()

    return pl.pallas_call(
        body, name="rs_pair_gather",
        out_shape=jax.ShapeDtypeStruct((rh, cols), f.dtype),
        in_specs=[HBM_SPEC], out_specs=HBM_SPEC,
        scratch_shapes=[pltpu.SemaphoreType.DMA, pltpu.SemaphoreType.DMA],
    )(f)


def _allreduce_small(v):
    r, cols = v.shape

    def body(x_ref, out_ref, slots, send_sems, recv_sems):
        x, y, c, _ = _place()
        bits = [(bx, by, bc) for bx in (0, 1) for by in (0, 1) for bc in (0, 1)]

        def flip(b):
            return (1 - x if b[0] else x, 1 - y if b[1] else y, 1 - c if b[2] else c)

        slots[0] = x_ref[...]
        cps = [_remote(x_ref, slots.at[k], send_sems.at[k - 1], recv_sems.at[k - 1], flip(bits[k]))
               for k in range(1, 8)]
        for cp in cps:
            cp.start()
        for cp in cps:
            cp.wait()
        acc = None
        for b in bits:
            fx, fy, fc = flip(b)
            term = slots[4 * fx + 2 * fy + fc]
            acc = term if acc is None else acc + term
        out_ref[...] = acc

    return pl.pallas_call(
        body, name="allreduce_small",
        out_shape=jax.ShapeDtypeStruct((r, cols), v.dtype),
        in_specs=[VMEM_SPEC], out_specs=VMEM_SPEC,
        scratch_shapes=[pltpu.VMEM((8, r, cols), v.dtype), pltpu.SemaphoreType.DMA((7,)),
                        pltpu.SemaphoreType.DMA((7,))],
        compiler_params=pltpu.CompilerParams(vmem_limit_bytes=VMEM_LIMIT),
    )(v)


BIG = (("a_w_in", 2), ("a_w_out", 1), ("w_kv", 1), ("b_w_q", 1), ("b_w_out", 1), ("f_w_up", 2), ("f_w_down", 1))
SMALL = (("a_norm", 1), ("a_conv", 2), ("a_A_log", None), ("a_dt_bias", None), ("a_out_norm", None),
         ("kv_norm", None), ("b_norm", None), ("b_rel_bias", None), ("f_norm", None), ("f_conv", 2),
         ("f_conv_b", None), ("final_norm", None))
WEIGHT_ORDER = ("a_norm", "a_w_in", "a_conv", "a_A_log", "a_dt_bias", "a_out_norm", "a_w_out", "kv_norm", "w_kv",
                "b_norm", "b_w_q", "b_rel_bias", "b_w_out", "f_norm", "f_w_up", "f_conv", "f_conv_b", "f_w_down",
                "final_norm")


def _pad_rows(flat, cols, quantum):
    n = flat.shape[-1]
    rows = -(-n // (cols * quantum)) * quantum
    pad = [(0, 0)] * (flat.ndim - 1) + [(0, rows * cols - n)]
    return jnp.pad(flat, pad).reshape(flat.shape[:-1] + (rows, cols))


def _pack_shards(shards, dtype):
    return _pad_rows(jnp.concatenate([shards[n].astype(dtype).reshape(-1) for n, _ in BIG]), PACK_COLS,
                     PACK_ROW_QUANTUM)


def _unpack_gathered(gathered, shard_shapes):
    flat = gathered.reshape(4, -1)
    out, off = {}, 0
    for name, axis in BIG:
        shp = shard_shapes[name]
        n = math.prod(shp)
        seg = flat[:, off:off + n].reshape((4,) + shp)
        out[name] = jnp.concatenate([seg[s] for s in range(4)], axis=axis)
        off += n
    return out


def _pack_grads(grads):
    parts = []
    for name, axis in BIG:
        g = grads[name]
        parts.append(jnp.stack([piece.reshape(-1) for piece in jnp.split(g, 4, axis=axis)]))
    return _pad_rows(jnp.concatenate(parts, axis=1), PACK_COLS, PACK_ROW_QUANTUM)


def _unpack_reduced(reduced, shard_shapes):
    flat = reduced.reshape(-1)
    out, off = {}, 0
    for name, _ in BIG:
        shp = shard_shapes[name]
        n = math.prod(shp)
        out[name] = flat[off:off + n].reshape(shp)
        off += n
    return out


def _pack_small(values, names):
    return _pad_rows(jnp.concatenate([values[n].reshape(-1) for n in names]), LANE, SUB)


def _unpack_small(packed, shapes, names):
    flat = packed.reshape(-1)
    out, off = {}, 0
    for n in names:
        size = math.prod(shapes[n])
        out[n] = flat[off:off + size].reshape(shapes[n])
        off += size
    return out


def _adamw_nd(w, g, m, v):
    shp = w.shape
    two = (math.prod(shp[:-1]), shp[-1])
    d, mn, vn = _adamw(w.reshape(two), g.reshape(two), m.reshape(two), v.reshape(two))
    return d.reshape(shp), mn.reshape(shp), vn.reshape(shp)


def kernel(x, a_norm, a_w_in, a_conv, a_A_log, a_dt_bias, a_out_norm, a_w_out, kv_norm, w_kv, b_norm, b_w_q, b_rel_bias, b_w_out, f_norm, f_w_up, f_conv, f_conv_b, f_w_down, final_norm, loss_target, m_a_norm, m_a_w_in, m_a_conv, m_a_A_log, m_a_dt_bias, m_a_out_norm, m_a_w_out, m_kv_norm, m_w_kv, m_b_norm, m_b_w_q, m_b_rel_bias, m_b_w_out, m_f_norm, m_f_w_up, m_f_conv, m_f_conv_b, m_f_w_down, m_final_norm, v_a_norm, v_a_w_in, v_a_conv, v_a_A_log, v_a_dt_bias, v_a_out_norm, v_a_w_out, v_kv_norm, v_w_kv, v_b_norm, v_b_w_q, v_b_rel_bias, v_b_w_out, v_f_norm, v_f_w_up, v_f_conv, v_f_conv_b, v_f_w_down, v_final_norm):
    w = dict(a_norm=a_norm, a_w_in=a_w_in, a_conv=a_conv, a_A_log=a_A_log, a_dt_bias=a_dt_bias,
             a_out_norm=a_out_norm, a_w_out=a_w_out, kv_norm=kv_norm, w_kv=w_kv, b_norm=b_norm, b_w_q=b_w_q,
             b_rel_bias=b_rel_bias, b_w_out=b_w_out, f_norm=f_norm, f_w_up=f_w_up, f_conv=f_conv,
             f_conv_b=f_conv_b, f_w_down=f_w_down, final_norm=final_norm)
    m = dict(a_norm=m_a_norm, a_w_in=m_a_w_in, a_conv=m_a_conv, a_A_log=m_a_A_log, a_dt_bias=m_a_dt_bias,
             a_out_norm=m_a_out_norm, a_w_out=m_a_w_out, kv_norm=m_kv_norm, w_kv=m_w_kv, b_norm=m_b_norm,
             b_w_q=m_b_w_q, b_rel_bias=m_b_rel_bias, b_w_out=m_b_w_out, f_norm=m_f_norm, f_w_up=m_f_w_up,
             f_conv=m_f_conv, f_conv_b=m_f_conv_b, f_w_down=m_f_w_down, final_norm=m_final_norm)
    v = dict(a_norm=v_a_norm, a_w_in=v_a_w_in, a_conv=v_a_conv, a_A_log=v_a_A_log, a_dt_bias=v_a_dt_bias,
             a_out_norm=v_a_out_norm, a_w_out=v_a_w_out, kv_norm=v_kv_norm, w_kv=v_w_kv, b_norm=v_b_norm,
             b_w_q=v_b_w_q, b_rel_bias=v_b_rel_bias, b_w_out=v_b_w_out, f_norm=v_f_norm, f_w_up=v_f_w_up,
             f_conv=v_f_conv, f_conv_b=v_f_conv_b, f_w_down=v_f_w_down, final_norm=v_final_norm)
    xi, yi, ci = lax.axis_index("x"), lax.axis_index("y"), lax.axis_index("c")
    chip = 2 * xi + yi
    shard_shapes = {n: w[n].shape for n in WEIGHT_ORDER}

    my_shard = _pack_shards(w, BF16)
    gathered = lax.dynamic_update_slice(_allgather_weights(my_shard), my_shard[None], (chip, 0, 0))
    full = _unpack_gathered(gathered, shard_shapes)
    sharded_small = [n for n, axis in SMALL if axis is not None]
    placed = {}
    for n, axis in SMALL:
        if axis is not None:
            wide = list(w[n].shape)
            wide[axis] *= 4
            mine_once = w[n] * (1 - ci).astype(F32)
            placed[n] = lax.dynamic_update_slice_in_dim(jnp.zeros(wide, F32), mine_once, chip * w[n].shape[axis], axis)
    placed_shapes = {n: placed[n].shape for n in sharded_small}
    full.update(_unpack_small(_allreduce_small(_pack_small(placed, sharded_small)), placed_shapes, sharded_small))
    for n, axis in SMALL:
        if axis is None:
            full[n] = w[n]

    loss_part, grad_x, grads = _local_step(x[0], loss_target[0], full)

    packed = _pack_grads(grads)
    c_idx = jnp.reshape(ci, (1,)).astype(jnp.int32)
    chip_idx = jnp.reshape(chip, (1,)).astype(jnp.int32)
    pair, pair_bf16 = _pair_add(packed, _pair_exchange(packed), c_idx)
    mine = _chip_add(pair, _chip_exchange(pair_bf16), chip_idx)
    theirs = _pair_gather(mine)
    both = jnp.concatenate([jnp.where(ci == 0, mine, theirs), jnp.where(ci == 0, theirs, mine)], axis=0)
    red = _unpack_reduced(both, shard_shapes)

    small_names = [n for n, _ in SMALL]
    small_vals = {n: grads[n] for n in small_names}
    small_vals["loss"] = loss_part[0, :1]
    names = ["loss"] + small_names
    shapes = {n: small_vals[n].shape for n in names}
    summed = _unpack_small(_allreduce_small(_pack_small(small_vals, names)), shapes, names)
    loss = summed["loss"][0]
    for n, axis in SMALL:
        g = summed[n]
        if axis is not None:
            g = lax.dynamic_slice_in_dim(g, chip * w[n].shape[axis], w[n].shape[axis], axis)
        red[n] = g

    delta, new_m, new_v = {}, {}, {}
    for n, _ in BIG:
        delta[n], new_m[n], new_v[n] = _adamw_nd(w[n], red[n], m[n], v[n])
    local_shapes = {n: w[n].shape for n in small_names}
    packs = [_pack_small(t, small_names) for t in (w, red, m, v)]
    outs = _adamw(*packs)
    ds, ms, vs = (_unpack_small(o, local_shapes, small_names) for o in outs)
    delta.update(ds)
    new_m.update(ms)
    new_v.update(vs)

    return (loss, grad_x[None], *[red[n] for n in WEIGHT_ORDER], *[delta[n] for n in WEIGHT_ORDER],
            *[new_m[n] for n in WEIGHT_ORDER], *[new_v[n] for n in WEIGHT_ORDER])
```

```python
import math

import jax
import jax.numpy as jnp
from jax import lax
from jax.experimental import pallas as pl
from jax.experimental.pallas import tpu as pltpu

F32 = jnp.float32
BF16 = jnp.bfloat16
HIGHEST = lax.Precision.HIGHEST
MESH = pl.DeviceIdType.MESH

D_MODEL = 1024
CHUNK = 64
A_HEADS = 8
A_HEAD = 128
A_QK = A_HEADS * A_HEAD
A_CONV_WIDTH = 3 * A_QK
B_HEADS = 16
B_HEAD = 64
LEFT = 8 * CHUNK
QBLK = 4 * CHUNK
KBLK = LEFT + QBLK
REL_CLIP = 256
REL_PAD = 640
FFN_DIM = 2816
EPS = 1e-6
NEG_INF = -1e30
LANE = 128
SUB = 8
VMEM_LIMIT = 56 * 1024 * 1024

ADAM_LR = 0.001
ADAM_B1 = 0.9
ADAM_B2 = 0.999
ADAM_EPS = 1e-08
ADAM_WD = 0.01
ADAM_STEP = 10


VMEM_FULL = pl.BlockSpec(memory_space=pltpu.VMEM)


def _params(sem=None):
    return pltpu.CompilerParams(dimension_semantics=sem, vmem_limit_bytes=VMEM_LIMIT)


def _tile(n, cap):
    if n <= cap:
        return n
    best = None
    for t in range(LANE, cap + 1, LANE):
        if n % t == 0:
            best = t
    assert best is not None, n
    return best


def _sigmoid(x):
    return 1.0 / (1.0 + jnp.exp(-x))


def _softplus(x):
    return jnp.maximum(x, 0.0) + jnp.log(1.0 + jnp.exp(-jnp.abs(x)))


def _dot(a, b, dims, prec=None):
    return lax.dot_general(a, b, (dims, ((), ())), preferred_element_type=F32, precision=prec)


NN = ((1,), (0,))
NT = ((1,), (1,))
TN = ((0,), (0,))


def _bdot(a, b, dims):
    return _dot(a.astype(BF16), b.astype(BF16), dims)


def _mm(a, b, mode="nn", out_dtype=F32, res=None, name="mm"):
    if mode == "nn":
        (m, k), (k2, n) = a.shape, b.shape
    elif mode == "nt":
        (m, k), (n, k2) = a.shape, b.shape
    else:
        (k, m), (k2, n) = a.shape, b.shape
    assert k == k2, (a.shape, b.shape, mode)
    tm, tn, tk = _tile(m, 1408), _tile(n, 1408), _tile(k, 1408)
    if m == 8192:
        tm = 1024
    if k == 8192:
        tk = 1024
    nk = k // tk
    dims = {"nn": NN, "nt": NT, "tn": TN}[mode]
    a_spec = {"nn": pl.BlockSpec((tm, tk), lambda i, j, kk: (i, kk)),
              "nt": pl.BlockSpec((tm, tk), lambda i, j, kk: (i, kk)),
              "tn": pl.BlockSpec((tk, tm), lambda i, j, kk: (kk, i))}[mode]
    b_spec = {"nn": pl.BlockSpec((tk, tn), lambda i, j, kk: (kk, j)),
              "nt": pl.BlockSpec((tn, tk), lambda i, j, kk: (j, kk)),
              "tn": pl.BlockSpec((tk, tn), lambda i, j, kk: (kk, j))}[mode]
    o_spec = pl.BlockSpec((tm, tn), lambda i, j, kk: (i, j))
    has_res = res is not None

    def body(a_ref, b_ref, *rest):
        if has_res:
            res_ref, o_ref, acc = rest
        else:
            o_ref, acc = rest
        kk = pl.program_id(2)

        @pl.when(kk == 0)
        def _():
            acc[...] = jnp.zeros_like(acc)

        acc[...] += _bdot(a_ref[...], b_ref[...], dims)

        @pl.when(kk == nk - 1)
        def _():
            r = acc[...]
            if has_res:
                r = r + res_ref[...]
            o_ref[...] = r.astype(out_dtype)

    args = [a, b] + ([res] if has_res else [])
    in_specs = [a_spec, b_spec] + ([o_spec] if has_res else [])
    return pl.pallas_call(
        body, name=name, grid=(m // tm, n // tn, nk),
        in_specs=in_specs, out_specs=o_spec,
        out_shape=jax.ShapeDtypeStruct((m, n), out_dtype),
        scratch_shapes=[pltpu.VMEM((tm, tn), F32)],
        compiler_params=_params(("parallel", "parallel", "arbitrary")),
    )(*args)


ROW_TILE = 1024
N_SHARDS = 4


def _mm_call(name, a, b, dims, grid, a_spec, b_spec, o_spec, out_shape, out_dtype, acc_shape, res=None):
    nk = grid[2]
    has_res = res is not None

    def flat(v):
        return v.reshape(-1, v.shape[-1]) if v.ndim == 3 else v

    def body(a_ref, b_ref, *rest):
        if has_res:
            res_ref, o_ref, acc = rest
        else:
            o_ref, acc = rest
        kk = pl.program_id(2)

        @pl.when(kk == 0)
        def _():
            acc[...] = jnp.zeros_like(acc)

        acc[...] += _bdot(flat(a_ref[...]), flat(b_ref[...]), dims)

        @pl.when(kk == nk - 1)
        def _():
            r = acc[...]
            if has_res:
                r = r + res_ref[...]
            o_ref[...] = r.reshape(o_ref.shape).astype(out_dtype)

    args = [a, b] + ([res] if has_res else [])
    in_specs = [a_spec, b_spec] + ([o_spec] if has_res else [])
    return pl.pallas_call(
        body, name=name, grid=grid, in_specs=in_specs, out_specs=o_spec,
        out_shape=jax.ShapeDtypeStruct(out_shape, out_dtype),
        scratch_shapes=[pltpu.VMEM(acc_shape, F32)],
        compiler_params=_params(("parallel", "parallel", "arbitrary")),
    )(*args)


def _shards_per_block(rows):
    return N_SHARDS if N_SHARDS * rows <= 1408 else 2


def _mm_rowsh(a, buf, rows, blk0, mode, name, res=None, out_dtype=F32):
    s = a.shape[0]
    cols = buf.shape[2]
    g = _shards_per_block(rows)
    tm = _tile(s, ROW_TILE)
    b_blk = (g, rows, cols)
    if mode == "nn":
        return _mm_call(name, a, buf, NN, (s // tm, 1, N_SHARDS // g),
                        pl.BlockSpec((tm, g * rows), lambda i, j, kk: (i, kk)),
                        pl.BlockSpec(b_blk, lambda i, j, kk: (kk, blk0, 0)),
                        pl.BlockSpec((tm, cols), lambda i, j, kk: (i, 0)),
                        (s, cols), out_dtype, (tm, cols), res)
    return _mm_call(name, a, buf, NT, (s // tm, N_SHARDS // g, 1),
                    pl.BlockSpec((tm, cols), lambda i, j, kk: (i, 0)),
                    pl.BlockSpec(b_blk, lambda i, j, kk: (j, blk0, 0)),
                    pl.BlockSpec((tm, g * rows), lambda i, j, kk: (i, j)),
                    (s, N_SHARDS * rows), out_dtype, (tm, g * rows), res)


def _mm_rowsh_dw(act, dy, rows, name):
    s = act.shape[0]
    cols = dy.shape[1]
    g = _shards_per_block(rows)
    ts = _tile(s, ROW_TILE)
    return _mm_call(name, act, dy, TN, (1, N_SHARDS // g, s // ts),
                    pl.BlockSpec((ts, g * rows), lambda i, j, kk: (kk, j)),
                    pl.BlockSpec((ts, cols), lambda i, j, kk: (kk, 0)),
                    pl.BlockSpec((g, rows, cols), lambda i, j, kk: (j, 0, 0)),
                    (N_SHARDS, rows, cols), F32, (g * rows, cols))


def _mm_colsh(a, buf, krows, blk0, mode, name, flat=False, res=None, out_dtype=F32):
    cols = buf.shape[2]
    b_nn = pl.BlockSpec((None, krows, cols), lambda i, j, kk: (j, blk0, 0))
    b_nt = pl.BlockSpec((None, krows, cols), lambda i, j, kk: (kk, blk0, 0))
    if mode == "nn":
        s = a.shape[0]
        tm = _tile(s, ROW_TILE)
        o_spec = (pl.BlockSpec((tm, cols), lambda i, j, kk: (i, j)) if flat
                  else pl.BlockSpec((None, tm, cols), lambda i, j, kk: (j, i, 0)))
        return _mm_call(name, a, buf, NN, (s // tm, N_SHARDS, 1),
                        pl.BlockSpec((tm, krows), lambda i, j, kk: (i, 0)), b_nn, o_spec,
                        (s, N_SHARDS * cols) if flat else (N_SHARDS, s, cols), out_dtype, (tm, cols), res)
    s = a.shape[0] if flat else a.shape[1]
    tm = _tile(s, ROW_TILE)
    a_spec = (pl.BlockSpec((tm, cols), lambda i, j, kk: (i, kk)) if flat
              else pl.BlockSpec((None, tm, cols), lambda i, j, kk: (kk, i, 0)))
    return _mm_call(name, a, buf, NT, (s // tm, 1, N_SHARDS), a_spec, b_nt,
                    pl.BlockSpec((tm, krows), lambda i, j, kk: (i, 0)),
                    (s, krows), out_dtype, (tm, krows), res)


def _mm_colsh_dw(x, dy, name, flat=False):
    s, k = x.shape
    cols = dy.shape[1] // N_SHARDS if flat else dy.shape[2]
    ts = _tile(s, ROW_TILE)
    b_spec = (pl.BlockSpec((ts, cols), lambda i, j, kk: (kk, j)) if flat
              else pl.BlockSpec((None, ts, cols), lambda i, j, kk: (j, kk, 0)))
    return _mm_call(name, x, dy, TN, (1, N_SHARDS, s // ts),
                    pl.BlockSpec((ts, k), lambda i, j, kk: (kk, 0)), b_spec,
                    pl.BlockSpec((None, k, cols), lambda i, j, kk: (j, 0, 0)),
                    (N_SHARDS, k, cols), F32, (k, cols))


def _rmsnorm_fwd(x, g):
    s, d = x.shape
    tr = _tile(s, 1024)

    def body(x_ref, g_ref, o_ref):
        xv = x_ref[...]
        r = lax.rsqrt(jnp.mean(xv * xv, axis=-1, keepdims=True) + EPS)
        o_ref[...] = (xv * r * g_ref[...]).astype(BF16)

    return pl.pallas_call(
        body, name="rmsnorm_fwd", grid=(s // tr,),
        in_specs=[pl.BlockSpec((tr, d), lambda i: (i, 0)), pl.BlockSpec((1, d), lambda i: (0, 0))],
        out_specs=pl.BlockSpec((tr, d), lambda i: (i, 0)),
        out_shape=jax.ShapeDtypeStruct((s, d), BF16),
        compiler_params=_params(("parallel",)),
    )(x, g)


def _rmsnorm_bwd(x, g, dxn, dres):
    s, d = x.shape
    tr = _tile(s, 1024)

    def body(x_ref, g_ref, dxn_ref, dres_ref, dx_ref, dg_ref):
        @pl.when(pl.program_id(0) == 0)
        def _():
            dg_ref[...] = jnp.zeros_like(dg_ref)

        xv = x_ref[...]
        r = lax.rsqrt(jnp.mean(xv * xv, axis=-1, keepdims=True) + EPS)
        dy = dxn_ref[...]
        t = dy * g_ref[...]
        c = jnp.mean(t * xv, axis=-1, keepdims=True)
        dx_ref[...] = dres_ref[...] + r * t - xv * (r * r * r) * c
        dg_ref[...] += jnp.sum(dy * xv * r, axis=0, keepdims=True)

    row = pl.BlockSpec((tr, d), lambda i: (i, 0))
    vec = pl.BlockSpec((1, d), lambda i: (0, 0))
    return pl.pallas_call(
        body, name="rmsnorm_bwd", grid=(s // tr,),
        in_specs=[row, vec, row, row], out_specs=[row, vec],
        out_shape=[jax.ShapeDtypeStruct((s, d), F32), jax.ShapeDtypeStruct((1, d), F32)],
        compiler_params=_params(("arbitrary",)),
    )(x, g, dxn, dres)


def _final_loss(h, g, tgt):
    s, d = h.shape
    tr = _tile(s, 1024)

    def body(x_ref, g_ref, t_ref, loss_ref, dx_ref, dg_ref):
        @pl.when(pl.program_id(0) == 0)
        def _():
            dg_ref[...] = jnp.zeros_like(dg_ref)
            loss_ref[...] = jnp.zeros_like(loss_ref)

        xv = x_ref[...]
        r = lax.rsqrt(jnp.mean(xv * xv, axis=-1, keepdims=True) + EPS)
        xh = xv * r
        err = xh * g_ref[...] - t_ref[...]
        per_row = jnp.mean(err * err, axis=-1, keepdims=True)
        loss_ref[...] += 0.5 * jnp.sum(per_row, axis=0, keepdims=True)
        dy = err * (1.0 / d)
        t = dy * g_ref[...]
        c = jnp.mean(t * xv, axis=-1, keepdims=True)
        dx_ref[...] = r * t - xv * (r * r * r) * c
        dg_ref[...] += jnp.sum(dy * xh, axis=0, keepdims=True)

    row = pl.BlockSpec((tr, d), lambda i: (i, 0))
    vec = pl.BlockSpec((1, d), lambda i: (0, 0))
    return pl.pallas_call(
        body, name="final_loss", grid=(s // tr,),
        in_specs=[row, vec, row],
        out_specs=[pl.BlockSpec((1, LANE), lambda i: (0, 0)), row, vec],
        out_shape=[jax.ShapeDtypeStruct((1, LANE), F32), jax.ShapeDtypeStruct((s, d), F32),
                   jax.ShapeDtypeStruct((1, d), F32)],
        compiler_params=_params(("arbitrary",)),
    )(h, g, tgt)


def _shift_rows(cur, edge, k, reverse):
    if k == 0:
        return cur
    tr, tc = cur.shape
    row = lax.broadcasted_iota(jnp.int32, (SUB, tc), 0)
    if not reverse:
        r = pltpu.roll(cur, k, 0)
        e = pltpu.roll(edge, k, 0)
        first = jnp.where(row < k, e, r[:SUB])
        return jnp.concatenate([first, r[SUB:]], axis=0)
    r = pltpu.roll(cur, tr - k, 0)
    e = pltpu.roll(edge, SUB - k, 0)
    last = jnp.where(row >= SUB - k, e, r[tr - SUB:])
    return jnp.concatenate([r[:tr - SUB], last], axis=0)


def _conv_taps(cur, edge, w, reverse):
    width = w.shape[0]
    acc = None
    for k in range(width):
        term = _shift_rows(cur, edge, width - 1 - k, reverse) * w[k:k + 1, :]
        acc = term if acc is None else acc + term
    return acc


def _conv_tiles(s, c):
    return _tile(s, 256), _tile(c, 1408)


def _conv_silu_fwd(pre, w):
    s, c = pre.shape
    width = w.shape[0]
    tr, tc = _conv_tiles(s, c)

    def body(x_ref, w_ref, o_ref, tail):
        @pl.when(pl.program_id(1) == 0)
        def _():
            tail[...] = jnp.zeros_like(tail)

        cur = x_ref[...]
        y = _conv_taps(cur, tail[...], w_ref[...], False)
        o_ref[...] = y * _sigmoid(y)
        tail[...] = cur[tr - SUB:]

    blk = pl.BlockSpec((tr, tc), lambda j, i: (i, j))
    return pl.pallas_call(
        body, name="conv_silu_fwd", grid=(c // tc, s // tr),
        in_specs=[blk, pl.BlockSpec((width, tc), lambda j, i: (0, j))], out_specs=blk,
        out_shape=jax.ShapeDtypeStruct((s, c), F32),
        scratch_shapes=[pltpu.VMEM((SUB, tc), F32)],
        compiler_params=_params(("parallel", "arbitrary")),
    )(pre, w)


def _prev_rows_index(i_blk, tr):
    return jnp.maximum(i_blk * (tr // SUB) - 1, 0)


def _conv_silu_bwd(pre, w, dact):
    s, c = pre.shape
    width = w.shape[0]
    tr, tc = _conv_tiles(s, c)
    nr = s // tr

    def body(x_ref, p_ref, w_ref, d_ref, dx_ref, dw_ref, head):
        @pl.when(pl.program_id(1) == 0)
        def _():
            head[...] = jnp.zeros_like(head)
            dw_ref[...] = jnp.zeros_like(dw_ref)

        cur = x_ref[...]
        edge = jnp.where(pl.program_id(1) == nr - 1, 0.0, p_ref[...])
        y = _conv_taps(cur, edge, w_ref[...], False)
        sg = _sigmoid(y)
        dy = d_ref[...] * sg * (1.0 + y * (1.0 - sg))
        for k in range(width):
            xs = _shift_rows(cur, edge, width - 1 - k, False)
            dw_ref[k:k + 1, :] += jnp.sum(dy * xs, axis=0, keepdims=True)
        dx_ref[...] = _conv_taps(dy, head[...], w_ref[...], True)
        head[...] = dy[:SUB]

    blk = pl.BlockSpec((tr, tc), lambda j, i: (nr - 1 - i, j))
    prev = pl.BlockSpec((SUB, tc), lambda j, i: (_prev_rows_index(nr - 1 - i, tr), j))
    wblk = pl.BlockSpec((width, tc), lambda j, i: (0, j))
    return pl.pallas_call(
        body, name="conv_silu_bwd", grid=(c // tc, nr),
        in_specs=[blk, prev, wblk, blk], out_specs=[blk, wblk],
        out_shape=[jax.ShapeDtypeStruct((s, c), F32), jax.ShapeDtypeStruct((width, c), F32)],
        scratch_shapes=[pltpu.VMEM((SUB, tc), F32)],
        compiler_params=_params(("parallel", "arbitrary")),
    )(pre, pre, w, dact)


def _ffn_act_fwd(pre, w, b):
    _, halves, s, c = pre.shape
    width = w.shape[2]
    tr = _tile(s, 256)

    def body(x_ref, w_ref, b_ref, o_ref, tail):
        @pl.when(pl.program_id(1) == 0)
        def _():
            tail[...] = jnp.zeros_like(tail)

        cg, cv = x_ref[0], x_ref[1]
        yg = _conv_taps(cg, tail[0], w_ref[0], False) + b_ref[0]
        yv = _conv_taps(cv, tail[1], w_ref[1], False) + b_ref[1]
        o_ref[...] = (yg * _sigmoid(yg) * yv).astype(BF16)
        tail[0] = cg[tr - SUB:]
        tail[1] = cv[tr - SUB:]

    return pl.pallas_call(
        body, name="ffn_act_fwd", grid=(halves, s // tr),
        in_specs=[pl.BlockSpec((2, None, tr, c), lambda j, i: (0, j, i, 0)),
                  pl.BlockSpec((2, None, width, c), lambda j, i: (0, j, 0, 0)),
                  pl.BlockSpec((2, None, 1, c), lambda j, i: (0, j, 0, 0))],
        out_specs=pl.BlockSpec((tr, c), lambda j, i: (i, j)),
        out_shape=jax.ShapeDtypeStruct((s, halves * c), BF16),
        scratch_shapes=[pltpu.VMEM((2, SUB, c), F32)],
        compiler_params=_params(("parallel", "arbitrary")),
    )(pre, w, b)


def _ffn_act_bwd(pre, w, b, dact):
    _, halves, s, c = pre.shape
    width = w.shape[2]
    tr = _tile(s, 256)
    nr = s // tr

    def body(x_ref, p_ref, w_ref, b_ref, d_ref, dx_ref, dw_ref, db_ref, head):
        @pl.when(pl.program_id(1) == 0)
        def _():
            for r in (head, dw_ref, db_ref):
                r[...] = jnp.zeros_like(r)

        first = pl.program_id(1) == nr - 1
        cg, cv = x_ref[0], x_ref[1]
        eg, ev = jnp.where(first, 0.0, p_ref[0]), jnp.where(first, 0.0, p_ref[1])
        yg = _conv_taps(cg, eg, w_ref[0], False) + b_ref[0]
        yv = _conv_taps(cv, ev, w_ref[1], False) + b_ref[1]
        sg = _sigmoid(yg)
        da = d_ref[...]
        dyv = da * yg * sg
        dyg = da * yv * sg * (1.0 + yg * (1.0 - sg))
        db_ref[0] += jnp.sum(dyg, axis=0, keepdims=True)
        db_ref[1] += jnp.sum(dyv, axis=0, keepdims=True)
        for k in range(width):
            dw_ref[0, k:k + 1, :] += jnp.sum(dyg * _shift_rows(cg, eg, width - 1 - k, False), axis=0, keepdims=True)
            dw_ref[1, k:k + 1, :] += jnp.sum(dyv * _shift_rows(cv, ev, width - 1 - k, False), axis=0, keepdims=True)
        dx_ref[0] = _conv_taps(dyg, head[0], w_ref[0], True)
        dx_ref[1] = _conv_taps(dyv, head[1], w_ref[1], True)
        head[0] = dyg[:SUB]
        head[1] = dyv[:SUB]

    blk = pl.BlockSpec((2, None, tr, c), lambda j, i: (0, j, nr - 1 - i, 0))
    prev = pl.BlockSpec((2, None, SUB, c), lambda j, i: (0, j, _prev_rows_index(nr - 1 - i, tr), 0))
    wblk = pl.BlockSpec((2, None, width, c), lambda j, i: (0, j, 0, 0))
    bblk = pl.BlockSpec((2, None, 1, c), lambda j, i: (0, j, 0, 0))
    return pl.pallas_call(
        body, name="ffn_act_bwd", grid=(halves, nr),
        in_specs=[blk, prev, wblk, bblk, pl.BlockSpec((tr, c), lambda j, i: (nr - 1 - i, j))],
        out_specs=[blk, wblk, bblk],
        out_shape=[jax.ShapeDtypeStruct(pre.shape, F32), jax.ShapeDtypeStruct(w.shape, F32),
                   jax.ShapeDtypeStruct(b.shape, F32)],
        scratch_shapes=[pltpu.VMEM((2, SUB, c), F32)],
        compiler_params=_params(("parallel", "arbitrary")),
    )(pre, pre, w, b, dact)


def _tri_masks():
    row = lax.broadcasted_iota(jnp.int32, (CHUNK, CHUNK), 0)
    col = lax.broadcasted_iota(jnp.int32, (CHUNK, CHUNK), 1)
    return row, col


def _tri_inv(ms, row, col):
    eye = (row == col).astype(F32)
    same_blk = (row >> 4) == (col >> 4)
    mds = [jnp.where(same_blk, m, 0.0) for m in ms]
    offs = [m - md for m, md in zip(ms, mds)]
    xs = [eye - md for md in mds]
    ps = [_bdot(md, md, NN) for md in mds]
    for _ in range(2):
        rs = [_bdot(jnp.concatenate([x, p], axis=0), p, NN) for x, p in zip(xs, ps)]
        xs = [x + r[:CHUNK] for x, r in zip(xs, rs)]
        ps = [r[CHUNK:] for r in rs]
    xs = [x + _bdot(x, p, NN) for x, p in zip(xs, ps)]
    ps = [_bdot(x, off, NN) for x, off in zip(xs, offs)]
    pps = [_bdot(p, p, NN) for p in ps]
    ys = [eye - p for p in ps]
    ys = [y + _bdot(y, pp, NN) for y, pp in zip(ys, pps)]
    return [_bdot(y, x, NN) for y, x in zip(ys, xs)]


def _gdn_gates(ba, alog, dtb, row, col):
    sig = _sigmoid(ba)
    neg_a = -jnp.exp(alog)
    g = neg_a * _softplus(ba + dtb)
    lower = (row >= col).astype(F32)
    gcum = _dot(lower, g, NN, HIGHEST)
    return sig, neg_a, g, gcum


def _gdn_head_common(q_raw, k_raw, v, beta, gc, gr, row, col):
    causal = row >= col
    strict = row > col
    rq = lax.rsqrt(jnp.sum(q_raw * q_raw, axis=-1, keepdims=True) + EPS)
    rk = lax.rsqrt(jnp.sum(k_raw * k_raw, axis=-1, keepdims=True) + EPS)
    q = q_raw * (rq * (A_HEAD ** -0.5))
    k = k_raw * rk
    decay = jnp.where(causal, jnp.exp(jnp.where(causal, gc - gr, 0.0)), 0.0)
    eg = jnp.exp(gc)
    gl = gc[CHUNK - 1:CHUNK, :]
    ekl = jnp.exp(gl - gc)
    dec = jnp.exp(gl)
    kb = k * beta
    kbq = jnp.concatenate([kb, q], axis=0)
    both = _bdot(kbq, k, NT)
    kk, qk = both[:CHUNK], both[CHUNK:]
    a = jnp.where(causal, qk * decay, 0.0)
    return dict(rq=rq, rk=rk, q=q, k=k, decay=decay, eg=eg, ekl=ekl, dec=dec, kb=kb, kbq=kbq, kk=kk, qk=qk, a=a,
                vb=v * beta, kbg=kb * eg, qd=q * eg, ke=k * ekl, causal=causal, strict=strict)


def _gdn_fwd(qkv, ba, z, alog, dtb, wn):
    s = qkv.shape[0]
    nc = s // CHUNK

    def body(qkv_ref, ba_ref, z_ref, alog_ref, dtb_ref, wn_ref, y_ref, o_ref, st_ref, t_ref, w_ref, vn_ref, state):
        @pl.when(pl.program_id(0) == 0)
        def _():
            state[...] = jnp.zeros_like(state)

        row, col = _tri_masks()
        sig, _, _, gcum = _gdn_gates(ba_ref[...], alog_ref[...], dtb_ref[...], row, col)
        gt = gcum.T
        heads = range(A_HEADS)
        lanes = [slice(h * A_HEAD, (h + 1) * A_HEAD) for h in heads]
        fs = [_gdn_head_common(qkv_ref[:, lanes[h]], qkv_ref[:, A_QK + h * A_HEAD:A_QK + (h + 1) * A_HEAD],
                               qkv_ref[:, 2 * A_QK + h * A_HEAD:2 * A_QK + (h + 1) * A_HEAD],
                               sig[:, h:h + 1], gcum[:, 8 + h:9 + h], gt[8 + h:9 + h, :], row, col) for h in heads]
        ts = [t.astype(BF16) for t in
              _tri_inv([jnp.where(f["strict"], f["kk"] * f["decay"], 0.0) for f in fs], row, col)]
        uws = [_bdot(t, jnp.concatenate([f["vb"], f["kbg"]], axis=1), NN) for t, f in zip(ts, fs)]
        s0s = [state[h] for h in heads]
        ws_ = [uw[:, A_HEAD:].astype(BF16) for uw in uws]
        wss = [_bdot(jnp.concatenate([w, f["qd"].astype(BF16)], axis=0), s0, NN) for w, f, s0 in zip(ws_, fs, s0s)]
        vnews = [(uw[:, :A_HEAD] - wsq[:CHUNK]).astype(BF16) for uw, wsq in zip(uws, wss)]
        os_ = [wsq[CHUNK:] + _bdot(f["a"], vn, NN) for wsq, f, vn in zip(wss, fs, vnews)]
        s1s = [s0 * f["dec"] + _bdot(f["ke"], vn, TN) for s0, f, vn in zip(s0s, fs, vnews)]
        for h in heads:
            ln = lanes[h]
            st_ref[0, h] = s0s[h]
            t_ref[0, h] = ts[h]
            w_ref[:, ln] = ws_[h]
            vn_ref[:, ln] = vnews[h]
            state[h] = s1s[h]
            o = os_[h]
            o_ref[:, ln] = o
            r = lax.rsqrt(jnp.mean(o * o, axis=-1, keepdims=True) + EPS)
            zz = z_ref[:, ln]
            y_ref[:, ln] = (o * r * wn_ref[...] * zz * _sigmoid(zz)).astype(BF16)

    vec = pl.BlockSpec((1, LANE), lambda n: (0, 0))
    wide = pl.BlockSpec((CHUNK, A_QK), lambda n: (n, 0))
    return pl.pallas_call(
        body, name="gdn_fwd", grid=(nc,),
        in_specs=[pl.BlockSpec((CHUNK, A_CONV_WIDTH), lambda n: (n, 0)),
                  pl.BlockSpec((CHUNK, LANE), lambda n: (n, 0)), wide, vec, vec, vec],
        out_specs=[wide, wide, pl.BlockSpec((1, A_HEADS, A_HEAD, A_HEAD), lambda n: (n, 0, 0, 0)),
                   pl.BlockSpec((1, A_HEADS, CHUNK, CHUNK), lambda n: (n, 0, 0, 0)), wide, wide],
        out_shape=[jax.ShapeDtypeStruct((s, A_QK), BF16), jax.ShapeDtypeStruct((s, A_QK), F32),
                   jax.ShapeDtypeStruct((nc, A_HEADS, A_HEAD, A_HEAD), F32),
                   jax.ShapeDtypeStruct((nc, A_HEADS, CHUNK, CHUNK), BF16),
                   jax.ShapeDtypeStruct((s, A_QK), BF16), jax.ShapeDtypeStruct((s, A_QK), BF16)],
        scratch_shapes=[pltpu.VMEM((A_HEADS, A_HEAD, A_HEAD), F32)],
        compiler_params=_params(("arbitrary",)),
    )(qkv, ba, z, alog, dtb, wn)


def _gdn_bwd(qkv, ba, z, o_raw, dy, states, t_all, w_all, vn_all, alog, dtb, wn):
    s = qkv.shape[0]
    nc = s // CHUNK

    def body(qkv_ref, ba_ref, z_ref, o_ref, dy_ref, st_ref, t_ref, w_ref, vn_ref, alog_ref, dtb_ref, wn_ref,
             dqkv_ref, dba_ref, dz_ref, dalog_ref, ddtb_ref, dwn_ref, dstate):
        @pl.when(pl.program_id(0) == 0)
        def _():
            for r in (dstate, dalog_ref, ddtb_ref, dwn_ref):
                r[...] = jnp.zeros_like(r)

        row, col = _tri_masks()
        bat = ba_ref[...]
        sig, neg_a, g, gcum = _gdn_gates(bat, alog_ref[...], dtb_ref[...], row, col)
        gt = gcum.T
        lane = lax.broadcasted_iota(jnp.int32, (CHUNK, LANE), 1)
        ones = jnp.ones((CHUNK, LANE), F32)
        last_row = lax.broadcasted_iota(jnp.int32, (CHUNK, 1), 0) == CHUNK - 1
        wnv = wn_ref[...]
        dgc_tile = jnp.zeros((CHUNK, LANE), F32)
        dbeta_tile = jnp.zeros((CHUNK, LANE), F32)
        dwn_acc = jnp.zeros((1, LANE), F32)
        hs = []
        for h in range(A_HEADS):
            ln = slice(h * A_HEAD, (h + 1) * A_HEAD)
            lk = slice(A_QK + h * A_HEAD, A_QK + (h + 1) * A_HEAD)
            lv = slice(2 * A_QK + h * A_HEAD, 2 * A_QK + (h + 1) * A_HEAD)
            q_raw, k_raw, v = qkv_ref[:, ln], qkv_ref[:, lk], qkv_ref[:, lv]
            f = _gdn_head_common(q_raw, k_raw, v, sig[:, h:h + 1], gcum[:, 8 + h:9 + h], gt[8 + h:9 + h, :], row, col)
            f.update(h=h, ln=ln, lk=lk, lv=lv, q_raw=q_raw, k_raw=k_raw, v=v, beta=sig[:, h:h + 1],
                     s0=st_ref[0, h], ds1=dstate[h], t=t_ref[0, h], w=w_ref[:, ln], vnew=vn_ref[:, ln])
            o = o_ref[:, ln]
            zz = z_ref[:, ln]
            dyv = dy_ref[:, ln]
            r = lax.rsqrt(jnp.mean(o * o, axis=-1, keepdims=True) + EPS)
            sz = _sigmoid(zz)
            silu = zz * sz
            dz_ref[:, ln] = dyv * o * r * wnv * sz * (1.0 + zz * (1.0 - sz))
            dwn_acc = dwn_acc + jnp.sum(dyv * silu * o * r, axis=0, keepdims=True)
            tt = dyv * silu * wnv
            do = r * tt - o * (r * r * r) * jnp.mean(tt * o, axis=-1, keepdims=True)
            f["do_b"] = do.astype(BF16)
            hs.append(f)
        for f in hs:
            f["dvnew"] = _bdot(f["a"], f["do_b"], TN) + _bdot(f["ke"], f["ds1"], NN)
            f["da"] = jnp.where(f["causal"], _bdot(f["do_b"], f["vnew"], NT), 0.0)
            f["dke"] = _bdot(f["vnew"], f["ds1"], NT)
            f["ddec"] = jnp.sum(jnp.sum(f["s0"] * f["ds1"], axis=1, keepdims=True), axis=0, keepdims=True)
        for f in hs:
            do_dv = jnp.concatenate([f["do_b"], f["dvnew"].astype(BF16)], axis=0)
            both = _bdot(do_dv, f["s0"], NT)
            f["dqd"], f["dw"] = both[:CHUNK], -both[CHUNK:]
            qd_w = jnp.concatenate([f["qd"].astype(BF16), -f["w"]], axis=0)
            dstate[f["h"]] = _bdot(qd_w, do_dv, TN) + f["dec"] * f["ds1"]
        for f in hs:
            dd = jnp.concatenate([f["dvnew"], f["dw"]], axis=1).astype(BF16)
            tdd = _bdot(f["t"], dd, TN)
            f["dvb"], f["dkbg"] = tdd[:, :A_HEAD], tdd[:, A_HEAD:]
            f["dt"] = _bdot(dd, jnp.concatenate([f["vb"], f["kbg"]], axis=1), NT)
        for f in hs:
            f["tdt"] = _bdot(f["t"], f["dt"], TN)
        for f in hs:
            dm = jnp.where(f["strict"], -_bdot(f["tdt"], f["t"], NT), 0.0)
            f["ddecay"] = (dm * f["kk"] + f["da"] * f["qk"]) * f["decay"]
            f["dboth"] = jnp.concatenate([dm * f["decay"], f["da"] * f["decay"]], axis=0).astype(BF16)
        for f in hs:
            f["r2"] = _bdot(f["dboth"], f["k"], NN)
            f["dk0"] = _bdot(f["dboth"], f["kbq"], TN)
        for f in hs:
            h, k, beta = f["h"], f["k"], f["beta"]
            dkb = f["r2"][:CHUNK] + f["dkbg"] * f["eg"]
            dq = f["r2"][CHUNK:] + f["dqd"] * f["eg"]
            dk = f["dk0"] + f["dke"] * f["ekl"] + dkb * beta
            dke_ke = jnp.sum(f["dke"] * f["ke"], axis=-1, keepdims=True)
            dgc = (jnp.sum(f["ddecay"], axis=-1, keepdims=True)
                   + jnp.sum(f["dqd"] * f["qd"], axis=-1, keepdims=True) - dke_ke
                   + jnp.sum(f["dkbg"] * f["kbg"], axis=-1, keepdims=True))
            dgl = jnp.sum(dke_ke, axis=0, keepdims=True) + f["ddec"] * f["dec"]
            dgc = dgc + jnp.where(last_row, dgl, 0.0)
            dbeta = jnp.sum(dkb * k, axis=-1, keepdims=True) + jnp.sum(f["dvb"] * f["v"], axis=-1, keepdims=True)
            dgc_tile = dgc_tile + jnp.where(lane == 8 + h, dgc, 0.0)
            dbeta_tile = dbeta_tile + jnp.where(lane == h, dbeta, 0.0)
            dqn = dq * (A_HEAD ** -0.5)
            rq, rk, q_raw, k_raw = f["rq"], f["rk"], f["q_raw"], f["k_raw"]
            dqkv_ref[:, f["ln"]] = rq * dqn - q_raw * (rq * rq * rq) * jnp.sum(dqn * q_raw, axis=-1, keepdims=True)
            dqkv_ref[:, f["lk"]] = rk * dk - k_raw * (rk * rk * rk) * jnp.sum(dk * k_raw, axis=-1, keepdims=True)
            dqkv_ref[:, f["lv"]] = f["dvb"] * beta
        ddecays = [f["ddecay"] for f in hs]
        col_sums = _dot(jnp.concatenate(ddecays, axis=1), ones, TN, HIGHEST)
        for h in range(A_HEADS):
            dgc_tile = dgc_tile - jnp.where(lane == 8 + h, col_sums[h * CHUNK:(h + 1) * CHUNK, :1], 0.0)
        upper = (row <= col).astype(F32)
        dg = _dot(upper, dgc_tile, NN, HIGHEST)
        da_raw = dg * neg_a * _sigmoid(bat + dtb_ref[...])
        dba_ref[...] = jnp.where(lane < 8, dbeta_tile * sig * (1.0 - sig), jnp.where(lane < 16, da_raw, 0.0))
        dwn_ref[...] += dwn_acc
        ddtb_ref[...] += jnp.sum(da_raw, axis=0, keepdims=True)
        dalog_ref[...] += jnp.sum(dg * g, axis=0, keepdims=True)

    rev = lambda n: (nc - 1 - n, 0)
    vec = pl.BlockSpec((1, LANE), lambda n: (0, 0))
    wide = pl.BlockSpec((CHUNK, A_QK), rev)
    qkv_blk = pl.BlockSpec((CHUNK, A_CONV_WIDTH), rev)
    ba_blk = pl.BlockSpec((CHUNK, LANE), rev)
    vsh = jax.ShapeDtypeStruct((1, LANE), F32)
    return pl.pallas_call(
        body, name="gdn_bwd", grid=(nc,),
        in_specs=[qkv_blk, ba_blk, wide, wide, wide,
                  pl.BlockSpec((1, A_HEADS, A_HEAD, A_HEAD), lambda n: (nc - 1 - n, 0, 0, 0)),
                  pl.BlockSpec((1, A_HEADS, CHUNK, CHUNK), lambda n: (nc - 1 - n, 0, 0, 0)), wide, wide,
                  vec, vec, vec],
        out_specs=[qkv_blk, ba_blk, wide, vec, vec, vec],
        out_shape=[jax.ShapeDtypeStruct((s, A_CONV_WIDTH), F32), jax.ShapeDtypeStruct((s, LANE), F32),
                   jax.ShapeDtypeStruct((s, A_QK), F32), vsh, vsh, vsh],
        scratch_shapes=[pltpu.VMEM((A_HEADS, A_HEAD, A_HEAD), F32)],
        compiler_params=_params(("arbitrary",)),
    )(qkv, ba, z, o_raw, dy, states, t_all, w_all, vn_all, alog, dtb, wn)


REL_RING = 1024
QBLK_BITS = 8


def _rel_ring_onehot():
    m = lax.broadcasted_iota(jnp.int32, (REL_RING, REL_PAD), 0)
    t = lax.broadcasted_iota(jnp.int32, (REL_RING, REL_PAD), 1)
    u = jnp.where(m < KBLK, m, m - REL_RING)
    idx = jnp.clip(LEFT - u, -REL_CLIP, REL_CLIP) + REL_CLIP
    return (t == idx).astype(F32)


def _relbias_ring(table, transpose):
    n_in, n_out = (REL_RING, REL_PAD) if transpose else (REL_PAD, REL_RING)

    def body(t_ref, o_ref):
        o_ref[...] = _dot(t_ref[...], _rel_ring_onehot(), NN if transpose else NT, HIGHEST)

    return pl.pallas_call(
        body, name="relbias_ring_bwd" if transpose else "relbias_ring",
        out_shape=jax.ShapeDtypeStruct((B_HEADS, n_out), F32),
        in_specs=[VMEM_FULL], out_specs=VMEM_FULL,
        compiler_params=_params(),
    )(table)


def _row_bit(shape, bit):
    return ((lax.broadcasted_iota(jnp.int32, shape, 0) >> bit) & 1) == 1


def _relbias_expand(ring):
    def body(r_ref, o_ref):
        b = jnp.broadcast_to(r_ref[0], (QBLK, REL_RING))
        for bit in range(QBLK_BITS):
            b = jnp.where(_row_bit(b.shape, bit), pltpu.roll(b, 1 << bit, 1), b)
        j = lax.broadcasted_iota(jnp.int32, (QBLK, KBLK), 1)
        r = lax.broadcasted_iota(jnp.int32, (QBLK, KBLK), 0)
        lo = (r >> 6) << 6
        o_ref[0] = jnp.where((j >= lo) & (j < lo + LEFT + CHUNK), b[:, :KBLK], NEG_INF)

    return pl.pallas_call(
        body, name="relbias_expand", grid=(B_HEADS,),
        in_specs=[pl.BlockSpec((1, 1, REL_RING), lambda h: (h, 0, 0))],
        out_specs=pl.BlockSpec((1, QBLK, KBLK), lambda h: (h, 0, 0)),
        out_shape=jax.ShapeDtypeStruct((B_HEADS, QBLK, KBLK), F32),
        compiler_params=_params(("parallel",)),
    )(ring)


def _relbias_reduce(ds):
    def body(d_ref, o_ref):
        d = jnp.concatenate([d_ref[0], jnp.zeros((QBLK, REL_RING - KBLK), F32)], axis=1)
        for bit in range(QBLK_BITS):
            d = jnp.where(_row_bit(d.shape, bit), pltpu.roll(d, REL_RING - (1 << bit), 1), d)
        o_ref[0] = jnp.sum(d, axis=0, keepdims=True)

    return pl.pallas_call(
        body, name="relbias_reduce", grid=(B_HEADS,),
        in_specs=[pl.BlockSpec((1, QBLK, KBLK), lambda h: (h, 0, 0))],
        out_specs=pl.BlockSpec((1, 1, REL_RING), lambda h: (h, 0, 0)),
        out_shape=jax.ShapeDtypeStruct((B_HEADS, 1, REL_RING), F32),
        compiler_params=_params(("parallel",)),
    )(ds)


def _attn_probs(q_ref, kb, b_ref, hh, q0):
    hl = slice(hh * B_HEAD, (hh + 1) * B_HEAD)
    qh = q_ref[:, hl]
    kh = kb[:, hl]
    jpos = lax.broadcasted_iota(jnp.int32, (QBLK, KBLK), 1)
    sc = _bdot(qh, kh, NT) * (B_HEAD ** -0.5) + b_ref[hh]
    sc = jnp.where(jpos + q0 >= LEFT, sc, NEG_INF)
    mx = jnp.max(sc, axis=-1, keepdims=True)
    p = jnp.exp(sc - mx)
    return p / jnp.sum(p, axis=-1, keepdims=True), qh, kh


def _attn_fwd(q, kpad, vpad, bias):
    s = q.shape[0]

    def body(q_ref, k_ref, v_ref, b_ref, o_ref):
        q0 = pl.multiple_of(pl.program_id(1) * QBLK, QBLK)
        kb = k_ref[pl.ds(q0, KBLK), :]
        vb = v_ref[pl.ds(q0, KBLK), :]
        outs = []
        for hh in range(2):
            p, _, _ = _attn_probs(q_ref, kb, b_ref, hh, q0)
            outs.append(_bdot(p, vb[:, hh * B_HEAD:(hh + 1) * B_HEAD], NN))
        o_ref[...] = jnp.concatenate(outs, axis=1).astype(BF16)

    qblk = pl.BlockSpec((QBLK, LANE), lambda g, m: (m, g))
    kblk = pl.BlockSpec((LEFT + s, LANE), lambda g, m: (0, g))
    return pl.pallas_call(
        body, name="attn_fwd", grid=(B_HEADS // 2, s // QBLK),
        in_specs=[qblk, kblk, kblk, pl.BlockSpec((2, QBLK, KBLK), lambda g, m: (g, 0, 0))],
        out_specs=qblk,
        out_shape=jax.ShapeDtypeStruct((s, D_MODEL), BF16),
        compiler_params=_params(("parallel", "arbitrary")),
    )(q, kpad, vpad, bias)


def _attn_bwd(q, kpad, vpad, bias, do):
    s = q.shape[0]

    def body(q_ref, k_ref, v_ref, b_ref, do_ref, dq_ref, dk_ref, dv_ref, db_ref):
        @pl.when(pl.program_id(1) == 0)
        def _():
            for r in (dk_ref, dv_ref, db_ref):
                r[...] = jnp.zeros_like(r)

        q0 = pl.multiple_of(pl.program_id(1) * QBLK, QBLK)
        kb = k_ref[pl.ds(q0, KBLK), :]
        vb = v_ref[pl.ds(q0, KBLK), :]
        dqs, dks, dvs = [], [], []
        for hh in range(2):
            hl = slice(hh * B_HEAD, (hh + 1) * B_HEAD)
            p, qh, kh = _attn_probs(q_ref, kb, b_ref, hh, q0)
            doh = do_ref[:, hl]
            dp = _bdot(doh, vb[:, hl], NT)
            dsc = p * (dp - jnp.sum(p * dp, axis=-1, keepdims=True))
            db_ref[hh] += dsc
            dqs.append(_bdot(dsc, kh, NN) * (B_HEAD ** -0.5))
            dks.append(_bdot(dsc, qh, TN) * (B_HEAD ** -0.5))
            dvs.append(_bdot(p, doh, TN))
        dq_ref[...] = jnp.concatenate(dqs, axis=1)
        dk_ref[pl.ds(q0, KBLK), :] += jnp.concatenate(dks, axis=1)
        dv_ref[pl.ds(q0, KBLK), :] += jnp.concatenate(dvs, axis=1)

    qblk = pl.BlockSpec((QBLK, LANE), lambda g, m: (m, g))
    kblk = pl.BlockSpec((LEFT + s, LANE), lambda g, m: (0, g))
    bblk = pl.BlockSpec((2, QBLK, KBLK), lambda g, m: (g, 0, 0))
    return pl.pallas_call(
        body, name="attn_bwd", grid=(B_HEADS // 2, s // QBLK),
        in_specs=[qblk, kblk, kblk, bblk, qblk],
        out_specs=[qblk, kblk, kblk, bblk],
        out_shape=[jax.ShapeDtypeStruct((s, D_MODEL), F32), jax.ShapeDtypeStruct((LEFT + s, D_MODEL), F32),
                   jax.ShapeDtypeStruct((LEFT + s, D_MODEL), F32),
                   jax.ShapeDtypeStruct((B_HEADS, QBLK, KBLK), F32)],
        compiler_params=_params(("parallel", "arbitrary")),
    )(q, kpad, vpad, bias, do)


def _adamw(w, g, m, v):
    r, c = w.shape
    tr = r
    for cand in (512, 256, 128, 64, 32, 16, 8):
        if r % cand == 0 and cand * c * 4 <= 2 * 1024 * 1024:
            tr = cand
            break
    c1 = 1.0 / (1.0 - ADAM_B1 ** ADAM_STEP)
    c2 = 1.0 / (1.0 - ADAM_B2 ** ADAM_STEP)

    def body(w_ref, g_ref, m_ref, v_ref, d_ref, mo_ref, vo_ref):
        gv = g_ref[...]
        mn = ADAM_B1 * m_ref[...] + (1.0 - ADAM_B1) * gv
        vn = ADAM_B2 * v_ref[...] + (1.0 - ADAM_B2) * (gv * gv)
        mo_ref[...] = mn
        vo_ref[...] = vn
        d_ref[...] = -ADAM_LR * ((mn * c1) / (jnp.sqrt(vn * c2) + ADAM_EPS) + ADAM_WD * w_ref[...])

    blk = pl.BlockSpec((tr, c), lambda i: (i, 0))
    sh = jax.ShapeDtypeStruct((r, c), F32)
    return pl.pallas_call(
        body, name="adamw", grid=(r // tr,),
        in_specs=[blk] * 4, out_specs=[blk] * 3, out_shape=[sh] * 3,
        compiler_params=_params(("parallel",)),
    )(w, g, m, v)


def _row(v, width=None):
    v = v.reshape(1, -1)
    if width is not None and v.shape[1] < width:
        v = jnp.pad(v, ((0, 0), (0, width - v.shape[1])))
    return v


def _gate_row(v):
    return jnp.pad(v.reshape(1, A_HEADS), ((0, 0), (A_HEADS, LANE - 2 * A_HEADS)))


DEPTH = 4
N_A = 2
N_B = 2
F_DOWN_ROWS = FFN_DIM // N_SHARDS
SQ_ROWS = D_MODEL // N_SHARDS
GD_A_OUT0 = DEPTH * F_DOWN_ROWS // SQ_ROWS
GD_B_Q0 = GD_A_OUT0 + N_A
GD_B_OUT0 = GD_B_Q0 + N_B
UP_COLS = 2 * FFN_DIM // N_SHARDS


def _a_layer_fwd(h, w, i):
    xn = _rmsnorm_fwd(h, _row(w["a_norm"][i]))
    w_qkv, w_z, w_ba = w["a_in"][i]
    pre = _mm(xn, w_qkv, name="a_qkv")
    z = _mm(xn, w_z, name="a_z")
    ba = _mm(xn, w_ba, name="a_ba")
    act = _conv_silu_fwd(pre, w["a_conv"][i])
    alog, dtb, wn = _gate_row(w["a_A_log"][i]), _gate_row(w["a_dt_bias"][i]), _row(w["a_out_norm"][i])
    y, o_raw, states, t_all, w_all, vn_all = _gdn_fwd(act, ba, z, alog, dtb, wn)
    h2 = _mm_rowsh(y, w["GD"], SQ_ROWS, GD_A_OUT0 + i, "nn", "a_out", res=h)
    saved = dict(h=h, xn=xn, pre=pre, z=z, ba=ba, act=act, o_raw=o_raw, y=y, states=states,
                 t_all=t_all, w_all=w_all, vn_all=vn_all, alog=alog, dtb=dtb, wn=wn)
    return h2, saved


def _a_layer_bwd(dh2, w, i, sv):
    g = {}
    g["w_out"] = _mm_rowsh_dw(sv["y"], dh2, SQ_ROWS, "a_out_dw")
    dy = _mm_rowsh(dh2, w["GD"], SQ_ROWS, GD_A_OUT0 + i, "nt", "a_out_dx")
    dact, dba, dz, dalog, ddtb, dwn = _gdn_bwd(sv["act"], sv["ba"], sv["z"], sv["o_raw"], dy, sv["states"],
                                               sv["t_all"], sv["w_all"], sv["vn_all"],
                                               sv["alog"], sv["dtb"], sv["wn"])
    dpre, g["conv"] = _conv_silu_bwd(sv["pre"], w["a_conv"][i], dact)
    xn = sv["xn"]
    w_qkv, w_z, w_ba = w["a_in"][i]
    d_in = jnp.concatenate([_mm(xn, dpre, "tn", name="a_qkv_dw"), _mm(xn, dz, "tn", name="a_z_dw"),
                            _mm(xn, dba, "tn", name="a_ba_dw")[:, :2 * A_HEADS]], axis=1)
    g["w_in"] = jnp.transpose(d_in.reshape(D_MODEL, N_SHARDS, -1), (1, 0, 2))
    dxn = _mm(dpre, w_qkv, "nt", name="a_qkv_dx")
    dxn = _mm(dz, w_z, "nt", res=dxn, name="a_z_dx")
    dxn = _mm(dba, w_ba, "nt", res=dxn, name="a_ba_dx")
    dh, dnorm = _rmsnorm_bwd(sv["h"], _row(w["a_norm"][i]), dxn, dh2)
    g["norm"] = dnorm[0]
    g["A_log"] = dalog[0, A_HEADS:2 * A_HEADS]
    g["dt_bias"] = ddtb[0, A_HEADS:2 * A_HEADS]
    g["out_norm"] = dwn[0]
    return dh, g


def _by_half(a, lead):
    return jnp.moveaxis(a.reshape(a.shape[:-1] + (2, 2, UP_COLS)), (-3, -2), (0, 1)).reshape((2, 2) + lead + (UP_COLS,))


def _from_half(a):
    lead = a.shape[2:-1]
    return jnp.moveaxis(a, (0, 1), (-3, -2)).reshape(lead + (N_SHARDS * UP_COLS,))


def _ffn_fwd(h, w, l):
    s = h.shape[0]
    xn = _rmsnorm_fwd(h, _row(w["f_norm"][l]))
    cw = _by_half(w["f_conv"][l], (w["f_conv"].shape[1],))
    cb = _by_half(w["f_conv_b"][l][None], (1,))
    pre = _mm_colsh(xn, w["GU"], D_MODEL, l, "nn", "f_up").reshape(2, 2, s, UP_COLS)
    act = _ffn_act_fwd(pre, cw, cb)
    h2 = _mm_rowsh(act, w["GD"], F_DOWN_ROWS, l, "nn", "f_down", res=h)
    return h2, dict(h=h, xn=xn, pre=pre, act=act, cw=cw, cb=cb)


def _ffn_bwd(dh2, w, l, sv):
    g = {}
    s = dh2.shape[0]
    g["w_down"] = _mm_rowsh_dw(sv["act"], dh2, F_DOWN_ROWS, "f_down_dw")
    dact = _mm_rowsh(dh2, w["GD"], F_DOWN_ROWS, l, "nt", "f_down_dx")
    dpre, dcw, dcb = _ffn_act_bwd(sv["pre"], sv["cw"], sv["cb"], dact)
    dpre = dpre.reshape(N_SHARDS, s, UP_COLS)
    g["w_up"] = _mm_colsh_dw(sv["xn"], dpre, "f_up_dw")
    g["conv"] = _from_half(dcw)
    g["conv_b"] = _from_half(dcb)[0]
    dxn = _mm_colsh(dpre, w["GU"], D_MODEL, l, "nt", "f_up_dx")
    dh, dnorm = _rmsnorm_bwd(sv["h"], _row(w["f_norm"][l]), dxn, dh2)
    g["norm"] = dnorm[0]
    return dh, g


def _b_layer_fwd(h, w, j, kpad, vpad):
    xn = _rmsnorm_fwd(h, _row(w["b_norm"][j]))
    q = _mm_rowsh(xn, w["GD"], SQ_ROWS, GD_B_Q0 + j, "nn", "b_q")
    rel = w["b_rel_bias"][j]
    table = jnp.pad(rel, ((0, 0), (0, REL_PAD - rel.shape[1])))
    bias = _relbias_expand(_relbias_ring(table, False).reshape(B_HEADS, 1, REL_RING))
    o = _attn_fwd(q, kpad, vpad, bias)
    h2 = _mm_rowsh(o, w["GD"], SQ_ROWS, GD_B_OUT0 + j, "nn", "b_out", res=h)
    return h2, dict(h=h, xn=xn, q=q, o=o, bias=bias)


def _b_layer_bwd(dh2, w, j, sv, kpad, vpad):
    g = {}
    g["w_out"] = _mm_rowsh_dw(sv["o"], dh2, SQ_ROWS, "b_out_dw")
    do = _mm_rowsh(dh2, w["GD"], SQ_ROWS, GD_B_OUT0 + j, "nt", "b_out_dx")
    dq, dkp, dvp, dsc = _attn_bwd(sv["q"], kpad, vpad, sv["bias"], do)
    dring = _relbias_reduce(dsc).reshape(B_HEADS, REL_RING)
    g["rel_bias"] = _relbias_ring(dring, True)[:, :2 * REL_CLIP + 1]
    g["w_q"] = _mm_rowsh_dw(sv["xn"], dq, SQ_ROWS, "b_q_dw")
    dxn = _mm_rowsh(dq, w["GD"], SQ_ROWS, GD_B_Q0 + j, "nt", "b_q_dx")
    dh, dnorm = _rmsnorm_bwd(sv["h"], _row(w["b_norm"][j]), dxn, dh2)
    g["norm"] = dnorm[0]
    return dh, g, dkp, dvp


def _local_step(x, tgt, w):
    h = x
    saved = []
    kv_saved = None
    kpad = vpad = None
    for layer in range(DEPTH):
        if layer < N_A:
            h, sm = _a_layer_fwd(h, w, layer)
        else:
            if layer == N_A:
                xn_kv = _rmsnorm_fwd(h, _row(w["kv_norm"]))
                kv = _mm_colsh(xn_kv, w["GK"], D_MODEL, 0, "nn", "kv", flat=True, out_dtype=BF16)
                kpad = jnp.pad(kv[:, :D_MODEL], ((LEFT, 0), (0, 0)))
                vpad = jnp.pad(kv[:, D_MODEL:], ((LEFT, 0), (0, 0)))
                kv_saved = dict(h=h, xn=xn_kv)
            h, sm = _b_layer_fwd(h, w, layer - N_A, kpad, vpad)
        h, sf = _ffn_fwd(h, w, layer)
        saved.append((sm, sf))

    loss, dh, dfinal = _final_loss(h, _row(w["final_norm"]), tgt)

    ga = [None] * N_A
    gb = [None] * N_B
    gf = [None] * DEPTH
    dk_tot = dv_tot = None
    g_kv = g_kvn = None
    for layer in reversed(range(DEPTH)):
        sm, sf = saved[layer]
        dh, gf[layer] = _ffn_bwd(dh, w, layer, sf)
        if layer >= N_A:
            dh, gb[layer - N_A], dkp, dvp = _b_layer_bwd(dh, w, layer - N_A, sm, kpad, vpad)
            dk_tot = dkp if dk_tot is None else dk_tot + dkp
            dv_tot = dvp if dv_tot is None else dv_tot + dvp
            if layer == N_A:
                dkv = jnp.concatenate([dk_tot[LEFT:], dv_tot[LEFT:]], axis=1)
                g_kv = _mm_colsh_dw(kv_saved["xn"], dkv, "kv_dw", flat=True)
                dxn = _mm_colsh(dkv, w["GK"], D_MODEL, 0, "nt", "kv_dx", flat=True)
                dh, g_kvn = _rmsnorm_bwd(kv_saved["h"], _row(w["kv_norm"]), dxn, dh)
        else:
            dh, ga[layer] = _a_layer_bwd(dh, w, layer, sm)

    def stack(gs, key):
        return jnp.stack([g[key] for g in gs])

    def rows(gs, key):
        return [g[key] for g in gs]

    grads = dict(
        GU=jnp.concatenate(rows(gf, "w_up"), axis=1),
        GD=jnp.concatenate(rows(gf, "w_down") + rows(ga, "w_out") + rows(gb, "w_q") + rows(gb, "w_out"), axis=1),
        GK=g_kv,
        GI=jnp.concatenate(rows(ga, "w_in"), axis=1),
        a_norm=stack(ga, "norm"), a_conv=stack(ga, "conv"), a_A_log=stack(ga, "A_log"),
        a_dt_bias=stack(ga, "dt_bias"), a_out_norm=stack(ga, "out_norm"), kv_norm=g_kvn[0],
        b_norm=stack(gb, "norm"), b_rel_bias=stack(gb, "rel_bias"),
        f_norm=stack(gf, "norm"), f_conv=stack(gf, "conv"), f_conv_b=stack(gf, "conv_b"), final_norm=dfinal[0])
    return loss, dh, grads


HBM_SPEC = pl.BlockSpec(memory_space=pl.ANY)
VMEM_SPEC = pl.BlockSpec(memory_space=pltpu.VMEM)


def _place():
    x, y, c = lax.axis_index("x"), lax.axis_index("y"), lax.axis_index("c")
    chips = [(1 - x, y), (x, 1 - y), (1 - x, 1 - y)]
    return x, y, c, chips


def _remote(src, dst, send_sem, recv_sem, to):
    return pltpu.make_async_remote_copy(src_ref=src, dst_ref=dst, send_sem=send_sem, recv_sem=recv_sem,
                                        device_id=to, device_id_type=MESH)


def _allgather_weights(shards):
    n = len(shards)

    def body(*refs):
        x_refs, out_refs, (send_sems, recv_sems) = refs[:n], refs[n:2 * n], refs[2 * n:]
        x, y, c, chips = _place()
        sibling = (x, y, 1 - c)

        def half(a, px, py, hc):
            rh = shards[a].shape[0] // 2
            return out_refs[a].at[2 * px + py, pl.ds(hc * rh, rh), :]

        def mine(a):
            rh = shards[a].shape[0] // 2
            return x_refs[a].at[pl.ds(c * rh, rh), :]

        first = [_remote(mine(a), half(a, x, y, c), send_sems.at[6 * a + j], recv_sems.at[6 * a + j], (*chip, c))
                 for a in range(n) for j, chip in enumerate(chips)]
        for cp in first:
            cp.start()
        passed = []
        for j, chip in enumerate(chips):
            for a in range(n):
                landed = half(a, *chip, c)
                _remote(landed, landed, send_sems.at[6 * a + j], recv_sems.at[6 * a + j], (*chip, c)).wait_recv()
                cp = _remote(landed, landed, send_sems.at[6 * a + 3 + j], recv_sems.at[6 * a + 3 + j], sibling)
                cp.start()
                passed.append(cp)
        for j, chip in enumerate(chips):
            for a in range(n):
                theirs = half(a, *chip, 1 - c)
                _remote(theirs, theirs, send_sems.at[6 * a + 3 + j], recv_sems.at[6 * a + 3 + j],
                        sibling).wait_recv()
        for cp in first + passed:
            cp.wait_send()

    return pl.pallas_call(
        body, name="allgather_weights",
        out_shape=[jax.ShapeDtypeStruct((N_SHARDS,) + sh.shape, sh.dtype) for sh in shards],
        in_specs=[HBM_SPEC] * n, out_specs=[HBM_SPEC] * n,
        scratch_shapes=[pltpu.SemaphoreType.DMA((6 * n,)), pltpu.SemaphoreType.DMA((6 * n,))],
    )(*shards)


def _pair_exchange(gs):
    n = len(gs)

    def body(*refs):
        g_refs, out_refs, (send_sems, recv_sems) = refs[:n], refs[n:2 * n], refs[2 * n:]
        x, y, c, _ = _place()
        cps = []
        for a in range(n):
            rh = gs[a].shape[1] // 2
            cps.append(_remote(g_refs[a].at[:, pl.ds((1 - c) * rh, rh), :], out_refs[a], send_sems.at[a],
                               recv_sems.at[a], (x, y, 1 - c)))
        for cp in cps:
            cp.start()
        for cp in cps:
            cp.wait()

    return pl.pallas_call(
        body, name="rs_pair_exchange",
        out_shape=[jax.ShapeDtypeStruct((g.shape[0], g.shape[1] // 2, g.shape[2]), g.dtype) for g in gs],
        in_specs=[HBM_SPEC] * n, out_specs=[HBM_SPEC] * n,
        scratch_shapes=[pltpu.SemaphoreType.DMA((n,)), pltpu.SemaphoreType.DMA((n,))],
    )(*gs)


def _add_rows(rows, cols):
    best = 16
    for t in range(16, rows + 1, 16):
        if rows % t == 0 and t * cols * 4 <= 2304 * 1024:
            best = t
    return best


def _pair_add(g, other, c_idx):
    n, r, cols = g.shape
    rh = r // 2
    tr = _add_rows(rh, cols)
    nb = rh // tr

    def body(c_ref, a_ref, b_ref, o_ref, ob_ref):
        sm = a_ref[...] + b_ref[...]
        o_ref[...] = sm
        ob_ref[...] = sm.astype(BF16)

    out_blk = pl.BlockSpec((1, tr, cols), lambda s, i, c_ref: (s, i, 0))
    return pl.pallas_call(
        body, name="rs_pair_add",
        grid_spec=pltpu.PrefetchScalarGridSpec(
            num_scalar_prefetch=1, grid=(n, nb),
            in_specs=[pl.BlockSpec((1, tr, cols), lambda s, i, c_ref: (s, c_ref[0] * nb + i, 0)), out_blk],
            out_specs=[out_blk, out_blk]),
        out_shape=[jax.ShapeDtypeStruct((n, rh, cols), F32), jax.ShapeDtypeStruct((n, rh, cols), BF16)],
        compiler_params=_params(("parallel", "parallel")),
    )(c_idx, g, other)


def _chip_exchange(ps):
    n = len(ps)

    def body(*refs):
        p_refs, out_refs, (send_sems, recv_sems) = refs[:n], refs[n:2 * n], refs[2 * n:]
        x, y, c, chips = _place()
        cps = [_remote(p_refs[a].at[2 * chip[0] + chip[1]], out_refs[a].at[j], send_sems.at[3 * a + j],
                       recv_sems.at[3 * a + j], (*chip, c))
               for a in range(n) for j, chip in enumerate(chips)]
        for cp in cps:
            cp.start()
        for cp in cps:
            cp.wait()

    return pl.pallas_call(
        body, name="rs_chip_exchange",
        out_shape=[jax.ShapeDtypeStruct((3,) + p.shape[1:], p.dtype) for p in ps],
        in_specs=[HBM_SPEC] * n, out_specs=[HBM_SPEC] * n,
        scratch_shapes=[pltpu.SemaphoreType.DMA((3 * n,)), pltpu.SemaphoreType.DMA((3 * n,))],
    )(*ps)


def _chip_add(p, recv, chip_idx):
    n, rh, cols = p.shape
    tr = _add_rows(rh, cols)

    def body(s_ref, own_ref, r_ref, o_ref):
        o_ref[...] = ((own_ref[0] + r_ref[0].astype(F32)) + r_ref[1].astype(F32)) + r_ref[2].astype(F32)

    return pl.pallas_call(
        body, name="rs_chip_add",
        grid_spec=pltpu.PrefetchScalarGridSpec(
            num_scalar_prefetch=1, grid=(rh // tr,),
            in_specs=[pl.BlockSpec((1, tr, cols), lambda i, s_ref: (s_ref[0], i, 0)),
                      pl.BlockSpec((3, tr, cols), lambda i, s_ref: (0, i, 0))],
            out_specs=pl.BlockSpec((tr, cols), lambda i, s_ref: (i, 0))),
        out_shape=jax.ShapeDtypeStruct((rh, cols), p.dtype),
        compiler_params=_params(("parallel",)),
    )(chip_idx, p, recv)


def _pair_gather(fs):
    n = len(fs)

    def body(*refs):
        f_refs, out_refs, (send_sems, recv_sems) = refs[:n], refs[n:2 * n], refs[2 * n:]
        x, y, c, _ = _place()
        cps = [_remote(f_refs[a], out_refs[a], send_sems.at[a], recv_sems.at[a], (x, y, 1 - c)) for a in range(n)]
        for cp in cps:
            cp.start()
        for cp in cps:
            cp.wait()

    return pl.pallas_call(
        body, name="rs_pair_gather",
        out_shape=[jax.ShapeDtypeStruct(f.shape, f.dtype) for f in fs],
        in_specs=[HBM_SPEC] * n, out_specs=[HBM_SPEC] * n,
        scratch_shapes=[pltpu.SemaphoreType.DMA((n,)), pltpu.SemaphoreType.DMA((n,))],
    )(*fs)


def _allreduce_small(v):
    r, cols = v.shape

    def body(x_ref, out_ref, slots, send_sems, recv_sems):
        x, y, c, _ = _place()
        bits = [(bx, by, bc) for bx in (0, 1) for by in (0, 1) for bc in (0, 1)]

        def flip(b):
            return (1 - x if b[0] else x, 1 - y if b[1] else y, 1 - c if b[2] else c)

        slots[0] = x_ref[...]
        cps = [_remote(x_ref, slots.at[k], send_sems.at[k - 1], recv_sems.at[k - 1], flip(bits[k]))
               for k in range(1, 8)]
        for cp in cps:
            cp.start()
        for cp in cps:
            cp.wait()
        acc = None
        for b in bits:
            fx, fy, fc = flip(b)
            term = slots[4 * fx + 2 * fy + fc]
            acc = term if acc is None else acc + term
        out_ref[...] = acc

    return pl.pallas_call(
        body, name="allreduce_small",
        out_shape=jax.ShapeDtypeStruct((r, cols), v.dtype),
        in_specs=[VMEM_SPEC], out_specs=VMEM_SPEC,
        scratch_shapes=[pltpu.VMEM((8, r, cols), v.dtype), pltpu.SemaphoreType.DMA((7,)),
                        pltpu.SemaphoreType.DMA((7,))],
        compiler_params=pltpu.CompilerParams(vmem_limit_bytes=VMEM_LIMIT),
    )(v)


BIG = (("a_w_in", 2), ("a_w_out", 1), ("w_kv", 1), ("b_w_q", 1), ("b_w_out", 1), ("f_w_up", 2), ("f_w_down", 1))
SMALL = (("a_norm", 1), ("a_conv", 2), ("a_A_log", None), ("a_dt_bias", None), ("a_out_norm", None),
         ("kv_norm", None), ("b_norm", None), ("b_rel_bias", None), ("f_norm", None), ("f_conv", 2),
         ("f_conv_b", None), ("final_norm", None))
WEIGHT_ORDER = ("a_norm", "a_w_in", "a_conv", "a_A_log", "a_dt_bias", "a_out_norm", "a_w_out", "kv_norm", "w_kv",
                "b_norm", "b_w_q", "b_rel_bias", "b_w_out", "f_norm", "f_w_up", "f_conv", "f_conv_b", "f_w_down",
                "final_norm")


def _pad_rows(flat, cols, quantum):
    n = flat.shape[-1]
    rows = -(-n // (cols * quantum)) * quantum
    pad = [(0, 0)] * (flat.ndim - 1) + [(0, rows * cols - n)]
    return jnp.pad(flat, pad).reshape(flat.shape[:-1] + (rows, cols))


GROUPS = ("GU", "GD", "GK", "GI")


def _group_shards(w, dtype):
    def two(a):
        return a.reshape(-1, a.shape[-1])

    return dict(GU=two(w["f_w_up"]).astype(dtype),
                GD=jnp.concatenate([two(w[n]) for n in ("f_w_down", "a_w_out", "b_w_q", "b_w_out")]).astype(dtype),
                GK=w["w_kv"].astype(dtype),
                GI=two(w["a_w_in"]).astype(dtype))


def _ungroup(red, shard_shapes):
    out = dict(f_w_up=red["GU"].reshape(shard_shapes["f_w_up"]), w_kv=red["GK"],
               a_w_in=red["GI"].reshape(shard_shapes["a_w_in"]))
    off = 0
    for n in ("f_w_down", "a_w_out", "b_w_q", "b_w_out"):
        shp = shard_shapes[n]
        rows = math.prod(shp[:-1])
        out[n] = red["GD"][off:off + rows].reshape(shp)
        off += rows
    return out


def _dense_a_in(gi):
    out = []
    for i in range(N_A):
        full = jnp.transpose(gi[:, i * D_MODEL:(i + 1) * D_MODEL], (1, 0, 2)).reshape(D_MODEL, -1)
        out.append((full[:, :A_CONV_WIDTH], full[:, A_CONV_WIDTH:A_CONV_WIDTH + A_QK],
                    jnp.pad(full[:, A_CONV_WIDTH + A_QK:], ((0, 0), (0, LANE - 2 * A_HEADS)))))
    return out


def _pack_small(values, names):
    return _pad_rows(jnp.concatenate([values[n].reshape(-1) for n in names]), LANE, SUB)


def _unpack_small(packed, shapes, names):
    flat = packed.reshape(-1)
    out, off = {}, 0
    for n in names:
        size = math.prod(shapes[n])
        out[n] = flat[off:off + size].reshape(shapes[n])
        off += size
    return out


def _adamw_nd(w, g, m, v):
    shp = w.shape
    two = (math.prod(shp[:-1]), shp[-1])
    d, mn, vn = _adamw(w.reshape(two), g.reshape(two), m.reshape(two), v.reshape(two))
    return d.reshape(shp), mn.reshape(shp), vn.reshape(shp)


def kernel(x, a_norm, a_w_in, a_conv, a_A_log, a_dt_bias, a_out_norm, a_w_out, kv_norm, w_kv, b_norm, b_w_q, b_rel_bias, b_w_out, f_norm, f_w_up, f_conv, f_conv_b, f_w_down, final_norm, loss_target, m_a_norm, m_a_w_in, m_a_conv, m_a_A_log, m_a_dt_bias, m_a_out_norm, m_a_w_out, m_kv_norm, m_w_kv, m_b_norm, m_b_w_q, m_b_rel_bias, m_b_w_out, m_f_norm, m_f_w_up, m_f_conv, m_f_conv_b, m_f_w_down, m_final_norm, v_a_norm, v_a_w_in, v_a_conv, v_a_A_log, v_a_dt_bias, v_a_out_norm, v_a_w_out, v_kv_norm, v_w_kv, v_b_norm, v_b_w_q, v_b_rel_bias, v_b_w_out, v_f_norm, v_f_w_up, v_f_conv, v_f_conv_b, v_f_w_down, v_final_norm):
    w = dict(a_norm=a_norm, a_w_in=a_w_in, a_conv=a_conv, a_A_log=a_A_log, a_dt_bias=a_dt_bias,
             a_out_norm=a_out_norm, a_w_out=a_w_out, kv_norm=kv_norm, w_kv=w_kv, b_norm=b_norm, b_w_q=b_w_q,
             b_rel_bias=b_rel_bias, b_w_out=b_w_out, f_norm=f_norm, f_w_up=f_w_up, f_conv=f_conv,
             f_conv_b=f_conv_b, f_w_down=f_w_down, final_norm=final_norm)
    m = dict(a_norm=m_a_norm, a_w_in=m_a_w_in, a_conv=m_a_conv, a_A_log=m_a_A_log, a_dt_bias=m_a_dt_bias,
             a_out_norm=m_a_out_norm, a_w_out=m_a_w_out, kv_norm=m_kv_norm, w_kv=m_w_kv, b_norm=m_b_norm,
             b_w_q=m_b_w_q, b_rel_bias=m_b_rel_bias, b_w_out=m_b_w_out, f_norm=m_f_norm, f_w_up=m_f_w_up,
             f_conv=m_f_conv, f_conv_b=m_f_conv_b, f_w_down=m_f_w_down, final_norm=m_final_norm)
    v = dict(a_norm=v_a_norm, a_w_in=v_a_w_in, a_conv=v_a_conv, a_A_log=v_a_A_log, a_dt_bias=v_a_dt_bias,
             a_out_norm=v_a_out_norm, a_w_out=v_a_w_out, kv_norm=v_kv_norm, w_kv=v_w_kv, b_norm=v_b_norm,
             b_w_q=v_b_w_q, b_rel_bias=v_b_rel_bias, b_w_out=v_b_w_out, f_norm=v_f_norm, f_w_up=v_f_w_up,
             f_conv=v_f_conv, f_conv_b=v_f_conv_b, f_w_down=v_f_w_down, final_norm=v_final_norm)
    xi, yi, ci = lax.axis_index("x"), lax.axis_index("y"), lax.axis_index("c")
    chip = 2 * xi + yi
    shard_shapes = {n: w[n].shape for n in WEIGHT_ORDER}

    mine_w = _group_shards(w, BF16)
    landed = _allgather_weights([mine_w[n] for n in GROUPS])
    full = {n: lax.dynamic_update_slice(buf, mine_w[n][None], (chip, 0, 0)) for n, buf in zip(GROUPS, landed)}
    full["a_in"] = _dense_a_in(full.pop("GI"))
    sharded_small = [n for n, axis in SMALL if axis is not None]
    placed = {}
    for n, axis in SMALL:
        if axis is not None:
            wide = list(w[n].shape)
            wide[axis] *= 4
            mine_once = w[n] * (1 - ci).astype(F32)
            placed[n] = lax.dynamic_update_slice_in_dim(jnp.zeros(wide, F32), mine_once, chip * w[n].shape[axis], axis)
    placed_shapes = {n: placed[n].shape for n in sharded_small}
    full.update(_unpack_small(_allreduce_small(_pack_small(placed, sharded_small)), placed_shapes, sharded_small))
    for n, axis in SMALL:
        if axis is None:
            full[n] = w[n]

    loss_part, grad_x, grads = _local_step(x[0], loss_target[0], full)

    packed = [grads[n] for n in GROUPS]
    c_idx = jnp.reshape(ci, (1,)).astype(jnp.int32)
    chip_idx = jnp.reshape(chip, (1,)).astype(jnp.int32)
    pairs = [_pair_add(g, o, c_idx) for g, o in zip(packed, _pair_exchange(packed))]
    arrived = _chip_exchange([pb for _, pb in pairs])
    mine = [_chip_add(p, r, chip_idx) for (p, _), r in zip(pairs, arrived)]
    theirs = _pair_gather(mine)
    red = _ungroup({n: jnp.concatenate([jnp.where(ci == 0, a, b), jnp.where(ci == 0, b, a)], axis=0)
                    for n, a, b in zip(GROUPS, mine, theirs)}, shard_shapes)

    small_names = [n for n, _ in SMALL]
    small_vals = {n: grads[n] for n in small_names}
    small_vals["loss"] = loss_part[0, :1]
    names = ["loss"] + small_names
    shapes = {n: small_vals[n].shape for n in names}
    summed = _unpack_small(_allreduce_small(_pack_small(small_vals, names)), shapes, names)
    loss = summed["loss"][0]
    for n, axis in SMALL:
        g = summed[n]
        if axis is not None:
            g = lax.dynamic_slice_in_dim(g, chip * w[n].shape[axis], w[n].shape[axis], axis)
        red[n] = g

    delta, new_m, new_v = {}, {}, {}
    for n, _ in BIG:
        delta[n], new_m[n], new_v[n] = _adamw_nd(w[n], red[n], m[n], v[n])
    local_shapes = {n: w[n].shape for n in small_names}
    packs = [_pack_small(t, small_names) for t in (w, red, m, v)]
    outs = _adamw(*packs)
    ds, ms, vs = (_unpack_small(o, local_shapes, small_names) for o in outs)
    delta.update(ds)
    new_m.update(ms)
    new_v.update(vs)

    return (loss, grad_x[None], *[red[n] for n in WEIGHT_ORDER], *[delta[n] for n in WEIGHT_ORDER],
            *[new_m[n] for n in WEIGHT_ORDER], *[new_v[n] for n in WEIGHT_ORDER])
```

```python
import math

import jax
import jax.numpy as jnp
from jax import lax
from jax.experimental import pallas as pl
from jax.experimental.pallas import tpu as pltpu

F32 = jnp.float32
BF16 = jnp.bfloat16
HIGHEST = lax.Precision.HIGHEST
MESH = pl.DeviceIdType.MESH

D_MODEL = 1024
CHUNK = 64
A_HEADS = 8
A_HEAD = 128
A_QK = A_HEADS * A_HEAD
A_CONV_WIDTH = 3 * A_QK
B_HEADS = 16
B_HEAD = 64
LEFT = 8 * CHUNK
QBLK = 4 * CHUNK
KBLK = LEFT + QBLK
REL_CLIP = 256
REL_PAD = 640
FFN_DIM = 2816
EPS = 1e-6
NEG_INF = -1e30
LANE = 128
SUB = 8
VMEM_LIMIT = 56 * 1024 * 1024

ADAM_LR = 0.001
ADAM_B1 = 0.9
ADAM_B2 = 0.999
ADAM_EPS = 1e-08
ADAM_WD = 0.01
ADAM_STEP = 10


VMEM_FULL = pl.BlockSpec(memory_space=pltpu.VMEM)


def _params(sem=None):
    return pltpu.CompilerParams(dimension_semantics=sem, vmem_limit_bytes=VMEM_LIMIT)


def _tile(n, cap):
    if n <= cap:
        return n
    best = None
    for t in range(LANE, cap + 1, LANE):
        if n % t == 0:
            best = t
    assert best is not None, n
    return best


def _sigmoid(x):
    return 1.0 / (1.0 + jnp.exp(-x))


def _softplus(x):
    return jnp.maximum(x, 0.0) + jnp.log(1.0 + jnp.exp(-jnp.abs(x)))


def _dot(a, b, dims, prec=None):
    return lax.dot_general(a, b, (dims, ((), ())), preferred_element_type=F32, precision=prec)


NN = ((1,), (0,))
NT = ((1,), (1,))
TN = ((0,), (0,))


def _bdot(a, b, dims):
    return _dot(a.astype(BF16), b.astype(BF16), dims)


def _mm(a, b, mode="nn", out_dtype=F32, res=None, name="mm"):
    if mode == "nn":
        (m, k), (k2, n) = a.shape, b.shape
    elif mode == "nt":
        (m, k), (n, k2) = a.shape, b.shape
    else:
        (k, m), (k2, n) = a.shape, b.shape
    assert k == k2, (a.shape, b.shape, mode)
    tm, tn, tk = _tile(m, 1408), _tile(n, 1408), _tile(k, 1408)
    if m == 8192:
        tm = 1024
    if k == 8192:
        tk = 1024
    nk = k // tk
    dims = {"nn": NN, "nt": NT, "tn": TN}[mode]
    a_spec = {"nn": pl.BlockSpec((tm, tk), lambda i, j, kk: (i, kk)),
              "nt": pl.BlockSpec((tm, tk), lambda i, j, kk: (i, kk)),
              "tn": pl.BlockSpec((tk, tm), lambda i, j, kk: (kk, i))}[mode]
    b_spec = {"nn": pl.BlockSpec((tk, tn), lambda i, j, kk: (kk, j)),
              "nt": pl.BlockSpec((tn, tk), lambda i, j, kk: (j, kk)),
              "tn": pl.BlockSpec((tk, tn), lambda i, j, kk: (kk, j))}[mode]
    o_spec = pl.BlockSpec((tm, tn), lambda i, j, kk: (i, j))
    has_res = res is not None

    def body(a_ref, b_ref, *rest):
        if has_res:
            res_ref, o_ref, acc = rest
        else:
            o_ref, acc = rest
        kk = pl.program_id(2)

        @pl.when(kk == 0)
        def _():
            acc[...] = jnp.zeros_like(acc)

        acc[...] += _bdot(a_ref[...], b_ref[...], dims)

        @pl.when(kk == nk - 1)
        def _():
            r = acc[...]
            if has_res:
                r = r + res_ref[...]
            o_ref[...] = r.astype(out_dtype)

    args = [a, b] + ([res] if has_res else [])
    in_specs = [a_spec, b_spec] + ([o_spec] if has_res else [])
    return pl.pallas_call(
        body, name=name, grid=(m // tm, n // tn, nk),
        in_specs=in_specs, out_specs=o_spec,
        out_shape=jax.ShapeDtypeStruct((m, n), out_dtype),
        scratch_shapes=[pltpu.VMEM((tm, tn), F32)],
        compiler_params=_params(("parallel", "parallel", "arbitrary")),
    )(*args)


ROW_TILE = 1024
N_SHARDS = 4


def _mm_call(name, a, b, dims, grid, a_spec, b_spec, o_spec, out_shape, out_dtype, acc_shape, res=None):
    nk = grid[2]
    has_res = res is not None

    def flat(v):
        return v.reshape(-1, v.shape[-1]) if v.ndim == 3 else v

    def body(a_ref, b_ref, *rest):
        if has_res:
            res_ref, o_ref, acc = rest
        else:
            o_ref, acc = rest
        kk = pl.program_id(2)

        @pl.when(kk == 0)
        def _():
            acc[...] = jnp.zeros_like(acc)

        acc[...] += _bdot(flat(a_ref[...]), flat(b_ref[...]), dims)

        @pl.when(kk == nk - 1)
        def _():
            r = acc[...]
            if has_res:
                r = r + res_ref[...]
            o_ref[...] = r.reshape(o_ref.shape).astype(out_dtype)

    args = [a, b] + ([res] if has_res else [])
    in_specs = [a_spec, b_spec] + ([o_spec] if has_res else [])
    return pl.pallas_call(
        body, name=name, grid=grid, in_specs=in_specs, out_specs=o_spec,
        out_shape=jax.ShapeDtypeStruct(out_shape, out_dtype),
        scratch_shapes=[pltpu.VMEM(acc_shape, F32)],
        compiler_params=_params(("parallel", "parallel", "arbitrary")),
    )(*args)


def _shards_per_block(rows):
    return N_SHARDS if N_SHARDS * rows <= 1408 else 2


def _mm_rowsh(a, buf, rows, blk0, mode, name, res=None, out_dtype=F32):
    s = a.shape[0]
    cols = buf.shape[2]
    g = _shards_per_block(rows)
    tm = _tile(s, ROW_TILE)
    b_blk = (g, rows, cols)
    if mode == "nn":
        return _mm_call(name, a, buf, NN, (s // tm, 1, N_SHARDS // g),
                        pl.BlockSpec((tm, g * rows), lambda i, j, kk: (i, kk)),
                        pl.BlockSpec(b_blk, lambda i, j, kk: (kk, blk0, 0)),
                        pl.BlockSpec((tm, cols), lambda i, j, kk: (i, 0)),
                        (s, cols), out_dtype, (tm, cols), res)
    return _mm_call(name, a, buf, NT, (s // tm, N_SHARDS // g, 1),
                    pl.BlockSpec((tm, cols), lambda i, j, kk: (i, 0)),
                    pl.BlockSpec(b_blk, lambda i, j, kk: (j, blk0, 0)),
                    pl.BlockSpec((tm, g * rows), lambda i, j, kk: (i, j)),
                    (s, N_SHARDS * rows), out_dtype, (tm, g * rows), res)


def _mm_rowsh_dw(act, dy, rows, name):
    s = act.shape[0]
    cols = dy.shape[1]
    g = _shards_per_block(rows)
    ts = _tile(s, ROW_TILE)
    return _mm_call(name, act, dy, TN, (1, N_SHARDS // g, s // ts),
                    pl.BlockSpec((ts, g * rows), lambda i, j, kk: (kk, j)),
                    pl.BlockSpec((ts, cols), lambda i, j, kk: (kk, 0)),
                    pl.BlockSpec((g, rows, cols), lambda i, j, kk: (j, 0, 0)),
                    (N_SHARDS, rows, cols), F32, (g * rows, cols))


def _mm_colsh(a, buf, krows, blk0, mode, name, flat=False, res=None, out_dtype=F32):
    cols = buf.shape[2]
    b_nn = pl.BlockSpec((None, krows, cols), lambda i, j, kk: (j, blk0, 0))
    b_nt = pl.BlockSpec((None, krows, cols), lambda i, j, kk: (kk, blk0, 0))
    if mode == "nn":
        s = a.shape[0]
        tm = _tile(s, ROW_TILE)
        o_spec = (pl.BlockSpec((tm, cols), lambda i, j, kk: (i, j)) if flat
                  else pl.BlockSpec((None, tm, cols), lambda i, j, kk: (j, i, 0)))
        return _mm_call(name, a, buf, NN, (s // tm, N_SHARDS, 1),
                        pl.BlockSpec((tm, krows), lambda i, j, kk: (i, 0)), b_nn, o_spec,
                        (s, N_SHARDS * cols) if flat else (N_SHARDS, s, cols), out_dtype, (tm, cols), res)
    s = a.shape[0] if flat else a.shape[1]
    tm = _tile(s, ROW_TILE)
    a_spec = (pl.BlockSpec((tm, cols), lambda i, j, kk: (i, kk)) if flat
              else pl.BlockSpec((None, tm, cols), lambda i, j, kk: (kk, i, 0)))
    return _mm_call(name, a, buf, NT, (s // tm, 1, N_SHARDS), a_spec, b_nt,
                    pl.BlockSpec((tm, krows), lambda i, j, kk: (i, 0)),
                    (s, krows), out_dtype, (tm, krows), res)


def _mm_colsh_dw(x, dy, name, flat=False):
    s, k = x.shape
    cols = dy.shape[1] // N_SHARDS if flat else dy.shape[2]
    ts = _tile(s, ROW_TILE)
    b_spec = (pl.BlockSpec((ts, cols), lambda i, j, kk: (kk, j)) if flat
              else pl.BlockSpec((None, ts, cols), lambda i, j, kk: (j, kk, 0)))
    return _mm_call(name, x, dy, TN, (1, N_SHARDS, s // ts),
                    pl.BlockSpec((ts, k), lambda i, j, kk: (kk, 0)), b_spec,
                    pl.BlockSpec((None, k, cols), lambda i, j, kk: (j, 0, 0)),
                    (N_SHARDS, k, cols), F32, (k, cols))


def _rmsnorm_fwd(x, g):
    s, d = x.shape
    tr = _tile(s, 1024)

    def body(x_ref, g_ref, o_ref):
        xv = x_ref[...]
        r = lax.rsqrt(jnp.mean(xv * xv, axis=-1, keepdims=True) + EPS)
        o_ref[...] = (xv * r * g_ref[...]).astype(BF16)

    return pl.pallas_call(
        body, name="rmsnorm_fwd", grid=(s // tr,),
        in_specs=[pl.BlockSpec((tr, d), lambda i: (i, 0)), pl.BlockSpec((1, d), lambda i: (0, 0))],
        out_specs=pl.BlockSpec((tr, d), lambda i: (i, 0)),
        out_shape=jax.ShapeDtypeStruct((s, d), BF16),
        compiler_params=_params(("parallel",)),
    )(x, g)


def _rmsnorm_bwd(x, g, dxn, dres):
    s, d = x.shape
    tr = _tile(s, 1024)

    def body(x_ref, g_ref, dxn_ref, dres_ref, dx_ref, dg_ref):
        @pl.when(pl.program_id(0) == 0)
        def _():
            dg_ref[...] = jnp.zeros_like(dg_ref)

        xv = x_ref[...]
        r = lax.rsqrt(jnp.mean(xv * xv, axis=-1, keepdims=True) + EPS)
        dy = dxn_ref[...]
        t = dy * g_ref[...]
        c = jnp.mean(t * xv, axis=-1, keepdims=True)
        dx_ref[...] = dres_ref[...] + r * t - xv * (r * r * r) * c
        dg_ref[...] += jnp.sum(dy * xv * r, axis=0, keepdims=True)

    row = pl.BlockSpec((tr, d), lambda i: (i, 0))
    vec = pl.BlockSpec((1, d), lambda i: (0, 0))
    return pl.pallas_call(
        body, name="rmsnorm_bwd", grid=(s // tr,),
        in_specs=[row, vec, row, row], out_specs=[row, vec],
        out_shape=[jax.ShapeDtypeStruct((s, d), F32), jax.ShapeDtypeStruct((1, d), F32)],
        compiler_params=_params(("arbitrary",)),
    )(x, g, dxn, dres)


def _final_loss(h, g, tgt):
    s, d = h.shape
    tr = _tile(s, 1024)

    def body(x_ref, g_ref, t_ref, loss_ref, dx_ref, dg_ref):
        @pl.when(pl.program_id(0) == 0)
        def _():
            dg_ref[...] = jnp.zeros_like(dg_ref)
            loss_ref[...] = jnp.zeros_like(loss_ref)

        xv = x_ref[...]
        r = lax.rsqrt(jnp.mean(xv * xv, axis=-1, keepdims=True) + EPS)
        xh = xv * r
        err = xh * g_ref[...] - t_ref[...]
        per_row = jnp.mean(err * err, axis=-1, keepdims=True)
        loss_ref[...] += 0.5 * jnp.sum(per_row, axis=0, keepdims=True)
        dy = err * (1.0 / d)
        t = dy * g_ref[...]
        c = jnp.mean(t * xv, axis=-1, keepdims=True)
        dx_ref[...] = r * t - xv * (r * r * r) * c
        dg_ref[...] += jnp.sum(dy * xh, axis=0, keepdims=True)

    row = pl.BlockSpec((tr, d), lambda i: (i, 0))
    vec = pl.BlockSpec((1, d), lambda i: (0, 0))
    return pl.pallas_call(
        body, name="final_loss", grid=(s // tr,),
        in_specs=[row, vec, row],
        out_specs=[pl.BlockSpec((1, LANE), lambda i: (0, 0)), row, vec],
        out_shape=[jax.ShapeDtypeStruct((1, LANE), F32), jax.ShapeDtypeStruct((s, d), F32),
                   jax.ShapeDtypeStruct((1, d), F32)],
        compiler_params=_params(("arbitrary",)),
    )(h, g, tgt)


CONV_ROWS = 2048
CONV_COLS = LANE
CONV_CHUNK = 64


def _lagged(window, lag):
    return (pltpu.roll(window, lag, 0) if lag else window)[SUB:]


def _led(window, lead):
    n = window.shape[0] - SUB
    return (pltpu.roll(window, window.shape[0] - lead, 0) if lead else window)[:n]


def _taps(shifted, w):
    acc = None
    for k, xs in enumerate(shifted):
        term = xs * w[k:k + 1, :]
        acc = term if acc is None else acc + term
    return acc


def _conv_tiles(s, c):
    return _tile(s, CONV_ROWS), CONV_COLS


def _conv_silu_fwd(pre, w):
    s, c = pre.shape
    width = w.shape[0]
    tr, tc = _conv_tiles(s, c)

    def body(x_ref, w_ref, o_ref, tail):
        @pl.when(pl.program_id(1) == 0)
        def _():
            tail[...] = jnp.zeros_like(tail)

        wv = w_ref[...]

        def do(c0, window):
            y = _taps([_lagged(window, width - 1 - k) for k in range(width)], wv)
            o_ref[pl.ds(c0, CONV_CHUNK), :] = y * _sigmoid(y)

        def chunk(ci, carry):
            c0 = pl.multiple_of(ci * CONV_CHUNK, CONV_CHUNK)
            do(c0, x_ref[pl.ds(pl.multiple_of(c0 - SUB, SUB), CONV_CHUNK + SUB), :])
            return carry

        do(0, jnp.concatenate([tail[...], x_ref[:CONV_CHUNK, :]], axis=0))
        lax.fori_loop(1, tr // CONV_CHUNK, chunk, 0)
        tail[...] = x_ref[tr - SUB:, :]

    blk = pl.BlockSpec((tr, tc), lambda j, i: (i, j))
    return pl.pallas_call(
        body, name="conv_silu_fwd", grid=(c // tc, s // tr),
        in_specs=[blk, pl.BlockSpec((width, tc), lambda j, i: (0, j))], out_specs=blk,
        out_shape=jax.ShapeDtypeStruct((s, c), F32),
        scratch_shapes=[pltpu.VMEM((SUB, tc), F32)],
        compiler_params=_params(("parallel", "arbitrary")),
    )(pre, w)


def _prev_rows_index(i_blk, tr):
    return jnp.maximum(i_blk * (tr // SUB) - 1, 0)


def _conv_silu_bwd(pre, w, dact):
    s, c = pre.shape
    width = w.shape[0]
    tr, tc = _conv_tiles(s, c)
    nr = s // tr

    nchunks = tr // CONV_CHUNK

    def body(x_ref, p_ref, w_ref, d_ref, dx_ref, dw_ref, head):
        @pl.when(pl.program_id(1) == 0)
        def _():
            head[...] = jnp.zeros_like(head)
            dw_ref[...] = jnp.zeros_like(dw_ref)

        wv = w_ref[...]

        def do(c0, window, later, dws):
            xs = [_lagged(window, width - 1 - k) for k in range(width)]
            y = _taps(xs, wv)
            sg = _sigmoid(y)
            dy = d_ref[pl.ds(c0, CONV_CHUNK), :] * sg * (1.0 + y * (1.0 - sg))
            dws = tuple(dw + jnp.sum(dy * x, axis=0, keepdims=True) for dw, x in zip(dws, xs))
            ahead = jnp.concatenate([dy, later], axis=0)
            dx_ref[pl.ds(c0, CONV_CHUNK), :] = _taps([_led(ahead, width - 1 - k) for k in range(width)], wv)
            return dy[:SUB], dws

        def chunk(it, carry):
            c0 = pl.multiple_of((nchunks - 1 - it) * CONV_CHUNK, CONV_CHUNK)
            return do(c0, x_ref[pl.ds(pl.multiple_of(c0 - SUB, SUB), CONV_CHUNK + SUB), :], *carry)

        zero = jnp.zeros((1, tc), F32)
        carry = lax.fori_loop(0, nchunks - 1, chunk, (head[...], (zero,) * width))
        before = jnp.where(pl.program_id(1) == nr - 1, 0.0, p_ref[...])
        later, dws = do(0, jnp.concatenate([before, x_ref[:CONV_CHUNK, :]], axis=0), *carry)
        head[...] = later
        for k in range(width):
            dw_ref[k:k + 1, :] += dws[k]

    blk = pl.BlockSpec((tr, tc), lambda j, i: (nr - 1 - i, j))
    prev = pl.BlockSpec((SUB, tc), lambda j, i: (_prev_rows_index(nr - 1 - i, tr), j))
    wblk = pl.BlockSpec((width, tc), lambda j, i: (0, j))
    return pl.pallas_call(
        body, name="conv_silu_bwd", grid=(c // tc, nr),
        in_specs=[blk, prev, wblk, blk], out_specs=[blk, wblk],
        out_shape=[jax.ShapeDtypeStruct((s, c), F32), jax.ShapeDtypeStruct((width, c), F32)],
        scratch_shapes=[pltpu.VMEM((SUB, tc), F32)],
        compiler_params=_params(("parallel", "arbitrary")),
    )(pre, pre, w, dact)


def _ffn_act_fwd(pre, w, b):
    _, halves, s, c = pre.shape
    width = w.shape[2]
    tr, tc = _conv_tiles(s, c)
    ncb = c // tc

    def body(x_ref, w_ref, b_ref, o_ref, tail):
        @pl.when(pl.program_id(2) == 0)
        def _():
            tail[...] = jnp.zeros_like(tail)

        wg, wv, bg, bv = w_ref[0], w_ref[1], b_ref[0], b_ref[1]

        def do(c0, win_g, win_v):
            yg = _taps([_lagged(win_g, width - 1 - k) for k in range(width)], wg) + bg
            yv = _taps([_lagged(win_v, width - 1 - k) for k in range(width)], wv) + bv
            o_ref[pl.ds(c0, CONV_CHUNK), :] = (yg * _sigmoid(yg) * yv).astype(BF16)

        def chunk(ci, carry):
            c0 = pl.multiple_of(ci * CONV_CHUNK, CONV_CHUNK)
            rows = pl.ds(pl.multiple_of(c0 - SUB, SUB), CONV_CHUNK + SUB)
            do(c0, x_ref[0, rows, :], x_ref[1, rows, :])
            return carry

        do(0, jnp.concatenate([tail[0], x_ref[0, :CONV_CHUNK, :]], axis=0),
           jnp.concatenate([tail[1], x_ref[1, :CONV_CHUNK, :]], axis=0))
        lax.fori_loop(1, tr // CONV_CHUNK, chunk, 0)
        tail[...] = x_ref[:, tr - SUB:, :]

    return pl.pallas_call(
        body, name="ffn_act_fwd", grid=(halves, ncb, s // tr),
        in_specs=[pl.BlockSpec((2, None, tr, tc), lambda h, j, i: (0, h, i, j)),
                  pl.BlockSpec((2, None, width, tc), lambda h, j, i: (0, h, 0, j)),
                  pl.BlockSpec((2, None, 1, tc), lambda h, j, i: (0, h, 0, j))],
        out_specs=pl.BlockSpec((tr, tc), lambda h, j, i: (i, h * ncb + j)),
        out_shape=jax.ShapeDtypeStruct((s, halves * c), BF16),
        scratch_shapes=[pltpu.VMEM((2, SUB, tc), F32)],
        compiler_params=_params(("parallel", "parallel", "arbitrary")),
    )(pre, w, b)


def _ffn_act_bwd(pre, w, b, dact):
    _, halves, s, c = pre.shape
    width = w.shape[2]
    tr, tc = _conv_tiles(s, c)
    ncb = c // tc
    nr = s // tr
    nchunks = tr // CONV_CHUNK

    def body(x_ref, p_ref, w_ref, b_ref, d_ref, dx_ref, dw_ref, db_ref, head):
        @pl.when(pl.program_id(2) == 0)
        def _():
            for r in (head, dw_ref, db_ref):
                r[...] = jnp.zeros_like(r)

        wg, wv, bg, bv = w_ref[0], w_ref[1], b_ref[0], b_ref[1]

        def do(c0, win_g, win_v, later_g, later_v, dwg, dwv, dbg, dbv):
            xg = [_lagged(win_g, width - 1 - k) for k in range(width)]
            xv = [_lagged(win_v, width - 1 - k) for k in range(width)]
            yg = _taps(xg, wg) + bg
            yv = _taps(xv, wv) + bv
            sg = _sigmoid(yg)
            da = d_ref[pl.ds(c0, CONV_CHUNK), :]
            dyv = da * yg * sg
            dyg = da * yv * sg * (1.0 + yg * (1.0 - sg))
            dwg = tuple(dw + jnp.sum(dyg * x, axis=0, keepdims=True) for dw, x in zip(dwg, xg))
            dwv = tuple(dw + jnp.sum(dyv * x, axis=0, keepdims=True) for dw, x in zip(dwv, xv))
            dbg = dbg + jnp.sum(dyg, axis=0, keepdims=True)
            dbv = dbv + jnp.sum(dyv, axis=0, keepdims=True)
            ahead_g = jnp.concatenate([dyg, later_g], axis=0)
            ahead_v = jnp.concatenate([dyv, later_v], axis=0)
            dx_ref[0, pl.ds(c0, CONV_CHUNK), :] = _taps([_led(ahead_g, width - 1 - k) for k in range(width)], wg)
            dx_ref[1, pl.ds(c0, CONV_CHUNK), :] = _taps([_led(ahead_v, width - 1 - k) for k in range(width)], wv)
            return dyg[:SUB], dyv[:SUB], dwg, dwv, dbg, dbv

        def chunk(it, carry):
            c0 = pl.multiple_of((nchunks - 1 - it) * CONV_CHUNK, CONV_CHUNK)
            rows = pl.ds(pl.multiple_of(c0 - SUB, SUB), CONV_CHUNK + SUB)
            return do(c0, x_ref[0, rows, :], x_ref[1, rows, :], *carry)

        zero = jnp.zeros((1, tc), F32)
        carry = lax.fori_loop(0, nchunks - 1, chunk,
                              (head[0], head[1], (zero,) * width, (zero,) * width, zero, zero))
        before = jnp.where(pl.program_id(2) == nr - 1, 0.0, p_ref[...])
        later_g, later_v, dwg, dwv, dbg, dbv = do(
            0, jnp.concatenate([before[0], x_ref[0, :CONV_CHUNK, :]], axis=0),
            jnp.concatenate([before[1], x_ref[1, :CONV_CHUNK, :]], axis=0), *carry)
        head[0] = later_g
        head[1] = later_v
        db_ref[0] += dbg
        db_ref[1] += dbv
        for k in range(width):
            dw_ref[0, k:k + 1, :] += dwg[k]
            dw_ref[1, k:k + 1, :] += dwv[k]

    blk = pl.BlockSpec((2, None, tr, tc), lambda h, j, i: (0, h, nr - 1 - i, j))
    prev = pl.BlockSpec((2, None, SUB, tc), lambda h, j, i: (0, h, _prev_rows_index(nr - 1 - i, tr), j))
    wblk = pl.BlockSpec((2, None, width, tc), lambda h, j, i: (0, h, 0, j))
    bblk = pl.BlockSpec((2, None, 1, tc), lambda h, j, i: (0, h, 0, j))
    return pl.pallas_call(
        body, name="ffn_act_bwd", grid=(halves, ncb, nr),
        in_specs=[blk, prev, wblk, bblk, pl.BlockSpec((tr, tc), lambda h, j, i: (nr - 1 - i, h * ncb + j))],
        out_specs=[blk, wblk, bblk],
        out_shape=[jax.ShapeDtypeStruct(pre.shape, F32), jax.ShapeDtypeStruct(w.shape, F32),
                   jax.ShapeDtypeStruct(b.shape, F32)],
        scratch_shapes=[pltpu.VMEM((2, SUB, tc), F32)],
        compiler_params=_params(("parallel", "parallel", "arbitrary")),
    )(pre, pre, w, b, dact)


def _tri_masks():
    row = lax.broadcasted_iota(jnp.int32, (CHUNK, CHUNK), 0)
    col = lax.broadcasted_iota(jnp.int32, (CHUNK, CHUNK), 1)
    return row, col


def _tri_inv(ms, row, col):
    eye = (row == col).astype(F32)
    same_blk = (row >> 4) == (col >> 4)
    mds = [jnp.where(same_blk, m, 0.0) for m in ms]
    offs = [m - md for m, md in zip(ms, mds)]
    xs = [eye - md for md in mds]
    ps = [_bdot(md, md, NN) for md in mds]
    for _ in range(2):
        rs = [_bdot(jnp.concatenate([x, p], axis=0), p, NN) for x, p in zip(xs, ps)]
        xs = [x + r[:CHUNK] for x, r in zip(xs, rs)]
        ps = [r[CHUNK:] for r in rs]
    xs = [x + _bdot(x, p, NN) for x, p in zip(xs, ps)]
    ps = [_bdot(x, off, NN) for x, off in zip(xs, offs)]
    pps = [_bdot(p, p, NN) for p in ps]
    ys = [eye - p for p in ps]
    ys = [y + _bdot(y, pp, NN) for y, pp in zip(ys, pps)]
    return [_bdot(y, x, NN) for y, x in zip(ys, xs)]


def _gdn_gates(ba, alog, dtb, row, col):
    sig = _sigmoid(ba)
    neg_a = -jnp.exp(alog)
    g = neg_a * _softplus(ba + dtb)
    lower = (row >= col).astype(F32)
    gcum = _dot(lower, g, NN, HIGHEST)
    return sig, neg_a, g, gcum


def _gdn_head_common(q_raw, k_raw, v, beta, gc, gr, row, col):
    causal = row >= col
    strict = row > col
    rq = lax.rsqrt(jnp.sum(q_raw * q_raw, axis=-1, keepdims=True) + EPS)
    rk = lax.rsqrt(jnp.sum(k_raw * k_raw, axis=-1, keepdims=True) + EPS)
    q = q_raw * (rq * (A_HEAD ** -0.5))
    k = k_raw * rk
    decay = jnp.where(causal, jnp.exp(jnp.where(causal, gc - gr, 0.0)), 0.0)
    eg = jnp.exp(gc)
    gl = gc[CHUNK - 1:CHUNK, :]
    ekl = jnp.exp(gl - gc)
    dec = jnp.exp(gl)
    kb = k * beta
    kbq = jnp.concatenate([kb, q], axis=0)
    both = _bdot(kbq, k, NT)
    kk, qk = both[:CHUNK], both[CHUNK:]
    a = jnp.where(causal, qk * decay, 0.0)
    return dict(rq=rq, rk=rk, q=q, k=k, decay=decay, eg=eg, ekl=ekl, dec=dec, kb=kb, kbq=kbq, kk=kk, qk=qk, a=a,
                vb=v * beta, kbg=kb * eg, qd=q * eg, ke=k * ekl, causal=causal, strict=strict)


def _gdn_fwd(qkv, ba, z, alog, dtb, wn):
    s = qkv.shape[0]
    nc = s // CHUNK

    def body(qkv_ref, ba_ref, z_ref, alog_ref, dtb_ref, wn_ref, y_ref, o_ref, st_ref, t_ref, w_ref, vn_ref, state):
        @pl.when(pl.program_id(0) == 0)
        def _():
            state[...] = jnp.zeros_like(state)

        row, col = _tri_masks()
        sig, _, _, gcum = _gdn_gates(ba_ref[...], alog_ref[...], dtb_ref[...], row, col)
        gt = gcum.T
        heads = range(A_HEADS)
        lanes = [slice(h * A_HEAD, (h + 1) * A_HEAD) for h in heads]
        fs = [_gdn_head_common(qkv_ref[:, lanes[h]], qkv_ref[:, A_QK + h * A_HEAD:A_QK + (h + 1) * A_HEAD],
                               qkv_ref[:, 2 * A_QK + h * A_HEAD:2 * A_QK + (h + 1) * A_HEAD],
                               sig[:, h:h + 1], gcum[:, 8 + h:9 + h], gt[8 + h:9 + h, :], row, col) for h in heads]
        ts = [t.astype(BF16) for t in
              _tri_inv([jnp.where(f["strict"], f["kk"] * f["decay"], 0.0) for f in fs], row, col)]
        uws = [_bdot(t, jnp.concatenate([f["vb"], f["kbg"]], axis=1), NN) for t, f in zip(ts, fs)]
        s0s = [state[h] for h in heads]
        ws_ = [uw[:, A_HEAD:].astype(BF16) for uw in uws]
        wss = [_bdot(jnp.concatenate([w, f["qd"].astype(BF16)], axis=0), s0, NN) for w, f, s0 in zip(ws_, fs, s0s)]
        vnews = [(uw[:, :A_HEAD] - wsq[:CHUNK]).astype(BF16) for uw, wsq in zip(uws, wss)]
        os_ = [wsq[CHUNK:] + _bdot(f["a"], vn, NN) for wsq, f, vn in zip(wss, fs, vnews)]
        s1s = [s0 * f["dec"] + _bdot(f["ke"], vn, TN) for s0, f, vn in zip(s0s, fs, vnews)]
        for h in heads:
            ln = lanes[h]
            st_ref[0, h] = s0s[h]
            t_ref[0, h] = ts[h]
            w_ref[:, ln] = ws_[h]
            vn_ref[:, ln] = vnews[h]
            state[h] = s1s[h]
            o = os_[h]
            o_ref[:, ln] = o
            r = lax.rsqrt(jnp.mean(o * o, axis=-1, keepdims=True) + EPS)
            zz = z_ref[:, ln]
            y_ref[:, ln] = (o * r * wn_ref[...] * zz * _sigmoid(zz)).astype(BF16)

    vec = pl.BlockSpec((1, LANE), lambda n: (0, 0))
    wide = pl.BlockSpec((CHUNK, A_QK), lambda n: (n, 0))
    return pl.pallas_call(
        body, name="gdn_fwd", grid=(nc,),
        in_specs=[pl.BlockSpec((CHUNK, A_CONV_WIDTH), lambda n: (n, 0)),
                  pl.BlockSpec((CHUNK, LANE), lambda n: (n, 0)), wide, vec, vec, vec],
        out_specs=[wide, wide, pl.BlockSpec((1, A_HEADS, A_HEAD, A_HEAD), lambda n: (n, 0, 0, 0)),
                   pl.BlockSpec((1, A_HEADS, CHUNK, CHUNK), lambda n: (n, 0, 0, 0)), wide, wide],
        out_shape=[jax.ShapeDtypeStruct((s, A_QK), BF16), jax.ShapeDtypeStruct((s, A_QK), F32),
                   jax.ShapeDtypeStruct((nc, A_HEADS, A_HEAD, A_HEAD), F32),
                   jax.ShapeDtypeStruct((nc, A_HEADS, CHUNK, CHUNK), BF16),
                   jax.ShapeDtypeStruct((s, A_QK), BF16), jax.ShapeDtypeStruct((s, A_QK), BF16)],
        scratch_shapes=[pltpu.VMEM((A_HEADS, A_HEAD, A_HEAD), F32)],
        compiler_params=_params(("arbitrary",)),
    )(qkv, ba, z, alog, dtb, wn)


def _gdn_bwd(qkv, ba, z, o_raw, dy, states, t_all, w_all, vn_all, alog, dtb, wn):
    s = qkv.shape[0]
    nc = s // CHUNK

    def body(qkv_ref, ba_ref, z_ref, o_ref, dy_ref, st_ref, t_ref, w_ref, vn_ref, alog_ref, dtb_ref, wn_ref,
             dqkv_ref, dba_ref, dz_ref, dalog_ref, ddtb_ref, dwn_ref, dstate):
        @pl.when(pl.program_id(0) == 0)
        def _():
            for r in (dstate, dalog_ref, ddtb_ref, dwn_ref):
                r[...] = jnp.zeros_like(r)

        row, col = _tri_masks()
        bat = ba_ref[...]
        sig, neg_a, g, gcum = _gdn_gates(bat, alog_ref[...], dtb_ref[...], row, col)
        gt = gcum.T
        lane = lax.broadcasted_iota(jnp.int32, (CHUNK, LANE), 1)
        ones = jnp.ones((CHUNK, LANE), F32)
        last_row = lax.broadcasted_iota(jnp.int32, (CHUNK, 1), 0) == CHUNK - 1
        wnv = wn_ref[...]
        dgc_tile = jnp.zeros((CHUNK, LANE), F32)
        dbeta_tile = jnp.zeros((CHUNK, LANE), F32)
        dwn_acc = jnp.zeros((1, LANE), F32)
        hs = []
        for h in range(A_HEADS):
            ln = slice(h * A_HEAD, (h + 1) * A_HEAD)
            lk = slice(A_QK + h * A_HEAD, A_QK + (h + 1) * A_HEAD)
            lv = slice(2 * A_QK + h * A_HEAD, 2 * A_QK + (h + 1) * A_HEAD)
            q_raw, k_raw, v = qkv_ref[:, ln], qkv_ref[:, lk], qkv_ref[:, lv]
            f = _gdn_head_common(q_raw, k_raw, v, sig[:, h:h + 1], gcum[:, 8 + h:9 + h], gt[8 + h:9 + h, :], row, col)
            f.update(h=h, ln=ln, lk=lk, lv=lv, q_raw=q_raw, k_raw=k_raw, v=v, beta=sig[:, h:h + 1],
                     s0=st_ref[0, h], ds1=dstate[h], t=t_ref[0, h], w=w_ref[:, ln], vnew=vn_ref[:, ln])
            o = o_ref[:, ln]
            zz = z_ref[:, ln]
            dyv = dy_ref[:, ln]
            r = lax.rsqrt(jnp.mean(o * o, axis=-1, keepdims=True) + EPS)
            sz = _sigmoid(zz)
            silu = zz * sz
            dz_ref[:, ln] = dyv * o * r * wnv * sz * (1.0 + zz * (1.0 - sz))
            dwn_acc = dwn_acc + jnp.sum(dyv * silu * o * r, axis=0, keepdims=True)
            tt = dyv * silu * wnv
            do = r * tt - o * (r * r * r) * jnp.mean(tt * o, axis=-1, keepdims=True)
            f["do_b"] = do.astype(BF16)
            hs.append(f)
        for f in hs:
            f["dvnew"] = _bdot(f["a"], f["do_b"], TN) + _bdot(f["ke"], f["ds1"], NN)
            f["da"] = jnp.where(f["causal"], _bdot(f["do_b"], f["vnew"], NT), 0.0)
            f["dke"] = _bdot(f["vnew"], f["ds1"], NT)
            f["ddec"] = jnp.sum(jnp.sum(f["s0"] * f["ds1"], axis=1, keepdims=True), axis=0, keepdims=True)
        for f in hs:
            do_dv = jnp.concatenate([f["do_b"], f["dvnew"].astype(BF16)], axis=0)
            both = _bdot(do_dv, f["s0"], NT)
            f["dqd"], f["dw"] = both[:CHUNK], -both[CHUNK:]
            qd_w = jnp.concatenate([f["qd"].astype(BF16), -f["w"]], axis=0)
            dstate[f["h"]] = _bdot(qd_w, do_dv, TN) + f["dec"] * f["ds1"]
        for f in hs:
            dd = jnp.concatenate([f["dvnew"], f["dw"]], axis=1).astype(BF16)
            tdd = _bdot(f["t"], dd, TN)
            f["dvb"], f["dkbg"] = tdd[:, :A_HEAD], tdd[:, A_HEAD:]
            f["dt"] = _bdot(dd, jnp.concatenate([f["vb"], f["kbg"]], axis=1), NT)
        for f in hs:
            f["tdt"] = _bdot(f["t"], f["dt"], TN)
        for f in hs:
            dm = jnp.where(f["strict"], -_bdot(f["tdt"], f["t"], NT), 0.0)
            f["ddecay"] = (dm * f["kk"] + f["da"] * f["qk"]) * f["decay"]
            f["dboth"] = jnp.concatenate([dm * f["decay"], f["da"] * f["decay"]], axis=0).astype(BF16)
        for f in hs:
            f["r2"] = _bdot(f["dboth"], f["k"], NN)
            f["dk0"] = _bdot(f["dboth"], f["kbq"], TN)
        for f in hs:
            h, k, beta = f["h"], f["k"], f["beta"]
            dkb = f["r2"][:CHUNK] + f["dkbg"] * f["eg"]
            dq = f["r2"][CHUNK:] + f["dqd"] * f["eg"]
            dk = f["dk0"] + f["dke"] * f["ekl"] + dkb * beta
            dke_ke = jnp.sum(f["dke"] * f["ke"], axis=-1, keepdims=True)
            dgc = (jnp.sum(f["ddecay"], axis=-1, keepdims=True)
                   + jnp.sum(f["dqd"] * f["qd"], axis=-1, keepdims=True) - dke_ke
                   + jnp.sum(f["dkbg"] * f["kbg"], axis=-1, keepdims=True))
            dgl = jnp.sum(dke_ke, axis=0, keepdims=True) + f["ddec"] * f["dec"]
            dgc = dgc + jnp.where(last_row, dgl, 0.0)
            dbeta = jnp.sum(dkb * k, axis=-1, keepdims=True) + jnp.sum(f["dvb"] * f["v"], axis=-1, keepdims=True)
            dgc_tile = dgc_tile + jnp.where(lane == 8 + h, dgc, 0.0)
            dbeta_tile = dbeta_tile + jnp.where(lane == h, dbeta, 0.0)
            dqn = dq * (A_HEAD ** -0.5)
            rq, rk, q_raw, k_raw = f["rq"], f["rk"], f["q_raw"], f["k_raw"]
            dqkv_ref[:, f["ln"]] = rq * dqn - q_raw * (rq * rq * rq) * jnp.sum(dqn * q_raw, axis=-1, keepdims=True)
            dqkv_ref[:, f["lk"]] = rk * dk - k_raw * (rk * rk * rk) * jnp.sum(dk * k_raw, axis=-1, keepdims=True)
            dqkv_ref[:, f["lv"]] = f["dvb"] * beta
        ddecays = [f["ddecay"] for f in hs]
        col_sums = _dot(jnp.concatenate(ddecays, axis=1), ones, TN, HIGHEST)
        for h in range(A_HEADS):
            dgc_tile = dgc_tile - jnp.where(lane == 8 + h, col_sums[h * CHUNK:(h + 1) * CHUNK, :1], 0.0)
        upper = (row <= col).astype(F32)
        dg = _dot(upper, dgc_tile, NN, HIGHEST)
        da_raw = dg * neg_a * _sigmoid(bat + dtb_ref[...])
        dba_ref[...] = jnp.where(lane < 8, dbeta_tile * sig * (1.0 - sig), jnp.where(lane < 16, da_raw, 0.0))
        dwn_ref[...] += dwn_acc
        ddtb_ref[...] += jnp.sum(da_raw, axis=0, keepdims=True)
        dalog_ref[...] += jnp.sum(dg * g, axis=0, keepdims=True)

    rev = lambda n: (nc - 1 - n, 0)
    vec = pl.BlockSpec((1, LANE), lambda n: (0, 0))
    wide = pl.BlockSpec((CHUNK, A_QK), rev)
    qkv_blk = pl.BlockSpec((CHUNK, A_CONV_WIDTH), rev)
    ba_blk = pl.BlockSpec((CHUNK, LANE), rev)
    vsh = jax.ShapeDtypeStruct((1, LANE), F32)
    return pl.pallas_call(
        body, name="gdn_bwd", grid=(nc,),
        in_specs=[qkv_blk, ba_blk, wide, wide, wide,
                  pl.BlockSpec((1, A_HEADS, A_HEAD, A_HEAD), lambda n: (nc - 1 - n, 0, 0, 0)),
                  pl.BlockSpec((1, A_HEADS, CHUNK, CHUNK), lambda n: (nc - 1 - n, 0, 0, 0)), wide, wide,
                  vec, vec, vec],
        out_specs=[qkv_blk, ba_blk, wide, vec, vec, vec],
        out_shape=[jax.ShapeDtypeStruct((s, A_CONV_WIDTH), F32), jax.ShapeDtypeStruct((s, LANE), F32),
                   jax.ShapeDtypeStruct((s, A_QK), F32), vsh, vsh, vsh],
        scratch_shapes=[pltpu.VMEM((A_HEADS, A_HEAD, A_HEAD), F32)],
        compiler_params=_params(("arbitrary",)),
    )(qkv, ba, z, o_raw, dy, states, t_all, w_all, vn_all, alog, dtb, wn)


REL_RING = 1024
QBLK_BITS = 8


def _rel_ring_onehot():
    m = lax.broadcasted_iota(jnp.int32, (REL_RING, REL_PAD), 0)
    t = lax.broadcasted_iota(jnp.int32, (REL_RING, REL_PAD), 1)
    u = jnp.where(m < KBLK, m, m - REL_RING)
    idx = jnp.clip(LEFT - u, -REL_CLIP, REL_CLIP) + REL_CLIP
    return (t == idx).astype(F32)


def _relbias_ring(table, transpose):
    n_in, n_out = (REL_RING, REL_PAD) if transpose else (REL_PAD, REL_RING)

    def body(t_ref, o_ref):
        o_ref[...] = _dot(t_ref[...], _rel_ring_onehot(), NN if transpose else NT, HIGHEST)

    return pl.pallas_call(
        body, name="relbias_ring_bwd" if transpose else "relbias_ring",
        out_shape=jax.ShapeDtypeStruct((B_HEADS, n_out), F32),
        in_specs=[VMEM_FULL], out_specs=VMEM_FULL,
        compiler_params=_params(),
    )(table)


def _row_bit(shape, bit):
    return ((lax.broadcasted_iota(jnp.int32, shape, 0) >> bit) & 1) == 1


def _relbias_expand(ring):
    def body(r_ref, o_ref):
        b = jnp.broadcast_to(r_ref[0], (QBLK, REL_RING))
        for bit in range(QBLK_BITS):
            b = jnp.where(_row_bit(b.shape, bit), pltpu.roll(b, 1 << bit, 1), b)
        j = lax.broadcasted_iota(jnp.int32, (QBLK, KBLK), 1)
        r = lax.broadcasted_iota(jnp.int32, (QBLK, KBLK), 0)
        lo = (r >> 6) << 6
        o_ref[0] = jnp.where((j >= lo) & (j < lo + LEFT + CHUNK), b[:, :KBLK], NEG_INF)

    return pl.pallas_call(
        body, name="relbias_expand", grid=(B_HEADS,),
        in_specs=[pl.BlockSpec((1, 1, REL_RING), lambda h: (h, 0, 0))],
        out_specs=pl.BlockSpec((1, QBLK, KBLK), lambda h: (h, 0, 0)),
        out_shape=jax.ShapeDtypeStruct((B_HEADS, QBLK, KBLK), F32),
        compiler_params=_params(("parallel",)),
    )(ring)


def _relbias_reduce(ds):
    def body(d_ref, o_ref):
        d = jnp.concatenate([d_ref[0], jnp.zeros((QBLK, REL_RING - KBLK), F32)], axis=1)
        for bit in range(QBLK_BITS):
            d = jnp.where(_row_bit(d.shape, bit), pltpu.roll(d, REL_RING - (1 << bit), 1), d)
        o_ref[0] = jnp.sum(d, axis=0, keepdims=True)

    return pl.pallas_call(
        body, name="relbias_reduce", grid=(B_HEADS,),
        in_specs=[pl.BlockSpec((1, QBLK, KBLK), lambda h: (h, 0, 0))],
        out_specs=pl.BlockSpec((1, 1, REL_RING), lambda h: (h, 0, 0)),
        out_shape=jax.ShapeDtypeStruct((B_HEADS, 1, REL_RING), F32),
        compiler_params=_params(("parallel",)),
    )(ds)


def _attn_probs(q_ref, kb, b_ref, hh, q0):
    hl = slice(hh * B_HEAD, (hh + 1) * B_HEAD)
    qh = q_ref[:, hl]
    kh = kb[:, hl]
    jpos = lax.broadcasted_iota(jnp.int32, (QBLK, KBLK), 1)
    sc = _bdot(qh, kh, NT) * (B_HEAD ** -0.5) + b_ref[hh]
    sc = jnp.where(jpos + q0 >= LEFT, sc, NEG_INF)
    mx = jnp.max(sc, axis=-1, keepdims=True)
    p = jnp.exp(sc - mx)
    return p / jnp.sum(p, axis=-1, keepdims=True), qh, kh


def _attn_fwd(q, kpad, vpad, bias):
    s = q.shape[0]

    def body(q_ref, k_ref, v_ref, b_ref, o_ref):
        q0 = pl.multiple_of(pl.program_id(1) * QBLK, QBLK)
        kb = k_ref[pl.ds(q0, KBLK), :]
        vb = v_ref[pl.ds(q0, KBLK), :]
        outs = []
        for hh in range(2):
            p, _, _ = _attn_probs(q_ref, kb, b_ref, hh, q0)
            outs.append(_bdot(p, vb[:, hh * B_HEAD:(hh + 1) * B_HEAD], NN))
        o_ref[...] = jnp.concatenate(outs, axis=1).astype(BF16)

    qblk = pl.BlockSpec((QBLK, LANE), lambda g, m: (m, g))
    kblk = pl.BlockSpec((LEFT + s, LANE), lambda g, m: (0, g))
    return pl.pallas_call(
        body, name="attn_fwd", grid=(B_HEADS // 2, s // QBLK),
        in_specs=[qblk, kblk, kblk, pl.BlockSpec((2, QBLK, KBLK), lambda g, m: (g, 0, 0))],
        out_specs=qblk,
        out_shape=jax.ShapeDtypeStruct((s, D_MODEL), BF16),
        compiler_params=_params(("parallel", "arbitrary")),
    )(q, kpad, vpad, bias)


def _attn_bwd(q, kpad, vpad, bias, do):
    s = q.shape[0]

    def body(q_ref, k_ref, v_ref, b_ref, do_ref, dq_ref, dk_ref, dv_ref, db_ref):
        @pl.when(pl.program_id(1) == 0)
        def _():
            for r in (dk_ref, dv_ref, db_ref):
                r[...] = jnp.zeros_like(r)

        q0 = pl.multiple_of(pl.program_id(1) * QBLK, QBLK)
        kb = k_ref[pl.ds(q0, KBLK), :]
        vb = v_ref[pl.ds(q0, KBLK), :]
        dqs, dks, dvs = [], [], []
        for hh in range(2):
            hl = slice(hh * B_HEAD, (hh + 1) * B_HEAD)
            p, qh, kh = _attn_probs(q_ref, kb, b_ref, hh, q0)
            doh = do_ref[:, hl]
            dp = _bdot(doh, vb[:, hl], NT)
            dsc = p * (dp - jnp.sum(p * dp, axis=-1, keepdims=True))
            db_ref[hh] += dsc
            dqs.append(_bdot(dsc, kh, NN) * (B_HEAD ** -0.5))
            dks.append(_bdot(dsc, qh, TN) * (B_HEAD ** -0.5))
            dvs.append(_bdot(p, doh, TN))
        dq_ref[...] = jnp.concatenate(dqs, axis=1)
        dk_ref[pl.ds(q0, KBLK), :] += jnp.concatenate(dks, axis=1)
        dv_ref[pl.ds(q0, KBLK), :] += jnp.concatenate(dvs, axis=1)

    qblk = pl.BlockSpec((QBLK, LANE), lambda g, m: (m, g))
    kblk = pl.BlockSpec((LEFT + s, LANE), lambda g, m: (0, g))
    bblk = pl.BlockSpec((2, QBLK, KBLK), lambda g, m: (g, 0, 0))
    return pl.pallas_call(
        body, name="attn_bwd", grid=(B_HEADS // 2, s // QBLK),
        in_specs=[qblk, kblk, kblk, bblk, qblk],
        out_specs=[qblk, kblk, kblk, bblk],
        out_shape=[jax.ShapeDtypeStruct((s, D_MODEL), F32), jax.ShapeDtypeStruct((LEFT + s, D_MODEL), F32),
                   jax.ShapeDtypeStruct((LEFT + s, D_MODEL), F32),
                   jax.ShapeDtypeStruct((B_HEADS, QBLK, KBLK), F32)],
        compiler_params=_params(("parallel", "arbitrary")),
    )(q, kpad, vpad, bias, do)


def _adamw(w, g, m, v):
    r, c = w.shape
    tr = r
    for cand in (512, 256, 128, 64, 32, 16, 8):
        if r % cand == 0 and cand * c * 4 <= 2 * 1024 * 1024:
            tr = cand
            break
    c1 = 1.0 / (1.0 - ADAM_B1 ** ADAM_STEP)
    c2 = 1.0 / (1.0 - ADAM_B2 ** ADAM_STEP)

    def body(w_ref, g_ref, m_ref, v_ref, d_ref, mo_ref, vo_ref):
        gv = g_ref[...]
        mn = ADAM_B1 * m_ref[...] + (1.0 - ADAM_B1) * gv
        vn = ADAM_B2 * v_ref[...] + (1.0 - ADAM_B2) * (gv * gv)
        mo_ref[...] = mn
        vo_ref[...] = vn
        d_ref[...] = -ADAM_LR * ((mn * c1) / (jnp.sqrt(vn * c2) + ADAM_EPS) + ADAM_WD * w_ref[...])

    blk = pl.BlockSpec((tr, c), lambda i: (i, 0))
    sh = jax.ShapeDtypeStruct((r, c), F32)
    return pl.pallas_call(
        body, name="adamw", grid=(r // tr,),
        in_specs=[blk] * 4, out_specs=[blk] * 3, out_shape=[sh] * 3,
        compiler_params=_params(("parallel",)),
    )(w, g, m, v)


def _row(v, width=None):
    v = v.reshape(1, -1)
    if width is not None and v.shape[1] < width:
        v = jnp.pad(v, ((0, 0), (0, width - v.shape[1])))
    return v


def _gate_row(v):
    return jnp.pad(v.reshape(1, A_HEADS), ((0, 0), (A_HEADS, LANE - 2 * A_HEADS)))


DEPTH = 4
N_A = 2
N_B = 2
F_DOWN_ROWS = FFN_DIM // N_SHARDS
SQ_ROWS = D_MODEL // N_SHARDS
GD_A_OUT0 = DEPTH * F_DOWN_ROWS // SQ_ROWS
GD_B_Q0 = GD_A_OUT0 + N_A
GD_B_OUT0 = GD_B_Q0 + N_B
UP_COLS = 2 * FFN_DIM // N_SHARDS


def _a_layer_fwd(h, w, i):
    xn = _rmsnorm_fwd(h, _row(w["a_norm"][i]))
    w_qkv, w_z, w_ba = w["a_in"][i]
    pre = _mm(xn, w_qkv, name="a_qkv")
    z = _mm(xn, w_z, name="a_z")
    ba = _mm(xn, w_ba, name="a_ba")
    act = _conv_silu_fwd(pre, w["a_conv"][i])
    alog, dtb, wn = _gate_row(w["a_A_log"][i]), _gate_row(w["a_dt_bias"][i]), _row(w["a_out_norm"][i])
    y, o_raw, states, t_all, w_all, vn_all = _gdn_fwd(act, ba, z, alog, dtb, wn)
    h2 = _mm_rowsh(y, w["GD"], SQ_ROWS, GD_A_OUT0 + i, "nn", "a_out", res=h)
    saved = dict(h=h, xn=xn, pre=pre, z=z, ba=ba, act=act, o_raw=o_raw, y=y, states=states,
                 t_all=t_all, w_all=w_all, vn_all=vn_all, alog=alog, dtb=dtb, wn=wn)
    return h2, saved


def _a_layer_bwd(dh2, w, i, sv):
    g = {}
    g["w_out"] = _mm_rowsh_dw(sv["y"], dh2, SQ_ROWS, "a_out_dw")
    dy = _mm_rowsh(dh2, w["GD"], SQ_ROWS, GD_A_OUT0 + i, "nt", "a_out_dx")
    dact, dba, dz, dalog, ddtb, dwn = _gdn_bwd(sv["act"], sv["ba"], sv["z"], sv["o_raw"], dy, sv["states"],
                                               sv["t_all"], sv["w_all"], sv["vn_all"],
                                               sv["alog"], sv["dtb"], sv["wn"])
    dpre, g["conv"] = _conv_silu_bwd(sv["pre"], w["a_conv"][i], dact)
    xn = sv["xn"]
    w_qkv, w_z, w_ba = w["a_in"][i]
    d_in = jnp.concatenate([_mm(xn, dpre, "tn", name="a_qkv_dw"), _mm(xn, dz, "tn", name="a_z_dw"),
                            _mm(xn, dba, "tn", name="a_ba_dw")[:, :2 * A_HEADS]], axis=1)
    g["w_in"] = jnp.transpose(d_in.reshape(D_MODEL, N_SHARDS, -1), (1, 0, 2))
    dxn = _mm(dpre, w_qkv, "nt", name="a_qkv_dx")
    dxn = _mm(dz, w_z, "nt", res=dxn, name="a_z_dx")
    dxn = _mm(dba, w_ba, "nt", res=dxn, name="a_ba_dx")
    dh, dnorm = _rmsnorm_bwd(sv["h"], _row(w["a_norm"][i]), dxn, dh2)
    g["norm"] = dnorm[0]
    g["A_log"] = dalog[0, A_HEADS:2 * A_HEADS]
    g["dt_bias"] = ddtb[0, A_HEADS:2 * A_HEADS]
    g["out_norm"] = dwn[0]
    return dh, g


def _by_half(a, lead):
    return jnp.moveaxis(a.reshape(a.shape[:-1] + (2, 2, UP_COLS)), (-3, -2), (0, 1)).reshape((2, 2) + lead + (UP_COLS,))


def _from_half(a):
    lead = a.shape[2:-1]
    return jnp.moveaxis(a, (0, 1), (-3, -2)).reshape(lead + (N_SHARDS * UP_COLS,))


def _ffn_fwd(h, w, l):
    s = h.shape[0]
    xn = _rmsnorm_fwd(h, _row(w["f_norm"][l]))
    cw = _by_half(w["f_conv"][l], (w["f_conv"].shape[1],))
    cb = _by_half(w["f_conv_b"][l][None], (1,))
    pre = _mm_colsh(xn, w["GU"], D_MODEL, l, "nn", "f_up").reshape(2, 2, s, UP_COLS)
    act = _ffn_act_fwd(pre, cw, cb)
    h2 = _mm_rowsh(act, w["GD"], F_DOWN_ROWS, l, "nn", "f_down", res=h)
    return h2, dict(h=h, xn=xn, pre=pre, act=act, cw=cw, cb=cb)


def _ffn_bwd(dh2, w, l, sv):
    g = {}
    s = dh2.shape[0]
    g["w_down"] = _mm_rowsh_dw(sv["act"], dh2, F_DOWN_ROWS, "f_down_dw")
    dact = _mm_rowsh(dh2, w["GD"], F_DOWN_ROWS, l, "nt", "f_down_dx")
    dpre, dcw, dcb = _ffn_act_bwd(sv["pre"], sv["cw"], sv["cb"], dact)
    dpre = dpre.reshape(N_SHARDS, s, UP_COLS)
    g["w_up"] = _mm_colsh_dw(sv["xn"], dpre, "f_up_dw")
    g["conv"] = _from_half(dcw)
    g["conv_b"] = _from_half(dcb)[0]
    dxn = _mm_colsh(dpre, w["GU"], D_MODEL, l, "nt", "f_up_dx")
    dh, dnorm = _rmsnorm_bwd(sv["h"], _row(w["f_norm"][l]), dxn, dh2)
    g["norm"] = dnorm[0]
    return dh, g


def _b_layer_fwd(h, w, j, kpad, vpad):
    xn = _rmsnorm_fwd(h, _row(w["b_norm"][j]))
    q = _mm_rowsh(xn, w["GD"], SQ_ROWS, GD_B_Q0 + j, "nn", "b_q")
    rel = w["b_rel_bias"][j]
    table = jnp.pad(rel, ((0, 0), (0, REL_PAD - rel.shape[1])))
    bias = _relbias_expand(_relbias_ring(table, False).reshape(B_HEADS, 1, REL_RING))
    o = _attn_fwd(q, kpad, vpad, bias)
    h2 = _mm_rowsh(o, w["GD"], SQ_ROWS, GD_B_OUT0 + j, "nn", "b_out", res=h)
    return h2, dict(h=h, xn=xn, q=q, o=o, bias=bias)


def _b_layer_bwd(dh2, w, j, sv, kpad, vpad):
    g = {}
    g["w_out"] = _mm_rowsh_dw(sv["o"], dh2, SQ_ROWS, "b_out_dw")
    do = _mm_rowsh(dh2, w["GD"], SQ_ROWS, GD_B_OUT0 + j, "nt", "b_out_dx")
    dq, dkp, dvp, dsc = _attn_bwd(sv["q"], kpad, vpad, sv["bias"], do)
    dring = _relbias_reduce(dsc).reshape(B_HEADS, REL_RING)
    g["rel_bias"] = _relbias_ring(dring, True)[:, :2 * REL_CLIP + 1]
    g["w_q"] = _mm_rowsh_dw(sv["xn"], dq, SQ_ROWS, "b_q_dw")
    dxn = _mm_rowsh(dq, w["GD"], SQ_ROWS, GD_B_Q0 + j, "nt", "b_q_dx")
    dh, dnorm = _rmsnorm_bwd(sv["h"], _row(w["b_norm"][j]), dxn, dh2)
    g["norm"] = dnorm[0]
    return dh, g, dkp, dvp


def _local_step(x, tgt, w):
    h = x
    saved = []
    kv_saved = None
    kpad = vpad = None
    for layer in range(DEPTH):
        if layer < N_A:
            h, sm = _a_layer_fwd(h, w, layer)
        else:
            if layer == N_A:
                xn_kv = _rmsnorm_fwd(h, _row(w["kv_norm"]))
                kv = _mm_colsh(xn_kv, w["GK"], D_MODEL, 0, "nn", "kv", flat=True, out_dtype=BF16)
                kpad = jnp.pad(kv[:, :D_MODEL], ((LEFT, 0), (0, 0)))
                vpad = jnp.pad(kv[:, D_MODEL:], ((LEFT, 0), (0, 0)))
                kv_saved = dict(h=h, xn=xn_kv)
            h, sm = _b_layer_fwd(h, w, layer - N_A, kpad, vpad)
        h, sf = _ffn_fwd(h, w, layer)
        saved.append((sm, sf))

    loss, dh, dfinal = _final_loss(h, _row(w["final_norm"]), tgt)

    ga = [None] * N_A
    gb = [None] * N_B
    gf = [None] * DEPTH
    dk_tot = dv_tot = None
    g_kv = g_kvn = None
    for layer in reversed(range(DEPTH)):
        sm, sf = saved[layer]
        dh, gf[layer] = _ffn_bwd(dh, w, layer, sf)
        if layer >= N_A:
            dh, gb[layer - N_A], dkp, dvp = _b_layer_bwd(dh, w, layer - N_A, sm, kpad, vpad)
            dk_tot = dkp if dk_tot is None else dk_tot + dkp
            dv_tot = dvp if dv_tot is None else dv_tot + dvp
            if layer == N_A:
                dkv = jnp.concatenate([dk_tot[LEFT:], dv_tot[LEFT:]], axis=1)
                g_kv = _mm_colsh_dw(kv_saved["xn"], dkv, "kv_dw", flat=True)
                dxn = _mm_colsh(dkv, w["GK"], D_MODEL, 0, "nt", "kv_dx", flat=True)
                dh, g_kvn = _rmsnorm_bwd(kv_saved["h"], _row(w["kv_norm"]), dxn, dh)
        else:
            dh, ga[layer] = _a_layer_bwd(dh, w, layer, sm)

    def stack(gs, key):
        return jnp.stack([g[key] for g in gs])

    def rows(gs, key):
        return [g[key] for g in gs]

    grads = dict(
        GU=jnp.concatenate(rows(gf, "w_up"), axis=1),
        GD=jnp.concatenate(rows(gf, "w_down") + rows(ga, "w_out") + rows(gb, "w_q") + rows(gb, "w_out"), axis=1),
        GK=g_kv,
        GI=jnp.concatenate(rows(ga, "w_in"), axis=1),
        a_norm=stack(ga, "norm"), a_conv=stack(ga, "conv"), a_A_log=stack(ga, "A_log"),
        a_dt_bias=stack(ga, "dt_bias"), a_out_norm=stack(ga, "out_norm"), kv_norm=g_kvn[0],
        b_norm=stack(gb, "norm"), b_rel_bias=stack(gb, "rel_bias"),
        f_norm=stack(gf, "norm"), f_conv=stack(gf, "conv"), f_conv_b=stack(gf, "conv_b"), final_norm=dfinal[0])
    return loss, dh, grads


HBM_SPEC = pl.BlockSpec(memory_space=pl.ANY)
VMEM_SPEC = pl.BlockSpec(memory_space=pltpu.VMEM)


def _place():
    x, y, c = lax.axis_index("x"), lax.axis_index("y"), lax.axis_index("c")
    chips = [(1 - x, y), (x, 1 - y), (1 - x, 1 - y)]
    return x, y, c, chips


def _remote(src, dst, send_sem, recv_sem, to):
    return pltpu.make_async_remote_copy(src_ref=src, dst_ref=dst, send_sem=send_sem, recv_sem=recv_sem,
                                        device_id=to, device_id_type=MESH)


def _allgather_weights(shards):
    n = len(shards)
    per = 7

    def body(*refs):
        x_refs, out_refs, (send_sems, recv_sems) = refs[:n], refs[n:2 * n], refs[2 * n:]
        x, y, c, chips = _place()
        sibling = (x, y, 1 - c)

        def half(a, px, py, hc):
            rh = shards[a].shape[0] // 2
            return out_refs[a].at[2 * px + py, pl.ds(hc * rh, rh), :]

        def mine(a):
            rh = shards[a].shape[0] // 2
            return x_refs[a].at[pl.ds(c * rh, rh), :]

        def sems(a, k):
            return send_sems.at[per * a + k], recv_sems.at[per * a + k]

        first = [_remote(mine(a), half(a, x, y, c), *sems(a, j), (*chip, c))
                 for a in range(n) for j, chip in enumerate(chips)]
        own = [_remote(x_refs[a], out_refs[a].at[2 * x + y], *sems(a, 6), sibling) for a in range(n)]
        for cp in first + own:
            cp.start()
        passed = []
        for j, chip in enumerate(chips):
            for a in range(n):
                landed = half(a, *chip, c)
                _remote(landed, landed, *sems(a, j), (*chip, c)).wait_recv()
                cp = _remote(landed, landed, *sems(a, 3 + j), sibling)
                cp.start()
                passed.append(cp)
        for j, chip in enumerate(chips):
            for a in range(n):
                theirs = half(a, *chip, 1 - c)
                _remote(theirs, theirs, *sems(a, 3 + j), sibling).wait_recv()
        for cp in own:
            cp.wait_recv()
        for cp in first + passed + own:
            cp.wait_send()

    return pl.pallas_call(
        body, name="allgather_weights",
        out_shape=[jax.ShapeDtypeStruct((N_SHARDS,) + sh.shape, sh.dtype) for sh in shards],
        in_specs=[HBM_SPEC] * n, out_specs=[HBM_SPEC] * n,
        scratch_shapes=[pltpu.SemaphoreType.DMA((per * n,)), pltpu.SemaphoreType.DMA((per * n,))],
    )(*shards)


def _pair_exchange(gs):
    n = len(gs)

    def body(*refs):
        g_refs, out_refs, (send_sems, recv_sems) = refs[:n], refs[n:2 * n], refs[2 * n:]
        x, y, c, _ = _place()
        cps = []
        for a in range(n):
            rh = gs[a].shape[1] // 2
            cps.append(_remote(g_refs[a].at[:, pl.ds((1 - c) * rh, rh), :], out_refs[a], send_sems.at[a],
                               recv_sems.at[a], (x, y, 1 - c)))
        for cp in cps:
            cp.start()
        for cp in cps:
            cp.wait()

    return pl.pallas_call(
        body, name="rs_pair_exchange",
        out_shape=[jax.ShapeDtypeStruct((g.shape[0], g.shape[1] // 2, g.shape[2]), g.dtype) for g in gs],
        in_specs=[HBM_SPEC] * n, out_specs=[HBM_SPEC] * n,
        scratch_shapes=[pltpu.SemaphoreType.DMA((n,)), pltpu.SemaphoreType.DMA((n,))],
    )(*gs)


def _add_rows(rows, cols):
    best = 16
    for t in range(16, rows + 1, 16):
        if rows % t == 0 and t * cols * 4 <= 2304 * 1024:
            best = t
    return best


def _pair_add(g, other, c_idx):
    n, r, cols = g.shape
    rh = r // 2
    tr = _add_rows(rh, cols)
    nb = rh // tr

    def body(c_ref, a_ref, b_ref, o_ref, ob_ref):
        sm = a_ref[...] + b_ref[...]
        o_ref[...] = sm
        ob_ref[...] = sm.astype(BF16)

    out_blk = pl.BlockSpec((1, tr, cols), lambda s, i, c_ref: (s, i, 0))
    return pl.pallas_call(
        body, name="rs_pair_add",
        grid_spec=pltpu.PrefetchScalarGridSpec(
            num_scalar_prefetch=1, grid=(n, nb),
            in_specs=[pl.BlockSpec((1, tr, cols), lambda s, i, c_ref: (s, c_ref[0] * nb + i, 0)), out_blk],
            out_specs=[out_blk, out_blk]),
        out_shape=[jax.ShapeDtypeStruct((n, rh, cols), F32), jax.ShapeDtypeStruct((n, rh, cols), BF16)],
        compiler_params=_params(("parallel", "parallel")),
    )(c_idx, g, other)


def _chip_exchange(ps):
    n = len(ps)

    def body(*refs):
        p_refs, out_refs, (send_sems, recv_sems) = refs[:n], refs[n:2 * n], refs[2 * n:]
        x, y, c, chips = _place()
        cps = [_remote(p_refs[a].at[2 * chip[0] + chip[1]], out_refs[a].at[j], send_sems.at[3 * a + j],
                       recv_sems.at[3 * a + j], (*chip, c))
               for a in range(n) for j, chip in enumerate(chips)]
        for cp in cps:
            cp.start()
        for cp in cps:
            cp.wait()

    return pl.pallas_call(
        body, name="rs_chip_exchange",
        out_shape=[jax.ShapeDtypeStruct((3,) + p.shape[1:], p.dtype) for p in ps],
        in_specs=[HBM_SPEC] * n, out_specs=[HBM_SPEC] * n,
        scratch_shapes=[pltpu.SemaphoreType.DMA((3 * n,)), pltpu.SemaphoreType.DMA((3 * n,))],
    )(*ps)


def _chip_add(p, recv, chip_idx):
    n, rh, cols = p.shape
    tr = _add_rows(rh, cols)

    def body(s_ref, own_ref, r_ref, o_ref):
        o_ref[...] = ((own_ref[0] + r_ref[0].astype(F32)) + r_ref[1].astype(F32)) + r_ref[2].astype(F32)

    return pl.pallas_call(
        body, name="rs_chip_add",
        grid_spec=pltpu.PrefetchScalarGridSpec(
            num_scalar_prefetch=1, grid=(rh // tr,),
            in_specs=[pl.BlockSpec((1, tr, cols), lambda i, s_ref: (s_ref[0], i, 0)),
                      pl.BlockSpec((3, tr, cols), lambda i, s_ref: (0, i, 0))],
            out_specs=pl.BlockSpec((tr, cols), lambda i, s_ref: (i, 0))),
        out_shape=jax.ShapeDtypeStruct((rh, cols), p.dtype),
        compiler_params=_params(("parallel",)),
    )(chip_idx, p, recv)


def _pair_gather(fs):
    n = len(fs)

    def body(*refs):
        f_refs, out_refs, (send_sems, recv_sems) = refs[:n], refs[n:2 * n], refs[2 * n:]
        x, y, c, _ = _place()
        cps = [_remote(f_refs[a], out_refs[a], send_sems.at[a], recv_sems.at[a], (x, y, 1 - c)) for a in range(n)]
        for cp in cps:
            cp.start()
        for cp in cps:
            cp.wait()

    return pl.pallas_call(
        body, name="rs_pair_gather",
        out_shape=[jax.ShapeDtypeStruct(f.shape, f.dtype) for f in fs],
        in_specs=[HBM_SPEC] * n, out_specs=[HBM_SPEC] * n,
        scratch_shapes=[pltpu.SemaphoreType.DMA((n,)), pltpu.SemaphoreType.DMA((n,))],
    )(*fs)


def _allreduce_small(v):
    r, cols = v.shape

    def body(x_ref, out_ref, slots, send_sems, recv_sems):
        x, y, c, _ = _place()
        bits = [(bx, by, bc) for bx in (0, 1) for by in (0, 1) for bc in (0, 1)]

        def flip(b):
            return (1 - x if b[0] else x, 1 - y if b[1] else y, 1 - c if b[2] else c)

        slots[0] = x_ref[...]
        cps = [_remote(x_ref, slots.at[k], send_sems.at[k - 1], recv_sems.at[k - 1], flip(bits[k]))
               for k in range(1, 8)]
        for cp in cps:
            cp.start()
        for cp in cps:
            cp.wait()
        acc = None
        for b in bits:
            fx, fy, fc = flip(b)
            term = slots[4 * fx + 2 * fy + fc]
            acc = term if acc is None else acc + term
        out_ref[...] = acc

    return pl.pallas_call(
        body, name="allreduce_small",
        out_shape=jax.ShapeDtypeStruct((r, cols), v.dtype),
        in_specs=[VMEM_SPEC], out_specs=VMEM_SPEC,
        scratch_shapes=[pltpu.VMEM((8, r, cols), v.dtype), pltpu.SemaphoreType.DMA((7,)),
                        pltpu.SemaphoreType.DMA((7,))],
        compiler_params=pltpu.CompilerParams(vmem_limit_bytes=VMEM_LIMIT),
    )(v)


BIG = (("a_w_in", 2), ("a_w_out", 1), ("w_kv", 1), ("b_w_q", 1), ("b_w_out", 1), ("f_w_up", 2), ("f_w_down", 1))
SMALL = (("a_norm", 1), ("a_conv", 2), ("a_A_log", None), ("a_dt_bias", None), ("a_out_norm", None),
         ("kv_norm", None), ("b_norm", None), ("b_rel_bias", None), ("f_norm", None), ("f_conv", 2),
         ("f_conv_b", None), ("final_norm", None))
WEIGHT_ORDER = ("a_norm", "a_w_in", "a_conv", "a_A_log", "a_dt_bias", "a_out_norm", "a_w_out", "kv_norm", "w_kv",
                "b_norm", "b_w_q", "b_rel_bias", "b_w_out", "f_norm", "f_w_up", "f_conv", "f_conv_b", "f_w_down",
                "final_norm")


def _pad_rows(flat, cols, quantum):
    n = flat.shape[-1]
    rows = -(-n // (cols * quantum)) * quantum
    pad = [(0, 0)] * (flat.ndim - 1) + [(0, rows * cols - n)]
    return jnp.pad(flat, pad).reshape(flat.shape[:-1] + (rows, cols))


GROUPS = ("GU", "GD", "GK", "GI")


def _group_shards(w, dtype):
    def two(a):
        return a.reshape(-1, a.shape[-1])

    return dict(GU=two(w["f_w_up"]).astype(dtype),
                GD=jnp.concatenate([two(w[n]) for n in ("f_w_down", "a_w_out", "b_w_q", "b_w_out")]).astype(dtype),
                GK=w["w_kv"].astype(dtype),
                GI=two(w["a_w_in"]).astype(dtype))


def _ungroup(red, shard_shapes):
    out = dict(f_w_up=red["GU"].reshape(shard_shapes["f_w_up"]), w_kv=red["GK"],
               a_w_in=red["GI"].reshape(shard_shapes["a_w_in"]))
    off = 0
    for n in ("f_w_down", "a_w_out", "b_w_q", "b_w_out"):
        shp = shard_shapes[n]
        rows = math.prod(shp[:-1])
        out[n] = red["GD"][off:off + rows].reshape(shp)
        off += rows
    return out


def _dense_a_in(gi):
    out = []
    for i in range(N_A):
        full = jnp.transpose(gi[:, i * D_MODEL:(i + 1) * D_MODEL], (1, 0, 2)).reshape(D_MODEL, -1)
        out.append((full[:, :A_CONV_WIDTH], full[:, A_CONV_WIDTH:A_CONV_WIDTH + A_QK],
                    jnp.pad(full[:, A_CONV_WIDTH + A_QK:], ((0, 0), (0, LANE - 2 * A_HEADS)))))
    return out


def _pack_small(values, names):
    return _pad_rows(jnp.concatenate([values[n].reshape(-1) for n in names]), LANE, SUB)


def _unpack_small(packed, shapes, names):
    flat = packed.reshape(-1)
    out, off = {}, 0
    for n in names:
        size = math.prod(shapes[n])
        out[n] = flat[off:off + size].reshape(shapes[n])
        off += size
    return out


def _adamw_nd(w, g, m, v):
    shp = w.shape
    two = (math.prod(shp[:-1]), shp[-1])
    d, mn, vn = _adamw(w.reshape(two), g.reshape(two), m.reshape(two), v.reshape(two))
    return d.reshape(shp), mn.reshape(shp), vn.reshape(shp)


def kernel(x, a_norm, a_w_in, a_conv, a_A_log, a_dt_bias, a_out_norm, a_w_out, kv_norm, w_kv, b_norm, b_w_q, b_rel_bias, b_w_out, f_norm, f_w_up, f_conv, f_conv_b, f_w_down, final_norm, loss_target, m_a_norm, m_a_w_in, m_a_conv, m_a_A_log, m_a_dt_bias, m_a_out_norm, m_a_w_out, m_kv_norm, m_w_kv, m_b_norm, m_b_w_q, m_b_rel_bias, m_b_w_out, m_f_norm, m_f_w_up, m_f_conv, m_f_conv_b, m_f_w_down, m_final_norm, v_a_norm, v_a_w_in, v_a_conv, v_a_A_log, v_a_dt_bias, v_a_out_norm, v_a_w_out, v_kv_norm, v_w_kv, v_b_norm, v_b_w_q, v_b_rel_bias, v_b_w_out, v_f_norm, v_f_w_up, v_f_conv, v_f_conv_b, v_f_w_down, v_final_norm):
    w = dict(a_norm=a_norm, a_w_in=a_w_in, a_conv=a_conv, a_A_log=a_A_log, a_dt_bias=a_dt_bias,
             a_out_norm=a_out_norm, a_w_out=a_w_out, kv_norm=kv_norm, w_kv=w_kv, b_norm=b_norm, b_w_q=b_w_q,
             b_rel_bias=b_rel_bias, b_w_out=b_w_out, f_norm=f_norm, f_w_up=f_w_up, f_conv=f_conv,
             f_conv_b=f_conv_b, f_w_down=f_w_down, final_norm=final_norm)
    m = dict(a_norm=m_a_norm, a_w_in=m_a_w_in, a_conv=m_a_conv, a_A_log=m_a_A_log, a_dt_bias=m_a_dt_bias,
             a_out_norm=m_a_out_norm, a_w_out=m_a_w_out, kv_norm=m_kv_norm, w_kv=m_w_kv, b_norm=m_b_norm,
             b_w_q=m_b_w_q, b_rel_bias=m_b_rel_bias, b_w_out=m_b_w_out, f_norm=m_f_norm, f_w_up=m_f_w_up,
             f_conv=m_f_conv, f_conv_b=m_f_conv_b, f_w_down=m_f_w_down, final_norm=m_final_norm)
    v = dict(a_norm=v_a_norm, a_w_in=v_a_w_in, a_conv=v_a_conv, a_A_log=v_a_A_log, a_dt_bias=v_a_dt_bias,
             a_out_norm=v_a_out_norm, a_w_out=v_a_w_out, kv_norm=v_kv_norm, w_kv=v_w_kv, b_norm=v_b_norm,
             b_w_q=v_b_w_q, b_rel_bias=v_b_rel_bias, b_w_out=v_b_w_out, f_norm=v_f_norm, f_w_up=v_f_w_up,
             f_conv=v_f_conv, f_conv_b=v_f_conv_b, f_w_down=v_f_w_down, final_norm=v_final_norm)
    xi, yi, ci = lax.axis_index("x"), lax.axis_index("y"), lax.axis_index("c")
    chip = 2 * xi + yi
    shard_shapes = {n: w[n].shape for n in WEIGHT_ORDER}

    mine_w = _group_shards(w, BF16)
    full = dict(zip(GROUPS, _allgather_weights([mine_w[n] for n in GROUPS])))
    full["a_in"] = _dense_a_in(full.pop("GI"))
    sharded_small = [n for n, axis in SMALL if axis is not None]
    placed = {}
    for n, axis in SMALL:
        if axis is not None:
            wide = list(w[n].shape)
            wide[axis] *= 4
            mine_once = w[n] * (1 - ci).astype(F32)
            placed[n] = lax.dynamic_update_slice_in_dim(jnp.zeros(wide, F32), mine_once, chip * w[n].shape[axis], axis)
    placed_shapes = {n: placed[n].shape for n in sharded_small}
    full.update(_unpack_small(_allreduce_small(_pack_small(placed, sharded_small)), placed_shapes, sharded_small))
    for n, axis in SMALL:
        if axis is None:
            full[n] = w[n]

    loss_part, grad_x, grads = _local_step(x[0], loss_target[0], full)

    packed = [grads[n] for n in GROUPS]
    c_idx = jnp.reshape(ci, (1,)).astype(jnp.int32)
    chip_idx = jnp.reshape(chip, (1,)).astype(jnp.int32)
    pairs = [_pair_add(g, o, c_idx) for g, o in zip(packed, _pair_exchange(packed))]
    arrived = _chip_exchange([pb for _, pb in pairs])
    mine = [_chip_add(p, r, chip_idx) for (p, _), r in zip(pairs, arrived)]
    theirs = _pair_gather(mine)
    red = _ungroup({n: jnp.concatenate([jnp.where(ci == 0, a, b), jnp.where(ci == 0, b, a)], axis=0)
                    for n, a, b in zip(GROUPS, mine, theirs)}, shard_shapes)

    small_names = [n for n, _ in SMALL]
    small_vals = {n: grads[n] for n in small_names}
    small_vals["loss"] = loss_part[0, :1]
    names = ["loss"] + small_names
    shapes = {n: small_vals[n].shape for n in names}
    summed = _unpack_small(_allreduce_small(_pack_small(small_vals, names)), shapes, names)
    loss = summed["loss"][0]
    for n, axis in SMALL:
        g = summed[n]
        if axis is not None:
            g = lax.dynamic_slice_in_dim(g, chip * w[n].shape[axis], w[n].shape[axis], axis)
        red[n] = g

    delta, new_m, new_v = {}, {}, {}
    for n, _ in BIG:
        delta[n], new_m[n], new_v[n] = _adamw_nd(w[n], red[n], m[n], v[n])
    local_shapes = {n: w[n].shape for n in small_names}
    packs = [_pack_small(t, small_names) for t in (w, red, m, v)]
    outs = _adamw(*packs)
    ds, ms, vs = (_unpack_small(o, local_shapes, small_names) for o in outs)
    delta.update(ds)
    new_m.update(ms)
    new_v.update(vs)

    return (loss, grad_x[None], *[red[n] for n in WEIGHT_ORDER], *[delta[n] for n in WEIGHT_ORDER],
            *[new_m[n] for n in WEIGHT_ORDER], *[new_v[n] for n in WEIGHT_ORDER])
```

```python
import math

import jax
import jax.numpy as jnp
from jax import lax
from jax.experimental import pallas as pl
from jax.experimental.pallas import tpu as pltpu

F32 = jnp.float32
BF16 = jnp.bfloat16
HIGHEST = lax.Precision.HIGHEST
MESH = pl.DeviceIdType.MESH

D_MODEL = 1024
CHUNK = 64
A_HEADS = 8
A_HEAD = 128
A_QK = A_HEADS * A_HEAD
A_CONV_WIDTH = 3 * A_QK
B_HEADS = 16
B_HEAD = 64
LEFT = 8 * CHUNK
QBLK = 4 * CHUNK
KBLK = LEFT + QBLK
REL_CLIP = 256
REL_PAD = 640
FFN_DIM = 2816
EPS = 1e-6
NEG_INF = -1e30
LANE = 128
SUB = 8
VMEM_LIMIT = 56 * 1024 * 1024

ADAM_LR = 0.001
ADAM_B1 = 0.9
ADAM_B2 = 0.999
ADAM_EPS = 1e-08
ADAM_WD = 0.01
ADAM_STEP = 10


VMEM_FULL = pl.BlockSpec(memory_space=pltpu.VMEM)


def _params(sem=None):
    return pltpu.CompilerParams(dimension_semantics=sem, vmem_limit_bytes=VMEM_LIMIT)


def _tile(n, cap):
    if n <= cap:
        return n
    best = None
    for t in range(LANE, cap + 1, LANE):
        if n % t == 0:
            best = t
    assert best is not None, n
    return best


def _sigmoid(x):
    return 1.0 / (1.0 + jnp.exp(-x))


def _softplus(x):
    return jnp.maximum(x, 0.0) + jnp.log(1.0 + jnp.exp(-jnp.abs(x)))


def _dot(a, b, dims, prec=None):
    return lax.dot_general(a, b, (dims, ((), ())), preferred_element_type=F32, precision=prec)


NN = ((1,), (0,))
NT = ((1,), (1,))
TN = ((0,), (0,))


def _bdot(a, b, dims):
    return _dot(a.astype(BF16), b.astype(BF16), dims)


def _mm(a, b, mode="nn", out_dtype=F32, res=None, name="mm", norm=None):
    if mode == "nn":
        (m, k), (k2, n) = a.shape, b.shape
    elif mode == "nt":
        (m, k), (n, k2) = a.shape, b.shape
    else:
        (k, m), (k2, n) = a.shape, b.shape
    assert k == k2, (a.shape, b.shape, mode)
    tm, tn, tk = _tile(m, 1408), _tile(n, 1408), _tile(k, 1408)
    if m == 8192:
        tm = 1024
    if k == 8192:
        tk = 1024
    nk = k // tk
    dims = {"nn": NN, "nt": NT, "tn": TN}[mode]
    a_spec = {"nn": pl.BlockSpec((tm, tk), lambda i, j, kk: (i, kk)),
              "nt": pl.BlockSpec((tm, tk), lambda i, j, kk: (i, kk)),
              "tn": pl.BlockSpec((tk, tm), lambda i, j, kk: (kk, i))}[mode]
    b_spec = {"nn": pl.BlockSpec((tk, tn), lambda i, j, kk: (kk, j)),
              "nt": pl.BlockSpec((tn, tk), lambda i, j, kk: (j, kk)),
              "tn": pl.BlockSpec((tk, tn), lambda i, j, kk: (kk, j))}[mode]
    o_spec = pl.BlockSpec((tm, tn), lambda i, j, kk: (i, j))
    return _mm_call(name, a, b, dims, (m // tm, n // tn, nk), a_spec, b_spec, o_spec, (m, n), out_dtype, (tm, tn),
                    res, norm)


ROW_TILE = 1024
N_SHARDS = 4


def _mm_call(name, a, b, dims, grid, a_spec, b_spec, o_spec, out_shape, out_dtype, acc_shape, res=None, norm=None):
    nk = grid[2]
    has_res = res is not None
    has_norm = norm is not None
    if has_norm:
        assert grid[1] == 1 and len(out_shape) == 2 and out_dtype == F32

    def flat(v):
        return v.reshape(-1, v.shape[-1]) if v.ndim == 3 else v

    def body(a_ref, b_ref, *rest):
        rest = list(rest)
        res_ref = rest.pop(0) if has_res else None
        x_ref, g_ref, dres_ref = (rest.pop(0), rest.pop(0), rest.pop(0)) if has_norm else (None, None, None)
        o_ref = rest.pop(0)
        dg_ref = rest.pop(0) if has_norm else None
        acc = rest.pop(0)
        kk = pl.program_id(2)
        first_rows = pl.program_id(0) == 0

        @pl.when(kk == 0)
        def _():
            acc[...] = jnp.zeros_like(acc)

        if has_norm:
            @pl.when(first_rows & (kk == 0))
            def _():
                dg_ref[...] = jnp.zeros_like(dg_ref)

        acc[...] += _bdot(flat(a_ref[...]), flat(b_ref[...]), dims)

        @pl.when(kk == nk - 1)
        def _():
            r = acc[...]
            if has_res:
                r = r + res_ref[...]
            if not has_norm:
                o_ref[...] = r.reshape(o_ref.shape).astype(out_dtype)
                return
            xv = x_ref[...]
            rs = lax.rsqrt(jnp.mean(xv * xv, axis=-1, keepdims=True) + EPS)
            t = r * g_ref[...]
            c = jnp.mean(t * xv, axis=-1, keepdims=True)
            o_ref[...] = dres_ref[...] + rs * t - xv * (rs * rs * rs) * c
            dg_ref[...] += jnp.sum(r * xv * rs, axis=0, keepdims=True)

    args = [a, b] + ([res] if has_res else [])
    in_specs = [a_spec, b_spec] + ([o_spec] if has_res else [])
    out_specs, out_shapes = o_spec, jax.ShapeDtypeStruct(out_shape, out_dtype)
    if has_norm:
        vec = pl.BlockSpec((1, out_shape[1]), lambda i, j, kk: (0, 0))
        args += list(norm)
        in_specs += [o_spec, vec, o_spec]
        out_specs = [o_spec, vec]
        out_shapes = [out_shapes, jax.ShapeDtypeStruct((1, out_shape[1]), F32)]
    return pl.pallas_call(
        body, name=name, grid=grid, in_specs=in_specs, out_specs=out_specs, out_shape=out_shapes,
        scratch_shapes=[pltpu.VMEM(acc_shape, F32)],
        compiler_params=_params(("arbitrary" if has_norm else "parallel", "parallel", "arbitrary")),
    )(*args)


def _shards_per_block(rows):
    return N_SHARDS if N_SHARDS * rows <= 1408 else 2


def _mm_rowsh(a, buf, rows, blk0, mode, name, res=None, out_dtype=F32, norm=None):
    s = a.shape[0]
    cols = buf.shape[2]
    g = _shards_per_block(rows)
    tm = _tile(s, ROW_TILE)
    b_blk = (g, rows, cols)
    if mode == "nn":
        return _mm_call(name, a, buf, NN, (s // tm, 1, N_SHARDS // g),
                        pl.BlockSpec((tm, g * rows), lambda i, j, kk: (i, kk)),
                        pl.BlockSpec(b_blk, lambda i, j, kk: (kk, blk0, 0)),
                        pl.BlockSpec((tm, cols), lambda i, j, kk: (i, 0)),
                        (s, cols), out_dtype, (tm, cols), res)
    return _mm_call(name, a, buf, NT, (s // tm, N_SHARDS // g, 1),
                    pl.BlockSpec((tm, cols), lambda i, j, kk: (i, 0)),
                    pl.BlockSpec(b_blk, lambda i, j, kk: (j, blk0, 0)),
                    pl.BlockSpec((tm, g * rows), lambda i, j, kk: (i, j)),
                    (s, N_SHARDS * rows), out_dtype, (tm, g * rows), res, norm)


def _mm_rowsh_dw(act, dy, rows, name):
    s = act.shape[0]
    cols = dy.shape[1]
    g = _shards_per_block(rows)
    ts = _tile(s, ROW_TILE)
    return _mm_call(name, act, dy, TN, (1, N_SHARDS // g, s // ts),
                    pl.BlockSpec((ts, g * rows), lambda i, j, kk: (kk, j)),
                    pl.BlockSpec((ts, cols), lambda i, j, kk: (kk, 0)),
                    pl.BlockSpec((g, rows, cols), lambda i, j, kk: (j, 0, 0)),
                    (N_SHARDS, rows, cols), F32, (g * rows, cols))


def _mm_colsh(a, buf, krows, blk0, mode, name, flat=False, res=None, out_dtype=F32, norm=None):
    cols = buf.shape[2]
    b_nn = pl.BlockSpec((None, krows, cols), lambda i, j, kk: (j, blk0, 0))
    b_nt = pl.BlockSpec((None, krows, cols), lambda i, j, kk: (kk, blk0, 0))
    if mode == "nn":
        s = a.shape[0]
        tm = _tile(s, ROW_TILE)
        o_spec = (pl.BlockSpec((tm, cols), lambda i, j, kk: (i, j)) if flat
                  else pl.BlockSpec((None, tm, cols), lambda i, j, kk: (j, i, 0)))
        return _mm_call(name, a, buf, NN, (s // tm, N_SHARDS, 1),
                        pl.BlockSpec((tm, krows), lambda i, j, kk: (i, 0)), b_nn, o_spec,
                        (s, N_SHARDS * cols) if flat else (N_SHARDS, s, cols), out_dtype, (tm, cols), res)
    s = a.shape[0] if flat else a.shape[1]
    tm = _tile(s, ROW_TILE)
    a_spec = (pl.BlockSpec((tm, cols), lambda i, j, kk: (i, kk)) if flat
              else pl.BlockSpec((None, tm, cols), lambda i, j, kk: (kk, i, 0)))
    return _mm_call(name, a, buf, NT, (s // tm, 1, N_SHARDS), a_spec, b_nt,
                    pl.BlockSpec((tm, krows), lambda i, j, kk: (i, 0)),
                    (s, krows), out_dtype, (tm, krows), res, norm)


def _mm_colsh_dw(x, dy, name, flat=False):
    s, k = x.shape
    cols = dy.shape[1] // N_SHARDS if flat else dy.shape[2]
    ts = _tile(s, ROW_TILE)
    b_spec = (pl.BlockSpec((ts, cols), lambda i, j, kk: (kk, j)) if flat
              else pl.BlockSpec((None, ts, cols), lambda i, j, kk: (j, kk, 0)))
    return _mm_call(name, x, dy, TN, (1, N_SHARDS, s // ts),
                    pl.BlockSpec((ts, k), lambda i, j, kk: (kk, 0)), b_spec,
                    pl.BlockSpec((None, k, cols), lambda i, j, kk: (j, 0, 0)),
                    (N_SHARDS, k, cols), F32, (k, cols))


def _rmsnorm_fwd(x, g):
    s, d = x.shape
    tr = _tile(s, 1024)

    def body(x_ref, g_ref, o_ref):
        xv = x_ref[...]
        r = lax.rsqrt(jnp.mean(xv * xv, axis=-1, keepdims=True) + EPS)
        o_ref[...] = (xv * r * g_ref[...]).astype(BF16)

    return pl.pallas_call(
        body, name="rmsnorm_fwd", grid=(s // tr,),
        in_specs=[pl.BlockSpec((tr, d), lambda i: (i, 0)), pl.BlockSpec((1, d), lambda i: (0, 0))],
        out_specs=pl.BlockSpec((tr, d), lambda i: (i, 0)),
        out_shape=jax.ShapeDtypeStruct((s, d), BF16),
        compiler_params=_params(("parallel",)),
    )(x, g)


def _final_loss(h, g, tgt):
    s, d = h.shape
    tr = _tile(s, 1024)

    def body(x_ref, g_ref, t_ref, loss_ref, dx_ref, dg_ref):
        @pl.when(pl.program_id(0) == 0)
        def _():
            dg_ref[...] = jnp.zeros_like(dg_ref)
            loss_ref[...] = jnp.zeros_like(loss_ref)

        xv = x_ref[...]
        r = lax.rsqrt(jnp.mean(xv * xv, axis=-1, keepdims=True) + EPS)
        xh = xv * r
        err = xh * g_ref[...] - t_ref[...]
        per_row = jnp.mean(err * err, axis=-1, keepdims=True)
        loss_ref[...] += 0.5 * jnp.sum(per_row, axis=0, keepdims=True)
        dy = err * (1.0 / d)
        t = dy * g_ref[...]
        c = jnp.mean(t * xv, axis=-1, keepdims=True)
        dx_ref[...] = r * t - xv * (r * r * r) * c
        dg_ref[...] += jnp.sum(dy * xh, axis=0, keepdims=True)

    row = pl.BlockSpec((tr, d), lambda i: (i, 0))
    vec = pl.BlockSpec((1, d), lambda i: (0, 0))
    return pl.pallas_call(
        body, name="final_loss", grid=(s // tr,),
        in_specs=[row, vec, row],
        out_specs=[pl.BlockSpec((1, LANE), lambda i: (0, 0)), row, vec],
        out_shape=[jax.ShapeDtypeStruct((1, LANE), F32), jax.ShapeDtypeStruct((s, d), F32),
                   jax.ShapeDtypeStruct((1, d), F32)],
        compiler_params=_params(("arbitrary",)),
    )(h, g, tgt)


CONV_ROWS = 512
CONV_COLS = 1408
CONV_CHUNK = 64


def _per_lane_tile(tile_body):
    def body(*refs):
        for lt in range(refs[0].shape[-1] // LANE):
            cols = slice(lt * LANE, (lt + 1) * LANE)
            tile_body(*[r.at[(slice(None),) * (len(r.shape) - 1) + (cols,)] for r in refs])
    return body


def _lagged(window, lag):
    return (pltpu.roll(window, lag, 0) if lag else window)[SUB:]


def _led(window, lead):
    n = window.shape[0] - SUB
    return (pltpu.roll(window, window.shape[0] - lead, 0) if lead else window)[:n]


def _taps(shifted, w):
    acc = None
    for k, xs in enumerate(shifted):
        term = xs * w[k:k + 1, :]
        acc = term if acc is None else acc + term
    return acc


def _conv_tiles(s, c):
    return _tile(s, CONV_ROWS), _tile(c, CONV_COLS)


def _conv_silu_fwd(pre, w):
    s, c = pre.shape
    width = w.shape[0]
    tr, tc = _conv_tiles(s, c)

    def body(x_ref, w_ref, o_ref, tail):
        @pl.when(pl.program_id(1) == 0)
        def _():
            tail[...] = jnp.zeros_like(tail)

        wv = w_ref[...]

        def do(c0, window):
            y = _taps([_lagged(window, width - 1 - k) for k in range(width)], wv)
            o_ref[pl.ds(c0, CONV_CHUNK), :] = y * _sigmoid(y)

        def chunk(ci, carry):
            c0 = pl.multiple_of(ci * CONV_CHUNK, CONV_CHUNK)
            do(c0, x_ref[pl.ds(pl.multiple_of(c0 - SUB, SUB), CONV_CHUNK + SUB), :])
            return carry

        do(0, jnp.concatenate([tail[...], x_ref[:CONV_CHUNK, :]], axis=0))
        lax.fori_loop(1, tr // CONV_CHUNK, chunk, 0)
        tail[...] = x_ref[tr - SUB:, :]

    blk = pl.BlockSpec((tr, tc), lambda j, i: (i, j))
    return pl.pallas_call(
        _per_lane_tile(body), name="conv_silu_fwd", grid=(c // tc, s // tr),
        in_specs=[blk, pl.BlockSpec((width, tc), lambda j, i: (0, j))], out_specs=blk,
        out_shape=jax.ShapeDtypeStruct((s, c), F32),
        scratch_shapes=[pltpu.VMEM((SUB, tc), F32)],
        compiler_params=_params(("parallel", "arbitrary")),
    )(pre, w)


def _prev_rows_index(i_blk, tr):
    return jnp.maximum(i_blk * (tr // SUB) - 1, 0)


def _conv_silu_bwd(pre, w, dact):
    s, c = pre.shape
    width = w.shape[0]
    tr, tc = _conv_tiles(s, c)
    nr = s // tr

    nchunks = tr // CONV_CHUNK

    def body(x_ref, p_ref, w_ref, d_ref, dx_ref, dw_ref, head):
        @pl.when(pl.program_id(1) == 0)
        def _():
            head[...] = jnp.zeros_like(head)
            dw_ref[...] = jnp.zeros_like(dw_ref)

        wv = w_ref[...]

        def do(c0, window, later, dws):
            xs = [_lagged(window, width - 1 - k) for k in range(width)]
            y = _taps(xs, wv)
            sg = _sigmoid(y)
            dy = d_ref[pl.ds(c0, CONV_CHUNK), :] * sg * (1.0 + y * (1.0 - sg))
            dws = tuple(dw + jnp.sum(dy * x, axis=0, keepdims=True) for dw, x in zip(dws, xs))
            ahead = jnp.concatenate([dy, later], axis=0)
            dx_ref[pl.ds(c0, CONV_CHUNK), :] = _taps([_led(ahead, width - 1 - k) for k in range(width)], wv)
            return dy[:SUB], dws

        def chunk(it, carry):
            c0 = pl.multiple_of((nchunks - 1 - it) * CONV_CHUNK, CONV_CHUNK)
            return do(c0, x_ref[pl.ds(pl.multiple_of(c0 - SUB, SUB), CONV_CHUNK + SUB), :], *carry)

        zero = jnp.zeros((1, LANE), F32)
        carry = lax.fori_loop(0, nchunks - 1, chunk, (head[...], (zero,) * width))
        before = jnp.where(pl.program_id(1) == nr - 1, 0.0, p_ref[...])
        later, dws = do(0, jnp.concatenate([before, x_ref[:CONV_CHUNK, :]], axis=0), *carry)
        head[...] = later
        for k in range(width):
            dw_ref[k:k + 1, :] += dws[k]

    blk = pl.BlockSpec((tr, tc), lambda j, i: (nr - 1 - i, j))
    prev = pl.BlockSpec((SUB, tc), lambda j, i: (_prev_rows_index(nr - 1 - i, tr), j))
    wblk = pl.BlockSpec((width, tc), lambda j, i: (0, j))
    return pl.pallas_call(
        _per_lane_tile(body), name="conv_silu_bwd", grid=(c // tc, nr),
        in_specs=[blk, prev, wblk, blk], out_specs=[blk, wblk],
        out_shape=[jax.ShapeDtypeStruct((s, c), F32), jax.ShapeDtypeStruct((width, c), F32)],
        scratch_shapes=[pltpu.VMEM((SUB, tc), F32)],
        compiler_params=_params(("parallel", "arbitrary")),
    )(pre, pre, w, dact)


def _ffn_act_fwd(pre, w, b):
    _, halves, s, c = pre.shape
    width = w.shape[2]
    tr, tc = _conv_tiles(s, c)
    ncb = c // tc

    def body(x_ref, w_ref, b_ref, o_ref, tail):
        @pl.when(pl.program_id(2) == 0)
        def _():
            tail[...] = jnp.zeros_like(tail)

        wg, wv, bg, bv = w_ref[0], w_ref[1], b_ref[0], b_ref[1]

        def do(c0, win_g, win_v):
            yg = _taps([_lagged(win_g, width - 1 - k) for k in range(width)], wg) + bg
            yv = _taps([_lagged(win_v, width - 1 - k) for k in range(width)], wv) + bv
            o_ref[pl.ds(c0, CONV_CHUNK), :] = (yg * _sigmoid(yg) * yv).astype(BF16)

        def chunk(ci, carry):
            c0 = pl.multiple_of(ci * CONV_CHUNK, CONV_CHUNK)
            rows = pl.ds(pl.multiple_of(c0 - SUB, SUB), CONV_CHUNK + SUB)
            do(c0, x_ref[0, rows, :], x_ref[1, rows, :])
            return carry

        do(0, jnp.concatenate([tail[0], x_ref[0, :CONV_CHUNK, :]], axis=0),
           jnp.concatenate([tail[1], x_ref[1, :CONV_CHUNK, :]], axis=0))
        lax.fori_loop(1, tr // CONV_CHUNK, chunk, 0)
        tail[...] = x_ref[:, tr - SUB:, :]

    return pl.pallas_call(
        _per_lane_tile(body), name="ffn_act_fwd", grid=(halves, ncb, s // tr),
        in_specs=[pl.BlockSpec((2, None, tr, tc), lambda h, j, i: (0, h, i, j)),
                  pl.BlockSpec((2, None, width, tc), lambda h, j, i: (0, h, 0, j)),
                  pl.BlockSpec((2, None, 1, tc), lambda h, j, i: (0, h, 0, j))],
        out_specs=pl.BlockSpec((tr, tc), lambda h, j, i: (i, h * ncb + j)),
        out_shape=jax.ShapeDtypeStruct((s, halves * c), BF16),
        scratch_shapes=[pltpu.VMEM((2, SUB, tc), F32)],
        compiler_params=_params(("parallel", "parallel", "arbitrary")),
    )(pre, w, b)


def _ffn_act_bwd(pre, w, b, dact):
    _, halves, s, c = pre.shape
    width = w.shape[2]
    tr, tc = _conv_tiles(s, c)
    ncb = c // tc
    nr = s // tr
    nchunks = tr // CONV_CHUNK

    def body(x_ref, p_ref, w_ref, b_ref, d_ref, dx_ref, dw_ref, db_ref, head):
        @pl.when(pl.program_id(2) == 0)
        def _():
            for r in (head, dw_ref, db_ref):
                r[...] = jnp.zeros_like(r)

        wg, wv, bg, bv = w_ref[0], w_ref[1], b_ref[0], b_ref[1]

        def do(c0, win_g, win_v, later_g, later_v, dwg, dwv, dbg, dbv):
            xg = [_lagged(win_g, width - 1 - k) for k in range(width)]
            xv = [_lagged(win_v, width - 1 - k) for k in range(width)]
            yg = _taps(xg, wg) + bg
            yv = _taps(xv, wv) + bv
            sg = _sigmoid(yg)
            da = d_ref[pl.ds(c0, CONV_CHUNK), :]
            dyv = da * yg * sg
            dyg = da * yv * sg * (1.0 + yg * (1.0 - sg))
            dwg = tuple(dw + jnp.sum(dyg * x, axis=0, keepdims=True) for dw, x in zip(dwg, xg))
            dwv = tuple(dw + jnp.sum(dyv * x, axis=0, keepdims=True) for dw, x in zip(dwv, xv))
            dbg = dbg + jnp.sum(dyg, axis=0, keepdims=True)
            dbv = dbv + jnp.sum(dyv, axis=0, keepdims=True)
            ahead_g = jnp.concatenate([dyg, later_g], axis=0)
            ahead_v = jnp.concatenate([dyv, later_v], axis=0)
            dx_ref[0, pl.ds(c0, CONV_CHUNK), :] = _taps([_led(ahead_g, width - 1 - k) for k in range(width)], wg)
            dx_ref[1, pl.ds(c0, CONV_CHUNK), :] = _taps([_led(ahead_v, width - 1 - k) for k in range(width)], wv)
            return dyg[:SUB], dyv[:SUB], dwg, dwv, dbg, dbv

        def chunk(it, carry):
            c0 = pl.multiple_of((nchunks - 1 - it) * CONV_CHUNK, CONV_CHUNK)
            rows = pl.ds(pl.multiple_of(c0 - SUB, SUB), CONV_CHUNK + SUB)
            return do(c0, x_ref[0, rows, :], x_ref[1, rows, :], *carry)

        zero = jnp.zeros((1, LANE), F32)
        carry = lax.fori_loop(0, nchunks - 1, chunk,
                              (head[0], head[1], (zero,) * width, (zero,) * width, zero, zero))
        before = jnp.where(pl.program_id(2) == nr - 1, 0.0, p_ref[...])
        later_g, later_v, dwg, dwv, dbg, dbv = do(
            0, jnp.concatenate([before[0], x_ref[0, :CONV_CHUNK, :]], axis=0),
            jnp.concatenate([before[1], x_ref[1, :CONV_CHUNK, :]], axis=0), *carry)
        head[0] = later_g
        head[1] = later_v
        db_ref[0] += dbg
        db_ref[1] += dbv
        for k in range(width):
            dw_ref[0, k:k + 1, :] += dwg[k]
            dw_ref[1, k:k + 1, :] += dwv[k]

    blk = pl.BlockSpec((2, None, tr, tc), lambda h, j, i: (0, h, nr - 1 - i, j))
    prev = pl.BlockSpec((2, None, SUB, tc), lambda h, j, i: (0, h, _prev_rows_index(nr - 1 - i, tr), j))
    wblk = pl.BlockSpec((2, None, width, tc), lambda h, j, i: (0, h, 0, j))
    bblk = pl.BlockSpec((2, None, 1, tc), lambda h, j, i: (0, h, 0, j))
    return pl.pallas_call(
        _per_lane_tile(body), name="ffn_act_bwd", grid=(halves, ncb, nr),
        in_specs=[blk, prev, wblk, bblk, pl.BlockSpec((tr, tc), lambda h, j, i: (nr - 1 - i, h * ncb + j))],
        out_specs=[blk, wblk, bblk],
        out_shape=[jax.ShapeDtypeStruct(pre.shape, F32), jax.ShapeDtypeStruct(w.shape, F32),
                   jax.ShapeDtypeStruct(b.shape, F32)],
        scratch_shapes=[pltpu.VMEM((2, SUB, tc), F32)],
        compiler_params=_params(("parallel", "parallel", "arbitrary")),
    )(pre, pre, w, b, dact)


def _tri_masks():
    row = lax.broadcasted_iota(jnp.int32, (CHUNK, CHUNK), 0)
    col = lax.broadcasted_iota(jnp.int32, (CHUNK, CHUNK), 1)
    return row, col


def _tri_inv(ms, row, col):
    eye = (row == col).astype(F32)
    same_blk = (row >> 4) == (col >> 4)
    mds = [jnp.where(same_blk, m, 0.0) for m in ms]
    offs = [m - md for m, md in zip(ms, mds)]
    xs = [eye - md for md in mds]
    ps = [_bdot(md, md, NN) for md in mds]
    for _ in range(2):
        rs = [_bdot(jnp.concatenate([x, p], axis=0), p, NN) for x, p in zip(xs, ps)]
        xs = [x + r[:CHUNK] for x, r in zip(xs, rs)]
        ps = [r[CHUNK:] for r in rs]
    xs = [x + _bdot(x, p, NN) for x, p in zip(xs, ps)]
    ps = [_bdot(x, off, NN) for x, off in zip(xs, offs)]
    pps = [_bdot(p, p, NN) for p in ps]
    ys = [eye - p for p in ps]
    ys = [y + _bdot(y, pp, NN) for y, pp in zip(ys, pps)]
    return [_bdot(y, x, NN) for y, x in zip(ys, xs)]


def _gdn_gates(ba, alog, dtb, row, col):
    sig = _sigmoid(ba)
    neg_a = -jnp.exp(alog)
    g = neg_a * _softplus(ba + dtb)
    lower = (row >= col).astype(F32)
    gcum = _dot(lower, g, NN, HIGHEST)
    return sig, neg_a, g, gcum


def _gdn_head_common(q_raw, k_raw, v, beta, gc, gr, row, col):
    causal = row >= col
    strict = row > col
    rq = lax.rsqrt(jnp.sum(q_raw * q_raw, axis=-1, keepdims=True) + EPS)
    rk = lax.rsqrt(jnp.sum(k_raw * k_raw, axis=-1, keepdims=True) + EPS)
    q = q_raw * (rq * (A_HEAD ** -0.5))
    k = k_raw * rk
    decay = jnp.where(causal, jnp.exp(jnp.where(causal, gc - gr, 0.0)), 0.0)
    eg = jnp.exp(gc)
    gl = gc[CHUNK - 1:CHUNK, :]
    ekl = jnp.exp(gl - gc)
    dec = jnp.exp(gl)
    kb = k * beta
    kbq = jnp.concatenate([kb, q], axis=0)
    both = _bdot(kbq, k, NT)
    kk, qk = both[:CHUNK], both[CHUNK:]
    a = jnp.where(causal, qk * decay, 0.0)
    return dict(rq=rq, rk=rk, q=q, k=k, decay=decay, eg=eg, ekl=ekl, dec=dec, kb=kb, kbq=kbq, kk=kk, qk=qk, a=a,
                vb=v * beta, kbg=kb * eg, qd=q * eg, ke=k * ekl, causal=causal, strict=strict)


def _gdn_fwd(qkv, ba, z, alog, dtb, wn):
    s = qkv.shape[0]
    nc = s // CHUNK

    def body(qkv_ref, ba_ref, z_ref, alog_ref, dtb_ref, wn_ref, y_ref, o_ref, st_ref, t_ref, w_ref, vn_ref, state):
        @pl.when(pl.program_id(0) == 0)
        def _():
            state[...] = jnp.zeros_like(state)

        row, col = _tri_masks()
        sig, _, _, gcum = _gdn_gates(ba_ref[...], alog_ref[...], dtb_ref[...], row, col)
        gt = gcum.T
        heads = range(A_HEADS)
        lanes = [slice(h * A_HEAD, (h + 1) * A_HEAD) for h in heads]
        fs = [_gdn_head_common(qkv_ref[:, lanes[h]], qkv_ref[:, A_QK + h * A_HEAD:A_QK + (h + 1) * A_HEAD],
                               qkv_ref[:, 2 * A_QK + h * A_HEAD:2 * A_QK + (h + 1) * A_HEAD],
                               sig[:, h:h + 1], gcum[:, 8 + h:9 + h], gt[8 + h:9 + h, :], row, col) for h in heads]
        ts = [t.astype(BF16) for t in
              _tri_inv([jnp.where(f["strict"], f["kk"] * f["decay"], 0.0) for f in fs], row, col)]
        uws = [_bdot(t, jnp.concatenate([f["vb"], f["kbg"]], axis=1), NN) for t, f in zip(ts, fs)]
        s0s = [state[h] for h in heads]
        ws_ = [uw[:, A_HEAD:].astype(BF16) for uw in uws]
        wss = [_bdot(jnp.concatenate([w, f["qd"].astype(BF16)], axis=0), s0, NN) for w, f, s0 in zip(ws_, fs, s0s)]
        vnews = [(uw[:, :A_HEAD] - wsq[:CHUNK]).astype(BF16) for uw, wsq in zip(uws, wss)]
        os_ = [wsq[CHUNK:] + _bdot(f["a"], vn, NN) for wsq, f, vn in zip(wss, fs, vnews)]
        s1s = [s0 * f["dec"] + _bdot(f["ke"], vn, TN) for s0, f, vn in zip(s0s, fs, vnews)]
        for h in heads:
            ln = lanes[h]
            st_ref[0, h] = s0s[h]
            t_ref[0, h] = ts[h]
            w_ref[:, ln] = ws_[h]
            vn_ref[:, ln] = vnews[h]
            state[h] = s1s[h]
            o = os_[h]
            o_ref[:, ln] = o
            r = lax.rsqrt(jnp.mean(o * o, axis=-1, keepdims=True) + EPS)
            zz = z_ref[:, ln]
            y_ref[:, ln] = (o * r * wn_ref[...] * zz * _sigmoid(zz)).astype(BF16)

    vec = pl.BlockSpec((1, LANE), lambda n: (0, 0))
    wide = pl.BlockSpec((CHUNK, A_QK), lambda n: (n, 0))
    return pl.pallas_call(
        body, name="gdn_fwd", grid=(nc,),
        in_specs=[pl.BlockSpec((CHUNK, A_CONV_WIDTH), lambda n: (n, 0)),
                  pl.BlockSpec((CHUNK, LANE), lambda n: (n, 0)), wide, vec, vec, vec],
        out_specs=[wide, wide, pl.BlockSpec((1, A_HEADS, A_HEAD, A_HEAD), lambda n: (n, 0, 0, 0)),
                   pl.BlockSpec((1, A_HEADS, CHUNK, CHUNK), lambda n: (n, 0, 0, 0)), wide, wide],
        out_shape=[jax.ShapeDtypeStruct((s, A_QK), BF16), jax.ShapeDtypeStruct((s, A_QK), F32),
                   jax.ShapeDtypeStruct((nc, A_HEADS, A_HEAD, A_HEAD), F32),
                   jax.ShapeDtypeStruct((nc, A_HEADS, CHUNK, CHUNK), BF16),
                   jax.ShapeDtypeStruct((s, A_QK), BF16), jax.ShapeDtypeStruct((s, A_QK), BF16)],
        scratch_shapes=[pltpu.VMEM((A_HEADS, A_HEAD, A_HEAD), F32)],
        compiler_params=_params(("arbitrary",)),
    )(qkv, ba, z, alog, dtb, wn)


def _gdn_bwd(qkv, ba, z, o_raw, dy, states, t_all, w_all, vn_all, alog, dtb, wn):
    s = qkv.shape[0]
    nc = s // CHUNK

    def body(qkv_ref, ba_ref, z_ref, o_ref, dy_ref, st_ref, t_ref, w_ref, vn_ref, alog_ref, dtb_ref, wn_ref,
             dqkv_ref, dba_ref, dz_ref, dalog_ref, ddtb_ref, dwn_ref, dstate):
        @pl.when(pl.program_id(0) == 0)
        def _():
            for r in (dstate, dalog_ref, ddtb_ref, dwn_ref):
                r[...] = jnp.zeros_like(r)

        row, col = _tri_masks()
        bat = ba_ref[...]
        sig, neg_a, g, gcum = _gdn_gates(bat, alog_ref[...], dtb_ref[...], row, col)
        gt = gcum.T
        lane = lax.broadcasted_iota(jnp.int32, (CHUNK, LANE), 1)
        ones = jnp.ones((CHUNK, LANE), F32)
        last_row = lax.broadcasted_iota(jnp.int32, (CHUNK, 1), 0) == CHUNK - 1
        wnv = wn_ref[...]
        dgc_tile = jnp.zeros((CHUNK, LANE), F32)
        dbeta_tile = jnp.zeros((CHUNK, LANE), F32)
        dwn_acc = jnp.zeros((1, LANE), F32)
        hs = []
        for h in range(A_HEADS):
            ln = slice(h * A_HEAD, (h + 1) * A_HEAD)
            lk = slice(A_QK + h * A_HEAD, A_QK + (h + 1) * A_HEAD)
            lv = slice(2 * A_QK + h * A_HEAD, 2 * A_QK + (h + 1) * A_HEAD)
            q_raw, k_raw, v = qkv_ref[:, ln], qkv_ref[:, lk], qkv_ref[:, lv]
            f = _gdn_head_common(q_raw, k_raw, v, sig[:, h:h + 1], gcum[:, 8 + h:9 + h], gt[8 + h:9 + h, :], row, col)
            f.update(h=h, ln=ln, lk=lk, lv=lv, q_raw=q_raw, k_raw=k_raw, v=v, beta=sig[:, h:h + 1],
                     s0=st_ref[0, h], ds1=dstate[h], t=t_ref[0, h], w=w_ref[:, ln], vnew=vn_ref[:, ln])
            o = o_ref[:, ln]
            zz = z_ref[:, ln]
            dyv = dy_ref[:, ln]
            r = lax.rsqrt(jnp.mean(o * o, axis=-1, keepdims=True) + EPS)
            sz = _sigmoid(zz)
            silu = zz * sz
            dz_ref[:, ln] = dyv * o * r * wnv * sz * (1.0 + zz * (1.0 - sz))
            dwn_acc = dwn_acc + jnp.sum(dyv * silu * o * r, axis=0, keepdims=True)
            tt = dyv * silu * wnv
            do = r * tt - o * (r * r * r) * jnp.mean(tt * o, axis=-1, keepdims=True)
            f["do_b"] = do.astype(BF16)
            hs.append(f)
        for f in hs:
            f["dvnew"] = _bdot(f["a"], f["do_b"], TN) + _bdot(f["ke"], f["ds1"], NN)
            f["da"] = jnp.where(f["causal"], _bdot(f["do_b"], f["vnew"], NT), 0.0)
            f["dke"] = _bdot(f["vnew"], f["ds1"], NT)
            f["ddec"] = jnp.sum(jnp.sum(f["s0"] * f["ds1"], axis=1, keepdims=True), axis=0, keepdims=True)
        for f in hs:
            do_dv = jnp.concatenate([f["do_b"], f["dvnew"].astype(BF16)], axis=0)
            both = _bdot(do_dv, f["s0"], NT)
            f["dqd"], f["dw"] = both[:CHUNK], -both[CHUNK:]
            qd_w = jnp.concatenate([f["qd"].astype(BF16), -f["w"]], axis=0)
            dstate[f["h"]] = _bdot(qd_w, do_dv, TN) + f["dec"] * f["ds1"]
        for f in hs:
            dd = jnp.concatenate([f["dvnew"], f["dw"]], axis=1).astype(BF16)
            tdd = _bdot(f["t"], dd, TN)
            f["dvb"], f["dkbg"] = tdd[:, :A_HEAD], tdd[:, A_HEAD:]
            f["dt"] = _bdot(dd, jnp.concatenate([f["vb"], f["kbg"]], axis=1), NT)
        for f in hs:
            f["tdt"] = _bdot(f["t"], f["dt"], TN)
        for f in hs:
            dm = jnp.where(f["strict"], -_bdot(f["tdt"], f["t"], NT), 0.0)
            f["ddecay"] = (dm * f["kk"] + f["da"] * f["qk"]) * f["decay"]
            f["dboth"] = jnp.concatenate([dm * f["decay"], f["da"] * f["decay"]], axis=0).astype(BF16)
        for f in hs:
            f["r2"] = _bdot(f["dboth"], f["k"], NN)
            f["dk0"] = _bdot(f["dboth"], f["kbq"], TN)
        for f in hs:
            h, k, beta = f["h"], f["k"], f["beta"]
            dkb = f["r2"][:CHUNK] + f["dkbg"] * f["eg"]
            dq = f["r2"][CHUNK:] + f["dqd"] * f["eg"]
            dk = f["dk0"] + f["dke"] * f["ekl"] + dkb * beta
            dke_ke = jnp.sum(f["dke"] * f["ke"], axis=-1, keepdims=True)
            dgc = (jnp.sum(f["ddecay"], axis=-1, keepdims=True)
                   + jnp.sum(f["dqd"] * f["qd"], axis=-1, keepdims=True) - dke_ke
                   + jnp.sum(f["dkbg"] * f["kbg"], axis=-1, keepdims=True))
            dgl = jnp.sum(dke_ke, axis=0, keepdims=True) + f["ddec"] * f["dec"]
            dgc = dgc + jnp.where(last_row, dgl, 0.0)
            dbeta = jnp.sum(dkb * k, axis=-1, keepdims=True) + jnp.sum(f["dvb"] * f["v"], axis=-1, keepdims=True)
            dgc_tile = dgc_tile + jnp.where(lane == 8 + h, dgc, 0.0)
            dbeta_tile = dbeta_tile + jnp.where(lane == h, dbeta, 0.0)
            dqn = dq * (A_HEAD ** -0.5)
            rq, rk, q_raw, k_raw = f["rq"], f["rk"], f["q_raw"], f["k_raw"]
            dqkv_ref[:, f["ln"]] = rq * dqn - q_raw * (rq * rq * rq) * jnp.sum(dqn * q_raw, axis=-1, keepdims=True)
            dqkv_ref[:, f["lk"]] = rk * dk - k_raw * (rk * rk * rk) * jnp.sum(dk * k_raw, axis=-1, keepdims=True)
            dqkv_ref[:, f["lv"]] = f["dvb"] * beta
        ddecays = [f["ddecay"] for f in hs]
        col_sums = _dot(jnp.concatenate(ddecays, axis=1), ones, TN, HIGHEST)
        for h in range(A_HEADS):
            dgc_tile = dgc_tile - jnp.where(lane == 8 + h, col_sums[h * CHUNK:(h + 1) * CHUNK, :1], 0.0)
        upper = (row <= col).astype(F32)
        dg = _dot(upper, dgc_tile, NN, HIGHEST)
        da_raw = dg * neg_a * _sigmoid(bat + dtb_ref[...])
        dba_ref[...] = jnp.where(lane < 8, dbeta_tile * sig * (1.0 - sig), jnp.where(lane < 16, da_raw, 0.0))
        dwn_ref[...] += dwn_acc
        ddtb_ref[...] += jnp.sum(da_raw, axis=0, keepdims=True)
        dalog_ref[...] += jnp.sum(dg * g, axis=0, keepdims=True)

    rev = lambda n: (nc - 1 - n, 0)
    vec = pl.BlockSpec((1, LANE), lambda n: (0, 0))
    wide = pl.BlockSpec((CHUNK, A_QK), rev)
    qkv_blk = pl.BlockSpec((CHUNK, A_CONV_WIDTH), rev)
    ba_blk = pl.BlockSpec((CHUNK, LANE), rev)
    vsh = jax.ShapeDtypeStruct((1, LANE), F32)
    return pl.pallas_call(
        body, name="gdn_bwd", grid=(nc,),
        in_specs=[qkv_blk, ba_blk, wide, wide, wide,
                  pl.BlockSpec((1, A_HEADS, A_HEAD, A_HEAD), lambda n: (nc - 1 - n, 0, 0, 0)),
                  pl.BlockSpec((1, A_HEADS, CHUNK, CHUNK), lambda n: (nc - 1 - n, 0, 0, 0)), wide, wide,
                  vec, vec, vec],
        out_specs=[qkv_blk, ba_blk, wide, vec, vec, vec],
        out_shape=[jax.ShapeDtypeStruct((s, A_CONV_WIDTH), F32), jax.ShapeDtypeStruct((s, LANE), F32),
                   jax.ShapeDtypeStruct((s, A_QK), F32), vsh, vsh, vsh],
        scratch_shapes=[pltpu.VMEM((A_HEADS, A_HEAD, A_HEAD), F32)],
        compiler_params=_params(("arbitrary",)),
    )(qkv, ba, z, o_raw, dy, states, t_all, w_all, vn_all, alog, dtb, wn)


REL_RING = 1024
QBLK_BITS = 8


def _rel_ring_onehot():
    m = lax.broadcasted_iota(jnp.int32, (REL_RING, REL_PAD), 0)
    t = lax.broadcasted_iota(jnp.int32, (REL_RING, REL_PAD), 1)
    u = jnp.where(m < KBLK, m, m - REL_RING)
    idx = jnp.clip(LEFT - u, -REL_CLIP, REL_CLIP) + REL_CLIP
    return (t == idx).astype(F32)


def _relbias_ring(table, transpose):
    n_in, n_out = (REL_RING, REL_PAD) if transpose else (REL_PAD, REL_RING)

    def body(t_ref, o_ref):
        o_ref[...] = _dot(t_ref[...], _rel_ring_onehot(), NN if transpose else NT, HIGHEST)

    return pl.pallas_call(
        body, name="relbias_ring_bwd" if transpose else "relbias_ring",
        out_shape=jax.ShapeDtypeStruct((B_HEADS, n_out), F32),
        in_specs=[VMEM_FULL], out_specs=VMEM_FULL,
        compiler_params=_params(),
    )(table)


def _row_bit(shape, bit):
    return ((lax.broadcasted_iota(jnp.int32, shape, 0) >> bit) & 1) == 1


def _relbias_expand(ring):
    def body(r_ref, o_ref):
        b = jnp.broadcast_to(r_ref[0], (QBLK, REL_RING))
        for bit in range(QBLK_BITS):
            b = jnp.where(_row_bit(b.shape, bit), pltpu.roll(b, 1 << bit, 1), b)
        j = lax.broadcasted_iota(jnp.int32, (QBLK, KBLK), 1)
        r = lax.broadcasted_iota(jnp.int32, (QBLK, KBLK), 0)
        lo = (r >> 6) << 6
        o_ref[0] = jnp.where((j >= lo) & (j < lo + LEFT + CHUNK), b[:, :KBLK], NEG_INF)

    return pl.pallas_call(
        body, name="relbias_expand", grid=(B_HEADS,),
        in_specs=[pl.BlockSpec((1, 1, REL_RING), lambda h: (h, 0, 0))],
        out_specs=pl.BlockSpec((1, QBLK, KBLK), lambda h: (h, 0, 0)),
        out_shape=jax.ShapeDtypeStruct((B_HEADS, QBLK, KBLK), F32),
        compiler_params=_params(("parallel",)),
    )(ring)


def _relbias_reduce(ds):
    def body(d_ref, o_ref):
        d = jnp.concatenate([d_ref[0], jnp.zeros((QBLK, REL_RING - KBLK), F32)], axis=1)
        for bit in range(QBLK_BITS):
            d = jnp.where(_row_bit(d.shape, bit), pltpu.roll(d, REL_RING - (1 << bit), 1), d)
        o_ref[0] = jnp.sum(d, axis=0, keepdims=True)

    return pl.pallas_call(
        body, name="relbias_reduce", grid=(B_HEADS,),
        in_specs=[pl.BlockSpec((1, QBLK, KBLK), lambda h: (h, 0, 0))],
        out_specs=pl.BlockSpec((1, 1, REL_RING), lambda h: (h, 0, 0)),
        out_shape=jax.ShapeDtypeStruct((B_HEADS, 1, REL_RING), F32),
        compiler_params=_params(("parallel",)),
    )(ds)


def _attn_probs(q_ref, kb, b_ref, hh, q0):
    hl = slice(hh * B_HEAD, (hh + 1) * B_HEAD)
    qh = q_ref[:, hl] * (B_HEAD ** -0.5)
    kh = kb[:, hl]
    jpos = lax.broadcasted_iota(jnp.int32, (QBLK, KBLK), 1)
    sc = _bdot(qh, kh, NT) + b_ref[hh]
    sc = jnp.where(jpos + q0 >= LEFT, sc, NEG_INF)
    mx = jnp.max(sc, axis=-1, keepdims=True)
    p = jnp.exp(sc - mx)
    return p * (1.0 / jnp.sum(p, axis=-1, keepdims=True)), qh, kh


def _attn_fwd(q, kpad, vpad, bias):
    s = q.shape[0]

    def body(q_ref, k_ref, v_ref, b_ref, o_ref):
        q0 = pl.multiple_of(pl.program_id(1) * QBLK, QBLK)
        kb = k_ref[pl.ds(q0, KBLK), :]
        vb = v_ref[pl.ds(q0, KBLK), :]
        outs = []
        for hh in range(2):
            p, _, _ = _attn_probs(q_ref, kb, b_ref, hh, q0)
            outs.append(_bdot(p, vb[:, hh * B_HEAD:(hh + 1) * B_HEAD], NN))
        o_ref[...] = jnp.concatenate(outs, axis=1).astype(BF16)

    qblk = pl.BlockSpec((QBLK, LANE), lambda g, m: (m, g))
    kblk = pl.BlockSpec((LEFT + s, LANE), lambda g, m: (0, g))
    return pl.pallas_call(
        body, name="attn_fwd", grid=(B_HEADS // 2, s // QBLK),
        in_specs=[qblk, kblk, kblk, pl.BlockSpec((2, QBLK, KBLK), lambda g, m: (g, 0, 0))],
        out_specs=qblk,
        out_shape=jax.ShapeDtypeStruct((s, D_MODEL), BF16),
        compiler_params=_params(("parallel", "arbitrary")),
    )(q, kpad, vpad, bias)


def _attn_bwd(q, kpad, vpad, bias, do):
    s = q.shape[0]

    def body(q_ref, k_ref, v_ref, b_ref, do_ref, dq_ref, dk_ref, dv_ref, db_ref):
        @pl.when(pl.program_id(1) == 0)
        def _():
            for r in (dk_ref, dv_ref, db_ref):
                r[...] = jnp.zeros_like(r)

        q0 = pl.multiple_of(pl.program_id(1) * QBLK, QBLK)
        kb = k_ref[pl.ds(q0, KBLK), :]
        vb = v_ref[pl.ds(q0, KBLK), :]
        dqs, dks, dvs = [], [], []
        for hh in range(2):
            hl = slice(hh * B_HEAD, (hh + 1) * B_HEAD)
            p, qh, kh = _attn_probs(q_ref, kb, b_ref, hh, q0)
            doh = do_ref[:, hl]
            dp = _bdot(doh, vb[:, hl], NT)
            dsc = p * (dp - jnp.sum(p * dp, axis=-1, keepdims=True))
            db_ref[hh] += dsc
            dqs.append(_bdot(dsc, kh, NN) * (B_HEAD ** -0.5))
            dks.append(_bdot(dsc, qh, TN))
            dvs.append(_bdot(p, doh, TN))
        dq_ref[...] = jnp.concatenate(dqs, axis=1)
        dk_ref[pl.ds(q0, KBLK), :] += jnp.concatenate(dks, axis=1)
        dv_ref[pl.ds(q0, KBLK), :] += jnp.concatenate(dvs, axis=1)

    qblk = pl.BlockSpec((QBLK, LANE), lambda g, m: (m, g))
    kblk = pl.BlockSpec((LEFT + s, LANE), lambda g, m: (0, g))
    bblk = pl.BlockSpec((2, QBLK, KBLK), lambda g, m: (g, 0, 0))
    return pl.pallas_call(
        body, name="attn_bwd", grid=(B_HEADS // 2, s // QBLK),
        in_specs=[qblk, kblk, kblk, bblk, qblk],
        out_specs=[qblk, kblk, kblk, bblk],
        out_shape=[jax.ShapeDtypeStruct((s, D_MODEL), F32), jax.ShapeDtypeStruct((LEFT + s, D_MODEL), F32),
                   jax.ShapeDtypeStruct((LEFT + s, D_MODEL), F32),
                   jax.ShapeDtypeStruct((B_HEADS, QBLK, KBLK), F32)],
        compiler_params=_params(("parallel", "arbitrary")),
    )(q, kpad, vpad, bias, do)


def _adamw(w, g, m, v):
    r, c = w.shape
    tr = r
    for cand in (512, 256, 128, 64, 32, 16, 8):
        if r % cand == 0 and cand * c * 4 <= 2 * 1024 * 1024:
            tr = cand
            break
    c1 = 1.0 / (1.0 - ADAM_B1 ** ADAM_STEP)
    c2 = 1.0 / (1.0 - ADAM_B2 ** ADAM_STEP)

    def body(w_ref, g_ref, m_ref, v_ref, d_ref, mo_ref, vo_ref):
        gv = g_ref[...]
        mn = ADAM_B1 * m_ref[...] + (1.0 - ADAM_B1) * gv
        vn = ADAM_B2 * v_ref[...] + (1.0 - ADAM_B2) * (gv * gv)
        mo_ref[...] = mn
        vo_ref[...] = vn
        d_ref[...] = -ADAM_LR * ((mn * c1) / (jnp.sqrt(vn * c2) + ADAM_EPS) + ADAM_WD * w_ref[...])

    blk = pl.BlockSpec((tr, c), lambda i: (i, 0))
    sh = jax.ShapeDtypeStruct((r, c), F32)
    return pl.pallas_call(
        body, name="adamw", grid=(r // tr,),
        in_specs=[blk] * 4, out_specs=[blk] * 3, out_shape=[sh] * 3,
        compiler_params=_params(("parallel",)),
    )(w, g, m, v)


def _row(v, width=None):
    v = v.reshape(1, -1)
    if width is not None and v.shape[1] < width:
        v = jnp.pad(v, ((0, 0), (0, width - v.shape[1])))
    return v


def _gate_row(v):
    return jnp.pad(v.reshape(1, A_HEADS), ((0, 0), (A_HEADS, LANE - 2 * A_HEADS)))


DEPTH = 4
N_A = 2
N_B = 2
F_DOWN_ROWS = FFN_DIM // N_SHARDS
SQ_ROWS = D_MODEL // N_SHARDS
GD_A_OUT0 = DEPTH * F_DOWN_ROWS // SQ_ROWS
GD_B_Q0 = GD_A_OUT0 + N_A
GD_B_OUT0 = GD_B_Q0 + N_B
UP_COLS = 2 * FFN_DIM // N_SHARDS


def _a_layer_fwd(h, w, i):
    xn = _rmsnorm_fwd(h, _row(w["a_norm"][i]))
    w_qkv, w_z, w_ba = w["a_in"][i]
    pre = _mm(xn, w_qkv, name="a_qkv")
    z = _mm(xn, w_z, name="a_z")
    ba = _mm(xn, w_ba, name="a_ba")
    act = _conv_silu_fwd(pre, w["a_conv"][i])
    alog, dtb, wn = _gate_row(w["a_A_log"][i]), _gate_row(w["a_dt_bias"][i]), _row(w["a_out_norm"][i])
    y, o_raw, states, t_all, w_all, vn_all = _gdn_fwd(act, ba, z, alog, dtb, wn)
    h2 = _mm_rowsh(y, w["GD"], SQ_ROWS, GD_A_OUT0 + i, "nn", "a_out", res=h)
    saved = dict(h=h, xn=xn, pre=pre, z=z, ba=ba, act=act, o_raw=o_raw, y=y, states=states,
                 t_all=t_all, w_all=w_all, vn_all=vn_all, alog=alog, dtb=dtb, wn=wn)
    return h2, saved


def _a_layer_bwd(dh2, w, i, sv):
    g = {}
    g["w_out"] = _mm_rowsh_dw(sv["y"], dh2, SQ_ROWS, "a_out_dw")
    dy = _mm_rowsh(dh2, w["GD"], SQ_ROWS, GD_A_OUT0 + i, "nt", "a_out_dx")
    dact, dba, dz, dalog, ddtb, dwn = _gdn_bwd(sv["act"], sv["ba"], sv["z"], sv["o_raw"], dy, sv["states"],
                                               sv["t_all"], sv["w_all"], sv["vn_all"],
                                               sv["alog"], sv["dtb"], sv["wn"])
    dpre, g["conv"] = _conv_silu_bwd(sv["pre"], w["a_conv"][i], dact)
    xn = sv["xn"]
    w_qkv, w_z, w_ba = w["a_in"][i]
    d_in = jnp.concatenate([_mm(xn, dpre, "tn", name="a_qkv_dw"), _mm(xn, dz, "tn", name="a_z_dw"),
                            _mm(xn, dba, "tn", name="a_ba_dw")[:, :2 * A_HEADS]], axis=1)
    g["w_in"] = jnp.transpose(d_in.reshape(D_MODEL, N_SHARDS, -1), (1, 0, 2))
    dxn = _mm(dpre, w_qkv, "nt", name="a_qkv_dx")
    dxn = _mm(dz, w_z, "nt", res=dxn, name="a_z_dx")
    dh, dnorm = _mm(dba, w_ba, "nt", res=dxn, name="a_ba_dx", norm=(sv["h"], _row(w["a_norm"][i]), dh2))
    g["norm"] = dnorm[0]
    g["A_log"] = dalog[0, A_HEADS:2 * A_HEADS]
    g["dt_bias"] = ddtb[0, A_HEADS:2 * A_HEADS]
    g["out_norm"] = dwn[0]
    return dh, g


def _by_half(a, lead):
    return jnp.moveaxis(a.reshape(a.shape[:-1] + (2, 2, UP_COLS)), (-3, -2), (0, 1)).reshape((2, 2) + lead + (UP_COLS,))


def _from_half(a):
    lead = a.shape[2:-1]
    return jnp.moveaxis(a, (0, 1), (-3, -2)).reshape(lead + (N_SHARDS * UP_COLS,))


def _ffn_fwd(h, w, l):
    s = h.shape[0]
    xn = _rmsnorm_fwd(h, _row(w["f_norm"][l]))
    cw = _by_half(w["f_conv"][l], (w["f_conv"].shape[1],))
    cb = _by_half(w["f_conv_b"][l][None], (1,))
    pre = _mm_colsh(xn, w["GU"], D_MODEL, l, "nn", "f_up").reshape(2, 2, s, UP_COLS)
    act = _ffn_act_fwd(pre, cw, cb)
    h2 = _mm_rowsh(act, w["GD"], F_DOWN_ROWS, l, "nn", "f_down", res=h)
    return h2, dict(h=h, xn=xn, pre=pre, act=act, cw=cw, cb=cb)


def _ffn_bwd(dh2, w, l, sv):
    g = {}
    s = dh2.shape[0]
    g["w_down"] = _mm_rowsh_dw(sv["act"], dh2, F_DOWN_ROWS, "f_down_dw")
    dact = _mm_rowsh(dh2, w["GD"], F_DOWN_ROWS, l, "nt", "f_down_dx")
    dpre, dcw, dcb = _ffn_act_bwd(sv["pre"], sv["cw"], sv["cb"], dact)
    dpre = dpre.reshape(N_SHARDS, s, UP_COLS)
    g["w_up"] = _mm_colsh_dw(sv["xn"], dpre, "f_up_dw")
    g["conv"] = _from_half(dcw)
    g["conv_b"] = _from_half(dcb)[0]
    dh, dnorm = _mm_colsh(dpre, w["GU"], D_MODEL, l, "nt", "f_up_dx", norm=(sv["h"], _row(w["f_norm"][l]), dh2))
    g["norm"] = dnorm[0]
    return dh, g


def _b_layer_fwd(h, w, j, kpad, vpad):
    xn = _rmsnorm_fwd(h, _row(w["b_norm"][j]))
    q = _mm_rowsh(xn, w["GD"], SQ_ROWS, GD_B_Q0 + j, "nn", "b_q")
    rel = w["b_rel_bias"][j]
    table = jnp.pad(rel, ((0, 0), (0, REL_PAD - rel.shape[1])))
    bias = _relbias_expand(_relbias_ring(table, False).reshape(B_HEADS, 1, REL_RING))
    o = _attn_fwd(q, kpad, vpad, bias)
    h2 = _mm_rowsh(o, w["GD"], SQ_ROWS, GD_B_OUT0 + j, "nn", "b_out", res=h)
    return h2, dict(h=h, xn=xn, q=q, o=o, bias=bias)


def _b_layer_bwd(dh2, w, j, sv, kpad, vpad):
    g = {}
    g["w_out"] = _mm_rowsh_dw(sv["o"], dh2, SQ_ROWS, "b_out_dw")
    do = _mm_rowsh(dh2, w["GD"], SQ_ROWS, GD_B_OUT0 + j, "nt", "b_out_dx")
    dq, dkp, dvp, dsc = _attn_bwd(sv["q"], kpad, vpad, sv["bias"], do)
    dring = _relbias_reduce(dsc).reshape(B_HEADS, REL_RING)
    g["rel_bias"] = _relbias_ring(dring, True)[:, :2 * REL_CLIP + 1]
    g["w_q"] = _mm_rowsh_dw(sv["xn"], dq, SQ_ROWS, "b_q_dw")
    dh, dnorm = _mm_rowsh(dq, w["GD"], SQ_ROWS, GD_B_Q0 + j, "nt", "b_q_dx",
                          norm=(sv["h"], _row(w["b_norm"][j]), dh2))
    g["norm"] = dnorm[0]
    return dh, g, dkp, dvp


def _local_step(x, tgt, w):
    h = x
    saved = []
    kv_saved = None
    kpad = vpad = None
    for layer in range(DEPTH):
        if layer < N_A:
            h, sm = _a_layer_fwd(h, w, layer)
        else:
            if layer == N_A:
                xn_kv = _rmsnorm_fwd(h, _row(w["kv_norm"]))
                kv = _mm_colsh(xn_kv, w["GK"], D_MODEL, 0, "nn", "kv", flat=True, out_dtype=BF16)
                kpad = jnp.pad(kv[:, :D_MODEL], ((LEFT, 0), (0, 0)))
                vpad = jnp.pad(kv[:, D_MODEL:], ((LEFT, 0), (0, 0)))
                kv_saved = dict(h=h, xn=xn_kv)
            h, sm = _b_layer_fwd(h, w, layer - N_A, kpad, vpad)
        h, sf = _ffn_fwd(h, w, layer)
        saved.append((sm, sf))

    loss, dh, dfinal = _final_loss(h, _row(w["final_norm"]), tgt)

    ga = [None] * N_A
    gb = [None] * N_B
    gf = [None] * DEPTH
    dk_tot = dv_tot = None
    g_kv = g_kvn = None
    for layer in reversed(range(DEPTH)):
        sm, sf = saved[layer]
        dh, gf[layer] = _ffn_bwd(dh, w, layer, sf)
        if layer >= N_A:
            dh, gb[layer - N_A], dkp, dvp = _b_layer_bwd(dh, w, layer - N_A, sm, kpad, vpad)
            dk_tot = dkp if dk_tot is None else dk_tot + dkp
            dv_tot = dvp if dv_tot is None else dv_tot + dvp
            if layer == N_A:
                dkv = jnp.concatenate([dk_tot[LEFT:], dv_tot[LEFT:]], axis=1)
                g_kv = _mm_colsh_dw(kv_saved["xn"], dkv, "kv_dw", flat=True)
                dh, g_kvn = _mm_colsh(dkv, w["GK"], D_MODEL, 0, "nt", "kv_dx", flat=True,
                                      norm=(kv_saved["h"], _row(w["kv_norm"]), dh))
        else:
            dh, ga[layer] = _a_layer_bwd(dh, w, layer, sm)

    def stack(gs, key):
        return jnp.stack([g[key] for g in gs])

    def rows(gs, key):
        return [g[key] for g in gs]

    grads = dict(
        GU=jnp.concatenate(rows(gf, "w_up"), axis=1),
        GD=jnp.concatenate(rows(gf, "w_down") + rows(ga, "w_out") + rows(gb, "w_q") + rows(gb, "w_out"), axis=1),
        GK=g_kv,
        GI=jnp.concatenate(rows(ga, "w_in"), axis=1),
        a_norm=stack(ga, "norm"), a_conv=stack(ga, "conv"), a_A_log=stack(ga, "A_log"),
        a_dt_bias=stack(ga, "dt_bias"), a_out_norm=stack(ga, "out_norm"), kv_norm=g_kvn[0],
        b_norm=stack(gb, "norm"), b_rel_bias=stack(gb, "rel_bias"),
        f_norm=stack(gf, "norm"), f_conv=stack(gf, "conv"), f_conv_b=stack(gf, "conv_b"), final_norm=dfinal[0])
    return loss, dh, grads


HBM_SPEC = pl.BlockSpec(memory_space=pl.ANY)
VMEM_SPEC = pl.BlockSpec(memory_space=pltpu.VMEM)


def _place():
    x, y, c = lax.axis_index("x"), lax.axis_index("y"), lax.axis_index("c")
    chips = [(1 - x, y), (x, 1 - y), (1 - x, 1 - y)]
    return x, y, c, chips


def _remote(src, dst, send_sem, recv_sem, to):
    return pltpu.make_async_remote_copy(src_ref=src, dst_ref=dst, send_sem=send_sem, recv_sem=recv_sem,
                                        device_id=to, device_id_type=MESH)


def _allgather_weights(shards):
    n = len(shards)
    per = 7

    def body(*refs):
        x_refs, out_refs, (send_sems, recv_sems) = refs[:n], refs[n:2 * n], refs[2 * n:]
        x, y, c, chips = _place()
        sibling = (x, y, 1 - c)

        def half(a, px, py, hc):
            rh = shards[a].shape[0] // 2
            return out_refs[a].at[2 * px + py, pl.ds(hc * rh, rh), :]

        def mine(a):
            rh = shards[a].shape[0] // 2
            return x_refs[a].at[pl.ds(c * rh, rh), :]

        def sems(a, k):
            return send_sems.at[per * a + k], recv_sems.at[per * a + k]

        first = [_remote(mine(a), half(a, x, y, c), *sems(a, j), (*chip, c))
                 for a in range(n) for j, chip in enumerate(chips)]
        own = [_remote(x_refs[a], out_refs[a].at[2 * x + y], *sems(a, 6), sibling) for a in range(n)]
        for cp in first + own:
            cp.start()
        passed = []
        for j, chip in enumerate(chips):
            for a in range(n):
                landed = half(a, *chip, c)
                _remote(landed, landed, *sems(a, j), (*chip, c)).wait_recv()
                cp = _remote(landed, landed, *sems(a, 3 + j), sibling)
                cp.start()
                passed.append(cp)
        for j, chip in enumerate(chips):
            for a in range(n):
                theirs = half(a, *chip, 1 - c)
                _remote(theirs, theirs, *sems(a, 3 + j), sibling).wait_recv()
        for cp in own:
            cp.wait_recv()
        for cp in first + passed + own:
            cp.wait_send()

    return pl.pallas_call(
        body, name="allgather_weights",
        out_shape=[jax.ShapeDtypeStruct((N_SHARDS,) + sh.shape, sh.dtype) for sh in shards],
        in_specs=[HBM_SPEC] * n, out_specs=[HBM_SPEC] * n,
        scratch_shapes=[pltpu.SemaphoreType.DMA((per * n,)), pltpu.SemaphoreType.DMA((per * n,))],
    )(*shards)


def _pair_exchange(gs):
    n = len(gs)

    def body(*refs):
        g_refs, out_refs, (send_sems, recv_sems) = refs[:n], refs[n:2 * n], refs[2 * n:]
        x, y, c, _ = _place()
        cps = []
        for a in range(n):
            rh = gs[a].shape[1] // 2
            cps.append(_remote(g_refs[a].at[:, pl.ds((1 - c) * rh, rh), :], out_refs[a], send_sems.at[a],
                               recv_sems.at[a], (x, y, 1 - c)))
        for cp in cps:
            cp.start()
        for cp in cps:
            cp.wait()

    return pl.pallas_call(
        body, name="rs_pair_exchange",
        out_shape=[jax.ShapeDtypeStruct((g.shape[0], g.shape[1] // 2, g.shape[2]), g.dtype) for g in gs],
        in_specs=[HBM_SPEC] * n, out_specs=[HBM_SPEC] * n,
        scratch_shapes=[pltpu.SemaphoreType.DMA((n,)), pltpu.SemaphoreType.DMA((n,))],
    )(*gs)


def _add_rows(rows, cols):
    best = 16
    for t in range(16, rows + 1, 16):
        if rows % t == 0 and t * cols * 4 <= 2304 * 1024:
            best = t
    return best


def _pair_add(g, other, c_idx):
    n, r, cols = g.shape
    rh = r // 2
    tr = _add_rows(rh, cols)
    nb = rh // tr

    def body(c_ref, a_ref, b_ref, o_ref, ob_ref):
        sm = a_ref[...] + b_ref[...]
        o_ref[...] = sm
        ob_ref[...] = sm.astype(BF16)

    out_blk = pl.BlockSpec((1, tr, cols), lambda s, i, c_ref: (s, i, 0))
    return pl.pallas_call(
        body, name="rs_pair_add",
        grid_spec=pltpu.PrefetchScalarGridSpec(
            num_scalar_prefetch=1, grid=(n, nb),
            in_specs=[pl.BlockSpec((1, tr, cols), lambda s, i, c_ref: (s, c_ref[0] * nb + i, 0)), out_blk],
            out_specs=[out_blk, out_blk]),
        out_shape=[jax.ShapeDtypeStruct((n, rh, cols), F32), jax.ShapeDtypeStruct((n, rh, cols), BF16)],
        compiler_params=_params(("parallel", "parallel")),
    )(c_idx, g, other)


def _chip_exchange(ps):
    n = len(ps)

    def body(*refs):
        p_refs, out_refs, (send_sems, recv_sems) = refs[:n], refs[n:2 * n], refs[2 * n:]
        x, y, c, chips = _place()
        cps = [_remote(p_refs[a].at[2 * chip[0] + chip[1]], out_refs[a].at[j], send_sems.at[3 * a + j],
                       recv_sems.at[3 * a + j], (*chip, c))
               for a in range(n) for j, chip in enumerate(chips)]
        for cp in cps:
            cp.start()
        for cp in cps:
            cp.wait()

    return pl.pallas_call(
        body, name="rs_chip_exchange",
        out_shape=[jax.ShapeDtypeStruct((3,) + p.shape[1:], p.dtype) for p in ps],
        in_specs=[HBM_SPEC] * n, out_specs=[HBM_SPEC] * n,
        scratch_shapes=[pltpu.SemaphoreType.DMA((3 * n,)), pltpu.SemaphoreType.DMA((3 * n,))],
    )(*ps)


def _chip_add(p, recv, chip_idx):
    n, rh, cols = p.shape
    tr = _add_rows(rh, cols)

    def body(s_ref, own_ref, r_ref, o_ref):
        o_ref[...] = ((own_ref[0] + r_ref[0].astype(F32)) + r_ref[1].astype(F32)) + r_ref[2].astype(F32)

    return pl.pallas_call(
        body, name="rs_chip_add",
        grid_spec=pltpu.PrefetchScalarGridSpec(
            num_scalar_prefetch=1, grid=(rh // tr,),
            in_specs=[pl.BlockSpec((1, tr, cols), lambda i, s_ref: (s_ref[0], i, 0)),
                      pl.BlockSpec((3, tr, cols), lambda i, s_ref: (0, i, 0))],
            out_specs=pl.BlockSpec((tr, cols), lambda i, s_ref: (i, 0))),
        out_shape=jax.ShapeDtypeStruct((rh, cols), p.dtype),
        compiler_params=_params(("parallel",)),
    )(chip_idx, p, recv)


def _pair_gather(fs):
    n = len(fs)

    def body(*refs):
        f_refs, out_refs, (send_sems, recv_sems) = refs[:n], refs[n:2 * n], refs[2 * n:]
        x, y, c, _ = _place()
        cps = [_remote(f_refs[a], out_refs[a], send_sems.at[a], recv_sems.at[a], (x, y, 1 - c)) for a in range(n)]
        for cp in cps:
            cp.start()
        for cp in cps:
            cp.wait()

    return pl.pallas_call(
        body, name="rs_pair_gather",
        out_shape=[jax.ShapeDtypeStruct(f.shape, f.dtype) for f in fs],
        in_specs=[HBM_SPEC] * n, out_specs=[HBM_SPEC] * n,
        scratch_shapes=[pltpu.SemaphoreType.DMA((n,)), pltpu.SemaphoreType.DMA((n,))],
    )(*fs)


def _allreduce_small(v):
    r, cols = v.shape

    def body(x_ref, out_ref, slots, send_sems, recv_sems):
        x, y, c, _ = _place()
        bits = [(bx, by, bc) for bx in (0, 1) for by in (0, 1) for bc in (0, 1)]

        def flip(b):
            return (1 - x if b[0] else x, 1 - y if b[1] else y, 1 - c if b[2] else c)

        slots[0] = x_ref[...]
        cps = [_remote(x_ref, slots.at[k], send_sems.at[k - 1], recv_sems.at[k - 1], flip(bits[k]))
               for k in range(1, 8)]
        for cp in cps:
            cp.start()
        for cp in cps:
            cp.wait()
        acc = None
        for b in bits:
            fx, fy, fc = flip(b)
            term = slots[4 * fx + 2 * fy + fc]
            acc = term if acc is None else acc + term
        out_ref[...] = acc

    return pl.pallas_call(
        body, name="allreduce_small",
        out_shape=jax.ShapeDtypeStruct((r, cols), v.dtype),
        in_specs=[VMEM_SPEC], out_specs=VMEM_SPEC,
        scratch_shapes=[pltpu.VMEM((8, r, cols), v.dtype), pltpu.SemaphoreType.DMA((7,)),
                        pltpu.SemaphoreType.DMA((7,))],
        compiler_params=pltpu.CompilerParams(vmem_limit_bytes=VMEM_LIMIT),
    )(v)


BIG = (("a_w_in", 2), ("a_w_out", 1), ("w_kv", 1), ("b_w_q", 1), ("b_w_out", 1), ("f_w_up", 2), ("f_w_down", 1))
SMALL = (("a_norm", 1), ("a_conv", 2), ("a_A_log", None), ("a_dt_bias", None), ("a_out_norm", None),
         ("kv_norm", None), ("b_norm", None), ("b_rel_bias", None), ("f_norm", None), ("f_conv", 2),
         ("f_conv_b", None), ("final_norm", None))
WEIGHT_ORDER = ("a_norm", "a_w_in", "a_conv", "a_A_log", "a_dt_bias", "a_out_norm", "a_w_out", "kv_norm", "w_kv",
                "b_norm", "b_w_q", "b_rel_bias", "b_w_out", "f_norm", "f_w_up", "f_conv", "f_conv_b", "f_w_down",
                "final_norm")


def _pad_rows(flat, cols, quantum):
    n = flat.shape[-1]
    rows = -(-n // (cols * quantum)) * quantum
    pad = [(0, 0)] * (flat.ndim - 1) + [(0, rows * cols - n)]
    return jnp.pad(flat, pad).reshape(flat.shape[:-1] + (rows, cols))


GROUPS = ("GU", "GD", "GK", "GI")


def _group_shards(w, dtype):
    def two(a):
        return a.reshape(-1, a.shape[-1])

    return dict(GU=two(w["f_w_up"]).astype(dtype),
                GD=jnp.concatenate([two(w[n]) for n in ("f_w_down", "a_w_out", "b_w_q", "b_w_out")]).astype(dtype),
                GK=w["w_kv"].astype(dtype),
                GI=two(w["a_w_in"]).astype(dtype))


def _ungroup(red, shard_shapes):
    out = dict(f_w_up=red["GU"].reshape(shard_shapes["f_w_up"]), w_kv=red["GK"],
               a_w_in=red["GI"].reshape(shard_shapes["a_w_in"]))
    off = 0
    for n in ("f_w_down", "a_w_out", "b_w_q", "b_w_out"):
        shp = shard_shapes[n]
        rows = math.prod(shp[:-1])
        out[n] = red["GD"][off:off + rows].reshape(shp)
        off += rows
    return out


def _dense_a_in(gi):
    out = []
    for i in range(N_A):
        full = jnp.transpose(gi[:, i * D_MODEL:(i + 1) * D_MODEL], (1, 0, 2)).reshape(D_MODEL, -1)
        out.append((full[:, :A_CONV_WIDTH], full[:, A_CONV_WIDTH:A_CONV_WIDTH + A_QK],
                    jnp.pad(full[:, A_CONV_WIDTH + A_QK:], ((0, 0), (0, LANE - 2 * A_HEADS)))))
    return out


def _pack_small(values, names):
    return _pad_rows(jnp.concatenate([values[n].reshape(-1) for n in names]), LANE, SUB)


def _unpack_small(packed, shapes, names):
    flat = packed.reshape(-1)
    out, off = {}, 0
    for n in names:
        size = math.prod(shapes[n])
        out[n] = flat[off:off + size].reshape(shapes[n])
        off += size
    return out


def _adamw_nd(w, g, m, v):
    shp = w.shape
    two = (math.prod(shp[:-1]), shp[-1])
    d, mn, vn = _adamw(w.reshape(two), g.reshape(two), m.reshape(two), v.reshape(two))
    return d.reshape(shp), mn.reshape(shp), vn.reshape(shp)


def kernel(x, a_norm, a_w_in, a_conv, a_A_log, a_dt_bias, a_out_norm, a_w_out, kv_norm, w_kv, b_norm, b_w_q, b_rel_bias, b_w_out, f_norm, f_w_up, f_conv, f_conv_b, f_w_down, final_norm, loss_target, m_a_norm, m_a_w_in, m_a_conv, m_a_A_log, m_a_dt_bias, m_a_out_norm, m_a_w_out, m_kv_norm, m_w_kv, m_b_norm, m_b_w_q, m_b_rel_bias, m_b_w_out, m_f_norm, m_f_w_up, m_f_conv, m_f_conv_b, m_f_w_down, m_final_norm, v_a_norm, v_a_w_in, v_a_conv, v_a_A_log, v_a_dt_bias, v_a_out_norm, v_a_w_out, v_kv_norm, v_w_kv, v_b_norm, v_b_w_q, v_b_rel_bias, v_b_w_out, v_f_norm, v_f_w_up, v_f_conv, v_f_conv_b, v_f_w_down, v_final_norm):
    w = dict(a_norm=a_norm, a_w_in=a_w_in, a_conv=a_conv, a_A_log=a_A_log, a_dt_bias=a_dt_bias,
             a_out_norm=a_out_norm, a_w_out=a_w_out, kv_norm=kv_norm, w_kv=w_kv, b_norm=b_norm, b_w_q=b_w_q,
             b_rel_bias=b_rel_bias, b_w_out=b_w_out, f_norm=f_norm, f_w_up=f_w_up, f_conv=f_conv,
             f_conv_b=f_conv_b, f_w_down=f_w_down, final_norm=final_norm)
    m = dict(a_norm=m_a_norm, a_w_in=m_a_w_in, a_conv=m_a_conv, a_A_log=m_a_A_log, a_dt_bias=m_a_dt_bias,
             a_out_norm=m_a_out_norm, a_w_out=m_a_w_out, kv_norm=m_kv_norm, w_kv=m_w_kv, b_norm=m_b_norm,
             b_w_q=m_b_w_q, b_rel_bias=m_b_rel_bias, b_w_out=m_b_w_out, f_norm=m_f_norm, f_w_up=m_f_w_up,
             f_conv=m_f_conv, f_conv_b=m_f_conv_b, f_w_down=m_f_w_down, final_norm=m_final_norm)
    v = dict(a_norm=v_a_norm, a_w_in=v_a_w_in, a_conv=v_a_conv, a_A_log=v_a_A_log, a_dt_bias=v_a_dt_bias,
             a_out_norm=v_a_out_norm, a_w_out=v_a_w_out, kv_norm=v_kv_norm, w_kv=v_w_kv, b_norm=v_b_norm,
             b_w_q=v_b_w_q, b_rel_bias=v_b_rel_bias, b_w_out=v_b_w_out, f_norm=v_f_norm, f_w_up=v_f_w_up,
             f_conv=v_f_conv, f_conv_b=v_f_conv_b, f_w_down=v_f_w_down, final_norm=v_final_norm)
    xi, yi, ci = lax.axis_index("x"), lax.axis_index("y"), lax.axis_index("c")
    chip = 2 * xi + yi
    shard_shapes = {n: w[n].shape for n in WEIGHT_ORDER}

    mine_w = _group_shards(w, BF16)
    full = dict(zip(GROUPS, _allgather_weights([mine_w[n] for n in GROUPS])))
    full["a_in"] = _dense_a_in(full.pop("GI"))
    sharded_small = [n for n, axis in SMALL if axis is not None]
    placed = {}
    for n, axis in SMALL:
        if axis is not None:
            wide = list(w[n].shape)
            wide[axis] *= 4
            mine_once = w[n] * (1 - ci).astype(F32)
            placed[n] = lax.dynamic_update_slice_in_dim(jnp.zeros(wide, F32), mine_once, chip * w[n].shape[axis], axis)
    placed_shapes = {n: placed[n].shape for n in sharded_small}
    full.update(_unpack_small(_allreduce_small(_pack_small(placed, sharded_small)), placed_shapes, sharded_small))
    for n, axis in SMALL:
        if axis is None:
            full[n] = w[n]

    loss_part, grad_x, grads = _local_step(x[0], loss_target[0], full)

    packed = [grads[n] for n in GROUPS]
    c_idx = jnp.reshape(ci, (1,)).astype(jnp.int32)
    chip_idx = jnp.reshape(chip, (1,)).astype(jnp.int32)
    pairs = [_pair_add(g, o, c_idx) for g, o in zip(packed, _pair_exchange(packed))]
    arrived = _chip_exchange([pb for _, pb in pairs])
    mine = [_chip_add(p, r, chip_idx) for (p, _), r in zip(pairs, arrived)]
    theirs = _pair_gather(mine)
    red = _ungroup({n: jnp.concatenate([jnp.where(ci == 0, a, b), jnp.where(ci == 0, b, a)], axis=0)
                    for n, a, b in zip(GROUPS, mine, theirs)}, shard_shapes)

    small_names = [n for n, _ in SMALL]
    small_vals = {n: grads[n] for n in small_names}
    small_vals["loss"] = loss_part[0, :1]
    names = ["loss"] + small_names
    shapes = {n: small_vals[n].shape for n in names}
    summed = _unpack_small(_allreduce_small(_pack_small(small_vals, names)), shapes, names)
    loss = summed["loss"][0]
    for n, axis in SMALL:
        g = summed[n]
        if axis is not None:
            g = lax.dynamic_slice_in_dim(g, chip * w[n].shape[axis], w[n].shape[axis], axis)
        red[n] = g

    delta, new_m, new_v = {}, {}, {}
    for n, _ in BIG:
        delta[n], new_m[n], new_v[n] = _adamw_nd(w[n], red[n], m[n], v[n])
    local_shapes = {n: w[n].shape for n in small_names}
    packs = [_pack_small(t, small_names) for t in (w, red, m, v)]
    outs = _adamw(*packs)
    ds, ms, vs = (_unpack_small(o, local_shapes, small_names) for o in outs)
    delta.update(ds)
    new_m.update(ms)
    new_v.update(vs)

    return (loss, grad_x[None], *[red[n] for n in WEIGHT_ORDER], *[delta[n] for n in WEIGHT_ORDER],
            *[new_m[n] for n in WEIGHT_ORDER], *[new_v[n] for n in WEIGHT_ORDER])
```

```python
import math

import jax
import jax.numpy as jnp
from jax import lax
from jax.experimental import pallas as pl
from jax.experimental.pallas import tpu as pltpu

F32 = jnp.float32
BF16 = jnp.bfloat16
HIGHEST = lax.Precision.HIGHEST
MESH = pl.DeviceIdType.MESH

D_MODEL = 1024
CHUNK = 64
A_HEADS = 8
A_HEAD = 128
A_QK = A_HEADS * A_HEAD
A_CONV_WIDTH = 3 * A_QK
B_HEADS = 16
B_HEAD = 64
LEFT = 8 * CHUNK
QBLK = 4 * CHUNK
KBLK = LEFT + QBLK
REL_CLIP = 256
REL_PAD = 640
FFN_DIM = 2816
EPS = 1e-6
NEG_INF = -1e30
LANE = 128
SUB = 8
VMEM_LIMIT = 56 * 1024 * 1024

ADAM_LR = 0.001
ADAM_B1 = 0.9
ADAM_B2 = 0.999
ADAM_EPS = 1e-08
ADAM_WD = 0.01
ADAM_STEP = 10


VMEM_FULL = pl.BlockSpec(memory_space=pltpu.VMEM)


def _params(sem=None):
    return pltpu.CompilerParams(dimension_semantics=sem, vmem_limit_bytes=VMEM_LIMIT)


def _tile(n, cap):
    if n <= cap:
        return n
    best = None
    for t in range(LANE, cap + 1, LANE):
        if n % t == 0:
            best = t
    assert best is not None, n
    return best


def _sigmoid(x):
    return 1.0 / (1.0 + jnp.exp(-x))


def _softplus(x):
    return jnp.maximum(x, 0.0) + jnp.log(1.0 + jnp.exp(-jnp.abs(x)))


def _dot(a, b, dims, prec=None):
    return lax.dot_general(a, b, (dims, ((), ())), preferred_element_type=F32, precision=prec)


NN = ((1,), (0,))
NT = ((1,), (1,))
TN = ((0,), (0,))


def _bdot(a, b, dims):
    return _dot(a.astype(BF16), b.astype(BF16), dims)


def _mm(a, b, mode="nn", out_dtype=F32, res=None, name="mm", norm=None):
    if mode == "nn":
        (m, k), (k2, n) = a.shape, b.shape
    elif mode == "nt":
        (m, k), (n, k2) = a.shape, b.shape
    else:
        (k, m), (k2, n) = a.shape, b.shape
    assert k == k2, (a.shape, b.shape, mode)
    tm, tn, tk = _tile(m, 1408), _tile(n, 1408), _tile(k, 1408)
    if m == 8192:
        tm = 1024
    if k == 8192:
        tk = 1024
    nk = k // tk
    dims = {"nn": NN, "nt": NT, "tn": TN}[mode]
    a_spec = {"nn": pl.BlockSpec((tm, tk), lambda i, j, kk: (i, kk)),
              "nt": pl.BlockSpec((tm, tk), lambda i, j, kk: (i, kk)),
              "tn": pl.BlockSpec((tk, tm), lambda i, j, kk: (kk, i))}[mode]
    b_spec = {"nn": pl.BlockSpec((tk, tn), lambda i, j, kk: (kk, j)),
              "nt": pl.BlockSpec((tn, tk), lambda i, j, kk: (j, kk)),
              "tn": pl.BlockSpec((tk, tn), lambda i, j, kk: (kk, j))}[mode]
    o_spec = pl.BlockSpec((tm, tn), lambda i, j, kk: (i, j))
    return _mm_call(name, a, b, dims, (m // tm, n // tn, nk), a_spec, b_spec, o_spec, (m, n), out_dtype, (tm, tn),
                    res, norm)


ROW_TILE = 1024
N_SHARDS = 4


def _mm_call(name, a, b, dims, grid, a_spec, b_spec, o_spec, out_shape, out_dtype, acc_shape, res=None, norm=None):
    nk = grid[2]
    has_res = res is not None
    has_norm = norm is not None
    if has_norm:
        assert grid[1] == 1 and len(out_shape) == 2 and out_dtype == F32

    def flat(v):
        return v.reshape(-1, v.shape[-1]) if v.ndim == 3 else v

    def body(a_ref, b_ref, *rest):
        rest = list(rest)
        res_ref = rest.pop(0) if has_res else None
        x_ref, g_ref, dres_ref = (rest.pop(0), rest.pop(0), rest.pop(0)) if has_norm else (None, None, None)
        o_ref = rest.pop(0)
        dg_ref = rest.pop(0) if has_norm else None
        acc = rest.pop(0)
        kk = pl.program_id(2)
        first_rows = pl.program_id(0) == 0

        @pl.when(kk == 0)
        def _():
            acc[...] = jnp.zeros_like(acc)

        if has_norm:
            @pl.when(first_rows & (kk == 0))
            def _():
                dg_ref[...] = jnp.zeros_like(dg_ref)

        acc[...] += _bdot(flat(a_ref[...]), flat(b_ref[...]), dims)

        @pl.when(kk == nk - 1)
        def _():
            r = acc[...]
            if has_res:
                r = r + res_ref[...]
            if not has_norm:
                o_ref[...] = r.reshape(o_ref.shape).astype(out_dtype)
                return
            xv = x_ref[...]
            rs = lax.rsqrt(jnp.mean(xv * xv, axis=-1, keepdims=True) + EPS)
            t = r * g_ref[...]
            c = jnp.mean(t * xv, axis=-1, keepdims=True)
            o_ref[...] = dres_ref[...] + rs * t - xv * (rs * rs * rs) * c
            dg_ref[...] += jnp.sum(r * xv * rs, axis=0, keepdims=True)

    args = [a, b] + ([res] if has_res else [])
    in_specs = [a_spec, b_spec] + ([o_spec] if has_res else [])
    out_specs, out_shapes = o_spec, jax.ShapeDtypeStruct(out_shape, out_dtype)
    if has_norm:
        vec = pl.BlockSpec((1, out_shape[1]), lambda i, j, kk: (0, 0))
        args += list(norm)
        in_specs += [o_spec, vec, o_spec]
        out_specs = [o_spec, vec]
        out_shapes = [out_shapes, jax.ShapeDtypeStruct((1, out_shape[1]), F32)]
    return pl.pallas_call(
        body, name=name, grid=grid, in_specs=in_specs, out_specs=out_specs, out_shape=out_shapes,
        scratch_shapes=[pltpu.VMEM(acc_shape, F32)],
        compiler_params=_params(("arbitrary" if has_norm else "parallel", "parallel", "arbitrary")),
    )(*args)


def _shards_per_block(rows):
    return N_SHARDS if N_SHARDS * rows <= 1408 else 2


def _mm_rowsh(a, buf, rows, blk0, mode, name, res=None, out_dtype=F32, norm=None):
    s = a.shape[0]
    cols = buf.shape[2]
    g = _shards_per_block(rows)
    tm = _tile(s, ROW_TILE)
    b_blk = (g, rows, cols)
    if mode == "nn":
        return _mm_call(name, a, buf, NN, (s // tm, 1, N_SHARDS // g),
                        pl.BlockSpec((tm, g * rows), lambda i, j, kk: (i, kk)),
                        pl.BlockSpec(b_blk, lambda i, j, kk: (kk, blk0, 0)),
                        pl.BlockSpec((tm, cols), lambda i, j, kk: (i, 0)),
                        (s, cols), out_dtype, (tm, cols), res)
    return _mm_call(name, a, buf, NT, (s // tm, N_SHARDS // g, 1),
                    pl.BlockSpec((tm, cols), lambda i, j, kk: (i, 0)),
                    pl.BlockSpec(b_blk, lambda i, j, kk: (j, blk0, 0)),
                    pl.BlockSpec((tm, g * rows), lambda i, j, kk: (i, j)),
                    (s, N_SHARDS * rows), out_dtype, (tm, g * rows), res, norm)


def _mm_rowsh_dw(act, dy, rows, name):
    s = act.shape[0]
    cols = dy.shape[1]
    g = _shards_per_block(rows)
    ts = _tile(s, ROW_TILE)
    return _mm_call(name, act, dy, TN, (1, N_SHARDS // g, s // ts),
                    pl.BlockSpec((ts, g * rows), lambda i, j, kk: (kk, j)),
                    pl.BlockSpec((ts, cols), lambda i, j, kk: (kk, 0)),
                    pl.BlockSpec((g, rows, cols), lambda i, j, kk: (j, 0, 0)),
                    (N_SHARDS, rows, cols), F32, (g * rows, cols))


def _mm_colsh(a, buf, krows, blk0, mode, name, flat=False, res=None, out_dtype=F32, norm=None):
    cols = buf.shape[2]
    b_nn = pl.BlockSpec((None, krows, cols), lambda i, j, kk: (j, blk0, 0))
    b_nt = pl.BlockSpec((None, krows, cols), lambda i, j, kk: (kk, blk0, 0))
    if mode == "nn":
        s = a.shape[0]
        tm = _tile(s, ROW_TILE)
        o_spec = (pl.BlockSpec((tm, cols), lambda i, j, kk: (i, j)) if flat
                  else pl.BlockSpec((None, tm, cols), lambda i, j, kk: (j, i, 0)))
        return _mm_call(name, a, buf, NN, (s // tm, N_SHARDS, 1),
                        pl.BlockSpec((tm, krows), lambda i, j, kk: (i, 0)), b_nn, o_spec,
                        (s, N_SHARDS * cols) if flat else (N_SHARDS, s, cols), out_dtype, (tm, cols), res)
    s = a.shape[0] if flat else a.shape[1]
    tm = _tile(s, ROW_TILE)
    a_spec = (pl.BlockSpec((tm, cols), lambda i, j, kk: (i, kk)) if flat
              else pl.BlockSpec((None, tm, cols), lambda i, j, kk: (kk, i, 0)))
    return _mm_call(name, a, buf, NT, (s // tm, 1, N_SHARDS), a_spec, b_nt,
                    pl.BlockSpec((tm, krows), lambda i, j, kk: (i, 0)),
                    (s, krows), out_dtype, (tm, krows), res, norm)


def _mm_colsh_dw(x, dy, name, flat=False):
    s, k = x.shape
    cols = dy.shape[1] // N_SHARDS if flat else dy.shape[2]
    ts = _tile(s, ROW_TILE)
    b_spec = (pl.BlockSpec((ts, cols), lambda i, j, kk: (kk, j)) if flat
              else pl.BlockSpec((None, ts, cols), lambda i, j, kk: (j, kk, 0)))
    return _mm_call(name, x, dy, TN, (1, N_SHARDS, s // ts),
                    pl.BlockSpec((ts, k), lambda i, j, kk: (kk, 0)), b_spec,
                    pl.BlockSpec((None, k, cols), lambda i, j, kk: (j, 0, 0)),
                    (N_SHARDS, k, cols), F32, (k, cols))


def _rmsnorm_fwd(x, g):
    s, d = x.shape
    tr = _tile(s, 1024)

    def body(x_ref, g_ref, o_ref):
        xv = x_ref[...]
        r = lax.rsqrt(jnp.mean(xv * xv, axis=-1, keepdims=True) + EPS)
        o_ref[...] = (xv * r * g_ref[...]).astype(BF16)

    return pl.pallas_call(
        body, name="rmsnorm_fwd", grid=(s // tr,),
        in_specs=[pl.BlockSpec((tr, d), lambda i: (i, 0)), pl.BlockSpec((1, d), lambda i: (0, 0))],
        out_specs=pl.BlockSpec((tr, d), lambda i: (i, 0)),
        out_shape=jax.ShapeDtypeStruct((s, d), BF16),
        compiler_params=_params(("parallel",)),
    )(x, g)


def _final_loss(h, g, tgt):
    s, d = h.shape
    tr = _tile(s, 1024)

    def body(x_ref, g_ref, t_ref, loss_ref, dx_ref, dg_ref):
        @pl.when(pl.program_id(0) == 0)
        def _():
            dg_ref[...] = jnp.zeros_like(dg_ref)
            loss_ref[...] = jnp.zeros_like(loss_ref)

        xv = x_ref[...]
        r = lax.rsqrt(jnp.mean(xv * xv, axis=-1, keepdims=True) + EPS)
        xh = xv * r
        err = xh * g_ref[...] - t_ref[...]
        per_row = jnp.mean(err * err, axis=-1, keepdims=True)
        loss_ref[...] += 0.5 * jnp.sum(per_row, axis=0, keepdims=True)
        dy = err * (1.0 / d)
        t = dy * g_ref[...]
        c = jnp.mean(t * xv, axis=-1, keepdims=True)
        dx_ref[...] = r * t - xv * (r * r * r) * c
        dg_ref[...] += jnp.sum(dy * xh, axis=0, keepdims=True)

    row = pl.BlockSpec((tr, d), lambda i: (i, 0))
    vec = pl.BlockSpec((1, d), lambda i: (0, 0))
    return pl.pallas_call(
        body, name="final_loss", grid=(s // tr,),
        in_specs=[row, vec, row],
        out_specs=[pl.BlockSpec((1, LANE), lambda i: (0, 0)), row, vec],
        out_shape=[jax.ShapeDtypeStruct((1, LANE), F32), jax.ShapeDtypeStruct((s, d), F32),
                   jax.ShapeDtypeStruct((1, d), F32)],
        compiler_params=_params(("arbitrary",)),
    )(h, g, tgt)


CONV_ROWS = 512
CONV_COLS = 1408
CONV_CHUNK = 64


def _per_lane_tile(tile_body):
    def body(*refs):
        for lt in range(refs[0].shape[-1] // LANE):
            cols = slice(lt * LANE, (lt + 1) * LANE)
            tile_body(*[r.at[(slice(None),) * (len(r.shape) - 1) + (cols,)] for r in refs])
    return body


def _lagged(window, lag):
    return (pltpu.roll(window, lag, 0) if lag else window)[SUB:]


def _led(window, lead):
    n = window.shape[0] - SUB
    return (pltpu.roll(window, window.shape[0] - lead, 0) if lead else window)[:n]


def _fold_rows(v):
    return jnp.sum(v.reshape(v.shape[0] // SUB, SUB, v.shape[1]), axis=0)


def _taps(shifted, w):
    acc = None
    for k, xs in enumerate(shifted):
        term = xs * w[k:k + 1, :]
        acc = term if acc is None else acc + term
    return acc


def _conv_tiles(s, c):
    return _tile(s, CONV_ROWS), _tile(c, CONV_COLS)


def _conv_silu_fwd(pre, w):
    s, c = pre.shape
    width = w.shape[0]
    tr, tc = _conv_tiles(s, c)

    def body(x_ref, w_ref, o_ref, tail):
        @pl.when(pl.program_id(1) == 0)
        def _():
            tail[...] = jnp.zeros_like(tail)

        wv = w_ref[...]

        def do(c0, window):
            y = _taps([_lagged(window, width - 1 - k) for k in range(width)], wv)
            o_ref[pl.ds(c0, CONV_CHUNK), :] = y * _sigmoid(y)

        def chunk(ci, carry):
            c0 = pl.multiple_of(ci * CONV_CHUNK, CONV_CHUNK)
            do(c0, x_ref[pl.ds(pl.multiple_of(c0 - SUB, SUB), CONV_CHUNK + SUB), :])
            return carry

        do(0, jnp.concatenate([tail[...], x_ref[:CONV_CHUNK, :]], axis=0))
        lax.fori_loop(1, tr // CONV_CHUNK, chunk, 0)
        tail[...] = x_ref[tr - SUB:, :]

    blk = pl.BlockSpec((tr, tc), lambda j, i: (i, j))
    return pl.pallas_call(
        _per_lane_tile(body), name="conv_silu_fwd", grid=(c // tc, s // tr),
        in_specs=[blk, pl.BlockSpec((width, tc), lambda j, i: (0, j))], out_specs=blk,
        out_shape=jax.ShapeDtypeStruct((s, c), F32),
        scratch_shapes=[pltpu.VMEM((SUB, tc), F32)],
        compiler_params=_params(("parallel", "arbitrary")),
    )(pre, w)


def _prev_rows_index(i_blk, tr):
    return jnp.maximum(i_blk * (tr // SUB) - 1, 0)


def _conv_silu_bwd(pre, w, dact):
    s, c = pre.shape
    width = w.shape[0]
    tr, tc = _conv_tiles(s, c)
    nr = s // tr

    nchunks = tr // CONV_CHUNK

    def body(x_ref, p_ref, w_ref, d_ref, dx_ref, dw_ref, head):
        @pl.when(pl.program_id(1) == 0)
        def _():
            head[...] = jnp.zeros_like(head)
            dw_ref[...] = jnp.zeros_like(dw_ref)

        wv = w_ref[...]

        def do(c0, window, later, dws):
            xs = [_lagged(window, width - 1 - k) for k in range(width)]
            y = _taps(xs, wv)
            sg = _sigmoid(y)
            dy = d_ref[pl.ds(c0, CONV_CHUNK), :] * sg * (1.0 + y * (1.0 - sg))
            dws = tuple(dw + _fold_rows(dy * x) for dw, x in zip(dws, xs))
            ahead = jnp.concatenate([dy, later], axis=0)
            dx_ref[pl.ds(c0, CONV_CHUNK), :] = _taps([_led(ahead, width - 1 - k) for k in range(width)],
                                                     wv).astype(BF16)
            return dy[:SUB], dws

        def chunk(it, carry):
            c0 = pl.multiple_of((nchunks - 1 - it) * CONV_CHUNK, CONV_CHUNK)
            return do(c0, x_ref[pl.ds(pl.multiple_of(c0 - SUB, SUB), CONV_CHUNK + SUB), :], *carry)

        zero = jnp.zeros((SUB, LANE), F32)
        carry = lax.fori_loop(0, nchunks - 1, chunk, (head[...], (zero,) * width))
        before = jnp.where(pl.program_id(1) == nr - 1, 0.0, p_ref[...])
        later, dws = do(0, jnp.concatenate([before, x_ref[:CONV_CHUNK, :]], axis=0), *carry)
        head[...] = later
        for k in range(width):
            dw_ref[k:k + 1, :] += jnp.sum(dws[k], axis=0, keepdims=True)

    blk = pl.BlockSpec((tr, tc), lambda j, i: (nr - 1 - i, j))
    prev = pl.BlockSpec((SUB, tc), lambda j, i: (_prev_rows_index(nr - 1 - i, tr), j))
    wblk = pl.BlockSpec((width, tc), lambda j, i: (0, j))
    return pl.pallas_call(
        _per_lane_tile(body), name="conv_silu_bwd", grid=(c // tc, nr),
        in_specs=[blk, prev, wblk, blk], out_specs=[blk, wblk],
        out_shape=[jax.ShapeDtypeStruct((s, c), BF16), jax.ShapeDtypeStruct((width, c), F32)],
        scratch_shapes=[pltpu.VMEM((SUB, tc), F32)],
        compiler_params=_params(("parallel", "arbitrary")),
    )(pre, pre, w, dact)


def _ffn_act_fwd(pre, w, b):
    _, halves, s, c = pre.shape
    width = w.shape[2]
    tr, tc = _conv_tiles(s, c)
    ncb = c // tc

    def body(x_ref, w_ref, b_ref, o_ref, tail):
        @pl.when(pl.program_id(2) == 0)
        def _():
            tail[...] = jnp.zeros_like(tail)

        wg, wv, bg, bv = w_ref[0], w_ref[1], b_ref[0], b_ref[1]

        def do(c0, win_g, win_v):
            yg = _taps([_lagged(win_g, width - 1 - k) for k in range(width)], wg) + bg
            yv = _taps([_lagged(win_v, width - 1 - k) for k in range(width)], wv) + bv
            o_ref[pl.ds(c0, CONV_CHUNK), :] = (yg * _sigmoid(yg) * yv).astype(BF16)

        def chunk(ci, carry):
            c0 = pl.multiple_of(ci * CONV_CHUNK, CONV_CHUNK)
            rows = pl.ds(pl.multiple_of(c0 - SUB, SUB), CONV_CHUNK + SUB)
            do(c0, x_ref[0, rows, :], x_ref[1, rows, :])
            return carry

        do(0, jnp.concatenate([tail[0], x_ref[0, :CONV_CHUNK, :]], axis=0),
           jnp.concatenate([tail[1], x_ref[1, :CONV_CHUNK, :]], axis=0))
        lax.fori_loop(1, tr // CONV_CHUNK, chunk, 0)
        tail[...] = x_ref[:, tr - SUB:, :]

    return pl.pallas_call(
        _per_lane_tile(body), name="ffn_act_fwd", grid=(halves, ncb, s // tr),
        in_specs=[pl.BlockSpec((2, None, tr, tc), lambda h, j, i: (0, h, i, j)),
                  pl.BlockSpec((2, None, width, tc), lambda h, j, i: (0, h, 0, j)),
                  pl.BlockSpec((2, None, 1, tc), lambda h, j, i: (0, h, 0, j))],
        out_specs=pl.BlockSpec((tr, tc), lambda h, j, i: (i, h * ncb + j)),
        out_shape=jax.ShapeDtypeStruct((s, halves * c), BF16),
        scratch_shapes=[pltpu.VMEM((2, SUB, tc), F32)],
        compiler_params=_params(("parallel", "parallel", "arbitrary")),
    )(pre, w, b)


def _ffn_act_bwd(pre, w, b, dact):
    _, halves, s, c = pre.shape
    width = w.shape[2]
    tr, tc = _conv_tiles(s, c)
    ncb = c // tc
    nr = s // tr
    nchunks = tr // CONV_CHUNK

    def body(x_ref, p_ref, w_ref, b_ref, d_ref, dx_ref, dw_ref, db_ref, head):
        @pl.when(pl.program_id(2) == 0)
        def _():
            for r in (head, dw_ref, db_ref):
                r[...] = jnp.zeros_like(r)

        wg, wv, bg, bv = w_ref[0], w_ref[1], b_ref[0], b_ref[1]

        def do(c0, win_g, win_v, later_g, later_v, dwg, dwv, dbg, dbv):
            xg = [_lagged(win_g, width - 1 - k) for k in range(width)]
            xv = [_lagged(win_v, width - 1 - k) for k in range(width)]
            yg = _taps(xg, wg) + bg
            yv = _taps(xv, wv) + bv
            sg = _sigmoid(yg)
            da = d_ref[pl.ds(c0, CONV_CHUNK), :]
            dyv = da * yg * sg
            dyg = da * yv * sg * (1.0 + yg * (1.0 - sg))
            dwg = tuple(dw + _fold_rows(dyg * x) for dw, x in zip(dwg, xg))
            dwv = tuple(dw + _fold_rows(dyv * x) for dw, x in zip(dwv, xv))
            dbg = dbg + _fold_rows(dyg)
            dbv = dbv + _fold_rows(dyv)
            ahead_g = jnp.concatenate([dyg, later_g], axis=0)
            ahead_v = jnp.concatenate([dyv, later_v], axis=0)
            dxg = _taps([_led(ahead_g, width - 1 - k) for k in range(width)], wg)
            dxv = _taps([_led(ahead_v, width - 1 - k) for k in range(width)], wv)
            dx_ref[0, pl.ds(c0, CONV_CHUNK), :] = dxg.astype(BF16)
            dx_ref[1, pl.ds(c0, CONV_CHUNK), :] = dxv.astype(BF16)
            return dyg[:SUB], dyv[:SUB], dwg, dwv, dbg, dbv

        def chunk(it, carry):
            c0 = pl.multiple_of((nchunks - 1 - it) * CONV_CHUNK, CONV_CHUNK)
            rows = pl.ds(pl.multiple_of(c0 - SUB, SUB), CONV_CHUNK + SUB)
            return do(c0, x_ref[0, rows, :], x_ref[1, rows, :], *carry)

        zero = jnp.zeros((SUB, LANE), F32)
        carry = lax.fori_loop(0, nchunks - 1, chunk,
                              (head[0], head[1], (zero,) * width, (zero,) * width, zero, zero))
        before = jnp.where(pl.program_id(2) == nr - 1, 0.0, p_ref[...])
        later_g, later_v, dwg, dwv, dbg, dbv = do(
            0, jnp.concatenate([before[0], x_ref[0, :CONV_CHUNK, :]], axis=0),
            jnp.concatenate([before[1], x_ref[1, :CONV_CHUNK, :]], axis=0), *carry)
        head[0] = later_g
        head[1] = later_v
        db_ref[0] += jnp.sum(dbg, axis=0, keepdims=True)
        db_ref[1] += jnp.sum(dbv, axis=0, keepdims=True)
        for k in range(width):
            dw_ref[0, k:k + 1, :] += jnp.sum(dwg[k], axis=0, keepdims=True)
            dw_ref[1, k:k + 1, :] += jnp.sum(dwv[k], axis=0, keepdims=True)

    blk = pl.BlockSpec((2, None, tr, tc), lambda h, j, i: (0, h, nr - 1 - i, j))
    prev = pl.BlockSpec((2, None, SUB, tc), lambda h, j, i: (0, h, _prev_rows_index(nr - 1 - i, tr), j))
    wblk = pl.BlockSpec((2, None, width, tc), lambda h, j, i: (0, h, 0, j))
    bblk = pl.BlockSpec((2, None, 1, tc), lambda h, j, i: (0, h, 0, j))
    return pl.pallas_call(
        _per_lane_tile(body), name="ffn_act_bwd", grid=(halves, ncb, nr),
        in_specs=[blk, prev, wblk, bblk, pl.BlockSpec((tr, tc), lambda h, j, i: (nr - 1 - i, h * ncb + j))],
        out_specs=[blk, wblk, bblk],
        out_shape=[jax.ShapeDtypeStruct(pre.shape, BF16), jax.ShapeDtypeStruct(w.shape, F32),
                   jax.ShapeDtypeStruct(b.shape, F32)],
        scratch_shapes=[pltpu.VMEM((2, SUB, tc), F32)],
        compiler_params=_params(("parallel", "parallel", "arbitrary")),
    )(pre, pre, w, b, dact)


def _tri_masks():
    row = lax.broadcasted_iota(jnp.int32, (CHUNK, CHUNK), 0)
    col = lax.broadcasted_iota(jnp.int32, (CHUNK, CHUNK), 1)
    return row, col


def _tri_inv(ms, row, col):
    eye = (row == col).astype(F32)
    same_blk = (row >> 4) == (col >> 4)
    mds = [jnp.where(same_blk, m, 0.0) for m in ms]
    offs = [m - md for m, md in zip(ms, mds)]
    xs = [eye - md for md in mds]
    ps = [_bdot(md, md, NN) for md in mds]
    for _ in range(2):
        rs = [_bdot(jnp.concatenate([x, p], axis=0), p, NN) for x, p in zip(xs, ps)]
        xs = [x + r[:CHUNK] for x, r in zip(xs, rs)]
        ps = [r[CHUNK:] for r in rs]
    xs = [x + _bdot(x, p, NN) for x, p in zip(xs, ps)]
    ps = [_bdot(x, off, NN) for x, off in zip(xs, offs)]
    pps = [_bdot(p, p, NN) for p in ps]
    ys = [eye - p for p in ps]
    ys = [y + _bdot(y, pp, NN) for y, pp in zip(ys, pps)]
    return [_bdot(y, x, NN) for y, x in zip(ys, xs)]


def _gdn_gates(ba, alog, dtb, row, col):
    sig = _sigmoid(ba)
    neg_a = -jnp.exp(alog)
    g = neg_a * _softplus(ba + dtb)
    lower = (row >= col).astype(F32)
    gcum = _dot(lower, g, NN, HIGHEST)
    return sig, neg_a, g, gcum


def _gdn_head_common(q_raw, k_raw, v, beta, gc, gr, row, col):
    causal = row >= col
    strict = row > col
    rq = lax.rsqrt(jnp.sum(q_raw * q_raw, axis=-1, keepdims=True) + EPS)
    rk = lax.rsqrt(jnp.sum(k_raw * k_raw, axis=-1, keepdims=True) + EPS)
    q = q_raw * (rq * (A_HEAD ** -0.5))
    k = k_raw * rk
    decay = jnp.where(causal, jnp.exp(jnp.where(causal, gc - gr, 0.0)), 0.0)
    eg = jnp.exp(gc)
    gl = gc[CHUNK - 1:CHUNK, :]
    ekl = jnp.exp(gl - gc)
    dec = jnp.exp(gl)
    kb = k * beta
    kbq = jnp.concatenate([kb, q], axis=0)
    both = _bdot(kbq, k, NT)
    kk, qk = both[:CHUNK], both[CHUNK:]
    a = jnp.where(causal, qk * decay, 0.0)
    return dict(rq=rq, rk=rk, q=q, k=k, decay=decay, eg=eg, ekl=ekl, dec=dec, kb=kb, kbq=kbq, kk=kk, qk=qk, a=a,
                vb=v * beta, kbg=kb * eg, qd=q * eg, ke=k * ekl, causal=causal, strict=strict)


def _gdn_fwd(qkv, ba, z, alog, dtb, wn):
    s = qkv.shape[0]
    nc = s // CHUNK

    def body(qkv_ref, ba_ref, z_ref, alog_ref, dtb_ref, wn_ref, y_ref, o_ref, st_ref, t_ref, w_ref, vn_ref, state):
        @pl.when(pl.program_id(0) == 0)
        def _():
            state[...] = jnp.zeros_like(state)

        row, col = _tri_masks()
        sig, _, _, gcum = _gdn_gates(ba_ref[...], alog_ref[...], dtb_ref[...], row, col)
        gt = gcum.T
        heads = range(A_HEADS)
        lanes = [slice(h * A_HEAD, (h + 1) * A_HEAD) for h in heads]
        fs = [_gdn_head_common(qkv_ref[:, lanes[h]], qkv_ref[:, A_QK + h * A_HEAD:A_QK + (h + 1) * A_HEAD],
                               qkv_ref[:, 2 * A_QK + h * A_HEAD:2 * A_QK + (h + 1) * A_HEAD],
                               sig[:, h:h + 1], gcum[:, 8 + h:9 + h], gt[8 + h:9 + h, :], row, col) for h in heads]
        ts = [t.astype(BF16) for t in
              _tri_inv([jnp.where(f["strict"], f["kk"] * f["decay"], 0.0) for f in fs], row, col)]
        uws = [_bdot(t, jnp.concatenate([f["vb"], f["kbg"]], axis=1), NN) for t, f in zip(ts, fs)]
        s0s = [state[h] for h in heads]
        ws_ = [uw[:, A_HEAD:].astype(BF16) for uw in uws]
        wss = [_bdot(jnp.concatenate([w, f["qd"].astype(BF16)], axis=0), s0, NN) for w, f, s0 in zip(ws_, fs, s0s)]
        vnews = [(uw[:, :A_HEAD] - wsq[:CHUNK]).astype(BF16) for uw, wsq in zip(uws, wss)]
        os_ = [wsq[CHUNK:] + _bdot(f["a"], vn, NN) for wsq, f, vn in zip(wss, fs, vnews)]
        s1s = [s0 * f["dec"] + _bdot(f["ke"], vn, TN) for s0, f, vn in zip(s0s, fs, vnews)]
        for h in heads:
            ln = lanes[h]
            st_ref[0, h] = s0s[h]
            t_ref[0, h] = ts[h]
            w_ref[:, ln] = ws_[h]
            vn_ref[:, ln] = vnews[h]
            state[h] = s1s[h]
            o = os_[h]
            o_ref[:, ln] = o
            r = lax.rsqrt(jnp.mean(o * o, axis=-1, keepdims=True) + EPS)
            zz = z_ref[:, ln]
            y_ref[:, ln] = (o * r * wn_ref[...] * zz * _sigmoid(zz)).astype(BF16)

    vec = pl.BlockSpec((1, LANE), lambda n: (0, 0))
    wide = pl.BlockSpec((CHUNK, A_QK), lambda n: (n, 0))
    return pl.pallas_call(
        body, name="gdn_fwd", grid=(nc,),
        in_specs=[pl.BlockSpec((CHUNK, A_CONV_WIDTH), lambda n: (n, 0)),
                  pl.BlockSpec((CHUNK, LANE), lambda n: (n, 0)), wide, vec, vec, vec],
        out_specs=[wide, wide, pl.BlockSpec((1, A_HEADS, A_HEAD, A_HEAD), lambda n: (n, 0, 0, 0)),
                   pl.BlockSpec((1, A_HEADS, CHUNK, CHUNK), lambda n: (n, 0, 0, 0)), wide, wide],
        out_shape=[jax.ShapeDtypeStruct((s, A_QK), BF16), jax.ShapeDtypeStruct((s, A_QK), F32),
                   jax.ShapeDtypeStruct((nc, A_HEADS, A_HEAD, A_HEAD), F32),
                   jax.ShapeDtypeStruct((nc, A_HEADS, CHUNK, CHUNK), BF16),
                   jax.ShapeDtypeStruct((s, A_QK), BF16), jax.ShapeDtypeStruct((s, A_QK), BF16)],
        scratch_shapes=[pltpu.VMEM((A_HEADS, A_HEAD, A_HEAD), F32)],
        compiler_params=_params(("arbitrary",)),
    )(qkv, ba, z, alog, dtb, wn)


def _gdn_bwd(qkv, ba, z, o_raw, dy, states, t_all, w_all, vn_all, alog, dtb, wn):
    s = qkv.shape[0]
    nc = s // CHUNK

    def body(qkv_ref, ba_ref, z_ref, o_ref, dy_ref, st_ref, t_ref, w_ref, vn_ref, alog_ref, dtb_ref, wn_ref,
             dqkv_ref, dba_ref, dz_ref, dalog_ref, ddtb_ref, dwn_ref, dstate):
        @pl.when(pl.program_id(0) == 0)
        def _():
            for r in (dstate, dalog_ref, ddtb_ref, dwn_ref):
                r[...] = jnp.zeros_like(r)

        row, col = _tri_masks()
        bat = ba_ref[...]
        sig, neg_a, g, gcum = _gdn_gates(bat, alog_ref[...], dtb_ref[...], row, col)
        gt = gcum.T
        lane = lax.broadcasted_iota(jnp.int32, (CHUNK, LANE), 1)
        ones = jnp.ones((CHUNK, LANE), F32)
        last_row = lax.broadcasted_iota(jnp.int32, (CHUNK, 1), 0) == CHUNK - 1
        wnv = wn_ref[...]
        dgc_tile = jnp.zeros((CHUNK, LANE), F32)
        dbeta_tile = jnp.zeros((CHUNK, LANE), F32)
        dwn_acc = jnp.zeros((1, LANE), F32)
        hs = []
        for h in range(A_HEADS):
            ln = slice(h * A_HEAD, (h + 1) * A_HEAD)
            lk = slice(A_QK + h * A_HEAD, A_QK + (h + 1) * A_HEAD)
            lv = slice(2 * A_QK + h * A_HEAD, 2 * A_QK + (h + 1) * A_HEAD)
            q_raw, k_raw, v = qkv_ref[:, ln], qkv_ref[:, lk], qkv_ref[:, lv]
            f = _gdn_head_common(q_raw, k_raw, v, sig[:, h:h + 1], gcum[:, 8 + h:9 + h], gt[8 + h:9 + h, :], row, col)
            f.update(h=h, ln=ln, lk=lk, lv=lv, q_raw=q_raw, k_raw=k_raw, v=v, beta=sig[:, h:h + 1],
                     s0=st_ref[0, h], ds1=dstate[h], t=t_ref[0, h], w=w_ref[:, ln], vnew=vn_ref[:, ln])
            o = o_ref[:, ln]
            zz = z_ref[:, ln]
            dyv = dy_ref[:, ln]
            r = lax.rsqrt(jnp.mean(o * o, axis=-1, keepdims=True) + EPS)
            sz = _sigmoid(zz)
            silu = zz * sz
            dz_ref[:, ln] = (dyv * o * r * wnv * sz * (1.0 + zz * (1.0 - sz))).astype(BF16)
            dwn_acc = dwn_acc + jnp.sum(dyv * silu * o * r, axis=0, keepdims=True)
            tt = dyv * silu * wnv
            do = r * tt - o * (r * r * r) * jnp.mean(tt * o, axis=-1, keepdims=True)
            f["do_b"] = do.astype(BF16)
            hs.append(f)
        for f in hs:
            f["dvnew"] = _bdot(f["a"], f["do_b"], TN) + _bdot(f["ke"], f["ds1"], NN)
            f["da"] = jnp.where(f["causal"], _bdot(f["do_b"], f["vnew"], NT), 0.0)
            f["dke"] = _bdot(f["vnew"], f["ds1"], NT)
            f["ddec"] = jnp.sum(jnp.sum(f["s0"] * f["ds1"], axis=1, keepdims=True), axis=0, keepdims=True)
        for f in hs:
            do_dv = jnp.concatenate([f["do_b"], f["dvnew"].astype(BF16)], axis=0)
            both = _bdot(do_dv, f["s0"], NT)
            f["dqd"], f["dw"] = both[:CHUNK], -both[CHUNK:]
            qd_w = jnp.concatenate([f["qd"].astype(BF16), -f["w"]], axis=0)
            dstate[f["h"]] = _bdot(qd_w, do_dv, TN) + f["dec"] * f["ds1"]
        for f in hs:
            dd = jnp.concatenate([f["dvnew"], f["dw"]], axis=1).astype(BF16)
            tdd = _bdot(f["t"], dd, TN)
            f["dvb"], f["dkbg"] = tdd[:, :A_HEAD], tdd[:, A_HEAD:]
            f["dt"] = _bdot(dd, jnp.concatenate([f["vb"], f["kbg"]], axis=1), NT)
        for f in hs:
            f["tdt"] = _bdot(f["t"], f["dt"], TN)
        for f in hs:
            dm = jnp.where(f["strict"], -_bdot(f["tdt"], f["t"], NT), 0.0)
            f["ddecay"] = (dm * f["kk"] + f["da"] * f["qk"]) * f["decay"]
            f["dboth"] = jnp.concatenate([dm * f["decay"], f["da"] * f["decay"]], axis=0).astype(BF16)
        for f in hs:
            f["r2"] = _bdot(f["dboth"], f["k"], NN)
            f["dk0"] = _bdot(f["dboth"], f["kbq"], TN)
        for f in hs:
            h, k, beta = f["h"], f["k"], f["beta"]
            dkb = f["r2"][:CHUNK] + f["dkbg"] * f["eg"]
            dq = f["r2"][CHUNK:] + f["dqd"] * f["eg"]
            dk = f["dk0"] + f["dke"] * f["ekl"] + dkb * beta
            dke_ke = jnp.sum(f["dke"] * f["ke"], axis=-1, keepdims=True)
            dgc = (jnp.sum(f["ddecay"], axis=-1, keepdims=True)
                   + jnp.sum(f["dqd"] * f["qd"], axis=-1, keepdims=True) - dke_ke
                   + jnp.sum(f["dkbg"] * f["kbg"], axis=-1, keepdims=True))
            dgl = jnp.sum(dke_ke, axis=0, keepdims=True) + f["ddec"] * f["dec"]
            dgc = dgc + jnp.where(last_row, dgl, 0.0)
            dbeta = jnp.sum(dkb * k, axis=-1, keepdims=True) + jnp.sum(f["dvb"] * f["v"], axis=-1, keepdims=True)
            dgc_tile = dgc_tile + jnp.where(lane == 8 + h, dgc, 0.0)
            dbeta_tile = dbeta_tile + jnp.where(lane == h, dbeta, 0.0)
            dqn = dq * (A_HEAD ** -0.5)
            rq, rk, q_raw, k_raw = f["rq"], f["rk"], f["q_raw"], f["k_raw"]
            dqkv_ref[:, f["ln"]] = rq * dqn - q_raw * (rq * rq * rq) * jnp.sum(dqn * q_raw, axis=-1, keepdims=True)
            dqkv_ref[:, f["lk"]] = rk * dk - k_raw * (rk * rk * rk) * jnp.sum(dk * k_raw, axis=-1, keepdims=True)
            dqkv_ref[:, f["lv"]] = f["dvb"] * beta
        ddecays = [f["ddecay"] for f in hs]
        col_sums = _dot(jnp.concatenate(ddecays, axis=1), ones, TN, HIGHEST)
        for h in range(A_HEADS):
            dgc_tile = dgc_tile - jnp.where(lane == 8 + h, col_sums[h * CHUNK:(h + 1) * CHUNK, :1], 0.0)
        upper = (row <= col).astype(F32)
        dg = _dot(upper, dgc_tile, NN, HIGHEST)
        da_raw = dg * neg_a * _sigmoid(bat + dtb_ref[...])
        dba_ref[...] = jnp.where(lane < 8, dbeta_tile * sig * (1.0 - sig),
                                 jnp.where(lane < 16, da_raw, 0.0)).astype(BF16)
        dwn_ref[...] += dwn_acc
        ddtb_ref[...] += jnp.sum(da_raw, axis=0, keepdims=True)
        dalog_ref[...] += jnp.sum(dg * g, axis=0, keepdims=True)

    rev = lambda n: (nc - 1 - n, 0)
    vec = pl.BlockSpec((1, LANE), lambda n: (0, 0))
    wide = pl.BlockSpec((CHUNK, A_QK), rev)
    qkv_blk = pl.BlockSpec((CHUNK, A_CONV_WIDTH), rev)
    ba_blk = pl.BlockSpec((CHUNK, LANE), rev)
    vsh = jax.ShapeDtypeStruct((1, LANE), F32)
    return pl.pallas_call(
        body, name="gdn_bwd", grid=(nc,),
        in_specs=[qkv_blk, ba_blk, wide, wide, wide,
                  pl.BlockSpec((1, A_HEADS, A_HEAD, A_HEAD), lambda n: (nc - 1 - n, 0, 0, 0)),
                  pl.BlockSpec((1, A_HEADS, CHUNK, CHUNK), lambda n: (nc - 1 - n, 0, 0, 0)), wide, wide,
                  vec, vec, vec],
        out_specs=[qkv_blk, ba_blk, wide, vec, vec, vec],
        out_shape=[jax.ShapeDtypeStruct((s, A_CONV_WIDTH), F32), jax.ShapeDtypeStruct((s, LANE), BF16),
                   jax.ShapeDtypeStruct((s, A_QK), BF16), vsh, vsh, vsh],
        scratch_shapes=[pltpu.VMEM((A_HEADS, A_HEAD, A_HEAD), F32)],
        compiler_params=_params(("arbitrary",)),
    )(qkv, ba, z, o_raw, dy, states, t_all, w_all, vn_all, alog, dtb, wn)


REL_RING = 1024
QBLK_BITS = 8


def _rel_ring_onehot():
    m = lax.broadcasted_iota(jnp.int32, (REL_RING, REL_PAD), 0)
    t = lax.broadcasted_iota(jnp.int32, (REL_RING, REL_PAD), 1)
    u = jnp.where(m < KBLK, m, m - REL_RING)
    idx = jnp.clip(LEFT - u, -REL_CLIP, REL_CLIP) + REL_CLIP
    return (t == idx).astype(F32)


def _relbias_ring(table, transpose):
    n_in, n_out = (REL_RING, REL_PAD) if transpose else (REL_PAD, REL_RING)

    def body(t_ref, o_ref):
        o_ref[...] = _dot(t_ref[...], _rel_ring_onehot(), NN if transpose else NT, HIGHEST)

    return pl.pallas_call(
        body, name="relbias_ring_bwd" if transpose else "relbias_ring",
        out_shape=jax.ShapeDtypeStruct((B_HEADS, n_out), F32),
        in_specs=[VMEM_FULL], out_specs=VMEM_FULL,
        compiler_params=_params(),
    )(table)


def _row_bit(shape, bit):
    return ((lax.broadcasted_iota(jnp.int32, shape, 0) >> bit) & 1) == 1


def _relbias_expand(ring):
    def body(r_ref, o_ref):
        b = jnp.broadcast_to(r_ref[0], (QBLK, REL_RING))
        for bit in range(QBLK_BITS):
            b = jnp.where(_row_bit(b.shape, bit), pltpu.roll(b, 1 << bit, 1), b)
        j = lax.broadcasted_iota(jnp.int32, (QBLK, KBLK), 1)
        r = lax.broadcasted_iota(jnp.int32, (QBLK, KBLK), 0)
        lo = (r >> 6) << 6
        o_ref[0] = jnp.where((j >= lo) & (j < lo + LEFT + CHUNK), b[:, :KBLK], NEG_INF)

    return pl.pallas_call(
        body, name="relbias_expand", grid=(B_HEADS,),
        in_specs=[pl.BlockSpec((1, 1, REL_RING), lambda h: (h, 0, 0))],
        out_specs=pl.BlockSpec((1, QBLK, KBLK), lambda h: (h, 0, 0)),
        out_shape=jax.ShapeDtypeStruct((B_HEADS, QBLK, KBLK), F32),
        compiler_params=_params(("parallel",)),
    )(ring)


def _relbias_reduce(ds):
    def body(d_ref, o_ref):
        d = jnp.concatenate([d_ref[0], jnp.zeros((QBLK, REL_RING - KBLK), F32)], axis=1)
        for bit in range(QBLK_BITS):
            d = jnp.where(_row_bit(d.shape, bit), pltpu.roll(d, REL_RING - (1 << bit), 1), d)
        o_ref[0] = jnp.sum(d, axis=0, keepdims=True)

    return pl.pallas_call(
        body, name="relbias_reduce", grid=(B_HEADS,),
        in_specs=[pl.BlockSpec((1, QBLK, KBLK), lambda h: (h, 0, 0))],
        out_specs=pl.BlockSpec((1, 1, REL_RING), lambda h: (h, 0, 0)),
        out_shape=jax.ShapeDtypeStruct((B_HEADS, 1, REL_RING), F32),
        compiler_params=_params(("parallel",)),
    )(ds)


def _attn_probs(q_ref, kb, b_ref, hh, q0):
    hl = slice(hh * B_HEAD, (hh + 1) * B_HEAD)
    qh = q_ref[:, hl] * (B_HEAD ** -0.5)
    kh = kb[:, hl]
    jpos = lax.broadcasted_iota(jnp.int32, (QBLK, KBLK), 1)
    sc = _bdot(qh, kh, NT) + b_ref[hh]
    sc = jnp.where(jpos + q0 >= LEFT, sc, NEG_INF)
    mx = jnp.max(sc, axis=-1, keepdims=True)
    p = jnp.exp(sc - mx)
    return p * (1.0 / jnp.sum(p, axis=-1, keepdims=True)), qh, kh


def _attn_fwd(q, kpad, vpad, bias):
    s = q.shape[0]

    def body(q_ref, k_ref, v_ref, b_ref, o_ref):
        q0 = pl.multiple_of(pl.program_id(1) * QBLK, QBLK)
        kb = k_ref[pl.ds(q0, KBLK), :]
        vb = v_ref[pl.ds(q0, KBLK), :]
        outs = []
        for hh in range(2):
            p, _, _ = _attn_probs(q_ref, kb, b_ref, hh, q0)
            outs.append(_bdot(p, vb[:, hh * B_HEAD:(hh + 1) * B_HEAD], NN))
        o_ref[...] = jnp.concatenate(outs, axis=1).astype(BF16)

    qblk = pl.BlockSpec((QBLK, LANE), lambda g, m: (m, g))
    kblk = pl.BlockSpec((LEFT + s, LANE), lambda g, m: (0, g))
    return pl.pallas_call(
        body, name="attn_fwd", grid=(B_HEADS // 2, s // QBLK),
        in_specs=[qblk, kblk, kblk, pl.BlockSpec((2, QBLK, KBLK), lambda g, m: (g, 0, 0))],
        out_specs=qblk,
        out_shape=jax.ShapeDtypeStruct((s, D_MODEL), BF16),
        compiler_params=_params(("parallel", "arbitrary")),
    )(q, kpad, vpad, bias)


def _attn_bwd(q, kpad, vpad, bias, do):
    s = q.shape[0]

    def body(q_ref, k_ref, v_ref, b_ref, do_ref, dq_ref, dk_ref, dv_ref, db_ref):
        @pl.when(pl.program_id(1) == 0)
        def _():
            for r in (dk_ref, dv_ref, db_ref):
                r[...] = jnp.zeros_like(r)

        q0 = pl.multiple_of(pl.program_id(1) * QBLK, QBLK)
        kb = k_ref[pl.ds(q0, KBLK), :]
        vb = v_ref[pl.ds(q0, KBLK), :]
        dqs, dks, dvs = [], [], []
        for hh in range(2):
            hl = slice(hh * B_HEAD, (hh + 1) * B_HEAD)
            p, qh, kh = _attn_probs(q_ref, kb, b_ref, hh, q0)
            doh = do_ref[:, hl]
            dp = _bdot(doh, vb[:, hl], NT)
            dsc = p * (dp - jnp.sum(p * dp, axis=-1, keepdims=True))
            db_ref[hh] += dsc
            dqs.append(_bdot(dsc, kh, NN) * (B_HEAD ** -0.5))
            dks.append(_bdot(dsc, qh, TN))
            dvs.append(_bdot(p, doh, TN))
        dq_ref[...] = jnp.concatenate(dqs, axis=1).astype(BF16)
        dk_ref[pl.ds(q0, KBLK), :] += jnp.concatenate(dks, axis=1)
        dv_ref[pl.ds(q0, KBLK), :] += jnp.concatenate(dvs, axis=1)

    qblk = pl.BlockSpec((QBLK, LANE), lambda g, m: (m, g))
    kblk = pl.BlockSpec((LEFT + s, LANE), lambda g, m: (0, g))
    bblk = pl.BlockSpec((2, QBLK, KBLK), lambda g, m: (g, 0, 0))
    return pl.pallas_call(
        body, name="attn_bwd", grid=(B_HEADS // 2, s // QBLK),
        in_specs=[qblk, kblk, kblk, bblk, qblk],
        out_specs=[qblk, kblk, kblk, bblk],
        out_shape=[jax.ShapeDtypeStruct((s, D_MODEL), BF16), jax.ShapeDtypeStruct((LEFT + s, D_MODEL), F32),
                   jax.ShapeDtypeStruct((LEFT + s, D_MODEL), F32),
                   jax.ShapeDtypeStruct((B_HEADS, QBLK, KBLK), F32)],
        compiler_params=_params(("parallel", "arbitrary")),
    )(q, kpad, vpad, bias, do)


def _adamw(w, g, m, v):
    r, c = w.shape
    tr = r
    for cand in (512, 256, 128, 64, 32, 16, 8):
        if r % cand == 0 and cand * c * 4 <= 2 * 1024 * 1024:
            tr = cand
            break
    c1 = 1.0 / (1.0 - ADAM_B1 ** ADAM_STEP)
    c2 = 1.0 / (1.0 - ADAM_B2 ** ADAM_STEP)

    def body(w_ref, g_ref, m_ref, v_ref, d_ref, mo_ref, vo_ref):
        gv = g_ref[...]
        mn = ADAM_B1 * m_ref[...] + (1.0 - ADAM_B1) * gv
        vn = ADAM_B2 * v_ref[...] + (1.0 - ADAM_B2) * (gv * gv)
        mo_ref[...] = mn
        vo_ref[...] = vn
        d_ref[...] = -ADAM_LR * ((mn * c1) / (jnp.sqrt(vn * c2) + ADAM_EPS) + ADAM_WD * w_ref[...])

    blk = pl.BlockSpec((tr, c), lambda i: (i, 0))
    sh = jax.ShapeDtypeStruct((r, c), F32)
    return pl.pallas_call(
        body, name="adamw", grid=(r // tr,),
        in_specs=[blk] * 4, out_specs=[blk] * 3, out_shape=[sh] * 3,
        compiler_params=_params(("parallel",)),
    )(w, g, m, v)


def _row(v, width=None):
    v = v.reshape(1, -1)
    if width is not None and v.shape[1] < width:
        v = jnp.pad(v, ((0, 0), (0, width - v.shape[1])))
    return v


def _gate_row(v):
    return jnp.pad(v.reshape(1, A_HEADS), ((0, 0), (A_HEADS, LANE - 2 * A_HEADS)))


DEPTH = 4
N_A = 2
N_B = 2
F_DOWN_ROWS = FFN_DIM // N_SHARDS
SQ_ROWS = D_MODEL // N_SHARDS
GD_A_OUT0 = DEPTH * F_DOWN_ROWS // SQ_ROWS
GD_B_Q0 = GD_A_OUT0 + N_A
GD_B_OUT0 = GD_B_Q0 + N_B
UP_COLS = 2 * FFN_DIM // N_SHARDS


def _a_layer_fwd(h, w, i):
    xn = _rmsnorm_fwd(h, _row(w["a_norm"][i]))
    w_qkv, w_z, w_ba = w["a_in"][i]
    pre = _mm(xn, w_qkv, name="a_qkv")
    z = _mm(xn, w_z, name="a_z")
    ba = _mm(xn, w_ba, name="a_ba")
    act = _conv_silu_fwd(pre, w["a_conv"][i])
    alog, dtb, wn = _gate_row(w["a_A_log"][i]), _gate_row(w["a_dt_bias"][i]), _row(w["a_out_norm"][i])
    y, o_raw, states, t_all, w_all, vn_all = _gdn_fwd(act, ba, z, alog, dtb, wn)
    h2 = _mm_rowsh(y, w["GD"], SQ_ROWS, GD_A_OUT0 + i, "nn", "a_out", res=h)
    saved = dict(h=h, xn=xn, pre=pre, z=z, ba=ba, act=act, o_raw=o_raw, y=y, states=states,
                 t_all=t_all, w_all=w_all, vn_all=vn_all, alog=alog, dtb=dtb, wn=wn)
    return h2, saved


def _a_layer_bwd(dh2, w, i, sv):
    g = {}
    g["w_out"] = _mm_rowsh_dw(sv["y"], dh2, SQ_ROWS, "a_out_dw")
    dy = _mm_rowsh(dh2, w["GD"], SQ_ROWS, GD_A_OUT0 + i, "nt", "a_out_dx")
    dact, dba, dz, dalog, ddtb, dwn = _gdn_bwd(sv["act"], sv["ba"], sv["z"], sv["o_raw"], dy, sv["states"],
                                               sv["t_all"], sv["w_all"], sv["vn_all"],
                                               sv["alog"], sv["dtb"], sv["wn"])
    dpre, g["conv"] = _conv_silu_bwd(sv["pre"], w["a_conv"][i], dact)
    xn = sv["xn"]
    w_qkv, w_z, w_ba = w["a_in"][i]
    d_in = jnp.concatenate([_mm(xn, dpre, "tn", name="a_qkv_dw"), _mm(xn, dz, "tn", name="a_z_dw"),
                            _mm(xn, dba, "tn", name="a_ba_dw")[:, :2 * A_HEADS]], axis=1)
    g["w_in"] = jnp.transpose(d_in.reshape(D_MODEL, N_SHARDS, -1), (1, 0, 2))
    dxn = _mm(dpre, w_qkv, "nt", name="a_qkv_dx")
    dxn = _mm(dz, w_z, "nt", res=dxn, name="a_z_dx")
    dh, dnorm = _mm(dba, w_ba, "nt", res=dxn, name="a_ba_dx", norm=(sv["h"], _row(w["a_norm"][i]), dh2))
    g["norm"] = dnorm[0]
    g["A_log"] = dalog[0, A_HEADS:2 * A_HEADS]
    g["dt_bias"] = ddtb[0, A_HEADS:2 * A_HEADS]
    g["out_norm"] = dwn[0]
    return dh, g


def _by_half(a, lead):
    return jnp.moveaxis(a.reshape(a.shape[:-1] + (2, 2, UP_COLS)), (-3, -2), (0, 1)).reshape((2, 2) + lead + (UP_COLS,))


def _from_half(a):
    lead = a.shape[2:-1]
    return jnp.moveaxis(a, (0, 1), (-3, -2)).reshape(lead + (N_SHARDS * UP_COLS,))


def _ffn_fwd(h, w, l):
    s = h.shape[0]
    xn = _rmsnorm_fwd(h, _row(w["f_norm"][l]))
    cw = _by_half(w["f_conv"][l], (w["f_conv"].shape[1],))
    cb = _by_half(w["f_conv_b"][l][None], (1,))
    pre = _mm_colsh(xn, w["GU"], D_MODEL, l, "nn", "f_up").reshape(2, 2, s, UP_COLS)
    act = _ffn_act_fwd(pre, cw, cb)
    h2 = _mm_rowsh(act, w["GD"], F_DOWN_ROWS, l, "nn", "f_down", res=h)
    return h2, dict(h=h, xn=xn, pre=pre, act=act, cw=cw, cb=cb)


def _ffn_bwd(dh2, w, l, sv):
    g = {}
    s = dh2.shape[0]
    g["w_down"] = _mm_rowsh_dw(sv["act"], dh2, F_DOWN_ROWS, "f_down_dw")
    dact = _mm_rowsh(dh2, w["GD"], F_DOWN_ROWS, l, "nt", "f_down_dx")
    dpre, dcw, dcb = _ffn_act_bwd(sv["pre"], sv["cw"], sv["cb"], dact)
    dpre = dpre.reshape(N_SHARDS, s, UP_COLS)
    g["w_up"] = _mm_colsh_dw(sv["xn"], dpre, "f_up_dw")
    g["conv"] = _from_half(dcw)
    g["conv_b"] = _from_half(dcb)[0]
    dh, dnorm = _mm_colsh(dpre, w["GU"], D_MODEL, l, "nt", "f_up_dx", norm=(sv["h"], _row(w["f_norm"][l]), dh2))
    g["norm"] = dnorm[0]
    return dh, g


def _b_layer_fwd(h, w, j, kpad, vpad):
    xn = _rmsnorm_fwd(h, _row(w["b_norm"][j]))
    q = _mm_rowsh(xn, w["GD"], SQ_ROWS, GD_B_Q0 + j, "nn", "b_q", out_dtype=BF16)
    rel = w["b_rel_bias"][j]
    table = jnp.pad(rel, ((0, 0), (0, REL_PAD - rel.shape[1])))
    bias = _relbias_expand(_relbias_ring(table, False).reshape(B_HEADS, 1, REL_RING))
    o = _attn_fwd(q, kpad, vpad, bias)
    h2 = _mm_rowsh(o, w["GD"], SQ_ROWS, GD_B_OUT0 + j, "nn", "b_out", res=h)
    return h2, dict(h=h, xn=xn, q=q, o=o, bias=bias)


def _b_layer_bwd(dh2, w, j, sv, kpad, vpad):
    g = {}
    g["w_out"] = _mm_rowsh_dw(sv["o"], dh2, SQ_ROWS, "b_out_dw")
    do = _mm_rowsh(dh2, w["GD"], SQ_ROWS, GD_B_OUT0 + j, "nt", "b_out_dx", out_dtype=BF16)
    dq, dkp, dvp, dsc = _attn_bwd(sv["q"], kpad, vpad, sv["bias"], do)
    dring = _relbias_reduce(dsc).reshape(B_HEADS, REL_RING)
    g["rel_bias"] = _relbias_ring(dring, True)[:, :2 * REL_CLIP + 1]
    g["w_q"] = _mm_rowsh_dw(sv["xn"], dq, SQ_ROWS, "b_q_dw")
    dh, dnorm = _mm_rowsh(dq, w["GD"], SQ_ROWS, GD_B_Q0 + j, "nt", "b_q_dx",
                          norm=(sv["h"], _row(w["b_norm"][j]), dh2))
    g["norm"] = dnorm[0]
    return dh, g, dkp, dvp


def _local_step(x, tgt, w):
    h = x
    saved = []
    kv_saved = None
    kpad = vpad = None
    for layer in range(DEPTH):
        if layer < N_A:
            h, sm = _a_layer_fwd(h, w, layer)
        else:
            if layer == N_A:
                xn_kv = _rmsnorm_fwd(h, _row(w["kv_norm"]))
                kv = _mm_colsh(xn_kv, w["GK"], D_MODEL, 0, "nn", "kv", flat=True, out_dtype=BF16)
                kpad = jnp.pad(kv[:, :D_MODEL], ((LEFT, 0), (0, 0)))
                vpad = jnp.pad(kv[:, D_MODEL:], ((LEFT, 0), (0, 0)))
                kv_saved = dict(h=h, xn=xn_kv)
            h, sm = _b_layer_fwd(h, w, layer - N_A, kpad, vpad)
        h, sf = _ffn_fwd(h, w, layer)
        saved.append((sm, sf))

    loss, dh, dfinal = _final_loss(h, _row(w["final_norm"]), tgt)

    ga = [None] * N_A
    gb = [None] * N_B
    gf = [None] * DEPTH
    dk_tot = dv_tot = None
    g_kv = g_kvn = None
    for layer in reversed(range(DEPTH)):
        sm, sf = saved[layer]
        dh, gf[layer] = _ffn_bwd(dh, w, layer, sf)
        if layer >= N_A:
            dh, gb[layer - N_A], dkp, dvp = _b_layer_bwd(dh, w, layer - N_A, sm, kpad, vpad)
            dk_tot = dkp if dk_tot is None else dk_tot + dkp
            dv_tot = dvp if dv_tot is None else dv_tot + dvp
            if layer == N_A:
                dkv = jnp.concatenate([dk_tot[LEFT:], dv_tot[LEFT:]], axis=1).astype(BF16)
                g_kv = _mm_colsh_dw(kv_saved["xn"], dkv, "kv_dw", flat=True)
                dh, g_kvn = _mm_colsh(dkv, w["GK"], D_MODEL, 0, "nt", "kv_dx", flat=True,
                                      norm=(kv_saved["h"], _row(w["kv_norm"]), dh))
        else:
            dh, ga[layer] = _a_layer_bwd(dh, w, layer, sm)

    def stack(gs, key):
        return jnp.stack([g[key] for g in gs])

    def rows(gs, key):
        return [g[key] for g in gs]

    grads = dict(
        GU=jnp.concatenate(rows(gf, "w_up"), axis=1),
        GD=jnp.concatenate(rows(gf, "w_down") + rows(ga, "w_out") + rows(gb, "w_q") + rows(gb, "w_out"), axis=1),
        GK=g_kv,
        GI=jnp.concatenate(rows(ga, "w_in"), axis=1),
        a_norm=stack(ga, "norm"), a_conv=stack(ga, "conv"), a_A_log=stack(ga, "A_log"),
        a_dt_bias=stack(ga, "dt_bias"), a_out_norm=stack(ga, "out_norm"), kv_norm=g_kvn[0],
        b_norm=stack(gb, "norm"), b_rel_bias=stack(gb, "rel_bias"),
        f_norm=stack(gf, "norm"), f_conv=stack(gf, "conv"), f_conv_b=stack(gf, "conv_b"), final_norm=dfinal[0])
    return loss, dh, grads


HBM_SPEC = pl.BlockSpec(memory_space=pl.ANY)
VMEM_SPEC = pl.BlockSpec(memory_space=pltpu.VMEM)


def _place():
    x, y, c = lax.axis_index("x"), lax.axis_index("y"), lax.axis_index("c")
    chips = [(1 - x, y), (x, 1 - y), (1 - x, 1 - y)]
    return x, y, c, chips


def _remote(src, dst, send_sem, recv_sem, to):
    return pltpu.make_async_remote_copy(src_ref=src, dst_ref=dst, send_sem=send_sem, recv_sem=recv_sem,
                                        device_id=to, device_id_type=MESH)


def _allgather_weights(shards):
    n = len(shards)
    per = 7

    def body(*refs):
        x_refs, out_refs, (send_sems, recv_sems) = refs[:n], refs[n:2 * n], refs[2 * n:]
        x, y, c, chips = _place()
        sibling = (x, y, 1 - c)

        def half(a, px, py, hc):
            rh = shards[a].shape[0] // 2
            return out_refs[a].at[2 * px + py, pl.ds(hc * rh, rh), :]

        def mine(a):
            rh = shards[a].shape[0] // 2
            return x_refs[a].at[pl.ds(c * rh, rh), :]

        def sems(a, k):
            return send_sems.at[per * a + k], recv_sems.at[per * a + k]

        first = [_remote(mine(a), half(a, x, y, c), *sems(a, j), (*chip, c))
                 for a in range(n) for j, chip in enumerate(chips)]
        own = [_remote(x_refs[a], out_refs[a].at[2 * x + y], *sems(a, 6), sibling) for a in range(n)]
        for cp in first + own:
            cp.start()
        passed = []
        for j, chip in enumerate(chips):
            for a in range(n):
                landed = half(a, *chip, c)
                _remote(landed, landed, *sems(a, j), (*chip, c)).wait_recv()
                cp = _remote(landed, landed, *sems(a, 3 + j), sibling)
                cp.start()
                passed.append(cp)
        for j, chip in enumerate(chips):
            for a in range(n):
                theirs = half(a, *chip, 1 - c)
                _remote(theirs, theirs, *sems(a, 3 + j), sibling).wait_recv()
        for cp in own:
            cp.wait_recv()
        for cp in first + passed + own:
            cp.wait_send()

    return pl.pallas_call(
        body, name="allgather_weights",
        out_shape=[jax.ShapeDtypeStruct((N_SHARDS,) + sh.shape, sh.dtype) for sh in shards],
        in_specs=[HBM_SPEC] * n, out_specs=[HBM_SPEC] * n,
        scratch_shapes=[pltpu.SemaphoreType.DMA((per * n,)), pltpu.SemaphoreType.DMA((per * n,))],
    )(*shards)


def _pair_exchange(gs):
    n = len(gs)

    def body(*refs):
        g_refs, out_refs, (send_sems, recv_sems) = refs[:n], refs[n:2 * n], refs[2 * n:]
        x, y, c, _ = _place()
        cps = []
        for a in range(n):
            rh = gs[a].shape[1] // 2
            cps.append(_remote(g_refs[a].at[:, pl.ds((1 - c) * rh, rh), :], out_refs[a], send_sems.at[a],
                               recv_sems.at[a], (x, y, 1 - c)))
        for cp in cps:
            cp.start()
        for cp in cps:
            cp.wait()

    return pl.pallas_call(
        body, name="rs_pair_exchange",
        out_shape=[jax.ShapeDtypeStruct((g.shape[0], g.shape[1] // 2, g.shape[2]), g.dtype) for g in gs],
        in_specs=[HBM_SPEC] * n, out_specs=[HBM_SPEC] * n,
        scratch_shapes=[pltpu.SemaphoreType.DMA((n,)), pltpu.SemaphoreType.DMA((n,))],
    )(*gs)


def _add_rows(rows, cols):
    best = 16
    for t in range(16, rows + 1, 16):
        if rows % t == 0 and t * cols * 4 <= 2304 * 1024:
            best = t
    return best


def _pair_add(g, other, c_idx):
    n, r, cols = g.shape
    rh = r // 2
    tr = _add_rows(rh, cols)
    nb = rh // tr

    def body(c_ref, a_ref, b_ref, o_ref, ob_ref):
        sm = a_ref[...] + b_ref[...]
        o_ref[...] = sm
        ob_ref[...] = sm.astype(BF16)

    out_blk = pl.BlockSpec((1, tr, cols), lambda s, i, c_ref: (s, i, 0))
    return pl.pallas_call(
        body, name="rs_pair_add",
        grid_spec=pltpu.PrefetchScalarGridSpec(
            num_scalar_prefetch=1, grid=(n, nb),
            in_specs=[pl.BlockSpec((1, tr, cols), lambda s, i, c_ref: (s, c_ref[0] * nb + i, 0)), out_blk],
            out_specs=[out_blk, out_blk]),
        out_shape=[jax.ShapeDtypeStruct((n, rh, cols), F32), jax.ShapeDtypeStruct((n, rh, cols), BF16)],
        compiler_params=_params(("parallel", "parallel")),
    )(c_idx, g, other)


def _chip_exchange(ps):
    n = len(ps)

    def body(*refs):
        p_refs, out_refs, (send_sems, recv_sems) = refs[:n], refs[n:2 * n], refs[2 * n:]
        x, y, c, chips = _place()
        cps = [_remote(p_refs[a].at[2 * chip[0] + chip[1]], out_refs[a].at[j], send_sems.at[3 * a + j],
                       recv_sems.at[3 * a + j], (*chip, c))
               for a in range(n) for j, chip in enumerate(chips)]
        for cp in cps:
            cp.start()
        for cp in cps:
            cp.wait()

    return pl.pallas_call(
        body, name="rs_chip_exchange",
        out_shape=[jax.ShapeDtypeStruct((3,) + p.shape[1:], p.dtype) for p in ps],
        in_specs=[HBM_SPEC] * n, out_specs=[HBM_SPEC] * n,
        scratch_shapes=[pltpu.SemaphoreType.DMA((3 * n,)), pltpu.SemaphoreType.DMA((3 * n,))],
    )(*ps)


def _chip_add(p, recv, chip_idx):
    n, rh, cols = p.shape
    tr = _add_rows(rh, cols)

    def body(s_ref, own_ref, r_ref, o_ref):
        o_ref[...] = ((own_ref[0] + r_ref[0].astype(F32)) + r_ref[1].astype(F32)) + r_ref[2].astype(F32)

    return pl.pallas_call(
        body, name="rs_chip_add",
        grid_spec=pltpu.PrefetchScalarGridSpec(
            num_scalar_prefetch=1, grid=(rh // tr,),
            in_specs=[pl.BlockSpec((1, tr, cols), lambda i, s_ref: (s_ref[0], i, 0)),
                      pl.BlockSpec((3, tr, cols), lambda i, s_ref: (0, i, 0))],
            out_specs=pl.BlockSpec((tr, cols), lambda i, s_ref: (i, 0))),
        out_shape=jax.ShapeDtypeStruct((rh, cols), p.dtype),
        compiler_params=_params(("parallel",)),
    )(chip_idx, p, recv)


def _pair_gather(fs):
    n = len(fs)

    def body(*refs):
        f_refs, out_refs, (send_sems, recv_sems) = refs[:n], refs[n:2 * n], refs[2 * n:]
        x, y, c, _ = _place()
        cps = [_remote(f_refs[a], out_refs[a], send_sems.at[a], recv_sems.at[a], (x, y, 1 - c)) for a in range(n)]
        for cp in cps:
            cp.start()
        for cp in cps:
            cp.wait()

    return pl.pallas_call(
        body, name="rs_pair_gather",
        out_shape=[jax.ShapeDtypeStruct(f.shape, f.dtype) for f in fs],
        in_specs=[HBM_SPEC] * n, out_specs=[HBM_SPEC] * n,
        scratch_shapes=[pltpu.SemaphoreType.DMA((n,)), pltpu.SemaphoreType.DMA((n,))],
    )(*fs)


def _allreduce_small(v):
    r, cols = v.shape

    def body(x_ref, out_ref, slots, send_sems, recv_sems):
        x, y, c, _ = _place()
        bits = [(bx, by, bc) for bx in (0, 1) for by in (0, 1) for bc in (0, 1)]

        def flip(b):
            return (1 - x if b[0] else x, 1 - y if b[1] else y, 1 - c if b[2] else c)

        slots[0] = x_ref[...]
        cps = [_remote(x_ref, slots.at[k], send_sems.at[k - 1], recv_sems.at[k - 1], flip(bits[k]))
               for k in range(1, 8)]
        for cp in cps:
            cp.start()
        for cp in cps:
            cp.wait()
        acc = None
        for b in bits:
            fx, fy, fc = flip(b)
            term = slots[4 * fx + 2 * fy + fc]
            acc = term if acc is None else acc + term
        out_ref[...] = acc

    return pl.pallas_call(
        body, name="allreduce_small",
        out_shape=jax.ShapeDtypeStruct((r, cols), v.dtype),
        in_specs=[VMEM_SPEC], out_specs=VMEM_SPEC,
        scratch_shapes=[pltpu.VMEM((8, r, cols), v.dtype), pltpu.SemaphoreType.DMA((7,)),
                        pltpu.SemaphoreType.DMA((7,))],
        compiler_params=pltpu.CompilerParams(vmem_limit_bytes=VMEM_LIMIT),
    )(v)


BIG = (("a_w_in", 2), ("a_w_out", 1), ("w_kv", 1), ("b_w_q", 1), ("b_w_out", 1), ("f_w_up", 2), ("f_w_down", 1))
SMALL = (("a_norm", 1), ("a_conv", 2), ("a_A_log", None), ("a_dt_bias", None), ("a_out_norm", None),
         ("kv_norm", None), ("b_norm", None), ("b_rel_bias", None), ("f_norm", None), ("f_conv", 2),
         ("f_conv_b", None), ("final_norm", None))
WEIGHT_ORDER = ("a_norm", "a_w_in", "a_conv", "a_A_log", "a_dt_bias", "a_out_norm", "a_w_out", "kv_norm", "w_kv",
                "b_norm", "b_w_q", "b_rel_bias", "b_w_out", "f_norm", "f_w_up", "f_conv", "f_conv_b", "f_w_down",
                "final_norm")


def _pad_rows(flat, cols, quantum):
    n = flat.shape[-1]
    rows = -(-n // (cols * quantum)) * quantum
    pad = [(0, 0)] * (flat.ndim - 1) + [(0, rows * cols - n)]
    return jnp.pad(flat, pad).reshape(flat.shape[:-1] + (rows, cols))


GROUPS = ("GU", "GD", "GK", "GI")


def _group_shards(w, dtype):
    def two(a):
        return a.reshape(-1, a.shape[-1])

    return dict(GU=two(w["f_w_up"]).astype(dtype),
                GD=jnp.concatenate([two(w[n]) for n in ("f_w_down", "a_w_out", "b_w_q", "b_w_out")]).astype(dtype),
                GK=w["w_kv"].astype(dtype),
                GI=two(w["a_w_in"]).astype(dtype))


def _ungroup(red, shard_shapes):
    out = dict(f_w_up=red["GU"].reshape(shard_shapes["f_w_up"]), w_kv=red["GK"],
               a_w_in=red["GI"].reshape(shard_shapes["a_w_in"]))
    off = 0
    for n in ("f_w_down", "a_w_out", "b_w_q", "b_w_out"):
        shp = shard_shapes[n]
        rows = math.prod(shp[:-1])
        out[n] = red["GD"][off:off + rows].reshape(shp)
        off += rows
    return out


def _dense_a_in(gi):
    out = []
    for i in range(N_A):
        full = jnp.transpose(gi[:, i * D_MODEL:(i + 1) * D_MODEL], (1, 0, 2)).reshape(D_MODEL, -1)
        out.append((full[:, :A_CONV_WIDTH], full[:, A_CONV_WIDTH:A_CONV_WIDTH + A_QK],
                    jnp.pad(full[:, A_CONV_WIDTH + A_QK:], ((0, 0), (0, LANE - 2 * A_HEADS)))))
    return out


def _pack_small(values, names):
    return _pad_rows(jnp.concatenate([values[n].reshape(-1) for n in names]), LANE, SUB)


def _unpack_small(packed, shapes, names):
    flat = packed.reshape(-1)
    out, off = {}, 0
    for n in names:
        size = math.prod(shapes[n])
        out[n] = flat[off:off + size].reshape(shapes[n])
        off += size
    return out


def _adamw_nd(w, g, m, v):
    shp = w.shape
    two = (math.prod(shp[:-1]), shp[-1])
    d, mn, vn = _adamw(w.reshape(two), g.reshape(two), m.reshape(two), v.reshape(two))
    return d.reshape(shp), mn.reshape(shp), vn.reshape(shp)


def kernel(x, a_norm, a_w_in, a_conv, a_A_log, a_dt_bias, a_out_norm, a_w_out, kv_norm, w_kv, b_norm, b_w_q, b_rel_bias, b_w_out, f_norm, f_w_up, f_conv, f_conv_b, f_w_down, final_norm, loss_target, m_a_norm, m_a_w_in, m_a_conv, m_a_A_log, m_a_dt_bias, m_a_out_norm, m_a_w_out, m_kv_norm, m_w_kv, m_b_norm, m_b_w_q, m_b_rel_bias, m_b_w_out, m_f_norm, m_f_w_up, m_f_conv, m_f_conv_b, m_f_w_down, m_final_norm, v_a_norm, v_a_w_in, v_a_conv, v_a_A_log, v_a_dt_bias, v_a_out_norm, v_a_w_out, v_kv_norm, v_w_kv, v_b_norm, v_b_w_q, v_b_rel_bias, v_b_w_out, v_f_norm, v_f_w_up, v_f_conv, v_f_conv_b, v_f_w_down, v_final_norm):
    w = dict(a_norm=a_norm, a_w_in=a_w_in, a_conv=a_conv, a_A_log=a_A_log, a_dt_bias=a_dt_bias,
             a_out_norm=a_out_norm, a_w_out=a_w_out, kv_norm=kv_norm, w_kv=w_kv, b_norm=b_norm, b_w_q=b_w_q,
             b_rel_bias=b_rel_bias, b_w_out=b_w_out, f_norm=f_norm, f_w_up=f_w_up, f_conv=f_conv,
             f_conv_b=f_conv_b, f_w_down=f_w_down, final_norm=final_norm)
    m = dict(a_norm=m_a_norm, a_w_in=m_a_w_in, a_conv=m_a_conv, a_A_log=m_a_A_log, a_dt_bias=m_a_dt_bias,
             a_out_norm=m_a_out_norm, a_w_out=m_a_w_out, kv_norm=m_kv_norm, w_kv=m_w_kv, b_norm=m_b_norm,
             b_w_q=m_b_w_q, b_rel_bias=m_b_rel_bias, b_w_out=m_b_w_out, f_norm=m_f_norm, f_w_up=m_f_w_up,
             f_conv=m_f_conv, f_conv_b=m_f_conv_b, f_w_down=m_f_w_down, final_norm=m_final_norm)
    v = dict(a_norm=v_a_norm, a_w_in=v_a_w_in, a_conv=v_a_conv, a_A_log=v_a_A_log, a_dt_bias=v_a_dt_bias,
             a_out_norm=v_a_out_norm, a_w_out=v_a_w_out, kv_norm=v_kv_norm, w_kv=v_w_kv, b_norm=v_b_norm,
             b_w_q=v_b_w_q, b_rel_bias=v_b_rel_bias, b_w_out=v_b_w_out, f_norm=v_f_norm, f_w_up=v_f_w_up,
             f_conv=v_f_conv, f_conv_b=v_f_conv_b, f_w_down=v_f_w_down, final_norm=v_final_norm)
    xi, yi, ci = lax.axis_index("x"), lax.axis_index("y"), lax.axis_index("c")
    chip = 2 * xi + yi
    shard_shapes = {n: w[n].shape for n in WEIGHT_ORDER}

    mine_w = _group_shards(w, BF16)
    full = dict(zip(GROUPS, _allgather_weights([mine_w[n] for n in GROUPS])))
    full["a_in"] = _dense_a_in(full.pop("GI"))
    sharded_small = [n for n, axis in SMALL if axis is not None]
    placed = {}
    for n, axis in SMALL:
        if axis is not None:
            wide = list(w[n].shape)
            wide[axis] *= 4
            mine_once = w[n] * (1 - ci).astype(F32)
            placed[n] = lax.dynamic_update_slice_in_dim(jnp.zeros(wide, F32), mine_once, chip * w[n].shape[axis], axis)
    placed_shapes = {n: placed[n].shape for n in sharded_small}
    full.update(_unpack_small(_allreduce_small(_pack_small(placed, sharded_small)), placed_shapes, sharded_small))
    for n, axis in SMALL:
        if axis is None:
            full[n] = w[n]

    loss_part, grad_x, grads = _local_step(x[0], loss_target[0], full)

    packed = [grads[n] for n in GROUPS]
    c_idx = jnp.reshape(ci, (1,)).astype(jnp.int32)
    chip_idx = jnp.reshape(chip, (1,)).astype(jnp.int32)
    pairs = [_pair_add(g, o, c_idx) for g, o in zip(packed, _pair_exchange(packed))]
    arrived = _chip_exchange([pb for _, pb in pairs])
    mine = [_chip_add(p, r, chip_idx) for (p, _), r in zip(pairs, arrived)]
    theirs = _pair_gather(mine)
    red = _ungroup({n: jnp.concatenate([jnp.where(ci == 0, a, b), jnp.where(ci == 0, b, a)], axis=0)
                    for n, a, b in zip(GROUPS, mine, theirs)}, shard_shapes)

    small_names = [n for n, _ in SMALL]
    small_vals = {n: grads[n] for n in small_names}
    small_vals["loss"] = loss_part[0, :1]
    names = ["loss"] + small_names
    shapes = {n: small_vals[n].shape for n in names}
    summed = _unpack_small(_allreduce_small(_pack_small(small_vals, names)), shapes, names)
    loss = summed["loss"][0]
    for n, axis in SMALL:
        g = summed[n]
        if axis is not None:
            g = lax.dynamic_slice_in_dim(g, chip * w[n].shape[axis], w[n].shape[axis], axis)
        red[n] = g

    delta, new_m, new_v = {}, {}, {}
    for n, _ in BIG:
        delta[n], new_m[n], new_v[n] = _adamw_nd(w[n], red[n], m[n], v[n])
    local_shapes = {n: w[n].shape for n in small_names}
    packs = [_pack_small(t, small_names) for t in (w, red, m, v)]
    outs = _adamw(*packs)
    ds, ms, vs = (_unpack_small(o, local_shapes, small_names) for o in outs)
    delta.update(ds)
    new_m.update(ms)
    new_v.update(vs)

    return (loss, grad_x[None], *[red[n] for n in WEIGHT_ORDER], *[delta[n] for n in WEIGHT_ORDER],
            *[new_m[n] for n in WEIGHT_ORDER], *[new_v[n] for n in WEIGHT_ORDER])
```

```python
import math

import jax
import jax.numpy as jnp
from jax import lax
from jax.experimental import pallas as pl
from jax.experimental.pallas import tpu as pltpu

F32 = jnp.float32
BF16 = jnp.bfloat16
HIGHEST = lax.Precision.HIGHEST
MESH = pl.DeviceIdType.MESH

D_MODEL = 1024
CHUNK = 64
A_HEADS = 8
A_HEAD = 128
A_QK = A_HEADS * A_HEAD
A_CONV_WIDTH = 3 * A_QK
B_HEADS = 16
B_HEAD = 64
LEFT = 8 * CHUNK
QBLK = 4 * CHUNK
KBLK = LEFT + QBLK
REL_CLIP = 256
REL_PAD = 640
FFN_DIM = 2816
EPS = 1e-6
NEG_INF = -1e30
LANE = 128
SUB = 8
VMEM_LIMIT = 56 * 1024 * 1024

ADAM_LR = 0.001
ADAM_B1 = 0.9
ADAM_B2 = 0.999
ADAM_EPS = 1e-08
ADAM_WD = 0.01
ADAM_STEP = 10


VMEM_FULL = pl.BlockSpec(memory_space=pltpu.VMEM)


def _params(sem=None):
    return pltpu.CompilerParams(dimension_semantics=sem, vmem_limit_bytes=VMEM_LIMIT)


def _tile(n, cap):
    if n <= cap:
        return n
    best = None
    for t in range(LANE, cap + 1, LANE):
        if n % t == 0:
            best = t
    assert best is not None, n
    return best


def _sigmoid(x):
    return 0.5 * jnp.tanh(0.5 * x) + 0.5


def _softplus(x):
    return jnp.maximum(x, 0.0) + jnp.log(1.0 + jnp.exp(-jnp.abs(x)))


def _dot(a, b, dims, prec=None):
    return lax.dot_general(a, b, (dims, ((), ())), preferred_element_type=F32, precision=prec)


NN = ((1,), (0,))
NT = ((1,), (1,))
TN = ((0,), (0,))


def _bdot(a, b, dims):
    return _dot(a.astype(BF16), b.astype(BF16), dims)


def _mm(a, b, mode="nn", out_dtype=F32, res=None, name="mm", norm=None):
    if mode == "nn":
        (m, k), (k2, n) = a.shape, b.shape
    elif mode == "nt":
        (m, k), (n, k2) = a.shape, b.shape
    else:
        (k, m), (k2, n) = a.shape, b.shape
    assert k == k2, (a.shape, b.shape, mode)
    tm, tn, tk = _tile(m, 1408), _tile(n, 1408), _tile(k, 1408)
    if m == 8192:
        tm = 1024
    if k == 8192:
        tk = 1024
    nk = k // tk
    dims = {"nn": NN, "nt": NT, "tn": TN}[mode]
    a_spec = {"nn": pl.BlockSpec((tm, tk), lambda i, j, kk: (i, kk)),
              "nt": pl.BlockSpec((tm, tk), lambda i, j, kk: (i, kk)),
              "tn": pl.BlockSpec((tk, tm), lambda i, j, kk: (kk, i))}[mode]
    b_spec = {"nn": pl.BlockSpec((tk, tn), lambda i, j, kk: (kk, j)),
              "nt": pl.BlockSpec((tn, tk), lambda i, j, kk: (j, kk)),
              "tn": pl.BlockSpec((tk, tn), lambda i, j, kk: (kk, j))}[mode]
    o_spec = pl.BlockSpec((tm, tn), lambda i, j, kk: (i, j))
    return _mm_call(name, a, b, dims, (m // tm, n // tn, nk), a_spec, b_spec, o_spec, (m, n), out_dtype, (tm, tn),
                    res, norm)


ROW_TILE = 1024
N_SHARDS = 4


def _mm_call(name, a, b, dims, grid, a_spec, b_spec, o_spec, out_shape, out_dtype, acc_shape, res=None, norm=None,
             into=None):
    nk = grid[2]
    has_res = res is not None
    has_norm = norm is not None
    has_into = into is not None
    if has_norm:
        assert grid[1] == 1 and len(out_shape) == 2 and out_dtype == F32

    def flat(v):
        return v.reshape(-1, v.shape[-1]) if v.ndim == 3 else v

    def body(a_ref, b_ref, *rest):
        rest = list(rest)
        res_ref = rest.pop(0) if has_res else None
        x_ref, g_ref, dres_ref = (rest.pop(0), rest.pop(0), rest.pop(0)) if has_norm else (None, None, None)
        if has_into:
            rest.pop(0)
        o_ref = rest.pop(0)
        dg_ref = rest.pop(0) if has_norm else None
        acc = rest.pop(0)
        kk = pl.program_id(2)
        first_rows = pl.program_id(0) == 0

        @pl.when(kk == 0)
        def _():
            acc[...] = jnp.zeros_like(acc)

        if has_norm:
            @pl.when(first_rows & (kk == 0))
            def _():
                dg_ref[...] = jnp.zeros_like(dg_ref)

        acc[...] += _bdot(flat(a_ref[...]), flat(b_ref[...]), dims)

        @pl.when(kk == nk - 1)
        def _():
            r = acc[...]
            if has_res:
                r = r + res_ref[...]
            if not has_norm:
                o_ref[...] = r.reshape(o_ref.shape).astype(out_dtype)
                return
            xv = x_ref[...]
            rs = lax.rsqrt(jnp.mean(xv * xv, axis=-1, keepdims=True) + EPS)
            t = r * g_ref[...]
            c = jnp.mean(t * xv, axis=-1, keepdims=True)
            o_ref[...] = dres_ref[...] + rs * t - xv * (rs * rs * rs) * c
            dg_ref[...] += jnp.sum(r * xv * rs, axis=0, keepdims=True)

    args = [a, b] + ([res] if has_res else [])
    in_specs = [a_spec, b_spec] + ([o_spec] if has_res else [])
    out_specs, out_shapes = o_spec, jax.ShapeDtypeStruct(out_shape, out_dtype)
    if has_norm:
        vec = pl.BlockSpec((1, out_shape[1]), lambda i, j, kk: (0, 0))
        args += list(norm)
        in_specs += [o_spec, vec, o_spec]
        out_specs = [o_spec, vec]
        out_shapes = [out_shapes, jax.ShapeDtypeStruct((1, out_shape[1]), F32)]
    aliases = {}
    if has_into:
        assert not has_norm and into.dtype == out_dtype
        aliases = {len(args): 0}
        args.append(into)
        in_specs.append(pl.BlockSpec(memory_space=pl.ANY))
        out_shapes = jax.ShapeDtypeStruct(into.shape, out_dtype)
    return pl.pallas_call(
        body, name=name, grid=grid, in_specs=in_specs, out_specs=out_specs, out_shape=out_shapes,
        scratch_shapes=[pltpu.VMEM(acc_shape, F32)], input_output_aliases=aliases,
        compiler_params=_params(("arbitrary" if has_norm else "parallel", "parallel", "arbitrary")),
    )(*args)


def _shards_per_block(rows):
    return N_SHARDS if N_SHARDS * rows <= 1408 else 2


def _mm_rowsh(a, buf, rows, blk0, mode, name, res=None, out_dtype=F32, norm=None):
    s = a.shape[0]
    cols = buf.shape[2]
    g = _shards_per_block(rows)
    tm = _tile(s, ROW_TILE)
    b_blk = (g, rows, cols)
    if mode == "nn":
        return _mm_call(name, a, buf, NN, (s // tm, 1, N_SHARDS // g),
                        pl.BlockSpec((tm, g * rows), lambda i, j, kk: (i, kk)),
                        pl.BlockSpec(b_blk, lambda i, j, kk: (kk, blk0, 0)),
                        pl.BlockSpec((tm, cols), lambda i, j, kk: (i, 0)),
                        (s, cols), out_dtype, (tm, cols), res)
    return _mm_call(name, a, buf, NT, (s // tm, N_SHARDS // g, 1),
                    pl.BlockSpec((tm, cols), lambda i, j, kk: (i, 0)),
                    pl.BlockSpec(b_blk, lambda i, j, kk: (j, blk0, 0)),
                    pl.BlockSpec((tm, g * rows), lambda i, j, kk: (i, j)),
                    (s, N_SHARDS * rows), out_dtype, (tm, g * rows), res, norm)


def _mm_rowsh_dw(act, dy, rows, name, into, blk0):
    s = act.shape[0]
    cols = dy.shape[1]
    g = _shards_per_block(rows)
    ts = _tile(s, ROW_TILE)
    return _mm_call(name, act, dy, TN, (1, N_SHARDS // g, s // ts),
                    pl.BlockSpec((ts, g * rows), lambda i, j, kk: (kk, j)),
                    pl.BlockSpec((ts, cols), lambda i, j, kk: (kk, 0)),
                    pl.BlockSpec((g, rows, cols), lambda i, j, kk: (j, blk0, 0)),
                    into.shape, F32, (g * rows, cols), into=into)


def _mm_colsh(a, buf, krows, blk0, mode, name, flat=False, res=None, out_dtype=F32, norm=None):
    cols = buf.shape[2]
    b_nn = pl.BlockSpec((None, krows, cols), lambda i, j, kk: (j, blk0, 0))
    b_nt = pl.BlockSpec((None, krows, cols), lambda i, j, kk: (kk, blk0, 0))
    if mode == "nn":
        s = a.shape[0]
        tm = _tile(s, ROW_TILE)
        o_spec = (pl.BlockSpec((tm, cols), lambda i, j, kk: (i, j)) if flat
                  else pl.BlockSpec((None, tm, cols), lambda i, j, kk: (j, i, 0)))
        return _mm_call(name, a, buf, NN, (s // tm, N_SHARDS, 1),
                        pl.BlockSpec((tm, krows), lambda i, j, kk: (i, 0)), b_nn, o_spec,
                        (s, N_SHARDS * cols) if flat else (N_SHARDS, s, cols), out_dtype, (tm, cols), res)
    s = a.shape[0] if flat else a.shape[1]
    tm = _tile(s, ROW_TILE)
    a_spec = (pl.BlockSpec((tm, cols), lambda i, j, kk: (i, kk)) if flat
              else pl.BlockSpec((None, tm, cols), lambda i, j, kk: (kk, i, 0)))
    return _mm_call(name, a, buf, NT, (s // tm, 1, N_SHARDS), a_spec, b_nt,
                    pl.BlockSpec((tm, krows), lambda i, j, kk: (i, 0)),
                    (s, krows), out_dtype, (tm, krows), res, norm)


def _mm_colsh_dw(x, dy, name, flat=False, into=None, blk0=0):
    s, k = x.shape
    cols = dy.shape[1] // N_SHARDS if flat else dy.shape[2]
    ts = _tile(s, ROW_TILE)
    b_spec = (pl.BlockSpec((ts, cols), lambda i, j, kk: (kk, j)) if flat
              else pl.BlockSpec((None, ts, cols), lambda i, j, kk: (j, kk, 0)))
    return _mm_call(name, x, dy, TN, (1, N_SHARDS, s // ts),
                    pl.BlockSpec((ts, k), lambda i, j, kk: (kk, 0)), b_spec,
                    pl.BlockSpec((None, k, cols), lambda i, j, kk: (j, blk0, 0)),
                    (N_SHARDS, k, cols) if into is None else into.shape, F32, (k, cols), into=into)


def _rmsnorm_fwd(x, g):
    s, d = x.shape
    tr = _tile(s, 1024)

    def body(x_ref, g_ref, o_ref):
        xv = x_ref[...]
        r = lax.rsqrt(jnp.mean(xv * xv, axis=-1, keepdims=True) + EPS)
        o_ref[...] = (xv * r * g_ref[...]).astype(BF16)

    return pl.pallas_call(
        body, name="rmsnorm_fwd", grid=(s // tr,),
        in_specs=[pl.BlockSpec((tr, d), lambda i: (i, 0)), pl.BlockSpec((1, d), lambda i: (0, 0))],
        out_specs=pl.BlockSpec((tr, d), lambda i: (i, 0)),
        out_shape=jax.ShapeDtypeStruct((s, d), BF16),
        compiler_params=_params(("parallel",)),
    )(x, g)


def _final_loss(h, g, tgt):
    s, d = h.shape
    tr = _tile(s, 1024)

    def body(x_ref, g_ref, t_ref, loss_ref, dx_ref, dg_ref):
        @pl.when(pl.program_id(0) == 0)
        def _():
            dg_ref[...] = jnp.zeros_like(dg_ref)
            loss_ref[...] = jnp.zeros_like(loss_ref)

        xv = x_ref[...]
        r = lax.rsqrt(jnp.mean(xv * xv, axis=-1, keepdims=True) + EPS)
        xh = xv * r
        err = xh * g_ref[...] - t_ref[...]
        per_row = jnp.mean(err * err, axis=-1, keepdims=True)
        loss_ref[...] += 0.5 * jnp.sum(per_row, axis=0, keepdims=True)
        dy = err * (1.0 / d)
        t = dy * g_ref[...]
        c = jnp.mean(t * xv, axis=-1, keepdims=True)
        dx_ref[...] = r * t - xv * (r * r * r) * c
        dg_ref[...] += jnp.sum(dy * xh, axis=0, keepdims=True)

    row = pl.BlockSpec((tr, d), lambda i: (i, 0))
    vec = pl.BlockSpec((1, d), lambda i: (0, 0))
    return pl.pallas_call(
        body, name="final_loss", grid=(s // tr,),
        in_specs=[row, vec, row],
        out_specs=[pl.BlockSpec((1, LANE), lambda i: (0, 0)), row, vec],
        out_shape=[jax.ShapeDtypeStruct((1, LANE), F32), jax.ShapeDtypeStruct((s, d), F32),
                   jax.ShapeDtypeStruct((1, d), F32)],
        compiler_params=_params(("arbitrary",)),
    )(h, g, tgt)


CONV_ROWS = 512
CONV_COLS = 1408
CONV_CHUNK = 64


def _per_lane_tile(tile_body):
    def body(*refs):
        for lt in range(refs[0].shape[-1] // LANE):
            cols = slice(lt * LANE, (lt + 1) * LANE)
            tile_body(*[r.at[(slice(None),) * (len(r.shape) - 1) + (cols,)] for r in refs])
    return body


def _lagged(window, lag):
    return (pltpu.roll(window, lag, 0) if lag else window)[SUB:]


def _led(window, lead):
    n = window.shape[0] - SUB
    return (pltpu.roll(window, window.shape[0] - lead, 0) if lead else window)[:n]


def _fold_rows(v):
    return jnp.sum(v.reshape(v.shape[0] // SUB, SUB, v.shape[1]), axis=0)


def _taps(shifted, w):
    acc = None
    for k, xs in enumerate(shifted):
        term = xs * w[k:k + 1, :]
        acc = term if acc is None else acc + term
    return acc


def _conv_tiles(s, c):
    return _tile(s, CONV_ROWS), _tile(c, CONV_COLS)


def _conv_silu_fwd(pre, w):
    s, c = pre.shape
    width = w.shape[0]
    tr, tc = _conv_tiles(s, c)

    def body(x_ref, w_ref, o_ref, tail):
        @pl.when(pl.program_id(1) == 0)
        def _():
            tail[...] = jnp.zeros_like(tail)

        wv = w_ref[...]

        def do(c0, window):
            y = _taps([_lagged(window, width - 1 - k) for k in range(width)], wv)
            o_ref[pl.ds(c0, CONV_CHUNK), :] = y * _sigmoid(y)

        def chunk(ci, carry):
            c0 = pl.multiple_of(ci * CONV_CHUNK, CONV_CHUNK)
            do(c0, x_ref[pl.ds(pl.multiple_of(c0 - SUB, SUB), CONV_CHUNK + SUB), :])
            return carry

        do(0, jnp.concatenate([tail[...], x_ref[:CONV_CHUNK, :]], axis=0))
        lax.fori_loop(1, tr // CONV_CHUNK, chunk, 0)
        tail[...] = x_ref[tr - SUB:, :]

    blk = pl.BlockSpec((tr, tc), lambda j, i: (i, j))
    return pl.pallas_call(
        _per_lane_tile(body), name="conv_silu_fwd", grid=(c // tc, s // tr),
        in_specs=[blk, pl.BlockSpec((width, tc), lambda j, i: (0, j))], out_specs=blk,
        out_shape=jax.ShapeDtypeStruct((s, c), F32),
        scratch_shapes=[pltpu.VMEM((SUB, tc), F32)],
        compiler_params=_params(("parallel", "arbitrary")),
    )(pre, w)


def _prev_rows_index(i_blk, tr):
    return jnp.maximum(i_blk * (tr // SUB) - 1, 0)


def _conv_silu_bwd(pre, w, dact):
    s, c = pre.shape
    width = w.shape[0]
    tr, tc = _conv_tiles(s, c)
    nr = s // tr

    nchunks = tr // CONV_CHUNK

    def body(x_ref, p_ref, w_ref, d_ref, dx_ref, dw_ref, head):
        @pl.when(pl.program_id(1) == 0)
        def _():
            head[...] = jnp.zeros_like(head)
            dw_ref[...] = jnp.zeros_like(dw_ref)

        wv = w_ref[...]

        def do(c0, window, later, dws):
            xs = [_lagged(window, width - 1 - k) for k in range(width)]
            y = _taps(xs, wv)
            sg = _sigmoid(y)
            dy = d_ref[pl.ds(c0, CONV_CHUNK), :] * sg * (1.0 + y * (1.0 - sg))
            dws = tuple(dw + _fold_rows(dy * x) for dw, x in zip(dws, xs))
            ahead = jnp.concatenate([dy, later], axis=0)
            dx_ref[pl.ds(c0, CONV_CHUNK), :] = _taps([_led(ahead, width - 1 - k) for k in range(width)],
                                                     wv).astype(BF16)
            return dy[:SUB], dws

        def chunk(it, carry):
            c0 = pl.multiple_of((nchunks - 1 - it) * CONV_CHUNK, CONV_CHUNK)
            return do(c0, x_ref[pl.ds(pl.multiple_of(c0 - SUB, SUB), CONV_CHUNK + SUB), :], *carry)

        zero = jnp.zeros((SUB, LANE), F32)
        carry = lax.fori_loop(0, nchunks - 1, chunk, (head[...], (zero,) * width))
        before = jnp.where(pl.program_id(1) == nr - 1, 0.0, p_ref[...])
        later, dws = do(0, jnp.concatenate([before, x_ref[:CONV_CHUNK, :]], axis=0), *carry)
        head[...] = later
        for k in range(width):
            dw_ref[k:k + 1, :] += jnp.sum(dws[k], axis=0, keepdims=True)

    blk = pl.BlockSpec((tr, tc), lambda j, i: (nr - 1 - i, j))
    prev = pl.BlockSpec((SUB, tc), lambda j, i: (_prev_rows_index(nr - 1 - i, tr), j))
    wblk = pl.BlockSpec((width, tc), lambda j, i: (0, j))
    return pl.pallas_call(
        _per_lane_tile(body), name="conv_silu_bwd", grid=(c // tc, nr),
        in_specs=[blk, prev, wblk, blk], out_specs=[blk, wblk],
        out_shape=[jax.ShapeDtypeStruct((s, c), BF16), jax.ShapeDtypeStruct((width, c), F32)],
        scratch_shapes=[pltpu.VMEM((SUB, tc), F32)],
        compiler_params=_params(("parallel", "arbitrary")),
    )(pre, pre, w, dact)


def _ffn_act_fwd(pre, w, b):
    _, halves, s, c = pre.shape
    width = w.shape[2]
    tr, tc = _conv_tiles(s, c)
    ncb = c // tc

    def body(x_ref, w_ref, b_ref, o_ref, tail):
        @pl.when(pl.program_id(2) == 0)
        def _():
            tail[...] = jnp.zeros_like(tail)

        wg, wv, bg, bv = w_ref[0], w_ref[1], b_ref[0], b_ref[1]

        def do(c0, win_g, win_v):
            yg = _taps([_lagged(win_g, width - 1 - k) for k in range(width)], wg) + bg
            yv = _taps([_lagged(win_v, width - 1 - k) for k in range(width)], wv) + bv
            o_ref[pl.ds(c0, CONV_CHUNK), :] = (yg * _sigmoid(yg) * yv).astype(BF16)

        def chunk(ci, carry):
            c0 = pl.multiple_of(ci * CONV_CHUNK, CONV_CHUNK)
            rows = pl.ds(pl.multiple_of(c0 - SUB, SUB), CONV_CHUNK + SUB)
            do(c0, x_ref[0, rows, :], x_ref[1, rows, :])
            return carry

        do(0, jnp.concatenate([tail[0], x_ref[0, :CONV_CHUNK, :]], axis=0),
           jnp.concatenate([tail[1], x_ref[1, :CONV_CHUNK, :]], axis=0))
        lax.fori_loop(1, tr // CONV_CHUNK, chunk, 0)
        tail[...] = x_ref[:, tr - SUB:, :]

    return pl.pallas_call(
        _per_lane_tile(body), name="ffn_act_fwd", grid=(halves, ncb, s // tr),
        in_specs=[pl.BlockSpec((2, None, tr, tc), lambda h, j, i: (0, h, i, j)),
                  pl.BlockSpec((2, None, width, tc), lambda h, j, i: (0, h, 0, j)),
                  pl.BlockSpec((2, None, 1, tc), lambda h, j, i: (0, h, 0, j))],
        out_specs=pl.BlockSpec((tr, tc), lambda h, j, i: (i, h * ncb + j)),
        out_shape=jax.ShapeDtypeStruct((s, halves * c), BF16),
        scratch_shapes=[pltpu.VMEM((2, SUB, tc), F32)],
        compiler_params=_params(("parallel", "parallel", "arbitrary")),
    )(pre, w, b)


def _ffn_act_bwd(pre, w, b, dact):
    _, halves, s, c = pre.shape
    width = w.shape[2]
    tr, tc = _conv_tiles(s, c)
    ncb = c // tc
    nr = s // tr
    nchunks = tr // CONV_CHUNK

    def body(x_ref, p_ref, w_ref, b_ref, d_ref, dx_ref, dw_ref, db_ref, head):
        @pl.when(pl.program_id(2) == 0)
        def _():
            for r in (head, dw_ref, db_ref):
                r[...] = jnp.zeros_like(r)

        wg, wv, bg, bv = w_ref[0], w_ref[1], b_ref[0], b_ref[1]

        def do(c0, win_g, win_v, later_g, later_v, dwg, dwv, dbg, dbv):
            xg = [_lagged(win_g, width - 1 - k) for k in range(width)]
            xv = [_lagged(win_v, width - 1 - k) for k in range(width)]
            yg = _taps(xg, wg) + bg
            yv = _taps(xv, wv) + bv
            sg = _sigmoid(yg)
            da = d_ref[pl.ds(c0, CONV_CHUNK), :]
            dyv = da * yg * sg
            dyg = da * yv * sg * (1.0 + yg * (1.0 - sg))
            dwg = tuple(dw + _fold_rows(dyg * x) for dw, x in zip(dwg, xg))
            dwv = tuple(dw + _fold_rows(dyv * x) for dw, x in zip(dwv, xv))
            dbg = dbg + _fold_rows(dyg)
            dbv = dbv + _fold_rows(dyv)
            ahead_g = jnp.concatenate([dyg, later_g], axis=0)
            ahead_v = jnp.concatenate([dyv, later_v], axis=0)
            dxg = _taps([_led(ahead_g, width - 1 - k) for k in range(width)], wg)
            dxv = _taps([_led(ahead_v, width - 1 - k) for k in range(width)], wv)
            dx_ref[0, pl.ds(c0, CONV_CHUNK), :] = dxg.astype(BF16)
            dx_ref[1, pl.ds(c0, CONV_CHUNK), :] = dxv.astype(BF16)
            return dyg[:SUB], dyv[:SUB], dwg, dwv, dbg, dbv

        def chunk(it, carry):
            c0 = pl.multiple_of((nchunks - 1 - it) * CONV_CHUNK, CONV_CHUNK)
            rows = pl.ds(pl.multiple_of(c0 - SUB, SUB), CONV_CHUNK + SUB)
            return do(c0, x_ref[0, rows, :], x_ref[1, rows, :], *carry)

        zero = jnp.zeros((SUB, LANE), F32)
        carry = lax.fori_loop(0, nchunks - 1, chunk,
                              (head[0], head[1], (zero,) * width, (zero,) * width, zero, zero))
        before = jnp.where(pl.program_id(2) == nr - 1, 0.0, p_ref[...])
        later_g, later_v, dwg, dwv, dbg, dbv = do(
            0, jnp.concatenate([before[0], x_ref[0, :CONV_CHUNK, :]], axis=0),
            jnp.concatenate([before[1], x_ref[1, :CONV_CHUNK, :]], axis=0), *carry)
        head[0] = later_g
        head[1] = later_v
        db_ref[0] += jnp.sum(dbg, axis=0, keepdims=True)
        db_ref[1] += jnp.sum(dbv, axis=0, keepdims=True)
        for k in range(width):
            dw_ref[0, k:k + 1, :] += jnp.sum(dwg[k], axis=0, keepdims=True)
            dw_ref[1, k:k + 1, :] += jnp.sum(dwv[k], axis=0, keepdims=True)

    blk = pl.BlockSpec((2, None, tr, tc), lambda h, j, i: (0, h, nr - 1 - i, j))
    prev = pl.BlockSpec((2, None, SUB, tc), lambda h, j, i: (0, h, _prev_rows_index(nr - 1 - i, tr), j))
    wblk = pl.BlockSpec((2, None, width, tc), lambda h, j, i: (0, h, 0, j))
    bblk = pl.BlockSpec((2, None, 1, tc), lambda h, j, i: (0, h, 0, j))
    return pl.pallas_call(
        _per_lane_tile(body), name="ffn_act_bwd", grid=(halves, ncb, nr),
        in_specs=[blk, prev, wblk, bblk, pl.BlockSpec((tr, tc), lambda h, j, i: (nr - 1 - i, h * ncb + j))],
        out_specs=[blk, wblk, bblk],
        out_shape=[jax.ShapeDtypeStruct(pre.shape, BF16), jax.ShapeDtypeStruct(w.shape, F32),
                   jax.ShapeDtypeStruct(b.shape, F32)],
        scratch_shapes=[pltpu.VMEM((2, SUB, tc), F32)],
        compiler_params=_params(("parallel", "parallel", "arbitrary")),
    )(pre, pre, w, b, dact)


def _tri_masks():
    row = lax.broadcasted_iota(jnp.int32, (CHUNK, CHUNK), 0)
    col = lax.broadcasted_iota(jnp.int32, (CHUNK, CHUNK), 1)
    return row, col


def _tri_inv(ms, row, col):
    eye = (row == col).astype(F32)
    same_blk = (row >> 4) == (col >> 4)
    mds = [jnp.where(same_blk, m, 0.0) for m in ms]
    offs = [m - md for m, md in zip(ms, mds)]
    xs = [eye - md for md in mds]
    ps = [_bdot(md, md, NN) for md in mds]
    for _ in range(2):
        rs = [_bdot(jnp.concatenate([x, p], axis=0), p, NN) for x, p in zip(xs, ps)]
        xs = [x + r[:CHUNK] for x, r in zip(xs, rs)]
        ps = [r[CHUNK:] for r in rs]
    xs = [x + _bdot(x, p, NN) for x, p in zip(xs, ps)]
    ps = [_bdot(x, off, NN) for x, off in zip(xs, offs)]
    pps = [_bdot(p, p, NN) for p in ps]
    ys = [eye - p for p in ps]
    ys = [y + _bdot(y, pp, NN) for y, pp in zip(ys, pps)]
    return [_bdot(y, x, NN) for y, x in zip(ys, xs)]


def _gdn_gates(ba, alog, dtb, row, col):
    sig = _sigmoid(ba)
    neg_a = -jnp.exp(alog)
    g = neg_a * _softplus(ba + dtb)
    lower = (row >= col).astype(F32)
    gcum = _dot(lower, g, NN, HIGHEST)
    return sig, neg_a, g, gcum


def _gdn_head_common(q_raw, k_raw, v, beta, gc, gr, row, col):
    causal = row >= col
    strict = row > col
    rq = lax.rsqrt(jnp.sum(q_raw * q_raw, axis=-1, keepdims=True) + EPS)
    rk = lax.rsqrt(jnp.sum(k_raw * k_raw, axis=-1, keepdims=True) + EPS)
    q = q_raw * (rq * (A_HEAD ** -0.5))
    k = k_raw * rk
    decay = jnp.where(causal, jnp.exp(jnp.where(causal, gc - gr, 0.0)), 0.0)
    eg = jnp.exp(gc)
    gl = gc[CHUNK - 1:CHUNK, :]
    ekl = jnp.exp(gl - gc)
    dec = jnp.exp(gl)
    kb = k * beta
    kbq = jnp.concatenate([kb, q], axis=0)
    both = _bdot(kbq, k, NT)
    kk, qk = both[:CHUNK], both[CHUNK:]
    a = jnp.where(causal, qk * decay, 0.0)
    return dict(rq=rq, rk=rk, q=q, k=k, decay=decay, eg=eg, ekl=ekl, dec=dec, kb=kb, kbq=kbq, kk=kk, qk=qk, a=a,
                vb=v * beta, kbg=kb * eg, qd=q * eg, ke=k * ekl, causal=causal, strict=strict)


def _gdn_fwd(qkv, ba, z, alog, dtb, wn):
    s = qkv.shape[0]
    nc = s // CHUNK

    def body(qkv_ref, ba_ref, z_ref, alog_ref, dtb_ref, wn_ref, y_ref, o_ref, st_ref, t_ref, w_ref, vn_ref, state):
        @pl.when(pl.program_id(0) == 0)
        def _():
            state[...] = jnp.zeros_like(state)

        row, col = _tri_masks()
        sig, _, _, gcum = _gdn_gates(ba_ref[...], alog_ref[...], dtb_ref[...], row, col)
        gt = gcum.T
        heads = range(A_HEADS)
        lanes = [slice(h * A_HEAD, (h + 1) * A_HEAD) for h in heads]
        fs = [_gdn_head_common(qkv_ref[:, lanes[h]], qkv_ref[:, A_QK + h * A_HEAD:A_QK + (h + 1) * A_HEAD],
                               qkv_ref[:, 2 * A_QK + h * A_HEAD:2 * A_QK + (h + 1) * A_HEAD],
                               sig[:, h:h + 1], gcum[:, 8 + h:9 + h], gt[8 + h:9 + h, :], row, col) for h in heads]
        ts = [t.astype(BF16) for t in
              _tri_inv([jnp.where(f["strict"], f["kk"] * f["decay"], 0.0) for f in fs], row, col)]
        uws = [_bdot(t, jnp.concatenate([f["vb"], f["kbg"]], axis=1), NN) for t, f in zip(ts, fs)]
        s0s = [state[h] for h in heads]
        ws_ = [uw[:, A_HEAD:].astype(BF16) for uw in uws]
        wss = [_bdot(jnp.concatenate([w, f["qd"].astype(BF16)], axis=0), s0, NN) for w, f, s0 in zip(ws_, fs, s0s)]
        vnews = [(uw[:, :A_HEAD] - wsq[:CHUNK]).astype(BF16) for uw, wsq in zip(uws, wss)]
        os_ = [wsq[CHUNK:] + _bdot(f["a"], vn, NN) for wsq, f, vn in zip(wss, fs, vnews)]
        s1s = [s0 * f["dec"] + _bdot(f["ke"], vn, TN) for s0, f, vn in zip(s0s, fs, vnews)]
        for h in heads:
            ln = lanes[h]
            st_ref[0, h] = s0s[h]
            t_ref[0, h] = ts[h]
            w_ref[:, ln] = ws_[h]
            vn_ref[:, ln] = vnews[h]
            state[h] = s1s[h]
            o = os_[h]
            o_ref[:, ln] = o
            r = lax.rsqrt(jnp.mean(o * o, axis=-1, keepdims=True) + EPS)
            zz = z_ref[:, ln]
            y_ref[:, ln] = (o * r * wn_ref[...] * zz * _sigmoid(zz)).astype(BF16)

    vec = pl.BlockSpec((1, LANE), lambda n: (0, 0))
    wide = pl.BlockSpec((CHUNK, A_QK), lambda n: (n, 0))
    return pl.pallas_call(
        body, name="gdn_fwd", grid=(nc,),
        in_specs=[pl.BlockSpec((CHUNK, A_CONV_WIDTH), lambda n: (n, 0)),
                  pl.BlockSpec((CHUNK, LANE), lambda n: (n, 0)), wide, vec, vec, vec],
        out_specs=[wide, wide, pl.BlockSpec((1, A_HEADS, A_HEAD, A_HEAD), lambda n: (n, 0, 0, 0)),
                   pl.BlockSpec((1, A_HEADS, CHUNK, CHUNK), lambda n: (n, 0, 0, 0)), wide, wide],
        out_shape=[jax.ShapeDtypeStruct((s, A_QK), BF16), jax.ShapeDtypeStruct((s, A_QK), F32),
                   jax.ShapeDtypeStruct((nc, A_HEADS, A_HEAD, A_HEAD), F32),
                   jax.ShapeDtypeStruct((nc, A_HEADS, CHUNK, CHUNK), BF16),
                   jax.ShapeDtypeStruct((s, A_QK), BF16), jax.ShapeDtypeStruct((s, A_QK), BF16)],
        scratch_shapes=[pltpu.VMEM((A_HEADS, A_HEAD, A_HEAD), F32)],
        compiler_params=_params(("arbitrary",)),
    )(qkv, ba, z, alog, dtb, wn)


def _gdn_bwd(qkv, ba, z, o_raw, dy, states, t_all, w_all, vn_all, alog, dtb, wn):
    s = qkv.shape[0]
    nc = s // CHUNK

    def body(qkv_ref, ba_ref, z_ref, o_ref, dy_ref, st_ref, t_ref, w_ref, vn_ref, alog_ref, dtb_ref, wn_ref,
             dqkv_ref, dba_ref, dz_ref, dalog_ref, ddtb_ref, dwn_ref, dstate):
        @pl.when(pl.program_id(0) == 0)
        def _():
            for r in (dstate, dalog_ref, ddtb_ref, dwn_ref):
                r[...] = jnp.zeros_like(r)

        row, col = _tri_masks()
        bat = ba_ref[...]
        sig, neg_a, g, gcum = _gdn_gates(bat, alog_ref[...], dtb_ref[...], row, col)
        gt = gcum.T
        lane = lax.broadcasted_iota(jnp.int32, (CHUNK, LANE), 1)
        ones = jnp.ones((CHUNK, LANE), F32)
        last_row = lax.broadcasted_iota(jnp.int32, (CHUNK, 1), 0) == CHUNK - 1
        wnv = wn_ref[...]
        dgc_tile = jnp.zeros((CHUNK, LANE), F32)
        dbeta_tile = jnp.zeros((CHUNK, LANE), F32)
        dwn_acc = jnp.zeros((1, LANE), F32)
        hs = []
        for h in range(A_HEADS):
            ln = slice(h * A_HEAD, (h + 1) * A_HEAD)
            lk = slice(A_QK + h * A_HEAD, A_QK + (h + 1) * A_HEAD)
            lv = slice(2 * A_QK + h * A_HEAD, 2 * A_QK + (h + 1) * A_HEAD)
            q_raw, k_raw, v = qkv_ref[:, ln], qkv_ref[:, lk], qkv_ref[:, lv]
            f = _gdn_head_common(q_raw, k_raw, v, sig[:, h:h + 1], gcum[:, 8 + h:9 + h], gt[8 + h:9 + h, :], row, col)
            f.update(h=h, ln=ln, lk=lk, lv=lv, q_raw=q_raw, k_raw=k_raw, v=v, beta=sig[:, h:h + 1],
                     s0=st_ref[0, h], ds1=dstate[h], t=t_ref[0, h], w=w_ref[:, ln], vnew=vn_ref[:, ln])
            o = o_ref[:, ln]
            zz = z_ref[:, ln]
            dyv = dy_ref[:, ln]
            r = lax.rsqrt(jnp.mean(o * o, axis=-1, keepdims=True) + EPS)
            sz = _sigmoid(zz)
            silu = zz * sz
            dz_ref[:, ln] = (dyv * o * r * wnv * sz * (1.0 + zz * (1.0 - sz))).astype(BF16)
            dwn_acc = dwn_acc + jnp.sum(dyv * silu * o * r, axis=0, keepdims=True)
            tt = dyv * silu * wnv
            do = r * tt - o * (r * r * r) * jnp.mean(tt * o, axis=-1, keepdims=True)
            f["do_b"] = do.astype(BF16)
            hs.append(f)
        for f in hs:
            f["dvnew"] = _bdot(f["a"], f["do_b"], TN) + _bdot(f["ke"], f["ds1"], NN)
            f["da"] = jnp.where(f["causal"], _bdot(f["do_b"], f["vnew"], NT), 0.0)
            f["dke"] = _bdot(f["vnew"], f["ds1"], NT)
            f["ddec"] = jnp.sum(jnp.sum(f["s0"] * f["ds1"], axis=1, keepdims=True), axis=0, keepdims=True)
        for f in hs:
            do_dv = jnp.concatenate([f["do_b"], f["dvnew"].astype(BF16)], axis=0)
            both = _bdot(do_dv, f["s0"], NT)
            f["dqd"], f["dw"] = both[:CHUNK], -both[CHUNK:]
            qd_w = jnp.concatenate([f["qd"].astype(BF16), -f["w"]], axis=0)
            dstate[f["h"]] = _bdot(qd_w, do_dv, TN) + f["dec"] * f["ds1"]
        for f in hs:
            dd = jnp.concatenate([f["dvnew"], f["dw"]], axis=1).astype(BF16)
            tdd = _bdot(f["t"], dd, TN)
            f["dvb"], f["dkbg"] = tdd[:, :A_HEAD], tdd[:, A_HEAD:]
            f["dt"] = _bdot(dd, jnp.concatenate([f["vb"], f["kbg"]], axis=1), NT)
        for f in hs:
            f["tdt"] = _bdot(f["t"], f["dt"], TN)
        for f in hs:
            dm = jnp.where(f["strict"], -_bdot(f["tdt"], f["t"], NT), 0.0)
            f["ddecay"] = (dm * f["kk"] + f["da"] * f["qk"]) * f["decay"]
            f["dboth"] = jnp.concatenate([dm * f["decay"], f["da"] * f["decay"]], axis=0).astype(BF16)
        for f in hs:
            f["r2"] = _bdot(f["dboth"], f["k"], NN)
            f["dk0"] = _bdot(f["dboth"], f["kbq"], TN)
        for f in hs:
            h, k, beta = f["h"], f["k"], f["beta"]
            dkb = f["r2"][:CHUNK] + f["dkbg"] * f["eg"]
            dq = f["r2"][CHUNK:] + f["dqd"] * f["eg"]
            dk = f["dk0"] + f["dke"] * f["ekl"] + dkb * beta
            dke_ke = jnp.sum(f["dke"] * f["ke"], axis=-1, keepdims=True)
            dgc = (jnp.sum(f["ddecay"], axis=-1, keepdims=True)
                   + jnp.sum(f["dqd"] * f["qd"], axis=-1, keepdims=True) - dke_ke
                   + jnp.sum(f["dkbg"] * f["kbg"], axis=-1, keepdims=True))
            dgl = jnp.sum(dke_ke, axis=0, keepdims=True) + f["ddec"] * f["dec"]
            dgc = dgc + jnp.where(last_row, dgl, 0.0)
            dbeta = jnp.sum(dkb * k, axis=-1, keepdims=True) + jnp.sum(f["dvb"] * f["v"], axis=-1, keepdims=True)
            dgc_tile = dgc_tile + jnp.where(lane == 8 + h, dgc, 0.0)
            dbeta_tile = dbeta_tile + jnp.where(lane == h, dbeta, 0.0)
            dqn = dq * (A_HEAD ** -0.5)
            rq, rk, q_raw, k_raw = f["rq"], f["rk"], f["q_raw"], f["k_raw"]
            dqkv_ref[:, f["ln"]] = rq * dqn - q_raw * (rq * rq * rq) * jnp.sum(dqn * q_raw, axis=-1, keepdims=True)
            dqkv_ref[:, f["lk"]] = rk * dk - k_raw * (rk * rk * rk) * jnp.sum(dk * k_raw, axis=-1, keepdims=True)
            dqkv_ref[:, f["lv"]] = f["dvb"] * beta
        ddecays = [f["ddecay"] for f in hs]
        col_sums = _dot(jnp.concatenate(ddecays, axis=1), ones, TN, HIGHEST)
        for h in range(A_HEADS):
            dgc_tile = dgc_tile - jnp.where(lane == 8 + h, col_sums[h * CHUNK:(h + 1) * CHUNK, :1], 0.0)
        upper = (row <= col).astype(F32)
        dg = _dot(upper, dgc_tile, NN, HIGHEST)
        da_raw = dg * neg_a * _sigmoid(bat + dtb_ref[...])
        dba_ref[...] = jnp.where(lane < 8, dbeta_tile * sig * (1.0 - sig),
                                 jnp.where(lane < 16, da_raw, 0.0)).astype(BF16)
        dwn_ref[...] += dwn_acc
        ddtb_ref[...] += jnp.sum(da_raw, axis=0, keepdims=True)
        dalog_ref[...] += jnp.sum(dg * g, axis=0, keepdims=True)

    rev = lambda n: (nc - 1 - n, 0)
    vec = pl.BlockSpec((1, LANE), lambda n: (0, 0))
    wide = pl.BlockSpec((CHUNK, A_QK), rev)
    qkv_blk = pl.BlockSpec((CHUNK, A_CONV_WIDTH), rev)
    ba_blk = pl.BlockSpec((CHUNK, LANE), rev)
    vsh = jax.ShapeDtypeStruct((1, LANE), F32)
    return pl.pallas_call(
        body, name="gdn_bwd", grid=(nc,),
        in_specs=[qkv_blk, ba_blk, wide, wide, wide,
                  pl.BlockSpec((1, A_HEADS, A_HEAD, A_HEAD), lambda n: (nc - 1 - n, 0, 0, 0)),
                  pl.BlockSpec((1, A_HEADS, CHUNK, CHUNK), lambda n: (nc - 1 - n, 0, 0, 0)), wide, wide,
                  vec, vec, vec],
        out_specs=[qkv_blk, ba_blk, wide, vec, vec, vec],
        out_shape=[jax.ShapeDtypeStruct((s, A_CONV_WIDTH), F32), jax.ShapeDtypeStruct((s, LANE), BF16),
                   jax.ShapeDtypeStruct((s, A_QK), BF16), vsh, vsh, vsh],
        scratch_shapes=[pltpu.VMEM((A_HEADS, A_HEAD, A_HEAD), F32)],
        compiler_params=_params(("arbitrary",)),
    )(qkv, ba, z, o_raw, dy, states, t_all, w_all, vn_all, alog, dtb, wn)


REL_RING = 1024
QBLK_BITS = 8


def _rel_ring_onehot():
    m = lax.broadcasted_iota(jnp.int32, (REL_RING, REL_PAD), 0)
    t = lax.broadcasted_iota(jnp.int32, (REL_RING, REL_PAD), 1)
    u = jnp.where(m < KBLK, m, m - REL_RING)
    idx = jnp.clip(LEFT - u, -REL_CLIP, REL_CLIP) + REL_CLIP
    return (t == idx).astype(F32)


def _relbias_ring(table, transpose):
    n_in, n_out = (REL_RING, REL_PAD) if transpose else (REL_PAD, REL_RING)

    def body(t_ref, o_ref):
        o_ref[...] = _dot(t_ref[...], _rel_ring_onehot(), NN if transpose else NT, HIGHEST)

    return pl.pallas_call(
        body, name="relbias_ring_bwd" if transpose else "relbias_ring",
        out_shape=jax.ShapeDtypeStruct((B_HEADS, n_out), F32),
        in_specs=[VMEM_FULL], out_specs=VMEM_FULL,
        compiler_params=_params(),
    )(table)


def _row_bit(shape, bit):
    return ((lax.broadcasted_iota(jnp.int32, shape, 0) >> bit) & 1) == 1


def _relbias_expand(ring):
    def body(r_ref, o_ref):
        b = jnp.broadcast_to(r_ref[0], (QBLK, REL_RING))
        for bit in range(QBLK_BITS):
            b = jnp.where(_row_bit(b.shape, bit), pltpu.roll(b, 1 << bit, 1), b)
        j = lax.broadcasted_iota(jnp.int32, (QBLK, KBLK), 1)
        r = lax.broadcasted_iota(jnp.int32, (QBLK, KBLK), 0)
        lo = (r >> 6) << 6
        o_ref[0] = jnp.where((j >= lo) & (j < lo + LEFT + CHUNK), b[:, :KBLK], NEG_INF)

    return pl.pallas_call(
        body, name="relbias_expand", grid=(B_HEADS,),
        in_specs=[pl.BlockSpec((1, 1, REL_RING), lambda h: (h, 0, 0))],
        out_specs=pl.BlockSpec((1, QBLK, KBLK), lambda h: (h, 0, 0)),
        out_shape=jax.ShapeDtypeStruct((B_HEADS, QBLK, KBLK), F32),
        compiler_params=_params(("parallel",)),
    )(ring)


def _relbias_reduce(ds):
    def body(d_ref, o_ref):
        d = jnp.concatenate([d_ref[0], jnp.zeros((QBLK, REL_RING - KBLK), F32)], axis=1)
        for bit in range(QBLK_BITS):
            d = jnp.where(_row_bit(d.shape, bit), pltpu.roll(d, REL_RING - (1 << bit), 1), d)
        o_ref[0] = jnp.sum(d, axis=0, keepdims=True)

    return pl.pallas_call(
        body, name="relbias_reduce", grid=(B_HEADS,),
        in_specs=[pl.BlockSpec((1, QBLK, KBLK), lambda h: (h, 0, 0))],
        out_specs=pl.BlockSpec((1, 1, REL_RING), lambda h: (h, 0, 0)),
        out_shape=jax.ShapeDtypeStruct((B_HEADS, 1, REL_RING), F32),
        compiler_params=_params(("parallel",)),
    )(ds)


def _attn_probs(q_ref, kb, b_ref, hh, q0):
    hl = slice(hh * B_HEAD, (hh + 1) * B_HEAD)
    qh = q_ref[:, hl] * (B_HEAD ** -0.5)
    kh = kb[:, hl]
    jpos = lax.broadcasted_iota(jnp.int32, (QBLK, KBLK), 1)
    sc = _bdot(qh, kh, NT) + b_ref[hh]
    sc = jnp.where(jpos + q0 >= LEFT, sc, NEG_INF)
    mx = jnp.max(sc, axis=-1, keepdims=True)
    p = jnp.exp(sc - mx)
    return p * (1.0 / jnp.sum(p, axis=-1, keepdims=True)), qh, kh


def _attn_fwd(q, kpad, vpad, bias):
    s = q.shape[0]

    def body(q_ref, k_ref, v_ref, b_ref, o_ref):
        q0 = pl.multiple_of(pl.program_id(1) * QBLK, QBLK)
        kb = k_ref[pl.ds(q0, KBLK), :]
        vb = v_ref[pl.ds(q0, KBLK), :]
        outs = []
        for hh in range(2):
            p, _, _ = _attn_probs(q_ref, kb, b_ref, hh, q0)
            outs.append(_bdot(p, vb[:, hh * B_HEAD:(hh + 1) * B_HEAD], NN))
        o_ref[...] = jnp.concatenate(outs, axis=1).astype(BF16)

    qblk = pl.BlockSpec((QBLK, LANE), lambda g, m: (m, g))
    kblk = pl.BlockSpec((LEFT + s, LANE), lambda g, m: (0, g))
    return pl.pallas_call(
        body, name="attn_fwd", grid=(B_HEADS // 2, s // QBLK),
        in_specs=[qblk, kblk, kblk, pl.BlockSpec((2, QBLK, KBLK), lambda g, m: (g, 0, 0))],
        out_specs=qblk,
        out_shape=jax.ShapeDtypeStruct((s, D_MODEL), BF16),
        compiler_params=_params(("parallel", "arbitrary")),
    )(q, kpad, vpad, bias)


def _attn_bwd(q, kpad, vpad, bias, do):
    s = q.shape[0]

    def body(q_ref, k_ref, v_ref, b_ref, do_ref, dq_ref, dk_ref, dv_ref, db_ref):
        @pl.when(pl.program_id(1) == 0)
        def _():
            for r in (dk_ref, dv_ref, db_ref):
                r[...] = jnp.zeros_like(r)

        q0 = pl.multiple_of(pl.program_id(1) * QBLK, QBLK)
        kb = k_ref[pl.ds(q0, KBLK), :]
        vb = v_ref[pl.ds(q0, KBLK), :]
        dqs, dks, dvs = [], [], []
        for hh in range(2):
            hl = slice(hh * B_HEAD, (hh + 1) * B_HEAD)
            p, qh, kh = _attn_probs(q_ref, kb, b_ref, hh, q0)
            doh = do_ref[:, hl]
            dp = _bdot(doh, vb[:, hl], NT)
            dsc = p * (dp - jnp.sum(p * dp, axis=-1, keepdims=True))
            db_ref[hh] += dsc
            dqs.append(_bdot(dsc, kh, NN) * (B_HEAD ** -0.5))
            dks.append(_bdot(dsc, qh, TN))
            dvs.append(_bdot(p, doh, TN))
        dq_ref[...] = jnp.concatenate(dqs, axis=1).astype(BF16)
        dk_ref[pl.ds(q0, KBLK), :] += jnp.concatenate(dks, axis=1)
        dv_ref[pl.ds(q0, KBLK), :] += jnp.concatenate(dvs, axis=1)

    qblk = pl.BlockSpec((QBLK, LANE), lambda g, m: (m, g))
    kblk = pl.BlockSpec((LEFT + s, LANE), lambda g, m: (0, g))
    bblk = pl.BlockSpec((2, QBLK, KBLK), lambda g, m: (g, 0, 0))
    return pl.pallas_call(
        body, name="attn_bwd", grid=(B_HEADS // 2, s // QBLK),
        in_specs=[qblk, kblk, kblk, bblk, qblk],
        out_specs=[qblk, kblk, kblk, bblk],
        out_shape=[jax.ShapeDtypeStruct((s, D_MODEL), BF16), jax.ShapeDtypeStruct((LEFT + s, D_MODEL), F32),
                   jax.ShapeDtypeStruct((LEFT + s, D_MODEL), F32),
                   jax.ShapeDtypeStruct((B_HEADS, QBLK, KBLK), F32)],
        compiler_params=_params(("parallel", "arbitrary")),
    )(q, kpad, vpad, bias, do)


def _adamw(w, g, m, v):
    r, c = w.shape
    tr = r
    for cand in (512, 256, 128, 64, 32, 16, 8):
        if r % cand == 0 and cand * c * 4 <= 2 * 1024 * 1024:
            tr = cand
            break
    c1 = 1.0 / (1.0 - ADAM_B1 ** ADAM_STEP)
    c2 = 1.0 / (1.0 - ADAM_B2 ** ADAM_STEP)

    def body(w_ref, g_ref, m_ref, v_ref, d_ref, mo_ref, vo_ref):
        gv = g_ref[...]
        mn = ADAM_B1 * m_ref[...] + (1.0 - ADAM_B1) * gv
        vn = ADAM_B2 * v_ref[...] + (1.0 - ADAM_B2) * (gv * gv)
        mo_ref[...] = mn
        vo_ref[...] = vn
        d_ref[...] = -ADAM_LR * ((mn * c1) / (jnp.sqrt(vn * c2) + ADAM_EPS) + ADAM_WD * w_ref[...])

    blk = pl.BlockSpec((tr, c), lambda i: (i, 0))
    sh = jax.ShapeDtypeStruct((r, c), F32)
    return pl.pallas_call(
        body, name="adamw", grid=(r // tr,),
        in_specs=[blk] * 4, out_specs=[blk] * 3, out_shape=[sh] * 3,
        compiler_params=_params(("parallel",)),
    )(w, g, m, v)


def _row(v, width=None):
    v = v.reshape(1, -1)
    if width is not None and v.shape[1] < width:
        v = jnp.pad(v, ((0, 0), (0, width - v.shape[1])))
    return v


def _gate_row(v):
    return jnp.pad(v.reshape(1, A_HEADS), ((0, 0), (A_HEADS, LANE - 2 * A_HEADS)))


DEPTH = 4
N_A = 2
N_B = 2
F_DOWN_ROWS = FFN_DIM // N_SHARDS
SQ_ROWS = D_MODEL // N_SHARDS
GD_A_OUT0 = DEPTH * F_DOWN_ROWS // SQ_ROWS
GD_B_Q0 = GD_A_OUT0 + N_A
GD_B_OUT0 = GD_B_Q0 + N_B
UP_COLS = 2 * FFN_DIM // N_SHARDS


def _a_layer_fwd(h, w, i):
    xn = _rmsnorm_fwd(h, _row(w["a_norm"][i]))
    w_qkv, w_z, w_ba = w["a_in"][i]
    pre = _mm(xn, w_qkv, name="a_qkv")
    z = _mm(xn, w_z, name="a_z")
    ba = _mm(xn, w_ba, name="a_ba")
    act = _conv_silu_fwd(pre, w["a_conv"][i])
    alog, dtb, wn = _gate_row(w["a_A_log"][i]), _gate_row(w["a_dt_bias"][i]), _row(w["a_out_norm"][i])
    y, o_raw, states, t_all, w_all, vn_all = _gdn_fwd(act, ba, z, alog, dtb, wn)
    h2 = _mm_rowsh(y, w["GD"], SQ_ROWS, GD_A_OUT0 + i, "nn", "a_out", res=h)
    saved = dict(h=h, xn=xn, pre=pre, z=z, ba=ba, act=act, o_raw=o_raw, y=y, states=states,
                 t_all=t_all, w_all=w_all, vn_all=vn_all, alog=alog, dtb=dtb, wn=wn)
    return h2, saved


def _a_layer_bwd(dh2, w, i, sv, acc):
    g = {}
    acc["GD"] = _mm_rowsh_dw(sv["y"], dh2, SQ_ROWS, "a_out_dw", acc["GD"], GD_A_OUT0 + i)
    dy = _mm_rowsh(dh2, w["GD"], SQ_ROWS, GD_A_OUT0 + i, "nt", "a_out_dx")
    dact, dba, dz, dalog, ddtb, dwn = _gdn_bwd(sv["act"], sv["ba"], sv["z"], sv["o_raw"], dy, sv["states"],
                                               sv["t_all"], sv["w_all"], sv["vn_all"],
                                               sv["alog"], sv["dtb"], sv["wn"])
    dpre, g["conv"] = _conv_silu_bwd(sv["pre"], w["a_conv"][i], dact)
    xn = sv["xn"]
    w_qkv, w_z, w_ba = w["a_in"][i]
    d_in = jnp.concatenate([_mm(xn, dpre, "tn", name="a_qkv_dw"), _mm(xn, dz, "tn", name="a_z_dw"),
                            _mm(xn, dba, "tn", name="a_ba_dw")[:, :2 * A_HEADS]], axis=1)
    g["w_in"] = jnp.transpose(d_in.reshape(D_MODEL, N_SHARDS, -1), (1, 0, 2))
    dxn = _mm(dpre, w_qkv, "nt", name="a_qkv_dx")
    dxn = _mm(dz, w_z, "nt", res=dxn, name="a_z_dx")
    dh, dnorm = _mm(dba, w_ba, "nt", res=dxn, name="a_ba_dx", norm=(sv["h"], _row(w["a_norm"][i]), dh2))
    g["norm"] = dnorm[0]
    g["A_log"] = dalog[0, A_HEADS:2 * A_HEADS]
    g["dt_bias"] = ddtb[0, A_HEADS:2 * A_HEADS]
    g["out_norm"] = dwn[0]
    return dh, g


def _by_half(a, lead):
    return jnp.moveaxis(a.reshape(a.shape[:-1] + (2, 2, UP_COLS)), (-3, -2), (0, 1)).reshape((2, 2) + lead + (UP_COLS,))


def _from_half(a):
    lead = a.shape[2:-1]
    return jnp.moveaxis(a, (0, 1), (-3, -2)).reshape(lead + (N_SHARDS * UP_COLS,))


def _ffn_fwd(h, w, l):
    s = h.shape[0]
    xn = _rmsnorm_fwd(h, _row(w["f_norm"][l]))
    cw = _by_half(w["f_conv"][l], (w["f_conv"].shape[1],))
    cb = _by_half(w["f_conv_b"][l][None], (1,))
    pre = _mm_colsh(xn, w["GU"], D_MODEL, l, "nn", "f_up").reshape(2, 2, s, UP_COLS)
    act = _ffn_act_fwd(pre, cw, cb)
    h2 = _mm_rowsh(act, w["GD"], F_DOWN_ROWS, l, "nn", "f_down", res=h)
    return h2, dict(h=h, xn=xn, pre=pre, act=act, cw=cw, cb=cb)


def _ffn_bwd(dh2, w, l, sv, acc):
    g = {}
    s = dh2.shape[0]
    acc["GD"] = _mm_rowsh_dw(sv["act"], dh2, F_DOWN_ROWS, "f_down_dw", acc["GD"], l)
    dact = _mm_rowsh(dh2, w["GD"], F_DOWN_ROWS, l, "nt", "f_down_dx")
    dpre, dcw, dcb = _ffn_act_bwd(sv["pre"], sv["cw"], sv["cb"], dact)
    dpre = dpre.reshape(N_SHARDS, s, UP_COLS)
    acc["GU"] = _mm_colsh_dw(sv["xn"], dpre, "f_up_dw", into=acc["GU"], blk0=l)
    g["conv"] = _from_half(dcw)
    g["conv_b"] = _from_half(dcb)[0]
    dh, dnorm = _mm_colsh(dpre, w["GU"], D_MODEL, l, "nt", "f_up_dx", norm=(sv["h"], _row(w["f_norm"][l]), dh2))
    g["norm"] = dnorm[0]
    return dh, g


def _b_layer_fwd(h, w, j, kpad, vpad):
    xn = _rmsnorm_fwd(h, _row(w["b_norm"][j]))
    q = _mm_rowsh(xn, w["GD"], SQ_ROWS, GD_B_Q0 + j, "nn", "b_q", out_dtype=BF16)
    rel = w["b_rel_bias"][j]
    table = jnp.pad(rel, ((0, 0), (0, REL_PAD - rel.shape[1])))
    bias = _relbias_expand(_relbias_ring(table, False).reshape(B_HEADS, 1, REL_RING))
    o = _attn_fwd(q, kpad, vpad, bias)
    h2 = _mm_rowsh(o, w["GD"], SQ_ROWS, GD_B_OUT0 + j, "nn", "b_out", res=h)
    return h2, dict(h=h, xn=xn, q=q, o=o, bias=bias)


def _b_layer_bwd(dh2, w, j, sv, kpad, vpad, acc):
    g = {}
    acc["GD"] = _mm_rowsh_dw(sv["o"], dh2, SQ_ROWS, "b_out_dw", acc["GD"], GD_B_OUT0 + j)
    do = _mm_rowsh(dh2, w["GD"], SQ_ROWS, GD_B_OUT0 + j, "nt", "b_out_dx", out_dtype=BF16)
    dq, dkp, dvp, dsc = _attn_bwd(sv["q"], kpad, vpad, sv["bias"], do)
    dring = _relbias_reduce(dsc).reshape(B_HEADS, REL_RING)
    g["rel_bias"] = _relbias_ring(dring, True)[:, :2 * REL_CLIP + 1]
    acc["GD"] = _mm_rowsh_dw(sv["xn"], dq, SQ_ROWS, "b_q_dw", acc["GD"], GD_B_Q0 + j)
    dh, dnorm = _mm_rowsh(dq, w["GD"], SQ_ROWS, GD_B_Q0 + j, "nt", "b_q_dx",
                          norm=(sv["h"], _row(w["b_norm"][j]), dh2))
    g["norm"] = dnorm[0]
    return dh, g, dkp, dvp


def _local_step(x, tgt, w):
    h = x
    saved = []
    kv_saved = None
    kpad = vpad = None
    for layer in range(DEPTH):
        if layer < N_A:
            h, sm = _a_layer_fwd(h, w, layer)
        else:
            if layer == N_A:
                xn_kv = _rmsnorm_fwd(h, _row(w["kv_norm"]))
                kv = _mm_colsh(xn_kv, w["GK"], D_MODEL, 0, "nn", "kv", flat=True, out_dtype=BF16)
                kpad = jnp.pad(kv[:, :D_MODEL], ((LEFT, 0), (0, 0)))
                vpad = jnp.pad(kv[:, D_MODEL:], ((LEFT, 0), (0, 0)))
                kv_saved = dict(h=h, xn=xn_kv)
            h, sm = _b_layer_fwd(h, w, layer - N_A, kpad, vpad)
        h, sf = _ffn_fwd(h, w, layer)
        saved.append((sm, sf))

    loss, dh, dfinal = _final_loss(h, _row(w["final_norm"]), tgt)

    ga = [None] * N_A
    gb = [None] * N_B
    gf = [None] * DEPTH
    dk_tot = dv_tot = None
    g_kv = g_kvn = None
    acc = dict(GU=lax.empty((N_SHARDS, DEPTH * D_MODEL, UP_COLS), F32),
               GD=lax.empty((N_SHARDS, (GD_B_OUT0 + N_B) * SQ_ROWS, D_MODEL), F32))
    for layer in reversed(range(DEPTH)):
        sm, sf = saved[layer]
        dh, gf[layer] = _ffn_bwd(dh, w, layer, sf, acc)
        if layer >= N_A:
            dh, gb[layer - N_A], dkp, dvp = _b_layer_bwd(dh, w, layer - N_A, sm, kpad, vpad, acc)
            dk_tot = dkp if dk_tot is None else dk_tot + dkp
            dv_tot = dvp if dv_tot is None else dv_tot + dvp
            if layer == N_A:
                dkv = jnp.concatenate([dk_tot[LEFT:], dv_tot[LEFT:]], axis=1).astype(BF16)
                g_kv = _mm_colsh_dw(kv_saved["xn"], dkv, "kv_dw", flat=True)
                dh, g_kvn = _mm_colsh(dkv, w["GK"], D_MODEL, 0, "nt", "kv_dx", flat=True,
                                      norm=(kv_saved["h"], _row(w["kv_norm"]), dh))
        else:
            dh, ga[layer] = _a_layer_bwd(dh, w, layer, sm, acc)

    def stack(gs, key):
        return jnp.stack([g[key] for g in gs])

    grads = dict(
        GU=acc["GU"], GD=acc["GD"], GK=g_kv,
        GI=jnp.concatenate([g["w_in"] for g in ga], axis=1),
        a_norm=stack(ga, "norm"), a_conv=stack(ga, "conv"), a_A_log=stack(ga, "A_log"),
        a_dt_bias=stack(ga, "dt_bias"), a_out_norm=stack(ga, "out_norm"), kv_norm=g_kvn[0],
        b_norm=stack(gb, "norm"), b_rel_bias=stack(gb, "rel_bias"),
        f_norm=stack(gf, "norm"), f_conv=stack(gf, "conv"), f_conv_b=stack(gf, "conv_b"), final_norm=dfinal[0])
    return loss, dh, grads


HBM_SPEC = pl.BlockSpec(memory_space=pl.ANY)
VMEM_SPEC = pl.BlockSpec(memory_space=pltpu.VMEM)


def _place():
    x, y, c = lax.axis_index("x"), lax.axis_index("y"), lax.axis_index("c")
    chips = [(1 - x, y), (x, 1 - y), (1 - x, 1 - y)]
    return x, y, c, chips


def _remote(src, dst, send_sem, recv_sem, to):
    return pltpu.make_async_remote_copy(src_ref=src, dst_ref=dst, send_sem=send_sem, recv_sem=recv_sem,
                                        device_id=to, device_id_type=MESH)


def _allgather_weights(shards):
    n = len(shards)
    per = 7

    def body(*refs):
        x_refs, out_refs, (send_sems, recv_sems) = refs[:n], refs[n:2 * n], refs[2 * n:]
        x, y, c, chips = _place()
        sibling = (x, y, 1 - c)

        def half(a, px, py, hc):
            rh = shards[a].shape[0] // 2
            return out_refs[a].at[2 * px + py, pl.ds(hc * rh, rh), :]

        def mine(a):
            rh = shards[a].shape[0] // 2
            return x_refs[a].at[pl.ds(c * rh, rh), :]

        def sems(a, k):
            return send_sems.at[per * a + k], recv_sems.at[per * a + k]

        first = [_remote(mine(a), half(a, x, y, c), *sems(a, j), (*chip, c))
                 for a in range(n) for j, chip in enumerate(chips)]
        own = [_remote(x_refs[a], out_refs[a].at[2 * x + y], *sems(a, 6), sibling) for a in range(n)]
        for cp in first + own:
            cp.start()
        passed = []
        for j, chip in enumerate(chips):
            for a in range(n):
                landed = half(a, *chip, c)
                _remote(landed, landed, *sems(a, j), (*chip, c)).wait_recv()
                cp = _remote(landed, landed, *sems(a, 3 + j), sibling)
                cp.start()
                passed.append(cp)
        for j, chip in enumerate(chips):
            for a in range(n):
                theirs = half(a, *chip, 1 - c)
                _remote(theirs, theirs, *sems(a, 3 + j), sibling).wait_recv()
        for cp in own:
            cp.wait_recv()
        for cp in first + passed + own:
            cp.wait_send()

    return pl.pallas_call(
        body, name="allgather_weights",
        out_shape=[jax.ShapeDtypeStruct((N_SHARDS,) + sh.shape, sh.dtype) for sh in shards],
        in_specs=[HBM_SPEC] * n, out_specs=[HBM_SPEC] * n,
        scratch_shapes=[pltpu.SemaphoreType.DMA((per * n,)), pltpu.SemaphoreType.DMA((per * n,))],
    )(*shards)


def _pair_exchange(gs):
    n = len(gs)

    def body(*refs):
        g_refs, out_refs, (send_sems, recv_sems) = refs[:n], refs[n:2 * n], refs[2 * n:]
        x, y, c, _ = _place()
        cps = []
        for a in range(n):
            rh = gs[a].shape[1] // 2
            cps.append(_remote(g_refs[a].at[:, pl.ds((1 - c) * rh, rh), :], out_refs[a], send_sems.at[a],
                               recv_sems.at[a], (x, y, 1 - c)))
        for cp in cps:
            cp.start()
        for cp in cps:
            cp.wait()

    return pl.pallas_call(
        body, name="rs_pair_exchange",
        out_shape=[jax.ShapeDtypeStruct((g.shape[0], g.shape[1] // 2, g.shape[2]), g.dtype) for g in gs],
        in_specs=[HBM_SPEC] * n, out_specs=[HBM_SPEC] * n,
        scratch_shapes=[pltpu.SemaphoreType.DMA((n,)), pltpu.SemaphoreType.DMA((n,))],
    )(*gs)


def _add_rows(rows, cols):
    best = 16
    for t in range(16, rows + 1, 16):
        if rows % t == 0 and t * cols * 4 <= 2304 * 1024:
            best = t
    return best


def _pair_add(g, other, c_idx):
    n, r, cols = g.shape
    rh = r // 2
    tr = _add_rows(rh, cols)
    nb = rh // tr

    def body(c_ref, a_ref, b_ref, o_ref, ob_ref):
        sm = a_ref[...] + b_ref[...]
        o_ref[...] = sm
        ob_ref[...] = sm.astype(BF16)

    out_blk = pl.BlockSpec((1, tr, cols), lambda s, i, c_ref: (s, i, 0))
    return pl.pallas_call(
        body, name="rs_pair_add",
        grid_spec=pltpu.PrefetchScalarGridSpec(
            num_scalar_prefetch=1, grid=(n, nb),
            in_specs=[pl.BlockSpec((1, tr, cols), lambda s, i, c_ref: (s, c_ref[0] * nb + i, 0)), out_blk],
            out_specs=[out_blk, out_blk]),
        out_shape=[jax.ShapeDtypeStruct((n, rh, cols), F32), jax.ShapeDtypeStruct((n, rh, cols), BF16)],
        compiler_params=_params(("parallel", "parallel")),
    )(c_idx, g, other)


def _chip_exchange(ps):
    n = len(ps)

    def body(*refs):
        p_refs, out_refs, (send_sems, recv_sems) = refs[:n], refs[n:2 * n], refs[2 * n:]
        x, y, c, chips = _place()
        cps = [_remote(p_refs[a].at[2 * chip[0] + chip[1]], out_refs[a].at[j], send_sems.at[3 * a + j],
                       recv_sems.at[3 * a + j], (*chip, c))
               for a in range(n) for j, chip in enumerate(chips)]
        for cp in cps:
            cp.start()
        for cp in cps:
            cp.wait()

    return pl.pallas_call(
        body, name="rs_chip_exchange",
        out_shape=[jax.ShapeDtypeStruct((3,) + p.shape[1:], p.dtype) for p in ps],
        in_specs=[HBM_SPEC] * n, out_specs=[HBM_SPEC] * n,
        scratch_shapes=[pltpu.SemaphoreType.DMA((3 * n,)), pltpu.SemaphoreType.DMA((3 * n,))],
    )(*ps)


def _chip_add(p, recv, chip_idx):
    n, rh, cols = p.shape
    tr = _add_rows(rh, cols)

    def body(s_ref, own_ref, r_ref, o_ref):
        o_ref[...] = ((own_ref[0] + r_ref[0].astype(F32)) + r_ref[1].astype(F32)) + r_ref[2].astype(F32)

    return pl.pallas_call(
        body, name="rs_chip_add",
        grid_spec=pltpu.PrefetchScalarGridSpec(
            num_scalar_prefetch=1, grid=(rh // tr,),
            in_specs=[pl.BlockSpec((1, tr, cols), lambda i, s_ref: (s_ref[0], i, 0)),
                      pl.BlockSpec((3, tr, cols), lambda i, s_ref: (0, i, 0))],
            out_specs=pl.BlockSpec((tr, cols), lambda i, s_ref: (i, 0))),
        out_shape=jax.ShapeDtypeStruct((rh, cols), p.dtype),
        compiler_params=_params(("parallel",)),
    )(chip_idx, p, recv)


def _pair_gather(fs):
    n = len(fs)

    def body(*refs):
        f_refs, out_refs, (send_sems, recv_sems) = refs[:n], refs[n:2 * n], refs[2 * n:]
        x, y, c, _ = _place()
        cps = [_remote(f_refs[a], out_refs[a], send_sems.at[a], recv_sems.at[a], (x, y, 1 - c)) for a in range(n)]
        for cp in cps:
            cp.start()
        for cp in cps:
            cp.wait()

    return pl.pallas_call(
        body, name="rs_pair_gather",
        out_shape=[jax.ShapeDtypeStruct(f.shape, f.dtype) for f in fs],
        in_specs=[HBM_SPEC] * n, out_specs=[HBM_SPEC] * n,
        scratch_shapes=[pltpu.SemaphoreType.DMA((n,)), pltpu.SemaphoreType.DMA((n,))],
    )(*fs)


def _allreduce_small(v):
    r, cols = v.shape

    def body(x_ref, out_ref, slots, send_sems, recv_sems):
        x, y, c, _ = _place()
        bits = [(bx, by, bc) for bx in (0, 1) for by in (0, 1) for bc in (0, 1)]

        def flip(b):
            return (1 - x if b[0] else x, 1 - y if b[1] else y, 1 - c if b[2] else c)

        slots[0] = x_ref[...]
        cps = [_remote(x_ref, slots.at[k], send_sems.at[k - 1], recv_sems.at[k - 1], flip(bits[k]))
               for k in range(1, 8)]
        for cp in cps:
            cp.start()
        for cp in cps:
            cp.wait()
        acc = None
        for b in bits:
            fx, fy, fc = flip(b)
            term = slots[4 * fx + 2 * fy + fc]
            acc = term if acc is None else acc + term
        out_ref[...] = acc

    return pl.pallas_call(
        body, name="allreduce_small",
        out_shape=jax.ShapeDtypeStruct((r, cols), v.dtype),
        in_specs=[VMEM_SPEC], out_specs=VMEM_SPEC,
        scratch_shapes=[pltpu.VMEM((8, r, cols), v.dtype), pltpu.SemaphoreType.DMA((7,)),
                        pltpu.SemaphoreType.DMA((7,))],
        compiler_params=pltpu.CompilerParams(vmem_limit_bytes=VMEM_LIMIT),
    )(v)


BIG = (("a_w_in", 2), ("a_w_out", 1), ("w_kv", 1), ("b_w_q", 1), ("b_w_out", 1), ("f_w_up", 2), ("f_w_down", 1))
SMALL = (("a_norm", 1), ("a_conv", 2), ("a_A_log", None), ("a_dt_bias", None), ("a_out_norm", None),
         ("kv_norm", None), ("b_norm", None), ("b_rel_bias", None), ("f_norm", None), ("f_conv", 2),
         ("f_conv_b", None), ("final_norm", None))
WEIGHT_ORDER = ("a_norm", "a_w_in", "a_conv", "a_A_log", "a_dt_bias", "a_out_norm", "a_w_out", "kv_norm", "w_kv",
                "b_norm", "b_w_q", "b_rel_bias", "b_w_out", "f_norm", "f_w_up", "f_conv", "f_conv_b", "f_w_down",
                "final_norm")


def _pad_rows(flat, cols, quantum):
    n = flat.shape[-1]
    rows = -(-n // (cols * quantum)) * quantum
    pad = [(0, 0)] * (flat.ndim - 1) + [(0, rows * cols - n)]
    return jnp.pad(flat, pad).reshape(flat.shape[:-1] + (rows, cols))


GROUPS = ("GU", "GD", "GK", "GI")


def _group_shards(w, dtype):
    def two(a):
        return a.reshape(-1, a.shape[-1])

    return dict(GU=two(w["f_w_up"]).astype(dtype),
                GD=jnp.concatenate([two(w[n]) for n in ("f_w_down", "a_w_out", "b_w_q", "b_w_out")]).astype(dtype),
                GK=w["w_kv"].astype(dtype),
                GI=two(w["a_w_in"]).astype(dtype))


def _ungroup(red, shard_shapes):
    out = dict(f_w_up=red["GU"].reshape(shard_shapes["f_w_up"]), w_kv=red["GK"],
               a_w_in=red["GI"].reshape(shard_shapes["a_w_in"]))
    off = 0
    for n in ("f_w_down", "a_w_out", "b_w_q", "b_w_out"):
        shp = shard_shapes[n]
        rows = math.prod(shp[:-1])
        out[n] = red["GD"][off:off + rows].reshape(shp)
        off += rows
    return out


def _dense_a_in(gi):
    out = []
    for i in range(N_A):
        full = jnp.transpose(gi[:, i * D_MODEL:(i + 1) * D_MODEL], (1, 0, 2)).reshape(D_MODEL, -1)
        out.append((full[:, :A_CONV_WIDTH], full[:, A_CONV_WIDTH:A_CONV_WIDTH + A_QK],
                    jnp.pad(full[:, A_CONV_WIDTH + A_QK:], ((0, 0), (0, LANE - 2 * A_HEADS)))))
    return out


def _pack_small(values, names):
    return _pad_rows(jnp.concatenate([values[n].reshape(-1) for n in names]), LANE, SUB)


def _unpack_small(packed, shapes, names):
    flat = packed.reshape(-1)
    out, off = {}, 0
    for n in names:
        size = math.prod(shapes[n])
        out[n] = flat[off:off + size].reshape(shapes[n])
        off += size
    return out


def _adamw_nd(w, g, m, v):
    shp = w.shape
    two = (math.prod(shp[:-1]), shp[-1])
    d, mn, vn = _adamw(w.reshape(two), g.reshape(two), m.reshape(two), v.reshape(two))
    return d.reshape(shp), mn.reshape(shp), vn.reshape(shp)


def kernel(x, a_norm, a_w_in, a_conv, a_A_log, a_dt_bias, a_out_norm, a_w_out, kv_norm, w_kv, b_norm, b_w_q, b_rel_bias, b_w_out, f_norm, f_w_up, f_conv, f_conv_b, f_w_down, final_norm, loss_target, m_a_norm, m_a_w_in, m_a_conv, m_a_A_log, m_a_dt_bias, m_a_out_norm, m_a_w_out, m_kv_norm, m_w_kv, m_b_norm, m_b_w_q, m_b_rel_bias, m_b_w_out, m_f_norm, m_f_w_up, m_f_conv, m_f_conv_b, m_f_w_down, m_final_norm, v_a_norm, v_a_w_in, v_a_conv, v_a_A_log, v_a_dt_bias, v_a_out_norm, v_a_w_out, v_kv_norm, v_w_kv, v_b_norm, v_b_w_q, v_b_rel_bias, v_b_w_out, v_f_norm, v_f_w_up, v_f_conv, v_f_conv_b, v_f_w_down, v_final_norm):
    w = dict(a_norm=a_norm, a_w_in=a_w_in, a_conv=a_conv, a_A_log=a_A_log, a_dt_bias=a_dt_bias,
             a_out_norm=a_out_norm, a_w_out=a_w_out, kv_norm=kv_norm, w_kv=w_kv, b_norm=b_norm, b_w_q=b_w_q,
             b_rel_bias=b_rel_bias, b_w_out=b_w_out, f_norm=f_norm, f_w_up=f_w_up, f_conv=f_conv,
             f_conv_b=f_conv_b, f_w_down=f_w_down, final_norm=final_norm)
    m = dict(a_norm=m_a_norm, a_w_in=m_a_w_in, a_conv=m_a_conv, a_A_log=m_a_A_log, a_dt_bias=m_a_dt_bias,
             a_out_norm=m_a_out_norm, a_w_out=m_a_w_out, kv_norm=m_kv_norm, w_kv=m_w_kv, b_norm=m_b_norm,
             b_w_q=m_b_w_q, b_rel_bias=m_b_rel_bias, b_w_out=m_b_w_out, f_norm=m_f_norm, f_w_up=m_f_w_up,
             f_conv=m_f_conv, f_conv_b=m_f_conv_b, f_w_down=m_f_w_down, final_norm=m_final_norm)
    v = dict(a_norm=v_a_norm, a_w_in=v_a_w_in, a_conv=v_a_conv, a_A_log=v_a_A_log, a_dt_bias=v_a_dt_bias,
             a_out_norm=v_a_out_norm, a_w_out=v_a_w_out, kv_norm=v_kv_norm, w_kv=v_w_kv, b_norm=v_b_norm,
             b_w_q=v_b_w_q, b_rel_bias=v_b_rel_bias, b_w_out=v_b_w_out, f_norm=v_f_norm, f_w_up=v_f_w_up,
             f_conv=v_f_conv, f_conv_b=v_f_conv_b, f_w_down=v_f_w_down, final_norm=v_final_norm)
    xi, yi, ci = lax.axis_index("x"), lax.axis_index("y"), lax.axis_index("c")
    chip = 2 * xi + yi
    shard_shapes = {n: w[n].shape for n in WEIGHT_ORDER}

    mine_w = _group_shards(w, BF16)
    full = dict(zip(GROUPS, _allgather_weights([mine_w[n] for n in GROUPS])))
    full["a_in"] = _dense_a_in(full.pop("GI"))
    sharded_small = [n for n, axis in SMALL if axis is not None]
    placed = {}
    for n, axis in SMALL:
        if axis is not None:
            wide = list(w[n].shape)
            wide[axis] *= 4
            mine_once = w[n] * (1 - ci).astype(F32)
            placed[n] = lax.dynamic_update_slice_in_dim(jnp.zeros(wide, F32), mine_once, chip * w[n].shape[axis], axis)
    placed_shapes = {n: placed[n].shape for n in sharded_small}
    full.update(_unpack_small(_allreduce_small(_pack_small(placed, sharded_small)), placed_shapes, sharded_small))
    for n, axis in SMALL:
        if axis is None:
            full[n] = w[n]

    loss_part, grad_x, grads = _local_step(x[0], loss_target[0], full)

    packed = [grads[n] for n in GROUPS]
    c_idx = jnp.reshape(ci, (1,)).astype(jnp.int32)
    chip_idx = jnp.reshape(chip, (1,)).astype(jnp.int32)
    pairs = [_pair_add(g, o, c_idx) for g, o in zip(packed, _pair_exchange(packed))]
    arrived = _chip_exchange([pb for _, pb in pairs])
    mine = [_chip_add(p, r, chip_idx) for (p, _), r in zip(pairs, arrived)]
    theirs = _pair_gather(mine)
    red = _ungroup({n: jnp.concatenate([jnp.where(ci == 0, a, b), jnp.where(ci == 0, b, a)], axis=0)
                    for n, a, b in zip(GROUPS, mine, theirs)}, shard_shapes)

    small_names = [n for n, _ in SMALL]
    small_vals = {n: grads[n] for n in small_names}
    small_vals["loss"] = loss_part[0, :1]
    names = ["loss"] + small_names
    shapes = {n: small_vals[n].shape for n in names}
    summed = _unpack_small(_allreduce_small(_pack_small(small_vals, names)), shapes, names)
    loss = summed["loss"][0]
    for n, axis in SMALL:
        g = summed[n]
        if axis is not None:
            g = lax.dynamic_slice_in_dim(g, chip * w[n].shape[axis], w[n].shape[axis], axis)
        red[n] = g

    delta, new_m, new_v = {}, {}, {}
    for n, _ in BIG:
        delta[n], new_m[n], new_v[n] = _adamw_nd(w[n], red[n], m[n], v[n])
    local_shapes = {n: w[n].shape for n in small_names}
    packs = [_pack_small(t, small_names) for t in (w, red, m, v)]
    outs = _adamw(*packs)
    ds, ms, vs = (_unpack_small(o, local_shapes, small_names) for o in outs)
    delta.update(ds)
    new_m.update(ms)
    new_v.update(vs)

    return (loss, grad_x[None], *[red[n] for n in WEIGHT_ORDER], *[delta[n] for n in WEIGHT_ORDER],
            *[new_m[n] for n in WEIGHT_ORDER], *[new_v[n] for n in WEIGHT_ORDER])
```

```python
import math

import jax
import jax.numpy as jnp
from jax import lax
from jax.experimental import pallas as pl
from jax.experimental.pallas import tpu as pltpu

F32 = jnp.float32
BF16 = jnp.bfloat16
HIGHEST = lax.Precision.HIGHEST
MESH = pl.DeviceIdType.MESH

D_MODEL = 1024
CHUNK = 64
A_HEADS = 8
A_HEAD = 128
A_QK = A_HEADS * A_HEAD
A_CONV_WIDTH = 3 * A_QK
B_HEADS = 16
B_HEAD = 64
LEFT = 8 * CHUNK
QBLK = 4 * CHUNK
KBLK = LEFT + QBLK
REL_CLIP = 256
REL_PAD = 640
FFN_DIM = 2816
EPS = 1e-6
NEG_INF = -1e30
LANE = 128
SUB = 8
VMEM_LIMIT = 56 * 1024 * 1024

ADAM_LR = 0.001
ADAM_B1 = 0.9
ADAM_B2 = 0.999
ADAM_EPS = 1e-08
ADAM_WD = 0.01
ADAM_STEP = 10


VMEM_FULL = pl.BlockSpec(memory_space=pltpu.VMEM)


def _params(sem=None):
    return pltpu.CompilerParams(dimension_semantics=sem, vmem_limit_bytes=VMEM_LIMIT)


def _tile(n, cap):
    if n <= cap:
        return n
    best = None
    for t in range(LANE, cap + 1, LANE):
        if n % t == 0:
            best = t
    assert best is not None, n
    return best


def _sigmoid(x):
    return 0.5 * jnp.tanh(0.5 * x) + 0.5


def _softplus(x):
    return jnp.maximum(x, 0.0) + jnp.log(1.0 + jnp.exp(-jnp.abs(x)))


def _dot(a, b, dims, prec=None):
    return lax.dot_general(a, b, (dims, ((), ())), preferred_element_type=F32, precision=prec)


NN = ((1,), (0,))
NT = ((1,), (1,))
TN = ((0,), (0,))


def _bdot(a, b, dims):
    return _dot(a.astype(BF16), b.astype(BF16), dims)


def _mm(a, b, mode="nn", out_dtype=F32, res=None, name="mm", norm=None):
    if mode == "nn":
        (m, k), (k2, n) = a.shape, b.shape
    elif mode == "nt":
        (m, k), (n, k2) = a.shape, b.shape
    else:
        (k, m), (k2, n) = a.shape, b.shape
    assert k == k2, (a.shape, b.shape, mode)
    tm, tn, tk = _tile(m, 1408), _tile(n, 1408), _tile(k, 1408)
    if m == 8192:
        tm = 1024
    if k == 8192:
        tk = 1024
    nk = k // tk
    dims = {"nn": NN, "nt": NT, "tn": TN}[mode]
    a_spec = {"nn": pl.BlockSpec((tm, tk), lambda i, j, kk: (i, kk)),
              "nt": pl.BlockSpec((tm, tk), lambda i, j, kk: (i, kk)),
              "tn": pl.BlockSpec((tk, tm), lambda i, j, kk: (kk, i))}[mode]
    b_spec = {"nn": pl.BlockSpec((tk, tn), lambda i, j, kk: (kk, j)),
              "nt": pl.BlockSpec((tn, tk), lambda i, j, kk: (j, kk)),
              "tn": pl.BlockSpec((tk, tn), lambda i, j, kk: (kk, j))}[mode]
    o_spec = pl.BlockSpec((tm, tn), lambda i, j, kk: (i, j))
    return _mm_call(name, a, b, dims, (m // tm, n // tn, nk), a_spec, b_spec, o_spec, (m, n), out_dtype, (tm, tn),
                    res, norm)


ROW_TILE = 1024
N_SHARDS = 4


def _mm_call(name, a, b, dims, grid, a_spec, b_spec, o_spec, out_shape, out_dtype, acc_shape, res=None, norm=None,
             into=None):
    nk = grid[2]
    has_res = res is not None
    has_norm = norm is not None
    has_into = into is not None
    if has_norm:
        assert grid[1] == 1 and len(out_shape) == 2 and out_dtype == F32

    def flat(v):
        return v.reshape(-1, v.shape[-1]) if v.ndim == 3 else v

    def body(a_ref, b_ref, *rest):
        rest = list(rest)
        res_ref = rest.pop(0) if has_res else None
        x_ref, g_ref, dres_ref = (rest.pop(0), rest.pop(0), rest.pop(0)) if has_norm else (None, None, None)
        if has_into:
            rest.pop(0)
        o_ref = rest.pop(0)
        dg_ref = rest.pop(0) if has_norm else None
        acc = rest.pop(0)
        kk = pl.program_id(2)
        first_rows = pl.program_id(0) == 0

        @pl.when(kk == 0)
        def _():
            acc[...] = jnp.zeros_like(acc)

        if has_norm:
            @pl.when(first_rows & (kk == 0))
            def _():
                dg_ref[...] = jnp.zeros_like(dg_ref)

        acc[...] += _bdot(flat(a_ref[...]), flat(b_ref[...]), dims)

        @pl.when(kk == nk - 1)
        def _():
            r = acc[...]
            if has_res:
                r = r + res_ref[...]
            if not has_norm:
                o_ref[...] = r.reshape(o_ref.shape).astype(out_dtype)
                return
            xv = x_ref[...]
            rs = lax.rsqrt(jnp.mean(xv * xv, axis=-1, keepdims=True) + EPS)
            t = r * g_ref[...]
            c = jnp.mean(t * xv, axis=-1, keepdims=True)
            o_ref[...] = dres_ref[...] + rs * t - xv * (rs * rs * rs) * c
            dg_ref[...] += jnp.sum(r * xv * rs, axis=0, keepdims=True)

    args = [a, b] + ([res] if has_res else [])
    in_specs = [a_spec, b_spec] + ([o_spec] if has_res else [])
    out_specs, out_shapes = o_spec, jax.ShapeDtypeStruct(out_shape, out_dtype)
    if has_norm:
        vec = pl.BlockSpec((1, out_shape[1]), lambda i, j, kk: (0, 0))
        args += list(norm)
        in_specs += [o_spec, vec, o_spec]
        out_specs = [o_spec, vec]
        out_shapes = [out_shapes, jax.ShapeDtypeStruct((1, out_shape[1]), F32)]
    aliases = {}
    if has_into:
        assert not has_norm and into.dtype == out_dtype
        aliases = {len(args): 0}
        args.append(into)
        in_specs.append(pl.BlockSpec(memory_space=pl.ANY))
        out_shapes = jax.ShapeDtypeStruct(into.shape, out_dtype)
    return pl.pallas_call(
        body, name=name, grid=grid, in_specs=in_specs, out_specs=out_specs, out_shape=out_shapes,
        scratch_shapes=[pltpu.VMEM(acc_shape, F32)], input_output_aliases=aliases,
        compiler_params=_params(("arbitrary" if has_norm else "parallel", "parallel", "arbitrary")),
    )(*args)


def _shards_per_block(rows):
    return N_SHARDS if N_SHARDS * rows <= 1408 else 2


def _mm_rowsh(a, buf, rows, blk0, mode, name, res=None, out_dtype=F32, norm=None):
    s = a.shape[0]
    cols = buf.shape[2]
    g = _shards_per_block(rows)
    tm = _tile(s, ROW_TILE)
    b_blk = (g, rows, cols)
    if mode == "nn":
        return _mm_call(name, a, buf, NN, (s // tm, 1, N_SHARDS // g),
                        pl.BlockSpec((tm, g * rows), lambda i, j, kk: (i, kk)),
                        pl.BlockSpec(b_blk, lambda i, j, kk: (kk, blk0, 0)),
                        pl.BlockSpec((tm, cols), lambda i, j, kk: (i, 0)),
                        (s, cols), out_dtype, (tm, cols), res)
    return _mm_call(name, a, buf, NT, (s // tm, N_SHARDS // g, 1),
                    pl.BlockSpec((tm, cols), lambda i, j, kk: (i, 0)),
                    pl.BlockSpec(b_blk, lambda i, j, kk: (j, blk0, 0)),
                    pl.BlockSpec((tm, g * rows), lambda i, j, kk: (i, j)),
                    (s, N_SHARDS * rows), out_dtype, (tm, g * rows), res, norm)


def _mm_rowsh_dw(act, dy, rows, name, into, blk0):
    s = act.shape[0]
    cols = dy.shape[1]
    g = _shards_per_block(rows)
    ts = _tile(s, ROW_TILE)
    return _mm_call(name, act, dy, TN, (1, N_SHARDS // g, s // ts),
                    pl.BlockSpec((ts, g * rows), lambda i, j, kk: (kk, j)),
                    pl.BlockSpec((ts, cols), lambda i, j, kk: (kk, 0)),
                    pl.BlockSpec((g, rows, cols), lambda i, j, kk: (j, blk0, 0)),
                    into.shape, F32, (g * rows, cols), into=into)


def _mm_colsh(a, buf, krows, blk0, mode, name, flat=False, res=None, out_dtype=F32, norm=None):
    cols = buf.shape[2]
    b_nn = pl.BlockSpec((None, krows, cols), lambda i, j, kk: (j, blk0, 0))
    b_nt = pl.BlockSpec((None, krows, cols), lambda i, j, kk: (kk, blk0, 0))
    if mode == "nn":
        s = a.shape[0]
        tm = _tile(s, ROW_TILE)
        o_spec = (pl.BlockSpec((tm, cols), lambda i, j, kk: (i, j)) if flat
                  else pl.BlockSpec((None, tm, cols), lambda i, j, kk: (j, i, 0)))
        return _mm_call(name, a, buf, NN, (s // tm, N_SHARDS, 1),
                        pl.BlockSpec((tm, krows), lambda i, j, kk: (i, 0)), b_nn, o_spec,
                        (s, N_SHARDS * cols) if flat else (N_SHARDS, s, cols), out_dtype, (tm, cols), res)
    s = a.shape[0] if flat else a.shape[1]
    tm = _tile(s, ROW_TILE)
    a_spec = (pl.BlockSpec((tm, cols), lambda i, j, kk: (i, kk)) if flat
              else pl.BlockSpec((None, tm, cols), lambda i, j, kk: (kk, i, 0)))
    return _mm_call(name, a, buf, NT, (s // tm, 1, N_SHARDS), a_spec, b_nt,
                    pl.BlockSpec((tm, krows), lambda i, j, kk: (i, 0)),
                    (s, krows), out_dtype, (tm, krows), res, norm)


def _mm_colsh_dw(x, dy, name, flat=False, into=None, blk0=0):
    s, k = x.shape
    cols = dy.shape[1] // N_SHARDS if flat else dy.shape[2]
    ts = _tile(s, ROW_TILE)
    b_spec = (pl.BlockSpec((ts, cols), lambda i, j, kk: (kk, j)) if flat
              else pl.BlockSpec((None, ts, cols), lambda i, j, kk: (j, kk, 0)))
    return _mm_call(name, x, dy, TN, (1, N_SHARDS, s // ts),
                    pl.BlockSpec((ts, k), lambda i, j, kk: (kk, 0)), b_spec,
                    pl.BlockSpec((None, k, cols), lambda i, j, kk: (j, blk0, 0)),
                    (N_SHARDS, k, cols) if into is None else into.shape, F32, (k, cols), into=into)


def _rmsnorm_fwd(x, g):
    s, d = x.shape
    tr = _tile(s, 1024)

    def body(x_ref, g_ref, o_ref):
        xv = x_ref[...]
        r = lax.rsqrt(jnp.mean(xv * xv, axis=-1, keepdims=True) + EPS)
        o_ref[...] = (xv * r * g_ref[...]).astype(BF16)

    return pl.pallas_call(
        body, name="rmsnorm_fwd", grid=(s // tr,),
        in_specs=[pl.BlockSpec((tr, d), lambda i: (i, 0)), pl.BlockSpec((1, d), lambda i: (0, 0))],
        out_specs=pl.BlockSpec((tr, d), lambda i: (i, 0)),
        out_shape=jax.ShapeDtypeStruct((s, d), BF16),
        compiler_params=_params(("parallel",)),
    )(x, g)


def _final_loss(h, g, tgt):
    s, d = h.shape
    tr = _tile(s, 1024)

    def body(x_ref, g_ref, t_ref, loss_ref, dx_ref, dg_ref):
        @pl.when(pl.program_id(0) == 0)
        def _():
            dg_ref[...] = jnp.zeros_like(dg_ref)
            loss_ref[...] = jnp.zeros_like(loss_ref)

        xv = x_ref[...]
        r = lax.rsqrt(jnp.mean(xv * xv, axis=-1, keepdims=True) + EPS)
        xh = xv * r
        err = xh * g_ref[...] - t_ref[...]
        per_row = jnp.mean(err * err, axis=-1, keepdims=True)
        loss_ref[...] += 0.5 * jnp.sum(per_row, axis=0, keepdims=True)
        dy = err * (1.0 / d)
        t = dy * g_ref[...]
        c = jnp.mean(t * xv, axis=-1, keepdims=True)
        dx_ref[...] = r * t - xv * (r * r * r) * c
        dg_ref[...] += jnp.sum(dy * xh, axis=0, keepdims=True)

    row = pl.BlockSpec((tr, d), lambda i: (i, 0))
    vec = pl.BlockSpec((1, d), lambda i: (0, 0))
    return pl.pallas_call(
        body, name="final_loss", grid=(s // tr,),
        in_specs=[row, vec, row],
        out_specs=[pl.BlockSpec((1, LANE), lambda i: (0, 0)), row, vec],
        out_shape=[jax.ShapeDtypeStruct((1, LANE), F32), jax.ShapeDtypeStruct((s, d), F32),
                   jax.ShapeDtypeStruct((1, d), F32)],
        compiler_params=_params(("arbitrary",)),
    )(h, g, tgt)


CONV_ROWS = 512
CONV_COLS = 1408
CONV_CHUNK = 64


def _per_lane_tile(tile_body):
    def body(*refs):
        for lt in range(refs[0].shape[-1] // LANE):
            cols = slice(lt * LANE, (lt + 1) * LANE)
            tile_body(*[r.at[(slice(None),) * (len(r.shape) - 1) + (cols,)] for r in refs])
    return body


def _lagged(window, lag):
    return (pltpu.roll(window, lag, 0) if lag else window)[SUB:]


def _led(window, lead):
    n = window.shape[0] - SUB
    return (pltpu.roll(window, window.shape[0] - lead, 0) if lead else window)[:n]


def _fold_rows(v):
    return jnp.sum(v.reshape(v.shape[0] // SUB, SUB, v.shape[1]), axis=0)


def _taps(shifted, w):
    acc = None
    for k, xs in enumerate(shifted):
        term = xs * w[k:k + 1, :]
        acc = term if acc is None else acc + term
    return acc


def _conv_tiles(s, c):
    return _tile(s, CONV_ROWS), _tile(c, CONV_COLS)


def _conv_silu_fwd(pre, w):
    s, c = pre.shape
    width = w.shape[0]
    tr, tc = _conv_tiles(s, c)

    def body(x_ref, w_ref, o_ref, tail):
        @pl.when(pl.program_id(1) == 0)
        def _():
            tail[...] = jnp.zeros_like(tail)

        wv = w_ref[...]

        def do(c0, window):
            y = _taps([_lagged(window, width - 1 - k) for k in range(width)], wv)
            o_ref[pl.ds(c0, CONV_CHUNK), :] = y * _sigmoid(y)

        def chunk(ci, carry):
            c0 = pl.multiple_of(ci * CONV_CHUNK, CONV_CHUNK)
            do(c0, x_ref[pl.ds(pl.multiple_of(c0 - SUB, SUB), CONV_CHUNK + SUB), :])
            return carry

        do(0, jnp.concatenate([tail[...], x_ref[:CONV_CHUNK, :]], axis=0))
        lax.fori_loop(1, tr // CONV_CHUNK, chunk, 0)
        tail[...] = x_ref[tr - SUB:, :]

    blk = pl.BlockSpec((tr, tc), lambda j, i: (i, j))
    return pl.pallas_call(
        _per_lane_tile(body), name="conv_silu_fwd", grid=(c // tc, s // tr),
        in_specs=[blk, pl.BlockSpec((width, tc), lambda j, i: (0, j))], out_specs=blk,
        out_shape=jax.ShapeDtypeStruct((s, c), F32),
        scratch_shapes=[pltpu.VMEM((SUB, tc), F32)],
        compiler_params=_params(("parallel", "arbitrary")),
    )(pre, w)


def _prev_rows_index(i_blk, tr):
    return jnp.maximum(i_blk * (tr // SUB) - 1, 0)


def _conv_silu_bwd(pre, w, dact):
    s, c = pre.shape
    width = w.shape[0]
    tr, tc = _conv_tiles(s, c)
    nr = s // tr

    nchunks = tr // CONV_CHUNK

    def body(x_ref, p_ref, w_ref, d_ref, dx_ref, dw_ref, head):
        @pl.when(pl.program_id(1) == 0)
        def _():
            head[...] = jnp.zeros_like(head)
            dw_ref[...] = jnp.zeros_like(dw_ref)

        wv = w_ref[...]

        def do(c0, window, later, dws):
            xs = [_lagged(window, width - 1 - k) for k in range(width)]
            y = _taps(xs, wv)
            sg = _sigmoid(y)
            dy = d_ref[pl.ds(c0, CONV_CHUNK), :] * sg * (1.0 + y * (1.0 - sg))
            dws = tuple(dw + _fold_rows(dy * x) for dw, x in zip(dws, xs))
            ahead = jnp.concatenate([dy, later], axis=0)
            dx_ref[pl.ds(c0, CONV_CHUNK), :] = _taps([_led(ahead, width - 1 - k) for k in range(width)],
                                                     wv).astype(BF16)
            return dy[:SUB], dws

        def chunk(it, carry):
            c0 = pl.multiple_of((nchunks - 1 - it) * CONV_CHUNK, CONV_CHUNK)
            return do(c0, x_ref[pl.ds(pl.multiple_of(c0 - SUB, SUB), CONV_CHUNK + SUB), :], *carry)

        zero = jnp.zeros((SUB, LANE), F32)
        carry = lax.fori_loop(0, nchunks - 1, chunk, (head[...], (zero,) * width))
        before = jnp.where(pl.program_id(1) == nr - 1, 0.0, p_ref[...])
        later, dws = do(0, jnp.concatenate([before, x_ref[:CONV_CHUNK, :]], axis=0), *carry)
        head[...] = later
        for k in range(width):
            dw_ref[k:k + 1, :] += jnp.sum(dws[k], axis=0, keepdims=True)

    blk = pl.BlockSpec((tr, tc), lambda j, i: (nr - 1 - i, j))
    prev = pl.BlockSpec((SUB, tc), lambda j, i: (_prev_rows_index(nr - 1 - i, tr), j))
    wblk = pl.BlockSpec((width, tc), lambda j, i: (0, j))
    return pl.pallas_call(
        _per_lane_tile(body), name="conv_silu_bwd", grid=(c // tc, nr),
        in_specs=[blk, prev, wblk, blk], out_specs=[blk, wblk],
        out_shape=[jax.ShapeDtypeStruct((s, c), BF16), jax.ShapeDtypeStruct((width, c), F32)],
        scratch_shapes=[pltpu.VMEM((SUB, tc), F32)],
        compiler_params=_params(("parallel", "arbitrary")),
    )(pre, pre, w, dact)


def _ffn_act_fwd(pre, w, b):
    _, halves, s, c = pre.shape
    width = w.shape[2]
    tr, tc = _conv_tiles(s, c)
    ncb = c // tc

    def body(x_ref, w_ref, b_ref, o_ref, tail):
        @pl.when(pl.program_id(2) == 0)
        def _():
            tail[...] = jnp.zeros_like(tail)

        wg, wv, bg, bv = w_ref[0], w_ref[1], b_ref[0], b_ref[1]

        def do(c0, win_g, win_v):
            yg = _taps([_lagged(win_g, width - 1 - k) for k in range(width)], wg) + bg
            yv = _taps([_lagged(win_v, width - 1 - k) for k in range(width)], wv) + bv
            o_ref[pl.ds(c0, CONV_CHUNK), :] = (yg * _sigmoid(yg) * yv).astype(BF16)

        def chunk(ci, carry):
            c0 = pl.multiple_of(ci * CONV_CHUNK, CONV_CHUNK)
            rows = pl.ds(pl.multiple_of(c0 - SUB, SUB), CONV_CHUNK + SUB)
            do(c0, x_ref[0, rows, :], x_ref[1, rows, :])
            return carry

        do(0, jnp.concatenate([tail[0], x_ref[0, :CONV_CHUNK, :]], axis=0),
           jnp.concatenate([tail[1], x_ref[1, :CONV_CHUNK, :]], axis=0))
        lax.fori_loop(1, tr // CONV_CHUNK, chunk, 0)
        tail[...] = x_ref[:, tr - SUB:, :]

    return pl.pallas_call(
        _per_lane_tile(body), name="ffn_act_fwd", grid=(halves, ncb, s // tr),
        in_specs=[pl.BlockSpec((2, None, tr, tc), lambda h, j, i: (0, h, i, j)),
                  pl.BlockSpec((2, None, width, tc), lambda h, j, i: (0, h, 0, j)),
                  pl.BlockSpec((2, None, 1, tc), lambda h, j, i: (0, h, 0, j))],
        out_specs=pl.BlockSpec((tr, tc), lambda h, j, i: (i, h * ncb + j)),
        out_shape=jax.ShapeDtypeStruct((s, halves * c), BF16),
        scratch_shapes=[pltpu.VMEM((2, SUB, tc), F32)],
        compiler_params=_params(("parallel", "parallel", "arbitrary")),
    )(pre, w, b)


def _ffn_act_bwd(pre, w, b, dact):
    _, halves, s, c = pre.shape
    width = w.shape[2]
    tr, tc = _conv_tiles(s, c)
    ncb = c // tc
    nr = s // tr
    nchunks = tr // CONV_CHUNK

    def body(x_ref, p_ref, w_ref, b_ref, d_ref, dx_ref, dw_ref, db_ref, head):
        @pl.when(pl.program_id(2) == 0)
        def _():
            for r in (head, dw_ref, db_ref):
                r[...] = jnp.zeros_like(r)

        wg, wv, bg, bv = w_ref[0], w_ref[1], b_ref[0], b_ref[1]

        def do(c0, win_g, win_v, later_g, later_v, dwg, dwv, dbg, dbv):
            xg = [_lagged(win_g, width - 1 - k) for k in range(width)]
            xv = [_lagged(win_v, width - 1 - k) for k in range(width)]
            yg = _taps(xg, wg) + bg
            yv = _taps(xv, wv) + bv
            sg = _sigmoid(yg)
            da = d_ref[pl.ds(c0, CONV_CHUNK), :]
            dyv = da * yg * sg
            dyg = da * yv * sg * (1.0 + yg * (1.0 - sg))
            dwg = tuple(dw + _fold_rows(dyg * x) for dw, x in zip(dwg, xg))
            dwv = tuple(dw + _fold_rows(dyv * x) for dw, x in zip(dwv, xv))
            dbg = dbg + _fold_rows(dyg)
            dbv = dbv + _fold_rows(dyv)
            ahead_g = jnp.concatenate([dyg, later_g], axis=0)
            ahead_v = jnp.concatenate([dyv, later_v], axis=0)
            dxg = _taps([_led(ahead_g, width - 1 - k) for k in range(width)], wg)
            dxv = _taps([_led(ahead_v, width - 1 - k) for k in range(width)], wv)
            dx_ref[0, pl.ds(c0, CONV_CHUNK), :] = dxg.astype(BF16)
            dx_ref[1, pl.ds(c0, CONV_CHUNK), :] = dxv.astype(BF16)
            return dyg[:SUB], dyv[:SUB], dwg, dwv, dbg, dbv

        def chunk(it, carry):
            c0 = pl.multiple_of((nchunks - 1 - it) * CONV_CHUNK, CONV_CHUNK)
            rows = pl.ds(pl.multiple_of(c0 - SUB, SUB), CONV_CHUNK + SUB)
            return do(c0, x_ref[0, rows, :], x_ref[1, rows, :], *carry)

        zero = jnp.zeros((SUB, LANE), F32)
        carry = lax.fori_loop(0, nchunks - 1, chunk,
                              (head[0], head[1], (zero,) * width, (zero,) * width, zero, zero))
        before = jnp.where(pl.program_id(2) == nr - 1, 0.0, p_ref[...])
        later_g, later_v, dwg, dwv, dbg, dbv = do(
            0, jnp.concatenate([before[0], x_ref[0, :CONV_CHUNK, :]], axis=0),
            jnp.concatenate([before[1], x_ref[1, :CONV_CHUNK, :]], axis=0), *carry)
        head[0] = later_g
        head[1] = later_v
        db_ref[0] += jnp.sum(dbg, axis=0, keepdims=True)
        db_ref[1] += jnp.sum(dbv, axis=0, keepdims=True)
        for k in range(width):
            dw_ref[0, k:k + 1, :] += jnp.sum(dwg[k], axis=0, keepdims=True)
            dw_ref[1, k:k + 1, :] += jnp.sum(dwv[k], axis=0, keepdims=True)

    blk = pl.BlockSpec((2, None, tr, tc), lambda h, j, i: (0, h, nr - 1 - i, j))
    prev = pl.BlockSpec((2, None, SUB, tc), lambda h, j, i: (0, h, _prev_rows_index(nr - 1 - i, tr), j))
    wblk = pl.BlockSpec((2, None, width, tc), lambda h, j, i: (0, h, 0, j))
    bblk = pl.BlockSpec((2, None, 1, tc), lambda h, j, i: (0, h, 0, j))
    return pl.pallas_call(
        _per_lane_tile(body), name="ffn_act_bwd", grid=(halves, ncb, nr),
        in_specs=[blk, prev, wblk, bblk, pl.BlockSpec((tr, tc), lambda h, j, i: (nr - 1 - i, h * ncb + j))],
        out_specs=[blk, wblk, bblk],
        out_shape=[jax.ShapeDtypeStruct(pre.shape, BF16), jax.ShapeDtypeStruct(w.shape, F32),
                   jax.ShapeDtypeStruct(b.shape, F32)],
        scratch_shapes=[pltpu.VMEM((2, SUB, tc), F32)],
        compiler_params=_params(("parallel", "parallel", "arbitrary")),
    )(pre, pre, w, b, dact)


def _tri_masks():
    row = lax.broadcasted_iota(jnp.int32, (CHUNK, CHUNK), 0)
    col = lax.broadcasted_iota(jnp.int32, (CHUNK, CHUNK), 1)
    return row, col


def _tri_inv(ms, row, col):
    eye = (row == col).astype(F32)
    same_blk = (row >> 4) == (col >> 4)
    mds = [jnp.where(same_blk, m, 0.0) for m in ms]
    offs = [m - md for m, md in zip(ms, mds)]
    xs = [eye - md for md in mds]
    ps = [_bdot(md, md, NN) for md in mds]
    for _ in range(2):
        rs = [_bdot(jnp.concatenate([x, p], axis=0), p, NN) for x, p in zip(xs, ps)]
        xs = [x + r[:CHUNK] for x, r in zip(xs, rs)]
        ps = [r[CHUNK:] for r in rs]
    xs = [x + _bdot(x, p, NN) for x, p in zip(xs, ps)]
    ps = [_bdot(x, off, NN) for x, off in zip(xs, offs)]
    pps = [_bdot(p, p, NN) for p in ps]
    ys = [eye - p for p in ps]
    ys = [y + _bdot(y, pp, NN) for y, pp in zip(ys, pps)]
    return [_bdot(y, x, NN) for y, x in zip(ys, xs)]


def _gdn_gates(ba, alog, dtb, row, col):
    sig = _sigmoid(ba)
    neg_a = -jnp.exp(alog)
    g = neg_a * _softplus(ba + dtb)
    lower = (row >= col).astype(F32)
    gcum = _dot(lower, g, NN, HIGHEST)
    return sig, neg_a, g, gcum


def _gdn_head_common(q_raw, k_raw, v, beta, gc, gr, row, col):
    causal = row >= col
    strict = row > col
    rq = lax.rsqrt(jnp.sum(q_raw * q_raw, axis=-1, keepdims=True) + EPS)
    rk = lax.rsqrt(jnp.sum(k_raw * k_raw, axis=-1, keepdims=True) + EPS)
    q = q_raw * (rq * (A_HEAD ** -0.5))
    k = k_raw * rk
    decay = jnp.where(causal, jnp.exp(jnp.where(causal, gc - gr, 0.0)), 0.0)
    eg = jnp.exp(gc)
    gl = gc[CHUNK - 1:CHUNK, :]
    ekl = jnp.exp(gl - gc)
    dec = jnp.exp(gl)
    kb = k * beta
    kbq = jnp.concatenate([kb, q], axis=0)
    both = _bdot(kbq, k, NT)
    kk, qk = both[:CHUNK], both[CHUNK:]
    a = jnp.where(causal, qk * decay, 0.0)
    return dict(rq=rq, rk=rk, q=q, k=k, decay=decay, eg=eg, ekl=ekl, dec=dec, kb=kb, kbq=kbq, kk=kk, qk=qk, a=a,
                vb=v * beta, kbg=kb * eg, qd=q * eg, ke=k * ekl, causal=causal, strict=strict)


def _gdn_fwd(qkv, ba, z, alog, dtb, wn, gather=()):
    s = qkv.shape[0]
    nc = s // CHUNK
    ng = len(gather)

    def body(qkv_ref, ba_ref, z_ref, alog_ref, dtb_ref, wn_ref, *rest):
        x_refs, rest = rest[:ng], rest[ng:]
        y_ref, o_ref, st_ref, t_ref, w_ref, vn_ref = rest[:6]
        out_refs, rest = rest[6:6 + ng], rest[6 + ng:]
        state = rest[0]
        exchange = _ShardGather(x_refs, out_refs, *rest[1:]) if ng else None

        @pl.when(pl.program_id(0) == 0)
        def _():
            state[...] = jnp.zeros_like(state)
            if ng:
                exchange.start()

        row, col = _tri_masks()
        sig, _, _, gcum = _gdn_gates(ba_ref[...], alog_ref[...], dtb_ref[...], row, col)
        gt = gcum.T
        heads = range(A_HEADS)
        lanes = [slice(h * A_HEAD, (h + 1) * A_HEAD) for h in heads]
        fs = [_gdn_head_common(qkv_ref[:, lanes[h]], qkv_ref[:, A_QK + h * A_HEAD:A_QK + (h + 1) * A_HEAD],
                               qkv_ref[:, 2 * A_QK + h * A_HEAD:2 * A_QK + (h + 1) * A_HEAD],
                               sig[:, h:h + 1], gcum[:, 8 + h:9 + h], gt[8 + h:9 + h, :], row, col) for h in heads]
        ts = [t.astype(BF16) for t in
              _tri_inv([jnp.where(f["strict"], f["kk"] * f["decay"], 0.0) for f in fs], row, col)]
        uws = [_bdot(t, jnp.concatenate([f["vb"], f["kbg"]], axis=1), NN) for t, f in zip(ts, fs)]
        s0s = [state[h] for h in heads]
        ws_ = [uw[:, A_HEAD:].astype(BF16) for uw in uws]
        wss = [_bdot(jnp.concatenate([w, f["qd"].astype(BF16)], axis=0), s0, NN) for w, f, s0 in zip(ws_, fs, s0s)]
        vnews = [(uw[:, :A_HEAD] - wsq[:CHUNK]).astype(BF16) for uw, wsq in zip(uws, wss)]
        os_ = [wsq[CHUNK:] + _bdot(f["a"], vn, NN) for wsq, f, vn in zip(wss, fs, vnews)]
        s1s = [s0 * f["dec"] + _bdot(f["ke"], vn, TN) for s0, f, vn in zip(s0s, fs, vnews)]
        for h in heads:
            ln = lanes[h]
            st_ref[0, h] = s0s[h]
            t_ref[0, h] = ts[h]
            w_ref[:, ln] = ws_[h]
            vn_ref[:, ln] = vnews[h]
            state[h] = s1s[h]
            o = os_[h]
            o_ref[:, ln] = o
            r = lax.rsqrt(jnp.mean(o * o, axis=-1, keepdims=True) + EPS)
            zz = z_ref[:, ln]
            y_ref[:, ln] = (o * r * wn_ref[...] * zz * _sigmoid(zz)).astype(BF16)

        if ng:
            @pl.when(pl.program_id(0) == nc - 1)
            def _():
                exchange.finish()

    vec = pl.BlockSpec((1, LANE), lambda n: (0, 0))
    wide = pl.BlockSpec((CHUNK, A_QK), lambda n: (n, 0))
    return pl.pallas_call(
        body, name="gdn_fwd_gather" if ng else "gdn_fwd", grid=(nc,),
        in_specs=[pl.BlockSpec((CHUNK, A_CONV_WIDTH), lambda n: (n, 0)),
                  pl.BlockSpec((CHUNK, LANE), lambda n: (n, 0)), wide, vec, vec, vec] + [HBM_SPEC] * ng,
        out_specs=[wide, wide, pl.BlockSpec((1, A_HEADS, A_HEAD, A_HEAD), lambda n: (n, 0, 0, 0)),
                   pl.BlockSpec((1, A_HEADS, CHUNK, CHUNK), lambda n: (n, 0, 0, 0)), wide, wide] + [HBM_SPEC] * ng,
        out_shape=[jax.ShapeDtypeStruct((s, A_QK), BF16), jax.ShapeDtypeStruct((s, A_QK), F32),
                   jax.ShapeDtypeStruct((nc, A_HEADS, A_HEAD, A_HEAD), F32),
                   jax.ShapeDtypeStruct((nc, A_HEADS, CHUNK, CHUNK), BF16),
                   jax.ShapeDtypeStruct((s, A_QK), BF16), jax.ShapeDtypeStruct((s, A_QK), BF16)]
        + [jax.ShapeDtypeStruct((N_SHARDS,) + g.shape, g.dtype) for g in gather],
        scratch_shapes=[pltpu.VMEM((A_HEADS, A_HEAD, A_HEAD), F32)] + (_gather_scratch(ng) if ng else []),
        compiler_params=_params(("arbitrary",)),
    )(qkv, ba, z, alog, dtb, wn, *gather)


def _gdn_bwd(qkv, ba, z, o_raw, dy, states, t_all, w_all, vn_all, alog, dtb, wn):
    s = qkv.shape[0]
    nc = s // CHUNK

    def body(qkv_ref, ba_ref, z_ref, o_ref, dy_ref, st_ref, t_ref, w_ref, vn_ref, alog_ref, dtb_ref, wn_ref,
             dqkv_ref, dba_ref, dz_ref, dalog_ref, ddtb_ref, dwn_ref, dstate):
        @pl.when(pl.program_id(0) == 0)
        def _():
            for r in (dstate, dalog_ref, ddtb_ref, dwn_ref):
                r[...] = jnp.zeros_like(r)

        row, col = _tri_masks()
        bat = ba_ref[...]
        sig, neg_a, g, gcum = _gdn_gates(bat, alog_ref[...], dtb_ref[...], row, col)
        gt = gcum.T
        lane = lax.broadcasted_iota(jnp.int32, (CHUNK, LANE), 1)
        ones = jnp.ones((CHUNK, LANE), F32)
        last_row = lax.broadcasted_iota(jnp.int32, (CHUNK, 1), 0) == CHUNK - 1
        wnv = wn_ref[...]
        dgc_tile = jnp.zeros((CHUNK, LANE), F32)
        dbeta_tile = jnp.zeros((CHUNK, LANE), F32)
        dwn_acc = jnp.zeros((1, LANE), F32)
        hs = []
        for h in range(A_HEADS):
            ln = slice(h * A_HEAD, (h + 1) * A_HEAD)
            lk = slice(A_QK + h * A_HEAD, A_QK + (h + 1) * A_HEAD)
            lv = slice(2 * A_QK + h * A_HEAD, 2 * A_QK + (h + 1) * A_HEAD)
            q_raw, k_raw, v = qkv_ref[:, ln], qkv_ref[:, lk], qkv_ref[:, lv]
            f = _gdn_head_common(q_raw, k_raw, v, sig[:, h:h + 1], gcum[:, 8 + h:9 + h], gt[8 + h:9 + h, :], row, col)
            f.update(h=h, ln=ln, lk=lk, lv=lv, q_raw=q_raw, k_raw=k_raw, v=v, beta=sig[:, h:h + 1],
                     s0=st_ref[0, h], ds1=dstate[h], t=t_ref[0, h], w=w_ref[:, ln], vnew=vn_ref[:, ln])
            o = o_ref[:, ln]
            zz = z_ref[:, ln]
            dyv = dy_ref[:, ln]
            r = lax.rsqrt(jnp.mean(o * o, axis=-1, keepdims=True) + EPS)
            sz = _sigmoid(zz)
            silu = zz * sz
            dz_ref[:, ln] = (dyv * o * r * wnv * sz * (1.0 + zz * (1.0 - sz))).astype(BF16)
            dwn_acc = dwn_acc + jnp.sum(dyv * silu * o * r, axis=0, keepdims=True)
            tt = dyv * silu * wnv
            do = r * tt - o * (r * r * r) * jnp.mean(tt * o, axis=-1, keepdims=True)
            f["do_b"] = do.astype(BF16)
            hs.append(f)
        for f in hs:
            f["dvnew"] = _bdot(f["a"], f["do_b"], TN) + _bdot(f["ke"], f["ds1"], NN)
            f["da"] = jnp.where(f["causal"], _bdot(f["do_b"], f["vnew"], NT), 0.0)
            f["dke"] = _bdot(f["vnew"], f["ds1"], NT)
            f["ddec"] = jnp.sum(jnp.sum(f["s0"] * f["ds1"], axis=1, keepdims=True), axis=0, keepdims=True)
        for f in hs:
            do_dv = jnp.concatenate([f["do_b"], f["dvnew"].astype(BF16)], axis=0)
            both = _bdot(do_dv, f["s0"], NT)
            f["dqd"], f["dw"] = both[:CHUNK], -both[CHUNK:]
            qd_w = jnp.concatenate([f["qd"].astype(BF16), -f["w"]], axis=0)
            dstate[f["h"]] = _bdot(qd_w, do_dv, TN) + f["dec"] * f["ds1"]
        for f in hs:
            dd = jnp.concatenate([f["dvnew"], f["dw"]], axis=1).astype(BF16)
            tdd = _bdot(f["t"], dd, TN)
            f["dvb"], f["dkbg"] = tdd[:, :A_HEAD], tdd[:, A_HEAD:]
            f["dt"] = _bdot(dd, jnp.concatenate([f["vb"], f["kbg"]], axis=1), NT)
        for f in hs:
            f["tdt"] = _bdot(f["t"], f["dt"], TN)
        for f in hs:
            dm = jnp.where(f["strict"], -_bdot(f["tdt"], f["t"], NT), 0.0)
            f["ddecay"] = (dm * f["kk"] + f["da"] * f["qk"]) * f["decay"]
            f["dboth"] = jnp.concatenate([dm * f["decay"], f["da"] * f["decay"]], axis=0).astype(BF16)
        for f in hs:
            f["r2"] = _bdot(f["dboth"], f["k"], NN)
            f["dk0"] = _bdot(f["dboth"], f["kbq"], TN)
        for f in hs:
            h, k, beta = f["h"], f["k"], f["beta"]
            dkb = f["r2"][:CHUNK] + f["dkbg"] * f["eg"]
            dq = f["r2"][CHUNK:] + f["dqd"] * f["eg"]
            dk = f["dk0"] + f["dke"] * f["ekl"] + dkb * beta
            dke_ke = jnp.sum(f["dke"] * f["ke"], axis=-1, keepdims=True)
            dgc = (jnp.sum(f["ddecay"], axis=-1, keepdims=True)
                   + jnp.sum(f["dqd"] * f["qd"], axis=-1, keepdims=True) - dke_ke
                   + jnp.sum(f["dkbg"] * f["kbg"], axis=-1, keepdims=True))
            dgl = jnp.sum(dke_ke, axis=0, keepdims=True) + f["ddec"] * f["dec"]
            dgc = dgc + jnp.where(last_row, dgl, 0.0)
            dbeta = jnp.sum(dkb * k, axis=-1, keepdims=True) + jnp.sum(f["dvb"] * f["v"], axis=-1, keepdims=True)
            dgc_tile = dgc_tile + jnp.where(lane == 8 + h, dgc, 0.0)
            dbeta_tile = dbeta_tile + jnp.where(lane == h, dbeta, 0.0)
            dqn = dq * (A_HEAD ** -0.5)
            rq, rk, q_raw, k_raw = f["rq"], f["rk"], f["q_raw"], f["k_raw"]
            dqkv_ref[:, f["ln"]] = rq * dqn - q_raw * (rq * rq * rq) * jnp.sum(dqn * q_raw, axis=-1, keepdims=True)
            dqkv_ref[:, f["lk"]] = rk * dk - k_raw * (rk * rk * rk) * jnp.sum(dk * k_raw, axis=-1, keepdims=True)
            dqkv_ref[:, f["lv"]] = f["dvb"] * beta
        ddecays = [f["ddecay"] for f in hs]
        col_sums = _dot(jnp.concatenate(ddecays, axis=1), ones, TN, HIGHEST)
        for h in range(A_HEADS):
            dgc_tile = dgc_tile - jnp.where(lane == 8 + h, col_sums[h * CHUNK:(h + 1) * CHUNK, :1], 0.0)
        upper = (row <= col).astype(F32)
        dg = _dot(upper, dgc_tile, NN, HIGHEST)
        da_raw = dg * neg_a * _sigmoid(bat + dtb_ref[...])
        dba_ref[...] = jnp.where(lane < 8, dbeta_tile * sig * (1.0 - sig),
                                 jnp.where(lane < 16, da_raw, 0.0)).astype(BF16)
        dwn_ref[...] += dwn_acc
        ddtb_ref[...] += jnp.sum(da_raw, axis=0, keepdims=True)
        dalog_ref[...] += jnp.sum(dg * g, axis=0, keepdims=True)

    rev = lambda n: (nc - 1 - n, 0)
    vec = pl.BlockSpec((1, LANE), lambda n: (0, 0))
    wide = pl.BlockSpec((CHUNK, A_QK), rev)
    qkv_blk = pl.BlockSpec((CHUNK, A_CONV_WIDTH), rev)
    ba_blk = pl.BlockSpec((CHUNK, LANE), rev)
    vsh = jax.ShapeDtypeStruct((1, LANE), F32)
    return pl.pallas_call(
        body, name="gdn_bwd", grid=(nc,),
        in_specs=[qkv_blk, ba_blk, wide, wide, wide,
                  pl.BlockSpec((1, A_HEADS, A_HEAD, A_HEAD), lambda n: (nc - 1 - n, 0, 0, 0)),
                  pl.BlockSpec((1, A_HEADS, CHUNK, CHUNK), lambda n: (nc - 1 - n, 0, 0, 0)), wide, wide,
                  vec, vec, vec],
        out_specs=[qkv_blk, ba_blk, wide, vec, vec, vec],
        out_shape=[jax.ShapeDtypeStruct((s, A_CONV_WIDTH), F32), jax.ShapeDtypeStruct((s, LANE), BF16),
                   jax.ShapeDtypeStruct((s, A_QK), BF16), vsh, vsh, vsh],
        scratch_shapes=[pltpu.VMEM((A_HEADS, A_HEAD, A_HEAD), F32)],
        compiler_params=_params(("arbitrary",)),
    )(qkv, ba, z, o_raw, dy, states, t_all, w_all, vn_all, alog, dtb, wn)


REL_RING = 1024
QBLK_BITS = 8


def _rel_ring_onehot():
    m = lax.broadcasted_iota(jnp.int32, (REL_RING, REL_PAD), 0)
    t = lax.broadcasted_iota(jnp.int32, (REL_RING, REL_PAD), 1)
    u = jnp.where(m < KBLK, m, m - REL_RING)
    idx = jnp.clip(LEFT - u, -REL_CLIP, REL_CLIP) + REL_CLIP
    return (t == idx).astype(F32)


def _relbias_ring(table, transpose):
    n_in, n_out = (REL_RING, REL_PAD) if transpose else (REL_PAD, REL_RING)

    def body(t_ref, o_ref):
        o_ref[...] = _dot(t_ref[...], _rel_ring_onehot(), NN if transpose else NT, HIGHEST)

    return pl.pallas_call(
        body, name="relbias_ring_bwd" if transpose else "relbias_ring",
        out_shape=jax.ShapeDtypeStruct((B_HEADS, n_out), F32),
        in_specs=[VMEM_FULL], out_specs=VMEM_FULL,
        compiler_params=_params(),
    )(table)


def _row_bit(shape, bit):
    return ((lax.broadcasted_iota(jnp.int32, shape, 0) >> bit) & 1) == 1


def _relbias_expand(ring):
    def body(r_ref, o_ref):
        b = jnp.broadcast_to(r_ref[0], (QBLK, REL_RING))
        for bit in range(QBLK_BITS):
            b = jnp.where(_row_bit(b.shape, bit), pltpu.roll(b, 1 << bit, 1), b)
        j = lax.broadcasted_iota(jnp.int32, (QBLK, KBLK), 1)
        r = lax.broadcasted_iota(jnp.int32, (QBLK, KBLK), 0)
        lo = (r >> 6) << 6
        o_ref[0] = jnp.where((j >= lo) & (j < lo + LEFT + CHUNK), b[:, :KBLK], NEG_INF)

    return pl.pallas_call(
        body, name="relbias_expand", grid=(B_HEADS,),
        in_specs=[pl.BlockSpec((1, 1, REL_RING), lambda h: (h, 0, 0))],
        out_specs=pl.BlockSpec((1, QBLK, KBLK), lambda h: (h, 0, 0)),
        out_shape=jax.ShapeDtypeStruct((B_HEADS, QBLK, KBLK), F32),
        compiler_params=_params(("parallel",)),
    )(ring)


def _relbias_reduce(ds):
    def body(d_ref, o_ref):
        d = jnp.concatenate([d_ref[0], jnp.zeros((QBLK, REL_RING - KBLK), F32)], axis=1)
        for bit in range(QBLK_BITS):
            d = jnp.where(_row_bit(d.shape, bit), pltpu.roll(d, REL_RING - (1 << bit), 1), d)
        o_ref[0] = jnp.sum(d, axis=0, keepdims=True)

    return pl.pallas_call(
        body, name="relbias_reduce", grid=(B_HEADS,),
        in_specs=[pl.BlockSpec((1, QBLK, KBLK), lambda h: (h, 0, 0))],
        out_specs=pl.BlockSpec((1, 1, REL_RING), lambda h: (h, 0, 0)),
        out_shape=jax.ShapeDtypeStruct((B_HEADS, 1, REL_RING), F32),
        compiler_params=_params(("parallel",)),
    )(ds)


def _attn_probs(q_ref, kb, b_ref, hh, q0):
    hl = slice(hh * B_HEAD, (hh + 1) * B_HEAD)
    qh = q_ref[:, hl] * (B_HEAD ** -0.5)
    kh = kb[:, hl]
    jpos = lax.broadcasted_iota(jnp.int32, (QBLK, KBLK), 1)
    sc = _bdot(qh, kh, NT) + b_ref[hh]
    sc = jnp.where(jpos + q0 >= LEFT, sc, NEG_INF)
    mx = jnp.max(sc, axis=-1, keepdims=True)
    p = jnp.exp(sc - mx)
    return p * (1.0 / jnp.sum(p, axis=-1, keepdims=True)), qh, kh


def _attn_fwd(q, kpad, vpad, bias):
    s = q.shape[0]

    def body(q_ref, k_ref, v_ref, b_ref, o_ref):
        q0 = pl.multiple_of(pl.program_id(1) * QBLK, QBLK)
        kb = k_ref[pl.ds(q0, KBLK), :]
        vb = v_ref[pl.ds(q0, KBLK), :]
        outs = []
        for hh in range(2):
            p, _, _ = _attn_probs(q_ref, kb, b_ref, hh, q0)
            outs.append(_bdot(p, vb[:, hh * B_HEAD:(hh + 1) * B_HEAD], NN))
        o_ref[...] = jnp.concatenate(outs, axis=1).astype(BF16)

    qblk = pl.BlockSpec((QBLK, LANE), lambda g, m: (m, g))
    kblk = pl.BlockSpec((LEFT + s, LANE), lambda g, m: (0, g))
    return pl.pallas_call(
        body, name="attn_fwd", grid=(B_HEADS // 2, s // QBLK),
        in_specs=[qblk, kblk, kblk, pl.BlockSpec((2, QBLK, KBLK), lambda g, m: (g, 0, 0))],
        out_specs=qblk,
        out_shape=jax.ShapeDtypeStruct((s, D_MODEL), BF16),
        compiler_params=_params(("parallel", "arbitrary")),
    )(q, kpad, vpad, bias)


def _attn_bwd(q, kpad, vpad, bias, do):
    s = q.shape[0]

    def body(q_ref, k_ref, v_ref, b_ref, do_ref, dq_ref, dk_ref, dv_ref, db_ref):
        @pl.when(pl.program_id(1) == 0)
        def _():
            for r in (dk_ref, dv_ref, db_ref):
                r[...] = jnp.zeros_like(r)

        q0 = pl.multiple_of(pl.program_id(1) * QBLK, QBLK)
        kb = k_ref[pl.ds(q0, KBLK), :]
        vb = v_ref[pl.ds(q0, KBLK), :]
        dqs, dks, dvs = [], [], []
        for hh in range(2):
            hl = slice(hh * B_HEAD, (hh + 1) * B_HEAD)
            p, qh, kh = _attn_probs(q_ref, kb, b_ref, hh, q0)
            doh = do_ref[:, hl]
            dp = _bdot(doh, vb[:, hl], NT)
            dsc = p * (dp - jnp.sum(p * dp, axis=-1, keepdims=True))
            db_ref[hh] += dsc
            dqs.append(_bdot(dsc, kh, NN) * (B_HEAD ** -0.5))
            dks.append(_bdot(dsc, qh, TN))
            dvs.append(_bdot(p, doh, TN))
        dq_ref[...] = jnp.concatenate(dqs, axis=1).astype(BF16)
        dk_ref[pl.ds(q0, KBLK), :] += jnp.concatenate(dks, axis=1)
        dv_ref[pl.ds(q0, KBLK), :] += jnp.concatenate(dvs, axis=1)

    qblk = pl.BlockSpec((QBLK, LANE), lambda g, m: (m, g))
    kblk = pl.BlockSpec((LEFT + s, LANE), lambda g, m: (0, g))
    bblk = pl.BlockSpec((2, QBLK, KBLK), lambda g, m: (g, 0, 0))
    return pl.pallas_call(
        body, name="attn_bwd", grid=(B_HEADS // 2, s // QBLK),
        in_specs=[qblk, kblk, kblk, bblk, qblk],
        out_specs=[qblk, kblk, kblk, bblk],
        out_shape=[jax.ShapeDtypeStruct((s, D_MODEL), BF16), jax.ShapeDtypeStruct((LEFT + s, D_MODEL), F32),
                   jax.ShapeDtypeStruct((LEFT + s, D_MODEL), F32),
                   jax.ShapeDtypeStruct((B_HEADS, QBLK, KBLK), F32)],
        compiler_params=_params(("parallel", "arbitrary")),
    )(q, kpad, vpad, bias, do)


def _adamw(w, g, m, v):
    r, c = w.shape
    tr = r
    for cand in (512, 256, 128, 64, 32, 16, 8):
        if r % cand == 0 and cand * c * 4 <= 2 * 1024 * 1024:
            tr = cand
            break
    c1 = 1.0 / (1.0 - ADAM_B1 ** ADAM_STEP)
    c2 = 1.0 / (1.0 - ADAM_B2 ** ADAM_STEP)

    def body(w_ref, g_ref, m_ref, v_ref, d_ref, mo_ref, vo_ref):
        gv = g_ref[...]
        mn = ADAM_B1 * m_ref[...] + (1.0 - ADAM_B1) * gv
        vn = ADAM_B2 * v_ref[...] + (1.0 - ADAM_B2) * (gv * gv)
        mo_ref[...] = mn
        vo_ref[...] = vn
        d_ref[...] = -ADAM_LR * ((mn * c1) / (jnp.sqrt(vn * c2) + ADAM_EPS) + ADAM_WD * w_ref[...])

    blk = pl.BlockSpec((tr, c), lambda i: (i, 0))
    sh = jax.ShapeDtypeStruct((r, c), F32)
    return pl.pallas_call(
        body, name="adamw", grid=(r // tr,),
        in_specs=[blk] * 4, out_specs=[blk] * 3, out_shape=[sh] * 3,
        compiler_params=_params(("parallel",)),
    )(w, g, m, v)


def _row(v, width=None):
    v = v.reshape(1, -1)
    if width is not None and v.shape[1] < width:
        v = jnp.pad(v, ((0, 0), (0, width - v.shape[1])))
    return v


def _gate_row(v):
    return jnp.pad(v.reshape(1, A_HEADS), ((0, 0), (A_HEADS, LANE - 2 * A_HEADS)))


DEPTH = 4
N_A = 2
N_B = 2
F_DOWN_ROWS = FFN_DIM // N_SHARDS
SQ_ROWS = D_MODEL // N_SHARDS
GD_B_Q0 = DEPTH * F_DOWN_ROWS // SQ_ROWS
GD_B_OUT0 = GD_B_Q0 + N_B
UP_COLS = 2 * FFN_DIM // N_SHARDS


def _a_layer_fwd(h, w, i, pending=None):
    xn = _rmsnorm_fwd(h, _row(w["a_norm"][i]))
    w_qkv, w_z, w_ba = w["a_in"][i]
    pre = _mm(xn, w_qkv, name="a_qkv")
    z = _mm(xn, w_z, name="a_z")
    ba = _mm(xn, w_ba, name="a_ba")
    act = _conv_silu_fwd(pre, w["a_conv"][i])
    alog, dtb, wn = _gate_row(w["a_A_log"][i]), _gate_row(w["a_dt_bias"][i]), _row(w["a_out_norm"][i])
    if pending is None:
        y, o_raw, states, t_all, w_all, vn_all = _gdn_fwd(act, ba, z, alog, dtb, wn)
    else:
        names = list(pending)
        y, o_raw, states, t_all, w_all, vn_all, *landed = _gdn_fwd(act, ba, z, alog, dtb, wn,
                                                                   gather=[pending[n] for n in names])
        w.update(zip(names, landed))
    h2 = _mm_rowsh(y, w["GA"], SQ_ROWS, i, "nn", "a_out", res=h)
    saved = dict(h=h, xn=xn, pre=pre, z=z, ba=ba, act=act, o_raw=o_raw, y=y, states=states,
                 t_all=t_all, w_all=w_all, vn_all=vn_all, alog=alog, dtb=dtb, wn=wn)
    return h2, saved


def _a_layer_bwd(dh2, w, i, sv, acc):
    g = {}
    acc["GA"] = _mm_rowsh_dw(sv["y"], dh2, SQ_ROWS, "a_out_dw", acc["GA"], i)
    dy = _mm_rowsh(dh2, w["GA"], SQ_ROWS, i, "nt", "a_out_dx")
    dact, dba, dz, dalog, ddtb, dwn = _gdn_bwd(sv["act"], sv["ba"], sv["z"], sv["o_raw"], dy, sv["states"],
                                               sv["t_all"], sv["w_all"], sv["vn_all"],
                                               sv["alog"], sv["dtb"], sv["wn"])
    dpre, g["conv"] = _conv_silu_bwd(sv["pre"], w["a_conv"][i], dact)
    xn = sv["xn"]
    w_qkv, w_z, w_ba = w["a_in"][i]
    d_in = jnp.concatenate([_mm(xn, dpre, "tn", name="a_qkv_dw"), _mm(xn, dz, "tn", name="a_z_dw"),
                            _mm(xn, dba, "tn", name="a_ba_dw")[:, :2 * A_HEADS]], axis=1)
    g["w_in"] = jnp.transpose(d_in.reshape(D_MODEL, N_SHARDS, -1), (1, 0, 2))
    dxn = _mm(dpre, w_qkv, "nt", name="a_qkv_dx")
    dxn = _mm(dz, w_z, "nt", res=dxn, name="a_z_dx")
    dh, dnorm = _mm(dba, w_ba, "nt", res=dxn, name="a_ba_dx", norm=(sv["h"], _row(w["a_norm"][i]), dh2))
    g["norm"] = dnorm[0]
    g["A_log"] = dalog[0, A_HEADS:2 * A_HEADS]
    g["dt_bias"] = ddtb[0, A_HEADS:2 * A_HEADS]
    g["out_norm"] = dwn[0]
    return dh, g


def _by_half(a, lead):
    return jnp.moveaxis(a.reshape(a.shape[:-1] + (2, 2, UP_COLS)), (-3, -2), (0, 1)).reshape((2, 2) + lead + (UP_COLS,))


def _from_half(a):
    lead = a.shape[2:-1]
    return jnp.moveaxis(a, (0, 1), (-3, -2)).reshape(lead + (N_SHARDS * UP_COLS,))


def _ffn_fwd(h, w, l):
    s = h.shape[0]
    xn = _rmsnorm_fwd(h, _row(w["f_norm"][l]))
    cw = _by_half(w["f_conv"][l], (w["f_conv"].shape[1],))
    cb = _by_half(w["f_conv_b"][l][None], (1,))
    pre = _mm_colsh(xn, w["GU"], D_MODEL, l, "nn", "f_up").reshape(2, 2, s, UP_COLS)
    act = _ffn_act_fwd(pre, cw, cb)
    h2 = _mm_rowsh(act, w["GD"], F_DOWN_ROWS, l, "nn", "f_down", res=h)
    return h2, dict(h=h, xn=xn, pre=pre, act=act, cw=cw, cb=cb)


def _ffn_bwd(dh2, w, l, sv, acc):
    g = {}
    s = dh2.shape[0]
    acc["GD"] = _mm_rowsh_dw(sv["act"], dh2, F_DOWN_ROWS, "f_down_dw", acc["GD"], l)
    dact = _mm_rowsh(dh2, w["GD"], F_DOWN_ROWS, l, "nt", "f_down_dx")
    dpre, dcw, dcb = _ffn_act_bwd(sv["pre"], sv["cw"], sv["cb"], dact)
    dpre = dpre.reshape(N_SHARDS, s, UP_COLS)
    acc["GU"] = _mm_colsh_dw(sv["xn"], dpre, "f_up_dw", into=acc["GU"], blk0=l)
    g["conv"] = _from_half(dcw)
    g["conv_b"] = _from_half(dcb)[0]
    dh, dnorm = _mm_colsh(dpre, w["GU"], D_MODEL, l, "nt", "f_up_dx", norm=(sv["h"], _row(w["f_norm"][l]), dh2))
    g["norm"] = dnorm[0]
    return dh, g


def _b_layer_fwd(h, w, j, kpad, vpad):
    xn = _rmsnorm_fwd(h, _row(w["b_norm"][j]))
    q = _mm_rowsh(xn, w["GD"], SQ_ROWS, GD_B_Q0 + j, "nn", "b_q", out_dtype=BF16)
    rel = w["b_rel_bias"][j]
    table = jnp.pad(rel, ((0, 0), (0, REL_PAD - rel.shape[1])))
    bias = _relbias_expand(_relbias_ring(table, False).reshape(B_HEADS, 1, REL_RING))
    o = _attn_fwd(q, kpad, vpad, bias)
    h2 = _mm_rowsh(o, w["GD"], SQ_ROWS, GD_B_OUT0 + j, "nn", "b_out", res=h)
    return h2, dict(h=h, xn=xn, q=q, o=o, bias=bias)


def _b_layer_bwd(dh2, w, j, sv, kpad, vpad, acc):
    g = {}
    acc["GD"] = _mm_rowsh_dw(sv["o"], dh2, SQ_ROWS, "b_out_dw", acc["GD"], GD_B_OUT0 + j)
    do = _mm_rowsh(dh2, w["GD"], SQ_ROWS, GD_B_OUT0 + j, "nt", "b_out_dx", out_dtype=BF16)
    dq, dkp, dvp, dsc = _attn_bwd(sv["q"], kpad, vpad, sv["bias"], do)
    dring = _relbias_reduce(dsc).reshape(B_HEADS, REL_RING)
    g["rel_bias"] = _relbias_ring(dring, True)[:, :2 * REL_CLIP + 1]
    acc["GD"] = _mm_rowsh_dw(sv["xn"], dq, SQ_ROWS, "b_q_dw", acc["GD"], GD_B_Q0 + j)
    dh, dnorm = _mm_rowsh(dq, w["GD"], SQ_ROWS, GD_B_Q0 + j, "nt", "b_q_dx",
                          norm=(sv["h"], _row(w["b_norm"][j]), dh2))
    g["norm"] = dnorm[0]
    return dh, g, dkp, dvp


def _local_step(x, tgt, w, pending=None):
    w = dict(w)
    h = x
    saved = []
    kv_saved = None
    kpad = vpad = None
    for layer in range(DEPTH):
        if layer < N_A:
            h, sm = _a_layer_fwd(h, w, layer, pending if layer == 0 else None)
        else:
            if layer == N_A:
                xn_kv = _rmsnorm_fwd(h, _row(w["kv_norm"]))
                kv = _mm_colsh(xn_kv, w["GK"], D_MODEL, 0, "nn", "kv", flat=True, out_dtype=BF16)
                kpad = jnp.pad(kv[:, :D_MODEL], ((LEFT, 0), (0, 0)))
                vpad = jnp.pad(kv[:, D_MODEL:], ((LEFT, 0), (0, 0)))
                kv_saved = dict(h=h, xn=xn_kv)
            h, sm = _b_layer_fwd(h, w, layer - N_A, kpad, vpad)
        h, sf = _ffn_fwd(h, w, layer)
        saved.append((sm, sf))

    loss, dh, dfinal = _final_loss(h, _row(w["final_norm"]), tgt)

    ga = [None] * N_A
    gb = [None] * N_B
    gf = [None] * DEPTH
    dk_tot = dv_tot = None
    g_kv = g_kvn = None
    acc = dict(GU=lax.empty((N_SHARDS, DEPTH * D_MODEL, UP_COLS), F32),
               GD=lax.empty((N_SHARDS, (GD_B_OUT0 + N_B) * SQ_ROWS, D_MODEL), F32),
               GA=lax.empty((N_SHARDS, N_A * SQ_ROWS, D_MODEL), F32))
    for layer in reversed(range(DEPTH)):
        sm, sf = saved[layer]
        dh, gf[layer] = _ffn_bwd(dh, w, layer, sf, acc)
        if layer >= N_A:
            dh, gb[layer - N_A], dkp, dvp = _b_layer_bwd(dh, w, layer - N_A, sm, kpad, vpad, acc)
            dk_tot = dkp if dk_tot is None else dk_tot + dkp
            dv_tot = dvp if dv_tot is None else dv_tot + dvp
            if layer == N_A:
                dkv = jnp.concatenate([dk_tot[LEFT:], dv_tot[LEFT:]], axis=1).astype(BF16)
                g_kv = _mm_colsh_dw(kv_saved["xn"], dkv, "kv_dw", flat=True)
                dh, g_kvn = _mm_colsh(dkv, w["GK"], D_MODEL, 0, "nt", "kv_dx", flat=True,
                                      norm=(kv_saved["h"], _row(w["kv_norm"]), dh))
        else:
            dh, ga[layer] = _a_layer_bwd(dh, w, layer, sm, acc)

    def stack(gs, key):
        return jnp.stack([g[key] for g in gs])

    grads = dict(
        GU=acc["GU"], GD=acc["GD"], GA=acc["GA"], GK=g_kv,
        GI=jnp.concatenate([g["w_in"] for g in ga], axis=1),
        a_norm=stack(ga, "norm"), a_conv=stack(ga, "conv"), a_A_log=stack(ga, "A_log"),
        a_dt_bias=stack(ga, "dt_bias"), a_out_norm=stack(ga, "out_norm"), kv_norm=g_kvn[0],
        b_norm=stack(gb, "norm"), b_rel_bias=stack(gb, "rel_bias"),
        f_norm=stack(gf, "norm"), f_conv=stack(gf, "conv"), f_conv_b=stack(gf, "conv_b"), final_norm=dfinal[0])
    return loss, dh, grads


HBM_SPEC = pl.BlockSpec(memory_space=pl.ANY)
VMEM_SPEC = pl.BlockSpec(memory_space=pltpu.VMEM)


def _place():
    x, y, c = lax.axis_index("x"), lax.axis_index("y"), lax.axis_index("c")
    chips = [(1 - x, y), (x, 1 - y), (1 - x, 1 - y)]
    return x, y, c, chips


def _remote(src, dst, send_sem, recv_sem, to):
    return pltpu.make_async_remote_copy(src_ref=src, dst_ref=dst, send_sem=send_sem, recv_sem=recv_sem,
                                        device_id=to, device_id_type=MESH)


GATHER_COPIES = 7


class _ShardGather:
    def __init__(self, x_refs, out_refs, send_sems, recv_sems):
        n = len(x_refs)
        x, y, c, chips = _place()
        sibling = (x, y, 1 - c)

        def half(a, px, py, hc):
            rh = x_refs[a].shape[0] // 2
            return out_refs[a].at[2 * px + py, pl.ds(hc * rh, rh), :]

        def mine(a):
            rh = x_refs[a].shape[0] // 2
            return x_refs[a].at[pl.ds(c * rh, rh), :]

        def sems(a, k):
            return send_sems.at[GATHER_COPIES * a + k], recv_sems.at[GATHER_COPIES * a + k]

        order = [(j, chip, a) for j, chip in enumerate(chips) for a in range(n)]
        self.first = [_remote(mine(a), half(a, x, y, c), *sems(a, j), (*chip, c)) for j, chip, a in order]
        self.own = [_remote(x_refs[a], out_refs[a].at[2 * x + y], *sems(a, 6), sibling) for a in range(n)]
        self.landed = [_remote(half(a, *chip, c), half(a, *chip, c), *sems(a, j), (*chip, c)) for j, chip, a in order]
        self.passed = [_remote(half(a, *chip, c), half(a, *chip, c), *sems(a, 3 + j), sibling)
                       for j, chip, a in order]
        self.theirs = [_remote(half(a, *chip, 1 - c), half(a, *chip, 1 - c), *sems(a, 3 + j), sibling)
                       for j, chip, a in order]

    def start(self):
        for cp in self.first + self.own:
            cp.start()

    def finish(self):
        for arrived, onward in zip(self.landed, self.passed):
            arrived.wait_recv()
            onward.start()
        for cp in self.theirs + self.own:
            cp.wait_recv()
        for cp in self.first + self.passed + self.own:
            cp.wait_send()


def _gather_scratch(n):
    return [pltpu.SemaphoreType.DMA((GATHER_COPIES * n,)), pltpu.SemaphoreType.DMA((GATHER_COPIES * n,))]


def _allgather_weights(shards):
    n = len(shards)

    def body(*refs):
        gather = _ShardGather(refs[:n], refs[n:2 * n], *refs[2 * n:])
        gather.start()
        gather.finish()

    return pl.pallas_call(
        body, name="allgather_weights",
        out_shape=[jax.ShapeDtypeStruct((N_SHARDS,) + sh.shape, sh.dtype) for sh in shards],
        in_specs=[HBM_SPEC] * n, out_specs=[HBM_SPEC] * n,
        scratch_shapes=_gather_scratch(n),
    )(*shards)


def _pair_exchange(gs):
    n = len(gs)

    def body(*refs):
        g_refs, out_refs, (send_sems, recv_sems) = refs[:n], refs[n:2 * n], refs[2 * n:]
        x, y, c, _ = _place()
        cps = []
        for a in range(n):
            rh = gs[a].shape[1] // 2
            cps.append(_remote(g_refs[a].at[:, pl.ds((1 - c) * rh, rh), :], out_refs[a], send_sems.at[a],
                               recv_sems.at[a], (x, y, 1 - c)))
        for cp in cps:
            cp.start()
        for cp in cps:
            cp.wait()

    return pl.pallas_call(
        body, name="rs_pair_exchange",
        out_shape=[jax.ShapeDtypeStruct((g.shape[0], g.shape[1] // 2, g.shape[2]), g.dtype) for g in gs],
        in_specs=[HBM_SPEC] * n, out_specs=[HBM_SPEC] * n,
        scratch_shapes=[pltpu.SemaphoreType.DMA((n,)), pltpu.SemaphoreType.DMA((n,))],
    )(*gs)


def _add_rows(rows, cols):
    best = 16
    for t in range(16, rows + 1, 16):
        if rows % t == 0 and t * cols * 4 <= 2304 * 1024:
            best = t
    return best


def _pair_add(g, other, c_idx):
    n, r, cols = g.shape
    rh = r // 2
    tr = _add_rows(rh, cols)
    nb = rh // tr

    def body(c_ref, a_ref, b_ref, o_ref, ob_ref):
        sm = a_ref[...] + b_ref[...]
        o_ref[...] = sm
        ob_ref[...] = sm.astype(BF16)

    out_blk = pl.BlockSpec((1, tr, cols), lambda s, i, c_ref: (s, i, 0))
    return pl.pallas_call(
        body, name="rs_pair_add",
        grid_spec=pltpu.PrefetchScalarGridSpec(
            num_scalar_prefetch=1, grid=(n, nb),
            in_specs=[pl.BlockSpec((1, tr, cols), lambda s, i, c_ref: (s, c_ref[0] * nb + i, 0)), out_blk],
            out_specs=[out_blk, out_blk]),
        out_shape=[jax.ShapeDtypeStruct((n, rh, cols), F32), jax.ShapeDtypeStruct((n, rh, cols), BF16)],
        compiler_params=_params(("parallel", "parallel")),
    )(c_idx, g, other)


def _chip_exchange(ps):
    n = len(ps)

    def body(*refs):
        p_refs, out_refs, (send_sems, recv_sems) = refs[:n], refs[n:2 * n], refs[2 * n:]
        x, y, c, chips = _place()
        cps = [_remote(p_refs[a].at[2 * chip[0] + chip[1]], out_refs[a].at[j], send_sems.at[3 * a + j],
                       recv_sems.at[3 * a + j], (*chip, c))
               for a in range(n) for j, chip in enumerate(chips)]
        for cp in cps:
            cp.start()
        for cp in cps:
            cp.wait()

    return pl.pallas_call(
        body, name="rs_chip_exchange",
        out_shape=[jax.ShapeDtypeStruct((3,) + p.shape[1:], p.dtype) for p in ps],
        in_specs=[HBM_SPEC] * n, out_specs=[HBM_SPEC] * n,
        scratch_shapes=[pltpu.SemaphoreType.DMA((3 * n,)), pltpu.SemaphoreType.DMA((3 * n,))],
    )(*ps)


def _chip_add(p, recv, chip_idx):
    n, rh, cols = p.shape
    tr = _add_rows(rh, cols)

    def body(s_ref, own_ref, r_ref, o_ref):
        o_ref[...] = ((own_ref[0] + r_ref[0].astype(F32)) + r_ref[1].astype(F32)) + r_ref[2].astype(F32)

    return pl.pallas_call(
        body, name="rs_chip_add",
        grid_spec=pltpu.PrefetchScalarGridSpec(
            num_scalar_prefetch=1, grid=(rh // tr,),
            in_specs=[pl.BlockSpec((1, tr, cols), lambda i, s_ref: (s_ref[0], i, 0)),
                      pl.BlockSpec((3, tr, cols), lambda i, s_ref: (0, i, 0))],
            out_specs=pl.BlockSpec((tr, cols), lambda i, s_ref: (i, 0))),
        out_shape=jax.ShapeDtypeStruct((rh, cols), p.dtype),
        compiler_params=_params(("parallel",)),
    )(chip_idx, p, recv)


def _pair_gather(fs):
    n = len(fs)

    def body(*refs):
        f_refs, out_refs, (send_sems, recv_sems) = refs[:n], refs[n:2 * n], refs[2 * n:]
        x, y, c, _ = _place()
        cps = [_remote(f_refs[a], out_refs[a], send_sems.at[a], recv_sems.at[a], (x, y, 1 - c)) for a in range(n)]
        for cp in cps:
            cp.start()
        for cp in cps:
            cp.wait()

    return pl.pallas_call(
        body, name="rs_pair_gather",
        out_shape=[jax.ShapeDtypeStruct(f.shape, f.dtype) for f in fs],
        in_specs=[HBM_SPEC] * n, out_specs=[HBM_SPEC] * n,
        scratch_shapes=[pltpu.SemaphoreType.DMA((n,)), pltpu.SemaphoreType.DMA((n,))],
    )(*fs)


def _allreduce_small(v):
    r, cols = v.shape

    def body(x_ref, out_ref, slots, send_sems, recv_sems):
        x, y, c, _ = _place()
        bits = [(bx, by, bc) for bx in (0, 1) for by in (0, 1) for bc in (0, 1)]

        def flip(b):
            return (1 - x if b[0] else x, 1 - y if b[1] else y, 1 - c if b[2] else c)

        slots[0] = x_ref[...]
        cps = [_remote(x_ref, slots.at[k], send_sems.at[k - 1], recv_sems.at[k - 1], flip(bits[k]))
               for k in range(1, 8)]
        for cp in cps:
            cp.start()
        for cp in cps:
            cp.wait()
        acc = None
        for b in bits:
            fx, fy, fc = flip(b)
            term = slots[4 * fx + 2 * fy + fc]
            acc = term if acc is None else acc + term
        out_ref[...] = acc

    return pl.pallas_call(
        body, name="allreduce_small",
        out_shape=jax.ShapeDtypeStruct((r, cols), v.dtype),
        in_specs=[VMEM_SPEC], out_specs=VMEM_SPEC,
        scratch_shapes=[pltpu.VMEM((8, r, cols), v.dtype), pltpu.SemaphoreType.DMA((7,)),
                        pltpu.SemaphoreType.DMA((7,))],
        compiler_params=pltpu.CompilerParams(vmem_limit_bytes=VMEM_LIMIT),
    )(v)


BIG = (("a_w_in", 2), ("a_w_out", 1), ("w_kv", 1), ("b_w_q", 1), ("b_w_out", 1), ("f_w_up", 2), ("f_w_down", 1))
SMALL = (("a_norm", 1), ("a_conv", 2), ("a_A_log", None), ("a_dt_bias", None), ("a_out_norm", None),
         ("kv_norm", None), ("b_norm", None), ("b_rel_bias", None), ("f_norm", None), ("f_conv", 2),
         ("f_conv_b", None), ("final_norm", None))
WEIGHT_ORDER = ("a_norm", "a_w_in", "a_conv", "a_A_log", "a_dt_bias", "a_out_norm", "a_w_out", "kv_norm", "w_kv",
                "b_norm", "b_w_q", "b_rel_bias", "b_w_out", "f_norm", "f_w_up", "f_conv", "f_conv_b", "f_w_down",
                "final_norm")


def _pad_rows(flat, cols, quantum):
    n = flat.shape[-1]
    rows = -(-n // (cols * quantum)) * quantum
    pad = [(0, 0)] * (flat.ndim - 1) + [(0, rows * cols - n)]
    return jnp.pad(flat, pad).reshape(flat.shape[:-1] + (rows, cols))


GROUPS = ("GU", "GD", "GK", "GI", "GA")
FIRST_GROUPS = ("GI", "GA")
GD_MEMBERS = ("f_w_down", "b_w_q", "b_w_out")


def _group_shards(w, dtype):
    def two(a):
        return a.reshape(-1, a.shape[-1])

    return dict(GU=two(w["f_w_up"]).astype(dtype),
                GD=jnp.concatenate([two(w[n]) for n in GD_MEMBERS]).astype(dtype),
                GK=w["w_kv"].astype(dtype),
                GI=two(w["a_w_in"]).astype(dtype),
                GA=two(w["a_w_out"]).astype(dtype))


def _ungroup(red, shard_shapes):
    out = dict(f_w_up=red["GU"].reshape(shard_shapes["f_w_up"]), w_kv=red["GK"],
               a_w_in=red["GI"].reshape(shard_shapes["a_w_in"]), a_w_out=red["GA"].reshape(shard_shapes["a_w_out"]))
    off = 0
    for n in GD_MEMBERS:
        shp = shard_shapes[n]
        rows = math.prod(shp[:-1])
        out[n] = red["GD"][off:off + rows].reshape(shp)
        off += rows
    return out


def _dense_a_in(gi):
    out = []
    for i in range(N_A):
        full = jnp.transpose(gi[:, i * D_MODEL:(i + 1) * D_MODEL], (1, 0, 2)).reshape(D_MODEL, -1)
        out.append((full[:, :A_CONV_WIDTH], full[:, A_CONV_WIDTH:A_CONV_WIDTH + A_QK],
                    jnp.pad(full[:, A_CONV_WIDTH + A_QK:], ((0, 0), (0, LANE - 2 * A_HEADS)))))
    return out


def _pack_small(values, names):
    return _pad_rows(jnp.concatenate([values[n].reshape(-1) for n in names]), LANE, SUB)


def _unpack_small(packed, shapes, names):
    flat = packed.reshape(-1)
    out, off = {}, 0
    for n in names:
        size = math.prod(shapes[n])
        out[n] = flat[off:off + size].reshape(shapes[n])
        off += size
    return out


def _adamw_nd(w, g, m, v):
    shp = w.shape
    two = (math.prod(shp[:-1]), shp[-1])
    d, mn, vn = _adamw(w.reshape(two), g.reshape(two), m.reshape(two), v.reshape(two))
    return d.reshape(shp), mn.reshape(shp), vn.reshape(shp)


def kernel(x, a_norm, a_w_in, a_conv, a_A_log, a_dt_bias, a_out_norm, a_w_out, kv_norm, w_kv, b_norm, b_w_q, b_rel_bias, b_w_out, f_norm, f_w_up, f_conv, f_conv_b, f_w_down, final_norm, loss_target, m_a_norm, m_a_w_in, m_a_conv, m_a_A_log, m_a_dt_bias, m_a_out_norm, m_a_w_out, m_kv_norm, m_w_kv, m_b_norm, m_b_w_q, m_b_rel_bias, m_b_w_out, m_f_norm, m_f_w_up, m_f_conv, m_f_conv_b, m_f_w_down, m_final_norm, v_a_norm, v_a_w_in, v_a_conv, v_a_A_log, v_a_dt_bias, v_a_out_norm, v_a_w_out, v_kv_norm, v_w_kv, v_b_norm, v_b_w_q, v_b_rel_bias, v_b_w_out, v_f_norm, v_f_w_up, v_f_conv, v_f_conv_b, v_f_w_down, v_final_norm):
    w = dict(a_norm=a_norm, a_w_in=a_w_in, a_conv=a_conv, a_A_log=a_A_log, a_dt_bias=a_dt_bias,
             a_out_norm=a_out_norm, a_w_out=a_w_out, kv_norm=kv_norm, w_kv=w_kv, b_norm=b_norm, b_w_q=b_w_q,
             b_rel_bias=b_rel_bias, b_w_out=b_w_out, f_norm=f_norm, f_w_up=f_w_up, f_conv=f_conv,
             f_conv_b=f_conv_b, f_w_down=f_w_down, final_norm=final_norm)
    m = dict(a_norm=m_a_norm, a_w_in=m_a_w_in, a_conv=m_a_conv, a_A_log=m_a_A_log, a_dt_bias=m_a_dt_bias,
             a_out_norm=m_a_out_norm, a_w_out=m_a_w_out, kv_norm=m_kv_norm, w_kv=m_w_kv, b_norm=m_b_norm,
             b_w_q=m_b_w_q, b_rel_bias=m_b_rel_bias, b_w_out=m_b_w_out, f_norm=m_f_norm, f_w_up=m_f_w_up,
             f_conv=m_f_conv, f_conv_b=m_f_conv_b, f_w_down=m_f_w_down, final_norm=m_final_norm)
    v = dict(a_norm=v_a_norm, a_w_in=v_a_w_in, a_conv=v_a_conv, a_A_log=v_a_A_log, a_dt_bias=v_a_dt_bias,
             a_out_norm=v_a_out_norm, a_w_out=v_a_w_out, kv_norm=v_kv_norm, w_kv=v_w_kv, b_norm=v_b_norm,
             b_w_q=v_b_w_q, b_rel_bias=v_b_rel_bias, b_w_out=v_b_w_out, f_norm=v_f_norm, f_w_up=v_f_w_up,
             f_conv=v_f_conv, f_conv_b=v_f_conv_b, f_w_down=v_f_w_down, final_norm=v_final_norm)
    xi, yi, ci = lax.axis_index("x"), lax.axis_index("y"), lax.axis_index("c")
    chip = 2 * xi + yi
    shard_shapes = {n: w[n].shape for n in WEIGHT_ORDER}

    mine_w = _group_shards(w, BF16)
    full = dict(zip(FIRST_GROUPS, _allgather_weights([mine_w[n] for n in FIRST_GROUPS])))
    full["a_in"] = _dense_a_in(full.pop("GI"))
    pending = {n: mine_w[n] for n in GROUPS if n not in FIRST_GROUPS}
    sharded_small = [n for n, axis in SMALL if axis is not None]
    placed = {}
    for n, axis in SMALL:
        if axis is not None:
            wide = list(w[n].shape)
            wide[axis] *= 4
            mine_once = w[n] * (1 - ci).astype(F32)
            placed[n] = lax.dynamic_update_slice_in_dim(jnp.zeros(wide, F32), mine_once, chip * w[n].shape[axis], axis)
    placed_shapes = {n: placed[n].shape for n in sharded_small}
    full.update(_unpack_small(_allreduce_small(_pack_small(placed, sharded_small)), placed_shapes, sharded_small))
    for n, axis in SMALL:
        if axis is None:
            full[n] = w[n]

    loss_part, grad_x, grads = _local_step(x[0], loss_target[0], full, pending)

    packed = [grads[n] for n in GROUPS]
    c_idx = jnp.reshape(ci, (1,)).astype(jnp.int32)
    chip_idx = jnp.reshape(chip, (1,)).astype(jnp.int32)
    pairs = [_pair_add(g, o, c_idx) for g, o in zip(packed, _pair_exchange(packed))]
    arrived = _chip_exchange([pb for _, pb in pairs])
    mine = [_chip_add(p, r, chip_idx) for (p, _), r in zip(pairs, arrived)]
    theirs = _pair_gather(mine)
    red = _ungroup({n: jnp.concatenate([jnp.where(ci == 0, a, b), jnp.where(ci == 0, b, a)], axis=0)
                    for n, a, b in zip(GROUPS, mine, theirs)}, shard_shapes)

    small_names = [n for n, _ in SMALL]
    small_vals = {n: grads[n] for n in small_names}
    small_vals["loss"] = loss_part[0, :1]
    names = ["loss"] + small_names
    shapes = {n: small_vals[n].shape for n in names}
    summed = _unpack_small(_allreduce_small(_pack_small(small_vals, names)), shapes, names)
    loss = summed["loss"][0]
    for n, axis in SMALL:
        g = summed[n]
        if axis is not None:
            g = lax.dynamic_slice_in_dim(g, chip * w[n].shape[axis], w[n].shape[axis], axis)
        red[n] = g

    delta, new_m, new_v = {}, {}, {}
    for n, _ in BIG:
        delta[n], new_m[n], new_v[n] = _adamw_nd(w[n], red[n], m[n], v[n])
    local_shapes = {n: w[n].shape for n in small_names}
    packs = [_pack_small(t, small_names) for t in (w, red, m, v)]
    outs = _adamw(*packs)
    ds, ms, vs = (_unpack_small(o, local_shapes, small_names) for o in outs)
    delta.update(ds)
    new_m.update(ms)
    new_v.update(vs)

    return (loss, grad_x[None], *[red[n] for n in WEIGHT_ORDER], *[delta[n] for n in WEIGHT_ORDER],
            *[new_m[n] for n in WEIGHT_ORDER], *[new_v[n] for n in WEIGHT_ORDER])
```

```python
import math

import jax
import jax.numpy as jnp
from jax import lax
from jax.experimental import pallas as pl
from jax.experimental.pallas import tpu as pltpu

F32 = jnp.float32
BF16 = jnp.bfloat16
HIGHEST = lax.Precision.HIGHEST
MESH = pl.DeviceIdType.MESH

D_MODEL = 1024
CHUNK = 64
A_HEADS = 8
A_HEAD = 128
A_QK = A_HEADS * A_HEAD
A_CONV_WIDTH = 3 * A_QK
B_HEADS = 16
B_HEAD = 64
LEFT = 8 * CHUNK
QBLK = 4 * CHUNK
KBLK = LEFT + QBLK
REL_CLIP = 256
REL_PAD = 640
FFN_DIM = 2816
EPS = 1e-6
NEG_INF = -1e30
LANE = 128
SUB = 8
VMEM_LIMIT = 56 * 1024 * 1024

ADAM_LR = 0.001
ADAM_B1 = 0.9
ADAM_B2 = 0.999
ADAM_EPS = 1e-08
ADAM_WD = 0.01
ADAM_STEP = 10


VMEM_FULL = pl.BlockSpec(memory_space=pltpu.VMEM)


def _params(sem=None):
    return pltpu.CompilerParams(dimension_semantics=sem, vmem_limit_bytes=VMEM_LIMIT)


def _tile(n, cap):
    if n <= cap:
        return n
    best = None
    for t in range(LANE, cap + 1, LANE):
        if n % t == 0:
            best = t
    assert best is not None, n
    return best


def _sigmoid(x):
    return 0.5 * jnp.tanh(0.5 * x) + 0.5


def _softplus(x):
    return jnp.maximum(x, 0.0) + jnp.log(1.0 + jnp.exp(-jnp.abs(x)))


def _dot(a, b, dims, prec=None):
    return lax.dot_general(a, b, (dims, ((), ())), preferred_element_type=F32, precision=prec)


NN = ((1,), (0,))
NT = ((1,), (1,))
TN = ((0,), (0,))


def _bdot(a, b, dims):
    return _dot(a.astype(BF16), b.astype(BF16), dims)


def _mm(a, b, mode="nn", out_dtype=F32, res=None, name="mm", norm=None):
    if mode == "nn":
        (m, k), (k2, n) = a.shape, b.shape
    elif mode == "nt":
        (m, k), (n, k2) = a.shape, b.shape
    else:
        (k, m), (k2, n) = a.shape, b.shape
    assert k == k2, (a.shape, b.shape, mode)
    tm, tn, tk = _tile(m, 1408), _tile(n, 1408), _tile(k, 1408)
    if m == 8192:
        tm = 1024
    if k == 8192:
        tk = 1024
    nk = k // tk
    dims = {"nn": NN, "nt": NT, "tn": TN}[mode]
    a_spec = {"nn": pl.BlockSpec((tm, tk), lambda i, j, kk: (i, kk)),
              "nt": pl.BlockSpec((tm, tk), lambda i, j, kk: (i, kk)),
              "tn": pl.BlockSpec((tk, tm), lambda i, j, kk: (kk, i))}[mode]
    b_spec = {"nn": pl.BlockSpec((tk, tn), lambda i, j, kk: (kk, j)),
              "nt": pl.BlockSpec((tn, tk), lambda i, j, kk: (j, kk)),
              "tn": pl.BlockSpec((tk, tn), lambda i, j, kk: (kk, j))}[mode]
    o_spec = pl.BlockSpec((tm, tn), lambda i, j, kk: (i, j))
    return _mm_call(name, a, b, dims, (m // tm, n // tn, nk), a_spec, b_spec, o_spec, (m, n), out_dtype, (tm, tn),
                    res, norm)


ROW_TILE = 1024
N_SHARDS = 4


def _mm_call(name, a, b, dims, grid, a_spec, b_spec, o_spec, out_shape, out_dtype, acc_shape, res=None, norm=None,
             into=None):
    nk = grid[2]
    has_res = res is not None
    has_norm = norm is not None
    has_into = into is not None
    if has_norm:
        assert grid[1] == 1 and len(out_shape) == 2 and out_dtype == F32

    def flat(v):
        return v.reshape(-1, v.shape[-1]) if v.ndim == 3 else v

    def body(a_ref, b_ref, *rest):
        rest = list(rest)
        res_ref = rest.pop(0) if has_res else None
        x_ref, g_ref, dres_ref = (rest.pop(0), rest.pop(0), rest.pop(0)) if has_norm else (None, None, None)
        if has_into:
            rest.pop(0)
        o_ref = rest.pop(0)
        dg_ref = rest.pop(0) if has_norm else None
        acc = rest.pop(0)
        kk = pl.program_id(2)
        first_rows = pl.program_id(0) == 0

        @pl.when(kk == 0)
        def _():
            acc[...] = jnp.zeros_like(acc)

        if has_norm:
            @pl.when(first_rows & (kk == 0))
            def _():
                dg_ref[...] = jnp.zeros_like(dg_ref)

        acc[...] += _bdot(flat(a_ref[...]), flat(b_ref[...]), dims)

        @pl.when(kk == nk - 1)
        def _():
            r = acc[...]
            if has_res:
                r = r + res_ref[...]
            if not has_norm:
                o_ref[...] = r.reshape(o_ref.shape).astype(out_dtype)
                return
            xv = x_ref[...]
            rs = lax.rsqrt(jnp.mean(xv * xv, axis=-1, keepdims=True) + EPS)
            t = r * g_ref[...]
            c = jnp.mean(t * xv, axis=-1, keepdims=True)
            o_ref[...] = dres_ref[...] + rs * t - xv * (rs * rs * rs) * c
            dg_ref[...] += jnp.sum(r * xv * rs, axis=0, keepdims=True)

    args = [a, b] + ([res] if has_res else [])
    in_specs = [a_spec, b_spec] + ([o_spec] if has_res else [])
    out_specs, out_shapes = o_spec, jax.ShapeDtypeStruct(out_shape, out_dtype)
    if has_norm:
        vec = pl.BlockSpec((1, out_shape[1]), lambda i, j, kk: (0, 0))
        args += list(norm)
        in_specs += [o_spec, vec, o_spec]
        out_specs = [o_spec, vec]
        out_shapes = [out_shapes, jax.ShapeDtypeStruct((1, out_shape[1]), F32)]
    aliases = {}
    if has_into:
        assert not has_norm and into.dtype == out_dtype
        aliases = {len(args): 0}
        args.append(into)
        in_specs.append(pl.BlockSpec(memory_space=pl.ANY))
        out_shapes = jax.ShapeDtypeStruct(into.shape, out_dtype)
    return pl.pallas_call(
        body, name=name, grid=grid, in_specs=in_specs, out_specs=out_specs, out_shape=out_shapes,
        scratch_shapes=[pltpu.VMEM(acc_shape, F32)], input_output_aliases=aliases,
        compiler_params=_params(("arbitrary" if has_norm else "parallel", "parallel", "arbitrary")),
    )(*args)


def _shards_per_block(rows):
    return N_SHARDS if N_SHARDS * rows <= 1408 else 2


def _mm_rowsh(a, buf, rows, blk0, mode, name, res=None, out_dtype=F32, norm=None):
    s = a.shape[0]
    cols = buf.shape[2]
    g = _shards_per_block(rows)
    tm = _tile(s, ROW_TILE)
    b_blk = (g, rows, cols)
    if mode == "nn":
        return _mm_call(name, a, buf, NN, (s // tm, 1, N_SHARDS // g),
                        pl.BlockSpec((tm, g * rows), lambda i, j, kk: (i, kk)),
                        pl.BlockSpec(b_blk, lambda i, j, kk: (kk, blk0, 0)),
                        pl.BlockSpec((tm, cols), lambda i, j, kk: (i, 0)),
                        (s, cols), out_dtype, (tm, cols), res)
    return _mm_call(name, a, buf, NT, (s // tm, N_SHARDS // g, 1),
                    pl.BlockSpec((tm, cols), lambda i, j, kk: (i, 0)),
                    pl.BlockSpec(b_blk, lambda i, j, kk: (j, blk0, 0)),
                    pl.BlockSpec((tm, g * rows), lambda i, j, kk: (i, j)),
                    (s, N_SHARDS * rows), out_dtype, (tm, g * rows), res, norm)


def _mm_rowsh_dw(act, dy, rows, name, into, blk0):
    s = act.shape[0]
    cols = dy.shape[1]
    g = _shards_per_block(rows)
    ts = _tile(s, ROW_TILE)
    return _mm_call(name, act, dy, TN, (1, N_SHARDS // g, s // ts),
                    pl.BlockSpec((ts, g * rows), lambda i, j, kk: (kk, j)),
                    pl.BlockSpec((ts, cols), lambda i, j, kk: (kk, 0)),
                    pl.BlockSpec((g, rows, cols), lambda i, j, kk: (j, blk0, 0)),
                    into.shape, F32, (g * rows, cols), into=into)


def _mm_colsh(a, buf, krows, blk0, mode, name, flat=False, res=None, out_dtype=F32, norm=None):
    cols = buf.shape[2]
    b_nn = pl.BlockSpec((None, krows, cols), lambda i, j, kk: (j, blk0, 0))
    b_nt = pl.BlockSpec((None, krows, cols), lambda i, j, kk: (kk, blk0, 0))
    if mode == "nn":
        s = a.shape[0]
        tm = _tile(s, ROW_TILE)
        o_spec = (pl.BlockSpec((tm, cols), lambda i, j, kk: (i, j)) if flat
                  else pl.BlockSpec((None, tm, cols), lambda i, j, kk: (j, i, 0)))
        return _mm_call(name, a, buf, NN, (s // tm, N_SHARDS, 1),
                        pl.BlockSpec((tm, krows), lambda i, j, kk: (i, 0)), b_nn, o_spec,
                        (s, N_SHARDS * cols) if flat else (N_SHARDS, s, cols), out_dtype, (tm, cols), res)
    s = a.shape[0] if flat else a.shape[1]
    tm = _tile(s, ROW_TILE)
    a_spec = (pl.BlockSpec((tm, cols), lambda i, j, kk: (i, kk)) if flat
              else pl.BlockSpec((None, tm, cols), lambda i, j, kk: (kk, i, 0)))
    return _mm_call(name, a, buf, NT, (s // tm, 1, N_SHARDS), a_spec, b_nt,
                    pl.BlockSpec((tm, krows), lambda i, j, kk: (i, 0)),
                    (s, krows), out_dtype, (tm, krows), res, norm)


def _mm_colsh_dw(x, dy, name, flat=False, into=None, blk0=0):
    s, k = x.shape
    cols = dy.shape[1] // N_SHARDS if flat else dy.shape[2]
    ts = _tile(s, ROW_TILE)
    b_spec = (pl.BlockSpec((ts, cols), lambda i, j, kk: (kk, j)) if flat
              else pl.BlockSpec((None, ts, cols), lambda i, j, kk: (j, kk, 0)))
    return _mm_call(name, x, dy, TN, (1, N_SHARDS, s // ts),
                    pl.BlockSpec((ts, k), lambda i, j, kk: (kk, 0)), b_spec,
                    pl.BlockSpec((None, k, cols), lambda i, j, kk: (j, blk0, 0)),
                    (N_SHARDS, k, cols) if into is None else into.shape, F32, (k, cols), into=into)


def _rmsnorm_fwd(x, g):
    s, d = x.shape
    tr = _tile(s, 1024)

    def body(x_ref, g_ref, o_ref):
        xv = x_ref[...]
        r = lax.rsqrt(jnp.mean(xv * xv, axis=-1, keepdims=True) + EPS)
        o_ref[...] = (xv * r * g_ref[...]).astype(BF16)

    return pl.pallas_call(
        body, name="rmsnorm_fwd", grid=(s // tr,),
        in_specs=[pl.BlockSpec((tr, d), lambda i: (i, 0)), pl.BlockSpec((1, d), lambda i: (0, 0))],
        out_specs=pl.BlockSpec((tr, d), lambda i: (i, 0)),
        out_shape=jax.ShapeDtypeStruct((s, d), BF16),
        compiler_params=_params(("parallel",)),
    )(x, g)


def _final_loss(h, g, tgt):
    s, d = h.shape
    tr = _tile(s, 1024)

    def body(x_ref, g_ref, t_ref, loss_ref, dx_ref, dg_ref):
        @pl.when(pl.program_id(0) == 0)
        def _():
            dg_ref[...] = jnp.zeros_like(dg_ref)
            loss_ref[...] = jnp.zeros_like(loss_ref)

        xv = x_ref[...]
        r = lax.rsqrt(jnp.mean(xv * xv, axis=-1, keepdims=True) + EPS)
        xh = xv * r
        err = xh * g_ref[...] - t_ref[...]
        per_row = jnp.mean(err * err, axis=-1, keepdims=True)
        loss_ref[...] += 0.5 * jnp.sum(per_row, axis=0, keepdims=True)
        dy = err * (1.0 / d)
        t = dy * g_ref[...]
        c = jnp.mean(t * xv, axis=-1, keepdims=True)
        dx_ref[...] = r * t - xv * (r * r * r) * c
        dg_ref[...] += jnp.sum(dy * xh, axis=0, keepdims=True)

    row = pl.BlockSpec((tr, d), lambda i: (i, 0))
    vec = pl.BlockSpec((1, d), lambda i: (0, 0))
    return pl.pallas_call(
        body, name="final_loss", grid=(s // tr,),
        in_specs=[row, vec, row],
        out_specs=[pl.BlockSpec((1, LANE), lambda i: (0, 0)), row, vec],
        out_shape=[jax.ShapeDtypeStruct((1, LANE), F32), jax.ShapeDtypeStruct((s, d), F32),
                   jax.ShapeDtypeStruct((1, d), F32)],
        compiler_params=_params(("arbitrary",)),
    )(h, g, tgt)


CONV_ROWS = 512
CONV_COLS = 1408
CONV_CHUNK = 64


def _per_lane_tile(tile_body):
    def body(*refs):
        for lt in range(refs[0].shape[-1] // LANE):
            cols = slice(lt * LANE, (lt + 1) * LANE)
            tile_body(*[r.at[(slice(None),) * (len(r.shape) - 1) + (cols,)] for r in refs])
    return body


def _lagged(window, lag):
    return (pltpu.roll(window, lag, 0) if lag else window)[SUB:]


def _led(window, lead):
    n = window.shape[0] - SUB
    return (pltpu.roll(window, window.shape[0] - lead, 0) if lead else window)[:n]


def _fold_rows(v):
    return jnp.sum(v.reshape(v.shape[0] // SUB, SUB, v.shape[1]), axis=0)


def _taps(shifted, w):
    acc = None
    for k, xs in enumerate(shifted):
        term = xs * w[k:k + 1, :]
        acc = term if acc is None else acc + term
    return acc


def _conv_tiles(s, c):
    return _tile(s, CONV_ROWS), _tile(c, CONV_COLS)


def _conv_silu_fwd(pre, w):
    s, c = pre.shape
    width = w.shape[0]
    tr, tc = _conv_tiles(s, c)

    def body(x_ref, w_ref, o_ref, tail):
        @pl.when(pl.program_id(1) == 0)
        def _():
            tail[...] = jnp.zeros_like(tail)

        wv = w_ref[...]

        def do(c0, window):
            y = _taps([_lagged(window, width - 1 - k) for k in range(width)], wv)
            o_ref[pl.ds(c0, CONV_CHUNK), :] = y * _sigmoid(y)

        def chunk(ci, carry):
            c0 = pl.multiple_of(ci * CONV_CHUNK, CONV_CHUNK)
            do(c0, x_ref[pl.ds(pl.multiple_of(c0 - SUB, SUB), CONV_CHUNK + SUB), :])
            return carry

        do(0, jnp.concatenate([tail[...], x_ref[:CONV_CHUNK, :]], axis=0))
        lax.fori_loop(1, tr // CONV_CHUNK, chunk, 0)
        tail[...] = x_ref[tr - SUB:, :]

    blk = pl.BlockSpec((tr, tc), lambda j, i: (i, j))
    return pl.pallas_call(
        _per_lane_tile(body), name="conv_silu_fwd", grid=(c // tc, s // tr),
        in_specs=[blk, pl.BlockSpec((width, tc), lambda j, i: (0, j))], out_specs=blk,
        out_shape=jax.ShapeDtypeStruct((s, c), F32),
        scratch_shapes=[pltpu.VMEM((SUB, tc), F32)],
        compiler_params=_params(("parallel", "arbitrary")),
    )(pre, w)


def _prev_rows_index(i_blk, tr):
    return jnp.maximum(i_blk * (tr // SUB) - 1, 0)


def _conv_silu_bwd(pre, w, dact):
    s, c = pre.shape
    width = w.shape[0]
    tr, tc = _conv_tiles(s, c)
    nr = s // tr

    nchunks = tr // CONV_CHUNK

    def body(x_ref, p_ref, w_ref, d_ref, dx_ref, dw_ref, head):
        @pl.when(pl.program_id(1) == 0)
        def _():
            head[...] = jnp.zeros_like(head)
            dw_ref[...] = jnp.zeros_like(dw_ref)

        wv = w_ref[...]

        def do(c0, window, later, dws):
            xs = [_lagged(window, width - 1 - k) for k in range(width)]
            y = _taps(xs, wv)
            sg = _sigmoid(y)
            dy = d_ref[pl.ds(c0, CONV_CHUNK), :] * sg * (1.0 + y * (1.0 - sg))
            dws = tuple(dw + _fold_rows(dy * x) for dw, x in zip(dws, xs))
            ahead = jnp.concatenate([dy, later], axis=0)
            dx_ref[pl.ds(c0, CONV_CHUNK), :] = _taps([_led(ahead, width - 1 - k) for k in range(width)],
                                                     wv).astype(BF16)
            return dy[:SUB], dws

        def chunk(it, carry):
            c0 = pl.multiple_of((nchunks - 1 - it) * CONV_CHUNK, CONV_CHUNK)
            return do(c0, x_ref[pl.ds(pl.multiple_of(c0 - SUB, SUB), CONV_CHUNK + SUB), :], *carry)

        zero = jnp.zeros((SUB, LANE), F32)
        carry = lax.fori_loop(0, nchunks - 1, chunk, (head[...], (zero,) * width))
        before = jnp.where(pl.program_id(1) == nr - 1, 0.0, p_ref[...])
        later, dws = do(0, jnp.concatenate([before, x_ref[:CONV_CHUNK, :]], axis=0), *carry)
        head[...] = later
        for k in range(width):
            dw_ref[k:k + 1, :] += jnp.sum(dws[k], axis=0, keepdims=True)

    blk = pl.BlockSpec((tr, tc), lambda j, i: (nr - 1 - i, j))
    prev = pl.BlockSpec((SUB, tc), lambda j, i: (_prev_rows_index(nr - 1 - i, tr), j))
    wblk = pl.BlockSpec((width, tc), lambda j, i: (0, j))
    return pl.pallas_call(
        _per_lane_tile(body), name="conv_silu_bwd", grid=(c // tc, nr),
        in_specs=[blk, prev, wblk, blk], out_specs=[blk, wblk],
        out_shape=[jax.ShapeDtypeStruct((s, c), BF16), jax.ShapeDtypeStruct((width, c), F32)],
        scratch_shapes=[pltpu.VMEM((SUB, tc), F32)],
        compiler_params=_params(("parallel", "arbitrary")),
    )(pre, pre, w, dact)


def _ffn_act_fwd(pre, w, b):
    _, halves, s, c = pre.shape
    width = w.shape[2]
    tr, tc = _conv_tiles(s, c)
    ncb = c // tc

    def body(x_ref, w_ref, b_ref, o_ref, tail):
        @pl.when(pl.program_id(2) == 0)
        def _():
            tail[...] = jnp.zeros_like(tail)

        wg, wv, bg, bv = w_ref[0], w_ref[1], b_ref[0], b_ref[1]

        def do(c0, win_g, win_v):
            yg = _taps([_lagged(win_g, width - 1 - k) for k in range(width)], wg) + bg
            yv = _taps([_lagged(win_v, width - 1 - k) for k in range(width)], wv) + bv
            o_ref[pl.ds(c0, CONV_CHUNK), :] = (yg * _sigmoid(yg) * yv).astype(BF16)

        def chunk(ci, carry):
            c0 = pl.multiple_of(ci * CONV_CHUNK, CONV_CHUNK)
            rows = pl.ds(pl.multiple_of(c0 - SUB, SUB), CONV_CHUNK + SUB)
            do(c0, x_ref[0, rows, :], x_ref[1, rows, :])
            return carry

        do(0, jnp.concatenate([tail[0], x_ref[0, :CONV_CHUNK, :]], axis=0),
           jnp.concatenate([tail[1], x_ref[1, :CONV_CHUNK, :]], axis=0))
        lax.fori_loop(1, tr // CONV_CHUNK, chunk, 0)
        tail[...] = x_ref[:, tr - SUB:, :]

    return pl.pallas_call(
        _per_lane_tile(body), name="ffn_act_fwd", grid=(halves, ncb, s // tr),
        in_specs=[pl.BlockSpec((2, None, tr, tc), lambda h, j, i: (0, h, i, j)),
                  pl.BlockSpec((2, None, width, tc), lambda h, j, i: (0, h, 0, j)),
                  pl.BlockSpec((2, None, 1, tc), lambda h, j, i: (0, h, 0, j))],
        out_specs=pl.BlockSpec((tr, tc), lambda h, j, i: (i, h * ncb + j)),
        out_shape=jax.ShapeDtypeStruct((s, halves * c), BF16),
        scratch_shapes=[pltpu.VMEM((2, SUB, tc), F32)],
        compiler_params=_params(("parallel", "parallel", "arbitrary")),
    )(pre, w, b)


def _ffn_act_bwd(pre, w, b, dact):
    _, halves, s, c = pre.shape
    width = w.shape[2]
    tr, tc = _conv_tiles(s, c)
    ncb = c // tc
    nr = s // tr
    nchunks = tr // CONV_CHUNK

    def body(x_ref, p_ref, w_ref, b_ref, d_ref, dx_ref, dw_ref, db_ref, head):
        @pl.when(pl.program_id(2) == 0)
        def _():
            for r in (head, dw_ref, db_ref):
                r[...] = jnp.zeros_like(r)

        wg, wv, bg, bv = w_ref[0], w_ref[1], b_ref[0], b_ref[1]

        def do(c0, win_g, win_v, later_g, later_v, dwg, dwv, dbg, dbv):
            xg = [_lagged(win_g, width - 1 - k) for k in range(width)]
            xv = [_lagged(win_v, width - 1 - k) for k in range(width)]
            yg = _taps(xg, wg) + bg
            yv = _taps(xv, wv) + bv
            sg = _sigmoid(yg)
            da = d_ref[pl.ds(c0, CONV_CHUNK), :]
            dyv = da * yg * sg
            dyg = da * yv * sg * (1.0 + yg * (1.0 - sg))
            dwg = tuple(dw + _fold_rows(dyg * x) for dw, x in zip(dwg, xg))
            dwv = tuple(dw + _fold_rows(dyv * x) for dw, x in zip(dwv, xv))
            dbg = dbg + _fold_rows(dyg)
            dbv = dbv + _fold_rows(dyv)
            ahead_g = jnp.concatenate([dyg, later_g], axis=0)
            ahead_v = jnp.concatenate([dyv, later_v], axis=0)
            dxg = _taps([_led(ahead_g, width - 1 - k) for k in range(width)], wg)
            dxv = _taps([_led(ahead_v, width - 1 - k) for k in range(width)], wv)
            dx_ref[0, pl.ds(c0, CONV_CHUNK), :] = dxg.astype(BF16)
            dx_ref[1, pl.ds(c0, CONV_CHUNK), :] = dxv.astype(BF16)
            return dyg[:SUB], dyv[:SUB], dwg, dwv, dbg, dbv

        def chunk(it, carry):
            c0 = pl.multiple_of((nchunks - 1 - it) * CONV_CHUNK, CONV_CHUNK)
            rows = pl.ds(pl.multiple_of(c0 - SUB, SUB), CONV_CHUNK + SUB)
            return do(c0, x_ref[0, rows, :], x_ref[1, rows, :], *carry)

        zero = jnp.zeros((SUB, LANE), F32)
        carry = lax.fori_loop(0, nchunks - 1, chunk,
                              (head[0], head[1], (zero,) * width, (zero,) * width, zero, zero))
        before = jnp.where(pl.program_id(2) == nr - 1, 0.0, p_ref[...])
        later_g, later_v, dwg, dwv, dbg, dbv = do(
            0, jnp.concatenate([before[0], x_ref[0, :CONV_CHUNK, :]], axis=0),
            jnp.concatenate([before[1], x_ref[1, :CONV_CHUNK, :]], axis=0), *carry)
        head[0] = later_g
        head[1] = later_v
        db_ref[0] += jnp.sum(dbg, axis=0, keepdims=True)
        db_ref[1] += jnp.sum(dbv, axis=0, keepdims=True)
        for k in range(width):
            dw_ref[0, k:k + 1, :] += jnp.sum(dwg[k], axis=0, keepdims=True)
            dw_ref[1, k:k + 1, :] += jnp.sum(dwv[k], axis=0, keepdims=True)

    blk = pl.BlockSpec((2, None, tr, tc), lambda h, j, i: (0, h, nr - 1 - i, j))
    prev = pl.BlockSpec((2, None, SUB, tc), lambda h, j, i: (0, h, _prev_rows_index(nr - 1 - i, tr), j))
    wblk = pl.BlockSpec((2, None, width, tc), lambda h, j, i: (0, h, 0, j))
    bblk = pl.BlockSpec((2, None, 1, tc), lambda h, j, i: (0, h, 0, j))
    return pl.pallas_call(
        _per_lane_tile(body), name="ffn_act_bwd", grid=(halves, ncb, nr),
        in_specs=[blk, prev, wblk, bblk, pl.BlockSpec((tr, tc), lambda h, j, i: (nr - 1 - i, h * ncb + j))],
        out_specs=[blk, wblk, bblk],
        out_shape=[jax.ShapeDtypeStruct(pre.shape, BF16), jax.ShapeDtypeStruct(w.shape, F32),
                   jax.ShapeDtypeStruct(b.shape, F32)],
        scratch_shapes=[pltpu.VMEM((2, SUB, tc), F32)],
        compiler_params=_params(("parallel", "parallel", "arbitrary")),
    )(pre, pre, w, b, dact)


def _tri_masks():
    row = lax.broadcasted_iota(jnp.int32, (CHUNK, CHUNK), 0)
    col = lax.broadcasted_iota(jnp.int32, (CHUNK, CHUNK), 1)
    return row, col


def _tri_inv(ms, row, col):
    eye = (row == col).astype(F32)
    same_blk = (row >> 4) == (col >> 4)
    mds = [jnp.where(same_blk, m, 0.0) for m in ms]
    offs = [m - md for m, md in zip(ms, mds)]
    xs = [eye - md for md in mds]
    ps = [_bdot(md, md, NN) for md in mds]
    for _ in range(2):
        rs = [_bdot(jnp.concatenate([x, p], axis=0), p, NN) for x, p in zip(xs, ps)]
        xs = [x + r[:CHUNK] for x, r in zip(xs, rs)]
        ps = [r[CHUNK:] for r in rs]
    xs = [x + _bdot(x, p, NN) for x, p in zip(xs, ps)]
    ps = [_bdot(x, off, NN) for x, off in zip(xs, offs)]
    pps = [_bdot(p, p, NN) for p in ps]
    ys = [eye - p for p in ps]
    ys = [y + _bdot(y, pp, NN) for y, pp in zip(ys, pps)]
    return [_bdot(y, x, NN) for y, x in zip(ys, xs)]


def _gdn_gates(ba, alog, dtb, row, col):
    sig = _sigmoid(ba)
    neg_a = -jnp.exp(alog)
    g = neg_a * _softplus(ba + dtb)
    lower = (row >= col).astype(F32)
    gcum = _dot(lower, g, NN, HIGHEST)
    return sig, neg_a, g, gcum


def _gdn_head_common(q_raw, k_raw, v, beta, gc, gr, row, col):
    causal = row >= col
    strict = row > col
    rq = lax.rsqrt(jnp.sum(q_raw * q_raw, axis=-1, keepdims=True) + EPS)
    rk = lax.rsqrt(jnp.sum(k_raw * k_raw, axis=-1, keepdims=True) + EPS)
    q = q_raw * (rq * (A_HEAD ** -0.5))
    k = k_raw * rk
    decay = jnp.where(causal, jnp.exp(jnp.where(causal, gc - gr, 0.0)), 0.0)
    eg = jnp.exp(gc)
    gl = gc[CHUNK - 1:CHUNK, :]
    ekl = jnp.exp(gl - gc)
    dec = jnp.exp(gl)
    kb = k * beta
    kbq = jnp.concatenate([kb, q], axis=0)
    both = _bdot(kbq, k, NT)
    kk, qk = both[:CHUNK], both[CHUNK:]
    a = jnp.where(causal, qk * decay, 0.0)
    return dict(rq=rq, rk=rk, q=q, k=k, decay=decay, eg=eg, ekl=ekl, dec=dec, kb=kb, kbq=kbq, kk=kk, qk=qk, a=a,
                vb=v * beta, kbg=kb * eg, qd=q * eg, ke=k * ekl, causal=causal, strict=strict)


def _gdn_fwd(qkv, ba, z, alog, dtb, wn, gather=()):
    s = qkv.shape[0]
    nc = s // CHUNK
    ng = len(gather)

    def body(qkv_ref, ba_ref, z_ref, alog_ref, dtb_ref, wn_ref, *rest):
        x_refs, rest = rest[:ng], rest[ng:]
        y_ref, o_ref, st_ref, t_ref, w_ref, vn_ref = rest[:6]
        out_refs, rest = rest[6:6 + ng], rest[6 + ng:]
        state = rest[0]
        exchange = _ShardGather(x_refs, out_refs, *rest[1:]) if ng else None

        @pl.when(pl.program_id(0) == 0)
        def _():
            state[...] = jnp.zeros_like(state)
            if ng:
                exchange.start()

        row, col = _tri_masks()
        sig, _, _, gcum = _gdn_gates(ba_ref[...], alog_ref[...], dtb_ref[...], row, col)
        gt = gcum.T
        heads = range(A_HEADS)
        lanes = [slice(h * A_HEAD, (h + 1) * A_HEAD) for h in heads]
        fs = [_gdn_head_common(qkv_ref[:, lanes[h]], qkv_ref[:, A_QK + h * A_HEAD:A_QK + (h + 1) * A_HEAD],
                               qkv_ref[:, 2 * A_QK + h * A_HEAD:2 * A_QK + (h + 1) * A_HEAD],
                               sig[:, h:h + 1], gcum[:, 8 + h:9 + h], gt[8 + h:9 + h, :], row, col) for h in heads]
        ts = [t.astype(BF16) for t in
              _tri_inv([jnp.where(f["strict"], f["kk"] * f["decay"], 0.0) for f in fs], row, col)]
        uws = [_bdot(t, jnp.concatenate([f["vb"], f["kbg"]], axis=1), NN) for t, f in zip(ts, fs)]
        s0s = [state[h] for h in heads]
        ws_ = [uw[:, A_HEAD:].astype(BF16) for uw in uws]
        wss = [_bdot(jnp.concatenate([w, f["qd"].astype(BF16)], axis=0), s0, NN) for w, f, s0 in zip(ws_, fs, s0s)]
        vnews = [(uw[:, :A_HEAD] - wsq[:CHUNK]).astype(BF16) for uw, wsq in zip(uws, wss)]
        os_ = [wsq[CHUNK:] + _bdot(f["a"], vn, NN) for wsq, f, vn in zip(wss, fs, vnews)]
        s1s = [s0 * f["dec"] + _bdot(f["ke"], vn, TN) for s0, f, vn in zip(s0s, fs, vnews)]
        for h in heads:
            ln = lanes[h]
            st_ref[0, h] = s0s[h]
            t_ref[0, h] = ts[h]
            w_ref[:, ln] = ws_[h]
            vn_ref[:, ln] = vnews[h]
            state[h] = s1s[h]
            o = os_[h]
            o_ref[:, ln] = o
            r = lax.rsqrt(jnp.mean(o * o, axis=-1, keepdims=True) + EPS)
            zz = z_ref[:, ln]
            y_ref[:, ln] = (o * r * wn_ref[...] * zz * _sigmoid(zz)).astype(BF16)

        if ng:
            @pl.when(pl.program_id(0) == nc - 1)
            def _():
                exchange.finish()

    vec = pl.BlockSpec((1, LANE), lambda n: (0, 0))
    wide = pl.BlockSpec((CHUNK, A_QK), lambda n: (n, 0))
    return pl.pallas_call(
        body, name="gdn_fwd_gather" if ng else "gdn_fwd", grid=(nc,),
        in_specs=[pl.BlockSpec((CHUNK, A_CONV_WIDTH), lambda n: (n, 0)),
                  pl.BlockSpec((CHUNK, LANE), lambda n: (n, 0)), wide, vec, vec, vec] + [HBM_SPEC] * ng,
        out_specs=[wide, wide, pl.BlockSpec((1, A_HEADS, A_HEAD, A_HEAD), lambda n: (n, 0, 0, 0)),
                   pl.BlockSpec((1, A_HEADS, CHUNK, CHUNK), lambda n: (n, 0, 0, 0)), wide, wide] + [HBM_SPEC] * ng,
        out_shape=[jax.ShapeDtypeStruct((s, A_QK), BF16), jax.ShapeDtypeStruct((s, A_QK), F32),
                   jax.ShapeDtypeStruct((nc, A_HEADS, A_HEAD, A_HEAD), F32),
                   jax.ShapeDtypeStruct((nc, A_HEADS, CHUNK, CHUNK), BF16),
                   jax.ShapeDtypeStruct((s, A_QK), BF16), jax.ShapeDtypeStruct((s, A_QK), BF16)]
        + [jax.ShapeDtypeStruct((N_SHARDS,) + g.shape, g.dtype) for g in gather],
        scratch_shapes=[pltpu.VMEM((A_HEADS, A_HEAD, A_HEAD), F32)] + (_gather_scratch(ng) if ng else []),
        compiler_params=_params(("arbitrary",)),
    )(qkv, ba, z, alog, dtb, wn, *gather)


def _gdn_bwd(qkv, ba, z, o_raw, dy, states, t_all, w_all, vn_all, alog, dtb, wn, exchange=()):
    s = qkv.shape[0]
    nc = s // CHUNK
    nx = len(exchange)

    def body(qkv_ref, ba_ref, z_ref, o_ref, dy_ref, st_ref, t_ref, w_ref, vn_ref, alog_ref, dtb_ref, wn_ref, *rest):
        p_refs, rest = rest[:nx], rest[nx:]
        dqkv_ref, dba_ref, dz_ref, dalog_ref, ddtb_ref, dwn_ref = rest[:6]
        arrive_refs, rest = rest[6:6 + nx], rest[6 + nx:]
        dstate = rest[0]
        swap = _ChipExchange(p_refs, arrive_refs, *rest[1:]) if nx else None

        @pl.when(pl.program_id(0) == 0)
        def _():
            for r in (dstate, dalog_ref, ddtb_ref, dwn_ref):
                r[...] = jnp.zeros_like(r)
            if nx:
                swap.start()

        row, col = _tri_masks()
        bat = ba_ref[...]
        sig, neg_a, g, gcum = _gdn_gates(bat, alog_ref[...], dtb_ref[...], row, col)
        gt = gcum.T
        lane = lax.broadcasted_iota(jnp.int32, (CHUNK, LANE), 1)
        ones = jnp.ones((CHUNK, LANE), F32)
        last_row = lax.broadcasted_iota(jnp.int32, (CHUNK, 1), 0) == CHUNK - 1
        wnv = wn_ref[...]
        dgc_tile = jnp.zeros((CHUNK, LANE), F32)
        dbeta_tile = jnp.zeros((CHUNK, LANE), F32)
        dwn_acc = jnp.zeros((1, LANE), F32)
        hs = []
        for h in range(A_HEADS):
            ln = slice(h * A_HEAD, (h + 1) * A_HEAD)
            lk = slice(A_QK + h * A_HEAD, A_QK + (h + 1) * A_HEAD)
            lv = slice(2 * A_QK + h * A_HEAD, 2 * A_QK + (h + 1) * A_HEAD)
            q_raw, k_raw, v = qkv_ref[:, ln], qkv_ref[:, lk], qkv_ref[:, lv]
            f = _gdn_head_common(q_raw, k_raw, v, sig[:, h:h + 1], gcum[:, 8 + h:9 + h], gt[8 + h:9 + h, :], row, col)
            f.update(h=h, ln=ln, lk=lk, lv=lv, q_raw=q_raw, k_raw=k_raw, v=v, beta=sig[:, h:h + 1],
                     s0=st_ref[0, h], ds1=dstate[h], t=t_ref[0, h], w=w_ref[:, ln], vnew=vn_ref[:, ln])
            o = o_ref[:, ln]
            zz = z_ref[:, ln]
            dyv = dy_ref[:, ln]
            r = lax.rsqrt(jnp.mean(o * o, axis=-1, keepdims=True) + EPS)
            sz = _sigmoid(zz)
            silu = zz * sz
            dz_ref[:, ln] = (dyv * o * r * wnv * sz * (1.0 + zz * (1.0 - sz))).astype(BF16)
            dwn_acc = dwn_acc + jnp.sum(dyv * silu * o * r, axis=0, keepdims=True)
            tt = dyv * silu * wnv
            do = r * tt - o * (r * r * r) * jnp.mean(tt * o, axis=-1, keepdims=True)
            f["do_b"] = do.astype(BF16)
            hs.append(f)
        for f in hs:
            f["dvnew"] = _bdot(f["a"], f["do_b"], TN) + _bdot(f["ke"], f["ds1"], NN)
            f["da"] = jnp.where(f["causal"], _bdot(f["do_b"], f["vnew"], NT), 0.0)
            f["dke"] = _bdot(f["vnew"], f["ds1"], NT)
            f["ddec"] = jnp.sum(jnp.sum(f["s0"] * f["ds1"], axis=1, keepdims=True), axis=0, keepdims=True)
        for f in hs:
            do_dv = jnp.concatenate([f["do_b"], f["dvnew"].astype(BF16)], axis=0)
            both = _bdot(do_dv, f["s0"], NT)
            f["dqd"], f["dw"] = both[:CHUNK], -both[CHUNK:]
            qd_w = jnp.concatenate([f["qd"].astype(BF16), -f["w"]], axis=0)
            dstate[f["h"]] = _bdot(qd_w, do_dv, TN) + f["dec"] * f["ds1"]
        for f in hs:
            dd = jnp.concatenate([f["dvnew"], f["dw"]], axis=1).astype(BF16)
            tdd = _bdot(f["t"], dd, TN)
            f["dvb"], f["dkbg"] = tdd[:, :A_HEAD], tdd[:, A_HEAD:]
            f["dt"] = _bdot(dd, jnp.concatenate([f["vb"], f["kbg"]], axis=1), NT)
        for f in hs:
            f["tdt"] = _bdot(f["t"], f["dt"], TN)
        for f in hs:
            dm = jnp.where(f["strict"], -_bdot(f["tdt"], f["t"], NT), 0.0)
            f["ddecay"] = (dm * f["kk"] + f["da"] * f["qk"]) * f["decay"]
            f["dboth"] = jnp.concatenate([dm * f["decay"], f["da"] * f["decay"]], axis=0).astype(BF16)
        for f in hs:
            f["r2"] = _bdot(f["dboth"], f["k"], NN)
            f["dk0"] = _bdot(f["dboth"], f["kbq"], TN)
        for f in hs:
            h, k, beta = f["h"], f["k"], f["beta"]
            dkb = f["r2"][:CHUNK] + f["dkbg"] * f["eg"]
            dq = f["r2"][CHUNK:] + f["dqd"] * f["eg"]
            dk = f["dk0"] + f["dke"] * f["ekl"] + dkb * beta
            dke_ke = jnp.sum(f["dke"] * f["ke"], axis=-1, keepdims=True)
            dgc = (jnp.sum(f["ddecay"], axis=-1, keepdims=True)
                   + jnp.sum(f["dqd"] * f["qd"], axis=-1, keepdims=True) - dke_ke
                   + jnp.sum(f["dkbg"] * f["kbg"], axis=-1, keepdims=True))
            dgl = jnp.sum(dke_ke, axis=0, keepdims=True) + f["ddec"] * f["dec"]
            dgc = dgc + jnp.where(last_row, dgl, 0.0)
            dbeta = jnp.sum(dkb * k, axis=-1, keepdims=True) + jnp.sum(f["dvb"] * f["v"], axis=-1, keepdims=True)
            dgc_tile = dgc_tile + jnp.where(lane == 8 + h, dgc, 0.0)
            dbeta_tile = dbeta_tile + jnp.where(lane == h, dbeta, 0.0)
            dqn = dq * (A_HEAD ** -0.5)
            rq, rk, q_raw, k_raw = f["rq"], f["rk"], f["q_raw"], f["k_raw"]
            dqkv_ref[:, f["ln"]] = rq * dqn - q_raw * (rq * rq * rq) * jnp.sum(dqn * q_raw, axis=-1, keepdims=True)
            dqkv_ref[:, f["lk"]] = rk * dk - k_raw * (rk * rk * rk) * jnp.sum(dk * k_raw, axis=-1, keepdims=True)
            dqkv_ref[:, f["lv"]] = f["dvb"] * beta
        ddecays = [f["ddecay"] for f in hs]
        col_sums = _dot(jnp.concatenate(ddecays, axis=1), ones, TN, HIGHEST)
        for h in range(A_HEADS):
            dgc_tile = dgc_tile - jnp.where(lane == 8 + h, col_sums[h * CHUNK:(h + 1) * CHUNK, :1], 0.0)
        upper = (row <= col).astype(F32)
        dg = _dot(upper, dgc_tile, NN, HIGHEST)
        da_raw = dg * neg_a * _sigmoid(bat + dtb_ref[...])
        dba_ref[...] = jnp.where(lane < 8, dbeta_tile * sig * (1.0 - sig),
                                 jnp.where(lane < 16, da_raw, 0.0)).astype(BF16)
        dwn_ref[...] += dwn_acc
        ddtb_ref[...] += jnp.sum(da_raw, axis=0, keepdims=True)
        dalog_ref[...] += jnp.sum(dg * g, axis=0, keepdims=True)

        if nx:
            @pl.when(pl.program_id(0) == nc - 1)
            def _():
                swap.finish()

    rev = lambda n: (nc - 1 - n, 0)
    vec = pl.BlockSpec((1, LANE), lambda n: (0, 0))
    wide = pl.BlockSpec((CHUNK, A_QK), rev)
    qkv_blk = pl.BlockSpec((CHUNK, A_CONV_WIDTH), rev)
    ba_blk = pl.BlockSpec((CHUNK, LANE), rev)
    vsh = jax.ShapeDtypeStruct((1, LANE), F32)
    return pl.pallas_call(
        body, name="gdn_bwd_exchange" if nx else "gdn_bwd", grid=(nc,),
        in_specs=[qkv_blk, ba_blk, wide, wide, wide,
                  pl.BlockSpec((1, A_HEADS, A_HEAD, A_HEAD), lambda n: (nc - 1 - n, 0, 0, 0)),
                  pl.BlockSpec((1, A_HEADS, CHUNK, CHUNK), lambda n: (nc - 1 - n, 0, 0, 0)), wide, wide,
                  vec, vec, vec] + [HBM_SPEC] * nx,
        out_specs=[qkv_blk, ba_blk, wide, vec, vec, vec] + [HBM_SPEC] * nx,
        out_shape=[jax.ShapeDtypeStruct((s, A_CONV_WIDTH), F32), jax.ShapeDtypeStruct((s, LANE), BF16),
                   jax.ShapeDtypeStruct((s, A_QK), BF16), vsh, vsh, vsh]
        + [jax.ShapeDtypeStruct((3,) + p.shape[1:], p.dtype) for p in exchange],
        scratch_shapes=[pltpu.VMEM((A_HEADS, A_HEAD, A_HEAD), F32)] + (_chip_scratch(nx) if nx else []),
        compiler_params=_params(("arbitrary",)),
    )(qkv, ba, z, o_raw, dy, states, t_all, w_all, vn_all, alog, dtb, wn, *exchange)


REL_RING = 1024
QBLK_BITS = 8


def _rel_ring_onehot():
    m = lax.broadcasted_iota(jnp.int32, (REL_RING, REL_PAD), 0)
    t = lax.broadcasted_iota(jnp.int32, (REL_RING, REL_PAD), 1)
    u = jnp.where(m < KBLK, m, m - REL_RING)
    idx = jnp.clip(LEFT - u, -REL_CLIP, REL_CLIP) + REL_CLIP
    return (t == idx).astype(F32)


def _relbias_ring(table, transpose):
    n_in, n_out = (REL_RING, REL_PAD) if transpose else (REL_PAD, REL_RING)

    def body(t_ref, o_ref):
        o_ref[...] = _dot(t_ref[...], _rel_ring_onehot(), NN if transpose else NT, HIGHEST)

    return pl.pallas_call(
        body, name="relbias_ring_bwd" if transpose else "relbias_ring",
        out_shape=jax.ShapeDtypeStruct((B_HEADS, n_out), F32),
        in_specs=[VMEM_FULL], out_specs=VMEM_FULL,
        compiler_params=_params(),
    )(table)


def _row_bit(shape, bit):
    return ((lax.broadcasted_iota(jnp.int32, shape, 0) >> bit) & 1) == 1


def _relbias_expand(ring):
    def body(r_ref, o_ref):
        b = jnp.broadcast_to(r_ref[0], (QBLK, REL_RING))
        for bit in range(QBLK_BITS):
            b = jnp.where(_row_bit(b.shape, bit), pltpu.roll(b, 1 << bit, 1), b)
        j = lax.broadcasted_iota(jnp.int32, (QBLK, KBLK), 1)
        r = lax.broadcasted_iota(jnp.int32, (QBLK, KBLK), 0)
        lo = (r >> 6) << 6
        o_ref[0] = jnp.where((j >= lo) & (j < lo + LEFT + CHUNK), b[:, :KBLK], NEG_INF)

    return pl.pallas_call(
        body, name="relbias_expand", grid=(B_HEADS,),
        in_specs=[pl.BlockSpec((1, 1, REL_RING), lambda h: (h, 0, 0))],
        out_specs=pl.BlockSpec((1, QBLK, KBLK), lambda h: (h, 0, 0)),
        out_shape=jax.ShapeDtypeStruct((B_HEADS, QBLK, KBLK), F32),
        compiler_params=_params(("parallel",)),
    )(ring)


def _relbias_reduce(ds):
    def body(d_ref, o_ref):
        d = jnp.concatenate([d_ref[0], jnp.zeros((QBLK, REL_RING - KBLK), F32)], axis=1)
        for bit in range(QBLK_BITS):
            d = jnp.where(_row_bit(d.shape, bit), pltpu.roll(d, REL_RING - (1 << bit), 1), d)
        o_ref[0] = jnp.sum(d, axis=0, keepdims=True)

    return pl.pallas_call(
        body, name="relbias_reduce", grid=(B_HEADS,),
        in_specs=[pl.BlockSpec((1, QBLK, KBLK), lambda h: (h, 0, 0))],
        out_specs=pl.BlockSpec((1, 1, REL_RING), lambda h: (h, 0, 0)),
        out_shape=jax.ShapeDtypeStruct((B_HEADS, 1, REL_RING), F32),
        compiler_params=_params(("parallel",)),
    )(ds)


def _attn_probs(q_ref, kb, b_ref, hh, q0):
    hl = slice(hh * B_HEAD, (hh + 1) * B_HEAD)
    qh = q_ref[:, hl] * (B_HEAD ** -0.5)
    kh = kb[:, hl]
    jpos = lax.broadcasted_iota(jnp.int32, (QBLK, KBLK), 1)
    sc = _bdot(qh, kh, NT) + b_ref[hh]
    sc = jnp.where(jpos + q0 >= LEFT, sc, NEG_INF)
    mx = jnp.max(sc, axis=-1, keepdims=True)
    p = jnp.exp(sc - mx)
    return p * (1.0 / jnp.sum(p, axis=-1, keepdims=True)), qh, kh


def _attn_fwd(q, kpad, vpad, bias):
    s = q.shape[0]

    def body(q_ref, k_ref, v_ref, b_ref, o_ref):
        q0 = pl.multiple_of(pl.program_id(1) * QBLK, QBLK)
        kb = k_ref[pl.ds(q0, KBLK), :]
        vb = v_ref[pl.ds(q0, KBLK), :]
        outs = []
        for hh in range(2):
            p, _, _ = _attn_probs(q_ref, kb, b_ref, hh, q0)
            outs.append(_bdot(p, vb[:, hh * B_HEAD:(hh + 1) * B_HEAD], NN))
        o_ref[...] = jnp.concatenate(outs, axis=1).astype(BF16)

    qblk = pl.BlockSpec((QBLK, LANE), lambda g, m: (m, g))
    kblk = pl.BlockSpec((LEFT + s, LANE), lambda g, m: (0, g))
    return pl.pallas_call(
        body, name="attn_fwd", grid=(B_HEADS // 2, s // QBLK),
        in_specs=[qblk, kblk, kblk, pl.BlockSpec((2, QBLK, KBLK), lambda g, m: (g, 0, 0))],
        out_specs=qblk,
        out_shape=jax.ShapeDtypeStruct((s, D_MODEL), BF16),
        compiler_params=_params(("parallel", "arbitrary")),
    )(q, kpad, vpad, bias)


def _attn_bwd(q, kpad, vpad, bias, do):
    s = q.shape[0]

    def body(q_ref, k_ref, v_ref, b_ref, do_ref, dq_ref, dk_ref, dv_ref, db_ref):
        @pl.when(pl.program_id(1) == 0)
        def _():
            for r in (dk_ref, dv_ref, db_ref):
                r[...] = jnp.zeros_like(r)

        q0 = pl.multiple_of(pl.program_id(1) * QBLK, QBLK)
        kb = k_ref[pl.ds(q0, KBLK), :]
        vb = v_ref[pl.ds(q0, KBLK), :]
        dqs, dks, dvs = [], [], []
        for hh in range(2):
            hl = slice(hh * B_HEAD, (hh + 1) * B_HEAD)
            p, qh, kh = _attn_probs(q_ref, kb, b_ref, hh, q0)
            doh = do_ref[:, hl]
            dp = _bdot(doh, vb[:, hl], NT)
            dsc = p * (dp - jnp.sum(p * dp, axis=-1, keepdims=True))
            db_ref[hh] += dsc
            dqs.append(_bdot(dsc, kh, NN) * (B_HEAD ** -0.5))
            dks.append(_bdot(dsc, qh, TN))
            dvs.append(_bdot(p, doh, TN))
        dq_ref[...] = jnp.concatenate(dqs, axis=1).astype(BF16)
        dk_ref[pl.ds(q0, KBLK), :] += jnp.concatenate(dks, axis=1)
        dv_ref[pl.ds(q0, KBLK), :] += jnp.concatenate(dvs, axis=1)

    qblk = pl.BlockSpec((QBLK, LANE), lambda g, m: (m, g))
    kblk = pl.BlockSpec((LEFT + s, LANE), lambda g, m: (0, g))
    bblk = pl.BlockSpec((2, QBLK, KBLK), lambda g, m: (g, 0, 0))
    return pl.pallas_call(
        body, name="attn_bwd", grid=(B_HEADS // 2, s // QBLK),
        in_specs=[qblk, kblk, kblk, bblk, qblk],
        out_specs=[qblk, kblk, kblk, bblk],
        out_shape=[jax.ShapeDtypeStruct((s, D_MODEL), BF16), jax.ShapeDtypeStruct((LEFT + s, D_MODEL), F32),
                   jax.ShapeDtypeStruct((LEFT + s, D_MODEL), F32),
                   jax.ShapeDtypeStruct((B_HEADS, QBLK, KBLK), F32)],
        compiler_params=_params(("parallel", "arbitrary")),
    )(q, kpad, vpad, bias, do)


def _adamw(w, g, m, v):
    r, c = w.shape
    tr = r
    for cand in (512, 256, 128, 64, 32, 16, 8):
        if r % cand == 0 and cand * c * 4 <= 2 * 1024 * 1024:
            tr = cand
            break
    c1 = 1.0 / (1.0 - ADAM_B1 ** ADAM_STEP)
    c2 = 1.0 / (1.0 - ADAM_B2 ** ADAM_STEP)

    def body(w_ref, g_ref, m_ref, v_ref, d_ref, mo_ref, vo_ref):
        gv = g_ref[...]
        mn = ADAM_B1 * m_ref[...] + (1.0 - ADAM_B1) * gv
        vn = ADAM_B2 * v_ref[...] + (1.0 - ADAM_B2) * (gv * gv)
        mo_ref[...] = mn
        vo_ref[...] = vn
        d_ref[...] = -ADAM_LR * ((mn * c1) / (jnp.sqrt(vn * c2) + ADAM_EPS) + ADAM_WD * w_ref[...])

    blk = pl.BlockSpec((tr, c), lambda i: (i, 0))
    sh = jax.ShapeDtypeStruct((r, c), F32)
    return pl.pallas_call(
        body, name="adamw", grid=(r // tr,),
        in_specs=[blk] * 4, out_specs=[blk] * 3, out_shape=[sh] * 3,
        compiler_params=_params(("parallel",)),
    )(w, g, m, v)


def _row(v, width=None):
    v = v.reshape(1, -1)
    if width is not None and v.shape[1] < width:
        v = jnp.pad(v, ((0, 0), (0, width - v.shape[1])))
    return v


def _gate_row(v):
    return jnp.pad(v.reshape(1, A_HEADS), ((0, 0), (A_HEADS, LANE - 2 * A_HEADS)))


DEPTH = 4
N_A = 2
N_B = 2
F_DOWN_ROWS = FFN_DIM // N_SHARDS
SQ_ROWS = D_MODEL // N_SHARDS
GD_B_Q0 = DEPTH * F_DOWN_ROWS // SQ_ROWS
GD_B_OUT0 = GD_B_Q0 + N_B
UP_COLS = 2 * FFN_DIM // N_SHARDS


def _a_layer_fwd(h, w, i, pending=None):
    xn = _rmsnorm_fwd(h, _row(w["a_norm"][i]))
    w_qkv, w_z, w_ba = w["a_in"][i]
    pre = _mm(xn, w_qkv, name="a_qkv")
    z = _mm(xn, w_z, name="a_z")
    ba = _mm(xn, w_ba, name="a_ba")
    act = _conv_silu_fwd(pre, w["a_conv"][i])
    alog, dtb, wn = _gate_row(w["a_A_log"][i]), _gate_row(w["a_dt_bias"][i]), _row(w["a_out_norm"][i])
    if pending is None:
        y, o_raw, states, t_all, w_all, vn_all = _gdn_fwd(act, ba, z, alog, dtb, wn)
    else:
        names = list(pending)
        y, o_raw, states, t_all, w_all, vn_all, *landed = _gdn_fwd(act, ba, z, alog, dtb, wn,
                                                                   gather=[pending[n] for n in names])
        w.update(zip(names, landed))
    h2 = _mm_rowsh(y, w["GA"], SQ_ROWS, i, "nn", "a_out", res=h)
    saved = dict(h=h, xn=xn, pre=pre, z=z, ba=ba, act=act, o_raw=o_raw, y=y, states=states,
                 t_all=t_all, w_all=w_all, vn_all=vn_all, alog=alog, dtb=dtb, wn=wn)
    return h2, saved


def _a_layer_bwd(dh2, w, i, sv, acc, exchange=()):
    g = {}
    acc["GA"] = _mm_rowsh_dw(sv["y"], dh2, SQ_ROWS, "a_out_dw", acc["GA"], i)
    dy = _mm_rowsh(dh2, w["GA"], SQ_ROWS, i, "nt", "a_out_dx")
    dact, dba, dz, dalog, ddtb, dwn, *arrived = _gdn_bwd(sv["act"], sv["ba"], sv["z"], sv["o_raw"], dy, sv["states"],
                                                         sv["t_all"], sv["w_all"], sv["vn_all"],
                                                         sv["alog"], sv["dtb"], sv["wn"], exchange=exchange)
    dpre, g["conv"] = _conv_silu_bwd(sv["pre"], w["a_conv"][i], dact)
    xn = sv["xn"]
    w_qkv, w_z, w_ba = w["a_in"][i]
    d_in = jnp.concatenate([_mm(xn, dpre, "tn", name="a_qkv_dw"), _mm(xn, dz, "tn", name="a_z_dw"),
                            _mm(xn, dba, "tn", name="a_ba_dw")[:, :2 * A_HEADS]], axis=1)
    g["w_in"] = jnp.transpose(d_in.reshape(D_MODEL, N_SHARDS, -1), (1, 0, 2))
    dxn = _mm(dpre, w_qkv, "nt", name="a_qkv_dx")
    dxn = _mm(dz, w_z, "nt", res=dxn, name="a_z_dx")
    dh, dnorm = _mm(dba, w_ba, "nt", res=dxn, name="a_ba_dx", norm=(sv["h"], _row(w["a_norm"][i]), dh2))
    g["norm"] = dnorm[0]
    g["A_log"] = dalog[0, A_HEADS:2 * A_HEADS]
    g["dt_bias"] = ddtb[0, A_HEADS:2 * A_HEADS]
    g["out_norm"] = dwn[0]
    return dh, g, arrived


def _by_half(a, lead):
    return jnp.moveaxis(a.reshape(a.shape[:-1] + (2, 2, UP_COLS)), (-3, -2), (0, 1)).reshape((2, 2) + lead + (UP_COLS,))


def _from_half(a):
    lead = a.shape[2:-1]
    return jnp.moveaxis(a, (0, 1), (-3, -2)).reshape(lead + (N_SHARDS * UP_COLS,))


def _ffn_fwd(h, w, l):
    s = h.shape[0]
    xn = _rmsnorm_fwd(h, _row(w["f_norm"][l]))
    cw = _by_half(w["f_conv"][l], (w["f_conv"].shape[1],))
    cb = _by_half(w["f_conv_b"][l][None], (1,))
    pre = _mm_colsh(xn, w["GU"], D_MODEL, l, "nn", "f_up").reshape(2, 2, s, UP_COLS)
    act = _ffn_act_fwd(pre, cw, cb)
    h2 = _mm_rowsh(act, w["GD"], F_DOWN_ROWS, l, "nn", "f_down", res=h)
    return h2, dict(h=h, xn=xn, pre=pre, act=act, cw=cw, cb=cb)


def _ffn_bwd(dh2, w, l, sv, acc):
    g = {}
    s = dh2.shape[0]
    late = l == 0
    down, up = ("GD0", "GU0") if late else ("GD1", "GU1")
    blk = 0 if late else l - 1
    acc[down] = _mm_rowsh_dw(sv["act"], dh2, F_DOWN_ROWS, "f_down_dw", acc[down], blk)
    dact = _mm_rowsh(dh2, w["GD"], F_DOWN_ROWS, l, "nt", "f_down_dx")
    dpre, dcw, dcb = _ffn_act_bwd(sv["pre"], sv["cw"], sv["cb"], dact)
    dpre = dpre.reshape(N_SHARDS, s, UP_COLS)
    acc[up] = _mm_colsh_dw(sv["xn"], dpre, "f_up_dw", into=acc[up], blk0=blk)
    g["conv"] = _from_half(dcw)
    g["conv_b"] = _from_half(dcb)[0]
    dh, dnorm = _mm_colsh(dpre, w["GU"], D_MODEL, l, "nt", "f_up_dx", norm=(sv["h"], _row(w["f_norm"][l]), dh2))
    g["norm"] = dnorm[0]
    return dh, g


def _b_layer_fwd(h, w, j, kpad, vpad):
    xn = _rmsnorm_fwd(h, _row(w["b_norm"][j]))
    q = _mm_rowsh(xn, w["GD"], SQ_ROWS, GD_B_Q0 + j, "nn", "b_q", out_dtype=BF16)
    rel = w["b_rel_bias"][j]
    table = jnp.pad(rel, ((0, 0), (0, REL_PAD - rel.shape[1])))
    bias = _relbias_expand(_relbias_ring(table, False).reshape(B_HEADS, 1, REL_RING))
    o = _attn_fwd(q, kpad, vpad, bias)
    h2 = _mm_rowsh(o, w["GD"], SQ_ROWS, GD_B_OUT0 + j, "nn", "b_out", res=h)
    return h2, dict(h=h, xn=xn, q=q, o=o, bias=bias)


def _b_layer_bwd(dh2, w, j, sv, kpad, vpad, acc):
    g = {}
    acc["GB"] = _mm_rowsh_dw(sv["o"], dh2, SQ_ROWS, "b_out_dw", acc["GB"], N_B + j)
    do = _mm_rowsh(dh2, w["GD"], SQ_ROWS, GD_B_OUT0 + j, "nt", "b_out_dx", out_dtype=BF16)
    dq, dkp, dvp, dsc = _attn_bwd(sv["q"], kpad, vpad, sv["bias"], do)
    dring = _relbias_reduce(dsc).reshape(B_HEADS, REL_RING)
    g["rel_bias"] = _relbias_ring(dring, True)[:, :2 * REL_CLIP + 1]
    acc["GB"] = _mm_rowsh_dw(sv["xn"], dq, SQ_ROWS, "b_q_dw", acc["GB"], j)
    dh, dnorm = _mm_rowsh(dq, w["GD"], SQ_ROWS, GD_B_Q0 + j, "nt", "b_q_dx",
                          norm=(sv["h"], _row(w["b_norm"][j]), dh2))
    g["norm"] = dnorm[0]
    return dh, g, dkp, dvp


EARLY = ("GU1", "GD1", "GB", "GK")
LATE = ("GU0", "GD0", "GA", "GI")


def _local_step(x, tgt, w, pending=None, rs_hooks=None):
    w = dict(w)
    h = x
    saved = []
    kv_saved = None
    kpad = vpad = None
    for layer in range(DEPTH):
        if layer < N_A:
            h, sm = _a_layer_fwd(h, w, layer, pending if layer == 0 else None)
        else:
            if layer == N_A:
                xn_kv = _rmsnorm_fwd(h, _row(w["kv_norm"]))
                kv = _mm_colsh(xn_kv, w["GK"], D_MODEL, 0, "nn", "kv", flat=True, out_dtype=BF16)
                kpad = jnp.pad(kv[:, :D_MODEL], ((LEFT, 0), (0, 0)))
                vpad = jnp.pad(kv[:, D_MODEL:], ((LEFT, 0), (0, 0)))
                kv_saved = dict(h=h, xn=xn_kv)
            h, sm = _b_layer_fwd(h, w, layer - N_A, kpad, vpad)
        h, sf = _ffn_fwd(h, w, layer)
        saved.append((sm, sf))

    loss, dh, dfinal = _final_loss(h, _row(w["final_norm"]), tgt)

    ga = [None] * N_A
    gb = [None] * N_B
    gf = [None] * DEPTH
    dk_tot = dv_tot = None
    g_kv = g_kvn = None
    acc = dict(GU1=lax.empty((N_SHARDS, (DEPTH - 1) * D_MODEL, UP_COLS), F32),
               GD1=lax.empty((N_SHARDS, (DEPTH - 1) * F_DOWN_ROWS, D_MODEL), F32),
               GB=lax.empty((N_SHARDS, 2 * N_B * SQ_ROWS, D_MODEL), F32),
               GU0=lax.empty((N_SHARDS, D_MODEL, UP_COLS), F32),
               GD0=lax.empty((N_SHARDS, F_DOWN_ROWS, D_MODEL), F32),
               GA=lax.empty((N_SHARDS, N_A * SQ_ROWS, D_MODEL), F32))
    early_state = None
    for layer in reversed(range(DEPTH)):
        sm, sf = saved[layer]
        dh, gf[layer] = _ffn_bwd(dh, w, layer, sf, acc)
        if layer >= N_A:
            dh, gb[layer - N_A], dkp, dvp = _b_layer_bwd(dh, w, layer - N_A, sm, kpad, vpad, acc)
            dk_tot = dkp if dk_tot is None else dk_tot + dkp
            dv_tot = dvp if dv_tot is None else dv_tot + dvp
            if layer == N_A:
                dkv = jnp.concatenate([dk_tot[LEFT:], dv_tot[LEFT:]], axis=1).astype(BF16)
                g_kv = _mm_colsh_dw(kv_saved["xn"], dkv, "kv_dw", flat=True)
                dh, g_kvn = _mm_colsh(dkv, w["GK"], D_MODEL, 0, "nt", "kv_dx", flat=True,
                                      norm=(kv_saved["h"], _row(w["kv_norm"]), dh))
        elif layer == N_A - 1 and rs_hooks is not None:
            prepare, finish = rs_hooks
            early_state, outgoing = prepare([acc[n] if n in acc else g_kv for n in EARLY])
            dh, ga[layer], arrived = _a_layer_bwd(dh, w, layer, sm, acc, exchange=outgoing)
            early_state = finish(early_state, arrived)
        else:
            dh, ga[layer], _ = _a_layer_bwd(dh, w, layer, sm, acc)

    def stack(gs, key):
        return jnp.stack([g[key] for g in gs])

    grads = dict(
        acc, GK=g_kv, early=early_state,
        GI=jnp.concatenate([g["w_in"] for g in ga], axis=1),
        a_norm=stack(ga, "norm"), a_conv=stack(ga, "conv"), a_A_log=stack(ga, "A_log"),
        a_dt_bias=stack(ga, "dt_bias"), a_out_norm=stack(ga, "out_norm"), kv_norm=g_kvn[0],
        b_norm=stack(gb, "norm"), b_rel_bias=stack(gb, "rel_bias"),
        f_norm=stack(gf, "norm"), f_conv=stack(gf, "conv"), f_conv_b=stack(gf, "conv_b"), final_norm=dfinal[0])
    return loss, dh, grads


HBM_SPEC = pl.BlockSpec(memory_space=pl.ANY)
VMEM_SPEC = pl.BlockSpec(memory_space=pltpu.VMEM)


def _place():
    x, y, c = lax.axis_index("x"), lax.axis_index("y"), lax.axis_index("c")
    chips = [(1 - x, y), (x, 1 - y), (1 - x, 1 - y)]
    return x, y, c, chips


def _remote(src, dst, send_sem, recv_sem, to):
    return pltpu.make_async_remote_copy(src_ref=src, dst_ref=dst, send_sem=send_sem, recv_sem=recv_sem,
                                        device_id=to, device_id_type=MESH)


GATHER_COPIES = 7


class _ShardGather:
    def __init__(self, x_refs, out_refs, send_sems, recv_sems):
        n = len(x_refs)
        x, y, c, chips = _place()
        sibling = (x, y, 1 - c)

        def half(a, px, py, hc):
            rh = x_refs[a].shape[0] // 2
            return out_refs[a].at[2 * px + py, pl.ds(hc * rh, rh), :]

        def mine(a):
            rh = x_refs[a].shape[0] // 2
            return x_refs[a].at[pl.ds(c * rh, rh), :]

        def sems(a, k):
            return send_sems.at[GATHER_COPIES * a + k], recv_sems.at[GATHER_COPIES * a + k]

        order = [(j, chip, a) for j, chip in enumerate(chips) for a in range(n)]
        self.first = [_remote(mine(a), half(a, x, y, c), *sems(a, j), (*chip, c)) for j, chip, a in order]
        self.own = [_remote(x_refs[a], out_refs[a].at[2 * x + y], *sems(a, 6), sibling) for a in range(n)]
        self.landed = [_remote(half(a, *chip, c), half(a, *chip, c), *sems(a, j), (*chip, c)) for j, chip, a in order]
        self.passed = [_remote(half(a, *chip, c), half(a, *chip, c), *sems(a, 3 + j), sibling)
                       for j, chip, a in order]
        self.theirs = [_remote(half(a, *chip, 1 - c), half(a, *chip, 1 - c), *sems(a, 3 + j), sibling)
                       for j, chip, a in order]

    def start(self):
        for cp in self.first + self.own:
            cp.start()

    def finish(self):
        for arrived, onward in zip(self.landed, self.passed):
            arrived.wait_recv()
            onward.start()
        for cp in self.theirs + self.own:
            cp.wait_recv()
        for cp in self.first + self.passed + self.own:
            cp.wait_send()


def _gather_scratch(n):
    return [pltpu.SemaphoreType.DMA((GATHER_COPIES * n,)), pltpu.SemaphoreType.DMA((GATHER_COPIES * n,))]


def _allgather_weights(shards):
    n = len(shards)

    def body(*refs):
        gather = _ShardGather(refs[:n], refs[n:2 * n], *refs[2 * n:])
        gather.start()
        gather.finish()

    return pl.pallas_call(
        body, name="allgather_weights",
        out_shape=[jax.ShapeDtypeStruct((N_SHARDS,) + sh.shape, sh.dtype) for sh in shards],
        in_specs=[HBM_SPEC] * n, out_specs=[HBM_SPEC] * n,
        scratch_shapes=_gather_scratch(n),
    )(*shards)


def _pair_exchange(gs):
    n = len(gs)

    def body(*refs):
        g_refs, out_refs, (send_sems, recv_sems) = refs[:n], refs[n:2 * n], refs[2 * n:]
        x, y, c, _ = _place()
        cps = []
        for a in range(n):
            rh = gs[a].shape[1] // 2
            cps.append(_remote(g_refs[a].at[:, pl.ds((1 - c) * rh, rh), :], out_refs[a], send_sems.at[a],
                               recv_sems.at[a], (x, y, 1 - c)))
        for cp in cps:
            cp.start()
        for cp in cps:
            cp.wait()

    return pl.pallas_call(
        body, name="rs_pair_exchange",
        out_shape=[jax.ShapeDtypeStruct((g.shape[0], g.shape[1] // 2, g.shape[2]), g.dtype) for g in gs],
        in_specs=[HBM_SPEC] * n, out_specs=[HBM_SPEC] * n,
        scratch_shapes=[pltpu.SemaphoreType.DMA((n,)), pltpu.SemaphoreType.DMA((n,))],
    )(*gs)


def _add_rows(rows, cols):
    best = 16
    for t in range(16, rows + 1, 16):
        if rows % t == 0 and t * cols * 4 <= 2304 * 1024:
            best = t
    return best


def _pair_add(g, other, c_idx):
    n, r, cols = g.shape
    rh = r // 2
    tr = _add_rows(rh, cols)
    nb = rh // tr

    def body(c_ref, a_ref, b_ref, o_ref, ob_ref):
        sm = a_ref[...] + b_ref[...]
        o_ref[...] = sm
        ob_ref[...] = sm.astype(BF16)

    out_blk = pl.BlockSpec((1, tr, cols), lambda s, i, c_ref: (s, i, 0))
    return pl.pallas_call(
        body, name="rs_pair_add",
        grid_spec=pltpu.PrefetchScalarGridSpec(
            num_scalar_prefetch=1, grid=(n, nb),
            in_specs=[pl.BlockSpec((1, tr, cols), lambda s, i, c_ref: (s, c_ref[0] * nb + i, 0)), out_blk],
            out_specs=[out_blk, out_blk]),
        out_shape=[jax.ShapeDtypeStruct((n, rh, cols), F32), jax.ShapeDtypeStruct((n, rh, cols), BF16)],
        compiler_params=_params(("parallel", "parallel")),
    )(c_idx, g, other)


class _ChipExchange:
    def __init__(self, p_refs, out_refs, send_sems, recv_sems):
        x, y, c, chips = _place()
        self.copies = [_remote(p_refs[a].at[2 * chip[0] + chip[1]], out_refs[a].at[j], send_sems.at[3 * a + j],
                               recv_sems.at[3 * a + j], (*chip, c))
                       for a in range(len(p_refs)) for j, chip in enumerate(chips)]

    def start(self):
        for cp in self.copies:
            cp.start()

    def finish(self):
        for cp in self.copies:
            cp.wait()


def _chip_scratch(n):
    return [pltpu.SemaphoreType.DMA((3 * n,)), pltpu.SemaphoreType.DMA((3 * n,))]


def _chip_exchange(ps):
    n = len(ps)

    def body(*refs):
        exchange = _ChipExchange(refs[:n], refs[n:2 * n], *refs[2 * n:])
        exchange.start()
        exchange.finish()

    return pl.pallas_call(
        body, name="rs_chip_exchange",
        out_shape=[jax.ShapeDtypeStruct((3,) + p.shape[1:], p.dtype) for p in ps],
        in_specs=[HBM_SPEC] * n, out_specs=[HBM_SPEC] * n,
        scratch_shapes=_chip_scratch(n),
    )(*ps)


def _chip_add(p, recv, chip_idx):
    n, rh, cols = p.shape
    tr = _add_rows(rh, cols)

    def body(s_ref, own_ref, r_ref, o_ref):
        o_ref[...] = ((own_ref[0] + r_ref[0].astype(F32)) + r_ref[1].astype(F32)) + r_ref[2].astype(F32)

    return pl.pallas_call(
        body, name="rs_chip_add",
        grid_spec=pltpu.PrefetchScalarGridSpec(
            num_scalar_prefetch=1, grid=(rh // tr,),
            in_specs=[pl.BlockSpec((1, tr, cols), lambda i, s_ref: (s_ref[0], i, 0)),
                      pl.BlockSpec((3, tr, cols), lambda i, s_ref: (0, i, 0))],
            out_specs=pl.BlockSpec((tr, cols), lambda i, s_ref: (i, 0))),
        out_shape=jax.ShapeDtypeStruct((rh, cols), p.dtype),
        compiler_params=_params(("parallel",)),
    )(chip_idx, p, recv)


def _pair_gather(fs):
    n = len(fs)

    def body(*refs):
        f_refs, out_refs, (send_sems, recv_sems) = refs[:n], refs[n:2 * n], refs[2 * n:]
        x, y, c, _ = _place()
        cps = [_remote(f_refs[a], out_refs[a], send_sems.at[a], recv_sems.at[a], (x, y, 1 - c)) for a in range(n)]
        for cp in cps:
            cp.start()
        for cp in cps:
            cp.wait()

    return pl.pallas_call(
        body, name="rs_pair_gather",
        out_shape=[jax.ShapeDtypeStruct(f.shape, f.dtype) for f in fs],
        in_specs=[HBM_SPEC] * n, out_specs=[HBM_SPEC] * n,
        scratch_shapes=[pltpu.SemaphoreType.DMA((n,)), pltpu.SemaphoreType.DMA((n,))],
    )(*fs)


def _allreduce_small(v):
    r, cols = v.shape

    def body(x_ref, out_ref, slots, send_sems, recv_sems):
        x, y, c, _ = _place()
        bits = [(bx, by, bc) for bx in (0, 1) for by in (0, 1) for bc in (0, 1)]

        def flip(b):
            return (1 - x if b[0] else x, 1 - y if b[1] else y, 1 - c if b[2] else c)

        slots[0] = x_ref[...]
        cps = [_remote(x_ref, slots.at[k], send_sems.at[k - 1], recv_sems.at[k - 1], flip(bits[k]))
               for k in range(1, 8)]
        for cp in cps:
            cp.start()
        for cp in cps:
            cp.wait()
        acc = None
        for b in bits:
            fx, fy, fc = flip(b)
            term = slots[4 * fx + 2 * fy + fc]
            acc = term if acc is None else acc + term
        out_ref[...] = acc

    return pl.pallas_call(
        body, name="allreduce_small",
        out_shape=jax.ShapeDtypeStruct((r, cols), v.dtype),
        in_specs=[VMEM_SPEC], out_specs=VMEM_SPEC,
        scratch_shapes=[pltpu.VMEM((8, r, cols), v.dtype), pltpu.SemaphoreType.DMA((7,)),
                        pltpu.SemaphoreType.DMA((7,))],
        compiler_params=pltpu.CompilerParams(vmem_limit_bytes=VMEM_LIMIT),
    )(v)


BIG = (("a_w_in", 2), ("a_w_out", 1), ("w_kv", 1), ("b_w_q", 1), ("b_w_out", 1), ("f_w_up", 2), ("f_w_down", 1))
SMALL = (("a_norm", 1), ("a_conv", 2), ("a_A_log", None), ("a_dt_bias", None), ("a_out_norm", None),
         ("kv_norm", None), ("b_norm", None), ("b_rel_bias", None), ("f_norm", None), ("f_conv", 2),
         ("f_conv_b", None), ("final_norm", None))
WEIGHT_ORDER = ("a_norm", "a_w_in", "a_conv", "a_A_log", "a_dt_bias", "a_out_norm", "a_w_out", "kv_norm", "w_kv",
                "b_norm", "b_w_q", "b_rel_bias", "b_w_out", "f_norm", "f_w_up", "f_conv", "f_conv_b", "f_w_down",
                "final_norm")


def _pad_rows(flat, cols, quantum):
    n = flat.shape[-1]
    rows = -(-n // (cols * quantum)) * quantum
    pad = [(0, 0)] * (flat.ndim - 1) + [(0, rows * cols - n)]
    return jnp.pad(flat, pad).reshape(flat.shape[:-1] + (rows, cols))


GROUPS = ("GU", "GD", "GK", "GI", "GA")
FIRST_GROUPS = ("GI", "GA")
GD_MEMBERS = ("f_w_down", "b_w_q", "b_w_out")


def _group_shards(w, dtype):
    def two(a):
        return a.reshape(-1, a.shape[-1])

    return dict(GU=two(w["f_w_up"]).astype(dtype),
                GD=jnp.concatenate([two(w[n]) for n in GD_MEMBERS]).astype(dtype),
                GK=w["w_kv"].astype(dtype),
                GI=two(w["a_w_in"]).astype(dtype),
                GA=two(w["a_w_out"]).astype(dtype))


def _ungroup(red, shard_shapes):
    b_rows = N_B * SQ_ROWS
    flat = dict(f_w_up=jnp.concatenate([red["GU0"], red["GU1"]]), f_w_down=jnp.concatenate([red["GD0"], red["GD1"]]),
                b_w_q=red["GB"][:b_rows], b_w_out=red["GB"][b_rows:], a_w_out=red["GA"], w_kv=red["GK"],
                a_w_in=red["GI"])
    return {n: v.reshape(shard_shapes[n]) for n, v in flat.items()}


def _dense_a_in(gi):
    out = []
    for i in range(N_A):
        full = jnp.transpose(gi[:, i * D_MODEL:(i + 1) * D_MODEL], (1, 0, 2)).reshape(D_MODEL, -1)
        out.append((full[:, :A_CONV_WIDTH], full[:, A_CONV_WIDTH:A_CONV_WIDTH + A_QK],
                    jnp.pad(full[:, A_CONV_WIDTH + A_QK:], ((0, 0), (0, LANE - 2 * A_HEADS)))))
    return out


def _pack_small(values, names):
    return _pad_rows(jnp.concatenate([values[n].reshape(-1) for n in names]), LANE, SUB)


def _unpack_small(packed, shapes, names):
    flat = packed.reshape(-1)
    out, off = {}, 0
    for n in names:
        size = math.prod(shapes[n])
        out[n] = flat[off:off + size].reshape(shapes[n])
        off += size
    return out


def _adamw_nd(w, g, m, v):
    shp = w.shape
    two = (math.prod(shp[:-1]), shp[-1])
    d, mn, vn = _adamw(w.reshape(two), g.reshape(two), m.reshape(two), v.reshape(two))
    return d.reshape(shp), mn.reshape(shp), vn.reshape(shp)


def kernel(x, a_norm, a_w_in, a_conv, a_A_log, a_dt_bias, a_out_norm, a_w_out, kv_norm, w_kv, b_norm, b_w_q, b_rel_bias, b_w_out, f_norm, f_w_up, f_conv, f_conv_b, f_w_down, final_norm, loss_target, m_a_norm, m_a_w_in, m_a_conv, m_a_A_log, m_a_dt_bias, m_a_out_norm, m_a_w_out, m_kv_norm, m_w_kv, m_b_norm, m_b_w_q, m_b_rel_bias, m_b_w_out, m_f_norm, m_f_w_up, m_f_conv, m_f_conv_b, m_f_w_down, m_final_norm, v_a_norm, v_a_w_in, v_a_conv, v_a_A_log, v_a_dt_bias, v_a_out_norm, v_a_w_out, v_kv_norm, v_w_kv, v_b_norm, v_b_w_q, v_b_rel_bias, v_b_w_out, v_f_norm, v_f_w_up, v_f_conv, v_f_conv_b, v_f_w_down, v_final_norm):
    w = dict(a_norm=a_norm, a_w_in=a_w_in, a_conv=a_conv, a_A_log=a_A_log, a_dt_bias=a_dt_bias,
             a_out_norm=a_out_norm, a_w_out=a_w_out, kv_norm=kv_norm, w_kv=w_kv, b_norm=b_norm, b_w_q=b_w_q,
             b_rel_bias=b_rel_bias, b_w_out=b_w_out, f_norm=f_norm, f_w_up=f_w_up, f_conv=f_conv,
             f_conv_b=f_conv_b, f_w_down=f_w_down, final_norm=final_norm)
    m = dict(a_norm=m_a_norm, a_w_in=m_a_w_in, a_conv=m_a_conv, a_A_log=m_a_A_log, a_dt_bias=m_a_dt_bias,
             a_out_norm=m_a_out_norm, a_w_out=m_a_w_out, kv_norm=m_kv_norm, w_kv=m_w_kv, b_norm=m_b_norm,
             b_w_q=m_b_w_q, b_rel_bias=m_b_rel_bias, b_w_out=m_b_w_out, f_norm=m_f_norm, f_w_up=m_f_w_up,
             f_conv=m_f_conv, f_conv_b=m_f_conv_b, f_w_down=m_f_w_down, final_norm=m_final_norm)
    v = dict(a_norm=v_a_norm, a_w_in=v_a_w_in, a_conv=v_a_conv, a_A_log=v_a_A_log, a_dt_bias=v_a_dt_bias,
             a_out_norm=v_a_out_norm, a_w_out=v_a_w_out, kv_norm=v_kv_norm, w_kv=v_w_kv, b_norm=v_b_norm,
             b_w_q=v_b_w_q, b_rel_bias=v_b_rel_bias, b_w_out=v_b_w_out, f_norm=v_f_norm, f_w_up=v_f_w_up,
             f_conv=v_f_conv, f_conv_b=v_f_conv_b, f_w_down=v_f_w_down, final_norm=v_final_norm)
    xi, yi, ci = lax.axis_index("x"), lax.axis_index("y"), lax.axis_index("c")
    chip = 2 * xi + yi
    shard_shapes = {n: w[n].shape for n in WEIGHT_ORDER}

    mine_w = _group_shards(w, BF16)
    full = dict(zip(FIRST_GROUPS, _allgather_weights([mine_w[n] for n in FIRST_GROUPS])))
    full["a_in"] = _dense_a_in(full.pop("GI"))
    pending = {n: mine_w[n] for n in GROUPS if n not in FIRST_GROUPS}
    sharded_small = [n for n, axis in SMALL if axis is not None]
    placed = {}
    for n, axis in SMALL:
        if axis is not None:
            wide = list(w[n].shape)
            wide[axis] *= 4
            mine_once = w[n] * (1 - ci).astype(F32)
            placed[n] = lax.dynamic_update_slice_in_dim(jnp.zeros(wide, F32), mine_once, chip * w[n].shape[axis], axis)
    placed_shapes = {n: placed[n].shape for n in sharded_small}
    full.update(_unpack_small(_allreduce_small(_pack_small(placed, sharded_small)), placed_shapes, sharded_small))
    for n, axis in SMALL:
        if axis is None:
            full[n] = w[n]

    c_idx = jnp.reshape(ci, (1,)).astype(jnp.int32)
    chip_idx = jnp.reshape(chip, (1,)).astype(jnp.int32)

    def pair_stage(gs):
        pairs = [_pair_add(g, o, c_idx) for g, o in zip(gs, _pair_exchange(gs))]
        return pairs, [pb for _, pb in pairs]

    def chip_stage(pairs, arrived):
        mine = [_chip_add(p, r, chip_idx) for (p, _), r in zip(pairs, arrived)]
        theirs = _pair_gather(mine)
        return [jnp.concatenate([jnp.where(ci == 0, a, b), jnp.where(ci == 0, b, a)], axis=0)
                for a, b in zip(mine, theirs)]

    loss_part, grad_x, grads = _local_step(x[0], loss_target[0], full, pending, (pair_stage, chip_stage))
    late_pairs, outgoing = pair_stage([grads[n] for n in LATE])
    late = chip_stage(late_pairs, _chip_exchange(outgoing))
    red = _ungroup(dict(list(zip(EARLY, grads["early"])) + list(zip(LATE, late))), shard_shapes)

    small_names = [n for n, _ in SMALL]
    small_vals = {n: grads[n] for n in small_names}
    small_vals["loss"] = loss_part[0, :1]
    names = ["loss"] + small_names
    shapes = {n: small_vals[n].shape for n in names}
    summed = _unpack_small(_allreduce_small(_pack_small(small_vals, names)), shapes, names)
    loss = summed["loss"][0]
    for n, axis in SMALL:
        g = summed[n]
        if axis is not None:
            g = lax.dynamic_slice_in_dim(g, chip * w[n].shape[axis], w[n].shape[axis], axis)
        red[n] = g

    delta, new_m, new_v = {}, {}, {}
    for n, _ in BIG:
        delta[n], new_m[n], new_v[n] = _adamw_nd(w[n], red[n], m[n], v[n])
    local_shapes = {n: w[n].shape for n in small_names}
    packs = [_pack_small(t, small_names) for t in (w, red, m, v)]
    outs = _adamw(*packs)
    ds, ms, vs = (_unpack_small(o, local_shapes, small_names) for o in outs)
    delta.update(ds)
    new_m.update(ms)
    new_v.update(vs)

    return (loss, grad_x[None], *[red[n] for n in WEIGHT_ORDER], *[delta[n] for n in WEIGHT_ORDER],
            *[new_m[n] for n in WEIGHT_ORDER], *[new_v[n] for n in WEIGHT_ORDER])
```

```python
import math

import jax
import jax.numpy as jnp
from jax import lax
from jax.experimental import pallas as pl
from jax.experimental.pallas import tpu as pltpu

F32 = jnp.float32
BF16 = jnp.bfloat16
HIGHEST = lax.Precision.HIGHEST
MESH = pl.DeviceIdType.MESH

D_MODEL = 1024
CHUNK = 64
A_HEADS = 8
A_HEAD = 128
A_QK = A_HEADS * A_HEAD
A_CONV_WIDTH = 3 * A_QK
B_HEADS = 16
B_HEAD = 64
LEFT = 8 * CHUNK
QBLK = 4 * CHUNK
KBLK = LEFT + QBLK
REL_CLIP = 256
REL_PAD = 640
FFN_DIM = 2816
EPS = 1e-6
NEG_INF = -1e30
LANE = 128
SUB = 8
VMEM_LIMIT = 56 * 1024 * 1024

ADAM_LR = 0.001
ADAM_B1 = 0.9
ADAM_B2 = 0.999
ADAM_EPS = 1e-08
ADAM_WD = 0.01
ADAM_STEP = 10


VMEM_FULL = pl.BlockSpec(memory_space=pltpu.VMEM)


def _params(sem=None):
    return pltpu.CompilerParams(dimension_semantics=sem, vmem_limit_bytes=VMEM_LIMIT)


def _tile(n, cap):
    if n <= cap:
        return n
    best = None
    for t in range(LANE, cap + 1, LANE):
        if n % t == 0:
            best = t
    assert best is not None, n
    return best


def _sigmoid(x):
    return 0.5 * jnp.tanh(0.5 * x) + 0.5


def _softplus(x):
    return jnp.maximum(x, 0.0) + jnp.log(1.0 + jnp.exp(-jnp.abs(x)))


def _dot(a, b, dims, prec=None):
    return lax.dot_general(a, b, (dims, ((), ())), preferred_element_type=F32, precision=prec)


NN = ((1,), (0,))
NT = ((1,), (1,))
TN = ((0,), (0,))


def _bdot(a, b, dims):
    return _dot(a.astype(BF16), b.astype(BF16), dims)


def _mm(a, b, mode="nn", out_dtype=F32, res=None, name="mm", norm=None):
    if mode == "nn":
        (m, k), (k2, n) = a.shape, b.shape
    elif mode == "nt":
        (m, k), (n, k2) = a.shape, b.shape
    else:
        (k, m), (k2, n) = a.shape, b.shape
    assert k == k2, (a.shape, b.shape, mode)
    tm, tn, tk = _tile(m, 1408), _tile(n, 1408), _tile(k, 1408)
    if m == 8192:
        tm = 1024
    if k == 8192:
        tk = 1024
    nk = k // tk
    dims = {"nn": NN, "nt": NT, "tn": TN}[mode]
    a_spec = {"nn": pl.BlockSpec((tm, tk), lambda i, j, kk: (i, kk)),
              "nt": pl.BlockSpec((tm, tk), lambda i, j, kk: (i, kk)),
              "tn": pl.BlockSpec((tk, tm), lambda i, j, kk: (kk, i))}[mode]
    b_spec = {"nn": pl.BlockSpec((tk, tn), lambda i, j, kk: (kk, j)),
              "nt": pl.BlockSpec((tn, tk), lambda i, j, kk: (j, kk)),
              "tn": pl.BlockSpec((tk, tn), lambda i, j, kk: (kk, j))}[mode]
    o_spec = pl.BlockSpec((tm, tn), lambda i, j, kk: (i, j))
    return _mm_call(name, a, b, dims, (m // tm, n // tn, nk), a_spec, b_spec, o_spec, (m, n), out_dtype, (tm, tn),
                    res, norm)


ROW_TILE = 1024
N_SHARDS = 4


def _mm_call(name, a, b, dims, grid, a_spec, b_spec, o_spec, out_shape, out_dtype, acc_shape, res=None, norm=None,
             into=None):
    nk = grid[2]
    has_res = res is not None
    has_norm = norm is not None
    has_into = into is not None
    if has_norm:
        assert grid[1] == 1 and len(out_shape) == 2 and out_dtype == F32

    def flat(v):
        return v.reshape(-1, v.shape[-1]) if v.ndim == 3 else v

    def body(a_ref, b_ref, *rest):
        rest = list(rest)
        res_ref = rest.pop(0) if has_res else None
        x_ref, g_ref, dres_ref = (rest.pop(0), rest.pop(0), rest.pop(0)) if has_norm else (None, None, None)
        if has_into:
            rest.pop(0)
        o_ref = rest.pop(0)
        dg_ref = rest.pop(0) if has_norm else None
        acc = rest.pop(0)
        kk = pl.program_id(2)
        first_rows = pl.program_id(0) == 0

        @pl.when(kk == 0)
        def _():
            acc[...] = jnp.zeros_like(acc)

        if has_norm:
            @pl.when(first_rows & (kk == 0))
            def _():
                dg_ref[...] = jnp.zeros_like(dg_ref)

        acc[...] += _bdot(flat(a_ref[...]), flat(b_ref[...]), dims)

        @pl.when(kk == nk - 1)
        def _():
            r = acc[...]
            if has_res:
                r = r + res_ref[...]
            if not has_norm:
                o_ref[...] = r.reshape(o_ref.shape).astype(out_dtype)
                return
            xv = x_ref[...]
            rs = lax.rsqrt(jnp.mean(xv * xv, axis=-1, keepdims=True) + EPS)
            t = r * g_ref[...]
            c = jnp.mean(t * xv, axis=-1, keepdims=True)
            o_ref[...] = dres_ref[...] + rs * t - xv * (rs * rs * rs) * c
            dg_ref[...] += jnp.sum(r * xv * rs, axis=0, keepdims=True)

    args = [a, b] + ([res] if has_res else [])
    in_specs = [a_spec, b_spec] + ([o_spec] if has_res else [])
    out_specs, out_shapes = o_spec, jax.ShapeDtypeStruct(out_shape, out_dtype)
    if has_norm:
        vec = pl.BlockSpec((1, out_shape[1]), lambda i, j, kk: (0, 0))
        args += list(norm)
        in_specs += [o_spec, vec, o_spec]
        out_specs = [o_spec, vec]
        out_shapes = [out_shapes, jax.ShapeDtypeStruct((1, out_shape[1]), F32)]
    aliases = {}
    if has_into:
        assert not has_norm and into.dtype == out_dtype
        aliases = {len(args): 0}
        args.append(into)
        in_specs.append(pl.BlockSpec(memory_space=pl.ANY))
        out_shapes = jax.ShapeDtypeStruct(into.shape, out_dtype)
    return pl.pallas_call(
        body, name=name, grid=grid, in_specs=in_specs, out_specs=out_specs, out_shape=out_shapes,
        scratch_shapes=[pltpu.VMEM(acc_shape, F32)], input_output_aliases=aliases,
        compiler_params=_params(("arbitrary" if has_norm else "parallel", "parallel", "arbitrary")),
    )(*args)


def _shards_per_block(rows):
    return N_SHARDS if N_SHARDS * rows <= 1408 else 2


def _mm_rowsh(a, buf, rows, blk0, mode, name, res=None, out_dtype=F32, norm=None):
    s = a.shape[0]
    cols = buf.shape[2]
    g = _shards_per_block(rows)
    tm = _tile(s, ROW_TILE)
    b_blk = (g, rows, cols)
    if mode == "nn":
        return _mm_call(name, a, buf, NN, (s // tm, 1, N_SHARDS // g),
                        pl.BlockSpec((tm, g * rows), lambda i, j, kk: (i, kk)),
                        pl.BlockSpec(b_blk, lambda i, j, kk: (kk, blk0, 0)),
                        pl.BlockSpec((tm, cols), lambda i, j, kk: (i, 0)),
                        (s, cols), out_dtype, (tm, cols), res)
    return _mm_call(name, a, buf, NT, (s // tm, N_SHARDS // g, 1),
                    pl.BlockSpec((tm, cols), lambda i, j, kk: (i, 0)),
                    pl.BlockSpec(b_blk, lambda i, j, kk: (j, blk0, 0)),
                    pl.BlockSpec((tm, g * rows), lambda i, j, kk: (i, j)),
                    (s, N_SHARDS * rows), out_dtype, (tm, g * rows), res, norm)


def _mm_rowsh_dw(act, dy, rows, name, into, blk0):
    s = act.shape[0]
    cols = dy.shape[1]
    g = _shards_per_block(rows)
    ts = _tile(s, ROW_TILE)
    return _mm_call(name, act, dy, TN, (1, N_SHARDS // g, s // ts),
                    pl.BlockSpec((ts, g * rows), lambda i, j, kk: (kk, j)),
                    pl.BlockSpec((ts, cols), lambda i, j, kk: (kk, 0)),
                    pl.BlockSpec((g, rows, cols), lambda i, j, kk: (j, blk0, 0)),
                    into.shape, F32, (g * rows, cols), into=into)


def _mm_colsh(a, buf, krows, blk0, mode, name, flat=False, res=None, out_dtype=F32, norm=None):
    cols = buf.shape[2]
    b_nn = pl.BlockSpec((None, krows, cols), lambda i, j, kk: (j, blk0, 0))
    b_nt = pl.BlockSpec((None, krows, cols), lambda i, j, kk: (kk, blk0, 0))
    if mode == "nn":
        s = a.shape[0]
        tm = _tile(s, ROW_TILE)
        o_spec = (pl.BlockSpec((tm, cols), lambda i, j, kk: (i, j)) if flat
                  else pl.BlockSpec((None, tm, cols), lambda i, j, kk: (j, i, 0)))
        return _mm_call(name, a, buf, NN, (s // tm, N_SHARDS, 1),
                        pl.BlockSpec((tm, krows), lambda i, j, kk: (i, 0)), b_nn, o_spec,
                        (s, N_SHARDS * cols) if flat else (N_SHARDS, s, cols), out_dtype, (tm, cols), res)
    s = a.shape[0] if flat else a.shape[1]
    tm = _tile(s, ROW_TILE)
    a_spec = (pl.BlockSpec((tm, cols), lambda i, j, kk: (i, kk)) if flat
              else pl.BlockSpec((None, tm, cols), lambda i, j, kk: (kk, i, 0)))
    return _mm_call(name, a, buf, NT, (s // tm, 1, N_SHARDS), a_spec, b_nt,
                    pl.BlockSpec((tm, krows), lambda i, j, kk: (i, 0)),
                    (s, krows), out_dtype, (tm, krows), res, norm)


def _mm_colsh_dw(x, dy, name, flat=False, into=None, blk0=0):
    s, k = x.shape
    cols = dy.shape[1] // N_SHARDS if flat else dy.shape[2]
    ts = _tile(s, ROW_TILE)
    b_spec = (pl.BlockSpec((ts, cols), lambda i, j, kk: (kk, j)) if flat
              else pl.BlockSpec((None, ts, cols), lambda i, j, kk: (j, kk, 0)))
    return _mm_call(name, x, dy, TN, (1, N_SHARDS, s // ts),
                    pl.BlockSpec((ts, k), lambda i, j, kk: (kk, 0)), b_spec,
                    pl.BlockSpec((None, k, cols), lambda i, j, kk: (j, blk0, 0)),
                    (N_SHARDS, k, cols) if into is None else into.shape, F32, (k, cols), into=into)


def _rmsnorm_fwd(x, g):
    s, d = x.shape
    tr = _tile(s, 1024)

    def body(x_ref, g_ref, o_ref):
        xv = x_ref[...]
        r = lax.rsqrt(jnp.mean(xv * xv, axis=-1, keepdims=True) + EPS)
        o_ref[...] = (xv * r * g_ref[...]).astype(BF16)

    return pl.pallas_call(
        body, name="rmsnorm_fwd", grid=(s // tr,),
        in_specs=[pl.BlockSpec((tr, d), lambda i: (i, 0)), pl.BlockSpec((1, d), lambda i: (0, 0))],
        out_specs=pl.BlockSpec((tr, d), lambda i: (i, 0)),
        out_shape=jax.ShapeDtypeStruct((s, d), BF16),
        compiler_params=_params(("parallel",)),
    )(x, g)


def _final_loss(h, g, tgt):
    s, d = h.shape
    tr = _tile(s, 1024)

    def body(x_ref, g_ref, t_ref, loss_ref, dx_ref, dg_ref):
        @pl.when(pl.program_id(0) == 0)
        def _():
            dg_ref[...] = jnp.zeros_like(dg_ref)
            loss_ref[...] = jnp.zeros_like(loss_ref)

        xv = x_ref[...]
        r = lax.rsqrt(jnp.mean(xv * xv, axis=-1, keepdims=True) + EPS)
        xh = xv * r
        err = xh * g_ref[...] - t_ref[...]
        per_row = jnp.mean(err * err, axis=-1, keepdims=True)
        loss_ref[...] += 0.5 * jnp.sum(per_row, axis=0, keepdims=True)
        dy = err * (1.0 / d)
        t = dy * g_ref[...]
        c = jnp.mean(t * xv, axis=-1, keepdims=True)
        dx_ref[...] = r * t - xv * (r * r * r) * c
        dg_ref[...] += jnp.sum(dy * xh, axis=0, keepdims=True)

    row = pl.BlockSpec((tr, d), lambda i: (i, 0))
    vec = pl.BlockSpec((1, d), lambda i: (0, 0))
    return pl.pallas_call(
        body, name="final_loss", grid=(s // tr,),
        in_specs=[row, vec, row],
        out_specs=[pl.BlockSpec((1, LANE), lambda i: (0, 0)), row, vec],
        out_shape=[jax.ShapeDtypeStruct((1, LANE), F32), jax.ShapeDtypeStruct((s, d), F32),
                   jax.ShapeDtypeStruct((1, d), F32)],
        compiler_params=_params(("arbitrary",)),
    )(h, g, tgt)


CONV_ROWS = 512
CONV_COLS = 1408
CONV_CHUNK = 64


def _per_lane_tile(tile_body):
    def body(*refs):
        for lt in range(refs[0].shape[-1] // LANE):
            cols = slice(lt * LANE, (lt + 1) * LANE)
            tile_body(*[r.at[(slice(None),) * (len(r.shape) - 1) + (cols,)] for r in refs])
    return body


def _lagged(window, lag):
    return (pltpu.roll(window, lag, 0) if lag else window)[SUB:]


def _led(window, lead):
    n = window.shape[0] - SUB
    return (pltpu.roll(window, window.shape[0] - lead, 0) if lead else window)[:n]


def _fold_rows(v):
    return jnp.sum(v.reshape(v.shape[0] // SUB, SUB, v.shape[1]), axis=0)


def _taps(shifted, w):
    acc = None
    for k, xs in enumerate(shifted):
        term = xs * w[k:k + 1, :]
        acc = term if acc is None else acc + term
    return acc


def _conv_tiles(s, c):
    return _tile(s, CONV_ROWS), _tile(c, CONV_COLS)


def _conv_silu_fwd(pre, w):
    s, c = pre.shape
    width = w.shape[0]
    tr, tc = _conv_tiles(s, c)

    def body(x_ref, w_ref, o_ref, tail):
        @pl.when(pl.program_id(1) == 0)
        def _():
            tail[...] = jnp.zeros_like(tail)

        wv = w_ref[...]

        def do(c0, window):
            y = _taps([_lagged(window, width - 1 - k) for k in range(width)], wv)
            o_ref[pl.ds(c0, CONV_CHUNK), :] = y * _sigmoid(y)

        def chunk(ci, carry):
            c0 = pl.multiple_of(ci * CONV_CHUNK, CONV_CHUNK)
            do(c0, x_ref[pl.ds(pl.multiple_of(c0 - SUB, SUB), CONV_CHUNK + SUB), :])
            return carry

        do(0, jnp.concatenate([tail[...], x_ref[:CONV_CHUNK, :]], axis=0))
        lax.fori_loop(1, tr // CONV_CHUNK, chunk, 0)
        tail[...] = x_ref[tr - SUB:, :]

    blk = pl.BlockSpec((tr, tc), lambda j, i: (i, j))
    return pl.pallas_call(
        _per_lane_tile(body), name="conv_silu_fwd", grid=(c // tc, s // tr),
        in_specs=[blk, pl.BlockSpec((width, tc), lambda j, i: (0, j))], out_specs=blk,
        out_shape=jax.ShapeDtypeStruct((s, c), F32),
        scratch_shapes=[pltpu.VMEM((SUB, tc), F32)],
        compiler_params=_params(("parallel", "arbitrary")),
    )(pre, w)


def _prev_rows_index(i_blk, tr):
    return jnp.maximum(i_blk * (tr // SUB) - 1, 0)


def _conv_silu_bwd(pre, w, dact):
    s, c = pre.shape
    width = w.shape[0]
    tr, tc = _conv_tiles(s, c)
    nr = s // tr

    nchunks = tr // CONV_CHUNK

    def body(x_ref, p_ref, w_ref, d_ref, dx_ref, dw_ref, head):
        @pl.when(pl.program_id(1) == 0)
        def _():
            head[...] = jnp.zeros_like(head)
            dw_ref[...] = jnp.zeros_like(dw_ref)

        wv = w_ref[...]

        def do(c0, window, later, dws):
            xs = [_lagged(window, width - 1 - k) for k in range(width)]
            y = _taps(xs, wv)
            sg = _sigmoid(y)
            dy = d_ref[pl.ds(c0, CONV_CHUNK), :] * sg * (1.0 + y * (1.0 - sg))
            dws = tuple(dw + _fold_rows(dy * x) for dw, x in zip(dws, xs))
            ahead = jnp.concatenate([dy, later], axis=0)
            dx_ref[pl.ds(c0, CONV_CHUNK), :] = _taps([_led(ahead, width - 1 - k) for k in range(width)],
                                                     wv).astype(BF16)
            return dy[:SUB], dws

        def chunk(it, carry):
            c0 = pl.multiple_of((nchunks - 1 - it) * CONV_CHUNK, CONV_CHUNK)
            return do(c0, x_ref[pl.ds(pl.multiple_of(c0 - SUB, SUB), CONV_CHUNK + SUB), :], *carry)

        zero = jnp.zeros((SUB, LANE), F32)
        carry = lax.fori_loop(0, nchunks - 1, chunk, (head[...], (zero,) * width))
        before = jnp.where(pl.program_id(1) == nr - 1, 0.0, p_ref[...])
        later, dws = do(0, jnp.concatenate([before, x_ref[:CONV_CHUNK, :]], axis=0), *carry)
        head[...] = later
        for k in range(width):
            dw_ref[k:k + 1, :] += jnp.sum(dws[k], axis=0, keepdims=True)

    blk = pl.BlockSpec((tr, tc), lambda j, i: (nr - 1 - i, j))
    prev = pl.BlockSpec((SUB, tc), lambda j, i: (_prev_rows_index(nr - 1 - i, tr), j))
    wblk = pl.BlockSpec((width, tc), lambda j, i: (0, j))
    return pl.pallas_call(
        _per_lane_tile(body), name="conv_silu_bwd", grid=(c // tc, nr),
        in_specs=[blk, prev, wblk, blk], out_specs=[blk, wblk],
        out_shape=[jax.ShapeDtypeStruct((s, c), BF16), jax.ShapeDtypeStruct((width, c), F32)],
        scratch_shapes=[pltpu.VMEM((SUB, tc), F32)],
        compiler_params=_params(("parallel", "arbitrary")),
    )(pre, pre, w, dact)


def _ffn_act_fwd(pre, w, b):
    _, halves, s, c = pre.shape
    width = w.shape[2]
    tr, tc = _conv_tiles(s, c)
    ncb = c // tc

    def body(x_ref, w_ref, b_ref, o_ref, tail):
        @pl.when(pl.program_id(2) == 0)
        def _():
            tail[...] = jnp.zeros_like(tail)

        wg, wv, bg, bv = w_ref[0], w_ref[1], b_ref[0], b_ref[1]

        def do(c0, win_g, win_v):
            yg = _taps([_lagged(win_g, width - 1 - k) for k in range(width)], wg) + bg
            yv = _taps([_lagged(win_v, width - 1 - k) for k in range(width)], wv) + bv
            o_ref[pl.ds(c0, CONV_CHUNK), :] = (yg * _sigmoid(yg) * yv).astype(BF16)

        def chunk(ci, carry):
            c0 = pl.multiple_of(ci * CONV_CHUNK, CONV_CHUNK)
            rows = pl.ds(pl.multiple_of(c0 - SUB, SUB), CONV_CHUNK + SUB)
            do(c0, x_ref[0, rows, :], x_ref[1, rows, :])
            return carry

        do(0, jnp.concatenate([tail[0], x_ref[0, :CONV_CHUNK, :]], axis=0),
           jnp.concatenate([tail[1], x_ref[1, :CONV_CHUNK, :]], axis=0))
        lax.fori_loop(1, tr // CONV_CHUNK, chunk, 0)
        tail[...] = x_ref[:, tr - SUB:, :]

    return pl.pallas_call(
        _per_lane_tile(body), name="ffn_act_fwd", grid=(halves, ncb, s // tr),
        in_specs=[pl.BlockSpec((2, None, tr, tc), lambda h, j, i: (0, h, i, j)),
                  pl.BlockSpec((2, None, width, tc), lambda h, j, i: (0, h, 0, j)),
                  pl.BlockSpec((2, None, 1, tc), lambda h, j, i: (0, h, 0, j))],
        out_specs=pl.BlockSpec((tr, tc), lambda h, j, i: (i, h * ncb + j)),
        out_shape=jax.ShapeDtypeStruct((s, halves * c), BF16),
        scratch_shapes=[pltpu.VMEM((2, SUB, tc), F32)],
        compiler_params=_params(("parallel", "parallel", "arbitrary")),
    )(pre, w, b)


def _ffn_act_bwd(pre, w, b, dact):
    _, halves, s, c = pre.shape
    width = w.shape[2]
    tr, tc = _conv_tiles(s, c)
    ncb = c // tc
    nr = s // tr
    nchunks = tr // CONV_CHUNK

    def body(x_ref, p_ref, w_ref, b_ref, d_ref, dx_ref, dw_ref, db_ref, head):
        @pl.when(pl.program_id(2) == 0)
        def _():
            for r in (head, dw_ref, db_ref):
                r[...] = jnp.zeros_like(r)

        wg, wv, bg, bv = w_ref[0], w_ref[1], b_ref[0], b_ref[1]

        def do(c0, win_g, win_v, later_g, later_v, dwg, dwv, dbg, dbv):
            xg = [_lagged(win_g, width - 1 - k) for k in range(width)]
            xv = [_lagged(win_v, width - 1 - k) for k in range(width)]
            yg = _taps(xg, wg) + bg
            yv = _taps(xv, wv) + bv
            sg = _sigmoid(yg)
            da = d_ref[pl.ds(c0, CONV_CHUNK), :]
            dyv = da * yg * sg
            dyg = da * yv * sg * (1.0 + yg * (1.0 - sg))
            dwg = tuple(dw + _fold_rows(dyg * x) for dw, x in zip(dwg, xg))
            dwv = tuple(dw + _fold_rows(dyv * x) for dw, x in zip(dwv, xv))
            dbg = dbg + _fold_rows(dyg)
            dbv = dbv + _fold_rows(dyv)
            ahead_g = jnp.concatenate([dyg, later_g], axis=0)
            ahead_v = jnp.concatenate([dyv, later_v], axis=0)
            dxg = _taps([_led(ahead_g, width - 1 - k) for k in range(width)], wg)
            dxv = _taps([_led(ahead_v, width - 1 - k) for k in range(width)], wv)
            dx_ref[0, pl.ds(c0, CONV_CHUNK), :] = dxg.astype(BF16)
            dx_ref[1, pl.ds(c0, CONV_CHUNK), :] = dxv.astype(BF16)
            return dyg[:SUB], dyv[:SUB], dwg, dwv, dbg, dbv

        def chunk(it, carry):
            c0 = pl.multiple_of((nchunks - 1 - it) * CONV_CHUNK, CONV_CHUNK)
            rows = pl.ds(pl.multiple_of(c0 - SUB, SUB), CONV_CHUNK + SUB)
            return do(c0, x_ref[0, rows, :], x_ref[1, rows, :], *carry)

        zero = jnp.zeros((SUB, LANE), F32)
        carry = lax.fori_loop(0, nchunks - 1, chunk,
                              (head[0], head[1], (zero,) * width, (zero,) * width, zero, zero))
        before = jnp.where(pl.program_id(2) == nr - 1, 0.0, p_ref[...])
        later_g, later_v, dwg, dwv, dbg, dbv = do(
            0, jnp.concatenate([before[0], x_ref[0, :CONV_CHUNK, :]], axis=0),
            jnp.concatenate([before[1], x_ref[1, :CONV_CHUNK, :]], axis=0), *carry)
        head[0] = later_g
        head[1] = later_v
        db_ref[0] += jnp.sum(dbg, axis=0, keepdims=True)
        db_ref[1] += jnp.sum(dbv, axis=0, keepdims=True)
        for k in range(width):
            dw_ref[0, k:k + 1, :] += jnp.sum(dwg[k], axis=0, keepdims=True)
            dw_ref[1, k:k + 1, :] += jnp.sum(dwv[k], axis=0, keepdims=True)

    blk = pl.BlockSpec((2, None, tr, tc), lambda h, j, i: (0, h, nr - 1 - i, j))
    prev = pl.BlockSpec((2, None, SUB, tc), lambda h, j, i: (0, h, _prev_rows_index(nr - 1 - i, tr), j))
    wblk = pl.BlockSpec((2, None, width, tc), lambda h, j, i: (0, h, 0, j))
    bblk = pl.BlockSpec((2, None, 1, tc), lambda h, j, i: (0, h, 0, j))
    return pl.pallas_call(
        _per_lane_tile(body), name="ffn_act_bwd", grid=(halves, ncb, nr),
        in_specs=[blk, prev, wblk, bblk, pl.BlockSpec((tr, tc), lambda h, j, i: (nr - 1 - i, h * ncb + j))],
        out_specs=[blk, wblk, bblk],
        out_shape=[jax.ShapeDtypeStruct(pre.shape, BF16), jax.ShapeDtypeStruct(w.shape, F32),
                   jax.ShapeDtypeStruct(b.shape, F32)],
        scratch_shapes=[pltpu.VMEM((2, SUB, tc), F32)],
        compiler_params=_params(("parallel", "parallel", "arbitrary")),
    )(pre, pre, w, b, dact)


def _tri_masks():
    row = lax.broadcasted_iota(jnp.int32, (CHUNK, CHUNK), 0)
    col = lax.broadcasted_iota(jnp.int32, (CHUNK, CHUNK), 1)
    return row, col


def _tri_inv(ms, row, col):
    eye = (row == col).astype(F32)
    same_blk = (row >> 4) == (col >> 4)
    mds = [jnp.where(same_blk, m, 0.0) for m in ms]
    offs = [m - md for m, md in zip(ms, mds)]
    xs = [eye - md for md in mds]
    ps = [_bdot(md, md, NN) for md in mds]
    for _ in range(2):
        rs = [_bdot(jnp.concatenate([x, p], axis=0), p, NN) for x, p in zip(xs, ps)]
        xs = [x + r[:CHUNK] for x, r in zip(xs, rs)]
        ps = [r[CHUNK:] for r in rs]
    xs = [x + _bdot(x, p, NN) for x, p in zip(xs, ps)]
    ps = [_bdot(x, off, NN) for x, off in zip(xs, offs)]
    pps = [_bdot(p, p, NN) for p in ps]
    ys = [eye - p for p in ps]
    ys = [y + _bdot(y, pp, NN) for y, pp in zip(ys, pps)]
    return [_bdot(y, x, NN) for y, x in zip(ys, xs)]


def _gdn_gates(ba, alog, dtb, row, col):
    sig = _sigmoid(ba)
    neg_a = -jnp.exp(alog)
    g = neg_a * _softplus(ba + dtb)
    lower = (row >= col).astype(F32)
    gcum = _dot(lower, g, NN, HIGHEST)
    return sig, neg_a, g, gcum


def _gdn_head_common(q_raw, k_raw, v, beta, gc, gr, row, col):
    causal = row >= col
    strict = row > col
    rq = lax.rsqrt(jnp.sum(q_raw * q_raw, axis=-1, keepdims=True) + EPS)
    rk = lax.rsqrt(jnp.sum(k_raw * k_raw, axis=-1, keepdims=True) + EPS)
    q = q_raw * (rq * (A_HEAD ** -0.5))
    k = k_raw * rk
    decay = jnp.where(causal, jnp.exp(jnp.where(causal, gc - gr, 0.0)), 0.0)
    eg = jnp.exp(gc)
    gl = gc[CHUNK - 1:CHUNK, :]
    ekl = jnp.exp(gl - gc)
    dec = jnp.exp(gl)
    kb = k * beta
    kbq = jnp.concatenate([kb, q], axis=0)
    both = _bdot(kbq, k, NT)
    kk, qk = both[:CHUNK], both[CHUNK:]
    a = jnp.where(causal, qk * decay, 0.0)
    return dict(rq=rq, rk=rk, q=q, k=k, decay=decay, eg=eg, ekl=ekl, dec=dec, kb=kb, kbq=kbq, kk=kk, qk=qk, a=a,
                vb=v * beta, kbg=kb * eg, qd=q * eg, ke=k * ekl, causal=causal, strict=strict)


def _gdn_fwd(qkv, ba, z, alog, dtb, wn, gather=()):
    s = qkv.shape[0]
    nc = s // CHUNK
    ng = len(gather)

    def body(qkv_ref, ba_ref, z_ref, alog_ref, dtb_ref, wn_ref, *rest):
        x_refs, rest = rest[:ng], rest[ng:]
        y_ref, o_ref, st_ref, t_ref, w_ref, vn_ref = rest[:6]
        out_refs, rest = rest[6:6 + ng], rest[6 + ng:]
        state = rest[0]
        exchange = _ShardGather(x_refs, out_refs, *rest[1:]) if ng else None

        @pl.when(pl.program_id(0) == 0)
        def _():
            state[...] = jnp.zeros_like(state)
            if ng:
                exchange.start()

        row, col = _tri_masks()
        sig, _, _, gcum = _gdn_gates(ba_ref[...], alog_ref[...], dtb_ref[...], row, col)
        gt = gcum.T
        heads = range(A_HEADS)
        lanes = [slice(h * A_HEAD, (h + 1) * A_HEAD) for h in heads]
        fs = [_gdn_head_common(qkv_ref[:, lanes[h]], qkv_ref[:, A_QK + h * A_HEAD:A_QK + (h + 1) * A_HEAD],
                               qkv_ref[:, 2 * A_QK + h * A_HEAD:2 * A_QK + (h + 1) * A_HEAD],
                               sig[:, h:h + 1], gcum[:, 8 + h:9 + h], gt[8 + h:9 + h, :], row, col) for h in heads]
        ts = [t.astype(BF16) for t in
              _tri_inv([jnp.where(f["strict"], f["kk"] * f["decay"], 0.0) for f in fs], row, col)]
        uws = [_bdot(t, jnp.concatenate([f["vb"], f["kbg"]], axis=1), NN) for t, f in zip(ts, fs)]
        s0s = [state[h] for h in heads]
        ws_ = [uw[:, A_HEAD:].astype(BF16) for uw in uws]
        wss = [_bdot(jnp.concatenate([w, f["qd"].astype(BF16)], axis=0), s0, NN) for w, f, s0 in zip(ws_, fs, s0s)]
        vnews = [(uw[:, :A_HEAD] - wsq[:CHUNK]).astype(BF16) for uw, wsq in zip(uws, wss)]
        os_ = [wsq[CHUNK:] + _bdot(f["a"], vn, NN) for wsq, f, vn in zip(wss, fs, vnews)]
        s1s = [s0 * f["dec"] + _bdot(f["ke"], vn, TN) for s0, f, vn in zip(s0s, fs, vnews)]
        for h in heads:
            ln = lanes[h]
            st_ref[0, h] = s0s[h]
            t_ref[0, h] = ts[h]
            w_ref[:, ln] = ws_[h]
            vn_ref[:, ln] = vnews[h]
            state[h] = s1s[h]
            o = os_[h]
            o_ref[:, ln] = o
            r = lax.rsqrt(jnp.mean(o * o, axis=-1, keepdims=True) + EPS)
            zz = z_ref[:, ln]
            y_ref[:, ln] = (o * r * wn_ref[...] * zz * _sigmoid(zz)).astype(BF16)

        if ng:
            @pl.when(pl.program_id(0) == nc - 1)
            def _():
                exchange.finish()

    vec = pl.BlockSpec((1, LANE), lambda n: (0, 0))
    wide = pl.BlockSpec((CHUNK, A_QK), lambda n: (n, 0))
    return pl.pallas_call(
        body, name="gdn_fwd_gather" if ng else "gdn_fwd", grid=(nc,),
        in_specs=[pl.BlockSpec((CHUNK, A_CONV_WIDTH), lambda n: (n, 0)),
                  pl.BlockSpec((CHUNK, LANE), lambda n: (n, 0)), wide, vec, vec, vec] + [HBM_SPEC] * ng,
        out_specs=[wide, wide, pl.BlockSpec((1, A_HEADS, A_HEAD, A_HEAD), lambda n: (n, 0, 0, 0)),
                   pl.BlockSpec((1, A_HEADS, CHUNK, CHUNK), lambda n: (n, 0, 0, 0)), wide, wide] + [HBM_SPEC] * ng,
        out_shape=[jax.ShapeDtypeStruct((s, A_QK), BF16), jax.ShapeDtypeStruct((s, A_QK), F32),
                   jax.ShapeDtypeStruct((nc, A_HEADS, A_HEAD, A_HEAD), F32),
                   jax.ShapeDtypeStruct((nc, A_HEADS, CHUNK, CHUNK), BF16),
                   jax.ShapeDtypeStruct((s, A_QK), BF16), jax.ShapeDtypeStruct((s, A_QK), BF16)]
        + [jax.ShapeDtypeStruct((N_SHARDS,) + g.shape, g.dtype) for g in gather],
        scratch_shapes=[pltpu.VMEM((A_HEADS, A_HEAD, A_HEAD), F32)] + (_gather_scratch(ng) if ng else []),
        compiler_params=_params(("arbitrary",)),
    )(qkv, ba, z, alog, dtb, wn, *gather)


def _gdn_bwd(qkv, ba, z, o_raw, dy, states, t_all, w_all, vn_all, alog, dtb, wn, exchange=()):
    s = qkv.shape[0]
    nc = s // CHUNK
    nx = len(exchange)

    def body(qkv_ref, ba_ref, z_ref, o_ref, dy_ref, st_ref, t_ref, w_ref, vn_ref, alog_ref, dtb_ref, wn_ref, *rest):
        p_refs, rest = rest[:nx], rest[nx:]
        dqkv_ref, dba_ref, dz_ref, dalog_ref, ddtb_ref, dwn_ref = rest[:6]
        arrive_refs, rest = rest[6:6 + nx], rest[6 + nx:]
        dstate = rest[0]
        swap = _ChipExchange(p_refs, arrive_refs, *rest[1:]) if nx else None

        @pl.when(pl.program_id(0) == 0)
        def _():
            for r in (dstate, dalog_ref, ddtb_ref, dwn_ref):
                r[...] = jnp.zeros_like(r)
            if nx:
                swap.start()

        row, col = _tri_masks()
        bat = ba_ref[...]
        sig, neg_a, g, gcum = _gdn_gates(bat, alog_ref[...], dtb_ref[...], row, col)
        gt = gcum.T
        lane = lax.broadcasted_iota(jnp.int32, (CHUNK, LANE), 1)
        ones = jnp.ones((CHUNK, LANE), F32)
        last_row = lax.broadcasted_iota(jnp.int32, (CHUNK, 1), 0) == CHUNK - 1
        wnv = wn_ref[...]
        dgc_tile = jnp.zeros((CHUNK, LANE), F32)
        dbeta_tile = jnp.zeros((CHUNK, LANE), F32)
        dwn_acc = jnp.zeros((1, LANE), F32)
        hs = []
        for h in range(A_HEADS):
            ln = slice(h * A_HEAD, (h + 1) * A_HEAD)
            lk = slice(A_QK + h * A_HEAD, A_QK + (h + 1) * A_HEAD)
            lv = slice(2 * A_QK + h * A_HEAD, 2 * A_QK + (h + 1) * A_HEAD)
            q_raw, k_raw, v = qkv_ref[:, ln], qkv_ref[:, lk], qkv_ref[:, lv]
            f = _gdn_head_common(q_raw, k_raw, v, sig[:, h:h + 1], gcum[:, 8 + h:9 + h], gt[8 + h:9 + h, :], row, col)
            f.update(h=h, ln=ln, lk=lk, lv=lv, q_raw=q_raw, k_raw=k_raw, v=v, beta=sig[:, h:h + 1],
                     s0=st_ref[0, h], ds1=dstate[h], t=t_ref[0, h], w=w_ref[:, ln], vnew=vn_ref[:, ln])
            o = o_ref[:, ln]
            zz = z_ref[:, ln]
            dyv = dy_ref[:, ln]
            r = lax.rsqrt(jnp.mean(o * o, axis=-1, keepdims=True) + EPS)
            sz = _sigmoid(zz)
            silu = zz * sz
            dz_ref[:, ln] = (dyv * o * r * wnv * sz * (1.0 + zz * (1.0 - sz))).astype(BF16)
            dwn_acc = dwn_acc + jnp.sum(dyv * silu * o * r, axis=0, keepdims=True)
            tt = dyv * silu * wnv
            do = r * tt - o * (r * r * r) * jnp.mean(tt * o, axis=-1, keepdims=True)
            f["do_b"] = do.astype(BF16)
            hs.append(f)
        for f in hs:
            f["dvnew"] = _bdot(f["a"], f["do_b"], TN) + _bdot(f["ke"], f["ds1"], NN)
            f["da"] = jnp.where(f["causal"], _bdot(f["do_b"], f["vnew"], NT), 0.0)
            f["dke"] = _bdot(f["vnew"], f["ds1"], NT)
            f["ddec"] = jnp.sum(jnp.sum(f["s0"] * f["ds1"], axis=1, keepdims=True), axis=0, keepdims=True)
        for f in hs:
            do_dv = jnp.concatenate([f["do_b"], f["dvnew"].astype(BF16)], axis=0)
            both = _bdot(do_dv, f["s0"], NT)
            f["dqd"], f["dw"] = both[:CHUNK], -both[CHUNK:]
            qd_w = jnp.concatenate([f["qd"].astype(BF16), -f["w"]], axis=0)
            dstate[f["h"]] = _bdot(qd_w, do_dv, TN) + f["dec"] * f["ds1"]
        for f in hs:
            dd = jnp.concatenate([f["dvnew"], f["dw"]], axis=1).astype(BF16)
            tdd = _bdot(f["t"], dd, TN)
            f["dvb"], f["dkbg"] = tdd[:, :A_HEAD], tdd[:, A_HEAD:]
            f["dt"] = _bdot(dd, jnp.concatenate([f["vb"], f["kbg"]], axis=1), NT)
        for f in hs:
            f["tdt"] = _bdot(f["t"], f["dt"], TN)
        for f in hs:
            dm = jnp.where(f["strict"], -_bdot(f["tdt"], f["t"], NT), 0.0)
            f["ddecay"] = (dm * f["kk"] + f["da"] * f["qk"]) * f["decay"]
            f["dboth"] = jnp.concatenate([dm * f["decay"], f["da"] * f["decay"]], axis=0).astype(BF16)
        for f in hs:
            f["r2"] = _bdot(f["dboth"], f["k"], NN)
            f["dk0"] = _bdot(f["dboth"], f["kbq"], TN)
        for f in hs:
            h, k, beta = f["h"], f["k"], f["beta"]
            dkb = f["r2"][:CHUNK] + f["dkbg"] * f["eg"]
            dq = f["r2"][CHUNK:] + f["dqd"] * f["eg"]
            dk = f["dk0"] + f["dke"] * f["ekl"] + dkb * beta
            dke_ke = jnp.sum(f["dke"] * f["ke"], axis=-1, keepdims=True)
            dgc = (jnp.sum(f["ddecay"], axis=-1, keepdims=True)
                   + jnp.sum(f["dqd"] * f["qd"], axis=-1, keepdims=True) - dke_ke
                   + jnp.sum(f["dkbg"] * f["kbg"], axis=-1, keepdims=True))
            dgl = jnp.sum(dke_ke, axis=0, keepdims=True) + f["ddec"] * f["dec"]
            dgc = dgc + jnp.where(last_row, dgl, 0.0)
            dbeta = jnp.sum(dkb * k, axis=-1, keepdims=True) + jnp.sum(f["dvb"] * f["v"], axis=-1, keepdims=True)
            dgc_tile = dgc_tile + jnp.where(lane == 8 + h, dgc, 0.0)
            dbeta_tile = dbeta_tile + jnp.where(lane == h, dbeta, 0.0)
            dqn = dq * (A_HEAD ** -0.5)
            rq, rk, q_raw, k_raw = f["rq"], f["rk"], f["q_raw"], f["k_raw"]
            dqkv_ref[:, f["ln"]] = rq * dqn - q_raw * (rq * rq * rq) * jnp.sum(dqn * q_raw, axis=-1, keepdims=True)
            dqkv_ref[:, f["lk"]] = rk * dk - k_raw * (rk * rk * rk) * jnp.sum(dk * k_raw, axis=-1, keepdims=True)
            dqkv_ref[:, f["lv"]] = f["dvb"] * beta
        ddecays = [f["ddecay"] for f in hs]
        col_sums = _dot(jnp.concatenate(ddecays, axis=1), ones, TN, HIGHEST)
        for h in range(A_HEADS):
            dgc_tile = dgc_tile - jnp.where(lane == 8 + h, col_sums[h * CHUNK:(h + 1) * CHUNK, :1], 0.0)
        upper = (row <= col).astype(F32)
        dg = _dot(upper, dgc_tile, NN, HIGHEST)
        da_raw = dg * neg_a * _sigmoid(bat + dtb_ref[...])
        dba_ref[...] = jnp.where(lane < 8, dbeta_tile * sig * (1.0 - sig),
                                 jnp.where(lane < 16, da_raw, 0.0)).astype(BF16)
        dwn_ref[...] += dwn_acc
        ddtb_ref[...] += jnp.sum(da_raw, axis=0, keepdims=True)
        dalog_ref[...] += jnp.sum(dg * g, axis=0, keepdims=True)

        if nx:
            @pl.when(pl.program_id(0) == nc - 1)
            def _():
                swap.finish()

    rev = lambda n: (nc - 1 - n, 0)
    vec = pl.BlockSpec((1, LANE), lambda n: (0, 0))
    wide = pl.BlockSpec((CHUNK, A_QK), rev)
    qkv_blk = pl.BlockSpec((CHUNK, A_CONV_WIDTH), rev)
    ba_blk = pl.BlockSpec((CHUNK, LANE), rev)
    vsh = jax.ShapeDtypeStruct((1, LANE), F32)
    return pl.pallas_call(
        body, name="gdn_bwd_exchange" if nx else "gdn_bwd", grid=(nc,),
        in_specs=[qkv_blk, ba_blk, wide, wide, wide,
                  pl.BlockSpec((1, A_HEADS, A_HEAD, A_HEAD), lambda n: (nc - 1 - n, 0, 0, 0)),
                  pl.BlockSpec((1, A_HEADS, CHUNK, CHUNK), lambda n: (nc - 1 - n, 0, 0, 0)), wide, wide,
                  vec, vec, vec] + [HBM_SPEC] * nx,
        out_specs=[qkv_blk, ba_blk, wide, vec, vec, vec] + [HBM_SPEC] * nx,
        out_shape=[jax.ShapeDtypeStruct((s, A_CONV_WIDTH), F32), jax.ShapeDtypeStruct((s, LANE), BF16),
                   jax.ShapeDtypeStruct((s, A_QK), BF16), vsh, vsh, vsh]
        + [jax.ShapeDtypeStruct((3,) + p.shape[1:], p.dtype) for p in exchange],
        scratch_shapes=[pltpu.VMEM((A_HEADS, A_HEAD, A_HEAD), F32)] + (_chip_scratch(nx) if nx else []),
        compiler_params=_params(("arbitrary",)),
    )(qkv, ba, z, o_raw, dy, states, t_all, w_all, vn_all, alog, dtb, wn, *exchange)


REL_RING = 1024
QBLK_BITS = 8


def _rel_ring_onehot():
    m = lax.broadcasted_iota(jnp.int32, (REL_RING, REL_PAD), 0)
    t = lax.broadcasted_iota(jnp.int32, (REL_RING, REL_PAD), 1)
    u = jnp.where(m < KBLK, m, m - REL_RING)
    idx = jnp.clip(LEFT - u, -REL_CLIP, REL_CLIP) + REL_CLIP
    return (t == idx).astype(F32)


def _relbias_ring(table, transpose):
    n_in, n_out = (REL_RING, REL_PAD) if transpose else (REL_PAD, REL_RING)

    def body(t_ref, o_ref):
        o_ref[...] = _dot(t_ref[...], _rel_ring_onehot(), NN if transpose else NT, HIGHEST)

    return pl.pallas_call(
        body, name="relbias_ring_bwd" if transpose else "relbias_ring",
        out_shape=jax.ShapeDtypeStruct((B_HEADS, n_out), F32),
        in_specs=[VMEM_FULL], out_specs=VMEM_FULL,
        compiler_params=_params(),
    )(table)


def _row_bit(shape, bit):
    return ((lax.broadcasted_iota(jnp.int32, shape, 0) >> bit) & 1) == 1


def _relbias_expand(ring):
    def body(r_ref, o_ref):
        b = jnp.broadcast_to(r_ref[0], (QBLK, REL_RING))
        for bit in range(QBLK_BITS):
            b = jnp.where(_row_bit(b.shape, bit), pltpu.roll(b, 1 << bit, 1), b)
        j = lax.broadcasted_iota(jnp.int32, (QBLK, KBLK), 1)
        r = lax.broadcasted_iota(jnp.int32, (QBLK, KBLK), 0)
        lo = (r >> 6) << 6
        o_ref[0] = jnp.where((j >= lo) & (j < lo + LEFT + CHUNK), b[:, :KBLK], NEG_INF)

    return pl.pallas_call(
        body, name="relbias_expand", grid=(B_HEADS,),
        in_specs=[pl.BlockSpec((1, 1, REL_RING), lambda h: (h, 0, 0))],
        out_specs=pl.BlockSpec((1, QBLK, KBLK), lambda h: (h, 0, 0)),
        out_shape=jax.ShapeDtypeStruct((B_HEADS, QBLK, KBLK), F32),
        compiler_params=_params(("parallel",)),
    )(ring)


def _relbias_reduce(ds):
    def body(d_ref, o_ref):
        d = jnp.concatenate([d_ref[0], jnp.zeros((QBLK, REL_RING - KBLK), F32)], axis=1)
        for bit in range(QBLK_BITS):
            d = jnp.where(_row_bit(d.shape, bit), pltpu.roll(d, REL_RING - (1 << bit), 1), d)
        o_ref[0] = jnp.sum(d, axis=0, keepdims=True)

    return pl.pallas_call(
        body, name="relbias_reduce", grid=(B_HEADS,),
        in_specs=[pl.BlockSpec((1, QBLK, KBLK), lambda h: (h, 0, 0))],
        out_specs=pl.BlockSpec((1, 1, REL_RING), lambda h: (h, 0, 0)),
        out_shape=jax.ShapeDtypeStruct((B_HEADS, 1, REL_RING), F32),
        compiler_params=_params(("parallel",)),
    )(ds)


def _attn_probs(q_ref, kb, b_ref, hh, q0):
    hl = slice(hh * B_HEAD, (hh + 1) * B_HEAD)
    qh = q_ref[:, hl] * (B_HEAD ** -0.5)
    kh = kb[:, hl]
    jpos = lax.broadcasted_iota(jnp.int32, (QBLK, KBLK), 1)
    sc = _bdot(qh, kh, NT) + b_ref[hh]
    sc = jnp.where(jpos + q0 >= LEFT, sc, NEG_INF)
    mx = jnp.max(sc, axis=-1, keepdims=True)
    p = jnp.exp(sc - mx)
    return p * (1.0 / jnp.sum(p, axis=-1, keepdims=True)), qh, kh


def _attn_fwd(q, kpad, vpad, bias):
    s = q.shape[0]

    def body(q_ref, k_ref, v_ref, b_ref, o_ref):
        q0 = pl.multiple_of(pl.program_id(1) * QBLK, QBLK)
        kb = k_ref[pl.ds(q0, KBLK), :]
        vb = v_ref[pl.ds(q0, KBLK), :]
        outs = []
        for hh in range(2):
            p, _, _ = _attn_probs(q_ref, kb, b_ref, hh, q0)
            outs.append(_bdot(p, vb[:, hh * B_HEAD:(hh + 1) * B_HEAD], NN))
        o_ref[...] = jnp.concatenate(outs, axis=1).astype(BF16)

    qblk = pl.BlockSpec((QBLK, LANE), lambda g, m: (m, g))
    kblk = pl.BlockSpec((LEFT + s, LANE), lambda g, m: (0, g))
    return pl.pallas_call(
        body, name="attn_fwd", grid=(B_HEADS // 2, s // QBLK),
        in_specs=[qblk, kblk, kblk, pl.BlockSpec((2, QBLK, KBLK), lambda g, m: (g, 0, 0))],
        out_specs=qblk,
        out_shape=jax.ShapeDtypeStruct((s, D_MODEL), BF16),
        compiler_params=_params(("parallel", "arbitrary")),
    )(q, kpad, vpad, bias)


def _attn_bwd(q, kpad, vpad, bias, do):
    s = q.shape[0]

    def body(q_ref, k_ref, v_ref, b_ref, do_ref, dq_ref, dk_ref, dv_ref, db_ref):
        @pl.when(pl.program_id(1) == 0)
        def _():
            for r in (dk_ref, dv_ref, db_ref):
                r[...] = jnp.zeros_like(r)

        q0 = pl.multiple_of(pl.program_id(1) * QBLK, QBLK)
        kb = k_ref[pl.ds(q0, KBLK), :]
        vb = v_ref[pl.ds(q0, KBLK), :]
        dqs, dks, dvs = [], [], []
        for hh in range(2):
            hl = slice(hh * B_HEAD, (hh + 1) * B_HEAD)
            p, qh, kh = _attn_probs(q_ref, kb, b_ref, hh, q0)
            doh = do_ref[:, hl]
            dp = _bdot(doh, vb[:, hl], NT)
            dsc = p * (dp - jnp.sum(p * dp, axis=-1, keepdims=True))
            db_ref[hh] += dsc
            dqs.append(_bdot(dsc, kh, NN) * (B_HEAD ** -0.5))
            dks.append(_bdot(dsc, qh, TN))
            dvs.append(_bdot(p, doh, TN))
        dq_ref[...] = jnp.concatenate(dqs, axis=1).astype(BF16)
        dk_ref[pl.ds(q0, KBLK), :] += jnp.concatenate(dks, axis=1)
        dv_ref[pl.ds(q0, KBLK), :] += jnp.concatenate(dvs, axis=1)

    qblk = pl.BlockSpec((QBLK, LANE), lambda g, m: (m, g))
    kblk = pl.BlockSpec((LEFT + s, LANE), lambda g, m: (0, g))
    bblk = pl.BlockSpec((2, QBLK, KBLK), lambda g, m: (g, 0, 0))
    return pl.pallas_call(
        body, name="attn_bwd", grid=(B_HEADS // 2, s // QBLK),
        in_specs=[qblk, kblk, kblk, bblk, qblk],
        out_specs=[qblk, kblk, kblk, bblk],
        out_shape=[jax.ShapeDtypeStruct((s, D_MODEL), BF16), jax.ShapeDtypeStruct((LEFT + s, D_MODEL), F32),
                   jax.ShapeDtypeStruct((LEFT + s, D_MODEL), F32),
                   jax.ShapeDtypeStruct((B_HEADS, QBLK, KBLK), F32)],
        compiler_params=_params(("parallel", "arbitrary")),
    )(q, kpad, vpad, bias, do)


def _adamw(w, g, m, v):
    r, c = w.shape
    tr = r
    for cand in (512, 256, 128, 64, 32, 16, 8):
        if r % cand == 0 and cand * c * 4 <= 2 * 1024 * 1024:
            tr = cand
            break
    c1 = 1.0 / (1.0 - ADAM_B1 ** ADAM_STEP)
    c2 = 1.0 / (1.0 - ADAM_B2 ** ADAM_STEP)

    def body(w_ref, g_ref, m_ref, v_ref, d_ref, mo_ref, vo_ref):
        gv = g_ref[...]
        mn = ADAM_B1 * m_ref[...] + (1.0 - ADAM_B1) * gv
        vn = ADAM_B2 * v_ref[...] + (1.0 - ADAM_B2) * (gv * gv)
        mo_ref[...] = mn
        vo_ref[...] = vn
        d_ref[...] = -ADAM_LR * ((mn * c1) / (jnp.sqrt(vn * c2) + ADAM_EPS) + ADAM_WD * w_ref[...])

    blk = pl.BlockSpec((tr, c), lambda i: (i, 0))
    sh = jax.ShapeDtypeStruct((r, c), F32)
    return pl.pallas_call(
        body, name="adamw", grid=(r // tr,),
        in_specs=[blk] * 4, out_specs=[blk] * 3, out_shape=[sh] * 3,
        compiler_params=_params(("parallel",)),
    )(w, g, m, v)


def _row(v, width=None):
    v = v.reshape(1, -1)
    if width is not None and v.shape[1] < width:
        v = jnp.pad(v, ((0, 0), (0, width - v.shape[1])))
    return v


def _gate_row(v):
    return jnp.pad(v.reshape(1, A_HEADS), ((0, 0), (A_HEADS, LANE - 2 * A_HEADS)))


DEPTH = 4
N_A = 2
N_B = 2
F_DOWN_ROWS = FFN_DIM // N_SHARDS
SQ_ROWS = D_MODEL // N_SHARDS
GD_B_Q0 = DEPTH * F_DOWN_ROWS // SQ_ROWS
GD_B_OUT0 = GD_B_Q0 + N_B
UP_COLS = 2 * FFN_DIM // N_SHARDS


def _a_layer_fwd(h, w, i, pending=None):
    xn = _rmsnorm_fwd(h, _row(w["a_norm"][i]))
    w_qkv, w_z, w_ba = w["a_in"][i]
    pre = _mm(xn, w_qkv, name="a_qkv")
    z = _mm(xn, w_z, name="a_z")
    ba = _mm(xn, w_ba, name="a_ba")
    act = _conv_silu_fwd(pre, w["a_conv"][i])
    alog, dtb, wn = _gate_row(w["a_A_log"][i]), _gate_row(w["a_dt_bias"][i]), _row(w["a_out_norm"][i])
    if pending is None:
        y, o_raw, states, t_all, w_all, vn_all = _gdn_fwd(act, ba, z, alog, dtb, wn)
    else:
        names = list(pending)
        y, o_raw, states, t_all, w_all, vn_all, *landed = _gdn_fwd(act, ba, z, alog, dtb, wn,
                                                                   gather=[pending[n] for n in names])
        w.update(zip(names, landed))
    h2 = _mm_rowsh(y, w["GA"], SQ_ROWS, i, "nn", "a_out", res=h)
    saved = dict(h=h, xn=xn, pre=pre, z=z, ba=ba, act=act, o_raw=o_raw, y=y, states=states,
                 t_all=t_all, w_all=w_all, vn_all=vn_all, alog=alog, dtb=dtb, wn=wn)
    return h2, saved


def _a_layer_bwd(dh2, w, i, sv, acc, outgoing=None):
    g = {}
    acc["GA"] = _mm_rowsh_dw(sv["y"], dh2, SQ_ROWS, "a_out_dw", acc["GA"], i)
    exchange = outgoing() if outgoing is not None else ()
    dy = _mm_rowsh(dh2, w["GA"], SQ_ROWS, i, "nt", "a_out_dx")
    dact, dba, dz, dalog, ddtb, dwn, *arrived = _gdn_bwd(sv["act"], sv["ba"], sv["z"], sv["o_raw"], dy, sv["states"],
                                                         sv["t_all"], sv["w_all"], sv["vn_all"],
                                                         sv["alog"], sv["dtb"], sv["wn"], exchange=exchange)
    dpre, g["conv"] = _conv_silu_bwd(sv["pre"], w["a_conv"][i], dact)
    xn = sv["xn"]
    w_qkv, w_z, w_ba = w["a_in"][i]
    d_in = jnp.concatenate([_mm(xn, dpre, "tn", name="a_qkv_dw"), _mm(xn, dz, "tn", name="a_z_dw"),
                            _mm(xn, dba, "tn", name="a_ba_dw")[:, :2 * A_HEADS]], axis=1)
    g["w_in"] = jnp.transpose(d_in.reshape(D_MODEL, N_SHARDS, -1), (1, 0, 2))
    dxn = _mm(dpre, w_qkv, "nt", name="a_qkv_dx")
    dxn = _mm(dz, w_z, "nt", res=dxn, name="a_z_dx")
    dh, dnorm = _mm(dba, w_ba, "nt", res=dxn, name="a_ba_dx", norm=(sv["h"], _row(w["a_norm"][i]), dh2))
    g["norm"] = dnorm[0]
    g["A_log"] = dalog[0, A_HEADS:2 * A_HEADS]
    g["dt_bias"] = ddtb[0, A_HEADS:2 * A_HEADS]
    g["out_norm"] = dwn[0]
    return dh, g, arrived


def _by_half(a, lead):
    return jnp.moveaxis(a.reshape(a.shape[:-1] + (2, 2, UP_COLS)), (-3, -2), (0, 1)).reshape((2, 2) + lead + (UP_COLS,))


def _from_half(a):
    lead = a.shape[2:-1]
    return jnp.moveaxis(a, (0, 1), (-3, -2)).reshape(lead + (N_SHARDS * UP_COLS,))


def _ffn_fwd(h, w, l):
    s = h.shape[0]
    xn = _rmsnorm_fwd(h, _row(w["f_norm"][l]))
    cw = _by_half(w["f_conv"][l], (w["f_conv"].shape[1],))
    cb = _by_half(w["f_conv_b"][l][None], (1,))
    pre = _mm_colsh(xn, w["GU"], D_MODEL, l, "nn", "f_up").reshape(2, 2, s, UP_COLS)
    act = _ffn_act_fwd(pre, cw, cb)
    h2 = _mm_rowsh(act, w["GD"], F_DOWN_ROWS, l, "nn", "f_down", res=h)
    return h2, dict(h=h, xn=xn, pre=pre, act=act, cw=cw, cb=cb)


def _ffn_bwd(dh2, w, l, sv, acc):
    g = {}
    s = dh2.shape[0]
    late = l == 0
    down, up = ("GD0", "GU0") if late else ("GD1", "GU1")
    blk = 0 if late else l - 1
    acc[down] = _mm_rowsh_dw(sv["act"], dh2, F_DOWN_ROWS, "f_down_dw", acc[down], blk)
    dact = _mm_rowsh(dh2, w["GD"], F_DOWN_ROWS, l, "nt", "f_down_dx")
    dpre, dcw, dcb = _ffn_act_bwd(sv["pre"], sv["cw"], sv["cb"], dact)
    dpre = dpre.reshape(N_SHARDS, s, UP_COLS)
    acc[up] = _mm_colsh_dw(sv["xn"], dpre, "f_up_dw", into=acc[up], blk0=blk)
    g["conv"] = _from_half(dcw)
    g["conv_b"] = _from_half(dcb)[0]
    dh, dnorm = _mm_colsh(dpre, w["GU"], D_MODEL, l, "nt", "f_up_dx", norm=(sv["h"], _row(w["f_norm"][l]), dh2))
    g["norm"] = dnorm[0]
    return dh, g


def _b_layer_fwd(h, w, j, kpad, vpad):
    xn = _rmsnorm_fwd(h, _row(w["b_norm"][j]))
    q = _mm_rowsh(xn, w["GD"], SQ_ROWS, GD_B_Q0 + j, "nn", "b_q", out_dtype=BF16)
    rel = w["b_rel_bias"][j]
    table = jnp.pad(rel, ((0, 0), (0, REL_PAD - rel.shape[1])))
    bias = _relbias_expand(_relbias_ring(table, False).reshape(B_HEADS, 1, REL_RING))
    o = _attn_fwd(q, kpad, vpad, bias)
    h2 = _mm_rowsh(o, w["GD"], SQ_ROWS, GD_B_OUT0 + j, "nn", "b_out", res=h)
    return h2, dict(h=h, xn=xn, q=q, o=o, bias=bias)


def _b_layer_bwd(dh2, w, j, sv, kpad, vpad, acc):
    g = {}
    acc["GB"] = _mm_rowsh_dw(sv["o"], dh2, SQ_ROWS, "b_out_dw", acc["GB"], N_B + j)
    do = _mm_rowsh(dh2, w["GD"], SQ_ROWS, GD_B_OUT0 + j, "nt", "b_out_dx", out_dtype=BF16)
    dq, dkp, dvp, dsc = _attn_bwd(sv["q"], kpad, vpad, sv["bias"], do)
    dring = _relbias_reduce(dsc).reshape(B_HEADS, REL_RING)
    g["rel_bias"] = _relbias_ring(dring, True)[:, :2 * REL_CLIP + 1]
    acc["GB"] = _mm_rowsh_dw(sv["xn"], dq, SQ_ROWS, "b_q_dw", acc["GB"], j)
    dh, dnorm = _mm_rowsh(dq, w["GD"], SQ_ROWS, GD_B_Q0 + j, "nt", "b_q_dx",
                          norm=(sv["h"], _row(w["b_norm"][j]), dh2))
    g["norm"] = dnorm[0]
    return dh, g, dkp, dvp


EARLY = ("GU1", "GD1", "GB", "GK")
MIDDLE = ("GU0", "GD0", "GA", "GI1")
FINAL = ("GI0",)


def _local_step(x, tgt, w, pending=None, rs_hooks=None):
    w = dict(w)
    h = x
    saved = []
    kv_saved = None
    kpad = vpad = None
    for layer in range(DEPTH):
        if layer < N_A:
            h, sm = _a_layer_fwd(h, w, layer, pending if layer == 0 else None)
        else:
            if layer == N_A:
                xn_kv = _rmsnorm_fwd(h, _row(w["kv_norm"]))
                kv = _mm_colsh(xn_kv, w["GK"], D_MODEL, 0, "nn", "kv", flat=True, out_dtype=BF16)
                kpad = jnp.pad(kv[:, :D_MODEL], ((LEFT, 0), (0, 0)))
                vpad = jnp.pad(kv[:, D_MODEL:], ((LEFT, 0), (0, 0)))
                kv_saved = dict(h=h, xn=xn_kv)
            h, sm = _b_layer_fwd(h, w, layer - N_A, kpad, vpad)
        h, sf = _ffn_fwd(h, w, layer)
        saved.append((sm, sf))

    loss, dh, dfinal = _final_loss(h, _row(w["final_norm"]), tgt)

    ga = [None] * N_A
    gb = [None] * N_B
    gf = [None] * DEPTH
    dk_tot = dv_tot = None
    g_kv = g_kvn = None
    acc = dict(GU1=lax.empty((N_SHARDS, (DEPTH - 1) * D_MODEL, UP_COLS), F32),
               GD1=lax.empty((N_SHARDS, (DEPTH - 1) * F_DOWN_ROWS, D_MODEL), F32),
               GB=lax.empty((N_SHARDS, 2 * N_B * SQ_ROWS, D_MODEL), F32),
               GU0=lax.empty((N_SHARDS, D_MODEL, UP_COLS), F32),
               GD0=lax.empty((N_SHARDS, F_DOWN_ROWS, D_MODEL), F32),
               GA=lax.empty((N_SHARDS, N_A * SQ_ROWS, D_MODEL), F32))
    reduced = {}
    for layer in reversed(range(DEPTH)):
        sm, sf = saved[layer]
        dh, gf[layer] = _ffn_bwd(dh, w, layer, sf, acc)
        if layer >= N_A:
            dh, gb[layer - N_A], dkp, dvp = _b_layer_bwd(dh, w, layer - N_A, sm, kpad, vpad, acc)
            dk_tot = dkp if dk_tot is None else dk_tot + dkp
            dv_tot = dvp if dv_tot is None else dv_tot + dvp
            if layer == N_A:
                dkv = jnp.concatenate([dk_tot[LEFT:], dv_tot[LEFT:]], axis=1).astype(BF16)
                g_kv = _mm_colsh_dw(kv_saved["xn"], dkv, "kv_dw", flat=True)
                dh, g_kvn = _mm_colsh(dkv, w["GK"], D_MODEL, 0, "nt", "kv_dx", flat=True,
                                      norm=(kv_saved["h"], _row(w["kv_norm"]), dh))
        elif rs_hooks is not None:
            prepare, finish = rs_hooks
            names = EARLY if layer == N_A - 1 else MIDDLE
            held = {}

            def outgoing(names=names, held=held):
                extra = dict(GK=g_kv, GI1=ga[N_A - 1]["w_in"] if ga[N_A - 1] else None)
                held["pairs"], partials = prepare([acc[n] if n in acc else extra[n] for n in names])
                return partials

            dh, ga[layer], arrived = _a_layer_bwd(dh, w, layer, sm, acc, outgoing)
            reduced.update(zip(names, finish(held["pairs"], arrived)))
        else:
            dh, ga[layer], _ = _a_layer_bwd(dh, w, layer, sm, acc)

    def stack(gs, key):
        return jnp.stack([g[key] for g in gs])

    grads = dict(
        acc, GK=g_kv, reduced=reduced, GI0=ga[0]["w_in"], GI1=ga[1]["w_in"],
        a_norm=stack(ga, "norm"), a_conv=stack(ga, "conv"), a_A_log=stack(ga, "A_log"),
        a_dt_bias=stack(ga, "dt_bias"), a_out_norm=stack(ga, "out_norm"), kv_norm=g_kvn[0],
        b_norm=stack(gb, "norm"), b_rel_bias=stack(gb, "rel_bias"),
        f_norm=stack(gf, "norm"), f_conv=stack(gf, "conv"), f_conv_b=stack(gf, "conv_b"), final_norm=dfinal[0])
    return loss, dh, grads


HBM_SPEC = pl.BlockSpec(memory_space=pl.ANY)
VMEM_SPEC = pl.BlockSpec(memory_space=pltpu.VMEM)


def _place():
    x, y, c = lax.axis_index("x"), lax.axis_index("y"), lax.axis_index("c")
    chips = [(1 - x, y), (x, 1 - y), (1 - x, 1 - y)]
    return x, y, c, chips


def _remote(src, dst, send_sem, recv_sem, to):
    return pltpu.make_async_remote_copy(src_ref=src, dst_ref=dst, send_sem=send_sem, recv_sem=recv_sem,
                                        device_id=to, device_id_type=MESH)


GATHER_COPIES = 7


class _ShardGather:
    def __init__(self, x_refs, out_refs, send_sems, recv_sems):
        n = len(x_refs)
        x, y, c, chips = _place()
        sibling = (x, y, 1 - c)

        def half(a, px, py, hc):
            rh = x_refs[a].shape[0] // 2
            return out_refs[a].at[2 * px + py, pl.ds(hc * rh, rh), :]

        def mine(a):
            rh = x_refs[a].shape[0] // 2
            return x_refs[a].at[pl.ds(c * rh, rh), :]

        def sems(a, k):
            return send_sems.at[GATHER_COPIES * a + k], recv_sems.at[GATHER_COPIES * a + k]

        order = [(j, chip, a) for j, chip in enumerate(chips) for a in range(n)]
        self.first = [_remote(mine(a), half(a, x, y, c), *sems(a, j), (*chip, c)) for j, chip, a in order]
        self.own = [_remote(x_refs[a], out_refs[a].at[2 * x + y], *sems(a, 6), sibling) for a in range(n)]
        self.landed = [_remote(half(a, *chip, c), half(a, *chip, c), *sems(a, j), (*chip, c)) for j, chip, a in order]
        self.passed = [_remote(half(a, *chip, c), half(a, *chip, c), *sems(a, 3 + j), sibling)
                       for j, chip, a in order]
        self.theirs = [_remote(half(a, *chip, 1 - c), half(a, *chip, 1 - c), *sems(a, 3 + j), sibling)
                       for j, chip, a in order]

    def start(self):
        for cp in self.first + self.own:
            cp.start()

    def finish(self):
        for arrived, onward in zip(self.landed, self.passed):
            arrived.wait_recv()
            onward.start()
        for cp in self.theirs + self.own:
            cp.wait_recv()
        for cp in self.first + self.passed + self.own:
            cp.wait_send()


def _gather_scratch(n):
    return [pltpu.SemaphoreType.DMA((GATHER_COPIES * n,)), pltpu.SemaphoreType.DMA((GATHER_COPIES * n,))]


def _allgather_weights(shards):
    n = len(shards)

    def body(*refs):
        gather = _ShardGather(refs[:n], refs[n:2 * n], *refs[2 * n:])
        gather.start()
        gather.finish()

    return pl.pallas_call(
        body, name="allgather_weights",
        out_shape=[jax.ShapeDtypeStruct((N_SHARDS,) + sh.shape, sh.dtype) for sh in shards],
        in_specs=[HBM_SPEC] * n, out_specs=[HBM_SPEC] * n,
        scratch_shapes=_gather_scratch(n),
    )(*shards)


def _pair_exchange(gs):
    n = len(gs)

    def body(*refs):
        g_refs, out_refs, (send_sems, recv_sems) = refs[:n], refs[n:2 * n], refs[2 * n:]
        x, y, c, _ = _place()
        cps = []
        for a in range(n):
            rh = gs[a].shape[1] // 2
            cps.append(_remote(g_refs[a].at[:, pl.ds((1 - c) * rh, rh), :], out_refs[a], send_sems.at[a],
                               recv_sems.at[a], (x, y, 1 - c)))
        for cp in cps:
            cp.start()
        for cp in cps:
            cp.wait()

    return pl.pallas_call(
        body, name="rs_pair_exchange",
        out_shape=[jax.ShapeDtypeStruct((g.shape[0], g.shape[1] // 2, g.shape[2]), g.dtype) for g in gs],
        in_specs=[HBM_SPEC] * n, out_specs=[HBM_SPEC] * n,
        scratch_shapes=[pltpu.SemaphoreType.DMA((n,)), pltpu.SemaphoreType.DMA((n,))],
    )(*gs)


def _add_rows(rows, cols):
    best = 16
    for t in range(16, rows + 1, 16):
        if rows % t == 0 and t * cols * 4 <= 2304 * 1024:
            best = t
    return best


def _pair_add(g, other, c_idx):
    n, r, cols = g.shape
    rh = r // 2
    tr = _add_rows(rh, cols)
    nb = rh // tr

    def body(c_ref, a_ref, b_ref, o_ref, ob_ref):
        sm = a_ref[...] + b_ref[...]
        o_ref[...] = sm
        ob_ref[...] = sm.astype(BF16)

    out_blk = pl.BlockSpec((1, tr, cols), lambda s, i, c_ref: (s, i, 0))
    return pl.pallas_call(
        body, name="rs_pair_add",
        grid_spec=pltpu.PrefetchScalarGridSpec(
            num_scalar_prefetch=1, grid=(n, nb),
            in_specs=[pl.BlockSpec((1, tr, cols), lambda s, i, c_ref: (s, c_ref[0] * nb + i, 0)), out_blk],
            out_specs=[out_blk, out_blk]),
        out_shape=[jax.ShapeDtypeStruct((n, rh, cols), F32), jax.ShapeDtypeStruct((n, rh, cols), BF16)],
        compiler_params=_params(("parallel", "parallel")),
    )(c_idx, g, other)


class _ChipExchange:
    def __init__(self, p_refs, out_refs, send_sems, recv_sems):
        x, y, c, chips = _place()
        self.copies = [_remote(p_refs[a].at[2 * chip[0] + chip[1]], out_refs[a].at[j], send_sems.at[3 * a + j],
                               recv_sems.at[3 * a + j], (*chip, c))
                       for a in range(len(p_refs)) for j, chip in enumerate(chips)]

    def start(self):
        for cp in self.copies:
            cp.start()

    def finish(self):
        for cp in self.copies:
            cp.wait()


def _chip_scratch(n):
    return [pltpu.SemaphoreType.DMA((3 * n,)), pltpu.SemaphoreType.DMA((3 * n,))]


def _chip_exchange(ps):
    n = len(ps)

    def body(*refs):
        exchange = _ChipExchange(refs[:n], refs[n:2 * n], *refs[2 * n:])
        exchange.start()
        exchange.finish()

    return pl.pallas_call(
        body, name="rs_chip_exchange",
        out_shape=[jax.ShapeDtypeStruct((3,) + p.shape[1:], p.dtype) for p in ps],
        in_specs=[HBM_SPEC] * n, out_specs=[HBM_SPEC] * n,
        scratch_shapes=_chip_scratch(n),
    )(*ps)


def _chip_add(p, recv, chip_idx):
    n, rh, cols = p.shape
    tr = _add_rows(rh, cols)

    def body(s_ref, own_ref, r_ref, o_ref):
        o_ref[...] = ((own_ref[0] + r_ref[0].astype(F32)) + r_ref[1].astype(F32)) + r_ref[2].astype(F32)

    return pl.pallas_call(
        body, name="rs_chip_add",
        grid_spec=pltpu.PrefetchScalarGridSpec(
            num_scalar_prefetch=1, grid=(rh // tr,),
            in_specs=[pl.BlockSpec((1, tr, cols), lambda i, s_ref: (s_ref[0], i, 0)),
                      pl.BlockSpec((3, tr, cols), lambda i, s_ref: (0, i, 0))],
            out_specs=pl.BlockSpec((tr, cols), lambda i, s_ref: (i, 0))),
        out_shape=jax.ShapeDtypeStruct((rh, cols), p.dtype),
        compiler_params=_params(("parallel",)),
    )(chip_idx, p, recv)


def _pair_gather(fs):
    n = len(fs)

    def body(*refs):
        f_refs, out_refs, (send_sems, recv_sems) = refs[:n], refs[n:2 * n], refs[2 * n:]
        x, y, c, _ = _place()
        cps = [_remote(f_refs[a], out_refs[a], send_sems.at[a], recv_sems.at[a], (x, y, 1 - c)) for a in range(n)]
        for cp in cps:
            cp.start()
        for cp in cps:
            cp.wait()

    return pl.pallas_call(
        body, name="rs_pair_gather",
        out_shape=[jax.ShapeDtypeStruct(f.shape, f.dtype) for f in fs],
        in_specs=[HBM_SPEC] * n, out_specs=[HBM_SPEC] * n,
        scratch_shapes=[pltpu.SemaphoreType.DMA((n,)), pltpu.SemaphoreType.DMA((n,))],
    )(*fs)


def _allreduce_small(v):
    r, cols = v.shape

    def body(x_ref, out_ref, slots, send_sems, recv_sems):
        x, y, c, _ = _place()
        bits = [(bx, by, bc) for bx in (0, 1) for by in (0, 1) for bc in (0, 1)]

        def flip(b):
            return (1 - x if b[0] else x, 1 - y if b[1] else y, 1 - c if b[2] else c)

        slots[0] = x_ref[...]
        cps = [_remote(x_ref, slots.at[k], send_sems.at[k - 1], recv_sems.at[k - 1], flip(bits[k]))
               for k in range(1, 8)]
        for cp in cps:
            cp.start()
        for cp in cps:
            cp.wait()
        acc = None
        for b in bits:
            fx, fy, fc = flip(b)
            term = slots[4 * fx + 2 * fy + fc]
            acc = term if acc is None else acc + term
        out_ref[...] = acc

    return pl.pallas_call(
        body, name="allreduce_small",
        out_shape=jax.ShapeDtypeStruct((r, cols), v.dtype),
        in_specs=[VMEM_SPEC], out_specs=VMEM_SPEC,
        scratch_shapes=[pltpu.VMEM((8, r, cols), v.dtype), pltpu.SemaphoreType.DMA((7,)),
                        pltpu.SemaphoreType.DMA((7,))],
        compiler_params=pltpu.CompilerParams(vmem_limit_bytes=VMEM_LIMIT),
    )(v)


BIG = (("a_w_in", 2), ("a_w_out", 1), ("w_kv", 1), ("b_w_q", 1), ("b_w_out", 1), ("f_w_up", 2), ("f_w_down", 1))
SMALL = (("a_norm", 1), ("a_conv", 2), ("a_A_log", None), ("a_dt_bias", None), ("a_out_norm", None),
         ("kv_norm", None), ("b_norm", None), ("b_rel_bias", None), ("f_norm", None), ("f_conv", 2),
         ("f_conv_b", None), ("final_norm", None))
WEIGHT_ORDER = ("a_norm", "a_w_in", "a_conv", "a_A_log", "a_dt_bias", "a_out_norm", "a_w_out", "kv_norm", "w_kv",
                "b_norm", "b_w_q", "b_rel_bias", "b_w_out", "f_norm", "f_w_up", "f_conv", "f_conv_b", "f_w_down",
                "final_norm")


def _pad_rows(flat, cols, quantum):
    n = flat.shape[-1]
    rows = -(-n // (cols * quantum)) * quantum
    pad = [(0, 0)] * (flat.ndim - 1) + [(0, rows * cols - n)]
    return jnp.pad(flat, pad).reshape(flat.shape[:-1] + (rows, cols))


GROUPS = ("GU", "GD", "GK", "GI", "GA")
FIRST_GROUPS = ("GI", "GA")
GD_MEMBERS = ("f_w_down", "b_w_q", "b_w_out")


def _group_shards(w, dtype):
    def two(a):
        return a.reshape(-1, a.shape[-1])

    return dict(GU=two(w["f_w_up"]).astype(dtype),
                GD=jnp.concatenate([two(w[n]) for n in GD_MEMBERS]).astype(dtype),
                GK=w["w_kv"].astype(dtype),
                GI=two(w["a_w_in"]).astype(dtype),
                GA=two(w["a_w_out"]).astype(dtype))


def _ungroup(red, shard_shapes):
    b_rows = N_B * SQ_ROWS
    flat = dict(f_w_up=jnp.concatenate([red["GU0"], red["GU1"]]), f_w_down=jnp.concatenate([red["GD0"], red["GD1"]]),
                b_w_q=red["GB"][:b_rows], b_w_out=red["GB"][b_rows:], a_w_out=red["GA"], w_kv=red["GK"],
                a_w_in=jnp.concatenate([red["GI0"], red["GI1"]]))
    return {n: v.reshape(shard_shapes[n]) for n, v in flat.items()}


def _dense_a_in(gi):
    out = []
    for i in range(N_A):
        full = jnp.transpose(gi[:, i * D_MODEL:(i + 1) * D_MODEL], (1, 0, 2)).reshape(D_MODEL, -1)
        out.append((full[:, :A_CONV_WIDTH], full[:, A_CONV_WIDTH:A_CONV_WIDTH + A_QK],
                    jnp.pad(full[:, A_CONV_WIDTH + A_QK:], ((0, 0), (0, LANE - 2 * A_HEADS)))))
    return out


def _pack_small(values, names):
    return _pad_rows(jnp.concatenate([values[n].reshape(-1) for n in names]), LANE, SUB)


def _unpack_small(packed, shapes, names):
    flat = packed.reshape(-1)
    out, off = {}, 0
    for n in names:
        size = math.prod(shapes[n])
        out[n] = flat[off:off + size].reshape(shapes[n])
        off += size
    return out


def _adamw_nd(w, g, m, v):
    shp = w.shape
    two = (math.prod(shp[:-1]), shp[-1])
    d, mn, vn = _adamw(w.reshape(two), g.reshape(two), m.reshape(two), v.reshape(two))
    return d.reshape(shp), mn.reshape(shp), vn.reshape(shp)


def kernel(x, a_norm, a_w_in, a_conv, a_A_log, a_dt_bias, a_out_norm, a_w_out, kv_norm, w_kv, b_norm, b_w_q, b_rel_bias, b_w_out, f_norm, f_w_up, f_conv, f_conv_b, f_w_down, final_norm, loss_target, m_a_norm, m_a_w_in, m_a_conv, m_a_A_log, m_a_dt_bias, m_a_out_norm, m_a_w_out, m_kv_norm, m_w_kv, m_b_norm, m_b_w_q, m_b_rel_bias, m_b_w_out, m_f_norm, m_f_w_up, m_f_conv, m_f_conv_b, m_f_w_down, m_final_norm, v_a_norm, v_a_w_in, v_a_conv, v_a_A_log, v_a_dt_bias, v_a_out_norm, v_a_w_out, v_kv_norm, v_w_kv, v_b_norm, v_b_w_q, v_b_rel_bias, v_b_w_out, v_f_norm, v_f_w_up, v_f_conv, v_f_conv_b, v_f_w_down, v_final_norm):
    w = dict(a_norm=a_norm, a_w_in=a_w_in, a_conv=a_conv, a_A_log=a_A_log, a_dt_bias=a_dt_bias,
             a_out_norm=a_out_norm, a_w_out=a_w_out, kv_norm=kv_norm, w_kv=w_kv, b_norm=b_norm, b_w_q=b_w_q,
             b_rel_bias=b_rel_bias, b_w_out=b_w_out, f_norm=f_norm, f_w_up=f_w_up, f_conv=f_conv,
             f_conv_b=f_conv_b, f_w_down=f_w_down, final_norm=final_norm)
    m = dict(a_norm=m_a_norm, a_w_in=m_a_w_in, a_conv=m_a_conv, a_A_log=m_a_A_log, a_dt_bias=m_a_dt_bias,
             a_out_norm=m_a_out_norm, a_w_out=m_a_w_out, kv_norm=m_kv_norm, w_kv=m_w_kv, b_norm=m_b_norm,
             b_w_q=m_b_w_q, b_rel_bias=m_b_rel_bias, b_w_out=m_b_w_out, f_norm=m_f_norm, f_w_up=m_f_w_up,
             f_conv=m_f_conv, f_conv_b=m_f_conv_b, f_w_down=m_f_w_down, final_norm=m_final_norm)
    v = dict(a_norm=v_a_norm, a_w_in=v_a_w_in, a_conv=v_a_conv, a_A_log=v_a_A_log, a_dt_bias=v_a_dt_bias,
             a_out_norm=v_a_out_norm, a_w_out=v_a_w_out, kv_norm=v_kv_norm, w_kv=v_w_kv, b_norm=v_b_norm,
             b_w_q=v_b_w_q, b_rel_bias=v_b_rel_bias, b_w_out=v_b_w_out, f_norm=v_f_norm, f_w_up=v_f_w_up,
             f_conv=v_f_conv, f_conv_b=v_f_conv_b, f_w_down=v_f_w_down, final_norm=v_final_norm)
    xi, yi, ci = lax.axis_index("x"), lax.axis_index("y"), lax.axis_index("c")
    chip = 2 * xi + yi
    shard_shapes = {n: w[n].shape for n in WEIGHT_ORDER}

    mine_w = _group_shards(w, BF16)
    full = dict(zip(FIRST_GROUPS, _allgather_weights([mine_w[n] for n in FIRST_GROUPS])))
    full["a_in"] = _dense_a_in(full.pop("GI"))
    pending = {n: mine_w[n] for n in GROUPS if n not in FIRST_GROUPS}
    sharded_small = [n for n, axis in SMALL if axis is not None]
    placed = {}
    for n, axis in SMALL:
        if axis is not None:
            wide = list(w[n].shape)
            wide[axis] *= 4
            mine_once = w[n] * (1 - ci).astype(F32)
            placed[n] = lax.dynamic_update_slice_in_dim(jnp.zeros(wide, F32), mine_once, chip * w[n].shape[axis], axis)
    placed_shapes = {n: placed[n].shape for n in sharded_small}
    full.update(_unpack_small(_allreduce_small(_pack_small(placed, sharded_small)), placed_shapes, sharded_small))
    for n, axis in SMALL:
        if axis is None:
            full[n] = w[n]

    c_idx = jnp.reshape(ci, (1,)).astype(jnp.int32)
    chip_idx = jnp.reshape(chip, (1,)).astype(jnp.int32)

    def pair_stage(gs):
        pairs = [_pair_add(g, o, c_idx) for g, o in zip(gs, _pair_exchange(gs))]
        return pairs, [pb for _, pb in pairs]

    def chip_stage(pairs, arrived):
        mine = [_chip_add(p, r, chip_idx) for (p, _), r in zip(pairs, arrived)]
        theirs = _pair_gather(mine)
        return [jnp.concatenate([jnp.where(ci == 0, a, b), jnp.where(ci == 0, b, a)], axis=0)
                for a, b in zip(mine, theirs)]

    loss_part, grad_x, grads = _local_step(x[0], loss_target[0], full, pending, (pair_stage, chip_stage))
    last_pairs, outgoing = pair_stage([grads[n] for n in FINAL])
    last = chip_stage(last_pairs, _chip_exchange(outgoing))
    red = _ungroup(dict(grads["reduced"], **dict(zip(FINAL, last))), shard_shapes)

    small_names = [n for n, _ in SMALL]
    small_vals = {n: grads[n] for n in small_names}
    small_vals["loss"] = loss_part[0, :1]
    names = ["loss"] + small_names
    shapes = {n: small_vals[n].shape for n in names}
    summed = _unpack_small(_allreduce_small(_pack_small(small_vals, names)), shapes, names)
    loss = summed["loss"][0]
    for n, axis in SMALL:
        g = summed[n]
        if axis is not None:
            g = lax.dynamic_slice_in_dim(g, chip * w[n].shape[axis], w[n].shape[axis], axis)
        red[n] = g

    delta, new_m, new_v = {}, {}, {}
    for n, _ in BIG:
        delta[n], new_m[n], new_v[n] = _adamw_nd(w[n], red[n], m[n], v[n])
    local_shapes = {n: w[n].shape for n in small_names}
    packs = [_pack_small(t, small_names) for t in (w, red, m, v)]
    outs = _adamw(*packs)
    ds, ms, vs = (_unpack_small(o, local_shapes, small_names) for o in outs)
    delta.update(ds)
    new_m.update(ms)
    new_v.update(vs)

    return (loss, grad_x[None], *[red[n] for n in WEIGHT_ORDER], *[delta[n] for n in WEIGHT_ORDER],
            *[new_m[n] for n in WEIGHT_ORDER], *[new_v[n] for n in WEIGHT_ORDER])
```

```python
import math

import jax
import jax.numpy as jnp
from jax import lax
from jax.experimental import pallas as pl
from jax.experimental.pallas import tpu as pltpu

F32 = jnp.float32
BF16 = jnp.bfloat16
HIGHEST = lax.Precision.HIGHEST
MESH = pl.DeviceIdType.MESH

D_MODEL = 1024
CHUNK = 64
A_HEADS = 8
A_HEAD = 128
A_QK = A_HEADS * A_HEAD
A_CONV_WIDTH = 3 * A_QK
B_HEADS = 16
B_HEAD = 64
LEFT = 8 * CHUNK
QBLK = 4 * CHUNK
KBLK = LEFT + QBLK
REL_CLIP = 256
REL_PAD = 640
FFN_DIM = 2816
EPS = 1e-6
NEG_INF = -1e30
LANE = 128
SUB = 8
VMEM_LIMIT = 56 * 1024 * 1024

ADAM_LR = 0.001
ADAM_B1 = 0.9
ADAM_B2 = 0.999
ADAM_EPS = 1e-08
ADAM_WD = 0.01
ADAM_STEP = 10


VMEM_FULL = pl.BlockSpec(memory_space=pltpu.VMEM)


def _params(sem=None):
    return pltpu.CompilerParams(dimension_semantics=sem, vmem_limit_bytes=VMEM_LIMIT)


def _tile(n, cap):
    if n <= cap:
        return n
    best = None
    for t in range(LANE, cap + 1, LANE):
        if n % t == 0:
            best = t
    assert best is not None, n
    return best


def _sigmoid(x):
    return 0.5 * jnp.tanh(0.5 * x) + 0.5


def _softplus(x):
    return jnp.maximum(x, 0.0) + jnp.log(1.0 + jnp.exp(-jnp.abs(x)))


def _dot(a, b, dims, prec=None):
    return lax.dot_general(a, b, (dims, ((), ())), preferred_element_type=F32, precision=prec)


NN = ((1,), (0,))
NT = ((1,), (1,))
TN = ((0,), (0,))


def _bdot(a, b, dims):
    return _dot(a.astype(BF16), b.astype(BF16), dims)


def _mm(a, b, mode="nn", out_dtype=F32, res=None, name="mm", norm=None):
    if mode == "nn":
        (m, k), (k2, n) = a.shape, b.shape
    elif mode == "nt":
        (m, k), (n, k2) = a.shape, b.shape
    else:
        (k, m), (k2, n) = a.shape, b.shape
    assert k == k2, (a.shape, b.shape, mode)
    tm, tn, tk = _tile(m, 1408), _tile(n, 1408), _tile(k, 1408)
    if m == 8192:
        tm = 1024
    if k == 8192:
        tk = 1024
    nk = k // tk
    dims = {"nn": NN, "nt": NT, "tn": TN}[mode]
    a_spec = {"nn": pl.BlockSpec((tm, tk), lambda i, j, kk: (i, kk)),
              "nt": pl.BlockSpec((tm, tk), lambda i, j, kk: (i, kk)),
              "tn": pl.BlockSpec((tk, tm), lambda i, j, kk: (kk, i))}[mode]
    b_spec = {"nn": pl.BlockSpec((tk, tn), lambda i, j, kk: (kk, j)),
              "nt": pl.BlockSpec((tn, tk), lambda i, j, kk: (j, kk)),
              "tn": pl.BlockSpec((tk, tn), lambda i, j, kk: (kk, j))}[mode]
    o_spec = pl.BlockSpec((tm, tn), lambda i, j, kk: (i, j))
    return _mm_call(name, a, b, dims, (m // tm, n // tn, nk), a_spec, b_spec, o_spec, (m, n), out_dtype, (tm, tn),
                    res, norm)


ROW_TILE = 1024
N_SHARDS = 4


def _mm_call(name, a, b, dims, grid, a_spec, b_spec, o_spec, out_shape, out_dtype, acc_shape, res=None, norm=None,
             into=None):
    nk = grid[2]
    has_res = res is not None
    has_norm = norm is not None
    has_into = into is not None
    if has_norm:
        assert grid[1] == 1 and len(out_shape) == 2 and out_dtype == F32

    def flat(v):
        return v.reshape(-1, v.shape[-1]) if v.ndim == 3 else v

    def body(a_ref, b_ref, *rest):
        rest = list(rest)
        res_ref = rest.pop(0) if has_res else None
        x_ref, g_ref, dres_ref = (rest.pop(0), rest.pop(0), rest.pop(0)) if has_norm else (None, None, None)
        if has_into:
            rest.pop(0)
        o_ref = rest.pop(0)
        dg_ref = rest.pop(0) if has_norm else None
        acc = rest.pop(0)
        kk = pl.program_id(2)
        first_rows = pl.program_id(0) == 0

        @pl.when(kk == 0)
        def _():
            acc[...] = jnp.zeros_like(acc)

        if has_norm:
            @pl.when(first_rows & (kk == 0))
            def _():
                dg_ref[...] = jnp.zeros_like(dg_ref)

        acc[...] += _bdot(flat(a_ref[...]), flat(b_ref[...]), dims)

        @pl.when(kk == nk - 1)
        def _():
            r = acc[...]
            if has_res:
                r = r + res_ref[...]
            if not has_norm:
                o_ref[...] = r.reshape(o_ref.shape).astype(out_dtype)
                return
            xv = x_ref[...]
            rs = lax.rsqrt(jnp.mean(xv * xv, axis=-1, keepdims=True) + EPS)
            t = r * g_ref[...]
            c = jnp.mean(t * xv, axis=-1, keepdims=True)
            o_ref[...] = dres_ref[...] + rs * t - xv * (rs * rs * rs) * c
            dg_ref[...] += jnp.sum(r * xv * rs, axis=0, keepdims=True)

    args = [a, b] + ([res] if has_res else [])
    in_specs = [a_spec, b_spec] + ([o_spec] if has_res else [])
    out_specs, out_shapes = o_spec, jax.ShapeDtypeStruct(out_shape, out_dtype)
    if has_norm:
        vec = pl.BlockSpec((1, out_shape[1]), lambda i, j, kk: (0, 0))
        args += list(norm)
        in_specs += [o_spec, vec, o_spec]
        out_specs = [o_spec, vec]
        out_shapes = [out_shapes, jax.ShapeDtypeStruct((1, out_shape[1]), F32)]
    aliases = {}
    if has_into:
        assert not has_norm and into.dtype == out_dtype
        aliases = {len(args): 0}
        args.append(into)
        in_specs.append(pl.BlockSpec(memory_space=pl.ANY))
        out_shapes = jax.ShapeDtypeStruct(into.shape, out_dtype)
    return pl.pallas_call(
        body, name=name, grid=grid, in_specs=in_specs, out_specs=out_specs, out_shape=out_shapes,
        scratch_shapes=[pltpu.VMEM(acc_shape, F32)], input_output_aliases=aliases,
        compiler_params=_params(("arbitrary" if has_norm else "parallel", "parallel", "arbitrary")),
    )(*args)


def _shards_per_block(rows):
    return N_SHARDS if N_SHARDS * rows <= 1408 else 2


def _mm_rowsh(a, buf, rows, blk0, mode, name, res=None, out_dtype=F32, norm=None):
    s = a.shape[0]
    cols = buf.shape[2]
    g = _shards_per_block(rows)
    tm = _tile(s, ROW_TILE)
    b_blk = (g, rows, cols)
    if mode == "nn":
        return _mm_call(name, a, buf, NN, (s // tm, 1, N_SHARDS // g),
                        pl.BlockSpec((tm, g * rows), lambda i, j, kk: (i, kk)),
                        pl.BlockSpec(b_blk, lambda i, j, kk: (kk, blk0, 0)),
                        pl.BlockSpec((tm, cols), lambda i, j, kk: (i, 0)),
                        (s, cols), out_dtype, (tm, cols), res)
    return _mm_call(name, a, buf, NT, (s // tm, N_SHARDS // g, 1),
                    pl.BlockSpec((tm, cols), lambda i, j, kk: (i, 0)),
                    pl.BlockSpec(b_blk, lambda i, j, kk: (j, blk0, 0)),
                    pl.BlockSpec((tm, g * rows), lambda i, j, kk: (i, j)),
                    (s, N_SHARDS * rows), out_dtype, (tm, g * rows), res, norm)


def _mm_rowsh_dw(act, dy, rows, name, into, blk0):
    s = act.shape[0]
    cols = dy.shape[1]
    g = _shards_per_block(rows)
    ts = _tile(s, ROW_TILE)
    return _mm_call(name, act, dy, TN, (1, N_SHARDS // g, s // ts),
                    pl.BlockSpec((ts, g * rows), lambda i, j, kk: (kk, j)),
                    pl.BlockSpec((ts, cols), lambda i, j, kk: (kk, 0)),
                    pl.BlockSpec((g, rows, cols), lambda i, j, kk: (j, blk0, 0)),
                    into.shape, F32, (g * rows, cols), into=into)


def _mm_colsh(a, buf, krows, blk0, mode, name, flat=False, res=None, out_dtype=F32, norm=None):
    cols = buf.shape[2]
    b_nn = pl.BlockSpec((None, krows, cols), lambda i, j, kk: (j, blk0, 0))
    b_nt = pl.BlockSpec((None, krows, cols), lambda i, j, kk: (kk, blk0, 0))
    if mode == "nn":
        s = a.shape[0]
        tm = _tile(s, ROW_TILE)
        o_spec = (pl.BlockSpec((tm, cols), lambda i, j, kk: (i, j)) if flat
                  else pl.BlockSpec((None, tm, cols), lambda i, j, kk: (j, i, 0)))
        return _mm_call(name, a, buf, NN, (s // tm, N_SHARDS, 1),
                        pl.BlockSpec((tm, krows), lambda i, j, kk: (i, 0)), b_nn, o_spec,
                        (s, N_SHARDS * cols) if flat else (N_SHARDS, s, cols), out_dtype, (tm, cols), res)
    s = a.shape[0] if flat else a.shape[1]
    tm = _tile(s, ROW_TILE)
    a_spec = (pl.BlockSpec((tm, cols), lambda i, j, kk: (i, kk)) if flat
              else pl.BlockSpec((None, tm, cols), lambda i, j, kk: (kk, i, 0)))
    return _mm_call(name, a, buf, NT, (s // tm, 1, N_SHARDS), a_spec, b_nt,
                    pl.BlockSpec((tm, krows), lambda i, j, kk: (i, 0)),
                    (s, krows), out_dtype, (tm, krows), res, norm)


def _mm_colsh_dw(x, dy, name, flat=False, into=None, blk0=0):
    s, k = x.shape
    cols = dy.shape[1] // N_SHARDS if flat else dy.shape[2]
    ts = _tile(s, ROW_TILE)
    b_spec = (pl.BlockSpec((ts, cols), lambda i, j, kk: (kk, j)) if flat
              else pl.BlockSpec((None, ts, cols), lambda i, j, kk: (j, kk, 0)))
    return _mm_call(name, x, dy, TN, (1, N_SHARDS, s // ts),
                    pl.BlockSpec((ts, k), lambda i, j, kk: (kk, 0)), b_spec,
                    pl.BlockSpec((None, k, cols), lambda i, j, kk: (j, blk0, 0)),
                    (N_SHARDS, k, cols) if into is None else into.shape, F32, (k, cols), into=into)


def _rmsnorm_fwd(x, g):
    s, d = x.shape
    tr = _tile(s, 1024)

    def body(x_ref, g_ref, o_ref):
        xv = x_ref[...]
        r = lax.rsqrt(jnp.mean(xv * xv, axis=-1, keepdims=True) + EPS)
        o_ref[...] = (xv * r * g_ref[...]).astype(BF16)

    return pl.pallas_call(
        body, name="rmsnorm_fwd", grid=(s // tr,),
        in_specs=[pl.BlockSpec((tr, d), lambda i: (i, 0)), pl.BlockSpec((1, d), lambda i: (0, 0))],
        out_specs=pl.BlockSpec((tr, d), lambda i: (i, 0)),
        out_shape=jax.ShapeDtypeStruct((s, d), BF16),
        compiler_params=_params(("parallel",)),
    )(x, g)


def _final_loss(h, g, tgt):
    s, d = h.shape
    tr = _tile(s, 1024)

    def body(x_ref, g_ref, t_ref, loss_ref, dx_ref, dg_ref):
        @pl.when(pl.program_id(0) == 0)
        def _():
            dg_ref[...] = jnp.zeros_like(dg_ref)
            loss_ref[...] = jnp.zeros_like(loss_ref)

        xv = x_ref[...]
        r = lax.rsqrt(jnp.mean(xv * xv, axis=-1, keepdims=True) + EPS)
        xh = xv * r
        err = xh * g_ref[...] - t_ref[...]
        per_row = jnp.mean(err * err, axis=-1, keepdims=True)
        loss_ref[...] += 0.5 * jnp.sum(per_row, axis=0, keepdims=True)
        dy = err * (1.0 / d)
        t = dy * g_ref[...]
        c = jnp.mean(t * xv, axis=-1, keepdims=True)
        dx_ref[...] = r * t - xv * (r * r * r) * c
        dg_ref[...] += jnp.sum(dy * xh, axis=0, keepdims=True)

    row = pl.BlockSpec((tr, d), lambda i: (i, 0))
    vec = pl.BlockSpec((1, d), lambda i: (0, 0))
    return pl.pallas_call(
        body, name="final_loss", grid=(s // tr,),
        in_specs=[row, vec, row],
        out_specs=[pl.BlockSpec((1, LANE), lambda i: (0, 0)), row, vec],
        out_shape=[jax.ShapeDtypeStruct((1, LANE), F32), jax.ShapeDtypeStruct((s, d), F32),
                   jax.ShapeDtypeStruct((1, d), F32)],
        compiler_params=_params(("arbitrary",)),
    )(h, g, tgt)


CONV_ROWS = 512
CONV_COLS = 1408
CONV_CHUNK = 64


def _per_lane_tile(tile_body):
    def body(*refs):
        for lt in range(refs[0].shape[-1] // LANE):
            cols = slice(lt * LANE, (lt + 1) * LANE)
            tile_body(*[r.at[(slice(None),) * (len(r.shape) - 1) + (cols,)] for r in refs])
    return body


def _lagged(window, lag):
    return (pltpu.roll(window, lag, 0) if lag else window)[SUB:]


def _led(window, lead):
    n = window.shape[0] - SUB
    return (pltpu.roll(window, window.shape[0] - lead, 0) if lead else window)[:n]


def _fold_rows(v):
    return jnp.sum(v.reshape(v.shape[0] // SUB, SUB, v.shape[1]), axis=0)


def _taps(shifted, w):
    acc = None
    for k, xs in enumerate(shifted):
        term = xs * w[k:k + 1, :]
        acc = term if acc is None else acc + term
    return acc


def _conv_tiles(s, c):
    return _tile(s, CONV_ROWS), _tile(c, CONV_COLS)


def _conv_silu_fwd(pre, w):
    s, c = pre.shape
    width = w.shape[0]
    tr, tc = _conv_tiles(s, c)

    def body(x_ref, w_ref, o_ref, tail):
        @pl.when(pl.program_id(1) == 0)
        def _():
            tail[...] = jnp.zeros_like(tail)

        wv = w_ref[...]

        def do(c0, window):
            y = _taps([_lagged(window, width - 1 - k) for k in range(width)], wv)
            o_ref[pl.ds(c0, CONV_CHUNK), :] = y * _sigmoid(y)

        def chunk(ci, carry):
            c0 = pl.multiple_of(ci * CONV_CHUNK, CONV_CHUNK)
            do(c0, x_ref[pl.ds(pl.multiple_of(c0 - SUB, SUB), CONV_CHUNK + SUB), :])
            return carry

        do(0, jnp.concatenate([tail[...], x_ref[:CONV_CHUNK, :]], axis=0))
        lax.fori_loop(1, tr // CONV_CHUNK, chunk, 0)
        tail[...] = x_ref[tr - SUB:, :]

    blk = pl.BlockSpec((tr, tc), lambda j, i: (i, j))
    return pl.pallas_call(
        _per_lane_tile(body), name="conv_silu_fwd", grid=(c // tc, s // tr),
        in_specs=[blk, pl.BlockSpec((width, tc), lambda j, i: (0, j))], out_specs=blk,
        out_shape=jax.ShapeDtypeStruct((s, c), F32),
        scratch_shapes=[pltpu.VMEM((SUB, tc), F32)],
        compiler_params=_params(("parallel", "arbitrary")),
    )(pre, w)


def _prev_rows_index(i_blk, tr):
    return jnp.maximum(i_blk * (tr // SUB) - 1, 0)


def _conv_silu_bwd(pre, w, dact):
    s, c = pre.shape
    width = w.shape[0]
    tr, tc = _conv_tiles(s, c)
    nr = s // tr

    nchunks = tr // CONV_CHUNK

    def body(x_ref, p_ref, w_ref, d_ref, dx_ref, dw_ref, head):
        @pl.when(pl.program_id(1) == 0)
        def _():
            head[...] = jnp.zeros_like(head)
            dw_ref[...] = jnp.zeros_like(dw_ref)

        wv = w_ref[...]

        def do(c0, window, later, dws):
            xs = [_lagged(window, width - 1 - k) for k in range(width)]
            y = _taps(xs, wv)
            sg = _sigmoid(y)
            dy = d_ref[pl.ds(c0, CONV_CHUNK), :] * sg * (1.0 + y * (1.0 - sg))
            dws = tuple(dw + _fold_rows(dy * x) for dw, x in zip(dws, xs))
            ahead = jnp.concatenate([dy, later], axis=0)
            dx_ref[pl.ds(c0, CONV_CHUNK), :] = _taps([_led(ahead, width - 1 - k) for k in range(width)],
                                                     wv).astype(BF16)
            return dy[:SUB], dws

        def chunk(it, carry):
            c0 = pl.multiple_of((nchunks - 1 - it) * CONV_CHUNK, CONV_CHUNK)
            return do(c0, x_ref[pl.ds(pl.multiple_of(c0 - SUB, SUB), CONV_CHUNK + SUB), :], *carry)

        zero = jnp.zeros((SUB, LANE), F32)
        carry = lax.fori_loop(0, nchunks - 1, chunk, (head[...], (zero,) * width))
        before = jnp.where(pl.program_id(1) == nr - 1, 0.0, p_ref[...])
        later, dws = do(0, jnp.concatenate([before, x_ref[:CONV_CHUNK, :]], axis=0), *carry)
        head[...] = later
        for k in range(width):
            dw_ref[k:k + 1, :] += jnp.sum(dws[k], axis=0, keepdims=True)

    blk = pl.BlockSpec((tr, tc), lambda j, i: (nr - 1 - i, j))
    prev = pl.BlockSpec((SUB, tc), lambda j, i: (_prev_rows_index(nr - 1 - i, tr), j))
    wblk = pl.BlockSpec((width, tc), lambda j, i: (0, j))
    return pl.pallas_call(
        _per_lane_tile(body), name="conv_silu_bwd", grid=(c // tc, nr),
        in_specs=[blk, prev, wblk, blk], out_specs=[blk, wblk],
        out_shape=[jax.ShapeDtypeStruct((s, c), BF16), jax.ShapeDtypeStruct((width, c), F32)],
        scratch_shapes=[pltpu.VMEM((SUB, tc), F32)],
        compiler_params=_params(("parallel", "arbitrary")),
    )(pre, pre, w, dact)


def _ffn_act_fwd(pre, w, b):
    _, halves, s, c = pre.shape
    width = w.shape[2]
    tr, tc = _conv_tiles(s, c)
    ncb = c // tc

    def body(x_ref, w_ref, b_ref, o_ref, tail):
        @pl.when(pl.program_id(2) == 0)
        def _():
            tail[...] = jnp.zeros_like(tail)

        wg, wv, bg, bv = w_ref[0], w_ref[1], b_ref[0], b_ref[1]

        def do(c0, win_g, win_v):
            yg = _taps([_lagged(win_g, width - 1 - k) for k in range(width)], wg) + bg
            yv = _taps([_lagged(win_v, width - 1 - k) for k in range(width)], wv) + bv
            o_ref[pl.ds(c0, CONV_CHUNK), :] = (yg * _sigmoid(yg) * yv).astype(BF16)

        def chunk(ci, carry):
            c0 = pl.multiple_of(ci * CONV_CHUNK, CONV_CHUNK)
            rows = pl.ds(pl.multiple_of(c0 - SUB, SUB), CONV_CHUNK + SUB)
            do(c0, x_ref[0, rows, :], x_ref[1, rows, :])
            return carry

        do(0, jnp.concatenate([tail[0], x_ref[0, :CONV_CHUNK, :]], axis=0),
           jnp.concatenate([tail[1], x_ref[1, :CONV_CHUNK, :]], axis=0))
        lax.fori_loop(1, tr // CONV_CHUNK, chunk, 0)
        tail[...] = x_ref[:, tr - SUB:, :]

    return pl.pallas_call(
        _per_lane_tile(body), name="ffn_act_fwd", grid=(halves, ncb, s // tr),
        in_specs=[pl.BlockSpec((2, None, tr, tc), lambda h, j, i: (0, h, i, j)),
                  pl.BlockSpec((2, None, width, tc), lambda h, j, i: (0, h, 0, j)),
                  pl.BlockSpec((2, None, 1, tc), lambda h, j, i: (0, h, 0, j))],
        out_specs=pl.BlockSpec((tr, tc), lambda h, j, i: (i, h * ncb + j)),
        out_shape=jax.ShapeDtypeStruct((s, halves * c), BF16),
        scratch_shapes=[pltpu.VMEM((2, SUB, tc), F32)],
        compiler_params=_params(("parallel", "parallel", "arbitrary")),
    )(pre, w, b)


def _ffn_act_bwd(pre, w, b, dact):
    _, halves, s, c = pre.shape
    width = w.shape[2]
    tr, tc = _conv_tiles(s, c)
    ncb = c // tc
    nr = s // tr
    nchunks = tr // CONV_CHUNK

    def body(x_ref, p_ref, w_ref, b_ref, d_ref, dx_ref, dw_ref, db_ref, head):
        @pl.when(pl.program_id(2) == 0)
        def _():
            for r in (head, dw_ref, db_ref):
                r[...] = jnp.zeros_like(r)

        wg, wv, bg, bv = w_ref[0], w_ref[1], b_ref[0], b_ref[1]

        def do(c0, win_g, win_v, later_g, later_v, dwg, dwv, dbg, dbv):
            xg = [_lagged(win_g, width - 1 - k) for k in range(width)]
            xv = [_lagged(win_v, width - 1 - k) for k in range(width)]
            yg = _taps(xg, wg) + bg
            yv = _taps(xv, wv) + bv
            sg = _sigmoid(yg)
            da = d_ref[pl.ds(c0, CONV_CHUNK), :]
            dyv = da * yg * sg
            dyg = da * yv * sg * (1.0 + yg * (1.0 - sg))
            dwg = tuple(dw + _fold_rows(dyg * x) for dw, x in zip(dwg, xg))
            dwv = tuple(dw + _fold_rows(dyv * x) for dw, x in zip(dwv, xv))
            dbg = dbg + _fold_rows(dyg)
            dbv = dbv + _fold_rows(dyv)
            ahead_g = jnp.concatenate([dyg, later_g], axis=0)
            ahead_v = jnp.concatenate([dyv, later_v], axis=0)
            dxg = _taps([_led(ahead_g, width - 1 - k) for k in range(width)], wg)
            dxv = _taps([_led(ahead_v, width - 1 - k) for k in range(width)], wv)
            dx_ref[0, pl.ds(c0, CONV_CHUNK), :] = dxg.astype(BF16)
            dx_ref[1, pl.ds(c0, CONV_CHUNK), :] = dxv.astype(BF16)
            return dyg[:SUB], dyv[:SUB], dwg, dwv, dbg, dbv

        def chunk(it, carry):
            c0 = pl.multiple_of((nchunks - 1 - it) * CONV_CHUNK, CONV_CHUNK)
            rows = pl.ds(pl.multiple_of(c0 - SUB, SUB), CONV_CHUNK + SUB)
            return do(c0, x_ref[0, rows, :], x_ref[1, rows, :], *carry)

        zero = jnp.zeros((SUB, LANE), F32)
        carry = lax.fori_loop(0, nchunks - 1, chunk,
                              (head[0], head[1], (zero,) * width, (zero,) * width, zero, zero))
        before = jnp.where(pl.program_id(2) == nr - 1, 0.0, p_ref[...])
        later_g, later_v, dwg, dwv, dbg, dbv = do(
            0, jnp.concatenate([before[0], x_ref[0, :CONV_CHUNK, :]], axis=0),
            jnp.concatenate([before[1], x_ref[1, :CONV_CHUNK, :]], axis=0), *carry)
        head[0] = later_g
        head[1] = later_v
        db_ref[0] += jnp.sum(dbg, axis=0, keepdims=True)
        db_ref[1] += jnp.sum(dbv, axis=0, keepdims=True)
        for k in range(width):
            dw_ref[0, k:k + 1, :] += jnp.sum(dwg[k], axis=0, keepdims=True)
            dw_ref[1, k:k + 1, :] += jnp.sum(dwv[k], axis=0, keepdims=True)

    blk = pl.BlockSpec((2, None, tr, tc), lambda h, j, i: (0, h, nr - 1 - i, j))
    prev = pl.BlockSpec((2, None, SUB, tc), lambda h, j, i: (0, h, _prev_rows_index(nr - 1 - i, tr), j))
    wblk = pl.BlockSpec((2, None, width, tc), lambda h, j, i: (0, h, 0, j))
    bblk = pl.BlockSpec((2, None, 1, tc), lambda h, j, i: (0, h, 0, j))
    return pl.pallas_call(
        _per_lane_tile(body), name="ffn_act_bwd", grid=(halves, ncb, nr),
        in_specs=[blk, prev, wblk, bblk, pl.BlockSpec((tr, tc), lambda h, j, i: (nr - 1 - i, h * ncb + j))],
        out_specs=[blk, wblk, bblk],
        out_shape=[jax.ShapeDtypeStruct(pre.shape, BF16), jax.ShapeDtypeStruct(w.shape, F32),
                   jax.ShapeDtypeStruct(b.shape, F32)],
        scratch_shapes=[pltpu.VMEM((2, SUB, tc), F32)],
        compiler_params=_params(("parallel", "parallel", "arbitrary")),
    )(pre, pre, w, b, dact)


def _tri_masks():
    row = lax.broadcasted_iota(jnp.int32, (CHUNK, CHUNK), 0)
    col = lax.broadcasted_iota(jnp.int32, (CHUNK, CHUNK), 1)
    return row, col


def _tri_inv(ms, row, col):
    eye = (row == col).astype(F32)
    same_blk = (row >> 4) == (col >> 4)
    mds = [jnp.where(same_blk, m, 0.0) for m in ms]
    offs = [m - md for m, md in zip(ms, mds)]
    xs = [eye - md for md in mds]
    ps = [_bdot(md, md, NN) for md in mds]
    for _ in range(2):
        rs = [_bdot(jnp.concatenate([x, p], axis=0), p, NN) for x, p in zip(xs, ps)]
        xs = [x + r[:CHUNK] for x, r in zip(xs, rs)]
        ps = [r[CHUNK:] for r in rs]
    xs = [x + _bdot(x, p, NN) for x, p in zip(xs, ps)]
    ps = [_bdot(x, off, NN) for x, off in zip(xs, offs)]
    pps = [_bdot(p, p, NN) for p in ps]
    ys = [eye - p for p in ps]
    ys = [y + _bdot(y, pp, NN) for y, pp in zip(ys, pps)]
    return [_bdot(y, x, NN) for y, x in zip(ys, xs)]


def _gdn_gates(ba, alog, dtb, row, col):
    sig = _sigmoid(ba)
    neg_a = -jnp.exp(alog)
    g = neg_a * _softplus(ba + dtb)
    lower = (row >= col).astype(F32)
    gcum = _dot(lower, g, NN, HIGHEST)
    return sig, neg_a, g, gcum


def _gdn_head_common(q_raw, k_raw, v, beta, gc, gr, row, col):
    causal = row >= col
    strict = row > col
    rq = lax.rsqrt(jnp.sum(q_raw * q_raw, axis=-1, keepdims=True) + EPS)
    rk = lax.rsqrt(jnp.sum(k_raw * k_raw, axis=-1, keepdims=True) + EPS)
    q = q_raw * (rq * (A_HEAD ** -0.5))
    k = k_raw * rk
    decay = jnp.where(causal, jnp.exp(jnp.where(causal, gc - gr, 0.0)), 0.0)
    eg = jnp.exp(gc)
    gl = gc[CHUNK - 1:CHUNK, :]
    ekl = jnp.exp(gl - gc)
    dec = jnp.exp(gl)
    kb = k * beta
    kbq = jnp.concatenate([kb, q], axis=0)
    both = _bdot(kbq, k, NT)
    kk, qk = both[:CHUNK], both[CHUNK:]
    a = jnp.where(causal, qk * decay, 0.0)
    return dict(rq=rq, rk=rk, q=q, k=k, decay=decay, eg=eg, ekl=ekl, dec=dec, kb=kb, kbq=kbq, kk=kk, qk=qk, a=a,
                vb=v * beta, kbg=kb * eg, qd=q * eg, ke=k * ekl, causal=causal, strict=strict)


def _gdn_fwd(qkv, ba, z, alog, dtb, wn, gather=()):
    s = qkv.shape[0]
    nc = s // CHUNK
    ng = len(gather)

    def body(qkv_ref, ba_ref, z_ref, alog_ref, dtb_ref, wn_ref, *rest):
        x_refs, rest = rest[:ng], rest[ng:]
        y_ref, o_ref, st_ref, t_ref, w_ref, vn_ref = rest[:6]
        out_refs, rest = rest[6:6 + ng], rest[6 + ng:]
        state = rest[0]
        exchange = _ShardGather(x_refs, out_refs, *rest[1:]) if ng else None

        @pl.when(pl.program_id(0) == 0)
        def _():
            state[...] = jnp.zeros_like(state)
            if ng:
                exchange.start()

        row, col = _tri_masks()
        sig, _, _, gcum = _gdn_gates(ba_ref[...], alog_ref[...], dtb_ref[...], row, col)
        gt = gcum.T
        heads = range(A_HEADS)
        lanes = [slice(h * A_HEAD, (h + 1) * A_HEAD) for h in heads]
        fs = [_gdn_head_common(qkv_ref[:, lanes[h]], qkv_ref[:, A_QK + h * A_HEAD:A_QK + (h + 1) * A_HEAD],
                               qkv_ref[:, 2 * A_QK + h * A_HEAD:2 * A_QK + (h + 1) * A_HEAD],
                               sig[:, h:h + 1], gcum[:, 8 + h:9 + h], gt[8 + h:9 + h, :], row, col) for h in heads]
        ts = [t.astype(BF16) for t in
              _tri_inv([jnp.where(f["strict"], f["kk"] * f["decay"], 0.0) for f in fs], row, col)]
        uws = [_bdot(t, jnp.concatenate([f["vb"], f["kbg"]], axis=1), NN) for t, f in zip(ts, fs)]
        s0s = [state[h] for h in heads]
        ws_ = [uw[:, A_HEAD:].astype(BF16) for uw in uws]
        wss = [_bdot(jnp.concatenate([w, f["qd"].astype(BF16)], axis=0), s0, NN) for w, f, s0 in zip(ws_, fs, s0s)]
        vnews = [(uw[:, :A_HEAD] - wsq[:CHUNK]).astype(BF16) for uw, wsq in zip(uws, wss)]
        os_ = [wsq[CHUNK:] + _bdot(f["a"], vn, NN) for wsq, f, vn in zip(wss, fs, vnews)]
        s1s = [s0 * f["dec"] + _bdot(f["ke"], vn, TN) for s0, f, vn in zip(s0s, fs, vnews)]
        for h in heads:
            ln = lanes[h]
            st_ref[0, h] = s0s[h]
            t_ref[0, h] = ts[h]
            w_ref[:, ln] = ws_[h]
            vn_ref[:, ln] = vnews[h]
            state[h] = s1s[h]
            o = os_[h]
            o_ref[:, ln] = o
            r = lax.rsqrt(jnp.mean(o * o, axis=-1, keepdims=True) + EPS)
            zz = z_ref[:, ln]
            y_ref[:, ln] = (o * r * wn_ref[...] * zz * _sigmoid(zz)).astype(BF16)

        if ng:
            @pl.when(pl.program_id(0) == nc - 1)
            def _():
                exchange.finish()

    vec = pl.BlockSpec((1, LANE), lambda n: (0, 0))
    wide = pl.BlockSpec((CHUNK, A_QK), lambda n: (n, 0))
    return pl.pallas_call(
        body, name="gdn_fwd_gather" if ng else "gdn_fwd", grid=(nc,),
        in_specs=[pl.BlockSpec((CHUNK, A_CONV_WIDTH), lambda n: (n, 0)),
                  pl.BlockSpec((CHUNK, LANE), lambda n: (n, 0)), wide, vec, vec, vec] + [HBM_SPEC] * ng,
        out_specs=[wide, wide, pl.BlockSpec((1, A_HEADS, A_HEAD, A_HEAD), lambda n: (n, 0, 0, 0)),
                   pl.BlockSpec((1, A_HEADS, CHUNK, CHUNK), lambda n: (n, 0, 0, 0)), wide, wide] + [HBM_SPEC] * ng,
        out_shape=[jax.ShapeDtypeStruct((s, A_QK), BF16), jax.ShapeDtypeStruct((s, A_QK), F32),
                   jax.ShapeDtypeStruct((nc, A_HEADS, A_HEAD, A_HEAD), F32),
                   jax.ShapeDtypeStruct((nc, A_HEADS, CHUNK, CHUNK), BF16),
                   jax.ShapeDtypeStruct((s, A_QK), BF16), jax.ShapeDtypeStruct((s, A_QK), BF16)]
        + [jax.ShapeDtypeStruct((N_SHARDS,) + g.shape, g.dtype) for g in gather],
        scratch_shapes=[pltpu.VMEM((A_HEADS, A_HEAD, A_HEAD), F32)] + (_gather_scratch(ng) if ng else []),
        compiler_params=_params(("arbitrary",)),
    )(qkv, ba, z, alog, dtb, wn, *gather)


def _gdn_bwd(qkv, ba, z, o_raw, dy, states, t_all, w_all, vn_all, alog, dtb, wn, exchange=()):
    s = qkv.shape[0]
    nc = s // CHUNK
    nx = len(exchange)

    def body(qkv_ref, ba_ref, z_ref, o_ref, dy_ref, st_ref, t_ref, w_ref, vn_ref, alog_ref, dtb_ref, wn_ref, *rest):
        p_refs, rest = rest[:nx], rest[nx:]
        dqkv_ref, dba_ref, dz_ref, dalog_ref, ddtb_ref, dwn_ref = rest[:6]
        arrive_refs, rest = rest[6:6 + nx], rest[6 + nx:]
        dstate = rest[0]
        swap = _ChipExchange(p_refs, arrive_refs, *rest[1:]) if nx else None

        @pl.when(pl.program_id(0) == 0)
        def _():
            for r in (dstate, dalog_ref, ddtb_ref, dwn_ref):
                r[...] = jnp.zeros_like(r)
            if nx:
                swap.start()

        row, col = _tri_masks()
        bat = ba_ref[...]
        sig, neg_a, g, gcum = _gdn_gates(bat, alog_ref[...], dtb_ref[...], row, col)
        gt = gcum.T
        lane = lax.broadcasted_iota(jnp.int32, (CHUNK, LANE), 1)
        ones = jnp.ones((CHUNK, LANE), F32)
        last_row = lax.broadcasted_iota(jnp.int32, (CHUNK, 1), 0) == CHUNK - 1
        wnv = wn_ref[...]
        dgc_tile = jnp.zeros((CHUNK, LANE), F32)
        dbeta_tile = jnp.zeros((CHUNK, LANE), F32)
        dwn_acc = jnp.zeros((1, LANE), F32)
        hs = []
        for h in range(A_HEADS):
            ln = slice(h * A_HEAD, (h + 1) * A_HEAD)
            lk = slice(A_QK + h * A_HEAD, A_QK + (h + 1) * A_HEAD)
            lv = slice(2 * A_QK + h * A_HEAD, 2 * A_QK + (h + 1) * A_HEAD)
            q_raw, k_raw, v = qkv_ref[:, ln], qkv_ref[:, lk], qkv_ref[:, lv]
            f = _gdn_head_common(q_raw, k_raw, v, sig[:, h:h + 1], gcum[:, 8 + h:9 + h], gt[8 + h:9 + h, :], row, col)
            f.update(h=h, ln=ln, lk=lk, lv=lv, q_raw=q_raw, k_raw=k_raw, v=v, beta=sig[:, h:h + 1],
                     s0=st_ref[0, h], ds1=dstate[h], t=t_ref[0, h], w=w_ref[:, ln], vnew=vn_ref[:, ln])
            o = o_ref[:, ln]
            zz = z_ref[:, ln]
            dyv = dy_ref[:, ln]
            r = lax.rsqrt(jnp.mean(o * o, axis=-1, keepdims=True) + EPS)
            sz = _sigmoid(zz)
            silu = zz * sz
            dz_ref[:, ln] = (dyv * o * r * wnv * sz * (1.0 + zz * (1.0 - sz))).astype(BF16)
            dwn_acc = dwn_acc + jnp.sum(dyv * silu * o * r, axis=0, keepdims=True)
            tt = dyv * silu * wnv
            do = r * tt - o * (r * r * r) * jnp.mean(tt * o, axis=-1, keepdims=True)
            f["do_b"] = do.astype(BF16)
            hs.append(f)
        for f in hs:
            f["dvnew"] = _bdot(f["a"], f["do_b"], TN) + _bdot(f["ke"], f["ds1"], NN)
            f["da"] = jnp.where(f["causal"], _bdot(f["do_b"], f["vnew"], NT), 0.0)
            f["dke"] = _bdot(f["vnew"], f["ds1"], NT)
            f["ddec"] = jnp.sum(jnp.sum(f["s0"] * f["ds1"], axis=1, keepdims=True), axis=0, keepdims=True)
        for f in hs:
            do_dv = jnp.concatenate([f["do_b"], f["dvnew"].astype(BF16)], axis=0)
            both = _bdot(do_dv, f["s0"], NT)
            f["dqd"], f["dw"] = both[:CHUNK], -both[CHUNK:]
            qd_w = jnp.concatenate([f["qd"].astype(BF16), -f["w"]], axis=0)
            dstate[f["h"]] = _bdot(qd_w, do_dv, TN) + f["dec"] * f["ds1"]
        for f in hs:
            dd = jnp.concatenate([f["dvnew"], f["dw"]], axis=1).astype(BF16)
            tdd = _bdot(f["t"], dd, TN)
            f["dvb"], f["dkbg"] = tdd[:, :A_HEAD], tdd[:, A_HEAD:]
            f["dt"] = _bdot(dd, jnp.concatenate([f["vb"], f["kbg"]], axis=1), NT)
        for f in hs:
            f["tdt"] = _bdot(f["t"], f["dt"], TN)
        for f in hs:
            dm = jnp.where(f["strict"], -_bdot(f["tdt"], f["t"], NT), 0.0)
            f["ddecay"] = (dm * f["kk"] + f["da"] * f["qk"]) * f["decay"]
            f["dboth"] = jnp.concatenate([dm * f["decay"], f["da"] * f["decay"]], axis=0).astype(BF16)
        for f in hs:
            f["r2"] = _bdot(f["dboth"], f["k"], NN)
            f["dk0"] = _bdot(f["dboth"], f["kbq"], TN)
        for f in hs:
            h, k, beta = f["h"], f["k"], f["beta"]
            dkb = f["r2"][:CHUNK] + f["dkbg"] * f["eg"]
            dq = f["r2"][CHUNK:] + f["dqd"] * f["eg"]
            dk = f["dk0"] + f["dke"] * f["ekl"] + dkb * beta
            dke_ke = jnp.sum(f["dke"] * f["ke"], axis=-1, keepdims=True)
            dgc = (jnp.sum(f["ddecay"], axis=-1, keepdims=True)
                   + jnp.sum(f["dqd"] * f["qd"], axis=-1, keepdims=True) - dke_ke
                   + jnp.sum(f["dkbg"] * f["kbg"], axis=-1, keepdims=True))
            dgl = jnp.sum(dke_ke, axis=0, keepdims=True) + f["ddec"] * f["dec"]
            dgc = dgc + jnp.where(last_row, dgl, 0.0)
            dbeta = jnp.sum(dkb * k, axis=-1, keepdims=True) + jnp.sum(f["dvb"] * f["v"], axis=-1, keepdims=True)
            dgc_tile = dgc_tile + jnp.where(lane == 8 + h, dgc, 0.0)
            dbeta_tile = dbeta_tile + jnp.where(lane == h, dbeta, 0.0)
            dqn = dq * (A_HEAD ** -0.5)
            rq, rk, q_raw, k_raw = f["rq"], f["rk"], f["q_raw"], f["k_raw"]
            dqkv_ref[:, f["ln"]] = rq * dqn - q_raw * (rq * rq * rq) * jnp.sum(dqn * q_raw, axis=-1, keepdims=True)
            dqkv_ref[:, f["lk"]] = rk * dk - k_raw * (rk * rk * rk) * jnp.sum(dk * k_raw, axis=-1, keepdims=True)
            dqkv_ref[:, f["lv"]] = f["dvb"] * beta
        ddecays = [f["ddecay"] for f in hs]
        col_sums = _dot(jnp.concatenate(ddecays, axis=1), ones, TN, HIGHEST)
        for h in range(A_HEADS):
            dgc_tile = dgc_tile - jnp.where(lane == 8 + h, col_sums[h * CHUNK:(h + 1) * CHUNK, :1], 0.0)
        upper = (row <= col).astype(F32)
        dg = _dot(upper, dgc_tile, NN, HIGHEST)
        da_raw = dg * neg_a * _sigmoid(bat + dtb_ref[...])
        dba_ref[...] = jnp.where(lane < 8, dbeta_tile * sig * (1.0 - sig),
                                 jnp.where(lane < 16, da_raw, 0.0)).astype(BF16)
        dwn_ref[...] += dwn_acc
        ddtb_ref[...] += jnp.sum(da_raw, axis=0, keepdims=True)
        dalog_ref[...] += jnp.sum(dg * g, axis=0, keepdims=True)

        if nx:
            @pl.when(pl.program_id(0) == nc - 1)
            def _():
                swap.finish()

    rev = lambda n: (nc - 1 - n, 0)
    vec = pl.BlockSpec((1, LANE), lambda n: (0, 0))
    wide = pl.BlockSpec((CHUNK, A_QK), rev)
    qkv_blk = pl.BlockSpec((CHUNK, A_CONV_WIDTH), rev)
    ba_blk = pl.BlockSpec((CHUNK, LANE), rev)
    vsh = jax.ShapeDtypeStruct((1, LANE), F32)
    return pl.pallas_call(
        body, name="gdn_bwd_exchange" if nx else "gdn_bwd", grid=(nc,),
        in_specs=[qkv_blk, ba_blk, wide, wide, wide,
                  pl.BlockSpec((1, A_HEADS, A_HEAD, A_HEAD), lambda n: (nc - 1 - n, 0, 0, 0)),
                  pl.BlockSpec((1, A_HEADS, CHUNK, CHUNK), lambda n: (nc - 1 - n, 0, 0, 0)), wide, wide,
                  vec, vec, vec] + [HBM_SPEC] * nx,
        out_specs=[qkv_blk, ba_blk, wide, vec, vec, vec] + [HBM_SPEC] * nx,
        out_shape=[jax.ShapeDtypeStruct((s, A_CONV_WIDTH), F32), jax.ShapeDtypeStruct((s, LANE), BF16),
                   jax.ShapeDtypeStruct((s, A_QK), BF16), vsh, vsh, vsh]
        + [jax.ShapeDtypeStruct((3,) + p.shape[1:], p.dtype) for p in exchange],
        scratch_shapes=[pltpu.VMEM((A_HEADS, A_HEAD, A_HEAD), F32)] + (_chip_scratch(nx) if nx else []),
        compiler_params=_params(("arbitrary",)),
    )(qkv, ba, z, o_raw, dy, states, t_all, w_all, vn_all, alog, dtb, wn, *exchange)


REL_RING = 1024
QBLK_BITS = 8


def _rel_ring_onehot():
    m = lax.broadcasted_iota(jnp.int32, (REL_RING, REL_PAD), 0)
    t = lax.broadcasted_iota(jnp.int32, (REL_RING, REL_PAD), 1)
    u = jnp.where(m < KBLK, m, m - REL_RING)
    idx = jnp.clip(LEFT - u, -REL_CLIP, REL_CLIP) + REL_CLIP
    return (t == idx).astype(F32)


def _relbias_ring(table, transpose):
    n_in, n_out = (REL_RING, REL_PAD) if transpose else (REL_PAD, REL_RING)

    def body(t_ref, o_ref):
        o_ref[...] = _dot(t_ref[...], _rel_ring_onehot(), NN if transpose else NT, HIGHEST)

    return pl.pallas_call(
        body, name="relbias_ring_bwd" if transpose else "relbias_ring",
        out_shape=jax.ShapeDtypeStruct((B_HEADS, n_out), F32),
        in_specs=[VMEM_FULL], out_specs=VMEM_FULL,
        compiler_params=_params(),
    )(table)


def _row_bit(shape, bit):
    return ((lax.broadcasted_iota(jnp.int32, shape, 0) >> bit) & 1) == 1


def _relbias_expand(ring):
    def body(r_ref, o_ref):
        b = jnp.broadcast_to(r_ref[0], (QBLK, REL_RING))
        for bit in range(QBLK_BITS):
            b = jnp.where(_row_bit(b.shape, bit), pltpu.roll(b, 1 << bit, 1), b)
        j = lax.broadcasted_iota(jnp.int32, (QBLK, KBLK), 1)
        r = lax.broadcasted_iota(jnp.int32, (QBLK, KBLK), 0)
        lo = (r >> 6) << 6
        o_ref[0] = jnp.where((j >= lo) & (j < lo + LEFT + CHUNK), b[:, :KBLK], NEG_INF)

    return pl.pallas_call(
        body, name="relbias_expand", grid=(B_HEADS,),
        in_specs=[pl.BlockSpec((1, 1, REL_RING), lambda h: (h, 0, 0))],
        out_specs=pl.BlockSpec((1, QBLK, KBLK), lambda h: (h, 0, 0)),
        out_shape=jax.ShapeDtypeStruct((B_HEADS, QBLK, KBLK), F32),
        compiler_params=_params(("parallel",)),
    )(ring)


def _relbias_reduce(ds):
    def body(d_ref, o_ref):
        d = jnp.concatenate([d_ref[0], jnp.zeros((QBLK, REL_RING - KBLK), F32)], axis=1)
        for bit in range(QBLK_BITS):
            d = jnp.where(_row_bit(d.shape, bit), pltpu.roll(d, REL_RING - (1 << bit), 1), d)
        o_ref[0] = jnp.sum(d, axis=0, keepdims=True)

    return pl.pallas_call(
        body, name="relbias_reduce", grid=(B_HEADS,),
        in_specs=[pl.BlockSpec((1, QBLK, KBLK), lambda h: (h, 0, 0))],
        out_specs=pl.BlockSpec((1, 1, REL_RING), lambda h: (h, 0, 0)),
        out_shape=jax.ShapeDtypeStruct((B_HEADS, 1, REL_RING), F32),
        compiler_params=_params(("parallel",)),
    )(ds)


def _attn_probs(q_ref, kb, b_ref, hh, q0):
    hl = slice(hh * B_HEAD, (hh + 1) * B_HEAD)
    qh = q_ref[:, hl] * (B_HEAD ** -0.5)
    kh = kb[:, hl]
    jpos = lax.broadcasted_iota(jnp.int32, (QBLK, KBLK), 1)
    sc = _bdot(qh, kh, NT) + b_ref[hh]
    sc = jnp.where(jpos + q0 >= LEFT, sc, NEG_INF)
    mx = jnp.max(sc, axis=-1, keepdims=True)
    p = jnp.exp(sc - mx)
    return p * (1.0 / jnp.sum(p, axis=-1, keepdims=True)), qh, kh


def _attn_fwd(q, kpad, vpad, bias):
    s = q.shape[0]

    def body(q_ref, k_ref, v_ref, b_ref, o_ref):
        q0 = pl.multiple_of(pl.program_id(1) * QBLK, QBLK)
        kb = k_ref[pl.ds(q0, KBLK), :]
        vb = v_ref[pl.ds(q0, KBLK), :]
        outs = []
        for hh in range(2):
            p, _, _ = _attn_probs(q_ref, kb, b_ref, hh, q0)
            outs.append(_bdot(p, vb[:, hh * B_HEAD:(hh + 1) * B_HEAD], NN))
        o_ref[...] = jnp.concatenate(outs, axis=1).astype(BF16)

    qblk = pl.BlockSpec((QBLK, LANE), lambda g, m: (m, g))
    kblk = pl.BlockSpec((LEFT + s, LANE), lambda g, m: (0, g))
    return pl.pallas_call(
        body, name="attn_fwd", grid=(B_HEADS // 2, s // QBLK),
        in_specs=[qblk, kblk, kblk, pl.BlockSpec((2, QBLK, KBLK), lambda g, m: (g, 0, 0))],
        out_specs=qblk,
        out_shape=jax.ShapeDtypeStruct((s, D_MODEL), BF16),
        compiler_params=_params(("parallel", "arbitrary")),
    )(q, kpad, vpad, bias)


def _attn_bwd(q, kpad, vpad, bias, do):
    s = q.shape[0]

    def body(q_ref, k_ref, v_ref, b_ref, do_ref, dq_ref, dk_ref, dv_ref, db_ref):
        @pl.when(pl.program_id(1) == 0)
        def _():
            for r in (dk_ref, dv_ref, db_ref):
                r[...] = jnp.zeros_like(r)

        q0 = pl.multiple_of(pl.program_id(1) * QBLK, QBLK)
        kb = k_ref[pl.ds(q0, KBLK), :]
        vb = v_ref[pl.ds(q0, KBLK), :]
        dqs, dks, dvs = [], [], []
        for hh in range(2):
            hl = slice(hh * B_HEAD, (hh + 1) * B_HEAD)
            p, qh, kh = _attn_probs(q_ref, kb, b_ref, hh, q0)
            doh = do_ref[:, hl]
            dp = _bdot(doh, vb[:, hl], NT)
            dsc = p * (dp - jnp.sum(p * dp, axis=-1, keepdims=True))
            db_ref[hh] += dsc
            dqs.append(_bdot(dsc, kh, NN) * (B_HEAD ** -0.5))
            dks.append(_bdot(dsc, qh, TN))
            dvs.append(_bdot(p, doh, TN))
        dq_ref[...] = jnp.concatenate(dqs, axis=1).astype(BF16)
        dk_ref[pl.ds(q0, KBLK), :] += jnp.concatenate(dks, axis=1)
        dv_ref[pl.ds(q0, KBLK), :] += jnp.concatenate(dvs, axis=1)

    qblk = pl.BlockSpec((QBLK, LANE), lambda g, m: (m, g))
    kblk = pl.BlockSpec((LEFT + s, LANE), lambda g, m: (0, g))
    bblk = pl.BlockSpec((2, QBLK, KBLK), lambda g, m: (g, 0, 0))
    return pl.pallas_call(
        body, name="attn_bwd", grid=(B_HEADS // 2, s // QBLK),
        in_specs=[qblk, kblk, kblk, bblk, qblk],
        out_specs=[qblk, kblk, kblk, bblk],
        out_shape=[jax.ShapeDtypeStruct((s, D_MODEL), BF16), jax.ShapeDtypeStruct((LEFT + s, D_MODEL), F32),
                   jax.ShapeDtypeStruct((LEFT + s, D_MODEL), F32),
                   jax.ShapeDtypeStruct((B_HEADS, QBLK, KBLK), F32)],
        compiler_params=_params(("parallel", "arbitrary")),
    )(q, kpad, vpad, bias, do)


def _adamw(w, g, m, v):
    r, c = w.shape
    tr = r
    for cand in (512, 256, 128, 64, 32, 16, 8):
        if r % cand == 0 and cand * c * 4 <= 2 * 1024 * 1024:
            tr = cand
            break
    c1 = 1.0 / (1.0 - ADAM_B1 ** ADAM_STEP)
    c2 = 1.0 / (1.0 - ADAM_B2 ** ADAM_STEP)

    def body(w_ref, g_ref, m_ref, v_ref, d_ref, mo_ref, vo_ref):
        gv = g_ref[...]
        mn = ADAM_B1 * m_ref[...] + (1.0 - ADAM_B1) * gv
        vn = ADAM_B2 * v_ref[...] + (1.0 - ADAM_B2) * (gv * gv)
        mo_ref[...] = mn
        vo_ref[...] = vn
        d_ref[...] = -ADAM_LR * ((mn * c1) / (jnp.sqrt(vn * c2) + ADAM_EPS) + ADAM_WD * w_ref[...])

    blk = pl.BlockSpec((tr, c), lambda i: (i, 0))
    sh = jax.ShapeDtypeStruct((r, c), F32)
    return pl.pallas_call(
        body, name="adamw", grid=(r // tr,),
        in_specs=[blk] * 4, out_specs=[blk] * 3, out_shape=[sh] * 3,
        compiler_params=_params(("parallel",)),
    )(w, g, m, v)


def _row(v, width=None):
    v = v.reshape(1, -1)
    if width is not None and v.shape[1] < width:
        v = jnp.pad(v, ((0, 0), (0, width - v.shape[1])))
    return v


def _gate_row(v):
    return jnp.pad(v.reshape(1, A_HEADS), ((0, 0), (A_HEADS, LANE - 2 * A_HEADS)))


DEPTH = 4
N_A = 2
N_B = 2
F_DOWN_ROWS = FFN_DIM // N_SHARDS
SQ_ROWS = D_MODEL // N_SHARDS
UP_COLS = 2 * FFN_DIM // N_SHARDS


def _a_layer_fwd(h, w, i, pending=None):
    xn = _rmsnorm_fwd(h, _row(w["a_norm"][i]))
    w_qkv, w_z, w_ba = w["a_in"][i]
    pre = _mm(xn, w_qkv, name="a_qkv")
    z = _mm(xn, w_z, name="a_z")
    ba = _mm(xn, w_ba, name="a_ba")
    act = _conv_silu_fwd(pre, w["a_conv"][i])
    alog, dtb, wn = _gate_row(w["a_A_log"][i]), _gate_row(w["a_dt_bias"][i]), _row(w["a_out_norm"][i])
    if pending is None:
        y, o_raw, states, t_all, w_all, vn_all = _gdn_fwd(act, ba, z, alog, dtb, wn)
    else:
        names = list(pending)
        y, o_raw, states, t_all, w_all, vn_all, *landed = _gdn_fwd(act, ba, z, alog, dtb, wn,
                                                                   gather=[pending[n] for n in names])
        w.update(zip(names, landed))
    h2 = _mm_rowsh(y, w["GA"], SQ_ROWS, i, "nn", "a_out", res=h)
    saved = dict(h=h, xn=xn, pre=pre, z=z, ba=ba, act=act, o_raw=o_raw, y=y, states=states,
                 t_all=t_all, w_all=w_all, vn_all=vn_all, alog=alog, dtb=dtb, wn=wn)
    return h2, saved


def _a_layer_bwd(dh2, w, i, sv, acc, outgoing=None):
    g = {}
    acc["GA"] = _mm_rowsh_dw(sv["y"], dh2, SQ_ROWS, "a_out_dw", acc["GA"], i)
    exchange = outgoing() if outgoing is not None else ()
    dy = _mm_rowsh(dh2, w["GA"], SQ_ROWS, i, "nt", "a_out_dx")
    dact, dba, dz, dalog, ddtb, dwn, *arrived = _gdn_bwd(sv["act"], sv["ba"], sv["z"], sv["o_raw"], dy, sv["states"],
                                                         sv["t_all"], sv["w_all"], sv["vn_all"],
                                                         sv["alog"], sv["dtb"], sv["wn"], exchange=exchange)
    dpre, g["conv"] = _conv_silu_bwd(sv["pre"], w["a_conv"][i], dact)
    xn = sv["xn"]
    w_qkv, w_z, w_ba = w["a_in"][i]
    d_in = jnp.concatenate([_mm(xn, dpre, "tn", name="a_qkv_dw"), _mm(xn, dz, "tn", name="a_z_dw"),
                            _mm(xn, dba, "tn", name="a_ba_dw")[:, :2 * A_HEADS]], axis=1)
    g["w_in"] = jnp.transpose(d_in.reshape(D_MODEL, N_SHARDS, -1), (1, 0, 2))
    dxn = _mm(dpre, w_qkv, "nt", name="a_qkv_dx")
    dxn = _mm(dz, w_z, "nt", res=dxn, name="a_z_dx")
    dh, dnorm = _mm(dba, w_ba, "nt", res=dxn, name="a_ba_dx", norm=(sv["h"], _row(w["a_norm"][i]), dh2))
    g["norm"] = dnorm[0]
    g["A_log"] = dalog[0, A_HEADS:2 * A_HEADS]
    g["dt_bias"] = ddtb[0, A_HEADS:2 * A_HEADS]
    g["out_norm"] = dwn[0]
    return dh, g, arrived


def _by_half(a, lead):
    return jnp.moveaxis(a.reshape(a.shape[:-1] + (2, 2, UP_COLS)), (-3, -2), (0, 1)).reshape((2, 2) + lead + (UP_COLS,))


def _from_half(a):
    lead = a.shape[2:-1]
    return jnp.moveaxis(a, (0, 1), (-3, -2)).reshape(lead + (N_SHARDS * UP_COLS,))


def _ffn_buffers(l):
    return ("GD0", "GU0", 0) if l == 0 else ("GD1", "GU1", l - 1)


def _ffn_fwd(h, w, l):
    s = h.shape[0]
    xn = _rmsnorm_fwd(h, _row(w["f_norm"][l]))
    cw = _by_half(w["f_conv"][l], (w["f_conv"].shape[1],))
    cb = _by_half(w["f_conv_b"][l][None], (1,))
    down, up, blk = _ffn_buffers(l)
    pre = _mm_colsh(xn, w[up], D_MODEL, blk, "nn", "f_up").reshape(2, 2, s, UP_COLS)
    act = _ffn_act_fwd(pre, cw, cb)
    h2 = _mm_rowsh(act, w[down], F_DOWN_ROWS, blk, "nn", "f_down", res=h)
    return h2, dict(h=h, xn=xn, pre=pre, act=act, cw=cw, cb=cb)


def _ffn_bwd(dh2, w, l, sv, acc):
    g = {}
    s = dh2.shape[0]
    down, up, blk = _ffn_buffers(l)
    acc[down] = _mm_rowsh_dw(sv["act"], dh2, F_DOWN_ROWS, "f_down_dw", acc[down], blk)
    dact = _mm_rowsh(dh2, w[down], F_DOWN_ROWS, blk, "nt", "f_down_dx")
    dpre, dcw, dcb = _ffn_act_bwd(sv["pre"], sv["cw"], sv["cb"], dact)
    dpre = dpre.reshape(N_SHARDS, s, UP_COLS)
    acc[up] = _mm_colsh_dw(sv["xn"], dpre, "f_up_dw", into=acc[up], blk0=blk)
    g["conv"] = _from_half(dcw)
    g["conv_b"] = _from_half(dcb)[0]
    dh, dnorm = _mm_colsh(dpre, w[up], D_MODEL, blk, "nt", "f_up_dx", norm=(sv["h"], _row(w["f_norm"][l]), dh2))
    g["norm"] = dnorm[0]
    return dh, g


def _b_layer_fwd(h, w, j, kpad, vpad):
    xn = _rmsnorm_fwd(h, _row(w["b_norm"][j]))
    q = _mm_rowsh(xn, w["GB"], SQ_ROWS, j, "nn", "b_q", out_dtype=BF16)
    rel = w["b_rel_bias"][j]
    table = jnp.pad(rel, ((0, 0), (0, REL_PAD - rel.shape[1])))
    bias = _relbias_expand(_relbias_ring(table, False).reshape(B_HEADS, 1, REL_RING))
    o = _attn_fwd(q, kpad, vpad, bias)
    h2 = _mm_rowsh(o, w["GB"], SQ_ROWS, N_B + j, "nn", "b_out", res=h)
    return h2, dict(h=h, xn=xn, q=q, o=o, bias=bias)


def _b_layer_bwd(dh2, w, j, sv, kpad, vpad, acc):
    g = {}
    acc["GB"] = _mm_rowsh_dw(sv["o"], dh2, SQ_ROWS, "b_out_dw", acc["GB"], N_B + j)
    do = _mm_rowsh(dh2, w["GB"], SQ_ROWS, N_B + j, "nt", "b_out_dx", out_dtype=BF16)
    dq, dkp, dvp, dsc = _attn_bwd(sv["q"], kpad, vpad, sv["bias"], do)
    dring = _relbias_reduce(dsc).reshape(B_HEADS, REL_RING)
    g["rel_bias"] = _relbias_ring(dring, True)[:, :2 * REL_CLIP + 1]
    acc["GB"] = _mm_rowsh_dw(sv["xn"], dq, SQ_ROWS, "b_q_dw", acc["GB"], j)
    dh, dnorm = _mm_rowsh(dq, w["GB"], SQ_ROWS, j, "nt", "b_q_dx",
                          norm=(sv["h"], _row(w["b_norm"][j]), dh2))
    g["norm"] = dnorm[0]
    return dh, g, dkp, dvp


EARLY = ("GU1", "GD1", "GB", "GK")
MIDDLE = ("GU0", "GD0", "GA", "GI1")
FINAL = ("GI0",)


def _local_step(x, tgt, w, pending=None, rs_hooks=None):
    w = dict(w)
    h = x
    saved = []
    kv_saved = None
    kpad = vpad = None
    for layer in range(DEPTH):
        if layer < N_A:
            h, sm = _a_layer_fwd(h, w, layer, pending[layer] if pending else None)
        else:
            if layer == N_A:
                xn_kv = _rmsnorm_fwd(h, _row(w["kv_norm"]))
                kv = _mm_colsh(xn_kv, w["GK"], D_MODEL, 0, "nn", "kv", flat=True, out_dtype=BF16)
                kpad = jnp.pad(kv[:, :D_MODEL], ((LEFT, 0), (0, 0)))
                vpad = jnp.pad(kv[:, D_MODEL:], ((LEFT, 0), (0, 0)))
                kv_saved = dict(h=h, xn=xn_kv)
            h, sm = _b_layer_fwd(h, w, layer - N_A, kpad, vpad)
        h, sf = _ffn_fwd(h, w, layer)
        saved.append((sm, sf))

    loss, dh, dfinal = _final_loss(h, _row(w["final_norm"]), tgt)

    ga = [None] * N_A
    gb = [None] * N_B
    gf = [None] * DEPTH
    dk_tot = dv_tot = None
    g_kv = g_kvn = None
    acc = dict(GU1=lax.empty((N_SHARDS, (DEPTH - 1) * D_MODEL, UP_COLS), F32),
               GD1=lax.empty((N_SHARDS, (DEPTH - 1) * F_DOWN_ROWS, D_MODEL), F32),
               GB=lax.empty((N_SHARDS, 2 * N_B * SQ_ROWS, D_MODEL), F32),
               GU0=lax.empty((N_SHARDS, D_MODEL, UP_COLS), F32),
               GD0=lax.empty((N_SHARDS, F_DOWN_ROWS, D_MODEL), F32),
               GA=lax.empty((N_SHARDS, N_A * SQ_ROWS, D_MODEL), F32))
    reduced = {}
    for layer in reversed(range(DEPTH)):
        sm, sf = saved[layer]
        dh, gf[layer] = _ffn_bwd(dh, w, layer, sf, acc)
        if layer >= N_A:
            dh, gb[layer - N_A], dkp, dvp = _b_layer_bwd(dh, w, layer - N_A, sm, kpad, vpad, acc)
            dk_tot = dkp if dk_tot is None else dk_tot + dkp
            dv_tot = dvp if dv_tot is None else dv_tot + dvp
            if layer == N_A:
                dkv = jnp.concatenate([dk_tot[LEFT:], dv_tot[LEFT:]], axis=1).astype(BF16)
                g_kv = _mm_colsh_dw(kv_saved["xn"], dkv, "kv_dw", flat=True)
                dh, g_kvn = _mm_colsh(dkv, w["GK"], D_MODEL, 0, "nt", "kv_dx", flat=True,
                                      norm=(kv_saved["h"], _row(w["kv_norm"]), dh))
        elif rs_hooks is not None:
            prepare, finish = rs_hooks
            names = EARLY if layer == N_A - 1 else MIDDLE
            held = {}

            def outgoing(names=names, held=held):
                extra = dict(GK=g_kv, GI1=ga[N_A - 1]["w_in"] if ga[N_A - 1] else None)
                held["pairs"], partials = prepare([acc[n] if n in acc else extra[n] for n in names])
                return partials

            dh, ga[layer], arrived = _a_layer_bwd(dh, w, layer, sm, acc, outgoing)
            reduced.update(zip(names, finish(held["pairs"], arrived)))
        else:
            dh, ga[layer], _ = _a_layer_bwd(dh, w, layer, sm, acc)

    def stack(gs, key):
        return jnp.stack([g[key] for g in gs])

    grads = dict(
        acc, GK=g_kv, reduced=reduced, GI0=ga[0]["w_in"], GI1=ga[1]["w_in"],
        a_norm=stack(ga, "norm"), a_conv=stack(ga, "conv"), a_A_log=stack(ga, "A_log"),
        a_dt_bias=stack(ga, "dt_bias"), a_out_norm=stack(ga, "out_norm"), kv_norm=g_kvn[0],
        b_norm=stack(gb, "norm"), b_rel_bias=stack(gb, "rel_bias"),
        f_norm=stack(gf, "norm"), f_conv=stack(gf, "conv"), f_conv_b=stack(gf, "conv_b"), final_norm=dfinal[0])
    return loss, dh, grads


HBM_SPEC = pl.BlockSpec(memory_space=pl.ANY)
VMEM_SPEC = pl.BlockSpec(memory_space=pltpu.VMEM)


def _place():
    x, y, c = lax.axis_index("x"), lax.axis_index("y"), lax.axis_index("c")
    chips = [(1 - x, y), (x, 1 - y), (1 - x, 1 - y)]
    return x, y, c, chips


def _remote(src, dst, send_sem, recv_sem, to):
    return pltpu.make_async_remote_copy(src_ref=src, dst_ref=dst, send_sem=send_sem, recv_sem=recv_sem,
                                        device_id=to, device_id_type=MESH)


GATHER_COPIES = 7


class _ShardGather:
    def __init__(self, x_refs, out_refs, send_sems, recv_sems):
        n = len(x_refs)
        x, y, c, chips = _place()
        sibling = (x, y, 1 - c)

        def half(a, px, py, hc):
            rh = x_refs[a].shape[0] // 2
            return out_refs[a].at[2 * px + py, pl.ds(hc * rh, rh), :]

        def mine(a):
            rh = x_refs[a].shape[0] // 2
            return x_refs[a].at[pl.ds(c * rh, rh), :]

        def sems(a, k):
            return send_sems.at[GATHER_COPIES * a + k], recv_sems.at[GATHER_COPIES * a + k]

        order = [(j, chip, a) for j, chip in enumerate(chips) for a in range(n)]
        self.first = [_remote(mine(a), half(a, x, y, c), *sems(a, j), (*chip, c)) for j, chip, a in order]
        self.own = [_remote(x_refs[a], out_refs[a].at[2 * x + y], *sems(a, 6), sibling) for a in range(n)]
        self.landed = [_remote(half(a, *chip, c), half(a, *chip, c), *sems(a, j), (*chip, c)) for j, chip, a in order]
        self.passed = [_remote(half(a, *chip, c), half(a, *chip, c), *sems(a, 3 + j), sibling)
                       for j, chip, a in order]
        self.theirs = [_remote(half(a, *chip, 1 - c), half(a, *chip, 1 - c), *sems(a, 3 + j), sibling)
                       for j, chip, a in order]

    def start(self):
        for cp in self.first + self.own:
            cp.start()

    def finish(self):
        for arrived, onward in zip(self.landed, self.passed):
            arrived.wait_recv()
            onward.start()
        for cp in self.theirs + self.own:
            cp.wait_recv()
        for cp in self.first + self.passed + self.own:
            cp.wait_send()


def _gather_scratch(n):
    return [pltpu.SemaphoreType.DMA((GATHER_COPIES * n,)), pltpu.SemaphoreType.DMA((GATHER_COPIES * n,))]


def _allgather_weights(shards):
    n = len(shards)

    def body(*refs):
        gather = _ShardGather(refs[:n], refs[n:2 * n], *refs[2 * n:])
        gather.start()
        gather.finish()

    return pl.pallas_call(
        body, name="allgather_weights",
        out_shape=[jax.ShapeDtypeStruct((N_SHARDS,) + sh.shape, sh.dtype) for sh in shards],
        in_specs=[HBM_SPEC] * n, out_specs=[HBM_SPEC] * n,
        scratch_shapes=_gather_scratch(n),
    )(*shards)


def _pair_exchange(gs):
    n = len(gs)

    def body(*refs):
        g_refs, out_refs, (send_sems, recv_sems) = refs[:n], refs[n:2 * n], refs[2 * n:]
        x, y, c, _ = _place()
        cps = []
        for a in range(n):
            rh = gs[a].shape[1] // 2
            cps.append(_remote(g_refs[a].at[:, pl.ds((1 - c) * rh, rh), :], out_refs[a], send_sems.at[a],
                               recv_sems.at[a], (x, y, 1 - c)))
        for cp in cps:
            cp.start()
        for cp in cps:
            cp.wait()

    return pl.pallas_call(
        body, name="rs_pair_exchange",
        out_shape=[jax.ShapeDtypeStruct((g.shape[0], g.shape[1] // 2, g.shape[2]), g.dtype) for g in gs],
        in_specs=[HBM_SPEC] * n, out_specs=[HBM_SPEC] * n,
        scratch_shapes=[pltpu.SemaphoreType.DMA((n,)), pltpu.SemaphoreType.DMA((n,))],
    )(*gs)


def _add_rows(rows, cols):
    best = 16
    for t in range(16, rows + 1, 16):
        if rows % t == 0 and t * cols * 4 <= 2304 * 1024:
            best = t
    return best


def _pair_add(g, other, c_idx):
    n, r, cols = g.shape
    rh = r // 2
    tr = _add_rows(rh, cols)
    nb = rh // tr

    def body(c_ref, a_ref, b_ref, o_ref, ob_ref):
        sm = a_ref[...] + b_ref[...]
        o_ref[...] = sm
        ob_ref[...] = sm.astype(BF16)

    out_blk = pl.BlockSpec((1, tr, cols), lambda s, i, c_ref: (s, i, 0))
    return pl.pallas_call(
        body, name="rs_pair_add",
        grid_spec=pltpu.PrefetchScalarGridSpec(
            num_scalar_prefetch=1, grid=(n, nb),
            in_specs=[pl.BlockSpec((1, tr, cols), lambda s, i, c_ref: (s, c_ref[0] * nb + i, 0)), out_blk],
            out_specs=[out_blk, out_blk]),
        out_shape=[jax.ShapeDtypeStruct((n, rh, cols), F32), jax.ShapeDtypeStruct((n, rh, cols), BF16)],
        compiler_params=_params(("parallel", "parallel")),
    )(c_idx, g, other)


class _ChipExchange:
    def __init__(self, p_refs, out_refs, send_sems, recv_sems):
        x, y, c, chips = _place()
        self.copies = [_remote(p_refs[a].at[2 * chip[0] + chip[1]], out_refs[a].at[j], send_sems.at[3 * a + j],
                               recv_sems.at[3 * a + j], (*chip, c))
                       for a in range(len(p_refs)) for j, chip in enumerate(chips)]

    def start(self):
        for cp in self.copies:
            cp.start()

    def finish(self):
        for cp in self.copies:
            cp.wait()


def _chip_scratch(n):
    return [pltpu.SemaphoreType.DMA((3 * n,)), pltpu.SemaphoreType.DMA((3 * n,))]


def _chip_exchange(ps):
    n = len(ps)

    def body(*refs):
        exchange = _ChipExchange(refs[:n], refs[n:2 * n], *refs[2 * n:])
        exchange.start()
        exchange.finish()

    return pl.pallas_call(
        body, name="rs_chip_exchange",
        out_shape=[jax.ShapeDtypeStruct((3,) + p.shape[1:], p.dtype) for p in ps],
        in_specs=[HBM_SPEC] * n, out_specs=[HBM_SPEC] * n,
        scratch_shapes=_chip_scratch(n),
    )(*ps)


def _chip_add(p, recv, chip_idx):
    n, rh, cols = p.shape
    tr = _add_rows(rh, cols)

    def body(s_ref, own_ref, r_ref, o_ref):
        o_ref[...] = ((own_ref[0] + r_ref[0].astype(F32)) + r_ref[1].astype(F32)) + r_ref[2].astype(F32)

    return pl.pallas_call(
        body, name="rs_chip_add",
        grid_spec=pltpu.PrefetchScalarGridSpec(
            num_scalar_prefetch=1, grid=(rh // tr,),
            in_specs=[pl.BlockSpec((1, tr, cols), lambda i, s_ref: (s_ref[0], i, 0)),
                      pl.BlockSpec((3, tr, cols), lambda i, s_ref: (0, i, 0))],
            out_specs=pl.BlockSpec((tr, cols), lambda i, s_ref: (i, 0))),
        out_shape=jax.ShapeDtypeStruct((rh, cols), p.dtype),
        compiler_params=_params(("parallel",)),
    )(chip_idx, p, recv)


def _pair_gather(fs):
    n = len(fs)

    def body(*refs):
        f_refs, out_refs, (send_sems, recv_sems) = refs[:n], refs[n:2 * n], refs[2 * n:]
        x, y, c, _ = _place()
        cps = [_remote(f_refs[a], out_refs[a], send_sems.at[a], recv_sems.at[a], (x, y, 1 - c)) for a in range(n)]
        for cp in cps:
            cp.start()
        for cp in cps:
            cp.wait()

    return pl.pallas_call(
        body, name="rs_pair_gather",
        out_shape=[jax.ShapeDtypeStruct(f.shape, f.dtype) for f in fs],
        in_specs=[HBM_SPEC] * n, out_specs=[HBM_SPEC] * n,
        scratch_shapes=[pltpu.SemaphoreType.DMA((n,)), pltpu.SemaphoreType.DMA((n,))],
    )(*fs)


def _allreduce_small(v):
    r, cols = v.shape

    def body(x_ref, out_ref, slots, send_sems, recv_sems):
        x, y, c, _ = _place()
        bits = [(bx, by, bc) for bx in (0, 1) for by in (0, 1) for bc in (0, 1)]

        def flip(b):
            return (1 - x if b[0] else x, 1 - y if b[1] else y, 1 - c if b[2] else c)

        slots[0] = x_ref[...]
        cps = [_remote(x_ref, slots.at[k], send_sems.at[k - 1], recv_sems.at[k - 1], flip(bits[k]))
               for k in range(1, 8)]
        for cp in cps:
            cp.start()
        for cp in cps:
            cp.wait()
        acc = None
        for b in bits:
            fx, fy, fc = flip(b)
            term = slots[4 * fx + 2 * fy + fc]
            acc = term if acc is None else acc + term
        out_ref[...] = acc

    return pl.pallas_call(
        body, name="allreduce_small",
        out_shape=jax.ShapeDtypeStruct((r, cols), v.dtype),
        in_specs=[VMEM_SPEC], out_specs=VMEM_SPEC,
        scratch_shapes=[pltpu.VMEM((8, r, cols), v.dtype), pltpu.SemaphoreType.DMA((7,)),
                        pltpu.SemaphoreType.DMA((7,))],
        compiler_params=pltpu.CompilerParams(vmem_limit_bytes=VMEM_LIMIT),
    )(v)


BIG = (("a_w_in", 2), ("a_w_out", 1), ("w_kv", 1), ("b_w_q", 1), ("b_w_out", 1), ("f_w_up", 2), ("f_w_down", 1))
SMALL = (("a_norm", 1), ("a_conv", 2), ("a_A_log", None), ("a_dt_bias", None), ("a_out_norm", None),
         ("kv_norm", None), ("b_norm", None), ("b_rel_bias", None), ("f_norm", None), ("f_conv", 2),
         ("f_conv_b", None), ("final_norm", None))
WEIGHT_ORDER = ("a_norm", "a_w_in", "a_conv", "a_A_log", "a_dt_bias", "a_out_norm", "a_w_out", "kv_norm", "w_kv",
                "b_norm", "b_w_q", "b_rel_bias", "b_w_out", "f_norm", "f_w_up", "f_conv", "f_conv_b", "f_w_down",
                "final_norm")


def _pad_rows(flat, cols, quantum):
    n = flat.shape[-1]
    rows = -(-n // (cols * quantum)) * quantum
    pad = [(0, 0)] * (flat.ndim - 1) + [(0, rows * cols - n)]
    return jnp.pad(flat, pad).reshape(flat.shape[:-1] + (rows, cols))


FIRST_GROUPS = ("GI", "GA")
RIDING_GROUPS = (("GU0", "GD0"), ("GU1", "GD1", "GB", "GK"))
GROUPS = FIRST_GROUPS + RIDING_GROUPS[0] + RIDING_GROUPS[1]


def _group_shards(w, dtype):
    def two(a):
        return a.reshape(-1, a.shape[-1]).astype(dtype)

    return dict(GU0=two(w["f_w_up"][:1]), GU1=two(w["f_w_up"][1:]), GD0=two(w["f_w_down"][:1]),
                GD1=two(w["f_w_down"][1:]), GB=jnp.concatenate([two(w["b_w_q"]), two(w["b_w_out"])]),
                GK=w["w_kv"].astype(dtype), GI=two(w["a_w_in"]), GA=two(w["a_w_out"]))


def _ungroup(red, shard_shapes):
    b_rows = N_B * SQ_ROWS
    flat = dict(f_w_up=jnp.concatenate([red["GU0"], red["GU1"]]), f_w_down=jnp.concatenate([red["GD0"], red["GD1"]]),
                b_w_q=red["GB"][:b_rows], b_w_out=red["GB"][b_rows:], a_w_out=red["GA"], w_kv=red["GK"],
                a_w_in=jnp.concatenate([red["GI0"], red["GI1"]]))
    return {n: v.reshape(shard_shapes[n]) for n, v in flat.items()}


def _dense_a_in(gi):
    out = []
    for i in range(N_A):
        full = jnp.transpose(gi[:, i * D_MODEL:(i + 1) * D_MODEL], (1, 0, 2)).reshape(D_MODEL, -1)
        out.append((full[:, :A_CONV_WIDTH], full[:, A_CONV_WIDTH:A_CONV_WIDTH + A_QK],
                    jnp.pad(full[:, A_CONV_WIDTH + A_QK:], ((0, 0), (0, LANE - 2 * A_HEADS)))))
    return out


def _pack_small(values, names):
    return _pad_rows(jnp.concatenate([values[n].reshape(-1) for n in names]), LANE, SUB)


def _unpack_small(packed, shapes, names):
    flat = packed.reshape(-1)
    out, off = {}, 0
    for n in names:
        size = math.prod(shapes[n])
        out[n] = flat[off:off + size].reshape(shapes[n])
        off += size
    return out


def _adamw_nd(w, g, m, v):
    shp = w.shape
    two = (math.prod(shp[:-1]), shp[-1])
    d, mn, vn = _adamw(w.reshape(two), g.reshape(two), m.reshape(two), v.reshape(two))
    return d.reshape(shp), mn.reshape(shp), vn.reshape(shp)


def kernel(x, a_norm, a_w_in, a_conv, a_A_log, a_dt_bias, a_out_norm, a_w_out, kv_norm, w_kv, b_norm, b_w_q, b_rel_bias, b_w_out, f_norm, f_w_up, f_conv, f_conv_b, f_w_down, final_norm, loss_target, m_a_norm, m_a_w_in, m_a_conv, m_a_A_log, m_a_dt_bias, m_a_out_norm, m_a_w_out, m_kv_norm, m_w_kv, m_b_norm, m_b_w_q, m_b_rel_bias, m_b_w_out, m_f_norm, m_f_w_up, m_f_conv, m_f_conv_b, m_f_w_down, m_final_norm, v_a_norm, v_a_w_in, v_a_conv, v_a_A_log, v_a_dt_bias, v_a_out_norm, v_a_w_out, v_kv_norm, v_w_kv, v_b_norm, v_b_w_q, v_b_rel_bias, v_b_w_out, v_f_norm, v_f_w_up, v_f_conv, v_f_conv_b, v_f_w_down, v_final_norm):
    w = dict(a_norm=a_norm, a_w_in=a_w_in, a_conv=a_conv, a_A_log=a_A_log, a_dt_bias=a_dt_bias,
             a_out_norm=a_out_norm, a_w_out=a_w_out, kv_norm=kv_norm, w_kv=w_kv, b_norm=b_norm, b_w_q=b_w_q,
             b_rel_bias=b_rel_bias, b_w_out=b_w_out, f_norm=f_norm, f_w_up=f_w_up, f_conv=f_conv,
             f_conv_b=f_conv_b, f_w_down=f_w_down, final_norm=final_norm)
    m = dict(a_norm=m_a_norm, a_w_in=m_a_w_in, a_conv=m_a_conv, a_A_log=m_a_A_log, a_dt_bias=m_a_dt_bias,
             a_out_norm=m_a_out_norm, a_w_out=m_a_w_out, kv_norm=m_kv_norm, w_kv=m_w_kv, b_norm=m_b_norm,
             b_w_q=m_b_w_q, b_rel_bias=m_b_rel_bias, b_w_out=m_b_w_out, f_norm=m_f_norm, f_w_up=m_f_w_up,
             f_conv=m_f_conv, f_conv_b=m_f_conv_b, f_w_down=m_f_w_down, final_norm=m_final_norm)
    v = dict(a_norm=v_a_norm, a_w_in=v_a_w_in, a_conv=v_a_conv, a_A_log=v_a_A_log, a_dt_bias=v_a_dt_bias,
             a_out_norm=v_a_out_norm, a_w_out=v_a_w_out, kv_norm=v_kv_norm, w_kv=v_w_kv, b_norm=v_b_norm,
             b_w_q=v_b_w_q, b_rel_bias=v_b_rel_bias, b_w_out=v_b_w_out, f_norm=v_f_norm, f_w_up=v_f_w_up,
             f_conv=v_f_conv, f_conv_b=v_f_conv_b, f_w_down=v_f_w_down, final_norm=v_final_norm)
    xi, yi, ci = lax.axis_index("x"), lax.axis_index("y"), lax.axis_index("c")
    chip = 2 * xi + yi
    shard_shapes = {n: w[n].shape for n in WEIGHT_ORDER}

    mine_w = _group_shards(w, BF16)
    full = dict(zip(FIRST_GROUPS, _allgather_weights([mine_w[n] for n in FIRST_GROUPS])))
    full["a_in"] = _dense_a_in(full.pop("GI"))
    pending = [{n: mine_w[n] for n in names} for names in RIDING_GROUPS]
    sharded_small = [n for n, axis in SMALL if axis is not None]
    placed = {}
    for n, axis in SMALL:
        if axis is not None:
            wide = list(w[n].shape)
            wide[axis] *= 4
            mine_once = w[n] * (1 - ci).astype(F32)
            placed[n] = lax.dynamic_update_slice_in_dim(jnp.zeros(wide, F32), mine_once, chip * w[n].shape[axis], axis)
    placed_shapes = {n: placed[n].shape for n in sharded_small}
    full.update(_unpack_small(_allreduce_small(_pack_small(placed, sharded_small)), placed_shapes, sharded_small))
    for n, axis in SMALL:
        if axis is None:
            full[n] = w[n]

    c_idx = jnp.reshape(ci, (1,)).astype(jnp.int32)
    chip_idx = jnp.reshape(chip, (1,)).astype(jnp.int32)

    def pair_stage(gs):
        pairs = [_pair_add(g, o, c_idx) for g, o in zip(gs, _pair_exchange(gs))]
        return pairs, [pb for _, pb in pairs]

    def chip_stage(pairs, arrived):
        mine = [_chip_add(p, r, chip_idx) for (p, _), r in zip(pairs, arrived)]
        theirs = _pair_gather(mine)
        return [jnp.concatenate([jnp.where(ci == 0, a, b), jnp.where(ci == 0, b, a)], axis=0)
                for a, b in zip(mine, theirs)]

    loss_part, grad_x, grads = _local_step(x[0], loss_target[0], full, pending, (pair_stage, chip_stage))
    last_pairs, outgoing = pair_stage([grads[n] for n in FINAL])
    last = chip_stage(last_pairs, _chip_exchange(outgoing))
    red = _ungroup(dict(grads["reduced"], **dict(zip(FINAL, last))), shard_shapes)

    small_names = [n for n, _ in SMALL]
    small_vals = {n: grads[n] for n in small_names}
    small_vals["loss"] = loss_part[0, :1]
    names = ["loss"] + small_names
    shapes = {n: small_vals[n].shape for n in names}
    summed = _unpack_small(_allreduce_small(_pack_small(small_vals, names)), shapes, names)
    loss = summed["loss"][0]
    for n, axis in SMALL:
        g = summed[n]
        if axis is not None:
            g = lax.dynamic_slice_in_dim(g, chip * w[n].shape[axis], w[n].shape[axis], axis)
        red[n] = g

    delta, new_m, new_v = {}, {}, {}
    for n, _ in BIG:
        delta[n], new_m[n], new_v[n] = _adamw_nd(w[n], red[n], m[n], v[n])
    local_shapes = {n: w[n].shape for n in small_names}
    packs = [_pack_small(t, small_names) for t in (w, red, m, v)]
    outs = _adamw(*packs)
    ds, ms, vs = (_unpack_small(o, local_shapes, small_names) for o in outs)
    delta.update(ds)
    new_m.update(ms)
    new_v.update(vs)

    return (loss, grad_x[None], *[red[n] for n in WEIGHT_ORDER], *[delta[n] for n in WEIGHT_ORDER],
            *[new_m[n] for n in WEIGHT_ORDER], *[new_v[n] for n in WEIGHT_ORDER])
```

```python
import math

import jax
import jax.numpy as jnp
from jax import lax
from jax.experimental import pallas as pl
from jax.experimental.pallas import tpu as pltpu

F32 = jnp.float32
BF16 = jnp.bfloat16
HIGHEST = lax.Precision.HIGHEST
MESH = pl.DeviceIdType.MESH

D_MODEL = 1024
CHUNK = 64
A_HEADS = 8
A_HEAD = 128
A_QK = A_HEADS * A_HEAD
A_CONV_WIDTH = 3 * A_QK
B_HEADS = 16
B_HEAD = 64
LEFT = 8 * CHUNK
QBLK = 4 * CHUNK
KBLK = LEFT + QBLK
REL_CLIP = 256
REL_PAD = 640
FFN_DIM = 2816
EPS = 1e-6
NEG_INF = -1e30
LANE = 128
SUB = 8
VMEM_LIMIT = 56 * 1024 * 1024

ADAM_LR = 0.001
ADAM_B1 = 0.9
ADAM_B2 = 0.999
ADAM_EPS = 1e-08
ADAM_WD = 0.01
ADAM_STEP = 10


VMEM_FULL = pl.BlockSpec(memory_space=pltpu.VMEM)


def _params(sem=None):
    return pltpu.CompilerParams(dimension_semantics=sem, vmem_limit_bytes=VMEM_LIMIT)


def _tile(n, cap):
    if n <= cap:
        return n
    best = None
    for t in range(LANE, cap + 1, LANE):
        if n % t == 0:
            best = t
    assert best is not None, n
    return best


def _sigmoid(x):
    return 0.5 * jnp.tanh(0.5 * x) + 0.5


def _softplus(x):
    return jnp.maximum(x, 0.0) + jnp.log(1.0 + jnp.exp(-jnp.abs(x)))


def _dot(a, b, dims, prec=None):
    return lax.dot_general(a, b, (dims, ((), ())), preferred_element_type=F32, precision=prec)


NN = ((1,), (0,))
NT = ((1,), (1,))
TN = ((0,), (0,))


def _bdot(a, b, dims):
    return _dot(a.astype(BF16), b.astype(BF16), dims)


def _mm(a, b, mode="nn", out_dtype=F32, res=None, name="mm", norm=None):
    if mode == "nn":
        (m, k), (k2, n) = a.shape, b.shape
    elif mode == "nt":
        (m, k), (n, k2) = a.shape, b.shape
    else:
        (k, m), (k2, n) = a.shape, b.shape
    assert k == k2, (a.shape, b.shape, mode)
    tm, tn, tk = _tile(m, 1408), _tile(n, 1408), _tile(k, 1408)
    if m == 8192:
        tm = 1024
    if k == 8192:
        tk = 1024
    nk = k // tk
    dims = {"nn": NN, "nt": NT, "tn": TN}[mode]
    a_spec = {"nn": pl.BlockSpec((tm, tk), lambda i, j, kk: (i, kk)),
              "nt": pl.BlockSpec((tm, tk), lambda i, j, kk: (i, kk)),
              "tn": pl.BlockSpec((tk, tm), lambda i, j, kk: (kk, i))}[mode]
    b_spec = {"nn": pl.BlockSpec((tk, tn), lambda i, j, kk: (kk, j)),
              "nt": pl.BlockSpec((tn, tk), lambda i, j, kk: (j, kk)),
              "tn": pl.BlockSpec((tk, tn), lambda i, j, kk: (kk, j))}[mode]
    o_spec = pl.BlockSpec((tm, tn), lambda i, j, kk: (i, j))
    return _mm_call(name, a, b, dims, (m // tm, n // tn, nk), a_spec, b_spec, o_spec, (m, n), out_dtype, (tm, tn),
                    res, norm)


ROW_TILE = 1024
N_SHARDS = 4


def _mm_call(name, a, b, dims, grid, a_spec, b_spec, o_spec, out_shape, out_dtype, acc_shape, res=None, norm=None,
             into=None):
    nk = grid[2]
    has_res = res is not None
    has_norm = norm is not None
    has_into = into is not None
    if has_norm:
        assert grid[1] == 1 and len(out_shape) == 2 and out_dtype == F32

    def flat(v):
        return v.reshape(-1, v.shape[-1]) if v.ndim == 3 else v

    def body(a_ref, b_ref, *rest):
        rest = list(rest)
        res_ref = rest.pop(0) if has_res else None
        x_ref, g_ref, dres_ref = (rest.pop(0), rest.pop(0), rest.pop(0)) if has_norm else (None, None, None)
        if has_into:
            rest.pop(0)
        o_ref = rest.pop(0)
        dg_ref = rest.pop(0) if has_norm else None
        acc = rest.pop(0)
        kk = pl.program_id(2)
        first_rows = pl.program_id(0) == 0

        @pl.when(kk == 0)
        def _():
            acc[...] = jnp.zeros_like(acc)

        if has_norm:
            @pl.when(first_rows & (kk == 0))
            def _():
                dg_ref[...] = jnp.zeros_like(dg_ref)

        acc[...] += _bdot(flat(a_ref[...]), flat(b_ref[...]), dims)

        @pl.when(kk == nk - 1)
        def _():
            r = acc[...]
            if has_res:
                r = r + res_ref[...]
            if not has_norm:
                o_ref[...] = r.reshape(o_ref.shape).astype(out_dtype)
                return
            xv = x_ref[...]
            rs = lax.rsqrt(jnp.mean(xv * xv, axis=-1, keepdims=True) + EPS)
            t = r * g_ref[...]
            c = jnp.mean(t * xv, axis=-1, keepdims=True)
            o_ref[...] = dres_ref[...] + rs * t - xv * (rs * rs * rs) * c
            dg_ref[...] += jnp.sum(r * xv * rs, axis=0, keepdims=True)

    args = [a, b] + ([res] if has_res else [])
    in_specs = [a_spec, b_spec] + ([o_spec] if has_res else [])
    out_specs, out_shapes = o_spec, jax.ShapeDtypeStruct(out_shape, out_dtype)
    if has_norm:
        vec = pl.BlockSpec((1, out_shape[1]), lambda i, j, kk: (0, 0))
        args += list(norm)
        in_specs += [o_spec, vec, o_spec]
        out_specs = [o_spec, vec]
        out_shapes = [out_shapes, jax.ShapeDtypeStruct((1, out_shape[1]), F32)]
    aliases = {}
    if has_into:
        assert not has_norm and into.dtype == out_dtype
        aliases = {len(args): 0}
        args.append(into)
        in_specs.append(pl.BlockSpec(memory_space=pl.ANY))
        out_shapes = jax.ShapeDtypeStruct(into.shape, out_dtype)
    return pl.pallas_call(
        body, name=name, grid=grid, in_specs=in_specs, out_specs=out_specs, out_shape=out_shapes,
        scratch_shapes=[pltpu.VMEM(acc_shape, F32)], input_output_aliases=aliases,
        compiler_params=_params(("arbitrary" if has_norm else "parallel", "parallel", "arbitrary")),
    )(*args)


def _shards_per_block(rows):
    return N_SHARDS if N_SHARDS * rows <= 1408 else 2


def _mm_rowsh(a, buf, rows, blk0, mode, name, res=None, out_dtype=F32, norm=None):
    s = a.shape[0]
    cols = buf.shape[2]
    g = _shards_per_block(rows)
    tm = _tile(s, ROW_TILE)
    b_blk = (g, rows, cols)
    if mode == "nn":
        return _mm_call(name, a, buf, NN, (s // tm, 1, N_SHARDS // g),
                        pl.BlockSpec((tm, g * rows), lambda i, j, kk: (i, kk)),
                        pl.BlockSpec(b_blk, lambda i, j, kk: (kk, blk0, 0)),
                        pl.BlockSpec((tm, cols), lambda i, j, kk: (i, 0)),
                        (s, cols), out_dtype, (tm, cols), res)
    return _mm_call(name, a, buf, NT, (s // tm, N_SHARDS // g, 1),
                    pl.BlockSpec((tm, cols), lambda i, j, kk: (i, 0)),
                    pl.BlockSpec(b_blk, lambda i, j, kk: (j, blk0, 0)),
                    pl.BlockSpec((tm, g * rows), lambda i, j, kk: (i, j)),
                    (s, N_SHARDS * rows), out_dtype, (tm, g * rows), res, norm)


def _mm_rowsh_dw(act, dy, rows, name, into, blk0):
    s = act.shape[0]
    cols = dy.shape[1]
    g = _shards_per_block(rows)
    ts = _tile(s, ROW_TILE)
    return _mm_call(name, act, dy, TN, (1, N_SHARDS // g, s // ts),
                    pl.BlockSpec((ts, g * rows), lambda i, j, kk: (kk, j)),
                    pl.BlockSpec((ts, cols), lambda i, j, kk: (kk, 0)),
                    pl.BlockSpec((g, rows, cols), lambda i, j, kk: (j, blk0, 0)),
                    into.shape, F32, (g * rows, cols), into=into)


def _mm_colsh(a, buf, krows, blk0, mode, name, flat=False, res=None, out_dtype=F32, norm=None):
    cols = buf.shape[2]
    b_nn = pl.BlockSpec((None, krows, cols), lambda i, j, kk: (j, blk0, 0))
    b_nt = pl.BlockSpec((None, krows, cols), lambda i, j, kk: (kk, blk0, 0))
    if mode == "nn":
        s = a.shape[0]
        tm = _tile(s, ROW_TILE)
        o_spec = (pl.BlockSpec((tm, cols), lambda i, j, kk: (i, j)) if flat
                  else pl.BlockSpec((None, tm, cols), lambda i, j, kk: (j, i, 0)))
        return _mm_call(name, a, buf, NN, (s // tm, N_SHARDS, 1),
                        pl.BlockSpec((tm, krows), lambda i, j, kk: (i, 0)), b_nn, o_spec,
                        (s, N_SHARDS * cols) if flat else (N_SHARDS, s, cols), out_dtype, (tm, cols), res)
    s = a.shape[0] if flat else a.shape[1]
    tm = _tile(s, ROW_TILE)
    a_spec = (pl.BlockSpec((tm, cols), lambda i, j, kk: (i, kk)) if flat
              else pl.BlockSpec((None, tm, cols), lambda i, j, kk: (kk, i, 0)))
    return _mm_call(name, a, buf, NT, (s // tm, 1, N_SHARDS), a_spec, b_nt,
                    pl.BlockSpec((tm, krows), lambda i, j, kk: (i, 0)),
                    (s, krows), out_dtype, (tm, krows), res, norm)


def _mm_colsh_dw(x, dy, name, flat=False, into=None, blk0=0):
    s, k = x.shape
    cols = dy.shape[1] // N_SHARDS if flat else dy.shape[2]
    ts = _tile(s, ROW_TILE)
    b_spec = (pl.BlockSpec((ts, cols), lambda i, j, kk: (kk, j)) if flat
              else pl.BlockSpec((None, ts, cols), lambda i, j, kk: (j, kk, 0)))
    return _mm_call(name, x, dy, TN, (1, N_SHARDS, s // ts),
                    pl.BlockSpec((ts, k), lambda i, j, kk: (kk, 0)), b_spec,
                    pl.BlockSpec((None, k, cols), lambda i, j, kk: (j, blk0, 0)),
                    (N_SHARDS, k, cols) if into is None else into.shape, F32, (k, cols), into=into)


def _rmsnorm_fwd(x, g):
    s, d = x.shape
    tr = _tile(s, 1024)

    def body(x_ref, g_ref, o_ref):
        xv = x_ref[...]
        r = lax.rsqrt(jnp.mean(xv * xv, axis=-1, keepdims=True) + EPS)
        o_ref[...] = (xv * r * g_ref[...]).astype(BF16)

    return pl.pallas_call(
        body, name="rmsnorm_fwd", grid=(s // tr,),
        in_specs=[pl.BlockSpec((tr, d), lambda i: (i, 0)), pl.BlockSpec((1, d), lambda i: (0, 0))],
        out_specs=pl.BlockSpec((tr, d), lambda i: (i, 0)),
        out_shape=jax.ShapeDtypeStruct((s, d), BF16),
        compiler_params=_params(("parallel",)),
    )(x, g)


def _final_loss(h, g, tgt):
    s, d = h.shape
    tr = _tile(s, 1024)

    def body(x_ref, g_ref, t_ref, loss_ref, dx_ref, dg_ref):
        @pl.when(pl.program_id(0) == 0)
        def _():
            dg_ref[...] = jnp.zeros_like(dg_ref)
            loss_ref[...] = jnp.zeros_like(loss_ref)

        xv = x_ref[...]
        r = lax.rsqrt(jnp.mean(xv * xv, axis=-1, keepdims=True) + EPS)
        xh = xv * r
        err = xh * g_ref[...] - t_ref[...]
        per_row = jnp.mean(err * err, axis=-1, keepdims=True)
        loss_ref[...] += 0.5 * jnp.sum(per_row, axis=0, keepdims=True)
        dy = err * (1.0 / d)
        t = dy * g_ref[...]
        c = jnp.mean(t * xv, axis=-1, keepdims=True)
        dx_ref[...] = r * t - xv * (r * r * r) * c
        dg_ref[...] += jnp.sum(dy * xh, axis=0, keepdims=True)

    row = pl.BlockSpec((tr, d), lambda i: (i, 0))
    vec = pl.BlockSpec((1, d), lambda i: (0, 0))
    return pl.pallas_call(
        body, name="final_loss", grid=(s // tr,),
        in_specs=[row, vec, row],
        out_specs=[pl.BlockSpec((1, LANE), lambda i: (0, 0)), row, vec],
        out_shape=[jax.ShapeDtypeStruct((1, LANE), F32), jax.ShapeDtypeStruct((s, d), F32),
                   jax.ShapeDtypeStruct((1, d), F32)],
        compiler_params=_params(("arbitrary",)),
    )(h, g, tgt)


CONV_ROWS = 512
CONV_COLS = 1408
CONV_CHUNK = 64


def _per_lane_tile(tile_body):
    def body(*refs):
        for lt in range(refs[0].shape[-1] // LANE):
            cols = slice(lt * LANE, (lt + 1) * LANE)
            tile_body(*[r.at[(slice(None),) * (len(r.shape) - 1) + (cols,)] for r in refs])
    return body


def _lagged(window, lag):
    return (pltpu.roll(window, lag, 0) if lag else window)[SUB:]


def _led(window, lead):
    n = window.shape[0] - SUB
    return (pltpu.roll(window, window.shape[0] - lead, 0) if lead else window)[:n]


def _fold_rows(v):
    return jnp.sum(v.reshape(v.shape[0] // SUB, SUB, v.shape[1]), axis=0)


def _taps(shifted, w):
    acc = None
    for k, xs in enumerate(shifted):
        term = xs * w[k:k + 1, :]
        acc = term if acc is None else acc + term
    return acc


def _conv_tiles(s, c):
    return _tile(s, CONV_ROWS), _tile(c, CONV_COLS)


def _conv_silu_fwd(pre, w):
    s, c = pre.shape
    width = w.shape[0]
    tr, tc = _conv_tiles(s, c)

    def body(x_ref, w_ref, o_ref, tail):
        @pl.when(pl.program_id(1) == 0)
        def _():
            tail[...] = jnp.zeros_like(tail)

        wv = w_ref[...]

        def do(c0, window):
            y = _taps([_lagged(window, width - 1 - k) for k in range(width)], wv)
            o_ref[pl.ds(c0, CONV_CHUNK), :] = y * _sigmoid(y)

        def chunk(ci, carry):
            c0 = pl.multiple_of(ci * CONV_CHUNK, CONV_CHUNK)
            do(c0, x_ref[pl.ds(pl.multiple_of(c0 - SUB, SUB), CONV_CHUNK + SUB), :])
            return carry

        do(0, jnp.concatenate([tail[...], x_ref[:CONV_CHUNK, :]], axis=0))
        lax.fori_loop(1, tr // CONV_CHUNK, chunk, 0)
        tail[...] = x_ref[tr - SUB:, :]

    blk = pl.BlockSpec((tr, tc), lambda j, i: (i, j))
    return pl.pallas_call(
        _per_lane_tile(body), name="conv_silu_fwd", grid=(c // tc, s // tr),
        in_specs=[blk, pl.BlockSpec((width, tc), lambda j, i: (0, j))], out_specs=blk,
        out_shape=jax.ShapeDtypeStruct((s, c), F32),
        scratch_shapes=[pltpu.VMEM((SUB, tc), F32)],
        compiler_params=_params(("parallel", "arbitrary")),
    )(pre, w)


def _prev_rows_index(i_blk, tr):
    return jnp.maximum(i_blk * (tr // SUB) - 1, 0)


def _conv_silu_bwd(pre, w, dact):
    s, c = pre.shape
    width = w.shape[0]
    tr, tc = _conv_tiles(s, c)
    nr = s // tr

    nchunks = tr // CONV_CHUNK

    def body(x_ref, p_ref, w_ref, d_ref, dx_ref, dw_ref, head):
        @pl.when(pl.program_id(1) == 0)
        def _():
            head[...] = jnp.zeros_like(head)
            dw_ref[...] = jnp.zeros_like(dw_ref)

        wv = w_ref[...]

        def do(c0, window, later, dws):
            xs = [_lagged(window, width - 1 - k) for k in range(width)]
            y = _taps(xs, wv)
            sg = _sigmoid(y)
            dy = d_ref[pl.ds(c0, CONV_CHUNK), :] * sg * (1.0 + y * (1.0 - sg))
            dws = tuple(dw + _fold_rows(dy * x) for dw, x in zip(dws, xs))
            ahead = jnp.concatenate([dy, later], axis=0)
            dx_ref[pl.ds(c0, CONV_CHUNK), :] = _taps([_led(ahead, width - 1 - k) for k in range(width)],
                                                     wv).astype(BF16)
            return dy[:SUB], dws

        def chunk(it, carry):
            c0 = pl.multiple_of((nchunks - 1 - it) * CONV_CHUNK, CONV_CHUNK)
            return do(c0, x_ref[pl.ds(pl.multiple_of(c0 - SUB, SUB), CONV_CHUNK + SUB), :], *carry)

        zero = jnp.zeros((SUB, LANE), F32)
        carry = lax.fori_loop(0, nchunks - 1, chunk, (head[...], (zero,) * width))
        before = jnp.where(pl.program_id(1) == nr - 1, 0.0, p_ref[...])
        later, dws = do(0, jnp.concatenate([before, x_ref[:CONV_CHUNK, :]], axis=0), *carry)
        head[...] = later
        for k in range(width):
            dw_ref[k:k + 1, :] += jnp.sum(dws[k], axis=0, keepdims=True)

    blk = pl.BlockSpec((tr, tc), lambda j, i: (nr - 1 - i, j))
    prev = pl.BlockSpec((SUB, tc), lambda j, i: (_prev_rows_index(nr - 1 - i, tr), j))
    wblk = pl.BlockSpec((width, tc), lambda j, i: (0, j))
    return pl.pallas_call(
        _per_lane_tile(body), name="conv_silu_bwd", grid=(c // tc, nr),
        in_specs=[blk, prev, wblk, blk], out_specs=[blk, wblk],
        out_shape=[jax.ShapeDtypeStruct((s, c), BF16), jax.ShapeDtypeStruct((width, c), F32)],
        scratch_shapes=[pltpu.VMEM((SUB, tc), F32)],
        compiler_params=_params(("parallel", "arbitrary")),
    )(pre, pre, w, dact)


def _ffn_act_fwd(pre, w, b):
    _, halves, s, c = pre.shape
    width = w.shape[2]
    tr, tc = _conv_tiles(s, c)
    ncb = c // tc

    def body(x_ref, w_ref, b_ref, o_ref, tail):
        @pl.when(pl.program_id(2) == 0)
        def _():
            tail[...] = jnp.zeros_like(tail)

        wg, wv, bg, bv = w_ref[0], w_ref[1], b_ref[0], b_ref[1]

        def do(c0, win_g, win_v):
            yg = _taps([_lagged(win_g, width - 1 - k) for k in range(width)], wg) + bg
            yv = _taps([_lagged(win_v, width - 1 - k) for k in range(width)], wv) + bv
            o_ref[pl.ds(c0, CONV_CHUNK), :] = (yg * _sigmoid(yg) * yv).astype(BF16)

        def chunk(ci, carry):
            c0 = pl.multiple_of(ci * CONV_CHUNK, CONV_CHUNK)
            rows = pl.ds(pl.multiple_of(c0 - SUB, SUB), CONV_CHUNK + SUB)
            do(c0, x_ref[0, rows, :], x_ref[1, rows, :])
            return carry

        do(0, jnp.concatenate([tail[0], x_ref[0, :CONV_CHUNK, :]], axis=0),
           jnp.concatenate([tail[1], x_ref[1, :CONV_CHUNK, :]], axis=0))
        lax.fori_loop(1, tr // CONV_CHUNK, chunk, 0)
        tail[...] = x_ref[:, tr - SUB:, :]

    return pl.pallas_call(
        _per_lane_tile(body), name="ffn_act_fwd", grid=(halves, ncb, s // tr),
        in_specs=[pl.BlockSpec((2, None, tr, tc), lambda h, j, i: (0, h, i, j)),
                  pl.BlockSpec((2, None, width, tc), lambda h, j, i: (0, h, 0, j)),
                  pl.BlockSpec((2, None, 1, tc), lambda h, j, i: (0, h, 0, j))],
        out_specs=pl.BlockSpec((tr, tc), lambda h, j, i: (i, h * ncb + j)),
        out_shape=jax.ShapeDtypeStruct((s, halves * c), BF16),
        scratch_shapes=[pltpu.VMEM((2, SUB, tc), F32)],
        compiler_params=_params(("parallel", "parallel", "arbitrary")),
    )(pre, w, b)


def _ffn_act_bwd(pre, w, b, dact):
    _, halves, s, c = pre.shape
    width = w.shape[2]
    tr, tc = _conv_tiles(s, c)
    ncb = c // tc
    nr = s // tr
    nchunks = tr // CONV_CHUNK

    def body(x_ref, p_ref, w_ref, b_ref, d_ref, dx_ref, dw_ref, db_ref, head):
        @pl.when(pl.program_id(2) == 0)
        def _():
            for r in (head, dw_ref, db_ref):
                r[...] = jnp.zeros_like(r)

        wg, wv, bg, bv = w_ref[0], w_ref[1], b_ref[0], b_ref[1]

        def do(c0, win_g, win_v, later_g, later_v, dwg, dwv, dbg, dbv):
            xg = [_lagged(win_g, width - 1 - k) for k in range(width)]
            xv = [_lagged(win_v, width - 1 - k) for k in range(width)]
            yg = _taps(xg, wg) + bg
            yv = _taps(xv, wv) + bv
            sg = _sigmoid(yg)
            da = d_ref[pl.ds(c0, CONV_CHUNK), :]
            dyv = da * yg * sg
            dyg = da * yv * sg * (1.0 + yg * (1.0 - sg))
            dwg = tuple(dw + _fold_rows(dyg * x) for dw, x in zip(dwg, xg))
            dwv = tuple(dw + _fold_rows(dyv * x) for dw, x in zip(dwv, xv))
            dbg = dbg + _fold_rows(dyg)
            dbv = dbv + _fold_rows(dyv)
            ahead_g = jnp.concatenate([dyg, later_g], axis=0)
            ahead_v = jnp.concatenate([dyv, later_v], axis=0)
            dxg = _taps([_led(ahead_g, width - 1 - k) for k in range(width)], wg)
            dxv = _taps([_led(ahead_v, width - 1 - k) for k in range(width)], wv)
            dx_ref[0, pl.ds(c0, CONV_CHUNK), :] = dxg.astype(BF16)
            dx_ref[1, pl.ds(c0, CONV_CHUNK), :] = dxv.astype(BF16)
            return dyg[:SUB], dyv[:SUB], dwg, dwv, dbg, dbv

        def chunk(it, carry):
            c0 = pl.multiple_of((nchunks - 1 - it) * CONV_CHUNK, CONV_CHUNK)
            rows = pl.ds(pl.multiple_of(c0 - SUB, SUB), CONV_CHUNK + SUB)
            return do(c0, x_ref[0, rows, :], x_ref[1, rows, :], *carry)

        zero = jnp.zeros((SUB, LANE), F32)
        carry = lax.fori_loop(0, nchunks - 1, chunk,
                              (head[0], head[1], (zero,) * width, (zero,) * width, zero, zero))
        before = jnp.where(pl.program_id(2) == nr - 1, 0.0, p_ref[...])
        later_g, later_v, dwg, dwv, dbg, dbv = do(
            0, jnp.concatenate([before[0], x_ref[0, :CONV_CHUNK, :]], axis=0),
            jnp.concatenate([before[1], x_ref[1, :CONV_CHUNK, :]], axis=0), *carry)
        head[0] = later_g
        head[1] = later_v
        db_ref[0] += jnp.sum(dbg, axis=0, keepdims=True)
        db_ref[1] += jnp.sum(dbv, axis=0, keepdims=True)
        for k in range(width):
            dw_ref[0, k:k + 1, :] += jnp.sum(dwg[k], axis=0, keepdims=True)
            dw_ref[1, k:k + 1, :] += jnp.sum(dwv[k], axis=0, keepdims=True)

    blk = pl.BlockSpec((2, None, tr, tc), lambda h, j, i: (0, h, nr - 1 - i, j))
    prev = pl.BlockSpec((2, None, SUB, tc), lambda h, j, i: (0, h, _prev_rows_index(nr - 1 - i, tr), j))
    wblk = pl.BlockSpec((2, None, width, tc), lambda h, j, i: (0, h, 0, j))
    bblk = pl.BlockSpec((2, None, 1, tc), lambda h, j, i: (0, h, 0, j))
    return pl.pallas_call(
        _per_lane_tile(body), name="ffn_act_bwd", grid=(halves, ncb, nr),
        in_specs=[blk, prev, wblk, bblk, pl.BlockSpec((tr, tc), lambda h, j, i: (nr - 1 - i, h * ncb + j))],
        out_specs=[blk, wblk, bblk],
        out_shape=[jax.ShapeDtypeStruct(pre.shape, BF16), jax.ShapeDtypeStruct(w.shape, F32),
                   jax.ShapeDtypeStruct(b.shape, F32)],
        scratch_shapes=[pltpu.VMEM((2, SUB, tc), F32)],
        compiler_params=_params(("parallel", "parallel", "arbitrary")),
    )(pre, pre, w, b, dact)


def _tri_masks():
    row = lax.broadcasted_iota(jnp.int32, (CHUNK, CHUNK), 0)
    col = lax.broadcasted_iota(jnp.int32, (CHUNK, CHUNK), 1)
    return row, col


def _tri_inv(ms, row, col):
    eye = (row == col).astype(F32)
    same_blk = (row >> 4) == (col >> 4)
    mds = [jnp.where(same_blk, m, 0.0) for m in ms]
    offs = [m - md for m, md in zip(ms, mds)]
    xs = [eye - md for md in mds]
    ps = [_bdot(md, md, NN) for md in mds]
    for _ in range(2):
        rs = [_bdot(jnp.concatenate([x, p], axis=0), p, NN) for x, p in zip(xs, ps)]
        xs = [x + r[:CHUNK] for x, r in zip(xs, rs)]
        ps = [r[CHUNK:] for r in rs]
    xs = [x + _bdot(x, p, NN) for x, p in zip(xs, ps)]
    ps = [_bdot(x, off, NN) for x, off in zip(xs, offs)]
    pps = [_bdot(p, p, NN) for p in ps]
    ys = [eye - p for p in ps]
    ys = [y + _bdot(y, pp, NN) for y, pp in zip(ys, pps)]
    return [_bdot(y, x, NN) for y, x in zip(ys, xs)]


def _gdn_gates(ba, alog, dtb, row, col):
    sig = _sigmoid(ba)
    neg_a = -jnp.exp(alog)
    g = neg_a * _softplus(ba + dtb)
    lower = (row >= col).astype(F32)
    gcum = _dot(lower, g, NN, HIGHEST)
    return sig, neg_a, g, gcum


def _gdn_head_common(q_raw, k_raw, v, beta, gc, gr, row, col):
    causal = row >= col
    strict = row > col
    rq = lax.rsqrt(jnp.sum(q_raw * q_raw, axis=-1, keepdims=True) + EPS)
    rk = lax.rsqrt(jnp.sum(k_raw * k_raw, axis=-1, keepdims=True) + EPS)
    q = q_raw * (rq * (A_HEAD ** -0.5))
    k = k_raw * rk
    decay = jnp.where(causal, jnp.exp(jnp.where(causal, gc - gr, 0.0)), 0.0)
    eg = jnp.exp(gc)
    gl = gc[CHUNK - 1:CHUNK, :]
    ekl = jnp.exp(gl - gc)
    dec = jnp.exp(gl)
    kb = k * beta
    kbq = jnp.concatenate([kb, q], axis=0)
    both = _bdot(kbq, k, NT)
    kk, qk = both[:CHUNK], both[CHUNK:]
    a = jnp.where(causal, qk * decay, 0.0)
    return dict(rq=rq, rk=rk, q=q, k=k, decay=decay, eg=eg, ekl=ekl, dec=dec, kb=kb, kbq=kbq, kk=kk, qk=qk, a=a,
                vb=v * beta, kbg=kb * eg, qd=q * eg, ke=k * ekl, causal=causal, strict=strict)


def _gdn_fwd(qkv, ba, z, alog, dtb, wn, gather=()):
    s = qkv.shape[0]
    nc = s // CHUNK
    ng = len(gather)

    def body(qkv_ref, ba_ref, z_ref, alog_ref, dtb_ref, wn_ref, *rest):
        x_refs, rest = rest[:ng], rest[ng:]
        y_ref, o_ref, st_ref, t_ref, w_ref, vn_ref = rest[:6]
        out_refs, rest = rest[6:6 + ng], rest[6 + ng:]
        state = rest[0]
        exchange = _ShardGather(x_refs, out_refs, *rest[1:]) if ng else None

        @pl.when(pl.program_id(0) == 0)
        def _():
            state[...] = jnp.zeros_like(state)
            if ng:
                exchange.start()

        row, col = _tri_masks()
        sig, _, _, gcum = _gdn_gates(ba_ref[...], alog_ref[...], dtb_ref[...], row, col)
        gt = gcum.T
        heads = range(A_HEADS)
        lanes = [slice(h * A_HEAD, (h + 1) * A_HEAD) for h in heads]
        fs = [_gdn_head_common(qkv_ref[:, lanes[h]], qkv_ref[:, A_QK + h * A_HEAD:A_QK + (h + 1) * A_HEAD],
                               qkv_ref[:, 2 * A_QK + h * A_HEAD:2 * A_QK + (h + 1) * A_HEAD],
                               sig[:, h:h + 1], gcum[:, 8 + h:9 + h], gt[8 + h:9 + h, :], row, col) for h in heads]
        ts = [t.astype(BF16) for t in
              _tri_inv([jnp.where(f["strict"], f["kk"] * f["decay"], 0.0) for f in fs], row, col)]
        uws = [_bdot(t, jnp.concatenate([f["vb"], f["kbg"]], axis=1), NN) for t, f in zip(ts, fs)]
        s0s = [state[h] for h in heads]
        ws_ = [uw[:, A_HEAD:].astype(BF16) for uw in uws]
        wss = [_bdot(jnp.concatenate([w, f["qd"].astype(BF16)], axis=0), s0, NN) for w, f, s0 in zip(ws_, fs, s0s)]
        vnews = [(uw[:, :A_HEAD] - wsq[:CHUNK]).astype(BF16) for uw, wsq in zip(uws, wss)]
        os_ = [wsq[CHUNK:] + _bdot(f["a"], vn, NN) for wsq, f, vn in zip(wss, fs, vnews)]
        s1s = [s0 * f["dec"] + _bdot(f["ke"], vn, TN) for s0, f, vn in zip(s0s, fs, vnews)]
        for h in heads:
            ln = lanes[h]
            st_ref[0, h] = s0s[h]
            t_ref[0, h] = ts[h]
            w_ref[:, ln] = ws_[h]
            vn_ref[:, ln] = vnews[h]
            state[h] = s1s[h]
            o = os_[h]
            o_ref[:, ln] = o
            r = lax.rsqrt(jnp.mean(o * o, axis=-1, keepdims=True) + EPS)
            zz = z_ref[:, ln]
            y_ref[:, ln] = (o * r * wn_ref[...] * zz * _sigmoid(zz)).astype(BF16)

        if ng:
            @pl.when(pl.program_id(0) == nc - 1)
            def _():
                exchange.finish()

    vec = pl.BlockSpec((1, LANE), lambda n: (0, 0))
    wide = pl.BlockSpec((CHUNK, A_QK), lambda n: (n, 0))
    return pl.pallas_call(
        body, name="gdn_fwd_gather" if ng else "gdn_fwd", grid=(nc,),
        in_specs=[pl.BlockSpec((CHUNK, A_CONV_WIDTH), lambda n: (n, 0)),
                  pl.BlockSpec((CHUNK, LANE), lambda n: (n, 0)), wide, vec, vec, vec] + [HBM_SPEC] * ng,
        out_specs=[wide, wide, pl.BlockSpec((1, A_HEADS, A_HEAD, A_HEAD), lambda n: (n, 0, 0, 0)),
                   pl.BlockSpec((1, A_HEADS, CHUNK, CHUNK), lambda n: (n, 0, 0, 0)), wide, wide] + [HBM_SPEC] * ng,
        out_shape=[jax.ShapeDtypeStruct((s, A_QK), BF16), jax.ShapeDtypeStruct((s, A_QK), F32),
                   jax.ShapeDtypeStruct((nc, A_HEADS, A_HEAD, A_HEAD), F32),
                   jax.ShapeDtypeStruct((nc, A_HEADS, CHUNK, CHUNK), BF16),
                   jax.ShapeDtypeStruct((s, A_QK), BF16), jax.ShapeDtypeStruct((s, A_QK), BF16)]
        + [jax.ShapeDtypeStruct((N_SHARDS,) + g.shape, g.dtype) for g in gather],
        scratch_shapes=[pltpu.VMEM((A_HEADS, A_HEAD, A_HEAD), F32)] + (_gather_scratch(ng) if ng else []),
        compiler_params=_params(("arbitrary",)),
    )(qkv, ba, z, alog, dtb, wn, *gather)


def _gdn_bwd(qkv, ba, z, o_raw, dy, states, t_all, w_all, vn_all, alog, dtb, wn, exchange=()):
    s = qkv.shape[0]
    nc = s // CHUNK
    nx = len(exchange)

    def body(qkv_ref, ba_ref, z_ref, o_ref, dy_ref, st_ref, t_ref, w_ref, vn_ref, alog_ref, dtb_ref, wn_ref, *rest):
        p_refs, rest = rest[:nx], rest[nx:]
        dqkv_ref, dba_ref, dz_ref, dalog_ref, ddtb_ref, dwn_ref = rest[:6]
        arrive_refs, rest = rest[6:6 + nx], rest[6 + nx:]
        dstate = rest[0]
        swap = _ChipExchange(p_refs, arrive_refs, *rest[1:]) if nx else None

        @pl.when(pl.program_id(0) == 0)
        def _():
            for r in (dstate, dalog_ref, ddtb_ref, dwn_ref):
                r[...] = jnp.zeros_like(r)
            if nx:
                swap.start()

        row, col = _tri_masks()
        bat = ba_ref[...]
        sig, neg_a, g, gcum = _gdn_gates(bat, alog_ref[...], dtb_ref[...], row, col)
        gt = gcum.T
        lane = lax.broadcasted_iota(jnp.int32, (CHUNK, LANE), 1)
        ones = jnp.ones((CHUNK, LANE), F32)
        last_row = lax.broadcasted_iota(jnp.int32, (CHUNK, 1), 0) == CHUNK - 1
        wnv = wn_ref[...]
        dgc_tile = jnp.zeros((CHUNK, LANE), F32)
        dbeta_tile = jnp.zeros((CHUNK, LANE), F32)
        dwn_acc = jnp.zeros((1, LANE), F32)
        hs = []
        for h in range(A_HEADS):
            ln = slice(h * A_HEAD, (h + 1) * A_HEAD)
            lk = slice(A_QK + h * A_HEAD, A_QK + (h + 1) * A_HEAD)
            lv = slice(2 * A_QK + h * A_HEAD, 2 * A_QK + (h + 1) * A_HEAD)
            q_raw, k_raw, v = qkv_ref[:, ln], qkv_ref[:, lk], qkv_ref[:, lv]
            f = _gdn_head_common(q_raw, k_raw, v, sig[:, h:h + 1], gcum[:, 8 + h:9 + h], gt[8 + h:9 + h, :], row, col)
            f.update(h=h, ln=ln, lk=lk, lv=lv, q_raw=q_raw, k_raw=k_raw, v=v, beta=sig[:, h:h + 1],
                     s0=st_ref[0, h], ds1=dstate[h], t=t_ref[0, h], w=w_ref[:, ln], vnew=vn_ref[:, ln])
            o = o_ref[:, ln]
            zz = z_ref[:, ln]
            dyv = dy_ref[:, ln]
            r = lax.rsqrt(jnp.mean(o * o, axis=-1, keepdims=True) + EPS)
            sz = _sigmoid(zz)
            silu = zz * sz
            dz_ref[:, ln] = (dyv * o * r * wnv * sz * (1.0 + zz * (1.0 - sz))).astype(BF16)
            dwn_acc = dwn_acc + jnp.sum(dyv * silu * o * r, axis=0, keepdims=True)
            tt = dyv * silu * wnv
            do = r * tt - o * (r * r * r) * jnp.mean(tt * o, axis=-1, keepdims=True)
            f["do_b"] = do.astype(BF16)
            hs.append(f)
        for f in hs:
            f["dvnew"] = _bdot(f["a"], f["do_b"], TN) + _bdot(f["ke"], f["ds1"], NN)
            f["da"] = jnp.where(f["causal"], _bdot(f["do_b"], f["vnew"], NT), 0.0)
            f["dke"] = _bdot(f["vnew"], f["ds1"], NT)
            f["ddec"] = jnp.sum(jnp.sum(f["s0"] * f["ds1"], axis=1, keepdims=True), axis=0, keepdims=True)
        for f in hs:
            do_dv = jnp.concatenate([f["do_b"], f["dvnew"].astype(BF16)], axis=0)
            both = _bdot(do_dv, f["s0"], NT)
            f["dqd"], f["dw"] = both[:CHUNK], -both[CHUNK:]
            qd_w = jnp.concatenate([f["qd"].astype(BF16), -f["w"]], axis=0)
            dstate[f["h"]] = _bdot(qd_w, do_dv, TN) + f["dec"] * f["ds1"]
        for f in hs:
            dd = jnp.concatenate([f["dvnew"], f["dw"]], axis=1).astype(BF16)
            tdd = _bdot(f["t"], dd, TN)
            f["dvb"], f["dkbg"] = tdd[:, :A_HEAD], tdd[:, A_HEAD:]
            f["dt"] = _bdot(dd, jnp.concatenate([f["vb"], f["kbg"]], axis=1), NT)
        for f in hs:
            f["tdt"] = _bdot(f["t"], f["dt"], TN)
        for f in hs:
            dm = jnp.where(f["strict"], -_bdot(f["tdt"], f["t"], NT), 0.0)
            f["ddecay"] = (dm * f["kk"] + f["da"] * f["qk"]) * f["decay"]
            f["dboth"] = jnp.concatenate([dm * f["decay"], f["da"] * f["decay"]], axis=0).astype(BF16)
        for f in hs:
            f["r2"] = _bdot(f["dboth"], f["k"], NN)
            f["dk0"] = _bdot(f["dboth"], f["kbq"], TN)
        for f in hs:
            h, k, beta = f["h"], f["k"], f["beta"]
            dkb = f["r2"][:CHUNK] + f["dkbg"] * f["eg"]
            dq = f["r2"][CHUNK:] + f["dqd"] * f["eg"]
            dk = f["dk0"] + f["dke"] * f["ekl"] + dkb * beta
            dke_ke = jnp.sum(f["dke"] * f["ke"], axis=-1, keepdims=True)
            dgc = (jnp.sum(f["ddecay"], axis=-1, keepdims=True)
                   + jnp.sum(f["dqd"] * f["qd"], axis=-1, keepdims=True) - dke_ke
                   + jnp.sum(f["dkbg"] * f["kbg"], axis=-1, keepdims=True))
            dgl = jnp.sum(dke_ke, axis=0, keepdims=True) + f["ddec"] * f["dec"]
            dgc = dgc + jnp.where(last_row, dgl, 0.0)
            dbeta = jnp.sum(dkb * k, axis=-1, keepdims=True) + jnp.sum(f["dvb"] * f["v"], axis=-1, keepdims=True)
            dgc_tile = dgc_tile + jnp.where(lane == 8 + h, dgc, 0.0)
            dbeta_tile = dbeta_tile + jnp.where(lane == h, dbeta, 0.0)
            dqn = dq * (A_HEAD ** -0.5)
            rq, rk, q_raw, k_raw = f["rq"], f["rk"], f["q_raw"], f["k_raw"]
            dqkv_ref[:, f["ln"]] = rq * dqn - q_raw * (rq * rq * rq) * jnp.sum(dqn * q_raw, axis=-1, keepdims=True)
            dqkv_ref[:, f["lk"]] = rk * dk - k_raw * (rk * rk * rk) * jnp.sum(dk * k_raw, axis=-1, keepdims=True)
            dqkv_ref[:, f["lv"]] = f["dvb"] * beta
        ddecays = [f["ddecay"] for f in hs]
        col_sums = _dot(jnp.concatenate(ddecays, axis=1), ones, TN, HIGHEST)
        for h in range(A_HEADS):
            dgc_tile = dgc_tile - jnp.where(lane == 8 + h, col_sums[h * CHUNK:(h + 1) * CHUNK, :1], 0.0)
        upper = (row <= col).astype(F32)
        dg = _dot(upper, dgc_tile, NN, HIGHEST)
        da_raw = dg * neg_a * _sigmoid(bat + dtb_ref[...])
        dba_ref[...] = jnp.where(lane < 8, dbeta_tile * sig * (1.0 - sig),
                                 jnp.where(lane < 16, da_raw, 0.0)).astype(BF16)
        dwn_ref[...] += dwn_acc
        ddtb_ref[...] += jnp.sum(da_raw, axis=0, keepdims=True)
        dalog_ref[...] += jnp.sum(dg * g, axis=0, keepdims=True)

        if nx:
            @pl.when(pl.program_id(0) == nc - 1)
            def _():
                swap.finish()

    rev = lambda n: (nc - 1 - n, 0)
    vec = pl.BlockSpec((1, LANE), lambda n: (0, 0))
    wide = pl.BlockSpec((CHUNK, A_QK), rev)
    qkv_blk = pl.BlockSpec((CHUNK, A_CONV_WIDTH), rev)
    ba_blk = pl.BlockSpec((CHUNK, LANE), rev)
    vsh = jax.ShapeDtypeStruct((1, LANE), F32)
    return pl.pallas_call(
        body, name="gdn_bwd_exchange" if nx else "gdn_bwd", grid=(nc,),
        in_specs=[qkv_blk, ba_blk, wide, wide, wide,
                  pl.BlockSpec((1, A_HEADS, A_HEAD, A_HEAD), lambda n: (nc - 1 - n, 0, 0, 0)),
                  pl.BlockSpec((1, A_HEADS, CHUNK, CHUNK), lambda n: (nc - 1 - n, 0, 0, 0)), wide, wide,
                  vec, vec, vec] + [HBM_SPEC] * nx,
        out_specs=[qkv_blk, ba_blk, wide, vec, vec, vec] + [HBM_SPEC] * nx,
        out_shape=[jax.ShapeDtypeStruct((s, A_CONV_WIDTH), F32), jax.ShapeDtypeStruct((s, LANE), BF16),
                   jax.ShapeDtypeStruct((s, A_QK), BF16), vsh, vsh, vsh]
        + [jax.ShapeDtypeStruct((3,) + p.shape[1:], p.dtype) for p in exchange],
        scratch_shapes=[pltpu.VMEM((A_HEADS, A_HEAD, A_HEAD), F32)] + (_chip_scratch(nx) if nx else []),
        compiler_params=_params(("arbitrary",)),
    )(qkv, ba, z, o_raw, dy, states, t_all, w_all, vn_all, alog, dtb, wn, *exchange)


REL_RING = 1024
QBLK_BITS = 8


def _rel_ring_onehot():
    m = lax.broadcasted_iota(jnp.int32, (REL_RING, REL_PAD), 0)
    t = lax.broadcasted_iota(jnp.int32, (REL_RING, REL_PAD), 1)
    u = jnp.where(m < KBLK, m, m - REL_RING)
    idx = jnp.clip(LEFT - u, -REL_CLIP, REL_CLIP) + REL_CLIP
    return (t == idx).astype(F32)


def _relbias_ring(table, transpose):
    n_in, n_out = (REL_RING, REL_PAD) if transpose else (REL_PAD, REL_RING)

    def body(t_ref, o_ref):
        o_ref[...] = _dot(t_ref[...], _rel_ring_onehot(), NN if transpose else NT, HIGHEST)

    return pl.pallas_call(
        body, name="relbias_ring_bwd" if transpose else "relbias_ring",
        out_shape=jax.ShapeDtypeStruct((B_HEADS, n_out), F32),
        in_specs=[VMEM_FULL], out_specs=VMEM_FULL,
        compiler_params=_params(),
    )(table)


def _row_bit(shape, bit):
    return ((lax.broadcasted_iota(jnp.int32, shape, 0) >> bit) & 1) == 1


def _relbias_expand(ring):
    def body(r_ref, o_ref):
        b = jnp.broadcast_to(r_ref[0], (QBLK, REL_RING))
        for bit in range(QBLK_BITS):
            b = jnp.where(_row_bit(b.shape, bit), pltpu.roll(b, 1 << bit, 1), b)
        j = lax.broadcasted_iota(jnp.int32, (QBLK, KBLK), 1)
        r = lax.broadcasted_iota(jnp.int32, (QBLK, KBLK), 0)
        lo = (r >> 6) << 6
        o_ref[0] = jnp.where((j >= lo) & (j < lo + LEFT + CHUNK), b[:, :KBLK], NEG_INF)

    return pl.pallas_call(
        body, name="relbias_expand", grid=(B_HEADS,),
        in_specs=[pl.BlockSpec((1, 1, REL_RING), lambda h: (h, 0, 0))],
        out_specs=pl.BlockSpec((1, QBLK, KBLK), lambda h: (h, 0, 0)),
        out_shape=jax.ShapeDtypeStruct((B_HEADS, QBLK, KBLK), F32),
        compiler_params=_params(("parallel",)),
    )(ring)


def _relbias_reduce(ds):
    def body(d_ref, o_ref):
        d = jnp.concatenate([d_ref[0], jnp.zeros((QBLK, REL_RING - KBLK), F32)], axis=1)
        for bit in range(QBLK_BITS):
            d = jnp.where(_row_bit(d.shape, bit), pltpu.roll(d, REL_RING - (1 << bit), 1), d)
        o_ref[0] = jnp.sum(d, axis=0, keepdims=True)

    return pl.pallas_call(
        body, name="relbias_reduce", grid=(B_HEADS,),
        in_specs=[pl.BlockSpec((1, QBLK, KBLK), lambda h: (h, 0, 0))],
        out_specs=pl.BlockSpec((1, 1, REL_RING), lambda h: (h, 0, 0)),
        out_shape=jax.ShapeDtypeStruct((B_HEADS, 1, REL_RING), F32),
        compiler_params=_params(("parallel",)),
    )(ds)


def _attn_probs(q_ref, kb, b_ref, hh, q0):
    hl = slice(hh * B_HEAD, (hh + 1) * B_HEAD)
    qh = q_ref[:, hl] * (B_HEAD ** -0.5)
    kh = kb[:, hl]
    jpos = lax.broadcasted_iota(jnp.int32, (QBLK, KBLK), 1)
    sc = _bdot(qh, kh, NT) + b_ref[hh]
    sc = jnp.where(jpos + q0 >= LEFT, sc, NEG_INF)
    mx = jnp.max(sc, axis=-1, keepdims=True)
    p = jnp.exp(sc - mx)
    return p * (1.0 / jnp.sum(p, axis=-1, keepdims=True)), qh, kh


def _attn_fwd(q, kpad, vpad, bias):
    s = q.shape[0]

    def body(q_ref, k_ref, v_ref, b_ref, o_ref):
        q0 = pl.multiple_of(pl.program_id(1) * QBLK, QBLK)
        kb = k_ref[pl.ds(q0, KBLK), :]
        vb = v_ref[pl.ds(q0, KBLK), :]
        outs = []
        for hh in range(2):
            p, _, _ = _attn_probs(q_ref, kb, b_ref, hh, q0)
            outs.append(_bdot(p, vb[:, hh * B_HEAD:(hh + 1) * B_HEAD], NN))
        o_ref[...] = jnp.concatenate(outs, axis=1).astype(BF16)

    qblk = pl.BlockSpec((QBLK, LANE), lambda g, m: (m, g))
    kblk = pl.BlockSpec((LEFT + s, LANE), lambda g, m: (0, g))
    return pl.pallas_call(
        body, name="attn_fwd", grid=(B_HEADS // 2, s // QBLK),
        in_specs=[qblk, kblk, kblk, pl.BlockSpec((2, QBLK, KBLK), lambda g, m: (g, 0, 0))],
        out_specs=qblk,
        out_shape=jax.ShapeDtypeStruct((s, D_MODEL), BF16),
        compiler_params=_params(("parallel", "arbitrary")),
    )(q, kpad, vpad, bias)


def _attn_bwd(q, kpad, vpad, bias, do):
    s = q.shape[0]

    def body(q_ref, k_ref, v_ref, b_ref, do_ref, dq_ref, dk_ref, dv_ref, db_ref):
        @pl.when(pl.program_id(1) == 0)
        def _():
            for r in (dk_ref, dv_ref, db_ref):
                r[...] = jnp.zeros_like(r)

        q0 = pl.multiple_of(pl.program_id(1) * QBLK, QBLK)
        kb = k_ref[pl.ds(q0, KBLK), :]
        vb = v_ref[pl.ds(q0, KBLK), :]
        dqs, dks, dvs = [], [], []
        for hh in range(2):
            hl = slice(hh * B_HEAD, (hh + 1) * B_HEAD)
            p, qh, kh = _attn_probs(q_ref, kb, b_ref, hh, q0)
            doh = do_ref[:, hl]
            dp = _bdot(doh, vb[:, hl], NT)
            dsc = p * (dp - jnp.sum(p * dp, axis=-1, keepdims=True))
            db_ref[hh] += dsc
            dqs.append(_bdot(dsc, kh, NN) * (B_HEAD ** -0.5))
            dks.append(_bdot(dsc, qh, TN))
            dvs.append(_bdot(p, doh, TN))
        dq_ref[...] = jnp.concatenate(dqs, axis=1).astype(BF16)
        dk_ref[pl.ds(q0, KBLK), :] += jnp.concatenate(dks, axis=1)
        dv_ref[pl.ds(q0, KBLK), :] += jnp.concatenate(dvs, axis=1)

    qblk = pl.BlockSpec((QBLK, LANE), lambda g, m: (m, g))
    kblk = pl.BlockSpec((LEFT + s, LANE), lambda g, m: (0, g))
    bblk = pl.BlockSpec((2, QBLK, KBLK), lambda g, m: (g, 0, 0))
    return pl.pallas_call(
        body, name="attn_bwd", grid=(B_HEADS // 2, s // QBLK),
        in_specs=[qblk, kblk, kblk, bblk, qblk],
        out_specs=[qblk, kblk, kblk, bblk],
        out_shape=[jax.ShapeDtypeStruct((s, D_MODEL), BF16), jax.ShapeDtypeStruct((LEFT + s, D_MODEL), F32),
                   jax.ShapeDtypeStruct((LEFT + s, D_MODEL), F32),
                   jax.ShapeDtypeStruct((B_HEADS, QBLK, KBLK), F32)],
        compiler_params=_params(("parallel", "arbitrary")),
    )(q, kpad, vpad, bias, do)


def _adamw(w, g, m, v):
    r, c = w.shape
    tr = r
    for cand in (512, 256, 128, 64, 32, 16, 8):
        if r % cand == 0 and cand * c * 4 <= 2 * 1024 * 1024:
            tr = cand
            break
    c1 = 1.0 / (1.0 - ADAM_B1 ** ADAM_STEP)
    c2 = 1.0 / (1.0 - ADAM_B2 ** ADAM_STEP)

    def body(w_ref, g_ref, m_ref, v_ref, d_ref, mo_ref, vo_ref):
        gv = g_ref[...]
        mn = ADAM_B1 * m_ref[...] + (1.0 - ADAM_B1) * gv
        vn = ADAM_B2 * v_ref[...] + (1.0 - ADAM_B2) * (gv * gv)
        mo_ref[...] = mn
        vo_ref[...] = vn
        d_ref[...] = -ADAM_LR * ((mn * c1) / (jnp.sqrt(vn * c2) + ADAM_EPS) + ADAM_WD * w_ref[...])

    blk = pl.BlockSpec((tr, c), lambda i: (i, 0))
    sh = jax.ShapeDtypeStruct((r, c), F32)
    return pl.pallas_call(
        body, name="adamw", grid=(r // tr,),
        in_specs=[blk] * 4, out_specs=[blk] * 3, out_shape=[sh] * 3,
        compiler_params=_params(("parallel",)),
    )(w, g, m, v)


def _row(v, width=None):
    v = v.reshape(1, -1)
    if width is not None and v.shape[1] < width:
        v = jnp.pad(v, ((0, 0), (0, width - v.shape[1])))
    return v


def _gate_row(v):
    return jnp.pad(v.reshape(1, A_HEADS), ((0, 0), (A_HEADS, LANE - 2 * A_HEADS)))


DEPTH = 4
N_A = 2
N_B = 2
F_DOWN_ROWS = FFN_DIM // N_SHARDS
SQ_ROWS = D_MODEL // N_SHARDS
UP_COLS = 2 * FFN_DIM // N_SHARDS


def _a_layer_fwd(h, w, i, pending=None):
    xn = _rmsnorm_fwd(h, _row(w["a_norm"][i]))
    w_qkv, w_z, w_ba = w["a_in"][i]
    pre = _mm(xn, w_qkv, name="a_qkv")
    z = _mm(xn, w_z, name="a_z")
    ba = _mm(xn, w_ba, name="a_ba")
    act = _conv_silu_fwd(pre, w["a_conv"][i])
    alog, dtb, wn = _gate_row(w["a_A_log"][i]), _gate_row(w["a_dt_bias"][i]), _row(w["a_out_norm"][i])
    if pending is None:
        y, o_raw, states, t_all, w_all, vn_all = _gdn_fwd(act, ba, z, alog, dtb, wn)
    else:
        names = list(pending)
        y, o_raw, states, t_all, w_all, vn_all, *landed = _gdn_fwd(act, ba, z, alog, dtb, wn,
                                                                   gather=[pending[n] for n in names])
        w.update(zip(names, landed))
    h2 = _mm_rowsh(y, w["GA"], SQ_ROWS, i, "nn", "a_out", res=h)
    saved = dict(h=h, xn=xn, pre=pre, z=z, ba=ba, act=act, o_raw=o_raw, y=y, states=states,
                 t_all=t_all, w_all=w_all, vn_all=vn_all, alog=alog, dtb=dtb, wn=wn)
    return h2, saved


def _a_layer_bwd(dh2, w, i, sv, acc, outgoing=None):
    g = {}
    acc["GA"] = _mm_rowsh_dw(sv["y"], dh2, SQ_ROWS, "a_out_dw", acc["GA"], i)
    exchange = outgoing() if outgoing is not None else ()
    dy = _mm_rowsh(dh2, w["GA"], SQ_ROWS, i, "nt", "a_out_dx")
    dact, dba, dz, dalog, ddtb, dwn, *arrived = _gdn_bwd(sv["act"], sv["ba"], sv["z"], sv["o_raw"], dy, sv["states"],
                                                         sv["t_all"], sv["w_all"], sv["vn_all"],
                                                         sv["alog"], sv["dtb"], sv["wn"], exchange=exchange)
    dpre, g["conv"] = _conv_silu_bwd(sv["pre"], w["a_conv"][i], dact)
    xn = sv["xn"]
    w_qkv, w_z, w_ba = w["a_in"][i]
    d_in = jnp.concatenate([_mm(xn, dpre, "tn", name="a_qkv_dw"), _mm(xn, dz, "tn", name="a_z_dw"),
                            _mm(xn, dba, "tn", name="a_ba_dw")[:, :2 * A_HEADS]], axis=1)
    g["w_in"] = jnp.transpose(d_in.reshape(D_MODEL, N_SHARDS, -1), (1, 0, 2))
    dxn = _mm(dpre, w_qkv, "nt", name="a_qkv_dx")
    dxn = _mm(dz, w_z, "nt", res=dxn, name="a_z_dx")
    dh, dnorm = _mm(dba, w_ba, "nt", res=dxn, name="a_ba_dx", norm=(sv["h"], _row(w["a_norm"][i]), dh2))
    g["norm"] = dnorm[0]
    g["A_log"] = dalog[0, A_HEADS:2 * A_HEADS]
    g["dt_bias"] = ddtb[0, A_HEADS:2 * A_HEADS]
    g["out_norm"] = dwn[0]
    return dh, g, arrived


def _by_half(a, lead):
    return jnp.moveaxis(a.reshape(a.shape[:-1] + (2, 2, UP_COLS)), (-3, -2), (0, 1)).reshape((2, 2) + lead + (UP_COLS,))


def _from_half(a):
    lead = a.shape[2:-1]
    return jnp.moveaxis(a, (0, 1), (-3, -2)).reshape(lead + (N_SHARDS * UP_COLS,))


def _ffn_buffers(l):
    return ("GD0", "GU0", 0) if l == 0 else ("GD1", "GU1", l - 1)


def _ffn_fwd(h, w, l):
    s = h.shape[0]
    xn = _rmsnorm_fwd(h, _row(w["f_norm"][l]))
    cw = _by_half(w["f_conv"][l], (w["f_conv"].shape[1],))
    cb = _by_half(w["f_conv_b"][l][None], (1,))
    down, up, blk = _ffn_buffers(l)
    pre = _mm_colsh(xn, w[up], D_MODEL, blk, "nn", "f_up").reshape(2, 2, s, UP_COLS)
    act = _ffn_act_fwd(pre, cw, cb)
    h2 = _mm_rowsh(act, w[down], F_DOWN_ROWS, blk, "nn", "f_down", res=h)
    return h2, dict(h=h, xn=xn, pre=pre, act=act, cw=cw, cb=cb)


def _ffn_bwd(dh2, w, l, sv, acc):
    g = {}
    s = dh2.shape[0]
    down, up, blk = _ffn_buffers(l)
    acc[down] = _mm_rowsh_dw(sv["act"], dh2, F_DOWN_ROWS, "f_down_dw", acc[down], blk)
    dact = _mm_rowsh(dh2, w[down], F_DOWN_ROWS, blk, "nt", "f_down_dx")
    dpre, dcw, dcb = _ffn_act_bwd(sv["pre"], sv["cw"], sv["cb"], dact)
    dpre = dpre.reshape(N_SHARDS, s, UP_COLS)
    acc[up] = _mm_colsh_dw(sv["xn"], dpre, "f_up_dw", into=acc[up], blk0=blk)
    g["conv"] = _from_half(dcw)
    g["conv_b"] = _from_half(dcb)[0]
    dh, dnorm = _mm_colsh(dpre, w[up], D_MODEL, blk, "nt", "f_up_dx", norm=(sv["h"], _row(w["f_norm"][l]), dh2))
    g["norm"] = dnorm[0]
    return dh, g


def _b_layer_fwd(h, w, j, kpad, vpad):
    xn = _rmsnorm_fwd(h, _row(w["b_norm"][j]))
    q = _mm_rowsh(xn, w["GB"], SQ_ROWS, j, "nn", "b_q", out_dtype=BF16)
    rel = w["b_rel_bias"][j]
    table = jnp.pad(rel, ((0, 0), (0, REL_PAD - rel.shape[1])))
    bias = _relbias_expand(_relbias_ring(table, False).reshape(B_HEADS, 1, REL_RING))
    o = _attn_fwd(q, kpad, vpad, bias)
    h2 = _mm_rowsh(o, w["GB"], SQ_ROWS, N_B + j, "nn", "b_out", res=h)
    return h2, dict(h=h, xn=xn, q=q, o=o, bias=bias)


def _b_layer_bwd(dh2, w, j, sv, kpad, vpad, acc):
    g = {}
    acc["GB"] = _mm_rowsh_dw(sv["o"], dh2, SQ_ROWS, "b_out_dw", acc["GB"], N_B + j)
    do = _mm_rowsh(dh2, w["GB"], SQ_ROWS, N_B + j, "nt", "b_out_dx", out_dtype=BF16)
    dq, dkp, dvp, dsc = _attn_bwd(sv["q"], kpad, vpad, sv["bias"], do)
    dring = _relbias_reduce(dsc).reshape(B_HEADS, REL_RING)
    g["rel_bias"] = _relbias_ring(dring, True)[:, :2 * REL_CLIP + 1]
    acc["GB"] = _mm_rowsh_dw(sv["xn"], dq, SQ_ROWS, "b_q_dw", acc["GB"], j)
    dh, dnorm = _mm_rowsh(dq, w["GB"], SQ_ROWS, j, "nt", "b_q_dx",
                          norm=(sv["h"], _row(w["b_norm"][j]), dh2))
    g["norm"] = dnorm[0]
    return dh, g, dkp, dvp


EARLY = ("GU1", "GD1", "GB", "GK")
MIDDLE = ("GU0", "GD0", "GA", "GI1")
FINAL = ("GI0",)


def _local_step(x, tgt, w, pending=None, rs_hooks=None):
    w = dict(w)
    h = x
    saved = []
    kv_saved = None
    kpad = vpad = None
    for layer in range(DEPTH):
        if layer < N_A:
            h, sm = _a_layer_fwd(h, w, layer, pending[layer] if pending else None)
        else:
            if layer == N_A:
                xn_kv = _rmsnorm_fwd(h, _row(w["kv_norm"]))
                kv = _mm_colsh(xn_kv, w["GK"], D_MODEL, 0, "nn", "kv", flat=True, out_dtype=BF16)
                kpad = jnp.pad(kv[:, :D_MODEL], ((LEFT, 0), (0, 0)))
                vpad = jnp.pad(kv[:, D_MODEL:], ((LEFT, 0), (0, 0)))
                kv_saved = dict(h=h, xn=xn_kv)
            h, sm = _b_layer_fwd(h, w, layer - N_A, kpad, vpad)
        h, sf = _ffn_fwd(h, w, layer)
        saved.append((sm, sf))

    loss, dh, dfinal = _final_loss(h, _row(w["final_norm"]), tgt)

    ga = [None] * N_A
    gb = [None] * N_B
    gf = [None] * DEPTH
    dk_tot = dv_tot = None
    g_kv = g_kvn = None
    acc = dict(GU1=lax.empty((N_SHARDS, (DEPTH - 1) * D_MODEL, UP_COLS), F32),
               GD1=lax.empty((N_SHARDS, (DEPTH - 1) * F_DOWN_ROWS, D_MODEL), F32),
               GB=lax.empty((N_SHARDS, 2 * N_B * SQ_ROWS, D_MODEL), F32),
               GU0=lax.empty((N_SHARDS, D_MODEL, UP_COLS), F32),
               GD0=lax.empty((N_SHARDS, F_DOWN_ROWS, D_MODEL), F32),
               GA=lax.empty((N_SHARDS, N_A * SQ_ROWS, D_MODEL), F32))
    reduced = {}
    for layer in reversed(range(DEPTH)):
        sm, sf = saved[layer]
        dh, gf[layer] = _ffn_bwd(dh, w, layer, sf, acc)
        if layer >= N_A:
            dh, gb[layer - N_A], dkp, dvp = _b_layer_bwd(dh, w, layer - N_A, sm, kpad, vpad, acc)
            dk_tot = dkp if dk_tot is None else dk_tot + dkp
            dv_tot = dvp if dv_tot is None else dv_tot + dvp
            if layer == N_A:
                dkv = jnp.concatenate([dk_tot[LEFT:], dv_tot[LEFT:]], axis=1).astype(BF16)
                g_kv = _mm_colsh_dw(kv_saved["xn"], dkv, "kv_dw", flat=True)
                dh, g_kvn = _mm_colsh(dkv, w["GK"], D_MODEL, 0, "nt", "kv_dx", flat=True,
                                      norm=(kv_saved["h"], _row(w["kv_norm"]), dh))
        elif rs_hooks is not None:
            prepare, finish = rs_hooks
            names = EARLY if layer == N_A - 1 else MIDDLE
            held = {}

            def outgoing(names=names, held=held):
                extra = dict(GK=g_kv, GI1=ga[N_A - 1]["w_in"] if ga[N_A - 1] else None)
                held["pairs"], partials = prepare([acc[n] if n in acc else extra[n] for n in names])
                return partials

            dh, ga[layer], arrived = _a_layer_bwd(dh, w, layer, sm, acc, outgoing)
            reduced.update(zip(names, finish(held["pairs"], arrived)))
        else:
            dh, ga[layer], _ = _a_layer_bwd(dh, w, layer, sm, acc)

    def stack(gs, key):
        return jnp.stack([g[key] for g in gs])

    grads = dict(
        acc, GK=g_kv, reduced=reduced, GI0=ga[0]["w_in"], GI1=ga[1]["w_in"],
        a_norm=stack(ga, "norm"), a_conv=stack(ga, "conv"), a_A_log=stack(ga, "A_log"),
        a_dt_bias=stack(ga, "dt_bias"), a_out_norm=stack(ga, "out_norm"), kv_norm=g_kvn[0],
        b_norm=stack(gb, "norm"), b_rel_bias=stack(gb, "rel_bias"),
        f_norm=stack(gf, "norm"), f_conv=stack(gf, "conv"), f_conv_b=stack(gf, "conv_b"), final_norm=dfinal[0])
    return loss, dh, grads


HBM_SPEC = pl.BlockSpec(memory_space=pl.ANY)
VMEM_SPEC = pl.BlockSpec(memory_space=pltpu.VMEM)


def _place():
    x, y, c = lax.axis_index("x"), lax.axis_index("y"), lax.axis_index("c")
    chips = [(1 - x, y), (x, 1 - y), (1 - x, 1 - y)]
    return x, y, c, chips


def _remote(src, dst, send_sem, recv_sem, to):
    return pltpu.make_async_remote_copy(src_ref=src, dst_ref=dst, send_sem=send_sem, recv_sem=recv_sem,
                                        device_id=to, device_id_type=MESH)


GATHER_COPIES = 7


class _ShardGather:
    def __init__(self, x_refs, out_refs, send_sems, recv_sems):
        n = len(x_refs)
        x, y, c, chips = _place()
        sibling = (x, y, 1 - c)

        def half(a, px, py, hc):
            rh = x_refs[a].shape[0] // 2
            return out_refs[a].at[2 * px + py, pl.ds(hc * rh, rh), :]

        def mine(a):
            rh = x_refs[a].shape[0] // 2
            return x_refs[a].at[pl.ds(c * rh, rh), :]

        def sems(a, k):
            return send_sems.at[GATHER_COPIES * a + k], recv_sems.at[GATHER_COPIES * a + k]

        order = [(j, chip, a) for j, chip in enumerate(chips) for a in range(n)]
        self.first = [_remote(mine(a), half(a, x, y, c), *sems(a, j), (*chip, c)) for j, chip, a in order]
        self.own = [_remote(x_refs[a], out_refs[a].at[2 * x + y], *sems(a, 6), sibling) for a in range(n)]
        self.landed = [_remote(half(a, *chip, c), half(a, *chip, c), *sems(a, j), (*chip, c)) for j, chip, a in order]
        self.passed = [_remote(half(a, *chip, c), half(a, *chip, c), *sems(a, 3 + j), sibling)
                       for j, chip, a in order]
        self.theirs = [_remote(half(a, *chip, 1 - c), half(a, *chip, 1 - c), *sems(a, 3 + j), sibling)
                       for j, chip, a in order]

    def start(self):
        for cp in self.first + self.own:
            cp.start()

    def finish(self):
        for arrived, onward in zip(self.landed, self.passed):
            arrived.wait_recv()
            onward.start()
        for cp in self.theirs + self.own:
            cp.wait_recv()
        for cp in self.first + self.passed + self.own:
            cp.wait_send()


def _gather_scratch(n):
    return [pltpu.SemaphoreType.DMA((GATHER_COPIES * n,)), pltpu.SemaphoreType.DMA((GATHER_COPIES * n,))]


def _allgather_weights(shards):
    n = len(shards)

    def body(*refs):
        gather = _ShardGather(refs[:n], refs[n:2 * n], *refs[2 * n:])
        gather.start()
        gather.finish()

    return pl.pallas_call(
        body, name="allgather_weights",
        out_shape=[jax.ShapeDtypeStruct((N_SHARDS,) + sh.shape, sh.dtype) for sh in shards],
        in_specs=[HBM_SPEC] * n, out_specs=[HBM_SPEC] * n,
        scratch_shapes=_gather_scratch(n),
    )(*shards)


def _pair_exchange(gs):
    n = len(gs)

    def body(*refs):
        g_refs, out_refs, (send_sems, recv_sems) = refs[:n], refs[n:2 * n], refs[2 * n:]
        x, y, c, _ = _place()
        cps = []
        for a in range(n):
            rh = gs[a].shape[1] // 2
            cps.append(_remote(g_refs[a].at[:, pl.ds((1 - c) * rh, rh), :], out_refs[a], send_sems.at[a],
                               recv_sems.at[a], (x, y, 1 - c)))
        for cp in cps:
            cp.start()
        for cp in cps:
            cp.wait()

    return pl.pallas_call(
        body, name="rs_pair_exchange",
        out_shape=[jax.ShapeDtypeStruct((g.shape[0], g.shape[1] // 2, g.shape[2]), g.dtype) for g in gs],
        in_specs=[HBM_SPEC] * n, out_specs=[HBM_SPEC] * n,
        scratch_shapes=[pltpu.SemaphoreType.DMA((n,)), pltpu.SemaphoreType.DMA((n,))],
    )(*gs)


def _add_rows(rows, cols):
    best = 16
    for t in range(16, rows + 1, 16):
        if rows % t == 0 and t * cols * 4 <= 2304 * 1024:
            best = t
    return best


def _pair_add(g, other, c_idx):
    n, r, cols = g.shape
    rh = r // 2
    tr = _add_rows(rh, cols)
    nb = rh // tr

    def body(c_ref, a_ref, b_ref, o_ref, ob_ref):
        sm = a_ref[...] + b_ref[...]
        o_ref[...] = sm
        ob_ref[...] = sm.astype(BF16)

    out_blk = pl.BlockSpec((1, tr, cols), lambda s, i, c_ref: (s, i, 0))
    return pl.pallas_call(
        body, name="rs_pair_add",
        grid_spec=pltpu.PrefetchScalarGridSpec(
            num_scalar_prefetch=1, grid=(n, nb),
            in_specs=[pl.BlockSpec((1, tr, cols), lambda s, i, c_ref: (s, c_ref[0] * nb + i, 0)), out_blk],
            out_specs=[out_blk, out_blk]),
        out_shape=[jax.ShapeDtypeStruct((n, rh, cols), F32), jax.ShapeDtypeStruct((n, rh, cols), BF16)],
        compiler_params=_params(("parallel", "parallel")),
    )(c_idx, g, other)


class _ChipExchange:
    def __init__(self, p_refs, out_refs, send_sems, recv_sems):
        x, y, c, chips = _place()
        self.copies = [_remote(p_refs[a].at[2 * chip[0] + chip[1]], out_refs[a].at[j], send_sems.at[3 * a + j],
                               recv_sems.at[3 * a + j], (*chip, c))
                       for a in range(len(p_refs)) for j, chip in enumerate(chips)]

    def start(self):
        for cp in self.copies:
            cp.start()

    def finish(self):
        for cp in self.copies:
            cp.wait()


def _chip_scratch(n):
    return [pltpu.SemaphoreType.DMA((3 * n,)), pltpu.SemaphoreType.DMA((3 * n,))]


def _chip_exchange(ps):
    n = len(ps)

    def body(*refs):
        exchange = _ChipExchange(refs[:n], refs[n:2 * n], *refs[2 * n:])
        exchange.start()
        exchange.finish()

    return pl.pallas_call(
        body, name="rs_chip_exchange",
        out_shape=[jax.ShapeDtypeStruct((3,) + p.shape[1:], p.dtype) for p in ps],
        in_specs=[HBM_SPEC] * n, out_specs=[HBM_SPEC] * n,
        scratch_shapes=_chip_scratch(n),
    )(*ps)


def _chip_add(p, recv, chip_idx):
    n, rh, cols = p.shape
    tr = _add_rows(rh, cols)

    def body(s_ref, own_ref, r_ref, o_ref):
        o_ref[...] = ((own_ref[0] + r_ref[0].astype(F32)) + r_ref[1].astype(F32)) + r_ref[2].astype(F32)

    return pl.pallas_call(
        body, name="rs_chip_add",
        grid_spec=pltpu.PrefetchScalarGridSpec(
            num_scalar_prefetch=1, grid=(rh // tr,),
            in_specs=[pl.BlockSpec((1, tr, cols), lambda i, s_ref: (s_ref[0], i, 0)),
                      pl.BlockSpec((3, tr, cols), lambda i, s_ref: (0, i, 0))],
            out_specs=pl.BlockSpec((tr, cols), lambda i, s_ref: (i, 0))),
        out_shape=jax.ShapeDtypeStruct((rh, cols), p.dtype),
        compiler_params=_params(("parallel",)),
    )(chip_idx, p, recv)


def _pair_gather(fs):
    n = len(fs)

    def body(*refs):
        f_refs, out_refs, (send_sems, recv_sems) = refs[:n], refs[n:2 * n], refs[2 * n:]
        x, y, c, _ = _place()
        cps = [_remote(f_refs[a], out_refs[a], send_sems.at[a], recv_sems.at[a], (x, y, 1 - c)) for a in range(n)]
        for cp in cps:
            cp.start()
        for cp in cps:
            cp.wait()

    return pl.pallas_call(
        body, name="rs_pair_gather",
        out_shape=[jax.ShapeDtypeStruct(f.shape, f.dtype) for f in fs],
        in_specs=[HBM_SPEC] * n, out_specs=[HBM_SPEC] * n,
        scratch_shapes=[pltpu.SemaphoreType.DMA((n,)), pltpu.SemaphoreType.DMA((n,))],
    )(*fs)


def _allreduce_small(v):
    r, cols = v.shape

    def body(x_ref, out_ref, slots, send_sems, recv_sems):
        x, y, c, _ = _place()
        bits = [(bx, by, bc) for bx in (0, 1) for by in (0, 1) for bc in (0, 1)]

        def flip(b):
            return (1 - x if b[0] else x, 1 - y if b[1] else y, 1 - c if b[2] else c)

        slots[0] = x_ref[...]
        cps = [_remote(x_ref, slots.at[k], send_sems.at[k - 1], recv_sems.at[k - 1], flip(bits[k]))
               for k in range(1, 8)]
        for cp in cps:
            cp.start()
        for cp in cps:
            cp.wait()
        acc = None
        for b in bits:
            fx, fy, fc = flip(b)
            term = slots[4 * fx + 2 * fy + fc]
            acc = term if acc is None else acc + term
        out_ref[...] = acc

    return pl.pallas_call(
        body, name="allreduce_small",
        out_shape=jax.ShapeDtypeStruct((r, cols), v.dtype),
        in_specs=[VMEM_SPEC], out_specs=VMEM_SPEC,
        scratch_shapes=[pltpu.VMEM((8, r, cols), v.dtype), pltpu.SemaphoreType.DMA((7,)),
                        pltpu.SemaphoreType.DMA((7,))],
        compiler_params=pltpu.CompilerParams(vmem_limit_bytes=VMEM_LIMIT),
    )(v)


BIG = (("a_w_in", 2), ("a_w_out", 1), ("w_kv", 1), ("b_w_q", 1), ("b_w_out", 1), ("f_w_up", 2), ("f_w_down", 1))
SMALL = (("a_norm", 1), ("a_conv", 2), ("a_A_log", None), ("a_dt_bias", None), ("a_out_norm", None),
         ("kv_norm", None), ("b_norm", None), ("b_rel_bias", None), ("f_norm", None), ("f_conv", 2),
         ("f_conv_b", None), ("final_norm", None))
WEIGHT_ORDER = ("a_norm", "a_w_in", "a_conv", "a_A_log", "a_dt_bias", "a_out_norm", "a_w_out", "kv_norm", "w_kv",
                "b_norm", "b_w_q", "b_rel_bias", "b_w_out", "f_norm", "f_w_up", "f_conv", "f_conv_b", "f_w_down",
                "final_norm")


def _pad_rows(flat, cols, quantum):
    n = flat.shape[-1]
    rows = -(-n // (cols * quantum)) * quantum
    pad = [(0, 0)] * (flat.ndim - 1) + [(0, rows * cols - n)]
    return jnp.pad(flat, pad).reshape(flat.shape[:-1] + (rows, cols))


FIRST_GROUPS = ("GI",)
RIDING_GROUPS = (("GA", "GU0", "GD0", "GB", "GK"), ("GU1", "GD1"))
GROUPS = FIRST_GROUPS + RIDING_GROUPS[0] + RIDING_GROUPS[1]


def _group_shards(w, dtype):
    def two(a):
        return a.reshape(-1, a.shape[-1]).astype(dtype)

    return dict(GU0=two(w["f_w_up"][:1]), GU1=two(w["f_w_up"][1:]), GD0=two(w["f_w_down"][:1]),
                GD1=two(w["f_w_down"][1:]), GB=jnp.concatenate([two(w["b_w_q"]), two(w["b_w_out"])]),
                GK=w["w_kv"].astype(dtype), GI=two(w["a_w_in"]), GA=two(w["a_w_out"]))


def _ungroup(red, shard_shapes):
    b_rows = N_B * SQ_ROWS
    flat = dict(f_w_up=jnp.concatenate([red["GU0"], red["GU1"]]), f_w_down=jnp.concatenate([red["GD0"], red["GD1"]]),
                b_w_q=red["GB"][:b_rows], b_w_out=red["GB"][b_rows:], a_w_out=red["GA"], w_kv=red["GK"],
                a_w_in=jnp.concatenate([red["GI0"], red["GI1"]]))
    return {n: v.reshape(shard_shapes[n]) for n, v in flat.items()}


def _dense_a_in(gi):
    out = []
    for i in range(N_A):
        full = jnp.transpose(gi[:, i * D_MODEL:(i + 1) * D_MODEL], (1, 0, 2)).reshape(D_MODEL, -1)
        out.append((full[:, :A_CONV_WIDTH], full[:, A_CONV_WIDTH:A_CONV_WIDTH + A_QK],
                    jnp.pad(full[:, A_CONV_WIDTH + A_QK:], ((0, 0), (0, LANE - 2 * A_HEADS)))))
    return out


def _pack_small(values, names):
    return _pad_rows(jnp.concatenate([values[n].reshape(-1) for n in names]), LANE, SUB)


def _unpack_small(packed, shapes, names):
    flat = packed.reshape(-1)
    out, off = {}, 0
    for n in names:
        size = math.prod(shapes[n])
        out[n] = flat[off:off + size].reshape(shapes[n])
        off += size
    return out


def _adamw_nd(w, g, m, v):
    shp = w.shape
    two = (math.prod(shp[:-1]), shp[-1])
    d, mn, vn = _adamw(w.reshape(two), g.reshape(two), m.reshape(two), v.reshape(two))
    return d.reshape(shp), mn.reshape(shp), vn.reshape(shp)


def kernel(x, a_norm, a_w_in, a_conv, a_A_log, a_dt_bias, a_out_norm, a_w_out, kv_norm, w_kv, b_norm, b_w_q, b_rel_bias, b_w_out, f_norm, f_w_up, f_conv, f_conv_b, f_w_down, final_norm, loss_target, m_a_norm, m_a_w_in, m_a_conv, m_a_A_log, m_a_dt_bias, m_a_out_norm, m_a_w_out, m_kv_norm, m_w_kv, m_b_norm, m_b_w_q, m_b_rel_bias, m_b_w_out, m_f_norm, m_f_w_up, m_f_conv, m_f_conv_b, m_f_w_down, m_final_norm, v_a_norm, v_a_w_in, v_a_conv, v_a_A_log, v_a_dt_bias, v_a_out_norm, v_a_w_out, v_kv_norm, v_w_kv, v_b_norm, v_b_w_q, v_b_rel_bias, v_b_w_out, v_f_norm, v_f_w_up, v_f_conv, v_f_conv_b, v_f_w_down, v_final_norm):
    w = dict(a_norm=a_norm, a_w_in=a_w_in, a_conv=a_conv, a_A_log=a_A_log, a_dt_bias=a_dt_bias,
             a_out_norm=a_out_norm, a_w_out=a_w_out, kv_norm=kv_norm, w_kv=w_kv, b_norm=b_norm, b_w_q=b_w_q,
             b_rel_bias=b_rel_bias, b_w_out=b_w_out, f_norm=f_norm, f_w_up=f_w_up, f_conv=f_conv,
             f_conv_b=f_conv_b, f_w_down=f_w_down, final_norm=final_norm)
    m = dict(a_norm=m_a_norm, a_w_in=m_a_w_in, a_conv=m_a_conv, a_A_log=m_a_A_log, a_dt_bias=m_a_dt_bias,
             a_out_norm=m_a_out_norm, a_w_out=m_a_w_out, kv_norm=m_kv_norm, w_kv=m_w_kv, b_norm=m_b_norm,
             b_w_q=m_b_w_q, b_rel_bias=m_b_rel_bias, b_w_out=m_b_w_out, f_norm=m_f_norm, f_w_up=m_f_w_up,
             f_conv=m_f_conv, f_conv_b=m_f_conv_b, f_w_down=m_f_w_down, final_norm=m_final_norm)
    v = dict(a_norm=v_a_norm, a_w_in=v_a_w_in, a_conv=v_a_conv, a_A_log=v_a_A_log, a_dt_bias=v_a_dt_bias,
             a_out_norm=v_a_out_norm, a_w_out=v_a_w_out, kv_norm=v_kv_norm, w_kv=v_w_kv, b_norm=v_b_norm,
             b_w_q=v_b_w_q, b_rel_bias=v_b_rel_bias, b_w_out=v_b_w_out, f_norm=v_f_norm, f_w_up=v_f_w_up,
             f_conv=v_f_conv, f_conv_b=v_f_conv_b, f_w_down=v_f_w_down, final_norm=v_final_norm)
    xi, yi, ci = lax.axis_index("x"), lax.axis_index("y"), lax.axis_index("c")
    chip = 2 * xi + yi
    shard_shapes = {n: w[n].shape for n in WEIGHT_ORDER}

    mine_w = _group_shards(w, BF16)
    full = dict(zip(FIRST_GROUPS, _allgather_weights([mine_w[n] for n in FIRST_GROUPS])))
    full["a_in"] = _dense_a_in(full.pop("GI"))
    pending = [{n: mine_w[n] for n in names} for names in RIDING_GROUPS]
    sharded_small = [n for n, axis in SMALL if axis is not None]
    placed = {}
    for n, axis in SMALL:
        if axis is not None:
            wide = list(w[n].shape)
            wide[axis] *= 4
            mine_once = w[n] * (1 - ci).astype(F32)
            placed[n] = lax.dynamic_update_slice_in_dim(jnp.zeros(wide, F32), mine_once, chip * w[n].shape[axis], axis)
    placed_shapes = {n: placed[n].shape for n in sharded_small}
    full.update(_unpack_small(_allreduce_small(_pack_small(placed, sharded_small)), placed_shapes, sharded_small))
    for n, axis in SMALL:
        if axis is None:
            full[n] = w[n]

    c_idx = jnp.reshape(ci, (1,)).astype(jnp.int32)
    chip_idx = jnp.reshape(chip, (1,)).astype(jnp.int32)

    def pair_stage(gs):
        pairs = [_pair_add(g, o, c_idx) for g, o in zip(gs, _pair_exchange(gs))]
        return pairs, [pb for _, pb in pairs]

    def chip_stage(pairs, arrived):
        mine = [_chip_add(p, r, chip_idx) for (p, _), r in zip(pairs, arrived)]
        theirs = _pair_gather(mine)
        return [jnp.concatenate([jnp.where(ci == 0, a, b), jnp.where(ci == 0, b, a)], axis=0)
                for a, b in zip(mine, theirs)]

    loss_part, grad_x, grads = _local_step(x[0], loss_target[0], full, pending, (pair_stage, chip_stage))
    last_pairs, outgoing = pair_stage([grads[n] for n in FINAL])
    last = chip_stage(last_pairs, _chip_exchange(outgoing))
    red = _ungroup(dict(grads["reduced"], **dict(zip(FINAL, last))), shard_shapes)

    small_names = [n for n, _ in SMALL]
    small_vals = {n: grads[n] for n in small_names}
    small_vals["loss"] = loss_part[0, :1]
    names = ["loss"] + small_names
    shapes = {n: small_vals[n].shape for n in names}
    summed = _unpack_small(_allreduce_small(_pack_small(small_vals, names)), shapes, names)
    loss = summed["loss"][0]
    for n, axis in SMALL:
        g = summed[n]
        if axis is not None:
            g = lax.dynamic_slice_in_dim(g, chip * w[n].shape[axis], w[n].shape[axis], axis)
        red[n] = g

    delta, new_m, new_v = {}, {}, {}
    for n, _ in BIG:
        delta[n], new_m[n], new_v[n] = _adamw_nd(w[n], red[n], m[n], v[n])
    local_shapes = {n: w[n].shape for n in small_names}
    packs = [_pack_small(t, small_names) for t in (w, red, m, v)]
    outs = _adamw(*packs)
    ds, ms, vs = (_unpack_small(o, local_shapes, small_names) for o in outs)
    delta.update(ds)
    new_m.update(ms)
    new_v.update(vs)

    return (loss, grad_x[None], *[red[n] for n in WEIGHT_ORDER], *[delta[n] for n in WEIGHT_ORDER],
            *[new_m[n] for n in WEIGHT_ORDER], *[new_v[n] for n in WEIGHT_ORDER])
```

```python
import math

import jax
import jax.numpy as jnp
from jax import lax
from jax.experimental import pallas as pl
from jax.experimental.pallas import tpu as pltpu

F32 = jnp.float32
BF16 = jnp.bfloat16
HIGHEST = lax.Precision.HIGHEST
MESH = pl.DeviceIdType.MESH

D_MODEL = 1024
CHUNK = 64
A_HEADS = 8
A_HEAD = 128
A_QK = A_HEADS * A_HEAD
A_CONV_WIDTH = 3 * A_QK
B_HEADS = 16
B_HEAD = 64
LEFT = 8 * CHUNK
QBLK = 4 * CHUNK
KBLK = LEFT + QBLK
REL_CLIP = 256
REL_PAD = 640
FFN_DIM = 2816
EPS = 1e-6
NEG_INF = -1e30
LANE = 128
SUB = 8
VMEM_LIMIT = 56 * 1024 * 1024

ADAM_LR = 0.001
ADAM_B1 = 0.9
ADAM_B2 = 0.999
ADAM_EPS = 1e-08
ADAM_WD = 0.01
ADAM_STEP = 10


VMEM_FULL = pl.BlockSpec(memory_space=pltpu.VMEM)


def _params(sem=None):
    return pltpu.CompilerParams(dimension_semantics=sem, vmem_limit_bytes=VMEM_LIMIT)


def _tile(n, cap):
    if n <= cap:
        return n
    best = None
    for t in range(LANE, cap + 1, LANE):
        if n % t == 0:
            best = t
    assert best is not None, n
    return best


def _sigmoid(x):
    return 0.5 * jnp.tanh(0.5 * x) + 0.5


def _softplus(x):
    return jnp.maximum(x, 0.0) + jnp.log(1.0 + jnp.exp(-jnp.abs(x)))


def _dot(a, b, dims, prec=None):
    return lax.dot_general(a, b, (dims, ((), ())), preferred_element_type=F32, precision=prec)


NN = ((1,), (0,))
NT = ((1,), (1,))
TN = ((0,), (0,))


def _bdot(a, b, dims):
    return _dot(a.astype(BF16), b.astype(BF16), dims)


def _mm(a, b, mode="nn", out_dtype=F32, res=None, name="mm", norm=None):
    if mode == "nn":
        (m, k), (k2, n) = a.shape, b.shape
    elif mode == "nt":
        (m, k), (n, k2) = a.shape, b.shape
    else:
        (k, m), (k2, n) = a.shape, b.shape
    assert k == k2, (a.shape, b.shape, mode)
    tm, tn, tk = _tile(m, 1408), _tile(n, 1408), _tile(k, 1408)
    if m == 8192:
        tm = 1024
    if k == 8192:
        tk = 1024
    nk = k // tk
    dims = {"nn": NN, "nt": NT, "tn": TN}[mode]
    a_spec = {"nn": pl.BlockSpec((tm, tk), lambda i, j, kk: (i, kk)),
              "nt": pl.BlockSpec((tm, tk), lambda i, j, kk: (i, kk)),
              "tn": pl.BlockSpec((tk, tm), lambda i, j, kk: (kk, i))}[mode]
    b_spec = {"nn": pl.BlockSpec((tk, tn), lambda i, j, kk: (kk, j)),
              "nt": pl.BlockSpec((tn, tk), lambda i, j, kk: (j, kk)),
              "tn": pl.BlockSpec((tk, tn), lambda i, j, kk: (kk, j))}[mode]
    o_spec = pl.BlockSpec((tm, tn), lambda i, j, kk: (i, j))
    return _mm_call(name, a, b, dims, (m // tm, n // tn, nk), a_spec, b_spec, o_spec, (m, n), out_dtype, (tm, tn),
                    res, norm)


ROW_TILE = 1024
N_SHARDS = 4


def _mm_call(name, a, b, dims, grid, a_spec, b_spec, o_spec, out_shape, out_dtype, acc_shape, res=None, norm=None,
             into=None):
    nk = grid[2]
    has_res = res is not None
    has_norm = norm is not None
    has_into = into is not None
    if has_norm:
        assert grid[1] == 1 and len(out_shape) == 2 and out_dtype == F32

    def flat(v):
        return v.reshape(-1, v.shape[-1]) if v.ndim == 3 else v

    def body(a_ref, b_ref, *rest):
        rest = list(rest)
        res_ref = rest.pop(0) if has_res else None
        x_ref, g_ref, dres_ref = (rest.pop(0), rest.pop(0), rest.pop(0)) if has_norm else (None, None, None)
        if has_into:
            rest.pop(0)
        o_ref = rest.pop(0)
        dg_ref = rest.pop(0) if has_norm else None
        acc = rest.pop(0)
        kk = pl.program_id(2)
        first_rows = pl.program_id(0) == 0

        @pl.when(kk == 0)
        def _():
            acc[...] = jnp.zeros_like(acc)

        if has_norm:
            @pl.when(first_rows & (kk == 0))
            def _():
                dg_ref[...] = jnp.zeros_like(dg_ref)

        acc[...] += _bdot(flat(a_ref[...]), flat(b_ref[...]), dims)

        @pl.when(kk == nk - 1)
        def _():
            r = acc[...]
            if has_res:
                r = r + res_ref[...]
            if not has_norm:
                o_ref[...] = r.reshape(o_ref.shape).astype(out_dtype)
                return
            xv = x_ref[...]
            rs = lax.rsqrt(jnp.mean(xv * xv, axis=-1, keepdims=True) + EPS)
            t = r * g_ref[...]
            c = jnp.mean(t * xv, axis=-1, keepdims=True)
            o_ref[...] = dres_ref[...] + rs * t - xv * (rs * rs * rs) * c
            dg_ref[...] += jnp.sum(r * xv * rs, axis=0, keepdims=True)

    args = [a, b] + ([res] if has_res else [])
    in_specs = [a_spec, b_spec] + ([o_spec] if has_res else [])
    out_specs, out_shapes = o_spec, jax.ShapeDtypeStruct(out_shape, out_dtype)
    if has_norm:
        vec = pl.BlockSpec((1, out_shape[1]), lambda i, j, kk: (0, 0))
        args += list(norm)
        in_specs += [o_spec, vec, o_spec]
        out_specs = [o_spec, vec]
        out_shapes = [out_shapes, jax.ShapeDtypeStruct((1, out_shape[1]), F32)]
    aliases = {}
    if has_into:
        assert not has_norm and into.dtype == out_dtype
        aliases = {len(args): 0}
        args.append(into)
        in_specs.append(pl.BlockSpec(memory_space=pl.ANY))
        out_shapes = jax.ShapeDtypeStruct(into.shape, out_dtype)
    return pl.pallas_call(
        body, name=name, grid=grid, in_specs=in_specs, out_specs=out_specs, out_shape=out_shapes,
        scratch_shapes=[pltpu.VMEM(acc_shape, F32)], input_output_aliases=aliases,
        compiler_params=_params(("arbitrary" if has_norm else "parallel", "parallel", "arbitrary")),
    )(*args)


def _shards_per_block(rows):
    return N_SHARDS if N_SHARDS * rows <= 1408 else 2


def _mm_rowsh(a, buf, rows, blk0, mode, name, res=None, out_dtype=F32, norm=None):
    s = a.shape[0]
    cols = buf.shape[2]
    g = _shards_per_block(rows)
    tm = _tile(s, ROW_TILE)
    b_blk = (g, rows, cols)
    if mode == "nn":
        return _mm_call(name, a, buf, NN, (s // tm, 1, N_SHARDS // g),
                        pl.BlockSpec((tm, g * rows), lambda i, j, kk: (i, kk)),
                        pl.BlockSpec(b_blk, lambda i, j, kk: (kk, blk0, 0)),
                        pl.BlockSpec((tm, cols), lambda i, j, kk: (i, 0)),
                        (s, cols), out_dtype, (tm, cols), res)
    return _mm_call(name, a, buf, NT, (s // tm, N_SHARDS // g, 1),
                    pl.BlockSpec((tm, cols), lambda i, j, kk: (i, 0)),
                    pl.BlockSpec(b_blk, lambda i, j, kk: (j, blk0, 0)),
                    pl.BlockSpec((tm, g * rows), lambda i, j, kk: (i, j)),
                    (s, N_SHARDS * rows), out_dtype, (tm, g * rows), res, norm)


def _mm_rowsh_dw(act, dy, rows, name, into, blk0):
    s = act.shape[0]
    cols = dy.shape[1]
    g = _shards_per_block(rows)
    ts = _tile(s, ROW_TILE)
    return _mm_call(name, act, dy, TN, (1, N_SHARDS // g, s // ts),
                    pl.BlockSpec((ts, g * rows), lambda i, j, kk: (kk, j)),
                    pl.BlockSpec((ts, cols), lambda i, j, kk: (kk, 0)),
                    pl.BlockSpec((g, rows, cols), lambda i, j, kk: (j, blk0, 0)),
                    into.shape, F32, (g * rows, cols), into=into)


def _mm_colsh(a, buf, krows, blk0, mode, name, flat=False, res=None, out_dtype=F32, norm=None):
    cols = buf.shape[2]
    b_nn = pl.BlockSpec((None, krows, cols), lambda i, j, kk: (j, blk0, 0))
    b_nt = pl.BlockSpec((None, krows, cols), lambda i, j, kk: (kk, blk0, 0))
    if mode == "nn":
        s = a.shape[0]
        tm = _tile(s, ROW_TILE)
        o_spec = (pl.BlockSpec((tm, cols), lambda i, j, kk: (i, j)) if flat
                  else pl.BlockSpec((None, tm, cols), lambda i, j, kk: (j, i, 0)))
        return _mm_call(name, a, buf, NN, (s // tm, N_SHARDS, 1),
                        pl.BlockSpec((tm, krows), lambda i, j, kk: (i, 0)), b_nn, o_spec,
                        (s, N_SHARDS * cols) if flat else (N_SHARDS, s, cols), out_dtype, (tm, cols), res)
    s = a.shape[0] if flat else a.shape[1]
    tm = _tile(s, ROW_TILE)
    a_spec = (pl.BlockSpec((tm, cols), lambda i, j, kk: (i, kk)) if flat
              else pl.BlockSpec((None, tm, cols), lambda i, j, kk: (kk, i, 0)))
    return _mm_call(name, a, buf, NT, (s // tm, 1, N_SHARDS), a_spec, b_nt,
                    pl.BlockSpec((tm, krows), lambda i, j, kk: (i, 0)),
                    (s, krows), out_dtype, (tm, krows), res, norm)


def _mm_colsh_dw(x, dy, name, flat=False, into=None, blk0=0):
    s, k = x.shape
    cols = dy.shape[1] // N_SHARDS if flat else dy.shape[2]
    ts = _tile(s, ROW_TILE)
    b_spec = (pl.BlockSpec((ts, cols), lambda i, j, kk: (kk, j)) if flat
              else pl.BlockSpec((None, ts, cols), lambda i, j, kk: (j, kk, 0)))
    return _mm_call(name, x, dy, TN, (1, N_SHARDS, s // ts),
                    pl.BlockSpec((ts, k), lambda i, j, kk: (kk, 0)), b_spec,
                    pl.BlockSpec((None, k, cols), lambda i, j, kk: (j, blk0, 0)),
                    (N_SHARDS, k, cols) if into is None else into.shape, F32, (k, cols), into=into)


def _rmsnorm_fwd(x, g):
    s, d = x.shape
    tr = _tile(s, 1024)

    def body(x_ref, g_ref, o_ref):
        xv = x_ref[...]
        r = lax.rsqrt(jnp.mean(xv * xv, axis=-1, keepdims=True) + EPS)
        o_ref[...] = (xv * r * g_ref[...]).astype(BF16)

    return pl.pallas_call(
        body, name="rmsnorm_fwd", grid=(s // tr,),
        in_specs=[pl.BlockSpec((tr, d), lambda i: (i, 0)), pl.BlockSpec((1, d), lambda i: (0, 0))],
        out_specs=pl.BlockSpec((tr, d), lambda i: (i, 0)),
        out_shape=jax.ShapeDtypeStruct((s, d), BF16),
        compiler_params=_params(("parallel",)),
    )(x, g)


def _final_loss(h, g, tgt):
    s, d = h.shape
    tr = _tile(s, 1024)

    def body(x_ref, g_ref, t_ref, loss_ref, dx_ref, dg_ref):
        @pl.when(pl.program_id(0) == 0)
        def _():
            dg_ref[...] = jnp.zeros_like(dg_ref)
            loss_ref[...] = jnp.zeros_like(loss_ref)

        xv = x_ref[...]
        r = lax.rsqrt(jnp.mean(xv * xv, axis=-1, keepdims=True) + EPS)
        xh = xv * r
        err = xh * g_ref[...] - t_ref[...]
        per_row = jnp.mean(err * err, axis=-1, keepdims=True)
        loss_ref[...] += 0.5 * jnp.sum(per_row, axis=0, keepdims=True)
        dy = err * (1.0 / d)
        t = dy * g_ref[...]
        c = jnp.mean(t * xv, axis=-1, keepdims=True)
        dx_ref[...] = r * t - xv * (r * r * r) * c
        dg_ref[...] += jnp.sum(dy * xh, axis=0, keepdims=True)

    row = pl.BlockSpec((tr, d), lambda i: (i, 0))
    vec = pl.BlockSpec((1, d), lambda i: (0, 0))
    return pl.pallas_call(
        body, name="final_loss", grid=(s // tr,),
        in_specs=[row, vec, row],
        out_specs=[pl.BlockSpec((1, LANE), lambda i: (0, 0)), row, vec],
        out_shape=[jax.ShapeDtypeStruct((1, LANE), F32), jax.ShapeDtypeStruct((s, d), F32),
                   jax.ShapeDtypeStruct((1, d), F32)],
        compiler_params=_params(("arbitrary",)),
    )(h, g, tgt)


CONV_ROWS = 512
CONV_COLS = 1408
CONV_CHUNK = 64


def _per_lane_tile(tile_body):
    def body(*refs):
        for lt in range(refs[0].shape[-1] // LANE):
            cols = slice(lt * LANE, (lt + 1) * LANE)
            tile_body(*[r.at[(slice(None),) * (len(r.shape) - 1) + (cols,)] for r in refs])
    return body


def _lagged(window, lag):
    return (pltpu.roll(window, lag, 0) if lag else window)[SUB:]


def _led(window, lead):
    n = window.shape[0] - SUB
    return (pltpu.roll(window, window.shape[0] - lead, 0) if lead else window)[:n]


def _fold_rows(v):
    return jnp.sum(v.reshape(v.shape[0] // SUB, SUB, v.shape[1]), axis=0)


def _taps(shifted, w):
    acc = None
    for k, xs in enumerate(shifted):
        term = xs * w[k:k + 1, :]
        acc = term if acc is None else acc + term
    return acc


def _conv_tiles(s, c):
    return _tile(s, CONV_ROWS), _tile(c, CONV_COLS)


def _conv_silu_fwd(pre, w):
    s, c = pre.shape
    width = w.shape[0]
    tr, tc = _conv_tiles(s, c)

    def body(x_ref, w_ref, o_ref, tail):
        @pl.when(pl.program_id(1) == 0)
        def _():
            tail[...] = jnp.zeros_like(tail)

        wv = w_ref[...]

        def do(c0, window):
            y = _taps([_lagged(window, width - 1 - k) for k in range(width)], wv)
            o_ref[pl.ds(c0, CONV_CHUNK), :] = y * _sigmoid(y)

        def chunk(ci, carry):
            c0 = pl.multiple_of(ci * CONV_CHUNK, CONV_CHUNK)
            do(c0, x_ref[pl.ds(pl.multiple_of(c0 - SUB, SUB), CONV_CHUNK + SUB), :])
            return carry

        do(0, jnp.concatenate([tail[...], x_ref[:CONV_CHUNK, :]], axis=0))
        lax.fori_loop(1, tr // CONV_CHUNK, chunk, 0)
        tail[...] = x_ref[tr - SUB:, :]

    blk = pl.BlockSpec((tr, tc), lambda j, i: (i, j))
    return pl.pallas_call(
        _per_lane_tile(body), name="conv_silu_fwd", grid=(c // tc, s // tr),
        in_specs=[blk, pl.BlockSpec((width, tc), lambda j, i: (0, j))], out_specs=blk,
        out_shape=jax.ShapeDtypeStruct((s, c), F32),
        scratch_shapes=[pltpu.VMEM((SUB, tc), F32)],
        compiler_params=_params(("parallel", "arbitrary")),
    )(pre, w)


def _prev_rows_index(i_blk, tr):
    return jnp.maximum(i_blk * (tr // SUB) - 1, 0)


def _conv_silu_bwd(pre, w, dact):
    s, c = pre.shape
    width = w.shape[0]
    tr, tc = _conv_tiles(s, c)
    nr = s // tr

    nchunks = tr // CONV_CHUNK

    def body(x_ref, p_ref, w_ref, d_ref, dx_ref, dw_ref, head):
        @pl.when(pl.program_id(1) == 0)
        def _():
            head[...] = jnp.zeros_like(head)
            dw_ref[...] = jnp.zeros_like(dw_ref)

        wv = w_ref[...]

        def do(c0, window, later, dws):
            xs = [_lagged(window, width - 1 - k) for k in range(width)]
            y = _taps(xs, wv)
            sg = _sigmoid(y)
            dy = d_ref[pl.ds(c0, CONV_CHUNK), :] * sg * (1.0 + y * (1.0 - sg))
            dws = tuple(dw + _fold_rows(dy * x) for dw, x in zip(dws, xs))
            ahead = jnp.concatenate([dy, later], axis=0)
            dx_ref[pl.ds(c0, CONV_CHUNK), :] = _taps([_led(ahead, width - 1 - k) for k in range(width)],
                                                     wv).astype(BF16)
            return dy[:SUB], dws

        def chunk(it, carry):
            c0 = pl.multiple_of((nchunks - 1 - it) * CONV_CHUNK, CONV_CHUNK)
            return do(c0, x_ref[pl.ds(pl.multiple_of(c0 - SUB, SUB), CONV_CHUNK + SUB), :], *carry)

        zero = jnp.zeros((SUB, LANE), F32)
        carry = lax.fori_loop(0, nchunks - 1, chunk, (head[...], (zero,) * width))
        before = jnp.where(pl.program_id(1) == nr - 1, 0.0, p_ref[...])
        later, dws = do(0, jnp.concatenate([before, x_ref[:CONV_CHUNK, :]], axis=0), *carry)
        head[...] = later
        for k in range(width):
            dw_ref[k:k + 1, :] += jnp.sum(dws[k], axis=0, keepdims=True)

    blk = pl.BlockSpec((tr, tc), lambda j, i: (nr - 1 - i, j))
    prev = pl.BlockSpec((SUB, tc), lambda j, i: (_prev_rows_index(nr - 1 - i, tr), j))
    wblk = pl.BlockSpec((width, tc), lambda j, i: (0, j))
    return pl.pallas_call(
        _per_lane_tile(body), name="conv_silu_bwd", grid=(c // tc, nr),
        in_specs=[blk, prev, wblk, blk], out_specs=[blk, wblk],
        out_shape=[jax.ShapeDtypeStruct((s, c), BF16), jax.ShapeDtypeStruct((width, c), F32)],
        scratch_shapes=[pltpu.VMEM((SUB, tc), F32)],
        compiler_params=_params(("parallel", "arbitrary")),
    )(pre, pre, w, dact)


def _ffn_act_fwd(pre, w, b):
    _, halves, s, c = pre.shape
    width = w.shape[2]
    tr, tc = _conv_tiles(s, c)
    ncb = c // tc

    def body(x_ref, w_ref, b_ref, o_ref, tail):
        @pl.when(pl.program_id(2) == 0)
        def _():
            tail[...] = jnp.zeros_like(tail)

        wg, wv, bg, bv = w_ref[0], w_ref[1], b_ref[0], b_ref[1]

        def do(c0, win_g, win_v):
            yg = _taps([_lagged(win_g, width - 1 - k) for k in range(width)], wg) + bg
            yv = _taps([_lagged(win_v, width - 1 - k) for k in range(width)], wv) + bv
            o_ref[pl.ds(c0, CONV_CHUNK), :] = (yg * _sigmoid(yg) * yv).astype(BF16)

        def chunk(ci, carry):
            c0 = pl.multiple_of(ci * CONV_CHUNK, CONV_CHUNK)
            rows = pl.ds(pl.multiple_of(c0 - SUB, SUB), CONV_CHUNK + SUB)
            do(c0, x_ref[0, rows, :], x_ref[1, rows, :])
            return carry

        do(0, jnp.concatenate([tail[0], x_ref[0, :CONV_CHUNK, :]], axis=0),
           jnp.concatenate([tail[1], x_ref[1, :CONV_CHUNK, :]], axis=0))
        lax.fori_loop(1, tr // CONV_CHUNK, chunk, 0)
        tail[...] = x_ref[:, tr - SUB:, :]

    return pl.pallas_call(
        _per_lane_tile(body), name="ffn_act_fwd", grid=(halves, ncb, s // tr),
        in_specs=[pl.BlockSpec((2, None, tr, tc), lambda h, j, i: (0, h, i, j)),
                  pl.BlockSpec((2, None, width, tc), lambda h, j, i: (0, h, 0, j)),
                  pl.BlockSpec((2, None, 1, tc), lambda h, j, i: (0, h, 0, j))],
        out_specs=pl.BlockSpec((tr, tc), lambda h, j, i: (i, h * ncb + j)),
        out_shape=jax.ShapeDtypeStruct((s, halves * c), BF16),
        scratch_shapes=[pltpu.VMEM((2, SUB, tc), F32)],
        compiler_params=_params(("parallel", "parallel", "arbitrary")),
    )(pre, w, b)


def _ffn_act_bwd(pre, w, b, dact):
    _, halves, s, c = pre.shape
    width = w.shape[2]
    tr, tc = _conv_tiles(s, c)
    ncb = c // tc
    nr = s // tr
    nchunks = tr // CONV_CHUNK

    def body(x_ref, p_ref, w_ref, b_ref, d_ref, dx_ref, dw_ref, db_ref, head):
        @pl.when(pl.program_id(2) == 0)
        def _():
            for r in (head, dw_ref, db_ref):
                r[...] = jnp.zeros_like(r)

        wg, wv, bg, bv = w_ref[0], w_ref[1], b_ref[0], b_ref[1]

        def do(c0, win_g, win_v, later_g, later_v, dwg, dwv, dbg, dbv):
            xg = [_lagged(win_g, width - 1 - k) for k in range(width)]
            xv = [_lagged(win_v, width - 1 - k) for k in range(width)]
            yg = _taps(xg, wg) + bg
            yv = _taps(xv, wv) + bv
            sg = _sigmoid(yg)
            da = d_ref[pl.ds(c0, CONV_CHUNK), :]
            dyv = da * yg * sg
            dyg = da * yv * sg * (1.0 + yg * (1.0 - sg))
            dwg = tuple(dw + _fold_rows(dyg * x) for dw, x in zip(dwg, xg))
            dwv = tuple(dw + _fold_rows(dyv * x) for dw, x in zip(dwv, xv))
            dbg = dbg + _fold_rows(dyg)
            dbv = dbv + _fold_rows(dyv)
            ahead_g = jnp.concatenate([dyg, later_g], axis=0)
            ahead_v = jnp.concatenate([dyv, later_v], axis=0)
            dxg = _taps([_led(ahead_g, width - 1 - k) for k in range(width)], wg)
            dxv = _taps([_led(ahead_v, width - 1 - k) for k in range(width)], wv)
            dx_ref[0, pl.ds(c0, CONV_CHUNK), :] = dxg.astype(BF16)
            dx_ref[1, pl.ds(c0, CONV_CHUNK), :] = dxv.astype(BF16)
            return dyg[:SUB], dyv[:SUB], dwg, dwv, dbg, dbv

        def chunk(it, carry):
            c0 = pl.multiple_of((nchunks - 1 - it) * CONV_CHUNK, CONV_CHUNK)
            rows = pl.ds(pl.multiple_of(c0 - SUB, SUB), CONV_CHUNK + SUB)
            return do(c0, x_ref[0, rows, :], x_ref[1, rows, :], *carry)

        zero = jnp.zeros((SUB, LANE), F32)
        carry = lax.fori_loop(0, nchunks - 1, chunk,
                              (head[0], head[1], (zero,) * width, (zero,) * width, zero, zero))
        before = jnp.where(pl.program_id(2) == nr - 1, 0.0, p_ref[...])
        later_g, later_v, dwg, dwv, dbg, dbv = do(
            0, jnp.concatenate([before[0], x_ref[0, :CONV_CHUNK, :]], axis=0),
            jnp.concatenate([before[1], x_ref[1, :CONV_CHUNK, :]], axis=0), *carry)
        head[0] = later_g
        head[1] = later_v
        db_ref[0] += jnp.sum(dbg, axis=0, keepdims=True)
        db_ref[1] += jnp.sum(dbv, axis=0, keepdims=True)
        for k in range(width):
            dw_ref[0, k:k + 1, :] += jnp.sum(dwg[k], axis=0, keepdims=True)
            dw_ref[1, k:k + 1, :] += jnp.sum(dwv[k], axis=0, keepdims=True)

    blk = pl.BlockSpec((2, None, tr, tc), lambda h, j, i: (0, h, nr - 1 - i, j))
    prev = pl.BlockSpec((2, None, SUB, tc), lambda h, j, i: (0, h, _prev_rows_index(nr - 1 - i, tr), j))
    wblk = pl.BlockSpec((2, None, width, tc), lambda h, j, i: (0, h, 0, j))
    bblk = pl.BlockSpec((2, None, 1, tc), lambda h, j, i: (0, h, 0, j))
    return pl.pallas_call(
        _per_lane_tile(body), name="ffn_act_bwd", grid=(halves, ncb, nr),
        in_specs=[blk, prev, wblk, bblk, pl.BlockSpec((tr, tc), lambda h, j, i: (nr - 1 - i, h * ncb + j))],
        out_specs=[blk, wblk, bblk],
        out_shape=[jax.ShapeDtypeStruct(pre.shape, BF16), jax.ShapeDtypeStruct(w.shape, F32),
                   jax.ShapeDtypeStruct(b.shape, F32)],
        scratch_shapes=[pltpu.VMEM((2, SUB, tc), F32)],
        compiler_params=_params(("parallel", "parallel", "arbitrary")),
    )(pre, pre, w, b, dact)


def _tri_masks():
    row = lax.broadcasted_iota(jnp.int32, (CHUNK, CHUNK), 0)
    col = lax.broadcasted_iota(jnp.int32, (CHUNK, CHUNK), 1)
    return row, col


def _tri_inv(ms, row, col):
    eye = (row == col).astype(F32)
    same_blk = (row >> 4) == (col >> 4)
    mds = [jnp.where(same_blk, m, 0.0) for m in ms]
    offs = [m - md for m, md in zip(ms, mds)]
    xs = [eye - md for md in mds]
    ps = [_bdot(md, md, NN) for md in mds]
    for _ in range(2):
        rs = [_bdot(jnp.concatenate([x, p], axis=0), p, NN) for x, p in zip(xs, ps)]
        xs = [x + r[:CHUNK] for x, r in zip(xs, rs)]
        ps = [r[CHUNK:] for r in rs]
    xs = [x + _bdot(x, p, NN) for x, p in zip(xs, ps)]
    ps = [_bdot(x, off, NN) for x, off in zip(xs, offs)]
    pps = [_bdot(p, p, NN) for p in ps]
    ys = [eye - p for p in ps]
    ys = [y + _bdot(y, pp, NN) for y, pp in zip(ys, pps)]
    return [_bdot(y, x, NN) for y, x in zip(ys, xs)]


def _gdn_gates(ba, alog, dtb, row, col):
    sig = _sigmoid(ba)
    neg_a = -jnp.exp(alog)
    g = neg_a * _softplus(ba + dtb)
    lower = (row >= col).astype(F32)
    gcum = _dot(lower, g, NN, HIGHEST)
    return sig, neg_a, g, gcum


def _gdn_head_common(q_raw, k_raw, v, beta, gc, gr, row, col):
    causal = row >= col
    strict = row > col
    rq = lax.rsqrt(jnp.sum(q_raw * q_raw, axis=-1, keepdims=True) + EPS)
    rk = lax.rsqrt(jnp.sum(k_raw * k_raw, axis=-1, keepdims=True) + EPS)
    q = q_raw * (rq * (A_HEAD ** -0.5))
    k = k_raw * rk
    decay = jnp.where(causal, jnp.exp(jnp.where(causal, gc - gr, 0.0)), 0.0)
    eg = jnp.exp(gc)
    gl = gc[CHUNK - 1:CHUNK, :]
    ekl = jnp.exp(gl - gc)
    dec = jnp.exp(gl)
    kb = k * beta
    kbq = jnp.concatenate([kb, q], axis=0)
    both = _bdot(kbq, k, NT)
    kk, qk = both[:CHUNK], both[CHUNK:]
    a = jnp.where(causal, qk * decay, 0.0)
    return dict(rq=rq, rk=rk, q=q, k=k, decay=decay, eg=eg, ekl=ekl, dec=dec, kb=kb, kbq=kbq, kk=kk, qk=qk, a=a,
                vb=v * beta, kbg=kb * eg, qd=q * eg, ke=k * ekl, causal=causal, strict=strict)


def _gdn_fwd(qkv, ba, z, alog, dtb, wn, gather=()):
    s = qkv.shape[0]
    nc = s // CHUNK
    ng = len(gather)

    def body(qkv_ref, ba_ref, z_ref, alog_ref, dtb_ref, wn_ref, *rest):
        x_refs, rest = rest[:ng], rest[ng:]
        y_ref, o_ref, st_ref, t_ref, w_ref, vn_ref = rest[:6]
        out_refs, rest = rest[6:6 + ng], rest[6 + ng:]
        state = rest[0]
        exchange = _ShardGather(x_refs, out_refs, *rest[1:]) if ng else None

        @pl.when(pl.program_id(0) == 0)
        def _():
            state[...] = jnp.zeros_like(state)
            if ng:
                exchange.start()

        row, col = _tri_masks()
        sig, _, _, gcum = _gdn_gates(ba_ref[...], alog_ref[...], dtb_ref[...], row, col)
        gt = gcum.T
        heads = range(A_HEADS)
        lanes = [slice(h * A_HEAD, (h + 1) * A_HEAD) for h in heads]
        fs = [_gdn_head_common(qkv_ref[:, lanes[h]], qkv_ref[:, A_QK + h * A_HEAD:A_QK + (h + 1) * A_HEAD],
                               qkv_ref[:, 2 * A_QK + h * A_HEAD:2 * A_QK + (h + 1) * A_HEAD],
                               sig[:, h:h + 1], gcum[:, 8 + h:9 + h], gt[8 + h:9 + h, :], row, col) for h in heads]
        ts = [t.astype(BF16) for t in
              _tri_inv([jnp.where(f["strict"], f["kk"] * f["decay"], 0.0) for f in fs], row, col)]
        uws = [_bdot(t, jnp.concatenate([f["vb"], f["kbg"]], axis=1), NN) for t, f in zip(ts, fs)]
        s0s = [state[h] for h in heads]
        ws_ = [uw[:, A_HEAD:].astype(BF16) for uw in uws]
        wss = [_bdot(jnp.concatenate([w, f["qd"].astype(BF16)], axis=0), s0, NN) for w, f, s0 in zip(ws_, fs, s0s)]
        vnews = [(uw[:, :A_HEAD] - wsq[:CHUNK]).astype(BF16) for uw, wsq in zip(uws, wss)]
        os_ = [wsq[CHUNK:] + _bdot(f["a"], vn, NN) for wsq, f, vn in zip(wss, fs, vnews)]
        s1s = [s0 * f["dec"] + _bdot(f["ke"], vn, TN) for s0, f, vn in zip(s0s, fs, vnews)]
        for h in heads:
            ln = lanes[h]
            st_ref[0, h] = s0s[h]
            t_ref[0, h] = ts[h]
            w_ref[:, ln] = ws_[h]
            vn_ref[:, ln] = vnews[h]
            state[h] = s1s[h]
            o = os_[h]
            o_ref[:, ln] = o
            r = lax.rsqrt(jnp.mean(o * o, axis=-1, keepdims=True) + EPS)
            zz = z_ref[:, ln]
            y_ref[:, ln] = (o * r * wn_ref[...] * zz * _sigmoid(zz)).astype(BF16)

        if ng:
            @pl.when(pl.program_id(0) == nc - 1)
            def _():
                exchange.finish()

    vec = pl.BlockSpec((1, LANE), lambda n: (0, 0))
    wide = pl.BlockSpec((CHUNK, A_QK), lambda n: (n, 0))
    return pl.pallas_call(
        body, name="gdn_fwd_gather" if ng else "gdn_fwd", grid=(nc,),
        in_specs=[pl.BlockSpec((CHUNK, A_CONV_WIDTH), lambda n: (n, 0)),
                  pl.BlockSpec((CHUNK, LANE), lambda n: (n, 0)), wide, vec, vec, vec] + [HBM_SPEC] * ng,
        out_specs=[wide, wide, pl.BlockSpec((1, A_HEADS, A_HEAD, A_HEAD), lambda n: (n, 0, 0, 0)),
                   pl.BlockSpec((1, A_HEADS, CHUNK, CHUNK), lambda n: (n, 0, 0, 0)), wide, wide] + [HBM_SPEC] * ng,
        out_shape=[jax.ShapeDtypeStruct((s, A_QK), BF16), jax.ShapeDtypeStruct((s, A_QK), F32),
                   jax.ShapeDtypeStruct((nc, A_HEADS, A_HEAD, A_HEAD), F32),
                   jax.ShapeDtypeStruct((nc, A_HEADS, CHUNK, CHUNK), BF16),
                   jax.ShapeDtypeStruct((s, A_QK), BF16), jax.ShapeDtypeStruct((s, A_QK), BF16)]
        + [jax.ShapeDtypeStruct((N_SHARDS,) + g.shape, g.dtype) for g in gather],
        scratch_shapes=[pltpu.VMEM((A_HEADS, A_HEAD, A_HEAD), F32)] + (_gather_scratch(ng) if ng else []),
        compiler_params=_params(("arbitrary",)),
    )(qkv, ba, z, alog, dtb, wn, *gather)


def _gdn_bwd(qkv, ba, z, o_raw, dy, states, t_all, w_all, vn_all, alog, dtb, wn, exchange=()):
    s = qkv.shape[0]
    nc = s // CHUNK
    nx = len(exchange)

    def body(qkv_ref, ba_ref, z_ref, o_ref, dy_ref, st_ref, t_ref, w_ref, vn_ref, alog_ref, dtb_ref, wn_ref, *rest):
        p_refs, rest = rest[:nx], rest[nx:]
        dqkv_ref, dba_ref, dz_ref, dalog_ref, ddtb_ref, dwn_ref = rest[:6]
        arrive_refs, rest = rest[6:6 + nx], rest[6 + nx:]
        dstate = rest[0]
        swap = _ChipExchange(p_refs, arrive_refs, *rest[1:]) if nx else None

        @pl.when(pl.program_id(0) == 0)
        def _():
            for r in (dstate, dalog_ref, ddtb_ref, dwn_ref):
                r[...] = jnp.zeros_like(r)
            if nx:
                swap.start()

        row, col = _tri_masks()
        bat = ba_ref[...]
        sig, neg_a, g, gcum = _gdn_gates(bat, alog_ref[...], dtb_ref[...], row, col)
        gt = gcum.T
        lane = lax.broadcasted_iota(jnp.int32, (CHUNK, LANE), 1)
        ones = jnp.ones((CHUNK, LANE), F32)
        last_row = lax.broadcasted_iota(jnp.int32, (CHUNK, 1), 0) == CHUNK - 1
        wnv = wn_ref[...]
        dgc_tile = jnp.zeros((CHUNK, LANE), F32)
        dbeta_tile = jnp.zeros((CHUNK, LANE), F32)
        dwn_acc = jnp.zeros((1, LANE), F32)
        hs = []
        for h in range(A_HEADS):
            ln = slice(h * A_HEAD, (h + 1) * A_HEAD)
            lk = slice(A_QK + h * A_HEAD, A_QK + (h + 1) * A_HEAD)
            lv = slice(2 * A_QK + h * A_HEAD, 2 * A_QK + (h + 1) * A_HEAD)
            q_raw, k_raw, v = qkv_ref[:, ln], qkv_ref[:, lk], qkv_ref[:, lv]
            f = _gdn_head_common(q_raw, k_raw, v, sig[:, h:h + 1], gcum[:, 8 + h:9 + h], gt[8 + h:9 + h, :], row, col)
            f.update(h=h, ln=ln, lk=lk, lv=lv, q_raw=q_raw, k_raw=k_raw, v=v, beta=sig[:, h:h + 1],
                     s0=st_ref[0, h], ds1=dstate[h], t=t_ref[0, h], w=w_ref[:, ln], vnew=vn_ref[:, ln])
            o = o_ref[:, ln]
            zz = z_ref[:, ln]
            dyv = dy_ref[:, ln]
            r = lax.rsqrt(jnp.mean(o * o, axis=-1, keepdims=True) + EPS)
            sz = _sigmoid(zz)
            silu = zz * sz
            dz_ref[:, ln] = (dyv * o * r * wnv * sz * (1.0 + zz * (1.0 - sz))).astype(BF16)
            dwn_acc = dwn_acc + jnp.sum(dyv * silu * o * r, axis=0, keepdims=True)
            tt = dyv * silu * wnv
            do = r * tt - o * (r * r * r) * jnp.mean(tt * o, axis=-1, keepdims=True)
            f["do_b"] = do.astype(BF16)
            hs.append(f)
        for f in hs:
            f["dvnew"] = _bdot(f["a"], f["do_b"], TN) + _bdot(f["ke"], f["ds1"], NN)
            f["da"] = jnp.where(f["causal"], _bdot(f["do_b"], f["vnew"], NT), 0.0)
            f["dke"] = _bdot(f["vnew"], f["ds1"], NT)
            f["ddec"] = jnp.sum(jnp.sum(f["s0"] * f["ds1"], axis=1, keepdims=True), axis=0, keepdims=True)
        for f in hs:
            do_dv = jnp.concatenate([f["do_b"], f["dvnew"].astype(BF16)], axis=0)
            both = _bdot(do_dv, f["s0"], NT)
            f["dqd"], f["dw"] = both[:CHUNK], -both[CHUNK:]
            qd_w = jnp.concatenate([f["qd"].astype(BF16), -f["w"]], axis=0)
            dstate[f["h"]] = _bdot(qd_w, do_dv, TN) + f["dec"] * f["ds1"]
        for f in hs:
            dd = jnp.concatenate([f["dvnew"], f["dw"]], axis=1).astype(BF16)
            tdd = _bdot(f["t"], dd, TN)
            f["dvb"], f["dkbg"] = tdd[:, :A_HEAD], tdd[:, A_HEAD:]
            f["dt"] = _bdot(dd, jnp.concatenate([f["vb"], f["kbg"]], axis=1), NT)
        for f in hs:
            f["tdt"] = _bdot(f["t"], f["dt"], TN)
        for f in hs:
            dm = jnp.where(f["strict"], -_bdot(f["tdt"], f["t"], NT), 0.0)
            f["ddecay"] = (dm * f["kk"] + f["da"] * f["qk"]) * f["decay"]
            f["dboth"] = jnp.concatenate([dm * f["decay"], f["da"] * f["decay"]], axis=0).astype(BF16)
        for f in hs:
            f["r2"] = _bdot(f["dboth"], f["k"], NN)
            f["dk0"] = _bdot(f["dboth"], f["kbq"], TN)
        for f in hs:
            h, k, beta = f["h"], f["k"], f["beta"]
            dkb = f["r2"][:CHUNK] + f["dkbg"] * f["eg"]
            dq = f["r2"][CHUNK:] + f["dqd"] * f["eg"]
            dk = f["dk0"] + f["dke"] * f["ekl"] + dkb * beta
            dke_ke = jnp.sum(f["dke"] * f["ke"], axis=-1, keepdims=True)
            dgc = (jnp.sum(f["ddecay"], axis=-1, keepdims=True)
                   + jnp.sum(f["dqd"] * f["qd"], axis=-1, keepdims=True) - dke_ke
                   + jnp.sum(f["dkbg"] * f["kbg"], axis=-1, keepdims=True))
            dgl = jnp.sum(dke_ke, axis=0, keepdims=True) + f["ddec"] * f["dec"]
            dgc = dgc + jnp.where(last_row, dgl, 0.0)
            dbeta = jnp.sum(dkb * k, axis=-1, keepdims=True) + jnp.sum(f["dvb"] * f["v"], axis=-1, keepdims=True)
            dgc_tile = dgc_tile + jnp.where(lane == 8 + h, dgc, 0.0)
            dbeta_tile = dbeta_tile + jnp.where(lane == h, dbeta, 0.0)
            dqn = dq * (A_HEAD ** -0.5)
            rq, rk, q_raw, k_raw = f["rq"], f["rk"], f["q_raw"], f["k_raw"]
            dqkv_ref[:, f["ln"]] = rq * dqn - q_raw * (rq * rq * rq) * jnp.sum(dqn * q_raw, axis=-1, keepdims=True)
            dqkv_ref[:, f["lk"]] = rk * dk - k_raw * (rk * rk * rk) * jnp.sum(dk * k_raw, axis=-1, keepdims=True)
            dqkv_ref[:, f["lv"]] = f["dvb"] * beta
        ddecays = [f["ddecay"] for f in hs]
        col_sums = _dot(jnp.concatenate(ddecays, axis=1), ones, TN, HIGHEST)
        for h in range(A_HEADS):
            dgc_tile = dgc_tile - jnp.where(lane == 8 + h, col_sums[h * CHUNK:(h + 1) * CHUNK, :1], 0.0)
        upper = (row <= col).astype(F32)
        dg = _dot(upper, dgc_tile, NN, HIGHEST)
        da_raw = dg * neg_a * _sigmoid(bat + dtb_ref[...])
        dba_ref[...] = jnp.where(lane < 8, dbeta_tile * sig * (1.0 - sig),
                                 jnp.where(lane < 16, da_raw, 0.0)).astype(BF16)
        dwn_ref[...] += dwn_acc
        ddtb_ref[...] += jnp.sum(da_raw, axis=0, keepdims=True)
        dalog_ref[...] += jnp.sum(dg * g, axis=0, keepdims=True)

        if nx:
            @pl.when(pl.program_id(0) == nc - 1)
            def _():
                swap.finish()

    rev = lambda n: (nc - 1 - n, 0)
    vec = pl.BlockSpec((1, LANE), lambda n: (0, 0))
    wide = pl.BlockSpec((CHUNK, A_QK), rev)
    qkv_blk = pl.BlockSpec((CHUNK, A_CONV_WIDTH), rev)
    ba_blk = pl.BlockSpec((CHUNK, LANE), rev)
    vsh = jax.ShapeDtypeStruct((1, LANE), F32)
    return pl.pallas_call(
        body, name="gdn_bwd_exchange" if nx else "gdn_bwd", grid=(nc,),
        in_specs=[qkv_blk, ba_blk, wide, wide, wide,
                  pl.BlockSpec((1, A_HEADS, A_HEAD, A_HEAD), lambda n: (nc - 1 - n, 0, 0, 0)),
                  pl.BlockSpec((1, A_HEADS, CHUNK, CHUNK), lambda n: (nc - 1 - n, 0, 0, 0)), wide, wide,
                  vec, vec, vec] + [HBM_SPEC] * nx,
        out_specs=[qkv_blk, ba_blk, wide, vec, vec, vec] + [HBM_SPEC] * nx,
        out_shape=[jax.ShapeDtypeStruct((s, A_CONV_WIDTH), F32), jax.ShapeDtypeStruct((s, LANE), BF16),
                   jax.ShapeDtypeStruct((s, A_QK), BF16), vsh, vsh, vsh]
        + [jax.ShapeDtypeStruct((3,) + p.shape[1:], p.dtype) for p in exchange],
        scratch_shapes=[pltpu.VMEM((A_HEADS, A_HEAD, A_HEAD), F32)] + (_chip_scratch(nx) if nx else []),
        compiler_params=_params(("arbitrary",)),
    )(qkv, ba, z, o_raw, dy, states, t_all, w_all, vn_all, alog, dtb, wn, *exchange)


REL_RING = 1024
QBLK_BITS = 8


def _rel_ring_onehot():
    m = lax.broadcasted_iota(jnp.int32, (REL_RING, REL_PAD), 0)
    t = lax.broadcasted_iota(jnp.int32, (REL_RING, REL_PAD), 1)
    u = jnp.where(m < KBLK, m, m - REL_RING)
    idx = jnp.clip(LEFT - u, -REL_CLIP, REL_CLIP) + REL_CLIP
    return (t == idx).astype(F32)


def _relbias_ring(table, transpose):
    n_in, n_out = (REL_RING, REL_PAD) if transpose else (REL_PAD, REL_RING)

    def body(t_ref, o_ref):
        o_ref[...] = _dot(t_ref[...], _rel_ring_onehot(), NN if transpose else NT, HIGHEST)

    return pl.pallas_call(
        body, name="relbias_ring_bwd" if transpose else "relbias_ring",
        out_shape=jax.ShapeDtypeStruct((B_HEADS, n_out), F32),
        in_specs=[VMEM_FULL], out_specs=VMEM_FULL,
        compiler_params=_params(),
    )(table)


def _row_bit(shape, bit):
    return ((lax.broadcasted_iota(jnp.int32, shape, 0) >> bit) & 1) == 1


def _relbias_expand(ring):
    def body(r_ref, o_ref):
        b = jnp.broadcast_to(r_ref[0], (QBLK, REL_RING))
        for bit in range(QBLK_BITS):
            b = jnp.where(_row_bit(b.shape, bit), pltpu.roll(b, 1 << bit, 1), b)
        j = lax.broadcasted_iota(jnp.int32, (QBLK, KBLK), 1)
        r = lax.broadcasted_iota(jnp.int32, (QBLK, KBLK), 0)
        lo = (r >> 6) << 6
        o_ref[0] = jnp.where((j >= lo) & (j < lo + LEFT + CHUNK), b[:, :KBLK], NEG_INF)

    return pl.pallas_call(
        body, name="relbias_expand", grid=(B_HEADS,),
        in_specs=[pl.BlockSpec((1, 1, REL_RING), lambda h: (h, 0, 0))],
        out_specs=pl.BlockSpec((1, QBLK, KBLK), lambda h: (h, 0, 0)),
        out_shape=jax.ShapeDtypeStruct((B_HEADS, QBLK, KBLK), F32),
        compiler_params=_params(("parallel",)),
    )(ring)


def _relbias_reduce(ds):
    def body(d_ref, o_ref):
        d = jnp.concatenate([d_ref[0], jnp.zeros((QBLK, REL_RING - KBLK), F32)], axis=1)
        for bit in range(QBLK_BITS):
            d = jnp.where(_row_bit(d.shape, bit), pltpu.roll(d, REL_RING - (1 << bit), 1), d)
        o_ref[0] = jnp.sum(d, axis=0, keepdims=True)

    return pl.pallas_call(
        body, name="relbias_reduce", grid=(B_HEADS,),
        in_specs=[pl.BlockSpec((1, QBLK, KBLK), lambda h: (h, 0, 0))],
        out_specs=pl.BlockSpec((1, 1, REL_RING), lambda h: (h, 0, 0)),
        out_shape=jax.ShapeDtypeStruct((B_HEADS, 1, REL_RING), F32),
        compiler_params=_params(("parallel",)),
    )(ds)


def _attn_probs(q_ref, k_ref, b_ref, hh, q0):
    hl = slice(hh * B_HEAD, (hh + 1) * B_HEAD)
    qh = q_ref[:, hl] * (B_HEAD ** -0.5)
    kh = k_ref[pl.ds(q0, KBLK), hl]
    jpos = lax.broadcasted_iota(jnp.int32, (QBLK, KBLK), 1)
    sc = _bdot(qh, kh, NT) + b_ref[hh]
    sc = jnp.where(jpos + q0 >= LEFT, sc, NEG_INF)
    mx = jnp.max(sc, axis=-1, keepdims=True)
    p = jnp.exp(sc - mx)
    return p * (1.0 / jnp.sum(p, axis=-1, keepdims=True)), qh, kh


def _attn_fwd(q, kpad, vpad, bias):
    s = q.shape[0]

    def body(q_ref, k_ref, v_ref, b_ref, o_ref):
        q0 = pl.multiple_of(pl.program_id(1) * QBLK, QBLK)
        outs = []
        for hh in range(2):
            p, _, _ = _attn_probs(q_ref, k_ref, b_ref, hh, q0)
            outs.append(_bdot(p, v_ref[pl.ds(q0, KBLK), hh * B_HEAD:(hh + 1) * B_HEAD], NN))
        o_ref[...] = jnp.concatenate(outs, axis=1).astype(BF16)

    qblk = pl.BlockSpec((QBLK, LANE), lambda g, m: (m, g))
    kblk = pl.BlockSpec((LEFT + s, LANE), lambda g, m: (0, g))
    return pl.pallas_call(
        body, name="attn_fwd", grid=(B_HEADS // 2, s // QBLK),
        in_specs=[qblk, kblk, kblk, pl.BlockSpec((2, QBLK, KBLK), lambda g, m: (g, 0, 0))],
        out_specs=qblk,
        out_shape=jax.ShapeDtypeStruct((s, D_MODEL), BF16),
        compiler_params=_params(("parallel", "arbitrary")),
    )(q, kpad, vpad, bias)


def _attn_bwd(q, kpad, vpad, bias, do):
    s = q.shape[0]

    def body(q_ref, k_ref, v_ref, b_ref, do_ref, dq_ref, dk_ref, dv_ref, db_ref):
        @pl.when(pl.program_id(1) == 0)
        def _():
            for r in (dk_ref, dv_ref, db_ref):
                r[...] = jnp.zeros_like(r)

        q0 = pl.multiple_of(pl.program_id(1) * QBLK, QBLK)
        dqs, dks, dvs = [], [], []
        for hh in range(2):
            hl = slice(hh * B_HEAD, (hh + 1) * B_HEAD)
            p, qh, kh = _attn_probs(q_ref, k_ref, b_ref, hh, q0)
            doh = do_ref[:, hl]
            dp = _bdot(doh, v_ref[pl.ds(q0, KBLK), hl], NT)
            dsc = p * (dp - jnp.sum(p * dp, axis=-1, keepdims=True))
            db_ref[hh] += dsc
            dqs.append(_bdot(dsc, kh, NN) * (B_HEAD ** -0.5))
            dks.append(_bdot(dsc, qh, TN))
            dvs.append(_bdot(p, doh, TN))
        dq_ref[...] = jnp.concatenate(dqs, axis=1).astype(BF16)
        dk_ref[pl.ds(q0, KBLK), :] += jnp.concatenate(dks, axis=1)
        dv_ref[pl.ds(q0, KBLK), :] += jnp.concatenate(dvs, axis=1)

    qblk = pl.BlockSpec((QBLK, LANE), lambda g, m: (m, g))
    kblk = pl.BlockSpec((LEFT + s, LANE), lambda g, m: (0, g))
    bblk = pl.BlockSpec((2, QBLK, KBLK), lambda g, m: (g, 0, 0))
    return pl.pallas_call(
        body, name="attn_bwd", grid=(B_HEADS // 2, s // QBLK),
        in_specs=[qblk, kblk, kblk, bblk, qblk],
        out_specs=[qblk, kblk, kblk, bblk],
        out_shape=[jax.ShapeDtypeStruct((s, D_MODEL), BF16), jax.ShapeDtypeStruct((LEFT + s, D_MODEL), F32),
                   jax.ShapeDtypeStruct((LEFT + s, D_MODEL), F32),
                   jax.ShapeDtypeStruct((B_HEADS, QBLK, KBLK), F32)],
        compiler_params=_params(("parallel", "arbitrary")),
    )(q, kpad, vpad, bias, do)


def _adamw(w, g, m, v):
    r, c = w.shape
    tr = r
    for cand in (512, 256, 128, 64, 32, 16, 8):
        if r % cand == 0 and cand * c * 4 <= 2 * 1024 * 1024:
            tr = cand
            break
    c1 = 1.0 / (1.0 - ADAM_B1 ** ADAM_STEP)
    c2 = 1.0 / (1.0 - ADAM_B2 ** ADAM_STEP)

    def body(w_ref, g_ref, m_ref, v_ref, d_ref, mo_ref, vo_ref):
        gv = g_ref[...]
        mn = ADAM_B1 * m_ref[...] + (1.0 - ADAM_B1) * gv
        vn = ADAM_B2 * v_ref[...] + (1.0 - ADAM_B2) * (gv * gv)
        mo_ref[...] = mn
        vo_ref[...] = vn
        d_ref[...] = -ADAM_LR * ((mn * c1) / (jnp.sqrt(vn * c2) + ADAM_EPS) + ADAM_WD * w_ref[...])

    blk = pl.BlockSpec((tr, c), lambda i: (i, 0))
    sh = jax.ShapeDtypeStruct((r, c), F32)
    return pl.pallas_call(
        body, name="adamw", grid=(r // tr,),
        in_specs=[blk] * 4, out_specs=[blk] * 3, out_shape=[sh] * 3,
        compiler_params=_params(("parallel",)),
    )(w, g, m, v)


def _row(v, width=None):
    v = v.reshape(1, -1)
    if width is not None and v.shape[1] < width:
        v = jnp.pad(v, ((0, 0), (0, width - v.shape[1])))
    return v


def _gate_row(v):
    return jnp.pad(v.reshape(1, A_HEADS), ((0, 0), (A_HEADS, LANE - 2 * A_HEADS)))


DEPTH = 4
N_A = 2
N_B = 2
F_DOWN_ROWS = FFN_DIM // N_SHARDS
SQ_ROWS = D_MODEL // N_SHARDS
UP_COLS = 2 * FFN_DIM // N_SHARDS


def _a_layer_fwd(h, w, i, pending=None):
    xn = _rmsnorm_fwd(h, _row(w["a_norm"][i]))
    w_qkv, w_z, w_ba = w["a_in"][i]
    pre = _mm(xn, w_qkv, name="a_qkv")
    z = _mm(xn, w_z, name="a_z")
    ba = _mm(xn, w_ba, name="a_ba")
    act = _conv_silu_fwd(pre, w["a_conv"][i])
    alog, dtb, wn = _gate_row(w["a_A_log"][i]), _gate_row(w["a_dt_bias"][i]), _row(w["a_out_norm"][i])
    if pending is None:
        y, o_raw, states, t_all, w_all, vn_all = _gdn_fwd(act, ba, z, alog, dtb, wn)
    else:
        names = list(pending)
        y, o_raw, states, t_all, w_all, vn_all, *landed = _gdn_fwd(act, ba, z, alog, dtb, wn,
                                                                   gather=[pending[n] for n in names])
        w.update(zip(names, landed))
    h2 = _mm_rowsh(y, w["GA"], SQ_ROWS, i, "nn", "a_out", res=h)
    saved = dict(h=h, xn=xn, pre=pre, z=z, ba=ba, act=act, o_raw=o_raw, y=y, states=states,
                 t_all=t_all, w_all=w_all, vn_all=vn_all, alog=alog, dtb=dtb, wn=wn)
    return h2, saved


def _a_layer_bwd(dh2, w, i, sv, acc, outgoing=None):
    g = {}
    acc["GA"] = _mm_rowsh_dw(sv["y"], dh2, SQ_ROWS, "a_out_dw", acc["GA"], i)
    exchange = outgoing() if outgoing is not None else ()
    dy = _mm_rowsh(dh2, w["GA"], SQ_ROWS, i, "nt", "a_out_dx")
    dact, dba, dz, dalog, ddtb, dwn, *arrived = _gdn_bwd(sv["act"], sv["ba"], sv["z"], sv["o_raw"], dy, sv["states"],
                                                         sv["t_all"], sv["w_all"], sv["vn_all"],
                                                         sv["alog"], sv["dtb"], sv["wn"], exchange=exchange)
    dpre, g["conv"] = _conv_silu_bwd(sv["pre"], w["a_conv"][i], dact)
    xn = sv["xn"]
    w_qkv, w_z, w_ba = w["a_in"][i]
    d_in = jnp.concatenate([_mm(xn, dpre, "tn", name="a_qkv_dw"), _mm(xn, dz, "tn", name="a_z_dw"),
                            _mm(xn, dba, "tn", name="a_ba_dw")[:, :2 * A_HEADS]], axis=1)
    g["w_in"] = jnp.transpose(d_in.reshape(D_MODEL, N_SHARDS, -1), (1, 0, 2))
    dxn = _mm(dpre, w_qkv, "nt", name="a_qkv_dx")
    dxn = _mm(dz, w_z, "nt", res=dxn, name="a_z_dx")
    dh, dnorm = _mm(dba, w_ba, "nt", res=dxn, name="a_ba_dx", norm=(sv["h"], _row(w["a_norm"][i]), dh2))
    g["norm"] = dnorm[0]
    g["A_log"] = dalog[0, A_HEADS:2 * A_HEADS]
    g["dt_bias"] = ddtb[0, A_HEADS:2 * A_HEADS]
    g["out_norm"] = dwn[0]
    return dh, g, arrived


def _by_half(a, lead):
    return jnp.moveaxis(a.reshape(a.shape[:-1] + (2, 2, UP_COLS)), (-3, -2), (0, 1)).reshape((2, 2) + lead + (UP_COLS,))


def _from_half(a):
    lead = a.shape[2:-1]
    return jnp.moveaxis(a, (0, 1), (-3, -2)).reshape(lead + (N_SHARDS * UP_COLS,))


def _ffn_buffers(l):
    return ("GD0", "GU0", 0) if l == 0 else ("GD1", "GU1", l - 1)


def _ffn_fwd(h, w, l):
    s = h.shape[0]
    xn = _rmsnorm_fwd(h, _row(w["f_norm"][l]))
    cw = _by_half(w["f_conv"][l], (w["f_conv"].shape[1],))
    cb = _by_half(w["f_conv_b"][l][None], (1,))
    down, up, blk = _ffn_buffers(l)
    pre = _mm_colsh(xn, w[up], D_MODEL, blk, "nn", "f_up").reshape(2, 2, s, UP_COLS)
    act = _ffn_act_fwd(pre, cw, cb)
    h2 = _mm_rowsh(act, w[down], F_DOWN_ROWS, blk, "nn", "f_down", res=h)
    return h2, dict(h=h, xn=xn, pre=pre, act=act, cw=cw, cb=cb)


def _ffn_bwd(dh2, w, l, sv, acc):
    g = {}
    s = dh2.shape[0]
    down, up, blk = _ffn_buffers(l)
    acc[down] = _mm_rowsh_dw(sv["act"], dh2, F_DOWN_ROWS, "f_down_dw", acc[down], blk)
    dact = _mm_rowsh(dh2, w[down], F_DOWN_ROWS, blk, "nt", "f_down_dx")
    dpre, dcw, dcb = _ffn_act_bwd(sv["pre"], sv["cw"], sv["cb"], dact)
    dpre = dpre.reshape(N_SHARDS, s, UP_COLS)
    acc[up] = _mm_colsh_dw(sv["xn"], dpre, "f_up_dw", into=acc[up], blk0=blk)
    g["conv"] = _from_half(dcw)
    g["conv_b"] = _from_half(dcb)[0]
    dh, dnorm = _mm_colsh(dpre, w[up], D_MODEL, blk, "nt", "f_up_dx", norm=(sv["h"], _row(w["f_norm"][l]), dh2))
    g["norm"] = dnorm[0]
    return dh, g


def _b_layer_fwd(h, w, j, kpad, vpad):
    xn = _rmsnorm_fwd(h, _row(w["b_norm"][j]))
    q = _mm_rowsh(xn, w["GB"], SQ_ROWS, j, "nn", "b_q", out_dtype=BF16)
    rel = w["b_rel_bias"][j]
    table = jnp.pad(rel, ((0, 0), (0, REL_PAD - rel.shape[1])))
    bias = _relbias_expand(_relbias_ring(table, False).reshape(B_HEADS, 1, REL_RING))
    o = _attn_fwd(q, kpad, vpad, bias)
    h2 = _mm_rowsh(o, w["GB"], SQ_ROWS, N_B + j, "nn", "b_out", res=h)
    return h2, dict(h=h, xn=xn, q=q, o=o, bias=bias)


def _b_layer_bwd(dh2, w, j, sv, kpad, vpad, acc):
    g = {}
    acc["GB"] = _mm_rowsh_dw(sv["o"], dh2, SQ_ROWS, "b_out_dw", acc["GB"], N_B + j)
    do = _mm_rowsh(dh2, w["GB"], SQ_ROWS, N_B + j, "nt", "b_out_dx", out_dtype=BF16)
    dq, dkp, dvp, dsc = _attn_bwd(sv["q"], kpad, vpad, sv["bias"], do)
    dring = _relbias_reduce(dsc).reshape(B_HEADS, REL_RING)
    g["rel_bias"] = _relbias_ring(dring, True)[:, :2 * REL_CLIP + 1]
    acc["GB"] = _mm_rowsh_dw(sv["xn"], dq, SQ_ROWS, "b_q_dw", acc["GB"], j)
    dh, dnorm = _mm_rowsh(dq, w["GB"], SQ_ROWS, j, "nt", "b_q_dx",
                          norm=(sv["h"], _row(w["b_norm"][j]), dh2))
    g["norm"] = dnorm[0]
    return dh, g, dkp, dvp


EARLY = ("GU1", "GD1", "GB", "GK")
MIDDLE = ("GU0", "GD0", "GA", "GI1")
FINAL = ("GI0",)


def _local_step(x, tgt, w, pending=None, rs_hooks=None):
    w = dict(w)
    h = x
    saved = []
    kv_saved = None
    kpad = vpad = None
    for layer in range(DEPTH):
        if layer < N_A:
            h, sm = _a_layer_fwd(h, w, layer, pending[layer] if pending else None)
        else:
            if layer == N_A:
                xn_kv = _rmsnorm_fwd(h, _row(w["kv_norm"]))
                kv = _mm_colsh(xn_kv, w["GK"], D_MODEL, 0, "nn", "kv", flat=True, out_dtype=BF16)
                kpad = jnp.pad(kv[:, :D_MODEL], ((LEFT, 0), (0, 0)))
                vpad = jnp.pad(kv[:, D_MODEL:], ((LEFT, 0), (0, 0)))
                kv_saved = dict(h=h, xn=xn_kv)
            h, sm = _b_layer_fwd(h, w, layer - N_A, kpad, vpad)
        h, sf = _ffn_fwd(h, w, layer)
        saved.append((sm, sf))

    loss, dh, dfinal = _final_loss(h, _row(w["final_norm"]), tgt)

    ga = [None] * N_A
    gb = [None] * N_B
    gf = [None] * DEPTH
    dk_tot = dv_tot = None
    g_kv = g_kvn = None
    acc = dict(GU1=lax.empty((N_SHARDS, (DEPTH - 1) * D_MODEL, UP_COLS), F32),
               GD1=lax.empty((N_SHARDS, (DEPTH - 1) * F_DOWN_ROWS, D_MODEL), F32),
               GB=lax.empty((N_SHARDS, 2 * N_B * SQ_ROWS, D_MODEL), F32),
               GU0=lax.empty((N_SHARDS, D_MODEL, UP_COLS), F32),
               GD0=lax.empty((N_SHARDS, F_DOWN_ROWS, D_MODEL), F32),
               GA=lax.empty((N_SHARDS, N_A * SQ_ROWS, D_MODEL), F32))
    reduced = {}
    for layer in reversed(range(DEPTH)):
        sm, sf = saved[layer]
        dh, gf[layer] = _ffn_bwd(dh, w, layer, sf, acc)
        if layer >= N_A:
            dh, gb[layer - N_A], dkp, dvp = _b_layer_bwd(dh, w, layer - N_A, sm, kpad, vpad, acc)
            dk_tot = dkp if dk_tot is None else dk_tot + dkp
            dv_tot = dvp if dv_tot is None else dv_tot + dvp
            if layer == N_A:
                dkv = jnp.concatenate([dk_tot[LEFT:], dv_tot[LEFT:]], axis=1).astype(BF16)
                g_kv = _mm_colsh_dw(kv_saved["xn"], dkv, "kv_dw", flat=True)
                dh, g_kvn = _mm_colsh(dkv, w["GK"], D_MODEL, 0, "nt", "kv_dx", flat=True,
                                      norm=(kv_saved["h"], _row(w["kv_norm"]), dh))
        elif rs_hooks is not None:
            prepare, finish = rs_hooks
            names = EARLY if layer == N_A - 1 else MIDDLE
            held = {}

            def outgoing(names=names, held=held):
                extra = dict(GK=g_kv, GI1=ga[N_A - 1]["w_in"] if ga[N_A - 1] else None)
                held["pairs"], partials = prepare([acc[n] if n in acc else extra[n] for n in names])
                return partials

            dh, ga[layer], arrived = _a_layer_bwd(dh, w, layer, sm, acc, outgoing)
            reduced.update(zip(names, finish(held["pairs"], arrived)))
        else:
            dh, ga[layer], _ = _a_layer_bwd(dh, w, layer, sm, acc)

    def stack(gs, key):
        return jnp.stack([g[key] for g in gs])

    grads = dict(
        acc, GK=g_kv, reduced=reduced, GI0=ga[0]["w_in"], GI1=ga[1]["w_in"],
        a_norm=stack(ga, "norm"), a_conv=stack(ga, "conv"), a_A_log=stack(ga, "A_log"),
        a_dt_bias=stack(ga, "dt_bias"), a_out_norm=stack(ga, "out_norm"), kv_norm=g_kvn[0],
        b_norm=stack(gb, "norm"), b_rel_bias=stack(gb, "rel_bias"),
        f_norm=stack(gf, "norm"), f_conv=stack(gf, "conv"), f_conv_b=stack(gf, "conv_b"), final_norm=dfinal[0])
    return loss, dh, grads


HBM_SPEC = pl.BlockSpec(memory_space=pl.ANY)
VMEM_SPEC = pl.BlockSpec(memory_space=pltpu.VMEM)


def _place():
    x, y, c = lax.axis_index("x"), lax.axis_index("y"), lax.axis_index("c")
    chips = [(1 - x, y), (x, 1 - y), (1 - x, 1 - y)]
    return x, y, c, chips


def _remote(src, dst, send_sem, recv_sem, to):
    return pltpu.make_async_remote_copy(src_ref=src, dst_ref=dst, send_sem=send_sem, recv_sem=recv_sem,
                                        device_id=to, device_id_type=MESH)


GATHER_COPIES = 7


class _ShardGather:
    def __init__(self, x_refs, out_refs, send_sems, recv_sems):
        n = len(x_refs)
        x, y, c, chips = _place()
        sibling = (x, y, 1 - c)

        def half(a, px, py, hc):
            rh = x_refs[a].shape[0] // 2
            return out_refs[a].at[2 * px + py, pl.ds(hc * rh, rh), :]

        def mine(a):
            rh = x_refs[a].shape[0] // 2
            return x_refs[a].at[pl.ds(c * rh, rh), :]

        def sems(a, k):
            return send_sems.at[GATHER_COPIES * a + k], recv_sems.at[GATHER_COPIES * a + k]

        order = [(j, chip, a) for j, chip in enumerate(chips) for a in range(n)]
        self.first = [_remote(mine(a), half(a, x, y, c), *sems(a, j), (*chip, c)) for j, chip, a in order]
        self.own = [_remote(x_refs[a], out_refs[a].at[2 * x + y], *sems(a, 6), sibling) for a in range(n)]
        self.landed = [_remote(half(a, *chip, c), half(a, *chip, c), *sems(a, j), (*chip, c)) for j, chip, a in order]
        self.passed = [_remote(half(a, *chip, c), half(a, *chip, c), *sems(a, 3 + j), sibling)
                       for j, chip, a in order]
        self.theirs = [_remote(half(a, *chip, 1 - c), half(a, *chip, 1 - c), *sems(a, 3 + j), sibling)
                       for j, chip, a in order]

    def start(self):
        for cp in self.first + self.own:
            cp.start()

    def finish(self):
        for arrived, onward in zip(self.landed, self.passed):
            arrived.wait_recv()
            onward.start()
        for cp in self.theirs + self.own:
            cp.wait_recv()
        for cp in self.first + self.passed + self.own:
            cp.wait_send()


def _gather_scratch(n):
    return [pltpu.SemaphoreType.DMA((GATHER_COPIES * n,)), pltpu.SemaphoreType.DMA((GATHER_COPIES * n,))]


def _allgather_weights(shards):
    n = len(shards)

    def body(*refs):
        gather = _ShardGather(refs[:n], refs[n:2 * n], *refs[2 * n:])
        gather.start()
        gather.finish()

    return pl.pallas_call(
        body, name="allgather_weights",
        out_shape=[jax.ShapeDtypeStruct((N_SHARDS,) + sh.shape, sh.dtype) for sh in shards],
        in_specs=[HBM_SPEC] * n, out_specs=[HBM_SPEC] * n,
        scratch_shapes=_gather_scratch(n),
    )(*shards)


def _pair_exchange(gs):
    n = len(gs)

    def body(*refs):
        g_refs, out_refs, (send_sems, recv_sems) = refs[:n], refs[n:2 * n], refs[2 * n:]
        x, y, c, _ = _place()
        cps = []
        for a in range(n):
            rh = gs[a].shape[1] // 2
            cps.append(_remote(g_refs[a].at[:, pl.ds((1 - c) * rh, rh), :], out_refs[a], send_sems.at[a],
                               recv_sems.at[a], (x, y, 1 - c)))
        for cp in cps:
            cp.start()
        for cp in cps:
            cp.wait()

    return pl.pallas_call(
        body, name="rs_pair_exchange",
        out_shape=[jax.ShapeDtypeStruct((g.shape[0], g.shape[1] // 2, g.shape[2]), g.dtype) for g in gs],
        in_specs=[HBM_SPEC] * n, out_specs=[HBM_SPEC] * n,
        scratch_shapes=[pltpu.SemaphoreType.DMA((n,)), pltpu.SemaphoreType.DMA((n,))],
    )(*gs)


def _add_rows(rows, cols):
    best = 16
    for t in range(16, rows + 1, 16):
        if rows % t == 0 and t * cols * 4 <= 2304 * 1024:
            best = t
    return best


def _pair_add(g, other, c_idx):
    n, r, cols = g.shape
    rh = r // 2
    tr = _add_rows(rh, cols)
    nb = rh // tr

    def body(c_ref, a_ref, b_ref, o_ref, ob_ref):
        sm = a_ref[...] + b_ref[...]
        o_ref[...] = sm
        ob_ref[...] = sm.astype(BF16)

    out_blk = pl.BlockSpec((1, tr, cols), lambda s, i, c_ref: (s, i, 0))
    return pl.pallas_call(
        body, name="rs_pair_add",
        grid_spec=pltpu.PrefetchScalarGridSpec(
            num_scalar_prefetch=1, grid=(n, nb),
            in_specs=[pl.BlockSpec((1, tr, cols), lambda s, i, c_ref: (s, c_ref[0] * nb + i, 0)), out_blk],
            out_specs=[out_blk, out_blk]),
        out_shape=[jax.ShapeDtypeStruct((n, rh, cols), F32), jax.ShapeDtypeStruct((n, rh, cols), BF16)],
        compiler_params=_params(("parallel", "parallel")),
    )(c_idx, g, other)


class _ChipExchange:
    def __init__(self, p_refs, out_refs, send_sems, recv_sems):
        x, y, c, chips = _place()
        self.copies = [_remote(p_refs[a].at[2 * chip[0] + chip[1]], out_refs[a].at[j], send_sems.at[3 * a + j],
                               recv_sems.at[3 * a + j], (*chip, c))
                       for a in range(len(p_refs)) for j, chip in enumerate(chips)]

    def start(self):
        for cp in self.copies:
            cp.start()

    def finish(self):
        for cp in self.copies:
            cp.wait()


def _chip_scratch(n):
    return [pltpu.SemaphoreType.DMA((3 * n,)), pltpu.SemaphoreType.DMA((3 * n,))]


def _chip_exchange(ps):
    n = len(ps)

    def body(*refs):
        exchange = _ChipExchange(refs[:n], refs[n:2 * n], *refs[2 * n:])
        exchange.start()
        exchange.finish()

    return pl.pallas_call(
        body, name="rs_chip_exchange",
        out_shape=[jax.ShapeDtypeStruct((3,) + p.shape[1:], p.dtype) for p in ps],
        in_specs=[HBM_SPEC] * n, out_specs=[HBM_SPEC] * n,
        scratch_shapes=_chip_scratch(n),
    )(*ps)


def _chip_add(p, recv, chip_idx):
    n, rh, cols = p.shape
    tr = _add_rows(rh, cols)

    def body(s_ref, own_ref, r_ref, o_ref):
        o_ref[...] = ((own_ref[0] + r_ref[0].astype(F32)) + r_ref[1].astype(F32)) + r_ref[2].astype(F32)

    return pl.pallas_call(
        body, name="rs_chip_add",
        grid_spec=pltpu.PrefetchScalarGridSpec(
            num_scalar_prefetch=1, grid=(rh // tr,),
            in_specs=[pl.BlockSpec((1, tr, cols), lambda i, s_ref: (s_ref[0], i, 0)),
                      pl.BlockSpec((3, tr, cols), lambda i, s_ref: (0, i, 0))],
            out_specs=pl.BlockSpec((tr, cols), lambda i, s_ref: (i, 0))),
        out_shape=jax.ShapeDtypeStruct((rh, cols), p.dtype),
        compiler_params=_params(("parallel",)),
    )(chip_idx, p, recv)


def _pair_gather(fs):
    n = len(fs)

    def body(*refs):
        f_refs, out_refs, (send_sems, recv_sems) = refs[:n], refs[n:2 * n], refs[2 * n:]
        x, y, c, _ = _place()
        cps = [_remote(f_refs[a], out_refs[a], send_sems.at[a], recv_sems.at[a], (x, y, 1 - c)) for a in range(n)]
        for cp in cps:
            cp.start()
        for cp in cps:
            cp.wait()

    return pl.pallas_call(
        body, name="rs_pair_gather",
        out_shape=[jax.ShapeDtypeStruct(f.shape, f.dtype) for f in fs],
        in_specs=[HBM_SPEC] * n, out_specs=[HBM_SPEC] * n,
        scratch_shapes=[pltpu.SemaphoreType.DMA((n,)), pltpu.SemaphoreType.DMA((n,))],
    )(*fs)


def _allreduce_small(v):
    r, cols = v.shape

    def body(x_ref, out_ref, slots, send_sems, recv_sems):
        x, y, c, _ = _place()
        bits = [(bx, by, bc) for bx in (0, 1) for by in (0, 1) for bc in (0, 1)]

        def flip(b):
            return (1 - x if b[0] else x, 1 - y if b[1] else y, 1 - c if b[2] else c)

        slots[0] = x_ref[...]
        cps = [_remote(x_ref, slots.at[k], send_sems.at[k - 1], recv_sems.at[k - 1], flip(bits[k]))
               for k in range(1, 8)]
        for cp in cps:
            cp.start()
        for cp in cps:
            cp.wait()
        acc = None
        for b in bits:
            fx, fy, fc = flip(b)
            term = slots[4 * fx + 2 * fy + fc]
            acc = term if acc is None else acc + term
        out_ref[...] = acc

    return pl.pallas_call(
        body, name="allreduce_small",
        out_shape=jax.ShapeDtypeStruct((r, cols), v.dtype),
        in_specs=[VMEM_SPEC], out_specs=VMEM_SPEC,
        scratch_shapes=[pltpu.VMEM((8, r, cols), v.dtype), pltpu.SemaphoreType.DMA((7,)),
                        pltpu.SemaphoreType.DMA((7,))],
        compiler_params=pltpu.CompilerParams(vmem_limit_bytes=VMEM_LIMIT),
    )(v)


BIG = (("a_w_in", 2), ("a_w_out", 1), ("w_kv", 1), ("b_w_q", 1), ("b_w_out", 1), ("f_w_up", 2), ("f_w_down", 1))
SMALL = (("a_norm", 1), ("a_conv", 2), ("a_A_log", None), ("a_dt_bias", None), ("a_out_norm", None),
         ("kv_norm", None), ("b_norm", None), ("b_rel_bias", None), ("f_norm", None), ("f_conv", 2),
         ("f_conv_b", None), ("final_norm", None))
WEIGHT_ORDER = ("a_norm", "a_w_in", "a_conv", "a_A_log", "a_dt_bias", "a_out_norm", "a_w_out", "kv_norm", "w_kv",
                "b_norm", "b_w_q", "b_rel_bias", "b_w_out", "f_norm", "f_w_up", "f_conv", "f_conv_b", "f_w_down",
                "final_norm")


def _pad_rows(flat, cols, quantum):
    n = flat.shape[-1]
    rows = -(-n // (cols * quantum)) * quantum
    pad = [(0, 0)] * (flat.ndim - 1) + [(0, rows * cols - n)]
    return jnp.pad(flat, pad).reshape(flat.shape[:-1] + (rows, cols))


FIRST_GROUPS = ("GI",)
RIDING_GROUPS = (("GA", "GU0", "GD0", "GB", "GK"), ("GU1", "GD1"))
GROUPS = FIRST_GROUPS + RIDING_GROUPS[0] + RIDING_GROUPS[1]


def _group_shards(w, dtype):
    def two(a):
        return a.reshape(-1, a.shape[-1]).astype(dtype)

    return dict(GU0=two(w["f_w_up"][:1]), GU1=two(w["f_w_up"][1:]), GD0=two(w["f_w_down"][:1]),
                GD1=two(w["f_w_down"][1:]), GB=jnp.concatenate([two(w["b_w_q"]), two(w["b_w_out"])]),
                GK=w["w_kv"].astype(dtype), GI=two(w["a_w_in"]), GA=two(w["a_w_out"]))


def _ungroup(red, shard_shapes):
    b_rows = N_B * SQ_ROWS
    flat = dict(f_w_up=jnp.concatenate([red["GU0"], red["GU1"]]), f_w_down=jnp.concatenate([red["GD0"], red["GD1"]]),
                b_w_q=red["GB"][:b_rows], b_w_out=red["GB"][b_rows:], a_w_out=red["GA"], w_kv=red["GK"],
                a_w_in=jnp.concatenate([red["GI0"], red["GI1"]]))
    return {n: v.reshape(shard_shapes[n]) for n, v in flat.items()}


def _dense_a_in(gi):
    out = []
    for i in range(N_A):
        full = jnp.transpose(gi[:, i * D_MODEL:(i + 1) * D_MODEL], (1, 0, 2)).reshape(D_MODEL, -1)
        out.append((full[:, :A_CONV_WIDTH], full[:, A_CONV_WIDTH:A_CONV_WIDTH + A_QK],
                    jnp.pad(full[:, A_CONV_WIDTH + A_QK:], ((0, 0), (0, LANE - 2 * A_HEADS)))))
    return out


def _pack_small(values, names):
    return _pad_rows(jnp.concatenate([values[n].reshape(-1) for n in names]), LANE, SUB)


def _unpack_small(packed, shapes, names):
    flat = packed.reshape(-1)
    out, off = {}, 0
    for n in names:
        size = math.prod(shapes[n])
        out[n] = flat[off:off + size].reshape(shapes[n])
        off += size
    return out


def _adamw_nd(w, g, m, v):
    shp = w.shape
    two = (math.prod(shp[:-1]), shp[-1])
    d, mn, vn = _adamw(w.reshape(two), g.reshape(two), m.reshape(two), v.reshape(two))
    return d.reshape(shp), mn.reshape(shp), vn.reshape(shp)


def kernel(x, a_norm, a_w_in, a_conv, a_A_log, a_dt_bias, a_out_norm, a_w_out, kv_norm, w_kv, b_norm, b_w_q, b_rel_bias, b_w_out, f_norm, f_w_up, f_conv, f_conv_b, f_w_down, final_norm, loss_target, m_a_norm, m_a_w_in, m_a_conv, m_a_A_log, m_a_dt_bias, m_a_out_norm, m_a_w_out, m_kv_norm, m_w_kv, m_b_norm, m_b_w_q, m_b_rel_bias, m_b_w_out, m_f_norm, m_f_w_up, m_f_conv, m_f_conv_b, m_f_w_down, m_final_norm, v_a_norm, v_a_w_in, v_a_conv, v_a_A_log, v_a_dt_bias, v_a_out_norm, v_a_w_out, v_kv_norm, v_w_kv, v_b_norm, v_b_w_q, v_b_rel_bias, v_b_w_out, v_f_norm, v_f_w_up, v_f_conv, v_f_conv_b, v_f_w_down, v_final_norm):
    w = dict(a_norm=a_norm, a_w_in=a_w_in, a_conv=a_conv, a_A_log=a_A_log, a_dt_bias=a_dt_bias,
             a_out_norm=a_out_norm, a_w_out=a_w_out, kv_norm=kv_norm, w_kv=w_kv, b_norm=b_norm, b_w_q=b_w_q,
             b_rel_bias=b_rel_bias, b_w_out=b_w_out, f_norm=f_norm, f_w_up=f_w_up, f_conv=f_conv,
             f_conv_b=f_conv_b, f_w_down=f_w_down, final_norm=final_norm)
    m = dict(a_norm=m_a_norm, a_w_in=m_a_w_in, a_conv=m_a_conv, a_A_log=m_a_A_log, a_dt_bias=m_a_dt_bias,
             a_out_norm=m_a_out_norm, a_w_out=m_a_w_out, kv_norm=m_kv_norm, w_kv=m_w_kv, b_norm=m_b_norm,
             b_w_q=m_b_w_q, b_rel_bias=m_b_rel_bias, b_w_out=m_b_w_out, f_norm=m_f_norm, f_w_up=m_f_w_up,
             f_conv=m_f_conv, f_conv_b=m_f_conv_b, f_w_down=m_f_w_down, final_norm=m_final_norm)
    v = dict(a_norm=v_a_norm, a_w_in=v_a_w_in, a_conv=v_a_conv, a_A_log=v_a_A_log, a_dt_bias=v_a_dt_bias,
             a_out_norm=v_a_out_norm, a_w_out=v_a_w_out, kv_norm=v_kv_norm, w_kv=v_w_kv, b_norm=v_b_norm,
             b_w_q=v_b_w_q, b_rel_bias=v_b_rel_bias, b_w_out=v_b_w_out, f_norm=v_f_norm, f_w_up=v_f_w_up,
             f_conv=v_f_conv, f_conv_b=v_f_conv_b, f_w_down=v_f_w_down, final_norm=v_final_norm)
    xi, yi, ci = lax.axis_index("x"), lax.axis_index("y"), lax.axis_index("c")
    chip = 2 * xi + yi
    shard_shapes = {n: w[n].shape for n in WEIGHT_ORDER}

    mine_w = _group_shards(w, BF16)
    full = dict(zip(FIRST_GROUPS, _allgather_weights([mine_w[n] for n in FIRST_GROUPS])))
    full["a_in"] = _dense_a_in(full.pop("GI"))
    pending = [{n: mine_w[n] for n in names} for names in RIDING_GROUPS]
    sharded_small = [n for n, axis in SMALL if axis is not None]
    placed = {}
    for n, axis in SMALL:
        if axis is not None:
            wide = list(w[n].shape)
            wide[axis] *= 4
            mine_once = w[n] * (1 - ci).astype(F32)
            placed[n] = lax.dynamic_update_slice_in_dim(jnp.zeros(wide, F32), mine_once, chip * w[n].shape[axis], axis)
    placed_shapes = {n: placed[n].shape for n in sharded_small}
    full.update(_unpack_small(_allreduce_small(_pack_small(placed, sharded_small)), placed_shapes, sharded_small))
    for n, axis in SMALL:
        if axis is None:
            full[n] = w[n]

    c_idx = jnp.reshape(ci, (1,)).astype(jnp.int32)
    chip_idx = jnp.reshape(chip, (1,)).astype(jnp.int32)

    def pair_stage(gs):
        pairs = [_pair_add(g, o, c_idx) for g, o in zip(gs, _pair_exchange(gs))]
        return pairs, [pb for _, pb in pairs]

    def chip_stage(pairs, arrived):
        mine = [_chip_add(p, r, chip_idx) for (p, _), r in zip(pairs, arrived)]
        theirs = _pair_gather(mine)
        return [jnp.concatenate([jnp.where(ci == 0, a, b), jnp.where(ci == 0, b, a)], axis=0)
                for a, b in zip(mine, theirs)]

    loss_part, grad_x, grads = _local_step(x[0], loss_target[0], full, pending, (pair_stage, chip_stage))
    last_pairs, outgoing = pair_stage([grads[n] for n in FINAL])
    last = chip_stage(last_pairs, _chip_exchange(outgoing))
    red = _ungroup(dict(grads["reduced"], **dict(zip(FINAL, last))), shard_shapes)

    small_names = [n for n, _ in SMALL]
    small_vals = {n: grads[n] for n in small_names}
    small_vals["loss"] = loss_part[0, :1]
    names = ["loss"] + small_names
    shapes = {n: small_vals[n].shape for n in names}
    summed = _unpack_small(_allreduce_small(_pack_small(small_vals, names)), shapes, names)
    loss = summed["loss"][0]
    for n, axis in SMALL:
        g = summed[n]
        if axis is not None:
            g = lax.dynamic_slice_in_dim(g, chip * w[n].shape[axis], w[n].shape[axis], axis)
        red[n] = g

    delta, new_m, new_v = {}, {}, {}
    for n, _ in BIG:
        delta[n], new_m[n], new_v[n] = _adamw_nd(w[n], red[n], m[n], v[n])
    local_shapes = {n: w[n].shape for n in small_names}
    packs = [_pack_small(t, small_names) for t in (w, red, m, v)]
    outs = _adamw(*packs)
    ds, ms, vs = (_unpack_small(o, local_shapes, small_names) for o in outs)
    delta.update(ds)
    new_m.update(ms)
    new_v.update(vs)

    return (loss, grad_x[None], *[red[n] for n in WEIGHT_ORDER], *[delta[n] for n in WEIGHT_ORDER],
            *[new_m[n] for n in WEIGHT_ORDER], *[new_v[n] for n in WEIGHT_ORDER])
```
